```python
import jax
import jax.numpy as jnp
from jax import lax

D_MODEL = 1024
BATCH = 32
SEQ = 256
DEPTH = 1
DEC_BATCH = 4
DEC_SEQ = 2048
PAST_LEN = 256

GRID_W = 64
D_MIX = D_MODEL
M_HEADS = 4
M_HEAD_DIM = 128
M_WIDTH = M_HEADS * M_HEAD_DIM
M_CHUNK = 64
A_HEADS = 8
A_NOPE = 64
A_ROPE = 32
A_QK = A_NOPE + A_ROPE
A_VDIM = 64
A_WIDTH = A_HEADS * A_VDIM
Q_LORA = 384
KV_LORA = 256
ROPE_BASE = 10000.0
D_FF = 4 * D_MODEL
EPS = 1e-6
Q_BLOCK = 128
IN_SIZES = (M_WIDTH, M_WIDTH, M_WIDTH, M_WIDTH, 4 * M_HEADS, Q_LORA, KV_LORA, A_ROPE)
N_IN = 4 * M_WIDTH + 4 * M_HEADS + Q_LORA + KV_LORA + A_ROPE

kernel_name = "hymba_mlstm_mla_prefix_diffusion_step"


def _rmsnorm(x, g):
    xf = x.astype(jnp.float32)
    y = xf * lax.rsqrt(jnp.mean(xf * xf, axis=-1, keepdims=True) + EPS)
    return (y * g.astype(jnp.float32)).astype(x.dtype)


def _split_cols(a, sizes):
    idx, acc = [], 0
    for s in sizes[:-1]:
        acc += s
        idx.append(acc)
    return jnp.split(a, idx, axis=-1)


def _rope_2d(rows):
    row = jnp.repeat(jnp.arange(rows, dtype=jnp.float32), GRID_W)
    col = jnp.tile(jnp.arange(GRID_W, dtype=jnp.float32), rows)
    half = A_ROPE // 2
    inv = ROPE_BASE ** (-jnp.arange(0, half, 2, dtype=jnp.float32) / half)
    ang = jnp.concatenate([row[:, None] * inv, col[:, None] * inv], axis=-1)
    return jnp.cos(ang), jnp.sin(ang)


def _apply_rope(x, cos, sin):
    x1, x2 = jnp.split(x, 2, axis=-1)
    c = cos[:, None, :].astype(x.dtype)
    s = sin[:, None, :].astype(x.dtype)
    return jnp.concatenate([x1 * c - x2 * s, x1 * s + x2 * c], axis=-1)


def _mlstm_chunkwise(q, k, v, ig, lf, C0, n0, m0):
    B, H, T, Dh = q.shape
    nc = T // M_CHUNK
    L = M_CHUNK

    def to_chunks(a):
        return jnp.moveaxis(a.reshape(B, H, nc, L, *a.shape[3:]), 2, 0)

    tril = jnp.tril(jnp.ones((L, L), dtype=bool))

    def step(carry, inp):
        C, n, m = carry
        qc, kc, vc, ic, fc = inp
        b = jnp.cumsum(fc, axis=-1)
        dmat = b[..., :, None] - b[..., None, :] + ic[..., None, :]
        dmat = jnp.where(tril, dmat, -jnp.inf)
        m_inter = b + m[..., None]
        m_t = jnp.maximum(m_inter, jnp.max(dmat, axis=-1))
        w_intra = jnp.exp(dmat - m_t[..., None])
        w_inter = jnp.exp(m_inter - m_t)
        s = jnp.einsum('bhtk,bhsk->bhts', qc, kc) * w_intra
        num = jnp.einsum('bhts,bhsv->bhtv', s, vc) + w_inter[..., None] * jnp.einsum('bhtk,bhkv->bhtv', qc, C)
        den = jnp.sum(s, axis=-1) + w_inter * jnp.einsum('bhtk,bhk->bht', qc, n)
        h = num / jnp.maximum(jnp.abs(den), jnp.exp(-m_t))[..., None]
        b_last = b[..., -1]
        log_w = b_last[..., None] - b + ic
        m_new = jnp.maximum(b_last + m, jnp.max(log_w, axis=-1))
        w = jnp.exp(log_w - m_new[..., None])
        decay = jnp.exp(b_last + m - m_new)
        C_new = decay[..., None, None] * C + jnp.einsum('bhs,bhsk,bhsv->bhkv', w, kc, vc)
        n_new = decay[..., None] * n + jnp.einsum('bhs,bhsk->bhk', w, kc)
        return (C_new, n_new, m_new), h

    carry0 = (C0.astype(jnp.float32), n0.astype(jnp.float32), m0.astype(jnp.float32))
    (C, n, m), hs = lax.scan(step, carry0, (to_chunks(q), to_chunks(k), to_chunks(v), to_chunks(ig), to_chunks(lf)))
    h = jnp.moveaxis(hs, 0, 2).reshape(B, H, T, Dh)
    return h, (C, n, m)


def _mlstm_bidir(q, k, v, ig, fg, C0, n0, m0):
    qh = jnp.transpose(q.astype(jnp.float32), (0, 2, 1, 3))
    kh = jnp.transpose(k.astype(jnp.float32), (0, 2, 1, 3)) * (M_HEAD_DIM ** -0.5)
    vh = jnp.transpose(v.astype(jnp.float32), (0, 2, 1, 3))
    ig = jnp.transpose(ig, (0, 2, 3, 1))
    lf = jax.nn.log_sigmoid(jnp.transpose(fg, (0, 2, 3, 1)))
    flip = lambda a: jnp.flip(a, axis=2)
    h_f, (Cf, nf, mf) = _mlstm_chunkwise(qh, kh, vh, ig[:, 0], lf[:, 0], C0[:, 0], n0[:, 0], m0[:, 0])
    h_b, (Cb, nb, mb) = _mlstm_chunkwise(flip(qh), flip(kh), flip(vh), flip(ig[:, 1]), flip(lf[:, 1]),
                                         C0[:, 1], n0[:, 1], m0[:, 1])
    h = jnp.transpose(h_f + flip(h_b), (0, 2, 1, 3))
    return h, jnp.stack([Cf, Cb], axis=1), jnp.stack([nf, nb], axis=1), jnp.stack([mf, mb], axis=1)


def _mla_queries(q_lat, q_lora_g, w_q_up, q_head_g, rope):
    B, T, _ = q_lat.shape
    q = (_rmsnorm(q_lat, q_lora_g) @ w_q_up).reshape(B, T, A_HEADS, A_QK)
    q = _rmsnorm(q, q_head_g)
    if rope is not None:
        q = jnp.concatenate([q[..., :A_NOPE], _apply_rope(q[..., A_NOPE:], *rope)], axis=-1)
    return q


def _mla_keys_values(ckv, k_rope, w_kv_up, k_head_g, rope):
    B, T, _ = ckv.shape
    kv = (ckv @ w_kv_up).reshape(B, T, A_HEADS, A_NOPE + A_VDIM)
    k_nope, v = kv[..., :A_NOPE], kv[..., A_NOPE:]
    k = jnp.concatenate([k_nope, jnp.broadcast_to(k_rope[:, :, None, :], (B, T, A_HEADS, A_ROPE))], axis=-1)
    k = _rmsnorm(k, k_head_g)
    if rope is not None:
        k = jnp.concatenate([k[..., :A_NOPE], _apply_rope(k[..., A_NOPE:], *rope)], axis=-1)
    return k, v


def _block_attention(q, k, v):
    B, Tq, H, Dq = q.shape
    nb = Tq // Q_BLOCK
    scale = Dq ** -0.5
    qb = jnp.moveaxis(q.astype(jnp.float32).reshape(B, nb, Q_BLOCK, H, Dq), 1, 0)
    kf = k.astype(jnp.float32)
    vf = v.astype(jnp.float32)

    def one(qi):
        s = jnp.einsum('bqhd,bkhd->bhqk', qi, kf) * scale
        p = jax.nn.softmax(s, axis=-1)
        return jnp.einsum('bhqk,bkhd->bqhd', p, vf)

    out = lax.map(one, qb)
    return jnp.moveaxis(out, 0, 1).reshape(B, Tq, H, v.shape[-1]).astype(q.dtype)


def _layer(x, cond, ctx_C, ctx_n, ctx_m, ctx_ckv, ctx_krope, rope,
           norm1_g, norm2_g, w_ada, b_ada, w_in, mlstm_gate_b, mlstm_norm_g,
           q_lora_g, kv_lora_g, w_q_up, w_kv_up, q_head_g, k_head_g, w_out, w_mlp_up, w_mlp_down):
    B, T, _ = x.shape
    mod = (jax.nn.silu(cond) @ w_ada + b_ada)[..., None, :]
    sh1, sc1, g1, sh2, sc2, g2 = jnp.split(mod, 6, axis=-1)
    h = _rmsnorm(x, norm1_g) * (1 + sc1) + sh1
    mq, mk, mv, mo, gates, q_lat, kv_lat, k_rope = _split_cols(h @ w_in, IN_SIZES)
    gates = (gates + mlstm_gate_b).astype(jnp.float32).reshape(B, T, 2, 2, M_HEADS)
    hm, C_new, n_new, m_new = _mlstm_bidir(mq.reshape(B, T, M_HEADS, M_HEAD_DIM),
                                           mk.reshape(B, T, M_HEADS, M_HEAD_DIM),
                                           mv.reshape(B, T, M_HEADS, M_HEAD_DIM),
                                           gates[:, :, :, 0], gates[:, :, :, 1], ctx_C, ctx_n, ctx_m)
    hm = _rmsnorm(hm, mlstm_norm_g.reshape(M_HEADS, M_HEAD_DIM)).astype(x.dtype)
    hm = (hm * jax.nn.sigmoid(mo.reshape(B, T, M_HEADS, M_HEAD_DIM))).reshape(B, T, M_WIDTH)
    ckv = _rmsnorm(kv_lat, kv_lora_g)
    q = _mla_queries(q_lat, q_lora_g, w_q_up, q_head_g, rope)
    k, v = _mla_keys_values(ckv, k_rope, w_kv_up, k_head_g, rope)
    if ctx_ckv is not None:
        kc, vc = _mla_keys_values(ctx_ckv, ctx_krope, w_kv_up, k_head_g, None)
        k = jnp.concatenate([k, kc.astype(k.dtype)], axis=1)
        v = jnp.concatenate([v, vc.astype(v.dtype)], axis=1)
    ha = _block_attention(q, k, v).reshape(B, T, A_WIDTH)
    x = x + g1 * (jnp.concatenate([hm, ha], axis=-1) @ w_out)
    h2 = _rmsnorm(x, norm2_g) * (1 + sc2) + sh2
    x = x + g2 * (jnp.square(jax.nn.relu(h2 @ w_mlp_up)) @ w_mlp_down)
    return x, (ckv, k_rope, C_new, n_new, m_new)


def setup_inputs(seed: int = 0) -> dict:
    key = jax.random.key(seed)
    ks = jax.random.split(key, 32)
    f32 = jnp.float32

    def nrm(k, shape, s):
        return s * jax.random.normal(k, shape, f32)

    L = DEPTH
    i_bias = nrm(ks[20], (L, 2, 1, M_HEADS), 0.1)
    f_bias = jnp.linspace(3.0, 6.0, M_HEADS, dtype=f32) + nrm(ks[21], (L, 2, 1, M_HEADS), 0.1)
    return {
        "x_prompt": nrm(ks[0], (BATCH, SEQ, D_MODEL), 1.0),
        "x_sample": nrm(ks[1], (DEC_BATCH, DEC_SEQ, D_MODEL), 1.0),
        "cache_mla_ckv": nrm(ks[2], (DEC_BATCH, L, PAST_LEN, KV_LORA), 1.0),
        "cache_mla_krope": nrm(ks[3], (DEC_BATCH, L, PAST_LEN, A_ROPE), 1.0),
        "state_mlstm_C": nrm(ks[4], (DEC_BATCH, L, 2, M_HEADS, M_HEAD_DIM, M_HEAD_DIM), 0.05),
        "state_mlstm_n": nrm(ks[5], (DEC_BATCH, L, 2, M_HEADS, M_HEAD_DIM), 0.5),
        "state_mlstm_m": jax.random.uniform(ks[6], (DEC_BATCH, L, 2, M_HEADS), f32, -1.0, 2.0),
        "c": nrm(ks[7], (DEC_BATCH, D_MODEL), 1.0),
        "c_ctx": nrm(ks[8], (D_MODEL,), 1.0),
        "norm1_g": 1.0 + nrm(ks[9], (L, D_MODEL), 0.02),
        "norm2_g": 1.0 + nrm(ks[10], (L, D_MODEL), 0.02),
        "w_ada": nrm(ks[11], (L, D_MODEL, 6 * D_MODEL), 0.5 * D_MODEL ** -0.5),
        "b_ada": nrm(ks[12], (L, 6 * D_MODEL), 0.02),
        "w_in": nrm(ks[13], (L, D_MODEL, N_IN), D_MODEL ** -0.5),
        "mlstm_gate_b": jnp.concatenate([i_bias, f_bias], axis=2).reshape(L, 4 * M_HEADS),
        "mlstm_norm_g": 1.0 + nrm(ks[14], (L, M_WIDTH), 0.02),
        "q_lora_g": 1.0 + nrm(ks[15], (L, Q_LORA), 0.02),
        "kv_lora_g": 1.0 + nrm(ks[16], (L, KV_LORA), 0.02),
        "w_q_up": nrm(ks[17], (L, Q_LORA, A_HEADS * A_QK), Q_LORA ** -0.5),
        "w_kv_up": nrm(ks[18], (L, KV_LORA, A_HEADS * (A_NOPE + A_VDIM)), KV_LORA ** -0.5),
        "q_head_g": 1.0 + nrm(ks[19], (L, A_QK), 0.02),
        "k_head_g": 1.0 + nrm(ks[22], (L, A_QK), 0.02),
        "w_out": nrm(ks[23], (L, D_MIX, D_MODEL), D_MIX ** -0.5),
        "w_mlp_up": nrm(ks[24], (L, D_MODEL, D_FF), D_MODEL ** -0.5),
        "w_mlp_down": nrm(ks[25], (L, D_FF, D_MODEL), D_FF ** -0.5),
    }


def reference(x_prompt, x_sample, cache_mla_ckv, cache_mla_krope, state_mlstm_C, state_mlstm_n, state_mlstm_m,
              c, c_ctx, norm1_g, norm2_g, w_ada, b_ada, w_in, mlstm_gate_b, mlstm_norm_g,
              q_lora_g, kv_lora_g, w_q_up, w_kv_up, q_head_g, k_head_g, w_out, w_mlp_up, w_mlp_down):
    Bp = x_prompt.shape[0]
    zero_C = jnp.zeros((Bp, 2, M_HEADS, M_HEAD_DIM, M_HEAD_DIM), jnp.float32)
    zero_n = jnp.zeros((Bp, 2, M_HEADS, M_HEAD_DIM), jnp.float32)
    zero_m = jnp.zeros((Bp, 2, M_HEADS), jnp.float32)
    y = x_prompt
    ckvs, kropes, Cs, ns, ms = [], [], [], [], []
    for l in range(DEPTH):
        y, (ckv, krope, Cl, nl, ml) = _layer(
            y, c_ctx, zero_C, zero_n, zero_m, None, None, None,
            norm1_g[l], norm2_g[l], w_ada[l], b_ada[l], w_in[l], mlstm_gate_b[l], mlstm_norm_g[l],
            q_lora_g[l], kv_lora_g[l], w_q_up[l], w_kv_up[l], q_head_g[l], k_head_g[l],
            w_out[l], w_mlp_up[l], w_mlp_down[l])
        ckvs.append(ckv)
        kropes.append(krope)
        Cs.append(Cl)
        ns.append(nl)
        ms.append(ml)
    rows = x_sample.shape[1] // GRID_W
    rope = _rope_2d(rows)
    z = x_sample
    for l in range(DEPTH):
        z, _ = _layer(
            z, c, state_mlstm_C[:, l], state_mlstm_n[:, l], state_mlstm_m[:, l],
            cache_mla_ckv[:, l], cache_mla_krope[:, l], rope,
            norm1_g[l], norm2_g[l], w_ada[l], b_ada[l], w_in[l], mlstm_gate_b[l], mlstm_norm_g[l],
            q_lora_g[l], kv_lora_g[l], w_q_up[l], w_kv_up[l], q_head_g[l], k_head_g[l],
            w_out[l], w_mlp_up[l], w_mlp_down[l])
    return (y, z, jnp.stack(ckvs, axis=1), jnp.stack(kropes, axis=1), jnp.stack(Cs, axis=1),
            jnp.stack(ns, axis=1), jnp.stack(ms, axis=1))
```

```python
import functools

import jax
import jax.numpy as jnp
from jax import lax
from jax.experimental import pallas as pl
from jax.experimental.pallas import tpu as pltpu

F32 = jnp.float32
BF16 = jnp.bfloat16

D_MODEL = 1024
GRID_W = 64
M_HEADS = 4
M_HEAD_DIM = 128
M_WIDTH = M_HEADS * M_HEAD_DIM
M_CHUNK = 64
A_HEADS = 8
A_NOPE = 64
A_ROPE = 32
A_QK = A_NOPE + A_ROPE
A_VDIM = 64
A_WIDTH = A_HEADS * A_VDIM
Q_LORA = 384
KV_LORA = 256
ROPE_BASE = 10000.0
D_FF = 4 * D_MODEL
EPS = 1e-6

LANES = 128
HEAD_PAD = LANES
QK_PAD = A_HEADS * HEAD_PAD
N_GATES = 4 * M_HEADS
VMEM_LIMIT = 56 * 1024 * 1024

TOKEN_TILE = 256
Q_TILE = 256
ADA_TILE_N = 1536
FF_TILE = 1024


def _dot(a, b):
    return jnp.dot(a, b, preferred_element_type=F32)


def _dot_nt(a, b):
    return lax.dot_general(a, b, (((1,), (1,)), ((), ())), preferred_element_type=F32)


def _dot_tn(a, b):
    return lax.dot_general(a, b, (((0,), (0,)), ((), ())), preferred_element_type=F32)


def _rms(x, g):
    y = x * lax.rsqrt(jnp.mean(x * x, axis=-1, keepdims=True) + EPS)
    return y * g


def _params(sem):
    return pltpu.CompilerParams(dimension_semantics=sem, vmem_limit_bytes=VMEM_LIMIT)


def _const_spec(shape):
    zeros = (0,) * len(shape)
    return pl.BlockSpec(shape, lambda *_: zeros)


def _ada_kernel(cond_ref, w_ref, b_ref, o_ref):
    c = cond_ref[...]
    s = (c * jax.nn.sigmoid(c)).astype(BF16)
    o_ref[...] = _dot(s, w_ref[...].astype(BF16)) + b_ref[...]


def _ada(cond8, w_ada, b_ada):
    n = w_ada.shape[1]
    return pl.pallas_call(
        _ada_kernel,
        grid=(n // ADA_TILE_N,),
        in_specs=[
            pl.BlockSpec((8, D_MODEL), lambda j: (0, 0)),
            pl.BlockSpec((D_MODEL, ADA_TILE_N), lambda j: (0, j)),
            pl.BlockSpec((1, ADA_TILE_N), lambda j: (0, j)),
        ],
        out_specs=pl.BlockSpec((8, ADA_TILE_N), lambda j: (0, j)),
        out_shape=jax.ShapeDtypeStruct((8, n), F32),
        compiler_params=_params(("parallel",)),
        name="ada",
    )(cond8, w_ada, b_ada.reshape(1, n))


def _head_norm_rope(xh, g_pad, rope):
    ss = jnp.sum(xh * xh, axis=-1, keepdims=True) * (1.0 / A_QK)
    y = xh * lax.rsqrt(ss + EPS) * g_pad
    if rope is None:
        return y
    cos_t, sin_lo, sin_hi = rope
    return (y * cos_t + pltpu.roll(y, A_ROPE // 2, 1) * sin_hi
            + pltpu.roll(y, HEAD_PAD - A_ROPE // 2, 1) * sin_lo)


def _write_heads(src, extra, g_pad, rope, dst_ref):
    for h in range(A_HEADS):
        sl = slice(h * HEAD_PAD, (h + 1) * HEAD_PAD)
        xh = src[:, sl]
        if extra is not None:
            xh = xh + extra
        dst_ref[:, sl] = _head_norm_rope(xh, g_pad, rope).astype(dst_ref.dtype)


def _pre_kernel(has_rope, emit_cache, *refs):
    (x_ref, mod_ref, g1_ref, wmain_ref, wlat_ref, gbias_ref, qlg_ref, kvg_ref, wq_ref, wkv_ref,
     qhg_ref, khg_ref) = refs[:12]
    pos = 12
    rope = None
    if has_rope:
        rope_ref = refs[pos]
        pos += 1
        rope = (rope_ref[0], rope_ref[1], rope_ref[2])
    (mq_ref, mk_ref, mv_ref, mo_ref, gcol_ref, grow_ref, q_ref, k_ref, v_ref) = refs[pos:pos + 9]
    pos += 9

    x = x_ref[...]
    mod = mod_ref[...]
    sh1 = mod[:, 0:D_MODEL]
    sc1 = mod[:, D_MODEL:2 * D_MODEL]
    h = _rms(x, g1_ref[...]) * (1.0 + sc1) + sh1
    hb = h.astype(BF16)

    pm = _dot(hb, wmain_ref[...])
    mq_ref[...] = pm[:, 0:M_WIDTH].astype(BF16)
    mk_ref[...] = pm[:, M_WIDTH:2 * M_WIDTH] * (M_HEAD_DIM ** -0.5)
    mv_ref[...] = pm[:, 2 * M_WIDTH:3 * M_WIDTH].astype(BF16)
    mo_ref[...] = pm[:, 3 * M_WIDTH:4 * M_WIDTH]

    plat = _dot(hb, wlat_ref[...])
    q_lat = plat[:, 0:Q_LORA]
    kv_lat = plat[:, Q_LORA:Q_LORA + KV_LORA]
    tail = plat[:, Q_LORA + KV_LORA:Q_LORA + KV_LORA + LANES]

    lane = lax.broadcasted_iota(jnp.int32, (1, LANES), 1)
    g = tail + gbias_ref[...]
    log_f = jnp.minimum(g, 0.0) - jnp.log1p(jnp.exp(-jnp.abs(g)))
    gates = jnp.where((lane & 1) == 1, log_f, g)
    gcol_ref[...] = gates[:, 0:N_GATES]
    grow_ref[...] = gates.T[0:N_GATES, :]

    krope_placed = jnp.where((lane >= A_NOPE) & (lane < A_QK), tail, 0.0)
    ckv = _rms(kv_lat, kvg_ref[...])
    qn = _rms(q_lat, qlg_ref[...])
    qf = _dot(qn.astype(BF16), wq_ref[...])
    kvf = _dot(ckv.astype(BF16), wkv_ref[...])
    v_ref[...] = kvf[:, QK_PAD:QK_PAD + A_WIDTH].astype(BF16)
    _write_heads(qf, None, qhg_ref[...], rope, q_ref)
    _write_heads(kvf, krope_placed, khg_ref[...], rope, k_ref)

    if emit_cache:
        ckv_ref, krope_ref = refs[pos:pos + 2]
        ckv_ref[...] = ckv
        krope_ref[...] = tail[:, A_NOPE:A_QK]


def _pre(x, mod3, mod_row0, mod_per_batch, wts, rope_tab, emit_cache):
    B, T, _ = x.shape
    tm = TOKEN_TILE
    has_rope = rope_tab is not None
    tok = lambda w: pl.BlockSpec((None, tm, w), lambda b, i: (b, i, 0))
    in_specs = [
        tok(D_MODEL),
        pl.BlockSpec((None, 1, 6 * D_MODEL), lambda b, i: (mod_row0 + b * mod_per_batch, 0, 0)),
        _const_spec((1, D_MODEL)),
        _const_spec((D_MODEL, 4 * M_WIDTH)),
        _const_spec((D_MODEL, Q_LORA + KV_LORA + LANES)),
        _const_spec((1, LANES)),
        _const_spec((1, Q_LORA)),
        _const_spec((1, KV_LORA)),
        _const_spec((Q_LORA, QK_PAD)),
        _const_spec((KV_LORA, QK_PAD + A_WIDTH)),
        _const_spec((1, HEAD_PAD)),
        _const_spec((1, HEAD_PAD)),
    ]
    args = [x, mod3, wts["g1"], wts["w_main"], wts["w_lat"], wts["gate_bias"], wts["q_lora_g"],
            wts["kv_lora_g"], wts["w_q"], wts["w_kv"], wts["q_head_g"], wts["k_head_g"]]
    if has_rope:
        in_specs.append(pl.BlockSpec((3, tm, HEAD_PAD), lambda b, i: (0, i, 0)))
        args.append(rope_tab)
    out_specs = [tok(M_WIDTH), tok(M_WIDTH), tok(M_WIDTH), tok(M_WIDTH), tok(N_GATES),
                 pl.BlockSpec((None, N_GATES, tm), lambda b, i: (b, 0, i)),
                 tok(QK_PAD), tok(QK_PAD), tok(A_WIDTH)]
    out_shape = [
        jax.ShapeDtypeStruct((B, T, M_WIDTH), BF16),
        jax.ShapeDtypeStruct((B, T, M_WIDTH), F32),
        jax.ShapeDtypeStruct((B, T, M_WIDTH), BF16),
        jax.ShapeDtypeStruct((B, T, M_WIDTH), F32),
        jax.ShapeDtypeStruct((B, T, N_GATES), F32),
        jax.ShapeDtypeStruct((B, N_GATES, T), F32),
        jax.ShapeDtypeStruct((B, T, QK_PAD), BF16),
        jax.ShapeDtypeStruct((B, T, QK_PAD), BF16),
        jax.ShapeDtypeStruct((B, T, A_WIDTH), BF16),
    ]
    if emit_cache:
        out_specs += [tok(KV_LORA), tok(A_ROPE)]
        out_shape += [jax.ShapeDtypeStruct((B, T, KV_LORA), F32),
                      jax.ShapeDtypeStruct((B, T, A_ROPE), F32)]
    return pl.pallas_call(
        functools.partial(_pre_kernel, has_rope, emit_cache),
        grid=(B, T // tm),
        in_specs=in_specs,
        out_specs=out_specs,
        out_shape=out_shape,
        compiler_params=_params(("parallel", "parallel")),
        name="pre_latent" if has_rope else "pre_context",
    )(*args)


def _ctxkv_kernel(ckv_ref, krp_ref, wkv_ref, khg_ref, k_ref, v_ref):
    kvf = _dot(ckv_ref[...].astype(BF16), wkv_ref[...])
    v_ref[...] = kvf[:, QK_PAD:QK_PAD + A_WIDTH].astype(BF16)
    _write_heads(kvf, krp_ref[...], khg_ref[...], None, k_ref)


def _ctxkv(ckv, krope_placed, wts):
    B, P, _ = ckv.shape
    tok = lambda w: pl.BlockSpec((None, P, w), lambda b: (b, 0, 0))
    return pl.pallas_call(
        _ctxkv_kernel,
        grid=(B,),
        in_specs=[tok(KV_LORA), tok(HEAD_PAD), _const_spec((KV_LORA, QK_PAD + A_WIDTH)),
                  _const_spec((1, HEAD_PAD))],
        out_specs=[tok(QK_PAD), tok(A_WIDTH)],
        out_shape=[jax.ShapeDtypeStruct((B, P, QK_PAD), BF16),
                   jax.ShapeDtypeStruct((B, P, A_WIDTH), BF16)],
        compiler_params=_params(("parallel",)),
        name="ctx_kv",
    )(ckv, krope_placed, wts["w_kv"], wts["k_head_g"])


def _mlstm_chunk(q, k, v, i_row, f_row, i_col, f_col, allow, allow_t, C, n, m):
    b_col = jnp.sum(jnp.where(allow, f_row, 0.0), axis=1, keepdims=True)
    b_row = jnp.sum(jnp.where(allow_t, f_col, 0.0), axis=0, keepdims=True)
    total = jnp.sum(f_row, axis=1, keepdims=True)
    a_row = i_row - b_row
    a_col = i_col - b_col
    dmat = jnp.where(allow, b_col + a_row, -jnp.inf)
    m_inter = b_col + m
    m_t = jnp.maximum(m_inter, jnp.max(dmat, axis=1, keepdims=True))
    w_intra = jnp.exp(dmat - m_t)
    w_inter = jnp.exp(m_inter - m_t)
    s = _dot_nt(q, k.astype(BF16)) * w_intra
    num = _dot(s.astype(BF16), v) + w_inter * _dot(q, C.astype(BF16))
    den = (jnp.sum(s, axis=1, keepdims=True)
           + w_inter * jnp.sum(q.astype(F32) * n, axis=1, keepdims=True))
    h = num / jnp.maximum(jnp.abs(den), jnp.exp(-m_t))
    log_w = total + a_col
    m_new = jnp.maximum(total + m, jnp.max(log_w, axis=0, keepdims=True))
    w_col = jnp.exp(log_w - m_new)
    decay = jnp.exp(total + m - m_new)
    kw = k * w_col
    C_new = decay * C + _dot_tn(kw.astype(BF16), v)
    n_new = decay * n + jnp.sum(kw, axis=0, keepdims=True)
    return h, C_new, n_new, m_new


def _mlstm_kernel(has_init, emit_state, n_chunks, *refs):
    q_ref, k_ref, v_ref, mo_ref, gcol_ref, grow_ref, ng_ref = refs[:7]
    pos = 7
    if has_init:
        c0_ref, n0_ref, m0_ref = refs[pos:pos + 3]
        pos += 3
    hm_ref = refs[pos]
    pos += 1
    if emit_state:
        c_ref, n_ref, m_ref = refs[pos:pos + 3]
        pos += 3
    hf_scr, hb_scr = refs[pos:pos + 2]

    L = M_CHUNK
    t_idx = lax.broadcasted_iota(jnp.int32, (L, L), 0)
    s_idx = lax.broadcasted_iota(jnp.int32, (L, L), 1)
    causal = s_idx <= t_idx
    anti = s_idx >= t_idx

    if has_init:
        init = (c0_ref[0], n0_ref[0], m0_ref[0:1, 0:1], c0_ref[1], n0_ref[1], m0_ref[1:2, 0:1])
    else:
        zc = jnp.zeros((M_HEAD_DIM, M_HEAD_DIM), F32)
        zn = jnp.zeros((1, M_HEAD_DIM), F32)
        zm = jnp.zeros((1, 1), F32)
        init = (zc, zn, zm, zc, zn, zm)

    def load(c, direction):
        o = pl.multiple_of(c * L, L)
        rows = pl.ds(o, L)
        gc = gcol_ref[rows, :]
        i_col = gc[:, 2 * direction:2 * direction + 1]
        f_col = gc[:, 2 * direction + 1:2 * direction + 2]
        i_row = grow_ref[2 * direction, pl.ds(c, 1), :]
        f_row = grow_ref[2 * direction + 1, pl.ds(c, 1), :]
        return rows, (q_ref[rows, :], k_ref[rows, :], v_ref[rows, :], i_row, f_row, i_col, f_col)

    def body(c, carry):
        Cf, nf, mf, Cb, nb, mb = carry
        rows_f, args_f = load(c, 0)
        hf, Cf, nf, mf = _mlstm_chunk(*args_f, causal, anti, Cf, nf, mf)
        hf_scr[rows_f, :] = hf
        rows_b, args_b = load(n_chunks - 1 - c, 1)
        hb, Cb, nb, mb = _mlstm_chunk(*args_b, anti, causal, Cb, nb, mb)
        hb_scr[rows_b, :] = hb
        return Cf, nf, mf, Cb, nb, mb

    Cf, nf, mf, Cb, nb, mb = lax.fori_loop(0, n_chunks, body, init)

    hs = hf_scr[...] + hb_scr[...]
    hn = _rms(hs, ng_ref[...])
    hm_ref[...] = (hn * jax.nn.sigmoid(mo_ref[...])).astype(hm_ref.dtype)

    if emit_state:
        c_ref[0] = Cf
        c_ref[1] = Cb
        n_ref[0] = nf
        n_ref[1] = nb
        m_ref[0:1, :] = jnp.broadcast_to(mf, (1, LANES))
        m_ref[1:2, :] = jnp.broadcast_to(mb, (1, LANES))


def _mlstm(mq, mk, mv, mo, gcol, grow, norm_g, init_state, emit_state):
    B, T, _ = mq.shape
    H, Dh, L = M_HEADS, M_HEAD_DIM, M_CHUNK
    nc = T // L
    has_init = init_state is not None
    head = pl.BlockSpec((None, T, Dh), lambda b, h: (b, 0, h))
    in_specs = [head, head, head, head,
                pl.BlockSpec((None, None, T, 4), lambda b, h: (b, h, 0, 0)),
                pl.BlockSpec((None, None, 4, nc, L), lambda b, h: (b, h, 0, 0, 0)),
                pl.BlockSpec((None, 1, Dh), lambda b, h: (h, 0, 0))]
    args = [mq, mk, mv, mo, gcol, grow, norm_g]
    state_specs = [pl.BlockSpec((None, 2, None, Dh, Dh), lambda b, h: (b, 0, h, 0, 0)),
                   pl.BlockSpec((None, 2, None, 1, Dh), lambda b, h: (b, 0, h, 0, 0)),
                   pl.BlockSpec((None, None, 2, LANES), lambda b, h: (b, h, 0, 0))]
    if has_init:
        in_specs += state_specs
        args += list(init_state)
    out_specs = [head]
    out_shape = [jax.ShapeDtypeStruct((B, T, M_WIDTH), BF16)]
    if emit_state:
        out_specs += state_specs
        out_shape += [jax.ShapeDtypeStruct((B, 2, H, Dh, Dh), F32),
                      jax.ShapeDtypeStruct((B, 2, H, 1, Dh), F32),
                      jax.ShapeDtypeStruct((B, H, 2, LANES), F32)]
    return pl.pallas_call(
        functools.partial(_mlstm_kernel, has_init, emit_state, nc),
        grid=(B, H),
        in_specs=in_specs,
        out_specs=out_specs,
        out_shape=out_shape,
        scratch_shapes=[pltpu.VMEM((T, Dh), F32), pltpu.VMEM((T, Dh), F32)],
        compiler_params=_params(("parallel", "parallel")),
        name="mlstm_latent" if has_init else "mlstm_context",
    )(*args)


def _attn_kernel(has_ctx, *refs):
    if has_ctx:
        q_ref, k_ref, v_ref, kc_ref, vc_ref, o_ref = refs
    else:
        q_ref, k_ref, v_ref, o_ref = refs
    scale = A_QK ** -0.5
    lane = lax.broadcasted_iota(jnp.int32, (1, LANES), 1)
    for pair in range(A_HEADS // 2):
        vsl = slice(pair * LANES, (pair + 1) * LANES)
        outs = []
        for e in range(2):
            hsl = slice((2 * pair + e) * HEAD_PAD, (2 * pair + e + 1) * HEAD_PAD)
            qh = q_ref[:, hsl]
            s = _dot_nt(qh, k_ref[:, hsl]) * scale
            mx = jnp.max(s, axis=1, keepdims=True)
            if has_ctx:
                sc = _dot_nt(qh, kc_ref[:, hsl]) * scale
                mx = jnp.maximum(mx, jnp.max(sc, axis=1, keepdims=True))
            p = jnp.exp(s - mx)
            den = jnp.sum(p, axis=1, keepdims=True)
            o = _dot(p.astype(BF16), v_ref[:, vsl])
            if has_ctx:
                pc = jnp.exp(sc - mx)
                den = den + jnp.sum(pc, axis=1, keepdims=True)
                o = o + _dot(pc.astype(BF16), vc_ref[:, vsl])
            outs.append(o / den)
        o_ref[:, vsl] = jnp.where(lane < A_VDIM, outs[0], outs[1]).astype(o_ref.dtype)


def _attn(q, k, v, ctx_kv):
    B, T, _ = q.shape
    tq = Q_TILE
    has_ctx = ctx_kv is not None
    full = lambda n, w: pl.BlockSpec((None, n, w), lambda b, i: (b, 0, 0))
    in_specs = [pl.BlockSpec((None, tq, QK_PAD), lambda b, i: (b, i, 0)), full(T, QK_PAD), full(T, A_WIDTH)]
    args = [q, k, v]
    if has_ctx:
        P = ctx_kv[0].shape[1]
        in_specs += [full(P, QK_PAD), full(P, A_WIDTH)]
        args += list(ctx_kv)
    return pl.pallas_call(
        functools.partial(_attn_kernel, has_ctx),
        grid=(B, T // tq),
        in_specs=in_specs,
        out_specs=pl.BlockSpec((None, tq, A_WIDTH), lambda b, i: (b, i, 0)),
        out_shape=jax.ShapeDtypeStruct((B, T, A_WIDTH), BF16),
        compiler_params=_params(("parallel", "parallel")),
        name="attn_latent" if has_ctx else "attn_context",
    )(*args)


def _post_kernel(x_ref, hm_ref, ha_ref, mod_ref, g2_ref, wout_ref, wup_ref, wdown_ref, y_ref):
    mod = mod_ref[...]
    gate1 = mod[:, 2 * D_MODEL:3 * D_MODEL]
    sh2 = mod[:, 3 * D_MODEL:4 * D_MODEL]
    sc2 = mod[:, 4 * D_MODEL:5 * D_MODEL]
    gate2 = mod[:, 5 * D_MODEL:6 * D_MODEL]
    mix = jnp.concatenate([hm_ref[...], ha_ref[...]], axis=-1)
    x1 = x_ref[...] + gate1 * _dot(mix, wout_ref[...])
    h2 = (_rms(x1, g2_ref[...]) * (1.0 + sc2) + sh2).astype(BF16)
    acc = jnp.zeros_like(x1)
    for c in range(D_FF // FF_TILE):
        sl = slice(c * FF_TILE, (c + 1) * FF_TILE)
        u = jnp.maximum(_dot(h2, wup_ref[:, sl]), 0.0)
        acc = acc + _dot((u * u).astype(BF16), wdown_ref[sl, :])
    y_ref[...] = x1 + gate2 * acc


def _post(x, hm, ha, mod3, mod_row0, mod_per_batch, wts):
    B, T, _ = x.shape
    tm = TOKEN_TILE
    tok = lambda w: pl.BlockSpec((None, tm, w), lambda b, i: (b, i, 0))
    return pl.pallas_call(
        _post_kernel,
        grid=(B, T // tm),
        in_specs=[tok(D_MODEL), tok(M_WIDTH), tok(A_WIDTH),
                  pl.BlockSpec((None, 1, 6 * D_MODEL), lambda b, i: (mod_row0 + b * mod_per_batch, 0, 0)),
                  _const_spec((1, D_MODEL)),
                  _const_spec((M_WIDTH + A_WIDTH, D_MODEL)),
                  _const_spec((D_MODEL, D_FF)),
                  _const_spec((D_FF, D_MODEL))],
        out_specs=tok(D_MODEL),
        out_shape=jax.ShapeDtypeStruct((B, T, D_MODEL), F32),
        compiler_params=_params(("parallel", "parallel")),
        name="post",
    )(x, hm, ha, mod3, wts["g2"], wts["w_out"], wts["w_up"], wts["w_down"])


def _prepare_weights(norm1_g, norm2_g, w_in, mlstm_gate_b, q_lora_g, kv_lora_g, w_q_up, w_kv_up,
                     q_head_g, k_head_g, w_out, w_mlp_up, w_mlp_down):
    o_g = 4 * M_WIDTH
    o_q = o_g + N_GATES
    o_kv = o_q + Q_LORA
    o_kr = o_kv + KV_LORA
    perm = jnp.arange(N_GATES).reshape(2, 2, M_HEADS).transpose(2, 0, 1).reshape(-1)
    w_gate = w_in[:, o_g:o_q][:, perm]
    tail = jnp.concatenate([
        w_gate, jnp.zeros((D_MODEL, A_NOPE - N_GATES), F32),
        w_in[:, o_kr:o_kr + A_ROPE], jnp.zeros((D_MODEL, LANES - A_QK), F32)], axis=1)
    w_lat = jnp.concatenate([w_in[:, o_q:o_kv], w_in[:, o_kv:o_kr], tail], axis=1)
    gate_bias = jnp.pad(mlstm_gate_b[perm], (0, LANES - N_GATES)).reshape(1, LANES)
    w_q = jnp.pad(w_q_up.reshape(Q_LORA, A_HEADS, A_QK), ((0, 0), (0, 0), (0, HEAD_PAD - A_QK)))
    w_kv3 = w_kv_up.reshape(KV_LORA, A_HEADS, A_NOPE + A_VDIM)
    w_k = jnp.pad(w_kv3[:, :, :A_NOPE], ((0, 0), (0, 0), (0, HEAD_PAD - A_NOPE)))
    w_v = w_kv3[:, :, A_NOPE:]
    w_kv = jnp.concatenate([w_k.reshape(KV_LORA, QK_PAD), w_v.reshape(KV_LORA, A_WIDTH)], axis=1)
    pad_head = lambda g: jnp.pad(g, (0, HEAD_PAD - A_QK)).reshape(1, HEAD_PAD)
    return {
        "g1": norm1_g.reshape(1, D_MODEL),
        "g2": norm2_g.reshape(1, D_MODEL),
        "w_main": w_in[:, :o_g].astype(BF16),
        "w_lat": w_lat.astype(BF16),
        "gate_bias": gate_bias,
        "q_lora_g": q_lora_g.reshape(1, Q_LORA),
        "kv_lora_g": kv_lora_g.reshape(1, KV_LORA),
        "w_q": w_q.reshape(Q_LORA, QK_PAD).astype(BF16),
        "w_kv": w_kv.astype(BF16),
        "q_head_g": pad_head(q_head_g),
        "k_head_g": pad_head(k_head_g),
        "w_out": w_out.astype(BF16),
        "w_up": w_mlp_up.astype(BF16),
        "w_down": w_mlp_down.astype(BF16),
    }


def _rope_tables(T):
    rows = T // GRID_W
    row = jnp.repeat(jnp.arange(rows, dtype=F32), GRID_W)
    col = jnp.tile(jnp.arange(GRID_W, dtype=F32), rows)
    half = A_ROPE // 2
    inv = ROPE_BASE ** (-jnp.arange(0, half, 2, dtype=F32) / half)
    ang = jnp.concatenate([row[:, None] * inv, col[:, None] * inv], axis=-1)
    cos, sin = jnp.cos(ang), jnp.sin(ang)
    ones = jnp.ones((T, A_NOPE), F32)
    z = lambda w: jnp.zeros((T, w), F32)
    tail = LANES - A_QK
    cos_t = jnp.concatenate([ones, cos, cos, z(tail)], axis=1)
    sin_lo = jnp.concatenate([z(A_NOPE), -sin, z(half), z(tail)], axis=1)
    sin_hi = jnp.concatenate([z(A_NOPE), z(half), sin, z(tail)], axis=1)
    return jnp.stack([cos_t, sin_lo, sin_hi], axis=0)


def _gate_layouts(gcol, grow):
    B, T, _ = gcol.shape
    gcol = gcol.reshape(B, T, M_HEADS, 4).transpose(0, 2, 1, 3)
    grow = grow.reshape(B, M_HEADS, 4, T // M_CHUNK, M_CHUNK)
    return gcol, grow


def _layer_pass(x, mod3, mod_row0, mod_per_batch, wts, norm_g, rope_tab, init_state, ctx_kv, is_context):
    pre = _pre(x, mod3, mod_row0, mod_per_batch, wts, rope_tab, emit_cache=is_context)
    mq, mk, mv, mo, gcol, grow, q, k, v = pre[:9]
    gcol, grow = _gate_layouts(gcol, grow)
    ml = _mlstm(mq, mk, mv, mo, gcol, grow, norm_g, init_state, emit_state=is_context)
    ha = _attn(q, k, v, ctx_kv)
    y = _post(x, ml[0], ha, mod3, mod_row0, mod_per_batch, wts)
    return y, pre[9:], ml[1:]


def kernel(x_prompt, x_sample, cache_mla_ckv, cache_mla_krope, state_mlstm_C, state_mlstm_n, state_mlstm_m,
           c, c_ctx, norm1_g, norm2_g, w_ada, b_ada, w_in, mlstm_gate_b, mlstm_norm_g,
           q_lora_g, kv_lora_g, w_q_up, w_kv_up, q_head_g, k_head_g, w_out, w_mlp_up, w_mlp_down):
    depth = w_in.shape[0]
    Bd = x_sample.shape[0]
    cond8 = jnp.concatenate([c_ctx[None, :], c, jnp.zeros((8 - 1 - Bd, D_MODEL), F32)], axis=0)
    rope_tab = _rope_tables(x_sample.shape[1])

    y, z = x_prompt, x_sample
    ckvs, kropes, Cs, ns, ms = [], [], [], [], []
    for l in range(depth):
        wts = _prepare_weights(norm1_g[l], norm2_g[l], w_in[l], mlstm_gate_b[l], q_lora_g[l], kv_lora_g[l],
                               w_q_up[l], w_kv_up[l], q_head_g[l], k_head_g[l], w_out[l], w_mlp_up[l],
                               w_mlp_down[l])
        norm_g = mlstm_norm_g[l].reshape(M_HEADS, 1, M_HEAD_DIM)
        mod3 = _ada(cond8, w_ada[l], b_ada[l]).reshape(8, 1, 6 * D_MODEL)

        y, (ckv, krope), (C_new, n_new, m_new) = _layer_pass(
            y, mod3, 0, 0, wts, norm_g, None, None, None, True)
        ckvs.append(ckv)
        kropes.append(krope)
        Cs.append(C_new)
        ns.append(n_new[:, :, :, 0, :])
        ms.append(m_new[:, :, :, 0].transpose(0, 2, 1))

        init_state = (state_mlstm_C[:, l],
                      state_mlstm_n[:, l][:, :, :, None, :],
                      jnp.broadcast_to(state_mlstm_m[:, l].transpose(0, 2, 1)[..., None],
                                       (Bd, M_HEADS, 2, LANES)))
        krope_placed = jnp.pad(cache_mla_krope[:, l], ((0, 0), (0, 0), (A_NOPE, LANES - A_QK)))
        ctx_kv = _ctxkv(cache_mla_ckv[:, l], krope_placed, wts)
        z, _, _ = _layer_pass(z, mod3, 1, 1, wts, norm_g, rope_tab, init_state, ctx_kv, False)

    return (y, z, jnp.stack(ckvs, axis=1), jnp.stack(kropes, axis=1), jnp.stack(Cs, axis=1),
            jnp.stack(ns, axis=1), jnp.stack(ms, axis=1))
```

```python
import functools

import jax
import jax.numpy as jnp
from jax import lax
from jax.experimental import pallas as pl
from jax.experimental.pallas import tpu as pltpu

F32 = jnp.float32
BF16 = jnp.bfloat16

D_MODEL = 1024
GRID_W = 64
M_HEADS = 4
M_HEAD_DIM = 128
M_WIDTH = M_HEADS * M_HEAD_DIM
M_BLOCK = 256
A_HEADS = 8
A_NOPE = 64
A_ROPE = 32
A_QK = A_NOPE + A_ROPE
A_VDIM = 64
A_WIDTH = A_HEADS * A_VDIM
Q_LORA = 384
KV_LORA = 256
ROPE_BASE = 10000.0
D_FF = 4 * D_MODEL
EPS = 1e-6

LANES = 128
HEAD_PAD = LANES
QK_PAD = A_HEADS * HEAD_PAD
N_GATES = 4 * M_HEADS
VMEM_LIMIT = 56 * 1024 * 1024

TOKEN_TILE = 256
Q_TILE = 256
ADA_TILE_N = 1536
FF_TILE = 1024


def _dot(a, b):
    return jnp.dot(a, b, preferred_element_type=F32)


def _dot_nt(a, b):
    return lax.dot_general(a, b, (((1,), (1,)), ((), ())), preferred_element_type=F32)


def _dot_tn(a, b):
    return lax.dot_general(a, b, (((0,), (0,)), ((), ())), preferred_element_type=F32)


def _rms(x, g):
    y = x * lax.rsqrt(jnp.mean(x * x, axis=-1, keepdims=True) + EPS)
    return y * g


def _params(sem):
    return pltpu.CompilerParams(dimension_semantics=sem, vmem_limit_bytes=VMEM_LIMIT)


def _const_spec(shape):
    zeros = (0,) * len(shape)
    return pl.BlockSpec(shape, lambda *_: zeros)


def _ada_kernel(cond_ref, w_ref, b_ref, o_ref):
    c = cond_ref[...]
    s = (c * jax.nn.sigmoid(c)).astype(BF16)
    o_ref[...] = _dot(s, w_ref[...].astype(BF16)) + b_ref[...]


def _ada(cond8, w_ada, b_ada):
    n = w_ada.shape[1]
    return pl.pallas_call(
        _ada_kernel,
        grid=(n // ADA_TILE_N,),
        in_specs=[
            pl.BlockSpec((8, D_MODEL), lambda j: (0, 0)),
            pl.BlockSpec((D_MODEL, ADA_TILE_N), lambda j: (0, j)),
            pl.BlockSpec((1, ADA_TILE_N), lambda j: (0, j)),
        ],
        out_specs=pl.BlockSpec((8, ADA_TILE_N), lambda j: (0, j)),
        out_shape=jax.ShapeDtypeStruct((8, n), F32),
        compiler_params=_params(("parallel",)),
        name="ada",
    )(cond8, w_ada, b_ada.reshape(1, n))


def _head_norm_rope(xh, g_pad, rope):
    ss = jnp.sum(xh * xh, axis=-1, keepdims=True) * (1.0 / A_QK)
    y = xh * lax.rsqrt(ss + EPS) * g_pad
    if rope is None:
        return y
    cos_t, sin_lo, sin_hi = rope
    return (y * cos_t + pltpu.roll(y, A_ROPE // 2, 1) * sin_hi
            + pltpu.roll(y, HEAD_PAD - A_ROPE // 2, 1) * sin_lo)


def _write_heads(src, extra, g_pad, rope, dst_ref):
    for h in range(A_HEADS):
        sl = slice(h * HEAD_PAD, (h + 1) * HEAD_PAD)
        xh = src[:, sl]
        if extra is not None:
            xh = xh + extra
        dst_ref[:, sl] = _head_norm_rope(xh, g_pad, rope).astype(dst_ref.dtype)


def _pre_kernel(has_rope, emit_cache, *refs):
    (x_ref, mod_ref, g1_ref, wmain_ref, wlat_ref, gbias_ref, qlg_ref, kvg_ref, wq_ref, wkv_ref,
     qhg_ref, khg_ref) = refs[:12]
    pos = 12
    rope = None
    if has_rope:
        rope_ref = refs[pos]
        pos += 1
        rope = (rope_ref[0], rope_ref[1], rope_ref[2])
    (mq_ref, mk_ref, mv_ref, mo_ref, gcol_ref, grow_ref, q_ref, k_ref, v_ref) = refs[pos:pos + 9]
    pos += 9

    x = x_ref[...]
    mod = mod_ref[...]
    sh1 = mod[:, 0:D_MODEL]
    sc1 = mod[:, D_MODEL:2 * D_MODEL]
    h = _rms(x, g1_ref[...]) * (1.0 + sc1) + sh1
    hb = h.astype(BF16)

    pm = _dot(hb, wmain_ref[...])
    mq_ref[...] = pm[:, 0:M_WIDTH].astype(BF16)
    mk_ref[...] = pm[:, M_WIDTH:2 * M_WIDTH] * (M_HEAD_DIM ** -0.5)
    mv_ref[...] = pm[:, 2 * M_WIDTH:3 * M_WIDTH].astype(BF16)
    mo_ref[...] = pm[:, 3 * M_WIDTH:4 * M_WIDTH]

    plat = _dot(hb, wlat_ref[...])
    q_lat = plat[:, 0:Q_LORA]
    kv_lat = plat[:, Q_LORA:Q_LORA + KV_LORA]
    tail = plat[:, Q_LORA + KV_LORA:Q_LORA + KV_LORA + LANES]

    lane = lax.broadcasted_iota(jnp.int32, (1, LANES), 1)
    g = tail + gbias_ref[...]
    log_f = jnp.minimum(g, 0.0) - jnp.log1p(jnp.exp(-jnp.abs(g)))
    gates = jnp.where((lane & 1) == 1, log_f, g)
    gcol_ref[...] = gates[:, 0:N_GATES]
    grow_ref[...] = gates.T[0:N_GATES, :]

    krope_placed = jnp.where((lane >= A_NOPE) & (lane < A_QK), tail, 0.0)
    ckv = _rms(kv_lat, kvg_ref[...])
    qn = _rms(q_lat, qlg_ref[...])
    qf = _dot(qn.astype(BF16), wq_ref[...])
    kvf = _dot(ckv.astype(BF16), wkv_ref[...])
    v_ref[...] = kvf[:, QK_PAD:QK_PAD + A_WIDTH].astype(BF16)
    _write_heads(qf, None, qhg_ref[...], rope, q_ref)
    _write_heads(kvf, krope_placed, khg_ref[...], rope, k_ref)

    if emit_cache:
        ckv_ref, krope_ref = refs[pos:pos + 2]
        ckv_ref[...] = ckv
        krope_ref[...] = tail[:, A_NOPE:A_QK]


def _pre(x, mod3, mod_row0, mod_per_batch, wts, rope_tab, emit_cache):
    B, T, _ = x.shape
    tm = TOKEN_TILE
    has_rope = rope_tab is not None
    tok = lambda w: pl.BlockSpec((None, tm, w), lambda b, i: (b, i, 0))
    in_specs = [
        tok(D_MODEL),
        pl.BlockSpec((None, 1, 6 * D_MODEL), lambda b, i: (mod_row0 + b * mod_per_batch, 0, 0)),
        _const_spec((1, D_MODEL)),
        _const_spec((D_MODEL, 4 * M_WIDTH)),
        _const_spec((D_MODEL, Q_LORA + KV_LORA + LANES)),
        _const_spec((1, LANES)),
        _const_spec((1, Q_LORA)),
        _const_spec((1, KV_LORA)),
        _const_spec((Q_LORA, QK_PAD)),
        _const_spec((KV_LORA, QK_PAD + A_WIDTH)),
        _const_spec((1, HEAD_PAD)),
        _const_spec((1, HEAD_PAD)),
    ]
    args = [x, mod3, wts["g1"], wts["w_main"], wts["w_lat"], wts["gate_bias"], wts["q_lora_g"],
            wts["kv_lora_g"], wts["w_q"], wts["w_kv"], wts["q_head_g"], wts["k_head_g"]]
    if has_rope:
        in_specs.append(pl.BlockSpec((3, tm, HEAD_PAD), lambda b, i: (0, i, 0)))
        args.append(rope_tab)
    out_specs = [tok(M_WIDTH), tok(M_WIDTH), tok(M_WIDTH), tok(M_WIDTH), tok(N_GATES),
                 pl.BlockSpec((None, N_GATES, tm), lambda b, i: (b, 0, i)),
                 tok(QK_PAD), tok(QK_PAD), tok(A_WIDTH)]
    out_shape = [
        jax.ShapeDtypeStruct((B, T, M_WIDTH), BF16),
        jax.ShapeDtypeStruct((B, T, M_WIDTH), F32),
        jax.ShapeDtypeStruct((B, T, M_WIDTH), BF16),
        jax.ShapeDtypeStruct((B, T, M_WIDTH), F32),
        jax.ShapeDtypeStruct((B, T, N_GATES), F32),
        jax.ShapeDtypeStruct((B, N_GATES, T), F32),
        jax.ShapeDtypeStruct((B, T, QK_PAD), BF16),
        jax.ShapeDtypeStruct((B, T, QK_PAD), BF16),
        jax.ShapeDtypeStruct((B, T, A_WIDTH), BF16),
    ]
    if emit_cache:
        out_specs += [tok(KV_LORA), tok(A_ROPE)]
        out_shape += [jax.ShapeDtypeStruct((B, T, KV_LORA), F32),
                      jax.ShapeDtypeStruct((B, T, A_ROPE), F32)]
    return pl.pallas_call(
        functools.partial(_pre_kernel, has_rope, emit_cache),
        grid=(B, T // tm),
        in_specs=in_specs,
        out_specs=out_specs,
        out_shape=out_shape,
        compiler_params=_params(("parallel", "parallel")),
        name="pre_latent" if has_rope else "pre_context",
    )(*args)


def _ctxkv_kernel(ckv_ref, krp_ref, wkv_ref, khg_ref, k_ref, v_ref):
    kvf = _dot(ckv_ref[...].astype(BF16), wkv_ref[...])
    v_ref[...] = kvf[:, QK_PAD:QK_PAD + A_WIDTH].astype(BF16)
    _write_heads(kvf, krp_ref[...], khg_ref[...], None, k_ref)


def _ctxkv(ckv, krope_placed, wts):
    B, P, _ = ckv.shape
    tok = lambda w: pl.BlockSpec((None, P, w), lambda b: (b, 0, 0))
    return pl.pallas_call(
        _ctxkv_kernel,
        grid=(B,),
        in_specs=[tok(KV_LORA), tok(HEAD_PAD), _const_spec((KV_LORA, QK_PAD + A_WIDTH)),
                  _const_spec((1, HEAD_PAD))],
        out_specs=[tok(QK_PAD), tok(A_WIDTH)],
        out_shape=[jax.ShapeDtypeStruct((B, P, QK_PAD), BF16),
                   jax.ShapeDtypeStruct((B, P, A_WIDTH), BF16)],
        compiler_params=_params(("parallel",)),
        name="ctx_kv",
    )(ckv, krope_placed, wts["w_kv"], wts["k_head_g"])


def _mlstm_chunk(s_raw, q, k, v, i_row, f_row, i_col, f_col, allow, allow_t, C, n, m):
    b_col = jnp.sum(jnp.where(allow, f_row, 0.0), axis=1, keepdims=True)
    b_row = jnp.sum(jnp.where(allow_t, f_col, 0.0), axis=0, keepdims=True)
    total = jnp.sum(f_row, axis=1, keepdims=True)
    a_row = i_row - b_row
    a_col = i_col - b_col
    dmat = jnp.where(allow, b_col + a_row, -jnp.inf)
    m_inter = b_col + m
    m_t = jnp.maximum(m_inter, jnp.max(dmat, axis=1, keepdims=True))
    w_intra = jnp.exp(dmat - m_t)
    w_inter = jnp.exp(m_inter - m_t)
    s = s_raw * w_intra
    num = _dot(s.astype(BF16), v) + w_inter * _dot(q, C.astype(BF16))
    den = (jnp.sum(s, axis=1, keepdims=True)
           + w_inter * jnp.sum(q.astype(F32) * n, axis=1, keepdims=True))
    h = num / jnp.maximum(jnp.abs(den), jnp.exp(-m_t))
    log_w = total + a_col
    m_new = jnp.maximum(total + m, jnp.max(log_w, axis=0, keepdims=True))
    w_col = jnp.exp(log_w - m_new)
    decay = jnp.exp(total + m - m_new)
    kw = k * w_col
    C_new = decay * C + _dot_tn(kw.astype(BF16), v)
    n_new = decay * n + jnp.sum(kw, axis=0, keepdims=True)
    return h, C_new, n_new, m_new


def _mlstm_kernel(has_init, emit_state, n_blocks, heads, *refs):
    q_ref, k_ref, v_ref, mo_ref, gcol_ref, grow_ref, ng_ref = refs[:7]
    pos = 7
    if has_init:
        c0_ref, n0_ref, m0_ref = refs[pos:pos + 3]
        pos += 3
    hm_ref = refs[pos]
    pos += 1
    if emit_state:
        c_ref, n_ref, m_ref = refs[pos:pos + 3]
        pos += 3

    L, Dh = M_BLOCK, M_HEAD_DIM
    t_idx = lax.broadcasted_iota(jnp.int32, (L, L), 0)
    s_idx = lax.broadcasted_iota(jnp.int32, (L, L), 1)
    causal = s_idx <= t_idx
    anti = s_idx >= t_idx

    def init_state(j):
        if has_init:
            return (c0_ref[0, j], n0_ref[0, j], m0_ref[j, 0:1, 0:1],
                    c0_ref[1, j], n0_ref[1, j], m0_ref[j, 1:2, 0:1])
        zc = jnp.zeros((Dh, Dh), F32)
        zn = jnp.zeros((1, Dh), F32)
        zm = jnp.zeros((1, 1), F32)
        return (zc, zn, zm, zc, zn, zm)

    def load(j, c):
        cols = slice(j * Dh, (j + 1) * Dh)
        if isinstance(c, int):
            rows, crow = slice(c * L, (c + 1) * L), slice(c, c + 1)
        else:
            rows, crow = pl.ds(pl.multiple_of(c * L, L), L), pl.ds(c, 1)
        q, k, v = q_ref[rows, cols], k_ref[rows, cols], v_ref[rows, cols]
        return rows, _dot_nt(q, k.astype(BF16)), q, k, v, gcol_ref[j, rows, :], crow

    def gates(j, gc, crow, direction):
        return (grow_ref[j, 2 * direction, crow, :], grow_ref[j, 2 * direction + 1, crow, :],
                gc[:, 2 * direction:2 * direction + 1], gc[:, 2 * direction + 1:2 * direction + 2])

    def finish(j, rows, hs):
        cols = slice(j * Dh, (j + 1) * Dh)
        hn = _rms(hs, ng_ref[j])
        hm_ref[rows, cols] = (hn * jax.nn.sigmoid(mo_ref[rows, cols])).astype(hm_ref.dtype)

    def emit(j, state):
        Cf, nf, mf, Cb, nb, mb = state
        c_ref[0, j] = Cf
        c_ref[1, j] = Cb
        n_ref[0, j] = nf
        n_ref[1, j] = nb
        m_ref[j, 0:1, :] = jnp.broadcast_to(mf, (1, LANES))
        m_ref[j, 1:2, :] = jnp.broadcast_to(mb, (1, LANES))

    if n_blocks == 1:
        for j in range(heads):
            Cf, nf, mf, Cb, nb, mb = init_state(j)
            rows, s_raw, q, k, v, gc, crow = load(j, 0)
            hf, Cf, nf, mf = _mlstm_chunk(s_raw, q, k, v, *gates(j, gc, crow, 0), causal, anti, Cf, nf, mf)
            hb, Cb, nb, mb = _mlstm_chunk(s_raw, q, k, v, *gates(j, gc, crow, 1), anti, causal, Cb, nb, mb)
            finish(j, rows, hf + hb)
            if emit_state:
                emit(j, (Cf, nf, mf, Cb, nb, mb))
        return

    hf_scr, hb_scr = refs[pos:pos + 2]

    def body(c, carry):
        out = []
        for j in range(heads):
            cols = slice(j * Dh, (j + 1) * Dh)
            Cf, nf, mf, Cb, nb, mb = carry[6 * j:6 * j + 6]
            rows, s_raw, q, k, v, gc, crow = load(j, c)
            hf, Cf, nf, mf = _mlstm_chunk(s_raw, q, k, v, *gates(j, gc, crow, 0), causal, anti, Cf, nf, mf)
            hf_scr[rows, cols] = hf
            rows, s_raw, q, k, v, gc, crow = load(j, n_blocks - 1 - c)
            hb, Cb, nb, mb = _mlstm_chunk(s_raw, q, k, v, *gates(j, gc, crow, 1), anti, causal, Cb, nb, mb)
            hb_scr[rows, cols] = hb
            out += [Cf, nf, mf, Cb, nb, mb]
        return tuple(out)

    init = tuple(x for j in range(heads) for x in init_state(j))
    final = lax.fori_loop(0, n_blocks, body, init)
    for j in range(heads):
        cols = slice(j * Dh, (j + 1) * Dh)
        finish(j, slice(None), hf_scr[:, cols] + hb_scr[:, cols])
        if emit_state:
            emit(j, final[6 * j:6 * j + 6])


def _mlstm(mq, mk, mv, mo, gcol, grow, norm_g, init_state, emit_state, heads):
    B, T, _ = mq.shape
    H, Dh, L = M_HEADS, M_HEAD_DIM, M_BLOCK
    nb = T // L
    w = heads * Dh
    has_init = init_state is not None
    tok = pl.BlockSpec((None, T, w), lambda b, h: (b, 0, h))
    in_specs = [tok, tok, tok, tok,
                pl.BlockSpec((None, heads, T, 4), lambda b, h: (b, h, 0, 0)),
                pl.BlockSpec((None, heads, 4, nb, L), lambda b, h: (b, h, 0, 0, 0)),
                pl.BlockSpec((heads, 1, Dh), lambda b, h: (h, 0, 0))]
    args = [mq, mk, mv, mo, gcol, grow, norm_g]
    state_specs = [pl.BlockSpec((None, 2, heads, Dh, Dh), lambda b, h: (b, 0, h, 0, 0)),
                   pl.BlockSpec((None, 2, heads, 1, Dh), lambda b, h: (b, 0, h, 0, 0)),
                   pl.BlockSpec((None, heads, 2, LANES), lambda b, h: (b, h, 0, 0))]
    if has_init:
        in_specs += state_specs
        args += list(init_state)
    out_specs = [tok]
    out_shape = [jax.ShapeDtypeStruct((B, T, M_WIDTH), BF16)]
    if emit_state:
        out_specs += state_specs
        out_shape += [jax.ShapeDtypeStruct((B, 2, H, Dh, Dh), F32),
                      jax.ShapeDtypeStruct((B, 2, H, 1, Dh), F32),
                      jax.ShapeDtypeStruct((B, H, 2, LANES), F32)]
    scratch = [] if nb == 1 else [pltpu.VMEM((T, w), F32), pltpu.VMEM((T, w), F32)]
    return pl.pallas_call(
        functools.partial(_mlstm_kernel, has_init, emit_state, nb, heads),
        grid=(B, H // heads),
        in_specs=in_specs,
        out_specs=out_specs,
        out_shape=out_shape,
        scratch_shapes=scratch,
        compiler_params=_params(("parallel", "parallel")),
        name="mlstm_latent" if has_init else "mlstm_context",
    )(*args)


def _attn_kernel(has_ctx, *refs):
    if has_ctx:
        q_ref, k_ref, v_ref, kc_ref, vc_ref, o_ref = refs
    else:
        q_ref, k_ref, v_ref, o_ref = refs
    scale = A_QK ** -0.5
    lane = lax.broadcasted_iota(jnp.int32, (1, LANES), 1)
    for pair in range(A_HEADS // 2):
        vsl = slice(pair * LANES, (pair + 1) * LANES)
        outs = []
        for e in range(2):
            hsl = slice((2 * pair + e) * HEAD_PAD, (2 * pair + e + 1) * HEAD_PAD)
            qh = q_ref[:, hsl]
            s = _dot_nt(qh, k_ref[:, hsl]) * scale
            mx = jnp.max(s, axis=1, keepdims=True)
            if has_ctx:
                sc = _dot_nt(qh, kc_ref[:, hsl]) * scale
                mx = jnp.maximum(mx, jnp.max(sc, axis=1, keepdims=True))
            p = jnp.exp(s - mx)
            den = jnp.sum(p, axis=1, keepdims=True)
            o = _dot(p.astype(BF16), v_ref[:, vsl])
            if has_ctx:
                pc = jnp.exp(sc - mx)
                den = den + jnp.sum(pc, axis=1, keepdims=True)
                o = o + _dot(pc.astype(BF16), vc_ref[:, vsl])
            outs.append(o / den)
        o_ref[:, vsl] = jnp.where(lane < A_VDIM, outs[0], outs[1]).astype(o_ref.dtype)


def _attn(q, k, v, ctx_kv):
    B, T, _ = q.shape
    tq = Q_TILE
    has_ctx = ctx_kv is not None
    full = lambda n, w: pl.BlockSpec((None, n, w), lambda b, i: (b, 0, 0))
    in_specs = [pl.BlockSpec((None, tq, QK_PAD), lambda b, i: (b, i, 0)), full(T, QK_PAD), full(T, A_WIDTH)]
    args = [q, k, v]
    if has_ctx:
        P = ctx_kv[0].shape[1]
        in_specs += [full(P, QK_PAD), full(P, A_WIDTH)]
        args += list(ctx_kv)
    return pl.pallas_call(
        functools.partial(_attn_kernel, has_ctx),
        grid=(B, T // tq),
        in_specs=in_specs,
        out_specs=pl.BlockSpec((None, tq, A_WIDTH), lambda b, i: (b, i, 0)),
        out_shape=jax.ShapeDtypeStruct((B, T, A_WIDTH), BF16),
        compiler_params=_params(("parallel", "parallel")),
        name="attn_latent" if has_ctx else "attn_context",
    )(*args)


def _post_kernel(x_ref, hm_ref, ha_ref, mod_ref, g2_ref, wout_ref, wup_ref, wdown_ref, y_ref):
    mod = mod_ref[...]
    gate1 = mod[:, 2 * D_MODEL:3 * D_MODEL]
    sh2 = mod[:, 3 * D_MODEL:4 * D_MODEL]
    sc2 = mod[:, 4 * D_MODEL:5 * D_MODEL]
    gate2 = mod[:, 5 * D_MODEL:6 * D_MODEL]
    mix = jnp.concatenate([hm_ref[...], ha_ref[...]], axis=-1)
    x1 = x_ref[...] + gate1 * _dot(mix, wout_ref[...])
    h2 = (_rms(x1, g2_ref[...]) * (1.0 + sc2) + sh2).astype(BF16)
    acc = jnp.zeros_like(x1)
    for c in range(D_FF // FF_TILE):
        sl = slice(c * FF_TILE, (c + 1) * FF_TILE)
        u = jnp.maximum(_dot(h2, wup_ref[:, sl]), 0.0)
        acc = acc + _dot((u * u).astype(BF16), wdown_ref[sl, :])
    y_ref[...] = x1 + gate2 * acc


def _post(x, hm, ha, mod3, mod_row0, mod_per_batch, wts):
    B, T, _ = x.shape
    tm = TOKEN_TILE
    tok = lambda w: pl.BlockSpec((None, tm, w), lambda b, i: (b, i, 0))
    return pl.pallas_call(
        _post_kernel,
        grid=(B, T // tm),
        in_specs=[tok(D_MODEL), tok(M_WIDTH), tok(A_WIDTH),
                  pl.BlockSpec((None, 1, 6 * D_MODEL), lambda b, i: (mod_row0 + b * mod_per_batch, 0, 0)),
                  _const_spec((1, D_MODEL)),
                  _const_spec((M_WIDTH + A_WIDTH, D_MODEL)),
                  _const_spec((D_MODEL, D_FF)),
                  _const_spec((D_FF, D_MODEL))],
        out_specs=tok(D_MODEL),
        out_shape=jax.ShapeDtypeStruct((B, T, D_MODEL), F32),
        compiler_params=_params(("parallel", "parallel")),
        name="post",
    )(x, hm, ha, mod3, wts["g2"], wts["w_out"], wts["w_up"], wts["w_down"])


def _prepare_weights(norm1_g, norm2_g, w_in, mlstm_gate_b, q_lora_g, kv_lora_g, w_q_up, w_kv_up,
                     q_head_g, k_head_g, w_out, w_mlp_up, w_mlp_down):
    o_g = 4 * M_WIDTH
    o_q = o_g + N_GATES
    o_kv = o_q + Q_LORA
    o_kr = o_kv + KV_LORA
    perm = jnp.arange(N_GATES).reshape(2, 2, M_HEADS).transpose(2, 0, 1).reshape(-1)
    w_gate = w_in[:, o_g:o_q][:, perm]
    tail = jnp.concatenate([
        w_gate, jnp.zeros((D_MODEL, A_NOPE - N_GATES), F32),
        w_in[:, o_kr:o_kr + A_ROPE], jnp.zeros((D_MODEL, LANES - A_QK), F32)], axis=1)
    w_lat = jnp.concatenate([w_in[:, o_q:o_kv], w_in[:, o_kv:o_kr], tail], axis=1)
    gate_bias = jnp.pad(mlstm_gate_b[perm], (0, LANES - N_GATES)).reshape(1, LANES)
    w_q = jnp.pad(w_q_up.reshape(Q_LORA, A_HEADS, A_QK), ((0, 0), (0, 0), (0, HEAD_PAD - A_QK)))
    w_kv3 = w_kv_up.reshape(KV_LORA, A_HEADS, A_NOPE + A_VDIM)
    w_k = jnp.pad(w_kv3[:, :, :A_NOPE], ((0, 0), (0, 0), (0, HEAD_PAD - A_NOPE)))
    w_v = w_kv3[:, :, A_NOPE:]
    w_kv = jnp.concatenate([w_k.reshape(KV_LORA, QK_PAD), w_v.reshape(KV_LORA, A_WIDTH)], axis=1)
    pad_head = lambda g: jnp.pad(g, (0, HEAD_PAD - A_QK)).reshape(1, HEAD_PAD)
    return {
        "g1": norm1_g.reshape(1, D_MODEL),
        "g2": norm2_g.reshape(1, D_MODEL),
        "w_main": w_in[:, :o_g].astype(BF16),
        "w_lat": w_lat.astype(BF16),
        "gate_bias": gate_bias,
        "q_lora_g": q_lora_g.reshape(1, Q_LORA),
        "kv_lora_g": kv_lora_g.reshape(1, KV_LORA),
        "w_q": w_q.reshape(Q_LORA, QK_PAD).astype(BF16),
        "w_kv": w_kv.astype(BF16),
        "q_head_g": pad_head(q_head_g),
        "k_head_g": pad_head(k_head_g),
        "w_out": w_out.astype(BF16),
        "w_up": w_mlp_up.astype(BF16),
        "w_down": w_mlp_down.astype(BF16),
    }


def _rope_tables(T):
    rows = T // GRID_W
    row = jnp.repeat(jnp.arange(rows, dtype=F32), GRID_W)
    col = jnp.tile(jnp.arange(GRID_W, dtype=F32), rows)
    half = A_ROPE // 2
    inv = ROPE_BASE ** (-jnp.arange(0, half, 2, dtype=F32) / half)
    ang = jnp.concatenate([row[:, None] * inv, col[:, None] * inv], axis=-1)
    cos, sin = jnp.cos(ang), jnp.sin(ang)
    ones = jnp.ones((T, A_NOPE), F32)
    z = lambda w: jnp.zeros((T, w), F32)
    tail = LANES - A_QK
    cos_t = jnp.concatenate([ones, cos, cos, z(tail)], axis=1)
    sin_lo = jnp.concatenate([z(A_NOPE), -sin, z(half), z(tail)], axis=1)
    sin_hi = jnp.concatenate([z(A_NOPE), z(half), sin, z(tail)], axis=1)
    return jnp.stack([cos_t, sin_lo, sin_hi], axis=0)


def _gate_layouts(gcol, grow):
    B, T, _ = gcol.shape
    gcol = gcol.reshape(B, T, M_HEADS, 4).transpose(0, 2, 1, 3)
    grow = grow.reshape(B, M_HEADS, 4, T // M_BLOCK, M_BLOCK)
    return gcol, grow


def _layer_pass(x, mod3, mod_row0, mod_per_batch, wts, norm_g, rope_tab, init_state, ctx_kv, is_context):
    pre = _pre(x, mod3, mod_row0, mod_per_batch, wts, rope_tab, emit_cache=is_context)
    mq, mk, mv, mo, gcol, grow, q, k, v = pre[:9]
    gcol, grow = _gate_layouts(gcol, grow)
    ml = _mlstm(mq, mk, mv, mo, gcol, grow, norm_g, init_state, emit_state=is_context,
                heads=M_HEADS if is_context else 1)
    ha = _attn(q, k, v, ctx_kv)
    y = _post(x, ml[0], ha, mod3, mod_row0, mod_per_batch, wts)
    return y, pre[9:], ml[1:]


def kernel(x_prompt, x_sample, cache_mla_ckv, cache_mla_krope, state_mlstm_C, state_mlstm_n, state_mlstm_m,
           c, c_ctx, norm1_g, norm2_g, w_ada, b_ada, w_in, mlstm_gate_b, mlstm_norm_g,
           q_lora_g, kv_lora_g, w_q_up, w_kv_up, q_head_g, k_head_g, w_out, w_mlp_up, w_mlp_down):
    depth = w_in.shape[0]
    Bd = x_sample.shape[0]
    cond8 = jnp.concatenate([c_ctx[None, :], c, jnp.zeros((8 - 1 - Bd, D_MODEL), F32)], axis=0)
    rope_tab = _rope_tables(x_sample.shape[1])

    y, z = x_prompt, x_sample
    ckvs, kropes, Cs, ns, ms = [], [], [], [], []
    for l in range(depth):
        wts = _prepare_weights(norm1_g[l], norm2_g[l], w_in[l], mlstm_gate_b[l], q_lora_g[l], kv_lora_g[l],
                               w_q_up[l], w_kv_up[l], q_head_g[l], k_head_g[l], w_out[l], w_mlp_up[l],
                               w_mlp_down[l])
        norm_g = mlstm_norm_g[l].reshape(M_HEADS, 1, M_HEAD_DIM)
        mod3 = _ada(cond8, w_ada[l], b_ada[l]).reshape(8, 1, 6 * D_MODEL)

        y, (ckv, krope), (C_new, n_new, m_new) = _layer_pass(
            y, mod3, 0, 0, wts, norm_g, None, None, None, True)
        ckvs.append(ckv)
        kropes.append(krope)
        Cs.append(C_new)
        ns.append(n_new[:, :, :, 0, :])
        ms.append(m_new[:, :, :, 0].transpose(0, 2, 1))

        init_state = (state_mlstm_C[:, l],
                      state_mlstm_n[:, l][:, :, :, None, :],
                      jnp.broadcast_to(state_mlstm_m[:, l].transpose(0, 2, 1)[..., None],
                                       (Bd, M_HEADS, 2, LANES)))
        krope_placed = jnp.pad(cache_mla_krope[:, l], ((0, 0), (0, 0), (A_NOPE, LANES - A_QK)))
        ctx_kv = _ctxkv(cache_mla_ckv[:, l], krope_placed, wts)
        z, _, _ = _layer_pass(z, mod3, 1, 1, wts, norm_g, rope_tab, init_state, ctx_kv, False)

    return (y, z, jnp.stack(ckvs, axis=1), jnp.stack(kropes, axis=1), jnp.stack(Cs, axis=1),
            jnp.stack(ns, axis=1), jnp.stack(ms, axis=1))
```

```python
import functools

import jax
import jax.numpy as jnp
from jax import lax
from jax.experimental import pallas as pl
from jax.experimental.pallas import tpu as pltpu

F32 = jnp.float32
BF16 = jnp.bfloat16

D_MODEL = 1024
GRID_W = 64
M_HEADS = 4
M_HEAD_DIM = 128
M_WIDTH = M_HEADS * M_HEAD_DIM
M_BLOCK = 256
A_HEADS = 8
A_NOPE = 64
A_ROPE = 32
A_QK = A_NOPE + A_ROPE
A_VDIM = 64
A_WIDTH = A_HEADS * A_VDIM
Q_LORA = 384
KV_LORA = 256
ROPE_BASE = 10000.0
D_FF = 4 * D_MODEL
EPS = 1e-6

LANES = 128
SUBLANES = 8
LOG2E = 1.4426950408889634
HEAD_PAD = LANES
QK_PAD = A_HEADS * HEAD_PAD
N_GATES = 4 * M_HEADS
LAT_WIDTH = Q_LORA + KV_LORA + 2 * LANES
VMEM_LIMIT = 56 * 1024 * 1024

TOKEN_TILE = 256
Q_TILE = 256
ADA_TILE_N = 1536
FF_TILE = 1024


def _dot(a, b):
    return jnp.dot(a, b, preferred_element_type=F32)


def _dot_nt(a, b):
    return lax.dot_general(a, b, (((1,), (1,)), ((), ())), preferred_element_type=F32)


def _dot_tn(a, b):
    return lax.dot_general(a, b, (((0,), (0,)), ((), ())), preferred_element_type=F32)


def _rms(x, g):
    y = x * lax.rsqrt(jnp.mean(x * x, axis=-1, keepdims=True) + EPS)
    return y * g


def _params(sem):
    return pltpu.CompilerParams(dimension_semantics=sem, vmem_limit_bytes=VMEM_LIMIT)


def _const_spec(shape):
    zeros = (0,) * len(shape)
    return pl.BlockSpec(shape, lambda *_: zeros)


def _ada_kernel(cond_ref, w_ref, b_ref, o_ref):
    c = cond_ref[...]
    s = (c * jax.nn.sigmoid(c)).astype(BF16)
    o_ref[...] = _dot(s, w_ref[...].astype(BF16)) + b_ref[...]


def _ada(cond8, w_ada, b_ada):
    n = w_ada.shape[1]
    return pl.pallas_call(
        _ada_kernel,
        grid=(n // ADA_TILE_N,),
        in_specs=[
            pl.BlockSpec((8, D_MODEL), lambda j: (0, 0)),
            pl.BlockSpec((D_MODEL, ADA_TILE_N), lambda j: (0, j)),
            pl.BlockSpec((1, ADA_TILE_N), lambda j: (0, j)),
        ],
        out_specs=pl.BlockSpec((8, ADA_TILE_N), lambda j: (0, j)),
        out_shape=jax.ShapeDtypeStruct((8, n), F32),
        compiler_params=_params(("parallel",)),
        name="ada",
    )(cond8, w_ada, b_ada.reshape(1, n))


def _write_heads(src, extra, g_pad, rot, dst_ref):
    for h in range(A_HEADS):
        sl = slice(h * HEAD_PAD, (h + 1) * HEAD_PAD)
        xh = src[:, sl]
        if extra is not None:
            xh = xh + extra
        ss = jnp.sum(xh * xh, axis=-1, keepdims=True) * (1.0 / A_QK)
        r = lax.rsqrt(ss + EPS)
        if rot is None:
            y = xh * r * g_pad
        else:
            partner, cos_g, sin_g = rot
            ph = partner if partner.shape[1] == HEAD_PAD else partner[:, sl]
            y = (xh * cos_g + ph * sin_g) * r
        dst_ref[:, sl] = y.astype(dst_ref.dtype)


def _time_scan(x, op, identity, reverse):
    n = x.shape[0]
    row = lax.broadcasted_iota(jnp.int32, x.shape, 0)
    shift = 1
    while shift < n:
        if shift < SUBLANES:
            if reverse:
                moved = jnp.where(row < n - shift, pltpu.roll(x, n - shift, 0), identity)
            else:
                moved = jnp.where(row >= shift, pltpu.roll(x, shift, 0), identity)
        else:
            fill = jnp.full((shift, x.shape[1]), identity, x.dtype)
            moved = (jnp.concatenate([x[shift:], fill], axis=0) if reverse
                     else jnp.concatenate([fill, x[:n - shift]], axis=0))
        x = op(x, moved)
        shift *= 2
    return x


def _pre_kernel(has_rope, emit_cache, *refs):
    (x_ref, mod_ref, g1_ref, wmain_ref, wkt_ref, wlat_ref, gbias_ref, qlg_ref, kvg_ref, wq_ref, wkv_ref,
     qhg_ref, khg_ref) = refs[:13]
    pos = 13
    if has_rope:
        rope_ref = refs[pos]
        pos += 1
    (mq_ref, mkt_ref, mv_ref, mo_ref, stats_ref, q_ref, k_ref, v_ref) = refs[pos:pos + 8]
    pos += 8

    x = x_ref[...]
    mod = mod_ref[...]
    sh1 = mod[:, 0:D_MODEL]
    sc1 = mod[:, D_MODEL:2 * D_MODEL]
    h = _rms(x, g1_ref[...]) * (1.0 + sc1) + sh1
    hb = h.astype(BF16)

    pm = _dot(hb, wmain_ref[...])
    mq_ref[...] = pm[:, 0:M_WIDTH].astype(BF16)
    mv_ref[...] = pm[:, M_WIDTH:2 * M_WIDTH].astype(BF16)
    mo_ref[...] = pm[:, 2 * M_WIDTH:3 * M_WIDTH]
    mkt_ref[...] = _dot_nt(wkt_ref[...], hb) * (M_HEAD_DIM ** -0.5)

    plat = _dot(hb, wlat_ref[...])
    q_lat = plat[:, 0:Q_LORA]
    kv_lat = plat[:, Q_LORA:Q_LORA + KV_LORA]
    tail = plat[:, Q_LORA + KV_LORA:Q_LORA + KV_LORA + LANES]
    tail2 = plat[:, LAT_WIDTH - LANES:LAT_WIDTH]

    lane = lax.broadcasted_iota(jnp.int32, (1, LANES), 1)
    fwd = lane < M_HEADS
    gate_i = tail + gbias_ref[0:1, :]
    gate_f = tail2 + gbias_ref[1:2, :]
    log_f = jnp.minimum(gate_f, 0.0) - jnp.log1p(jnp.exp(-jnp.abs(gate_f)))
    b = jnp.where(fwd, _time_scan(log_f, jnp.add, 0.0, False), _time_scan(log_f, jnp.add, 0.0, True))
    a = gate_i - b
    amax = jnp.where(fwd, _time_scan(a, jnp.maximum, -jnp.inf, False),
                     _time_scan(a, jnp.maximum, -jnp.inf, True))
    stats_ref[0] = b.T[0:2 * M_HEADS, :]
    stats_ref[1] = a.T[0:2 * M_HEADS, :]
    stats_ref[2] = amax.T[0:2 * M_HEADS, :]

    krope_placed = jnp.where((lane >= A_NOPE) & (lane < A_QK), tail, 0.0)
    ckv = _rms(kv_lat, kvg_ref[...])
    qn = _rms(q_lat, qlg_ref[...])
    qf = _dot(qn.astype(BF16), wq_ref[...])
    kvf = _dot(ckv.astype(BF16), wkv_ref[...])
    v_ref[...] = kvf[:, QK_PAD:QK_PAD + A_WIDTH].astype(BF16)
    q_rot = k_rot = None
    if has_rope:
        cos_t, sin_t = rope_ref[0], rope_ref[1]
        q_rot = (qf[:, QK_PAD:2 * QK_PAD], cos_t * qhg_ref[0:1, :], sin_t * qhg_ref[1:2, :])
        k_rot = (tail2, cos_t * khg_ref[0:1, :], sin_t * khg_ref[1:2, :])
    _write_heads(qf, None, qhg_ref[0:1, :], q_rot, q_ref)
    _write_heads(kvf, krope_placed, khg_ref[0:1, :], k_rot, k_ref)

    if emit_cache:
        ckv_ref, krope_ref = refs[pos:pos + 2]
        ckv_ref[...] = ckv
        krope_ref[...] = tail[:, A_NOPE:A_QK]


def _pre(x, mod3, mod_row0, mod_per_batch, wts, rope_tab, emit_cache):
    B, T, _ = x.shape
    tm = TOKEN_TILE
    has_rope = rope_tab is not None
    tok = lambda w: pl.BlockSpec((None, tm, w), lambda b, i: (b, i, 0))
    in_specs = [
        tok(D_MODEL),
        pl.BlockSpec((None, 1, 6 * D_MODEL), lambda b, i: (mod_row0 + b * mod_per_batch, 0, 0)),
        _const_spec((1, D_MODEL)),
        _const_spec((D_MODEL, 3 * M_WIDTH)),
        _const_spec((M_WIDTH, D_MODEL)),
        _const_spec((D_MODEL, LAT_WIDTH)),
        _const_spec((2, LANES)),
        _const_spec((1, Q_LORA)),
        _const_spec((1, KV_LORA)),
        _const_spec((Q_LORA, 2 * QK_PAD if has_rope else QK_PAD)),
        _const_spec((KV_LORA, QK_PAD + A_WIDTH)),
        _const_spec((2, HEAD_PAD)),
        _const_spec((2, HEAD_PAD)),
    ]
    assert tm == M_BLOCK
    args = [x, mod3, wts["g1"], wts["w_main"], wts["w_kt"], wts["w_lat"], wts["gate_bias"], wts["q_lora_g"],
            wts["kv_lora_g"], wts["w_q_rot"] if has_rope else wts["w_q"], wts["w_kv"],
            wts["q_head_g"], wts["k_head_g"]]
    if has_rope:
        in_specs.append(pl.BlockSpec((2, tm, HEAD_PAD), lambda b, i: (0, i, 0)))
        args.append(rope_tab)
    out_specs = [tok(M_WIDTH),
                 pl.BlockSpec((None, M_WIDTH, tm), lambda b, i: (b, 0, i)),
                 tok(M_WIDTH), tok(M_WIDTH),
                 pl.BlockSpec((None, 3, 2 * M_HEADS, tm), lambda b, i: (b, 0, 0, i)),
                 tok(QK_PAD), tok(QK_PAD), tok(A_WIDTH)]
    out_shape = [
        jax.ShapeDtypeStruct((B, T, M_WIDTH), BF16),
        jax.ShapeDtypeStruct((B, M_WIDTH, T), F32),
        jax.ShapeDtypeStruct((B, T, M_WIDTH), BF16),
        jax.ShapeDtypeStruct((B, T, M_WIDTH), F32),
        jax.ShapeDtypeStruct((B, 3, 2 * M_HEADS, T), F32),
        jax.ShapeDtypeStruct((B, T, QK_PAD), BF16),
        jax.ShapeDtypeStruct((B, T, QK_PAD), BF16),
        jax.ShapeDtypeStruct((B, T, A_WIDTH), BF16),
    ]
    if emit_cache:
        out_specs += [tok(KV_LORA), tok(A_ROPE)]
        out_shape += [jax.ShapeDtypeStruct((B, T, KV_LORA), F32),
                      jax.ShapeDtypeStruct((B, T, A_ROPE), F32)]
    return pl.pallas_call(
        functools.partial(_pre_kernel, has_rope, emit_cache),
        grid=(B, T // tm),
        in_specs=in_specs,
        out_specs=out_specs,
        out_shape=out_shape,
        compiler_params=_params(("parallel", "parallel")),
        name="pre_latent" if has_rope else "pre_context",
    )(*args)


def _ctxkv_kernel(ckv_ref, krp_ref, wkv_ref, khg_ref, k_ref, v_ref):
    kvf = _dot(ckv_ref[...].astype(BF16), wkv_ref[...])
    v_ref[...] = kvf[:, QK_PAD:QK_PAD + A_WIDTH].astype(BF16)
    _write_heads(kvf, krp_ref[...], khg_ref[0:1, :], None, k_ref)


def _ctxkv(ckv, krope_placed, wts):
    B, P, _ = ckv.shape
    tok = lambda w: pl.BlockSpec((None, P, w), lambda b: (b, 0, 0))
    return pl.pallas_call(
        _ctxkv_kernel,
        grid=(B,),
        in_specs=[tok(KV_LORA), tok(HEAD_PAD), _const_spec((KV_LORA, QK_PAD + A_WIDTH)),
                  _const_spec((2, HEAD_PAD))],
        out_specs=[tok(QK_PAD), tok(A_WIDTH)],
        out_shape=[jax.ShapeDtypeStruct((B, P, QK_PAD), BF16),
                   jax.ShapeDtypeStruct((B, P, A_WIDTH), BF16)],
        compiler_params=_params(("parallel",)),
        name="ctx_kv",
    )(ckv, krope_placed, wts["w_kv"], wts["k_head_g"])


def _rows_to_lane_broadcast(rows, spread):
    x = jnp.concatenate(rows, axis=0)
    p1 = x.astype(BF16)
    r1 = x - p1.astype(F32)
    p2 = r1.astype(BF16)
    p3 = (r1 - p2.astype(F32)).astype(BF16)
    pad = jnp.zeros((spread.shape[0] - 3 * len(rows), x.shape[1]), BF16)
    return _dot_tn(jnp.concatenate([p1, p2, p3, pad], axis=0), spread)


def _mlstm_gate_rows(b_row, a_row, amax_row, forward, m):
    L = b_row.shape[1]
    last = slice(L - 1, L) if forward else slice(0, 1)
    total = b_row[:, last]
    g_row = jnp.maximum(m, amax_row)
    m_new = total + jnp.maximum(m, amax_row[:, last])
    w_key_row = jnp.exp2((a_row + (total - m_new)) * LOG2E)
    decay = jnp.exp(total + m - m_new)
    return g_row * LOG2E, (b_row + g_row) * LOG2E, a_row * LOG2E, w_key_row, decay, m_new


def _mlstm_block(s_raw, q, kt, v_aug, g2, mt2, a2_row, w_key_row, decay, allow, CN, m):
    w_intra = jnp.exp2(jnp.where(allow, a2_row - jnp.concatenate([g2, g2], axis=1), -jnp.inf))
    w_inter = jnp.exp2(m * LOG2E - g2)
    s = (s_raw * w_intra).astype(BF16)
    nd = _dot(s, v_aug) + jnp.concatenate([w_inter, w_inter], axis=1) * _dot(q, CN.astype(BF16))
    num, den = nd[:, 0:M_HEAD_DIM], nd[:, M_HEAD_DIM:2 * M_HEAD_DIM]
    h = num / jnp.maximum(jnp.abs(den), jnp.exp2(-mt2))
    CN_new = decay * CN + _dot((kt * w_key_row).astype(BF16), v_aug)
    return h, CN_new


def _mlstm_kernel(has_init, emit_state, n_blocks, heads, *refs):
    q_ref, kt_ref, v_ref, mo_ref, stats_ref, ng_ref, spread_ref = refs[:7]
    pos = 7
    if has_init:
        c0_ref, n0_ref, m0_ref = refs[pos:pos + 3]
        pos += 3
    hm_ref = refs[pos]
    pos += 1
    if emit_state:
        c_ref, n_ref, m_ref = refs[pos:pos + 3]
        pos += 3

    L, Dh = M_BLOCK, M_HEAD_DIM
    t_idx = lax.broadcasted_iota(jnp.int32, (L, L), 0)
    s_idx = lax.broadcasted_iota(jnp.int32, (L, L), 1)
    allow = (s_idx <= t_idx, s_idx >= t_idx)
    spread = spread_ref[...]
    ones = jnp.ones((L, Dh), BF16)

    def lane_broadcast_n(n_row):
        return jnp.broadcast_to(n_row, (Dh, Dh)).T

    def init_state(j, d):
        if has_init:
            return (jnp.concatenate([c0_ref[d, j], lane_broadcast_n(n0_ref[d, j])], axis=1),
                    m0_ref[j, d:d + 1, 0:1])
        return jnp.zeros((Dh, 2 * Dh), F32), jnp.zeros((1, 1), F32)

    def gate_rows(j, c, d, m):
        r = slice(c, c + 1)
        return _mlstm_gate_rows(stats_ref[0, d, j, r, :], stats_ref[1, d, j, r, :], stats_ref[2, d, j, r, :],
                                d == 0, m)

    def blocks(j, jobs, states):
        loaded, rows6, cols_in = {}, [], []
        for (c, d), (CN, m) in zip(jobs, states):
            rows6.append(gate_rows(j, c, d, m))
            cols_in += [rows6[-1][0], rows6[-1][1]]
            if c not in loaded:
                rows, cols = slice(c * L, (c + 1) * L), slice(j * Dh, (j + 1) * Dh)
                q, kt = q_ref[rows, cols], kt_ref[cols, rows]
                v_aug = jnp.concatenate([v_ref[rows, cols], ones], axis=1)
                loaded[c] = (_dot(q, kt.astype(BF16)), q, kt, v_aug)
        cols_out = _rows_to_lane_broadcast(cols_in, spread)
        hs, new_states = [], []
        for idx, ((c, d), (CN, m)) in enumerate(zip(jobs, states)):
            g2 = cols_out[:, (2 * idx) * LANES:(2 * idx + 1) * LANES]
            mt2 = cols_out[:, (2 * idx + 1) * LANES:(2 * idx + 2) * LANES]
            _, _, a2_row, w_key_row, decay, m_new = rows6[idx]
            h, CN_new = _mlstm_block(*loaded[c], g2, mt2, a2_row, w_key_row, decay, allow[d], CN, m)
            hs.append(h)
            new_states.append((CN_new, m_new))
        return hs, new_states

    def finish(j, rows, hs):
        cols = slice(j * Dh, (j + 1) * Dh)
        hn = _rms(hs, ng_ref[j])
        hm_ref[rows, cols] = (hn * jax.nn.sigmoid(mo_ref[rows, cols])).astype(hm_ref.dtype)

    def emit(j, d, state):
        CN, m = state
        c_ref[d, j] = CN[:, 0:Dh]
        n_ref[d, j] = CN[:, Dh:2 * Dh].T[0:1, :]
        m_ref[j, d:d + 1, :] = jnp.broadcast_to(m, (1, LANES))

    if n_blocks > 1:
        hf_scr, hb_scr = refs[pos:pos + 2]

    for j in range(heads):
        cols = slice(j * Dh, (j + 1) * Dh)
        states = [init_state(j, 0), init_state(j, 1)]
        if n_blocks == 1:
            (hf, hb), states = blocks(j, [(0, 0), (0, 1)], states)
            finish(j, slice(0, L), hf + hb)
        else:
            for step in range(n_blocks):
                cf, cb = step, n_blocks - 1 - step
                (hf, hb), states = blocks(j, [(cf, 0), (cb, 1)], states)
                hf_scr[cf * L:(cf + 1) * L, cols] = hf
                hb_scr[cb * L:(cb + 1) * L, cols] = hb
            finish(j, slice(None), hf_scr[:, cols] + hb_scr[:, cols])
        if emit_state:
            emit(j, 0, states[0])
            emit(j, 1, states[1])


def _mlstm(mq, mkt, mv, mo, stats, norm_g, init_state, emit_state, heads):
    B, T, _ = mq.shape
    H, Dh, L = M_HEADS, M_HEAD_DIM, M_BLOCK
    nb = T // L
    w = heads * Dh
    has_init = init_state is not None
    n_rows = 4
    spread = jnp.tile(jnp.repeat(jnp.eye(n_rows, dtype=BF16), LANES, axis=1), (3, 1))
    spread = jnp.pad(spread, ((0, 2 * SUBLANES - 3 * n_rows), (0, 0)))
    stats = stats.reshape(B, 3, 2, H, nb, L)
    tok = pl.BlockSpec((None, T, w), lambda b, h: (b, 0, h))
    in_specs = [tok, pl.BlockSpec((None, w, T), lambda b, h: (b, h, 0)), tok, tok,
                pl.BlockSpec((None, 3, 2, heads, nb, L), lambda b, h: (b, 0, 0, h, 0, 0)),
                pl.BlockSpec((heads, 1, Dh), lambda b, h: (h, 0, 0)),
                _const_spec((2 * SUBLANES, n_rows * LANES))]
    args = [mq, mkt, mv, mo, stats, norm_g, spread]
    state_specs = [pl.BlockSpec((None, 2, heads, Dh, Dh), lambda b, h: (b, 0, h, 0, 0)),
                   pl.BlockSpec((None, 2, heads, 1, Dh), lambda b, h: (b, 0, h, 0, 0)),
                   pl.BlockSpec((None, heads, 2, LANES), lambda b, h: (b, h, 0, 0))]
    if has_init:
        in_specs += state_specs
        args += list(init_state)
    out_specs = [tok]
    out_shape = [jax.ShapeDtypeStruct((B, T, M_WIDTH), BF16)]
    if emit_state:
        out_specs += state_specs
        out_shape += [jax.ShapeDtypeStruct((B, 2, H, Dh, Dh), F32),
                      jax.ShapeDtypeStruct((B, 2, H, 1, Dh), F32),
                      jax.ShapeDtypeStruct((B, H, 2, LANES), F32)]
    scratch = [] if nb == 1 else [pltpu.VMEM((T, w), F32), pltpu.VMEM((T, w), F32)]
    return pl.pallas_call(
        functools.partial(_mlstm_kernel, has_init, emit_state, nb, heads),
        grid=(B, H // heads),
        in_specs=in_specs,
        out_specs=out_specs,
        out_shape=out_shape,
        scratch_shapes=scratch,
        compiler_params=_params(("parallel", "parallel")),
        name="mlstm_latent" if has_init else "mlstm_context",
    )(*args)


def _attn_kernel(has_ctx, *refs):
    if has_ctx:
        q_ref, k_ref, v_ref, kc_ref, vc_ref, o_ref = refs
    else:
        q_ref, k_ref, v_ref, o_ref = refs
    scale = A_QK ** -0.5
    lane = lax.broadcasted_iota(jnp.int32, (1, LANES), 1)
    for pair in range(A_HEADS // 2):
        vsl = slice(pair * LANES, (pair + 1) * LANES)
        outs = []
        for e in range(2):
            hsl = slice((2 * pair + e) * HEAD_PAD, (2 * pair + e + 1) * HEAD_PAD)
            qh = q_ref[:, hsl]
            s = _dot_nt(qh, k_ref[:, hsl]) * scale
            mx = jnp.max(s, axis=1, keepdims=True)
            if has_ctx:
                sc = _dot_nt(qh, kc_ref[:, hsl]) * scale
                mx = jnp.maximum(mx, jnp.max(sc, axis=1, keepdims=True))
            p = jnp.exp(s - mx)
            den = jnp.sum(p, axis=1, keepdims=True)
            o = _dot(p.astype(BF16), v_ref[:, vsl])
            if has_ctx:
                pc = jnp.exp(sc - mx)
                den = den + jnp.sum(pc, axis=1, keepdims=True)
                o = o + _dot(pc.astype(BF16), vc_ref[:, vsl])
            outs.append(o / den)
        o_ref[:, vsl] = jnp.where(lane < A_VDIM, outs[0], outs[1]).astype(o_ref.dtype)


def _attn(q, k, v, ctx_kv):
    B, T, _ = q.shape
    tq = Q_TILE
    has_ctx = ctx_kv is not None
    full = lambda n, w: pl.BlockSpec((None, n, w), lambda b, i: (b, 0, 0))
    in_specs = [pl.BlockSpec((None, tq, QK_PAD), lambda b, i: (b, i, 0)), full(T, QK_PAD), full(T, A_WIDTH)]
    args = [q, k, v]
    if has_ctx:
        P = ctx_kv[0].shape[1]
        in_specs += [full(P, QK_PAD), full(P, A_WIDTH)]
        args += list(ctx_kv)
    return pl.pallas_call(
        functools.partial(_attn_kernel, has_ctx),
        grid=(B, T // tq),
        in_specs=in_specs,
        out_specs=pl.BlockSpec((None, tq, A_WIDTH), lambda b, i: (b, i, 0)),
        out_shape=jax.ShapeDtypeStruct((B, T, A_WIDTH), BF16),
        compiler_params=_params(("parallel", "parallel")),
        name="attn_latent" if has_ctx else "attn_context",
    )(*args)


def _post_kernel(x_ref, hm_ref, ha_ref, mod_ref, g2_ref, wout_ref, wup_ref, wdown_ref, y_ref):
    mod = mod_ref[...]
    gate1 = mod[:, 2 * D_MODEL:3 * D_MODEL]
    sh2 = mod[:, 3 * D_MODEL:4 * D_MODEL]
    sc2 = mod[:, 4 * D_MODEL:5 * D_MODEL]
    gate2 = mod[:, 5 * D_MODEL:6 * D_MODEL]
    mix = jnp.concatenate([hm_ref[...], ha_ref[...]], axis=-1)
    x1 = x_ref[...] + gate1 * _dot(mix, wout_ref[...])
    h2 = (_rms(x1, g2_ref[...]) * (1.0 + sc2) + sh2).astype(BF16)
    acc = jnp.zeros_like(x1)
    for c in range(D_FF // FF_TILE):
        sl = slice(c * FF_TILE, (c + 1) * FF_TILE)
        u = jnp.maximum(_dot(h2, wup_ref[:, sl]), 0.0)
        acc = acc + _dot((u * u).astype(BF16), wdown_ref[sl, :])
    y_ref[...] = x1 + gate2 * acc


def _post(x, hm, ha, mod3, mod_row0, mod_per_batch, wts):
    B, T, _ = x.shape
    tm = TOKEN_TILE
    tok = lambda w: pl.BlockSpec((None, tm, w), lambda b, i: (b, i, 0))
    return pl.pallas_call(
        _post_kernel,
        grid=(B, T // tm),
        in_specs=[tok(D_MODEL), tok(M_WIDTH), tok(A_WIDTH),
                  pl.BlockSpec((None, 1, 6 * D_MODEL), lambda b, i: (mod_row0 + b * mod_per_batch, 0, 0)),
                  _const_spec((1, D_MODEL)),
                  _const_spec((M_WIDTH + A_WIDTH, D_MODEL)),
                  _const_spec((D_MODEL, D_FF)),
                  _const_spec((D_FF, D_MODEL))],
        out_specs=tok(D_MODEL),
        out_shape=jax.ShapeDtypeStruct((B, T, D_MODEL), F32),
        compiler_params=_params(("parallel", "parallel")),
        name="post",
    )(x, hm, ha, mod3, wts["g2"], wts["w_out"], wts["w_up"], wts["w_down"])


def _prepare_weights(norm1_g, norm2_g, w_in, mlstm_gate_b, q_lora_g, kv_lora_g, w_q_up, w_kv_up,
                     q_head_g, k_head_g, w_out, w_mlp_up, w_mlp_down):
    o_g = 4 * M_WIDTH
    o_q = o_g + N_GATES
    o_kv = o_q + Q_LORA
    o_kr = o_kv + KV_LORA
    half = A_ROPE // 2
    n_dh = 2 * M_HEADS
    w_gate = w_in[:, o_g:o_q].reshape(D_MODEL, 2, 2, M_HEADS)
    bias = mlstm_gate_b.reshape(2, 2, M_HEADS)

    def gate_tile(which, rope_cols):
        return jnp.concatenate([
            w_gate[:, :, which, :].reshape(D_MODEL, n_dh), jnp.zeros((D_MODEL, A_NOPE - n_dh), F32),
            rope_cols, jnp.zeros((D_MODEL, LANES - A_QK), F32)], axis=1)

    def rot_partner(a):
        z = jnp.zeros(a.shape[:-1] + (A_NOPE,), a.dtype)
        return jnp.concatenate([z, a[..., A_NOPE + half:A_QK], a[..., A_NOPE:A_NOPE + half]], axis=-1)

    pad_tile = lambda a: jnp.pad(a, [(0, 0)] * (a.ndim - 1) + [(0, HEAD_PAD - A_QK)])
    w_kr = w_in[:, o_kr:o_kr + A_ROPE]
    w_kr_partner = jnp.concatenate([w_kr[:, half:], w_kr[:, :half]], axis=1)
    w_lat = jnp.concatenate([w_in[:, o_q:o_kv], w_in[:, o_kv:o_kr],
                             gate_tile(0, w_kr), gate_tile(1, w_kr_partner)], axis=1)
    gate_bias = jnp.pad(jnp.stack([bias[:, 0, :].reshape(n_dh), bias[:, 1, :].reshape(n_dh)], axis=0),
                        ((0, 0), (0, LANES - n_dh)))
    w_q3 = w_q_up.reshape(Q_LORA, A_HEADS, A_QK)
    w_q = pad_tile(w_q3).reshape(Q_LORA, QK_PAD)
    w_q_partner = pad_tile(rot_partner(w_q3)).reshape(Q_LORA, QK_PAD)
    w_kv3 = w_kv_up.reshape(KV_LORA, A_HEADS, A_NOPE + A_VDIM)
    w_k = jnp.pad(w_kv3[:, :, :A_NOPE], ((0, 0), (0, 0), (0, HEAD_PAD - A_NOPE)))
    w_v = w_kv3[:, :, A_NOPE:]
    w_kv = jnp.concatenate([w_k.reshape(KV_LORA, QK_PAD), w_v.reshape(KV_LORA, A_WIDTH)], axis=1)
    pad_head = lambda g: jnp.stack([pad_tile(g), pad_tile(rot_partner(g))], axis=0)
    return {
        "g1": norm1_g.reshape(1, D_MODEL),
        "g2": norm2_g.reshape(1, D_MODEL),
        "w_main": jnp.concatenate([w_in[:, 0:M_WIDTH], w_in[:, 2 * M_WIDTH:o_g]], axis=1).astype(BF16),
        "w_kt": w_in[:, M_WIDTH:2 * M_WIDTH].T.astype(BF16),
        "w_lat": w_lat.astype(BF16),
        "gate_bias": gate_bias,
        "q_lora_g": q_lora_g.reshape(1, Q_LORA),
        "kv_lora_g": kv_lora_g.reshape(1, KV_LORA),
        "w_q": w_q.astype(BF16),
        "w_q_rot": jnp.concatenate([w_q, w_q_partner], axis=1).astype(BF16),
        "w_kv": w_kv.astype(BF16),
        "q_head_g": pad_head(q_head_g),
        "k_head_g": pad_head(k_head_g),
        "w_out": w_out.astype(BF16),
        "w_up": w_mlp_up.astype(BF16),
        "w_down": w_mlp_down.astype(BF16),
    }


def _rope_tables(T):
    rows = T // GRID_W
    row = jnp.repeat(jnp.arange(rows, dtype=F32), GRID_W)
    col = jnp.tile(jnp.arange(GRID_W, dtype=F32), rows)
    half = A_ROPE // 2
    inv = ROPE_BASE ** (-jnp.arange(0, half, 2, dtype=F32) / half)
    ang = jnp.concatenate([row[:, None] * inv, col[:, None] * inv], axis=-1)
    cos, sin = jnp.cos(ang), jnp.sin(ang)
    ones = jnp.ones((T, A_NOPE), F32)
    z = lambda w: jnp.zeros((T, w), F32)
    tail = LANES - A_QK
    cos_t = jnp.concatenate([ones, cos, cos, z(tail)], axis=1)
    sin_t = jnp.concatenate([z(A_NOPE), -sin, sin, z(tail)], axis=1)
    return jnp.stack([cos_t, sin_t], axis=0)


def _layer_pass(x, mod3, mod_row0, mod_per_batch, wts, norm_g, rope_tab, init_state, ctx_kv, is_context):
    pre = _pre(x, mod3, mod_row0, mod_per_batch, wts, rope_tab, emit_cache=is_context)
    mq, mkt, mv, mo, stats, q, k, v = pre[:8]
    ml = _mlstm(mq, mkt, mv, mo, stats, norm_g, init_state, emit_state=is_context,
                heads=M_HEADS if is_context else 1)
    ha = _attn(q, k, v, ctx_kv)
    y = _post(x, ml[0], ha, mod3, mod_row0, mod_per_batch, wts)
    return y, pre[8:], ml[1:]


def kernel(x_prompt, x_sample, cache_mla_ckv, cache_mla_krope, state_mlstm_C, state_mlstm_n, state_mlstm_m,
           c, c_ctx, norm1_g, norm2_g, w_ada, b_ada, w_in, mlstm_gate_b, mlstm_norm_g,
           q_lora_g, kv_lora_g, w_q_up, w_kv_up, q_head_g, k_head_g, w_out, w_mlp_up, w_mlp_down):
    depth = w_in.shape[0]
    Bd = x_sample.shape[0]
    cond8 = jnp.concatenate([c_ctx[None, :], c, jnp.zeros((8 - 1 - Bd, D_MODEL), F32)], axis=0)
    rope_tab = _rope_tables(x_sample.shape[1])

    y, z = x_prompt, x_sample
    ckvs, kropes, Cs, ns, ms = [], [], [], [], []
    for l in range(depth):
        wts = _prepare_weights(norm1_g[l], norm2_g[l], w_in[l], mlstm_gate_b[l], q_lora_g[l], kv_lora_g[l],
                               w_q_up[l], w_kv_up[l], q_head_g[l], k_head_g[l], w_out[l], w_mlp_up[l],
                               w_mlp_down[l])
        norm_g = mlstm_norm_g[l].reshape(M_HEADS, 1, M_HEAD_DIM)
        mod3 = _ada(cond8, w_ada[l], b_ada[l]).reshape(8, 1, 6 * D_MODEL)

        y, (ckv, krope), (C_new, n_new, m_new) = _layer_pass(
            y, mod3, 0, 0, wts, norm_g, None, None, None, True)
        ckvs.append(ckv)
        kropes.append(krope)
        Cs.append(C_new)
        ns.append(n_new[:, :, :, 0, :])
        ms.append(m_new[:, :, :, 0].transpose(0, 2, 1))

        init_state = (state_mlstm_C[:, l],
                      state_mlstm_n[:, l][:, :, :, None, :],
                      jnp.broadcast_to(state_mlstm_m[:, l].transpose(0, 2, 1)[..., None],
                                       (Bd, M_HEADS, 2, LANES)))
        krope_placed = jnp.pad(cache_mla_krope[:, l], ((0, 0), (0, 0), (A_NOPE, LANES - A_QK)))
        ctx_kv = _ctxkv(cache_mla_ckv[:, l], krope_placed, wts)
        z, _, _ = _layer_pass(z, mod3, 1, 1, wts, norm_g, rope_tab, init_state, ctx_kv, False)

    return (y, z, jnp.stack(ckvs, axis=1), jnp.stack(kropes, axis=1), jnp.stack(Cs, axis=1),
            jnp.stack(ns, axis=1), jnp.stack(ms, axis=1))
```

```python
import functools

import jax
import jax.numpy as jnp
from jax import lax
from jax.experimental import pallas as pl
from jax.experimental.pallas import tpu as pltpu

F32 = jnp.float32
BF16 = jnp.bfloat16

D_MODEL = 1024
GRID_W = 64
M_HEADS = 4
M_HEAD_DIM = 128
M_WIDTH = M_HEADS * M_HEAD_DIM
M_BLOCK = 256
A_HEADS = 8
A_NOPE = 64
A_ROPE = 32
A_QK = A_NOPE + A_ROPE
A_VDIM = 64
A_WIDTH = A_HEADS * A_VDIM
Q_LORA = 384
KV_LORA = 256
ROPE_BASE = 10000.0
D_FF = 4 * D_MODEL
EPS = 1e-6

LANES = 128
SUBLANES = 8
LOG2E = 1.4426950408889634
HEAD_PAD = LANES
QK_PAD = A_HEADS * HEAD_PAD
N_GATES = 4 * M_HEADS
LAT_WIDTH = Q_LORA + KV_LORA + 2 * LANES
VMEM_LIMIT = 56 * 1024 * 1024

TOKEN_TILE = 256
Q_TILE = 256
ADA_TILE_N = 1536
FF_TILE = 1024


def _dot(a, b):
    return jnp.dot(a, b, preferred_element_type=F32)


def _dot_nt(a, b):
    return lax.dot_general(a, b, (((1,), (1,)), ((), ())), preferred_element_type=F32)


def _dot_tn(a, b):
    return lax.dot_general(a, b, (((0,), (0,)), ((), ())), preferred_element_type=F32)


def _rms(x, g):
    y = x * lax.rsqrt(jnp.mean(x * x, axis=-1, keepdims=True) + EPS)
    return y * g


def _params(sem):
    return pltpu.CompilerParams(dimension_semantics=sem, vmem_limit_bytes=VMEM_LIMIT)


def _const_spec(shape):
    zeros = (0,) * len(shape)
    return pl.BlockSpec(shape, lambda *_: zeros)


def _ada_kernel(cond_ref, w_ref, b_ref, o_ref):
    c = cond_ref[...]
    s = (c * jax.nn.sigmoid(c)).astype(BF16)
    o_ref[...] = _dot(s, w_ref[...].astype(BF16)) + b_ref[...]


def _ada(cond8, w_ada, b_ada):
    n = w_ada.shape[1]
    return pl.pallas_call(
        _ada_kernel,
        grid=(n // ADA_TILE_N,),
        in_specs=[
            pl.BlockSpec((8, D_MODEL), lambda j: (0, 0)),
            pl.BlockSpec((D_MODEL, ADA_TILE_N), lambda j: (0, j)),
            pl.BlockSpec((1, ADA_TILE_N), lambda j: (0, j)),
        ],
        out_specs=pl.BlockSpec((8, ADA_TILE_N), lambda j: (0, j)),
        out_shape=jax.ShapeDtypeStruct((8, n), F32),
        compiler_params=_params(("parallel",)),
        name="ada",
    )(cond8, w_ada, b_ada.reshape(1, n))


def _write_heads(src, extra, g_pad, rot, dst_ref):
    for h in range(A_HEADS):
        sl = slice(h * HEAD_PAD, (h + 1) * HEAD_PAD)
        xh = src[:, sl]
        if extra is not None:
            xh = xh + extra
        ss = jnp.sum(xh * xh, axis=-1, keepdims=True) * (1.0 / A_QK)
        r = lax.rsqrt(ss + EPS)
        if rot is None:
            y = xh * r * g_pad
        else:
            partner, cos_g, sin_g = rot
            ph = partner if partner.shape[1] == HEAD_PAD else partner[:, sl]
            y = (xh * cos_g + ph * sin_g) * r
        dst_ref[:, sl] = y.astype(dst_ref.dtype)


def _time_scan(x, op, identity, reverse):
    n = x.shape[0]
    row = lax.broadcasted_iota(jnp.int32, x.shape, 0)
    shift = 1
    while shift < n:
        if shift < SUBLANES:
            if reverse:
                moved = jnp.where(row < n - shift, pltpu.roll(x, n - shift, 0), identity)
            else:
                moved = jnp.where(row >= shift, pltpu.roll(x, shift, 0), identity)
        else:
            fill = jnp.full((shift, x.shape[1]), identity, x.dtype)
            moved = (jnp.concatenate([x[shift:], fill], axis=0) if reverse
                     else jnp.concatenate([fill, x[:n - shift]], axis=0))
        x = op(x, moved)
        shift *= 2
    return x


def _pre_kernel(has_rope, emit_cache, *refs):
    (x_ref, mod_ref, g1_ref, wmain_ref, wkt_ref, wlat_ref, gbias_ref, qlg_ref, kvg_ref, wq_ref, wkv_ref,
     qhg_ref, khg_ref) = refs[:13]
    pos = 13
    if has_rope:
        rope_ref = refs[pos]
        pos += 1
    (mq_ref, mkt_ref, mv_ref, mo_ref, stats_ref, q_ref, k_ref, v_ref) = refs[pos:pos + 8]
    pos += 8

    x = x_ref[...]
    mod = mod_ref[...]
    sh1 = mod[:, 0:D_MODEL]
    sc1 = mod[:, D_MODEL:2 * D_MODEL]
    h = _rms(x, g1_ref[...]) * (1.0 + sc1) + sh1
    hb = h.astype(BF16)

    plat = _dot(hb, wlat_ref[...])
    q_lat = plat[:, 0:Q_LORA]
    kv_lat = plat[:, Q_LORA:Q_LORA + KV_LORA]
    tail = plat[:, Q_LORA + KV_LORA:Q_LORA + KV_LORA + LANES]
    tail2 = plat[:, LAT_WIDTH - LANES:LAT_WIDTH]

    lane = lax.broadcasted_iota(jnp.int32, (1, LANES), 1)
    fwd = lane < M_HEADS
    gate_i = tail + gbias_ref[0:1, :]
    gate_f = tail2 + gbias_ref[1:2, :]
    log_f = jnp.minimum(gate_f, 0.0) - jnp.log1p(jnp.exp(-jnp.abs(gate_f)))
    b = jnp.where(fwd, _time_scan(log_f, jnp.add, 0.0, False), _time_scan(log_f, jnp.add, 0.0, True))
    a = gate_i - b
    amax = jnp.where(fwd, _time_scan(a, jnp.maximum, -jnp.inf, False),
                     _time_scan(a, jnp.maximum, -jnp.inf, True))
    stats_ref[0] = b.T[0:2 * M_HEADS, :]
    stats_ref[1] = a.T[0:2 * M_HEADS, :]
    stats_ref[2] = amax.T[0:2 * M_HEADS, :]

    krope_placed = jnp.where((lane >= A_NOPE) & (lane < A_QK), tail, 0.0)
    ckv = _rms(kv_lat, kvg_ref[...])
    qn = _rms(q_lat, qlg_ref[...])
    qf = _dot(qn.astype(BF16), wq_ref[...])
    kvf = _dot(ckv.astype(BF16), wkv_ref[...])
    v_ref[...] = kvf[:, QK_PAD:QK_PAD + A_WIDTH].astype(BF16)
    q_rot = k_rot = None
    qhg = qhg_ref[...] * (A_QK ** -0.5 * LOG2E)
    if has_rope:
        cos_t, sin_t = rope_ref[0], rope_ref[1]
        q_rot = (qf[:, QK_PAD:2 * QK_PAD], cos_t * qhg[0:1, :], sin_t * qhg[1:2, :])
        k_rot = (tail2, cos_t * khg_ref[0:1, :], sin_t * khg_ref[1:2, :])
    _write_heads(qf, None, qhg[0:1, :], q_rot, q_ref)
    _write_heads(kvf, krope_placed, khg_ref[0:1, :], k_rot, k_ref)

    if emit_cache:
        ckv_ref, krope_ref = refs[pos:pos + 2]
        ckv_ref[...] = ckv
        krope_ref[...] = tail[:, A_NOPE:A_QK]

    pm = _dot(hb, wmain_ref[...])
    mq_ref[...] = pm[:, 0:M_WIDTH].astype(BF16)
    mv_ref[...] = pm[:, M_WIDTH:2 * M_WIDTH].astype(BF16)
    mo_ref[...] = pm[:, 2 * M_WIDTH:3 * M_WIDTH]
    mkt_ref[...] = _dot_nt(wkt_ref[...], hb) * (M_HEAD_DIM ** -0.5)


def _pre(x, mod3, mod_row0, mod_per_batch, wts, rope_tab, emit_cache):
    B, T, _ = x.shape
    tm = TOKEN_TILE
    has_rope = rope_tab is not None
    tok = lambda w: pl.BlockSpec((None, tm, w), lambda b, i: (b, i, 0))
    in_specs = [
        tok(D_MODEL),
        pl.BlockSpec((None, 1, 6 * D_MODEL), lambda b, i: (mod_row0 + b * mod_per_batch, 0, 0)),
        _const_spec((1, D_MODEL)),
        _const_spec((D_MODEL, 3 * M_WIDTH)),
        _const_spec((M_WIDTH, D_MODEL)),
        _const_spec((D_MODEL, LAT_WIDTH)),
        _const_spec((2, LANES)),
        _const_spec((1, Q_LORA)),
        _const_spec((1, KV_LORA)),
        _const_spec((Q_LORA, 2 * QK_PAD if has_rope else QK_PAD)),
        _const_spec((KV_LORA, QK_PAD + A_WIDTH)),
        _const_spec((2, HEAD_PAD)),
        _const_spec((2, HEAD_PAD)),
    ]
    assert tm == M_BLOCK
    args = [x, mod3, wts["g1"], wts["w_main"], wts["w_kt"], wts["w_lat"], wts["gate_bias"], wts["q_lora_g"],
            wts["kv_lora_g"], wts["w_q_rot"] if has_rope else wts["w_q"], wts["w_kv"],
            wts["q_head_g"], wts["k_head_g"]]
    if has_rope:
        in_specs.append(pl.BlockSpec((2, tm, HEAD_PAD), lambda b, i: (0, i, 0)))
        args.append(rope_tab)
    out_specs = [tok(M_WIDTH),
                 pl.BlockSpec((None, M_WIDTH, tm), lambda b, i: (b, 0, i)),
                 tok(M_WIDTH), tok(M_WIDTH),
                 pl.BlockSpec((None, 3, 2 * M_HEADS, tm), lambda b, i: (b, 0, 0, i)),
                 tok(QK_PAD), tok(QK_PAD), tok(A_WIDTH)]
    out_shape = [
        jax.ShapeDtypeStruct((B, T, M_WIDTH), BF16),
        jax.ShapeDtypeStruct((B, M_WIDTH, T), F32),
        jax.ShapeDtypeStruct((B, T, M_WIDTH), BF16),
        jax.ShapeDtypeStruct((B, T, M_WIDTH), F32),
        jax.ShapeDtypeStruct((B, 3, 2 * M_HEADS, T), F32),
        jax.ShapeDtypeStruct((B, T, QK_PAD), BF16),
        jax.ShapeDtypeStruct((B, T, QK_PAD), BF16),
        jax.ShapeDtypeStruct((B, T, A_WIDTH), BF16),
    ]
    if emit_cache:
        out_specs += [tok(KV_LORA), tok(A_ROPE)]
        out_shape += [jax.ShapeDtypeStruct((B, T, KV_LORA), F32),
                      jax.ShapeDtypeStruct((B, T, A_ROPE), F32)]
    return pl.pallas_call(
        functools.partial(_pre_kernel, has_rope, emit_cache),
        grid=(B, T // tm),
        in_specs=in_specs,
        out_specs=out_specs,
        out_shape=out_shape,
        compiler_params=_params(("parallel", "parallel")),
        name="pre_latent" if has_rope else "pre_context",
    )(*args)


def _ctxkv_kernel(ckv_ref, krp_ref, wkv_ref, khg_ref, k_ref, v_ref):
    kvf = _dot(ckv_ref[...].astype(BF16), wkv_ref[...])
    v_ref[...] = kvf[:, QK_PAD:QK_PAD + A_WIDTH].astype(BF16)
    _write_heads(kvf, krp_ref[...], khg_ref[0:1, :], None, k_ref)


def _ctxkv(ckv, krope_placed, wts):
    B, P, _ = ckv.shape
    tok = lambda w: pl.BlockSpec((None, P, w), lambda b: (b, 0, 0))
    return pl.pallas_call(
        _ctxkv_kernel,
        grid=(B,),
        in_specs=[tok(KV_LORA), tok(HEAD_PAD), _const_spec((KV_LORA, QK_PAD + A_WIDTH)),
                  _const_spec((2, HEAD_PAD))],
        out_specs=[tok(QK_PAD), tok(A_WIDTH)],
        out_shape=[jax.ShapeDtypeStruct((B, P, QK_PAD), BF16),
                   jax.ShapeDtypeStruct((B, P, A_WIDTH), BF16)],
        compiler_params=_params(("parallel",)),
        name="ctx_kv",
    )(ckv, krope_placed, wts["w_kv"], wts["k_head_g"])


def _rows_to_lane_broadcast(rows, spread):
    x = jnp.concatenate(rows, axis=0)
    p1 = x.astype(BF16)
    r1 = x - p1.astype(F32)
    p2 = r1.astype(BF16)
    p3 = (r1 - p2.astype(F32)).astype(BF16)
    pad = jnp.zeros((spread.shape[0] - 3 * len(rows), x.shape[1]), BF16)
    return _dot_tn(jnp.concatenate([p1, p2, p3, pad], axis=0), spread)


def _mlstm_gate_rows(b_row, a_row, amax_row, forward, m):
    L = b_row.shape[1]
    last = slice(L - 1, L) if forward else slice(0, 1)
    total = b_row[:, last]
    g_row = jnp.maximum(m, amax_row)
    m_new = total + jnp.maximum(m, amax_row[:, last])
    w_key_row = jnp.exp2((a_row + (total - m_new)) * LOG2E)
    decay = jnp.exp(total + m - m_new)
    return g_row * LOG2E, (b_row + g_row) * LOG2E, a_row * LOG2E, w_key_row, decay, m_new


def _mlstm_block(s_raw, q, kt, v_aug, g2, mt2, a2_row, w_key_row, decay, allow, CN, m):
    w_intra = jnp.exp2(jnp.where(allow, a2_row - jnp.concatenate([g2, g2], axis=1), -jnp.inf))
    w_inter = jnp.exp2(m * LOG2E - g2)
    s = (s_raw * w_intra).astype(BF16)
    nd = _dot(s, v_aug) + jnp.concatenate([w_inter, w_inter], axis=1) * _dot(q, CN.astype(BF16))
    num, den = nd[:, 0:M_HEAD_DIM], nd[:, M_HEAD_DIM:2 * M_HEAD_DIM]
    h = num / jnp.maximum(jnp.abs(den), jnp.exp2(-mt2))
    CN_new = decay * CN + _dot((kt * w_key_row).astype(BF16), v_aug)
    return h, CN_new


def _mlstm_kernel(has_init, emit_state, n_blocks, heads, *refs):
    q_ref, kt_ref, v_ref, mo_ref, stats_ref, ng_ref, spread_ref = refs[:7]
    pos = 7
    if has_init:
        c0_ref, n0_ref, m0_ref = refs[pos:pos + 3]
        pos += 3
    hm_ref = refs[pos]
    pos += 1
    if emit_state:
        c_ref, n_ref, m_ref = refs[pos:pos + 3]
        pos += 3

    L, Dh = M_BLOCK, M_HEAD_DIM
    t_idx = lax.broadcasted_iota(jnp.int32, (L, L), 0)
    s_idx = lax.broadcasted_iota(jnp.int32, (L, L), 1)
    allow = (s_idx <= t_idx, s_idx >= t_idx)
    spread = spread_ref[...]
    ones = jnp.ones((L, Dh), BF16)

    def lane_broadcast_n(n_row):
        return jnp.broadcast_to(n_row, (Dh, Dh)).T

    def init_state(j, d):
        if has_init:
            return (jnp.concatenate([c0_ref[d, j], lane_broadcast_n(n0_ref[d, j])], axis=1),
                    m0_ref[j, d:d + 1, 0:1])
        return jnp.zeros((Dh, 2 * Dh), F32), jnp.zeros((1, 1), F32)

    def gate_rows(j, c, d, m):
        r = slice(c, c + 1)
        return _mlstm_gate_rows(stats_ref[0, d, j, r, :], stats_ref[1, d, j, r, :], stats_ref[2, d, j, r, :],
                                d == 0, m)

    def blocks(j, jobs, states):
        loaded, rows6, cols_in = {}, [], []
        for (c, d), (CN, m) in zip(jobs, states):
            rows6.append(gate_rows(j, c, d, m))
            cols_in += [rows6[-1][0], rows6[-1][1]]
            if c not in loaded:
                rows, cols = slice(c * L, (c + 1) * L), slice(j * Dh, (j + 1) * Dh)
                q, kt = q_ref[rows, cols], kt_ref[cols, rows]
                v_aug = jnp.concatenate([v_ref[rows, cols], ones], axis=1)
                loaded[c] = (_dot(q, kt.astype(BF16)), q, kt, v_aug)
        cols_out = _rows_to_lane_broadcast(cols_in, spread)
        hs, new_states = [], []
        for idx, ((c, d), (CN, m)) in enumerate(zip(jobs, states)):
            g2 = cols_out[:, (2 * idx) * LANES:(2 * idx + 1) * LANES]
            mt2 = cols_out[:, (2 * idx + 1) * LANES:(2 * idx + 2) * LANES]
            _, _, a2_row, w_key_row, decay, m_new = rows6[idx]
            h, CN_new = _mlstm_block(*loaded[c], g2, mt2, a2_row, w_key_row, decay, allow[d], CN, m)
            hs.append(h)
            new_states.append((CN_new, m_new))
        return hs, new_states

    def finish(j, rows, hs):
        cols = slice(j * Dh, (j + 1) * Dh)
        hn = _rms(hs, ng_ref[j])
        hm_ref[rows, cols] = (hn * jax.nn.sigmoid(mo_ref[rows, cols])).astype(hm_ref.dtype)

    def emit(j, d, state):
        CN, m = state
        c_ref[d, j] = CN[:, 0:Dh]
        n_ref[d, j] = CN[:, Dh:2 * Dh].T[0:1, :]
        m_ref[j, d:d + 1, :] = jnp.broadcast_to(m, (1, LANES))

    if n_blocks > 1:
        hf_scr, hb_scr = refs[pos:pos + 2]

    for j in range(heads):
        cols = slice(j * Dh, (j + 1) * Dh)
        states = [init_state(j, 0), init_state(j, 1)]
        if n_blocks == 1:
            (hf, hb), states = blocks(j, [(0, 0), (0, 1)], states)
            finish(j, slice(0, L), hf + hb)
        else:
            for step in range(n_blocks):
                cf, cb = step, n_blocks - 1 - step
                (hf, hb), states = blocks(j, [(cf, 0), (cb, 1)], states)
                hf_scr[cf * L:(cf + 1) * L, cols] = hf
                hb_scr[cb * L:(cb + 1) * L, cols] = hb
            finish(j, slice(None), hf_scr[:, cols] + hb_scr[:, cols])
        if emit_state:
            emit(j, 0, states[0])
            emit(j, 1, states[1])


def _mlstm(mq, mkt, mv, mo, stats, norm_g, init_state, emit_state, heads):
    B, T, _ = mq.shape
    H, Dh, L = M_HEADS, M_HEAD_DIM, M_BLOCK
    nb = T // L
    w = heads * Dh
    has_init = init_state is not None
    n_rows = 4
    spread = jnp.tile(jnp.repeat(jnp.eye(n_rows, dtype=BF16), LANES, axis=1), (3, 1))
    spread = jnp.pad(spread, ((0, 2 * SUBLANES - 3 * n_rows), (0, 0)))
    stats = stats.reshape(B, 3, 2, H, nb, L)
    tok = pl.BlockSpec((None, T, w), lambda b, h: (b, 0, h))
    in_specs = [tok, pl.BlockSpec((None, w, T), lambda b, h: (b, h, 0)), tok, tok,
                pl.BlockSpec((None, 3, 2, heads, nb, L), lambda b, h: (b, 0, 0, h, 0, 0)),
                pl.BlockSpec((heads, 1, Dh), lambda b, h: (h, 0, 0)),
                _const_spec((2 * SUBLANES, n_rows * LANES))]
    args = [mq, mkt, mv, mo, stats, norm_g, spread]
    state_specs = [pl.BlockSpec((None, 2, heads, Dh, Dh), lambda b, h: (b, 0, h, 0, 0)),
                   pl.BlockSpec((None, 2, heads, 1, Dh), lambda b, h: (b, 0, h, 0, 0)),
                   pl.BlockSpec((None, heads, 2, LANES), lambda b, h: (b, h, 0, 0))]
    if has_init:
        in_specs += state_specs
        args += list(init_state)
    out_specs = [tok]
    out_shape = [jax.ShapeDtypeStruct((B, T, M_WIDTH), BF16)]
    if emit_state:
        out_specs += state_specs
        out_shape += [jax.ShapeDtypeStruct((B, 2, H, Dh, Dh), F32),
                      jax.ShapeDtypeStruct((B, 2, H, 1, Dh), F32),
                      jax.ShapeDtypeStruct((B, H, 2, LANES), F32)]
    scratch = [] if nb == 1 else [pltpu.VMEM((T, w), F32), pltpu.VMEM((T, w), F32)]
    return pl.pallas_call(
        functools.partial(_mlstm_kernel, has_init, emit_state, nb, heads),
        grid=(B, H // heads),
        in_specs=in_specs,
        out_specs=out_specs,
        out_shape=out_shape,
        scratch_shapes=scratch,
        compiler_params=_params(("parallel", "parallel")),
        name="mlstm_latent" if has_init else "mlstm_context",
    )(*args)


def _attn_kernel(has_ctx, *refs):
    if has_ctx:
        q_ref, k_ref, v_ref, kc_ref, vc_ref, o_ref = refs
    else:
        q_ref, k_ref, v_ref, o_ref = refs
    lane = lax.broadcasted_iota(jnp.int32, (1, LANES), 1)
    ones = lambda n: jnp.ones((n, LANES), BF16)
    for pair in range(A_HEADS // 2):
        vsl = slice(pair * LANES, (pair + 1) * LANES)
        v_aug = jnp.concatenate([v_ref[:, vsl], ones(v_ref.shape[0])], axis=1)
        if has_ctx:
            vc_aug = jnp.concatenate([vc_ref[:, vsl], ones(vc_ref.shape[0])], axis=1)
        outs = []
        for e in range(2):
            hsl = slice((2 * pair + e) * HEAD_PAD, (2 * pair + e + 1) * HEAD_PAD)
            qh = q_ref[:, hsl]
            s = _dot_nt(qh, k_ref[:, hsl])
            mx = jnp.max(s, axis=1, keepdims=True)
            if has_ctx:
                sc = _dot_nt(qh, kc_ref[:, hsl])
                mx = jnp.maximum(mx, jnp.max(sc, axis=1, keepdims=True))
            od = _dot(jnp.exp2(s - mx).astype(BF16), v_aug)
            if has_ctx:
                od = od + _dot(jnp.exp2(sc - mx).astype(BF16), vc_aug)
            outs.append(od[:, 0:LANES] / od[:, LANES:2 * LANES])
        o_ref[:, vsl] = jnp.where(lane < A_VDIM, outs[0], outs[1]).astype(o_ref.dtype)


def _attn(q, k, v, ctx_kv):
    B, T, _ = q.shape
    tq = Q_TILE
    has_ctx = ctx_kv is not None
    full = lambda n, w: pl.BlockSpec((None, n, w), lambda b, i: (b, 0, 0))
    in_specs = [pl.BlockSpec((None, tq, QK_PAD), lambda b, i: (b, i, 0)), full(T, QK_PAD), full(T, A_WIDTH)]
    args = [q, k, v]
    if has_ctx:
        P = ctx_kv[0].shape[1]
        in_specs += [full(P, QK_PAD), full(P, A_WIDTH)]
        args += list(ctx_kv)
    return pl.pallas_call(
        functools.partial(_attn_kernel, has_ctx),
        grid=(B, T // tq),
        in_specs=in_specs,
        out_specs=pl.BlockSpec((None, tq, A_WIDTH), lambda b, i: (b, i, 0)),
        out_shape=jax.ShapeDtypeStruct((B, T, A_WIDTH), BF16),
        compiler_params=_params(("parallel", "parallel")),
        name="attn_latent" if has_ctx else "attn_context",
    )(*args)


def _post_kernel(x_ref, hm_ref, ha_ref, mod_ref, g2_ref, wout_ref, wup_ref, wdown_ref, y_ref):
    mod = mod_ref[...]
    gate1 = mod[:, 2 * D_MODEL:3 * D_MODEL]
    sh2 = mod[:, 3 * D_MODEL:4 * D_MODEL]
    sc2 = mod[:, 4 * D_MODEL:5 * D_MODEL]
    gate2 = mod[:, 5 * D_MODEL:6 * D_MODEL]
    mix = jnp.concatenate([hm_ref[...], ha_ref[...]], axis=-1)
    x1 = x_ref[...] + gate1 * _dot(mix, wout_ref[...])
    h2 = (_rms(x1, g2_ref[...]) * (1.0 + sc2) + sh2).astype(BF16)
    acc = jnp.zeros_like(x1)
    for c in range(D_FF // FF_TILE):
        sl = slice(c * FF_TILE, (c + 1) * FF_TILE)
        u = jnp.maximum(_dot(h2, wup_ref[:, sl]), 0.0)
        acc = acc + _dot((u * u).astype(BF16), wdown_ref[sl, :])
    y_ref[...] = x1 + gate2 * acc


def _post(x, hm, ha, mod3, mod_row0, mod_per_batch, wts):
    B, T, _ = x.shape
    tm = TOKEN_TILE
    tok = lambda w: pl.BlockSpec((None, tm, w), lambda b, i: (b, i, 0))
    return pl.pallas_call(
        _post_kernel,
        grid=(B, T // tm),
        in_specs=[tok(D_MODEL), tok(M_WIDTH), tok(A_WIDTH),
                  pl.BlockSpec((None, 1, 6 * D_MODEL), lambda b, i: (mod_row0 + b * mod_per_batch, 0, 0)),
                  _const_spec((1, D_MODEL)),
                  _const_spec((M_WIDTH + A_WIDTH, D_MODEL)),
                  _const_spec((D_MODEL, D_FF)),
                  _const_spec((D_FF, D_MODEL))],
        out_specs=tok(D_MODEL),
        out_shape=jax.ShapeDtypeStruct((B, T, D_MODEL), F32),
        compiler_params=_params(("parallel", "parallel")),
        name="post",
    )(x, hm, ha, mod3, wts["g2"], wts["w_out"], wts["w_up"], wts["w_down"])


def _prepare_weights(norm1_g, norm2_g, w_in, mlstm_gate_b, q_lora_g, kv_lora_g, w_q_up, w_kv_up,
                     q_head_g, k_head_g, w_out, w_mlp_up, w_mlp_down):
    o_g = 4 * M_WIDTH
    o_q = o_g + N_GATES
    o_kv = o_q + Q_LORA
    o_kr = o_kv + KV_LORA
    half = A_ROPE // 2
    n_dh = 2 * M_HEADS
    w_gate = w_in[:, o_g:o_q].reshape(D_MODEL, 2, 2, M_HEADS)
    bias = mlstm_gate_b.reshape(2, 2, M_HEADS)

    def gate_tile(which, rope_cols):
        return jnp.concatenate([
            w_gate[:, :, which, :].reshape(D_MODEL, n_dh), jnp.zeros((D_MODEL, A_NOPE - n_dh), F32),
            rope_cols, jnp.zeros((D_MODEL, LANES - A_QK), F32)], axis=1)

    def rot_partner(a):
        z = jnp.zeros(a.shape[:-1] + (A_NOPE,), a.dtype)
        return jnp.concatenate([z, a[..., A_NOPE + half:A_QK], a[..., A_NOPE:A_NOPE + half]], axis=-1)

    pad_tile = lambda a: jnp.pad(a, [(0, 0)] * (a.ndim - 1) + [(0, HEAD_PAD - A_QK)])
    w_kr = w_in[:, o_kr:o_kr + A_ROPE]
    w_kr_partner = jnp.concatenate([w_kr[:, half:], w_kr[:, :half]], axis=1)
    w_lat = jnp.concatenate([w_in[:, o_q:o_kv], w_in[:, o_kv:o_kr],
                             gate_tile(0, w_kr), gate_tile(1, w_kr_partner)], axis=1)
    gate_bias = jnp.pad(jnp.stack([bias[:, 0, :].reshape(n_dh), bias[:, 1, :].reshape(n_dh)], axis=0),
                        ((0, 0), (0, LANES - n_dh)))
    w_q3 = w_q_up.reshape(Q_LORA, A_HEADS, A_QK)
    w_q = pad_tile(w_q3).reshape(Q_LORA, QK_PAD)
    w_q_partner = pad_tile(rot_partner(w_q3)).reshape(Q_LORA, QK_PAD)
    w_kv3 = w_kv_up.reshape(KV_LORA, A_HEADS, A_NOPE + A_VDIM)
    w_k = jnp.pad(w_kv3[:, :, :A_NOPE], ((0, 0), (0, 0), (0, HEAD_PAD - A_NOPE)))
    w_v = w_kv3[:, :, A_NOPE:]
    w_kv = jnp.concatenate([w_k.reshape(KV_LORA, QK_PAD), w_v.reshape(KV_LORA, A_WIDTH)], axis=1)
    pad_head = lambda g: jnp.stack([pad_tile(g), pad_tile(rot_partner(g))], axis=0)
    return {
        "g1": norm1_g.reshape(1, D_MODEL),
        "g2": norm2_g.reshape(1, D_MODEL),
        "w_main": jnp.concatenate([w_in[:, 0:M_WIDTH], w_in[:, 2 * M_WIDTH:o_g]], axis=1).astype(BF16),
        "w_kt": w_in[:, M_WIDTH:2 * M_WIDTH].T.astype(BF16),
        "w_lat": w_lat.astype(BF16),
        "gate_bias": gate_bias,
        "q_lora_g": q_lora_g.reshape(1, Q_LORA),
        "kv_lora_g": kv_lora_g.reshape(1, KV_LORA),
        "w_q": w_q.astype(BF16),
        "w_q_rot": jnp.concatenate([w_q, w_q_partner], axis=1).astype(BF16),
        "w_kv": w_kv.astype(BF16),
        "q_head_g": pad_head(q_head_g),
        "k_head_g": pad_head(k_head_g),
        "w_out": w_out.astype(BF16),
        "w_up": w_mlp_up.astype(BF16),
        "w_down": w_mlp_down.astype(BF16),
    }


def _rope_tables(T):
    rows = T // GRID_W
    row = jnp.repeat(jnp.arange(rows, dtype=F32), GRID_W)
    col = jnp.tile(jnp.arange(GRID_W, dtype=F32), rows)
    half = A_ROPE // 2
    inv = ROPE_BASE ** (-jnp.arange(0, half, 2, dtype=F32) / half)
    ang = jnp.concatenate([row[:, None] * inv, col[:, None] * inv], axis=-1)
    cos, sin = jnp.cos(ang), jnp.sin(ang)
    ones = jnp.ones((T, A_NOPE), F32)
    z = lambda w: jnp.zeros((T, w), F32)
    tail = LANES - A_QK
    cos_t = jnp.concatenate([ones, cos, cos, z(tail)], axis=1)
    sin_t = jnp.concatenate([z(A_NOPE), -sin, sin, z(tail)], axis=1)
    return jnp.stack([cos_t, sin_t], axis=0)


def _layer_pass(x, mod3, mod_row0, mod_per_batch, wts, norm_g, rope_tab, init_state, ctx_kv, is_context):
    pre = _pre(x, mod3, mod_row0, mod_per_batch, wts, rope_tab, emit_cache=is_context)
    mq, mkt, mv, mo, stats, q, k, v = pre[:8]
    ml = _mlstm(mq, mkt, mv, mo, stats, norm_g, init_state, emit_state=is_context,
                heads=M_HEADS if is_context else 1)
    ha = _attn(q, k, v, ctx_kv)
    y = _post(x, ml[0], ha, mod3, mod_row0, mod_per_batch, wts)
    return y, pre[8:], ml[1:]


def kernel(x_prompt, x_sample, cache_mla_ckv, cache_mla_krope, state_mlstm_C, state_mlstm_n, state_mlstm_m,
           c, c_ctx, norm1_g, norm2_g, w_ada, b_ada, w_in, mlstm_gate_b, mlstm_norm_g,
           q_lora_g, kv_lora_g, w_q_up, w_kv_up, q_head_g, k_head_g, w_out, w_mlp_up, w_mlp_down):
    depth = w_in.shape[0]
    Bd = x_sample.shape[0]
    cond8 = jnp.concatenate([c_ctx[None, :], c, jnp.zeros((8 - 1 - Bd, D_MODEL), F32)], axis=0)
    rope_tab = _rope_tables(x_sample.shape[1])

    y, z = x_prompt, x_sample
    ckvs, kropes, Cs, ns, ms = [], [], [], [], []
    for l in range(depth):
        wts = _prepare_weights(norm1_g[l], norm2_g[l], w_in[l], mlstm_gate_b[l], q_lora_g[l], kv_lora_g[l],
                               w_q_up[l], w_kv_up[l], q_head_g[l], k_head_g[l], w_out[l], w_mlp_up[l],
                               w_mlp_down[l])
        norm_g = mlstm_norm_g[l].reshape(M_HEADS, 1, M_HEAD_DIM)
        mod3 = _ada(cond8, w_ada[l], b_ada[l]).reshape(8, 1, 6 * D_MODEL)

        y, (ckv, krope), (C_new, n_new, m_new) = _layer_pass(
            y, mod3, 0, 0, wts, norm_g, None, None, None, True)
        ckvs.append(ckv)
        kropes.append(krope)
        Cs.append(C_new)
        ns.append(n_new[:, :, :, 0, :])
        ms.append(m_new[:, :, :, 0].transpose(0, 2, 1))

        init_state = (state_mlstm_C[:, l],
                      state_mlstm_n[:, l][:, :, :, None, :],
                      jnp.broadcast_to(state_mlstm_m[:, l].transpose(0, 2, 1)[..., None],
                                       (Bd, M_HEADS, 2, LANES)))
        krope_placed = jnp.pad(cache_mla_krope[:, l], ((0, 0), (0, 0), (A_NOPE, LANES - A_QK)))
        ctx_kv = _ctxkv(cache_mla_ckv[:, l], krope_placed, wts)
        z, _, _ = _layer_pass(z, mod3, 1, 1, wts, norm_g, rope_tab, init_state, ctx_kv, False)

    return (y, z, jnp.stack(ckvs, axis=1), jnp.stack(kropes, axis=1), jnp.stack(Cs, axis=1),
            jnp.stack(ns, axis=1), jnp.stack(ms, axis=1))
```

```python
import functools

import jax
import jax.numpy as jnp
from jax import lax
from jax.experimental import pallas as pl
from jax.experimental.pallas import tpu as pltpu

F32 = jnp.float32
BF16 = jnp.bfloat16

D_MODEL = 1024
GRID_W = 64
M_HEADS = 4
M_HEAD_DIM = 128
M_WIDTH = M_HEADS * M_HEAD_DIM
M_BLOCK = 256
A_HEADS = 8
A_NOPE = 64
A_ROPE = 32
A_QK = A_NOPE + A_ROPE
A_VDIM = 64
A_WIDTH = A_HEADS * A_VDIM
Q_LORA = 384
KV_LORA = 256
ROPE_BASE = 10000.0
D_FF = 4 * D_MODEL
EPS = 1e-6

LANES = 128
SUBLANES = 8
LOG2E = 1.4426950408889634
HEAD_PAD = LANES
QK_PAD = A_HEADS * HEAD_PAD
N_GATES = 4 * M_HEADS
LAT_WIDTH = Q_LORA + KV_LORA + 2 * LANES
VMEM_LIMIT = 56 * 1024 * 1024

TOKEN_TILE = 256
POST_TILE = 512
Q_TILE = 256
ADA_TILE_N = 1536
FF_TILE = 1024


def _dot(a, b):
    return jnp.dot(a, b, preferred_element_type=F32)


def _dot_nt(a, b):
    return lax.dot_general(a, b, (((1,), (1,)), ((), ())), preferred_element_type=F32)


def _dot_tn(a, b):
    return lax.dot_general(a, b, (((0,), (0,)), ((), ())), preferred_element_type=F32)


def _rms(x, g):
    y = x * lax.rsqrt(jnp.mean(x * x, axis=-1, keepdims=True) + EPS)
    return y * g


def _params(sem):
    return pltpu.CompilerParams(dimension_semantics=sem, vmem_limit_bytes=VMEM_LIMIT)


def _const_spec(shape):
    zeros = (0,) * len(shape)
    return pl.BlockSpec(shape, lambda *_: zeros, pipeline_mode=pl.Buffered(1))


def _ada_kernel(cond_ref, w_ref, b_ref, o_ref):
    c = cond_ref[...]
    s = (c * jax.nn.sigmoid(c)).astype(BF16)
    o_ref[...] = _dot(s, w_ref[...].astype(BF16)) + b_ref[...]


def _ada(cond8, w_ada, b_ada):
    n = w_ada.shape[1]
    return pl.pallas_call(
        _ada_kernel,
        grid=(n // ADA_TILE_N,),
        in_specs=[
            pl.BlockSpec((8, D_MODEL), lambda j: (0, 0)),
            pl.BlockSpec((D_MODEL, ADA_TILE_N), lambda j: (0, j)),
            pl.BlockSpec((1, ADA_TILE_N), lambda j: (0, j)),
        ],
        out_specs=pl.BlockSpec((8, ADA_TILE_N), lambda j: (0, j)),
        out_shape=jax.ShapeDtypeStruct((8, n), F32),
        compiler_params=_params(("parallel",)),
        name="ada",
    )(cond8, w_ada, b_ada.reshape(1, n))


def _write_heads(src, extra, g_pad, rot, dst_ref):
    for h in range(A_HEADS):
        sl = slice(h * HEAD_PAD, (h + 1) * HEAD_PAD)
        xh = src[:, sl]
        if extra is not None:
            xh = xh + extra
        ss = jnp.sum(xh * xh, axis=-1, keepdims=True) * (1.0 / A_QK)
        r = lax.rsqrt(ss + EPS)
        if rot is None:
            y = xh * r * g_pad
        else:
            partner, cos_g, sin_g = rot
            ph = partner if partner.shape[1] == HEAD_PAD else partner[:, sl]
            y = (xh * cos_g + ph * sin_g) * r
        dst_ref[:, sl] = y.astype(dst_ref.dtype)


def _time_scan(x, op, identity, reverse):
    n = x.shape[0]
    row = lax.broadcasted_iota(jnp.int32, x.shape, 0)
    shift = 1
    while shift < n:
        if shift < SUBLANES:
            if reverse:
                moved = jnp.where(row < n - shift, pltpu.roll(x, n - shift, 0), identity)
            else:
                moved = jnp.where(row >= shift, pltpu.roll(x, shift, 0), identity)
        else:
            fill = jnp.full((shift, x.shape[1]), identity, x.dtype)
            moved = (jnp.concatenate([x[shift:], fill], axis=0) if reverse
                     else jnp.concatenate([fill, x[:n - shift]], axis=0))
        x = op(x, moved)
        shift *= 2
    return x


def _pre_kernel(has_rope, emit_cache, *refs):
    (x_ref, mod_ref, g1_ref, wmain_ref, wkt_ref, wlat_ref, gbias_ref, qlg_ref, kvg_ref, wq_ref, wkv_ref,
     qhg_ref, khg_ref) = refs[:13]
    pos = 13
    if has_rope:
        rope_ref = refs[pos]
        pos += 1
    (mq_ref, mkt_ref, mv_ref, mo_ref, stats_ref, q_ref, k_ref, v_ref) = refs[pos:pos + 8]
    pos += 8

    x = x_ref[...]
    mod = mod_ref[...]
    sh1 = mod[:, 0:D_MODEL]
    sc1 = mod[:, D_MODEL:2 * D_MODEL]
    h = _rms(x, g1_ref[...]) * (1.0 + sc1) + sh1
    hb = h.astype(BF16)

    plat = _dot_nt(hb, wlat_ref[...])
    q_lat = plat[:, 0:Q_LORA]
    kv_lat = plat[:, Q_LORA:Q_LORA + KV_LORA]
    tail = plat[:, Q_LORA + KV_LORA:Q_LORA + KV_LORA + LANES]
    tail2 = plat[:, LAT_WIDTH - LANES:LAT_WIDTH]

    lane = lax.broadcasted_iota(jnp.int32, (1, LANES), 1)
    fwd = lane < M_HEADS
    gate_i = tail + gbias_ref[0:1, :]
    gate_f = tail2 + gbias_ref[1:2, :]
    log_f = jnp.minimum(gate_f, 0.0) - jnp.log1p(jnp.exp(-jnp.abs(gate_f)))
    b = jnp.where(fwd, _time_scan(log_f, jnp.add, 0.0, False), _time_scan(log_f, jnp.add, 0.0, True))
    a = gate_i - b
    amax = jnp.where(fwd, _time_scan(a, jnp.maximum, -jnp.inf, False),
                     _time_scan(a, jnp.maximum, -jnp.inf, True))
    stats_ref[0] = b.T[0:2 * M_HEADS, :]
    stats_ref[1] = a.T[0:2 * M_HEADS, :]
    stats_ref[2] = amax.T[0:2 * M_HEADS, :]

    krope_placed = jnp.where((lane >= A_NOPE) & (lane < A_QK), tail, 0.0)
    ckv = _rms(kv_lat, kvg_ref[...])
    qn = _rms(q_lat, qlg_ref[...])
    qf = _dot(qn.astype(BF16), wq_ref[...])
    kvf = _dot(ckv.astype(BF16), wkv_ref[...])
    v_ref[...] = kvf[:, QK_PAD:QK_PAD + A_WIDTH].astype(BF16)
    q_rot = k_rot = None
    qhg = qhg_ref[...] * (A_QK ** -0.5 * LOG2E)
    if has_rope:
        cos_t, sin_t = rope_ref[0], rope_ref[1]
        q_rot = (qf[:, QK_PAD:2 * QK_PAD], cos_t * qhg[0:1, :], sin_t * qhg[1:2, :])
        k_rot = (tail2, cos_t * khg_ref[0:1, :], sin_t * khg_ref[1:2, :])
    _write_heads(qf, None, qhg[0:1, :], q_rot, q_ref)
    _write_heads(kvf, krope_placed, khg_ref[0:1, :], k_rot, k_ref)

    if emit_cache:
        ckv_ref, krope_ref = refs[pos:pos + 2]
        ckv_ref[...] = ckv
        krope_ref[...] = tail[:, A_NOPE:A_QK]

    pm = _dot_nt(hb, wmain_ref[...])
    mq_ref[...] = pm[:, 0:M_WIDTH].astype(BF16)
    mv_ref[...] = pm[:, M_WIDTH:2 * M_WIDTH].astype(BF16)
    mo_ref[...] = pm[:, 2 * M_WIDTH:3 * M_WIDTH]
    mkt_ref[...] = _dot_nt(wkt_ref[...], hb) * (M_HEAD_DIM ** -0.5)


def _pre(x, mod3, mod_row0, mod_per_batch, wts, rope_tab, emit_cache):
    B, T, _ = x.shape
    tm = TOKEN_TILE
    has_rope = rope_tab is not None
    tok = lambda w: pl.BlockSpec((None, tm, w), lambda b, i: (b, i, 0))
    in_specs = [
        tok(D_MODEL),
        pl.BlockSpec((None, 1, 6 * D_MODEL), lambda b, i: (mod_row0 + b * mod_per_batch, 0, 0)),
        _const_spec((1, D_MODEL)),
        _const_spec((3 * M_WIDTH, D_MODEL)),
        _const_spec((M_WIDTH, D_MODEL)),
        _const_spec((LAT_WIDTH, D_MODEL)),
        _const_spec((2, LANES)),
        _const_spec((1, Q_LORA)),
        _const_spec((1, KV_LORA)),
        _const_spec((Q_LORA, 2 * QK_PAD if has_rope else QK_PAD)),
        _const_spec((KV_LORA, QK_PAD + A_WIDTH)),
        _const_spec((2, HEAD_PAD)),
        _const_spec((2, HEAD_PAD)),
    ]
    assert tm == M_BLOCK
    args = [x, mod3, wts["g1"], wts["w_main"], wts["w_kt"], wts["w_lat"], wts["gate_bias"], wts["q_lora_g"],
            wts["kv_lora_g"], wts["w_q_rot"] if has_rope else wts["w_q"], wts["w_kv"],
            wts["q_head_g"], wts["k_head_g"]]
    if has_rope:
        in_specs.append(pl.BlockSpec((2, tm, HEAD_PAD), lambda b, i: (0, i, 0)))
        args.append(rope_tab)
    out_specs = [tok(M_WIDTH),
                 pl.BlockSpec((None, M_WIDTH, tm), lambda b, i: (b, 0, i)),
                 tok(M_WIDTH), tok(M_WIDTH),
                 pl.BlockSpec((None, 3, 2 * M_HEADS, tm), lambda b, i: (b, 0, 0, i)),
                 tok(QK_PAD), tok(QK_PAD), tok(A_WIDTH)]
    out_shape = [
        jax.ShapeDtypeStruct((B, T, M_WIDTH), BF16),
        jax.ShapeDtypeStruct((B, M_WIDTH, T), F32),
        jax.ShapeDtypeStruct((B, T, M_WIDTH), BF16),
        jax.ShapeDtypeStruct((B, T, M_WIDTH), F32),
        jax.ShapeDtypeStruct((B, 3, 2 * M_HEADS, T), F32),
        jax.ShapeDtypeStruct((B, T, QK_PAD), BF16),
        jax.ShapeDtypeStruct((B, T, QK_PAD), BF16),
        jax.ShapeDtypeStruct((B, T, A_WIDTH), BF16),
    ]
    if emit_cache:
        out_specs += [tok(KV_LORA), tok(A_ROPE)]
        out_shape += [jax.ShapeDtypeStruct((B, T, KV_LORA), F32),
                      jax.ShapeDtypeStruct((B, T, A_ROPE), F32)]
    return pl.pallas_call(
        functools.partial(_pre_kernel, has_rope, emit_cache),
        grid=(B, T // tm),
        in_specs=in_specs,
        out_specs=out_specs,
        out_shape=out_shape,
        compiler_params=_params(("parallel", "parallel")),
        name="pre_latent" if has_rope else "pre_context",
    )(*args)


def _ctxkv_kernel(ckv_ref, krp_ref, wkv_ref, khg_ref, k_ref, v_ref):
    kvf = _dot(ckv_ref[...].astype(BF16), wkv_ref[...])
    v_ref[...] = kvf[:, QK_PAD:QK_PAD + A_WIDTH].astype(BF16)
    _write_heads(kvf, krp_ref[...], khg_ref[0:1, :], None, k_ref)


def _ctxkv(ckv, krope_placed, wts):
    B, P, _ = ckv.shape
    tok = lambda w: pl.BlockSpec((None, P, w), lambda b: (b, 0, 0))
    return pl.pallas_call(
        _ctxkv_kernel,
        grid=(B,),
        in_specs=[tok(KV_LORA), tok(HEAD_PAD), _const_spec((KV_LORA, QK_PAD + A_WIDTH)),
                  _const_spec((2, HEAD_PAD))],
        out_specs=[tok(QK_PAD), tok(A_WIDTH)],
        out_shape=[jax.ShapeDtypeStruct((B, P, QK_PAD), BF16),
                   jax.ShapeDtypeStruct((B, P, A_WIDTH), BF16)],
        compiler_params=_params(("parallel",)),
        name="ctx_kv",
    )(ckv, krope_placed, wts["w_kv"], wts["k_head_g"])


def _rows_to_lane_broadcast(rows, spread):
    x = jnp.concatenate(rows, axis=0)
    p1 = x.astype(BF16)
    r1 = x - p1.astype(F32)
    p2 = r1.astype(BF16)
    p3 = (r1 - p2.astype(F32)).astype(BF16)
    pad = jnp.zeros((spread.shape[0] - 3 * len(rows), x.shape[1]), BF16)
    return _dot_tn(jnp.concatenate([p1, p2, p3, pad], axis=0), spread)


def _mlstm_gate_rows(b_row, a_row, amax_row, forward, m):
    L = b_row.shape[1]
    last = slice(L - 1, L) if forward else slice(0, 1)
    total = b_row[:, last]
    g_row = jnp.maximum(m, amax_row)
    m_new = total + jnp.maximum(m, amax_row[:, last])
    w_key_row = jnp.exp2((a_row + (total - m_new)) * LOG2E)
    decay = jnp.exp(total + m - m_new)
    return g_row * LOG2E, (b_row + g_row) * LOG2E, a_row * LOG2E, w_key_row, decay, m_new


def _mlstm_block(s_raw, q, kt, v_aug, g2, mt2, a2_row, w_key_row, decay, allow, CN, m):
    w_intra = jnp.exp2(jnp.where(allow, a2_row - jnp.concatenate([g2, g2], axis=1), -jnp.inf))
    w_inter = jnp.exp2(m * LOG2E - g2)
    s = (s_raw * w_intra).astype(BF16)
    nd = _dot(s, v_aug) + jnp.concatenate([w_inter, w_inter], axis=1) * _dot(q, CN.astype(BF16))
    num, den = nd[:, 0:M_HEAD_DIM], nd[:, M_HEAD_DIM:2 * M_HEAD_DIM]
    h = num / jnp.maximum(jnp.abs(den), jnp.exp2(-mt2))
    CN_new = decay * CN + _dot((kt * w_key_row).astype(BF16), v_aug)
    return h, CN_new


def _mlstm_kernel(has_init, emit_state, n_blocks, heads, *refs):
    q_ref, kt_ref, v_ref, mo_ref, stats_ref, ng_ref, spread_ref = refs[:7]
    pos = 7
    if has_init:
        c0_ref, n0_ref, m0_ref = refs[pos:pos + 3]
        pos += 3
    hm_ref = refs[pos]
    pos += 1
    if emit_state:
        c_ref, n_ref, m_ref = refs[pos:pos + 3]
        pos += 3

    L, Dh = M_BLOCK, M_HEAD_DIM
    t_idx = lax.broadcasted_iota(jnp.int32, (L, L), 0)
    s_idx = lax.broadcasted_iota(jnp.int32, (L, L), 1)
    allow = (s_idx <= t_idx, s_idx >= t_idx)
    spread = spread_ref[...]
    ones = jnp.ones((L, Dh), BF16)

    def lane_broadcast_n(n_row):
        return jnp.broadcast_to(n_row, (Dh, Dh)).T

    def init_state(j, d):
        if has_init:
            return (jnp.concatenate([c0_ref[d, j], lane_broadcast_n(n0_ref[d, j])], axis=1),
                    m0_ref[j, d:d + 1, 0:1])
        return jnp.zeros((Dh, 2 * Dh), F32), jnp.zeros((1, 1), F32)

    def gate_rows(j, c, d, m):
        r = slice(c, c + 1)
        return _mlstm_gate_rows(stats_ref[0, d, j, r, :], stats_ref[1, d, j, r, :], stats_ref[2, d, j, r, :],
                                d == 0, m)

    def blocks(j, jobs, states):
        loaded, rows6, cols_in = {}, [], []
        for (c, d), (CN, m) in zip(jobs, states):
            rows6.append(gate_rows(j, c, d, m))
            cols_in += [rows6[-1][0], rows6[-1][1]]
            if c not in loaded:
                rows, cols = slice(c * L, (c + 1) * L), slice(j * Dh, (j + 1) * Dh)
                q, kt = q_ref[rows, cols], kt_ref[cols, rows]
                v_aug = jnp.concatenate([v_ref[rows, cols], ones], axis=1)
                loaded[c] = (_dot(q, kt.astype(BF16)), q, kt, v_aug)
        cols_out = _rows_to_lane_broadcast(cols_in, spread)
        hs, new_states = [], []
        for idx, ((c, d), (CN, m)) in enumerate(zip(jobs, states)):
            g2 = cols_out[:, (2 * idx) * LANES:(2 * idx + 1) * LANES]
            mt2 = cols_out[:, (2 * idx + 1) * LANES:(2 * idx + 2) * LANES]
            _, _, a2_row, w_key_row, decay, m_new = rows6[idx]
            h, CN_new = _mlstm_block(*loaded[c], g2, mt2, a2_row, w_key_row, decay, allow[d], CN, m)
            hs.append(h)
            new_states.append((CN_new, m_new))
        return hs, new_states

    def finish(j, rows, hs):
        cols = slice(j * Dh, (j + 1) * Dh)
        hn = _rms(hs, ng_ref[j])
        hm_ref[rows, cols] = (hn * jax.nn.sigmoid(mo_ref[rows, cols])).astype(hm_ref.dtype)

    def emit(j, d, state):
        CN, m = state
        c_ref[d, j] = CN[:, 0:Dh]
        n_ref[d, j] = CN[:, Dh:2 * Dh].T[0:1, :]
        m_ref[j, d:d + 1, :] = jnp.broadcast_to(m, (1, LANES))

    if n_blocks > 1:
        hf_scr, hb_scr = refs[pos:pos + 2]

    for j in range(heads):
        cols = slice(j * Dh, (j + 1) * Dh)
        states = [init_state(j, 0), init_state(j, 1)]
        if n_blocks == 1:
            (hf, hb), states = blocks(j, [(0, 0), (0, 1)], states)
            finish(j, slice(0, L), hf + hb)
        else:
            for step in range(n_blocks):
                cf, cb = step, n_blocks - 1 - step
                (hf, hb), states = blocks(j, [(cf, 0), (cb, 1)], states)
                hf_scr[cf * L:(cf + 1) * L, cols] = hf
                hb_scr[cb * L:(cb + 1) * L, cols] = hb
            finish(j, slice(None), hf_scr[:, cols] + hb_scr[:, cols])
        if emit_state:
            emit(j, 0, states[0])
            emit(j, 1, states[1])


def _mlstm(mq, mkt, mv, mo, stats, norm_g, init_state, emit_state, heads):
    B, T, _ = mq.shape
    H, Dh, L = M_HEADS, M_HEAD_DIM, M_BLOCK
    nb = T // L
    w = heads * Dh
    has_init = init_state is not None
    n_rows = 4
    spread = jnp.tile(jnp.repeat(jnp.eye(n_rows, dtype=BF16), LANES, axis=1), (3, 1))
    spread = jnp.pad(spread, ((0, 2 * SUBLANES - 3 * n_rows), (0, 0)))
    stats = stats.reshape(B, 3, 2, H, nb, L)
    tok = pl.BlockSpec((None, T, w), lambda b, h: (b, 0, h))
    in_specs = [tok, pl.BlockSpec((None, w, T), lambda b, h: (b, h, 0)), tok, tok,
                pl.BlockSpec((None, 3, 2, heads, nb, L), lambda b, h: (b, 0, 0, h, 0, 0)),
                pl.BlockSpec((heads, 1, Dh), lambda b, h: (h, 0, 0)),
                _const_spec((2 * SUBLANES, n_rows * LANES))]
    args = [mq, mkt, mv, mo, stats, norm_g, spread]
    state_specs = [pl.BlockSpec((None, 2, heads, Dh, Dh), lambda b, h: (b, 0, h, 0, 0)),
                   pl.BlockSpec((None, 2, heads, 1, Dh), lambda b, h: (b, 0, h, 0, 0)),
                   pl.BlockSpec((None, heads, 2, LANES), lambda b, h: (b, h, 0, 0))]
    if has_init:
        in_specs += state_specs
        args += list(init_state)
    out_specs = [tok]
    out_shape = [jax.ShapeDtypeStruct((B, T, M_WIDTH), BF16)]
    if emit_state:
        out_specs += state_specs
        out_shape += [jax.ShapeDtypeStruct((B, 2, H, Dh, Dh), F32),
                      jax.ShapeDtypeStruct((B, 2, H, 1, Dh), F32),
                      jax.ShapeDtypeStruct((B, H, 2, LANES), F32)]
    scratch = [] if nb == 1 else [pltpu.VMEM((T, w), F32), pltpu.VMEM((T, w), F32)]
    return pl.pallas_call(
        functools.partial(_mlstm_kernel, has_init, emit_state, nb, heads),
        grid=(B, H // heads),
        in_specs=in_specs,
        out_specs=out_specs,
        out_shape=out_shape,
        scratch_shapes=scratch,
        compiler_params=_params(("parallel", "parallel")),
        name="mlstm_latent" if has_init else "mlstm_context",
    )(*args)


def _attn_kernel(has_ctx, *refs):
    if has_ctx:
        q_ref, k_ref, v_ref, kc_ref, vc_ref, o_ref = refs
    else:
        q_ref, k_ref, v_ref, o_ref = refs
    lane = lax.broadcasted_iota(jnp.int32, (1, LANES), 1)
    ones = lambda n: jnp.ones((n, LANES), BF16)
    for pair in range(A_HEADS // 2):
        vsl = slice(pair * LANES, (pair + 1) * LANES)
        v_aug = jnp.concatenate([v_ref[:, vsl], ones(v_ref.shape[0])], axis=1)
        if has_ctx:
            vc_aug = jnp.concatenate([vc_ref[:, vsl], ones(vc_ref.shape[0])], axis=1)
        outs = []
        for e in range(2):
            hsl = slice((2 * pair + e) * HEAD_PAD, (2 * pair + e + 1) * HEAD_PAD)
            qh = q_ref[:, hsl]
            s = _dot_nt(qh, k_ref[:, hsl])
            mx = jnp.max(s, axis=1, keepdims=True)
            if has_ctx:
                sc = _dot_nt(qh, kc_ref[:, hsl])
                mx = jnp.maximum(mx, jnp.max(sc, axis=1, keepdims=True))
            od = _dot(jnp.exp2(s - mx).astype(BF16), v_aug)
            if has_ctx:
                od = od + _dot(jnp.exp2(sc - mx).astype(BF16), vc_aug)
            outs.append(od[:, 0:LANES] / od[:, LANES:2 * LANES])
        o_ref[:, vsl] = jnp.where(lane < A_VDIM, outs[0], outs[1]).astype(o_ref.dtype)


def _attn(q, k, v, ctx_kv):
    B, T, _ = q.shape
    tq = Q_TILE
    has_ctx = ctx_kv is not None
    full = lambda n, w: pl.BlockSpec((None, n, w), lambda b, i: (b, 0, 0))
    in_specs = [pl.BlockSpec((None, tq, QK_PAD), lambda b, i: (b, i, 0)), full(T, QK_PAD), full(T, A_WIDTH)]
    args = [q, k, v]
    if has_ctx:
        P = ctx_kv[0].shape[1]
        in_specs += [full(P, QK_PAD), full(P, A_WIDTH)]
        args += list(ctx_kv)
    return pl.pallas_call(
        functools.partial(_attn_kernel, has_ctx),
        grid=(B, T // tq),
        in_specs=in_specs,
        out_specs=pl.BlockSpec((None, tq, A_WIDTH), lambda b, i: (b, i, 0)),
        out_shape=jax.ShapeDtypeStruct((B, T, A_WIDTH), BF16),
        compiler_params=_params(("parallel", "parallel")),
        name="attn_latent" if has_ctx else "attn_context",
    )(*args)


def _post_kernel(x_ref, hm_ref, ha_ref, mod_ref, g2_ref, wout_ref, wup_ref, wdown_ref, y_ref):
    mod = mod_ref[...]
    gate1 = mod[:, 2 * D_MODEL:3 * D_MODEL]
    sh2 = mod[:, 3 * D_MODEL:4 * D_MODEL]
    sc2 = mod[:, 4 * D_MODEL:5 * D_MODEL]
    gate2 = mod[:, 5 * D_MODEL:6 * D_MODEL]
    mix = jnp.concatenate([hm_ref[...], ha_ref[...]], axis=-1)
    x1 = x_ref[...] + gate1 * _dot(mix, wout_ref[...])
    h2 = (_rms(x1, g2_ref[...]) * (1.0 + sc2) + sh2).astype(BF16)
    acc = jnp.zeros_like(x1)
    for c in range(D_FF // FF_TILE):
        sl = slice(c * FF_TILE, (c + 1) * FF_TILE)
        u = jnp.maximum(_dot(h2, wup_ref[:, sl]), 0.0)
        acc = acc + _dot((u * u).astype(BF16), wdown_ref[sl, :])
    y_ref[...] = x1 + gate2 * acc


def _post(x, hm, ha, mod3, mod_row0, mod_per_batch, wts):
    shape = x.shape
    if not mod_per_batch:
        x, hm, ha = (a.reshape(1, -1, a.shape[-1]) for a in (x, hm, ha))
    B, T, _ = x.shape
    tm = POST_TILE
    tok = lambda w: pl.BlockSpec((None, tm, w), lambda b, i: (b, i, 0))
    return _post_call(x, hm, ha, mod3, mod_row0, mod_per_batch, wts, B, T, tm, tok).reshape(shape)


def _post_call(x, hm, ha, mod3, mod_row0, mod_per_batch, wts, B, T, tm, tok):
    return pl.pallas_call(
        _post_kernel,
        grid=(B, T // tm),
        in_specs=[tok(D_MODEL), tok(M_WIDTH), tok(A_WIDTH),
                  pl.BlockSpec((None, 1, 6 * D_MODEL), lambda b, i: (mod_row0 + b * mod_per_batch, 0, 0)),
                  _const_spec((1, D_MODEL)),
                  _const_spec((M_WIDTH + A_WIDTH, D_MODEL)),
                  _const_spec((D_MODEL, D_FF)),
                  _const_spec((D_FF, D_MODEL))],
        out_specs=tok(D_MODEL),
        out_shape=jax.ShapeDtypeStruct((B, T, D_MODEL), F32),
        compiler_params=_params(("parallel", "parallel")),
        name="post",
    )(x, hm, ha, mod3, wts["g2"], wts["w_out"], wts["w_up"], wts["w_down"])


def _prepare_weights(norm1_g, norm2_g, w_in, mlstm_gate_b, q_lora_g, kv_lora_g, w_q_up, w_kv_up,
                     q_head_g, k_head_g, w_out, w_mlp_up, w_mlp_down):
    o_g = 4 * M_WIDTH
    o_q = o_g + N_GATES
    o_kv = o_q + Q_LORA
    o_kr = o_kv + KV_LORA
    half = A_ROPE // 2
    n_dh = 2 * M_HEADS
    wt = w_in.T.astype(BF16)
    w_gate = wt[o_g:o_q].reshape(2, 2, M_HEADS, D_MODEL)
    bias = mlstm_gate_b.reshape(2, 2, M_HEADS)

    def gate_tile(which, rope_rows):
        return jnp.concatenate([
            w_gate[:, which].reshape(n_dh, D_MODEL), jnp.zeros((A_NOPE - n_dh, D_MODEL), BF16),
            rope_rows, jnp.zeros((LANES - A_QK, D_MODEL), BF16)], axis=0)

    def rot_partner(a):
        z = jnp.zeros(a.shape[:-1] + (A_NOPE,), a.dtype)
        return jnp.concatenate([z, a[..., A_NOPE + half:A_QK], a[..., A_NOPE:A_NOPE + half]], axis=-1)

    pad_tile = lambda a: jnp.pad(a, [(0, 0)] * (a.ndim - 1) + [(0, HEAD_PAD - A_QK)])
    w_kr = wt[o_kr:o_kr + A_ROPE]
    w_kr_partner = jnp.concatenate([w_kr[half:], w_kr[:half]], axis=0)
    w_lat = jnp.concatenate([wt[o_q:o_kr], gate_tile(0, w_kr), gate_tile(1, w_kr_partner)], axis=0)
    gate_bias = jnp.pad(jnp.stack([bias[:, 0, :].reshape(n_dh), bias[:, 1, :].reshape(n_dh)], axis=0),
                        ((0, 0), (0, LANES - n_dh)))
    w_q3 = w_q_up.reshape(Q_LORA, A_HEADS, A_QK)
    w_q = pad_tile(w_q3).reshape(Q_LORA, QK_PAD)
    w_q_partner = pad_tile(rot_partner(w_q3)).reshape(Q_LORA, QK_PAD)
    w_kv3 = w_kv_up.reshape(KV_LORA, A_HEADS, A_NOPE + A_VDIM)
    w_k = jnp.pad(w_kv3[:, :, :A_NOPE], ((0, 0), (0, 0), (0, HEAD_PAD - A_NOPE)))
    w_v = w_kv3[:, :, A_NOPE:]
    w_kv = jnp.concatenate([w_k.reshape(KV_LORA, QK_PAD), w_v.reshape(KV_LORA, A_WIDTH)], axis=1)
    pad_head = lambda g: jnp.stack([pad_tile(g), pad_tile(rot_partner(g))], axis=0)
    return {
        "g1": norm1_g.reshape(1, D_MODEL),
        "g2": norm2_g.reshape(1, D_MODEL),
        "w_main": jnp.concatenate([wt[0:M_WIDTH], wt[2 * M_WIDTH:o_g]], axis=0),
        "w_kt": wt[M_WIDTH:2 * M_WIDTH],
        "w_lat": w_lat,
        "gate_bias": gate_bias,
        "q_lora_g": q_lora_g.reshape(1, Q_LORA),
        "kv_lora_g": kv_lora_g.reshape(1, KV_LORA),
        "w_q": w_q.astype(BF16),
        "w_q_rot": jnp.concatenate([w_q, w_q_partner], axis=1).astype(BF16),
        "w_kv": w_kv.astype(BF16),
        "q_head_g": pad_head(q_head_g),
        "k_head_g": pad_head(k_head_g),
        "w_out": w_out.astype(BF16),
        "w_up": w_mlp_up.astype(BF16),
        "w_down": w_mlp_down.astype(BF16),
    }


def _rope_tables(T):
    rows = T // GRID_W
    row = jnp.repeat(jnp.arange(rows, dtype=F32), GRID_W)
    col = jnp.tile(jnp.arange(GRID_W, dtype=F32), rows)
    half = A_ROPE // 2
    inv = ROPE_BASE ** (-jnp.arange(0, half, 2, dtype=F32) / half)
    ang = jnp.concatenate([row[:, None] * inv, col[:, None] * inv], axis=-1)
    cos, sin = jnp.cos(ang), jnp.sin(ang)
    ones = jnp.ones((T, A_NOPE), F32)
    z = lambda w: jnp.zeros((T, w), F32)
    tail = LANES - A_QK
    cos_t = jnp.concatenate([ones, cos, cos, z(tail)], axis=1)
    sin_t = jnp.concatenate([z(A_NOPE), -sin, sin, z(tail)], axis=1)
    return jnp.stack([cos_t, sin_t], axis=0)


def _layer_pass(x, mod3, mod_row0, mod_per_batch, wts, norm_g, rope_tab, init_state, ctx_kv, is_context):
    pre = _pre(x, mod3, mod_row0, mod_per_batch, wts, rope_tab, emit_cache=is_context)
    mq, mkt, mv, mo, stats, q, k, v = pre[:8]
    ml = _mlstm(mq, mkt, mv, mo, stats, norm_g, init_state, emit_state=is_context,
                heads=M_HEADS if is_context else 1)
    ha = _attn(q, k, v, ctx_kv)
    y = _post(x, ml[0], ha, mod3, mod_row0, mod_per_batch, wts)
    return y, pre[8:], ml[1:]


def kernel(x_prompt, x_sample, cache_mla_ckv, cache_mla_krope, state_mlstm_C, state_mlstm_n, state_mlstm_m,
           c, c_ctx, norm1_g, norm2_g, w_ada, b_ada, w_in, mlstm_gate_b, mlstm_norm_g,
           q_lora_g, kv_lora_g, w_q_up, w_kv_up, q_head_g, k_head_g, w_out, w_mlp_up, w_mlp_down):
    depth = w_in.shape[0]
    Bd = x_sample.shape[0]
    cond8 = jnp.concatenate([c_ctx[None, :], c, jnp.zeros((8 - 1 - Bd, D_MODEL), F32)], axis=0)
    rope_tab = _rope_tables(x_sample.shape[1])

    y, z = x_prompt, x_sample
    ckvs, kropes, Cs, ns, ms = [], [], [], [], []
    for l in range(depth):
        wts = _prepare_weights(norm1_g[l], norm2_g[l], w_in[l], mlstm_gate_b[l], q_lora_g[l], kv_lora_g[l],
                               w_q_up[l], w_kv_up[l], q_head_g[l], k_head_g[l], w_out[l], w_mlp_up[l],
                               w_mlp_down[l])
        norm_g = mlstm_norm_g[l].reshape(M_HEADS, 1, M_HEAD_DIM)
        mod3 = _ada(cond8, w_ada[l], b_ada[l]).reshape(8, 1, 6 * D_MODEL)

        y, (ckv, krope), (C_new, n_new, m_new) = _layer_pass(
            y, mod3, 0, 0, wts, norm_g, None, None, None, True)
        ckvs.append(ckv)
        kropes.append(krope)
        Cs.append(C_new)
        ns.append(n_new[:, :, :, 0, :])
        ms.append(m_new[:, :, :, 0].transpose(0, 2, 1))

        init_state = (state_mlstm_C[:, l],
                      state_mlstm_n[:, l][:, :, :, None, :],
                      jnp.broadcast_to(state_mlstm_m[:, l].transpose(0, 2, 1)[..., None],
                                       (Bd, M_HEADS, 2, LANES)))
        krope_placed = jnp.pad(cache_mla_krope[:, l], ((0, 0), (0, 0), (A_NOPE, LANES - A_QK)))
        ctx_kv = _ctxkv(cache_mla_ckv[:, l], krope_placed, wts)
        z, _, _ = _layer_pass(z, mod3, 1, 1, wts, norm_g, rope_tab, init_state, ctx_kv, False)

    return (y, z, jnp.stack(ckvs, axis=1), jnp.stack(kropes, axis=1), jnp.stack(Cs, axis=1),
            jnp.stack(ns, axis=1), jnp.stack(ms, axis=1))
```

```python
import functools

import jax
import jax.numpy as jnp
import numpy as np
from jax import lax
from jax.experimental import pallas as pl
from jax.experimental.pallas import tpu as pltpu

F32 = jnp.float32
BF16 = jnp.bfloat16

D_MODEL = 1024
GRID_W = 64
M_HEADS = 4
M_HEAD_DIM = 128
M_WIDTH = M_HEADS * M_HEAD_DIM
M_BLOCK = 256
A_HEADS = 8
A_NOPE = 64
A_ROPE = 32
A_QK = A_NOPE + A_ROPE
A_VDIM = 64
A_WIDTH = A_HEADS * A_VDIM
Q_LORA = 384
KV_LORA = 256
ROPE_BASE = 10000.0
D_FF = 4 * D_MODEL
EPS = 1e-6

LANES = 128
SUBLANES = 8
LOG2E = 1.4426950408889634
HEAD_PAD = LANES
QK_PAD = A_HEADS * HEAD_PAD
N_GATES = 4 * M_HEADS
LAT_WIDTH = Q_LORA + KV_LORA + 2 * LANES
VMEM_LIMIT = 56 * 1024 * 1024

TOKEN_TILE = 256
POST_TILE = 512
MLSTM_CONTEXT_SEQS = 2
ATTN_CONTEXT_SEQS = 4
Q_TILE = 512
ADA_TILE_N = 1536
FF_TILE = 1024


def _dot(a, b):
    return jnp.dot(a, b, preferred_element_type=F32)


def _dot_nt(a, b):
    return lax.dot_general(a, b, (((1,), (1,)), ((), ())), preferred_element_type=F32)


def _dot_tn(a, b):
    return lax.dot_general(a, b, (((0,), (0,)), ((), ())), preferred_element_type=F32)


def _rms(x, g):
    y = x * lax.rsqrt(jnp.mean(x * x, axis=-1, keepdims=True) + EPS)
    return y * g


def _params(sem):
    return pltpu.CompilerParams(dimension_semantics=sem, vmem_limit_bytes=VMEM_LIMIT)


def _const_spec(shape):
    zeros = (0,) * len(shape)
    return pl.BlockSpec(shape, lambda *_: zeros, pipeline_mode=pl.Buffered(1))


def _ada_kernel(cond_ref, w_ref, b_ref, o_ref):
    c = cond_ref[...]
    s = (c * jax.nn.sigmoid(c)).astype(BF16)
    o_ref[...] = _dot(s, w_ref[...].astype(BF16)) + b_ref[...]


def _ada(cond8, w_ada, b_ada):
    n = w_ada.shape[1]
    return pl.pallas_call(
        _ada_kernel,
        grid=(n // ADA_TILE_N,),
        in_specs=[
            pl.BlockSpec((8, D_MODEL), lambda j: (0, 0)),
            pl.BlockSpec((D_MODEL, ADA_TILE_N), lambda j: (0, j)),
            pl.BlockSpec((1, ADA_TILE_N), lambda j: (0, j)),
        ],
        out_specs=pl.BlockSpec((8, ADA_TILE_N), lambda j: (0, j)),
        out_shape=jax.ShapeDtypeStruct((8, n), F32),
        compiler_params=_params(("parallel",)),
        name="ada",
    )(cond8, w_ada, b_ada.reshape(1, n))


def _write_heads(src, extra, g_pad, rot, dst_ref):
    for h in range(A_HEADS):
        sl = slice(h * HEAD_PAD, (h + 1) * HEAD_PAD)
        xh = src[:, sl]
        if extra is not None:
            xh = xh + extra
        ss = jnp.sum(xh * xh, axis=-1, keepdims=True) * (1.0 / A_QK)
        r = lax.rsqrt(ss + EPS)
        if rot is None:
            y = xh * r * g_pad
        else:
            partner, cos_g, sin_g = rot
            ph = partner if partner.shape[1] == HEAD_PAD else partner[:, sl]
            y = (xh * cos_g + ph * sin_g) * r
        dst_ref[:, sl] = y.astype(dst_ref.dtype)


def _time_scan(x, op, identity, reverse):
    n = x.shape[0]
    row = lax.broadcasted_iota(jnp.int32, x.shape, 0)
    shift = 1
    while shift < n:
        if shift < SUBLANES:
            if reverse:
                moved = jnp.where(row < n - shift, pltpu.roll(x, n - shift, 0), identity)
            else:
                moved = jnp.where(row >= shift, pltpu.roll(x, shift, 0), identity)
        else:
            fill = jnp.full((shift, x.shape[1]), identity, x.dtype)
            moved = (jnp.concatenate([x[shift:], fill], axis=0) if reverse
                     else jnp.concatenate([fill, x[:n - shift]], axis=0))
        x = op(x, moved)
        shift *= 2
    return x


def _mod_row(mod_ref, row0, per_batch):
    if per_batch:
        return mod_ref[pl.ds(row0 + pl.program_id(0) * per_batch, 1), :]
    return mod_ref[row0:row0 + 1, :]


def _pre_kernel(has_rope, emit_cache, mod_row0, mod_per_batch, *refs):
    (x_ref, mod_ref, g1_ref, wmain_ref, wkt_ref, wlat_ref, gbias_ref, qlg_ref, kvg_ref, wq_ref, wkv_ref,
     qhg_ref, khg_ref) = refs[:13]
    pos = 13
    if has_rope:
        rope_ref = refs[pos]
        pos += 1
    (mq_ref, mkt_ref, mv_ref, mo_ref, stats_ref, q_ref, k_ref, v_ref) = refs[pos:pos + 8]
    pos += 8

    x = x_ref[...]
    mod = _mod_row(mod_ref, mod_row0, mod_per_batch)
    sh1 = mod[:, 0:D_MODEL]
    sc1 = mod[:, D_MODEL:2 * D_MODEL]
    h = _rms(x, g1_ref[...]) * (1.0 + sc1) + sh1
    hb = h.astype(BF16)

    plat = _dot_nt(hb, wlat_ref[...])
    q_lat = plat[:, 0:Q_LORA]
    kv_lat = plat[:, Q_LORA:Q_LORA + KV_LORA]
    tail = plat[:, Q_LORA + KV_LORA:Q_LORA + KV_LORA + LANES]
    tail2 = plat[:, LAT_WIDTH - LANES:LAT_WIDTH]

    lane = lax.broadcasted_iota(jnp.int32, (1, LANES), 1)
    fwd = lane < M_HEADS
    gate_i = tail + gbias_ref[0:1, :]
    gate_f = tail2 + gbias_ref[1:2, :]
    log_f = jnp.minimum(gate_f, 0.0) - jnp.log1p(jnp.exp(-jnp.abs(gate_f)))
    b = jnp.where(fwd, _time_scan(log_f, jnp.add, 0.0, False), _time_scan(log_f, jnp.add, 0.0, True))
    a = gate_i - b
    amax = jnp.where(fwd, _time_scan(a, jnp.maximum, -jnp.inf, False),
                     _time_scan(a, jnp.maximum, -jnp.inf, True))
    stats_ref[0] = b.T[0:2 * M_HEADS, :]
    stats_ref[1] = a.T[0:2 * M_HEADS, :]
    stats_ref[2] = amax.T[0:2 * M_HEADS, :]

    krope_placed = jnp.where((lane >= A_NOPE) & (lane < A_QK), tail, 0.0)
    ckv = _rms(kv_lat, kvg_ref[...])
    qn = _rms(q_lat, qlg_ref[...])
    qf = _dot(qn.astype(BF16), wq_ref[...])
    kvf = _dot(ckv.astype(BF16), wkv_ref[...])
    v_ref[...] = kvf[:, QK_PAD:QK_PAD + A_WIDTH].astype(BF16)
    q_rot = k_rot = None
    qhg = qhg_ref[...] * (A_QK ** -0.5 * LOG2E)
    if has_rope:
        cos_t, sin_t = rope_ref[0], rope_ref[1]
        q_rot = (qf[:, QK_PAD:2 * QK_PAD], cos_t * qhg[0:1, :], sin_t * qhg[1:2, :])
        k_rot = (tail2, cos_t * khg_ref[0:1, :], sin_t * khg_ref[1:2, :])
    _write_heads(qf, None, qhg[0:1, :], q_rot, q_ref)
    _write_heads(kvf, krope_placed, khg_ref[0:1, :], k_rot, k_ref)

    if emit_cache:
        ckv_ref, krope_ref = refs[pos:pos + 2]
        ckv_ref[...] = ckv
        krope_ref[...] = tail[:, A_NOPE:A_QK]

    pm = _dot_nt(hb, wmain_ref[...])
    mq_ref[...] = pm[:, 0:M_WIDTH].astype(BF16)
    mv_ref[...] = pm[:, M_WIDTH:2 * M_WIDTH].astype(BF16)
    mo_ref[...] = pm[:, 2 * M_WIDTH:3 * M_WIDTH]
    mkt_ref[...] = _dot_nt(wkt_ref[...], hb) * (M_HEAD_DIM ** -0.5)


def _pre(x, mod3, mod_row0, mod_per_batch, wts, rope_tab, emit_cache):
    B, T, _ = x.shape
    tm = TOKEN_TILE
    has_rope = rope_tab is not None
    tok = lambda w: pl.BlockSpec((None, tm, w), lambda b, i: (b, i, 0))
    in_specs = [
        tok(D_MODEL),
        _const_spec((8, 6 * D_MODEL)),
        _const_spec((1, D_MODEL)),
        _const_spec((3 * M_WIDTH, D_MODEL)),
        _const_spec((M_WIDTH, D_MODEL)),
        _const_spec((LAT_WIDTH, D_MODEL)),
        _const_spec((2, LANES)),
        _const_spec((1, Q_LORA)),
        _const_spec((1, KV_LORA)),
        _const_spec((Q_LORA, 2 * QK_PAD if has_rope else QK_PAD)),
        _const_spec((KV_LORA, QK_PAD + A_WIDTH)),
        _const_spec((2, HEAD_PAD)),
        _const_spec((2, HEAD_PAD)),
    ]
    assert tm == M_BLOCK
    args = [x, mod3, wts["g1"], wts["w_main"], wts["w_kt"], wts["w_lat"], wts["gate_bias"], wts["q_lora_g"],
            wts["kv_lora_g"], wts["w_q_rot"] if has_rope else wts["w_q"], wts["w_kv"],
            wts["q_head_g"], wts["k_head_g"]]
    if has_rope:
        in_specs.append(pl.BlockSpec((2, tm, HEAD_PAD), lambda b, i: (0, i, 0)))
        args.append(rope_tab)
    out_specs = [tok(M_WIDTH),
                 pl.BlockSpec((None, M_WIDTH, tm), lambda b, i: (b, 0, i)),
                 tok(M_WIDTH), tok(M_WIDTH),
                 pl.BlockSpec((None, 3, 2 * M_HEADS, tm), lambda b, i: (b, 0, 0, i)),
                 tok(QK_PAD), tok(QK_PAD), tok(A_WIDTH)]
    out_shape = [
        jax.ShapeDtypeStruct((B, T, M_WIDTH), BF16),
        jax.ShapeDtypeStruct((B, M_WIDTH, T), F32),
        jax.ShapeDtypeStruct((B, T, M_WIDTH), BF16),
        jax.ShapeDtypeStruct((B, T, M_WIDTH), F32),
        jax.ShapeDtypeStruct((B, 3, 2 * M_HEADS, T), F32),
        jax.ShapeDtypeStruct((B, T, QK_PAD), BF16),
        jax.ShapeDtypeStruct((B, T, QK_PAD), BF16),
        jax.ShapeDtypeStruct((B, T, A_WIDTH), BF16),
    ]
    if emit_cache:
        out_specs += [tok(KV_LORA), tok(A_ROPE)]
        out_shape += [jax.ShapeDtypeStruct((B, T, KV_LORA), F32),
                      jax.ShapeDtypeStruct((B, T, A_ROPE), F32)]
    return pl.pallas_call(
        functools.partial(_pre_kernel, has_rope, emit_cache, mod_row0, mod_per_batch),
        grid=(B, T // tm),
        in_specs=in_specs,
        out_specs=out_specs,
        out_shape=out_shape,
        compiler_params=_params(("parallel", "parallel")),
        name="pre_latent" if has_rope else "pre_context",
    )(*args)


def _ctxkv_kernel(ckv_ref, krp_ref, wkv_ref, khg_ref, k_ref, v_ref):
    kvf = _dot(ckv_ref[...].astype(BF16), wkv_ref[...])
    v_ref[...] = kvf[:, QK_PAD:QK_PAD + A_WIDTH].astype(BF16)
    _write_heads(kvf, krp_ref[...], khg_ref[0:1, :], None, k_ref)


def _ctxkv(ckv, krope_placed, wts):
    B, P, _ = ckv.shape
    tok = lambda w: pl.BlockSpec((None, P, w), lambda b: (b, 0, 0))
    return pl.pallas_call(
        _ctxkv_kernel,
        grid=(B,),
        in_specs=[tok(KV_LORA), tok(HEAD_PAD), _const_spec((KV_LORA, QK_PAD + A_WIDTH)),
                  _const_spec((2, HEAD_PAD))],
        out_specs=[tok(QK_PAD), tok(A_WIDTH)],
        out_shape=[jax.ShapeDtypeStruct((B, P, QK_PAD), BF16),
                   jax.ShapeDtypeStruct((B, P, A_WIDTH), BF16)],
        compiler_params=_params(("parallel",)),
        name="ctx_kv",
    )(ckv, krope_placed, wts["w_kv"], wts["k_head_g"])


def _rows_to_lane_broadcast(rows, spread):
    x = jnp.concatenate(rows, axis=0)
    p1 = x.astype(BF16)
    r1 = x - p1.astype(F32)
    p2 = r1.astype(BF16)
    p3 = (r1 - p2.astype(F32)).astype(BF16)
    pad = jnp.zeros((spread.shape[0] - 3 * len(rows), x.shape[1]), BF16)
    return _dot_tn(jnp.concatenate([p1, p2, p3, pad], axis=0), spread)


def _mlstm_gate_rows(b_row, a_row, amax_row, forward, m):
    L = b_row.shape[1]
    last = slice(L - 1, L) if forward else slice(0, 1)
    total = b_row[:, last]
    g_row = jnp.maximum(m, amax_row)
    m_new = total + jnp.maximum(m, amax_row[:, last])
    w_key_row = jnp.exp2((a_row + (total - m_new)) * LOG2E)
    decay = jnp.exp(total + m - m_new)
    return g_row * LOG2E, (b_row + g_row) * LOG2E, a_row * LOG2E, w_key_row, decay, m_new


def _mlstm_block(s_raw, q, kt, v_aug, g2, mt2, a2_row, w_key_row, decay, allow, CN, m):
    w_intra = jnp.exp2(jnp.where(allow, a2_row - jnp.concatenate([g2, g2], axis=1), -jnp.inf))
    w_inter = jnp.exp2(m * LOG2E - g2)
    s = (s_raw * w_intra).astype(BF16)
    nd = _dot(s, v_aug) + jnp.concatenate([w_inter, w_inter], axis=1) * _dot(q, CN.astype(BF16))
    num, den = nd[:, 0:M_HEAD_DIM], nd[:, M_HEAD_DIM:2 * M_HEAD_DIM]
    h = num / jnp.maximum(jnp.abs(den), jnp.exp2(-mt2))
    CN_new = decay * CN + _dot((kt * w_key_row).astype(BF16), v_aug)
    return h, CN_new


def _mlstm_kernel(has_init, emit_state, n_blocks, heads, seqs, *refs):
    q_ref, kt_ref, v_ref, mo_ref, stats_ref, ng_ref, spread_ref = refs[:7]
    pos = 7
    if has_init:
        c0_ref, n0_ref, m0_ref = refs[pos:pos + 3]
        pos += 3
    hm_ref = refs[pos]
    pos += 1
    if emit_state:
        c_ref, n_ref, m_ref = refs[pos:pos + 3]
        pos += 3

    L, Dh = M_BLOCK, M_HEAD_DIM
    t_idx = lax.broadcasted_iota(jnp.int32, (L, L), 0)
    s_idx = lax.broadcasted_iota(jnp.int32, (L, L), 1)
    allow = (s_idx <= t_idx, s_idx >= t_idx)
    spread = spread_ref[...]
    ones = jnp.ones((L, Dh), BF16)

    def lane_broadcast_n(n_row):
        return jnp.broadcast_to(n_row, (Dh, Dh)).T

    def init_state(sq, j, d):
        if has_init:
            return (jnp.concatenate([c0_ref[sq, d, j], lane_broadcast_n(n0_ref[sq, d, j])], axis=1),
                    m0_ref[sq, j, d:d + 1, 0:1])
        return jnp.zeros((Dh, 2 * Dh), F32), jnp.zeros((1, 1), F32)

    def gate_rows(sq, j, c, d, m):
        head = j if heads == M_HEADS else pl.program_id(1)
        r, lanes = pl.ds(d * M_HEADS + head, 1), slice(c * L, (c + 1) * L)
        return _mlstm_gate_rows(stats_ref[sq, 0, r, lanes], stats_ref[sq, 1, r, lanes], stats_ref[sq, 2, r, lanes],
                                d == 0, m)

    def blocks(sq, j, jobs, states):
        loaded, rows6, cols_in = {}, [], []
        for (c, d), (CN, m) in zip(jobs, states):
            rows6.append(gate_rows(sq, j, c, d, m))
            cols_in += [rows6[-1][0], rows6[-1][1]]
            if c not in loaded:
                rows, cols = slice(c * L, (c + 1) * L), slice(j * Dh, (j + 1) * Dh)
                q, kt = q_ref[sq, rows, cols], kt_ref[sq, cols, rows]
                v_aug = jnp.concatenate([v_ref[sq, rows, cols], ones], axis=1)
                loaded[c] = (_dot(q, kt.astype(BF16)), q, kt, v_aug)
        cols_out = _rows_to_lane_broadcast(cols_in, spread)
        hs, new_states = [], []
        for idx, ((c, d), (CN, m)) in enumerate(zip(jobs, states)):
            g2 = cols_out[:, (2 * idx) * LANES:(2 * idx + 1) * LANES]
            mt2 = cols_out[:, (2 * idx + 1) * LANES:(2 * idx + 2) * LANES]
            _, _, a2_row, w_key_row, decay, m_new = rows6[idx]
            h, CN_new = _mlstm_block(*loaded[c], g2, mt2, a2_row, w_key_row, decay, allow[d], CN, m)
            hs.append(h)
            new_states.append((CN_new, m_new))
        return hs, new_states

    def finish(sq, j, rows, hs):
        cols = slice(j * Dh, (j + 1) * Dh)
        hn = _rms(hs, ng_ref[j])
        hm_ref[sq, rows, cols] = (hn * jax.nn.sigmoid(mo_ref[sq, rows, cols])).astype(hm_ref.dtype)

    def emit(sq, j, d, state):
        CN, m = state
        c_ref[sq, d, j] = CN[:, 0:Dh]
        n_ref[sq, d, j] = CN[:, Dh:2 * Dh].T[0:1, :]
        m_ref[sq, j, d:d + 1, :] = jnp.broadcast_to(m, (1, LANES))

    if n_blocks > 1:
        hf_scr, hb_scr = refs[pos:pos + 2]

    for sq, j in [(sq, j) for sq in range(seqs) for j in range(heads)]:
        cols = slice(j * Dh, (j + 1) * Dh)
        states = [init_state(sq, j, 0), init_state(sq, j, 1)]
        if n_blocks == 1:
            (hf, hb), states = blocks(sq, j, [(0, 0), (0, 1)], states)
            finish(sq, j, slice(0, L), hf + hb)
        else:
            for step in range(n_blocks):
                cf, cb = step, n_blocks - 1 - step
                (hf, hb), states = blocks(sq, j, [(cf, 0), (cb, 1)], states)
                hf_scr[cf * L:(cf + 1) * L, cols] = hf
                hb_scr[cb * L:(cb + 1) * L, cols] = hb
            finish(sq, j, slice(None), hf_scr[:, cols] + hb_scr[:, cols])
        if emit_state:
            emit(sq, j, 0, states[0])
            emit(sq, j, 1, states[1])


def _mlstm(mq, mkt, mv, mo, stats, norm_g, init_state, emit_state, heads, seqs):
    B, T, _ = mq.shape
    H, Dh, L = M_HEADS, M_HEAD_DIM, M_BLOCK
    nb = T // L
    w = heads * Dh
    has_init = init_state is not None
    n_rows = 4
    spread = jnp.tile(jnp.repeat(jnp.eye(n_rows, dtype=BF16), LANES, axis=1), (3, 1))
    spread = jnp.pad(spread, ((0, 2 * SUBLANES - 3 * n_rows), (0, 0)))
    tok = pl.BlockSpec((seqs, T, w), lambda b, h: (b, 0, h))
    in_specs = [tok, pl.BlockSpec((seqs, w, T), lambda b, h: (b, h, 0)), tok, tok,
                pl.BlockSpec((seqs, 3, 2 * H, T), lambda b, h: (b, 0, 0, 0)),
                pl.BlockSpec((heads, 1, Dh), lambda b, h: (h, 0, 0)),
                _const_spec((2 * SUBLANES, n_rows * LANES))]
    args = [mq, mkt, mv, mo, stats, norm_g, spread]
    state_specs = [pl.BlockSpec((seqs, 2, heads, Dh, Dh), lambda b, h: (b, 0, h, 0, 0)),
                   pl.BlockSpec((seqs, 2, heads, 1, Dh), lambda b, h: (b, 0, h, 0, 0)),
                   pl.BlockSpec((seqs, heads, 2, LANES), lambda b, h: (b, h, 0, 0))]
    if has_init:
        in_specs += state_specs
        args += list(init_state)
    out_specs = [tok]
    out_shape = [jax.ShapeDtypeStruct((B, T, M_WIDTH), BF16)]
    if emit_state:
        out_specs += state_specs
        out_shape += [jax.ShapeDtypeStruct((B, 2, H, Dh, Dh), F32),
                      jax.ShapeDtypeStruct((B, 2, H, 1, Dh), F32),
                      jax.ShapeDtypeStruct((B, H, 2, LANES), F32)]
    scratch = [] if nb == 1 else [pltpu.VMEM((T, w), F32), pltpu.VMEM((T, w), F32)]
    return pl.pallas_call(
        functools.partial(_mlstm_kernel, has_init, emit_state, nb, heads, seqs),
        grid=(B // seqs, H // heads),
        in_specs=in_specs,
        out_specs=out_specs,
        out_shape=out_shape,
        scratch_shapes=scratch,
        compiler_params=_params(("parallel", "parallel")),
        name="mlstm_latent" if has_init else "mlstm_context",
    )(*args)


def _attn_kernel(has_ctx, seqs, *refs):
    if has_ctx:
        q_ref, k_ref, v_ref, kc_ref, vc_ref, o_ref = refs
    else:
        q_ref, k_ref, v_ref, o_ref = refs
    lane = lax.broadcasted_iota(jnp.int32, (1, LANES), 1)
    ones = lambda n: jnp.ones((n, LANES), BF16)
    for sq, pair in [(sq, pair) for sq in range(seqs) for pair in range(A_HEADS // 2)]:
        vsl = slice(pair * LANES, (pair + 1) * LANES)
        v_aug = jnp.concatenate([v_ref[sq, :, vsl], ones(v_ref.shape[1])], axis=1)
        if has_ctx:
            vc_aug = jnp.concatenate([vc_ref[sq, :, vsl], ones(vc_ref.shape[1])], axis=1)
        outs = []
        for e in range(2):
            hsl = slice((2 * pair + e) * HEAD_PAD, (2 * pair + e + 1) * HEAD_PAD)
            qh = q_ref[sq, :, hsl]
            s = _dot_nt(qh, k_ref[sq, :, hsl])
            mx = jnp.max(s, axis=1, keepdims=True)
            if has_ctx:
                sc = _dot_nt(qh, kc_ref[sq, :, hsl])
                mx = jnp.maximum(mx, jnp.max(sc, axis=1, keepdims=True))
            od = _dot(jnp.exp2(s - mx).astype(BF16), v_aug)
            if has_ctx:
                od = od + _dot(jnp.exp2(sc - mx).astype(BF16), vc_aug)
            outs.append(od[:, 0:LANES] / od[:, LANES:2 * LANES])
        o_ref[sq, :, vsl] = jnp.where(lane < A_VDIM, outs[0], outs[1]).astype(o_ref.dtype)


def _attn(q, k, v, ctx_kv, seqs):
    B, T, _ = q.shape
    tq = min(Q_TILE, T)
    has_ctx = ctx_kv is not None
    full = lambda n, w: pl.BlockSpec((seqs, n, w), lambda b, i: (b, 0, 0))
    in_specs = [pl.BlockSpec((seqs, tq, QK_PAD), lambda b, i: (b, i, 0)), full(T, QK_PAD), full(T, A_WIDTH)]
    args = [q, k, v]
    if has_ctx:
        P = ctx_kv[0].shape[1]
        in_specs += [full(P, QK_PAD), full(P, A_WIDTH)]
        args += list(ctx_kv)
    return pl.pallas_call(
        functools.partial(_attn_kernel, has_ctx, seqs),
        grid=(B // seqs, T // tq),
        in_specs=in_specs,
        out_specs=pl.BlockSpec((seqs, tq, A_WIDTH), lambda b, i: (b, i, 0)),
        out_shape=jax.ShapeDtypeStruct((B, T, A_WIDTH), BF16),
        compiler_params=_params(("parallel", "parallel")),
        name="attn_latent" if has_ctx else "attn_context",
    )(*args)


def _post_kernel(mod_row0, mod_per_batch, x_ref, hm_ref, ha_ref, mod_ref, g2_ref, wout_ref, wup_ref, wdown_ref,
                 y_ref):
    mod = _mod_row(mod_ref, mod_row0, mod_per_batch)
    gate1 = mod[:, 2 * D_MODEL:3 * D_MODEL]
    sh2 = mod[:, 3 * D_MODEL:4 * D_MODEL]
    sc2 = mod[:, 4 * D_MODEL:5 * D_MODEL]
    gate2 = mod[:, 5 * D_MODEL:6 * D_MODEL]
    mix = jnp.concatenate([hm_ref[...], ha_ref[...]], axis=-1)
    x1 = x_ref[...] + gate1 * _dot(mix, wout_ref[...])
    h2 = (_rms(x1, g2_ref[...]) * (1.0 + sc2) + sh2).astype(BF16)
    acc = jnp.zeros_like(x1)
    for c in range(D_FF // FF_TILE):
        sl = slice(c * FF_TILE, (c + 1) * FF_TILE)
        u = jnp.maximum(_dot(h2, wup_ref[:, sl]), 0.0)
        acc = acc + _dot((u * u).astype(BF16), wdown_ref[sl, :])
    y_ref[...] = x1 + gate2 * acc


def _post(x, hm, ha, mod3, mod_row0, mod_per_batch, wts):
    shape = x.shape
    if not mod_per_batch:
        x, hm, ha = (a.reshape(1, -1, a.shape[-1]) for a in (x, hm, ha))
    B, T, _ = x.shape
    tm = POST_TILE
    tok = lambda w: pl.BlockSpec((None, tm, w), lambda b, i: (b, i, 0))
    return _post_call(x, hm, ha, mod3, mod_row0, mod_per_batch, wts, B, T, tm, tok).reshape(shape)


def _post_call(x, hm, ha, mod3, mod_row0, mod_per_batch, wts, B, T, tm, tok):
    return pl.pallas_call(
        functools.partial(_post_kernel, mod_row0, mod_per_batch),
        grid=(B, T // tm),
        in_specs=[tok(D_MODEL), tok(M_WIDTH), tok(A_WIDTH),
                  _const_spec((8, 6 * D_MODEL)),
                  _const_spec((1, D_MODEL)),
                  _const_spec((M_WIDTH + A_WIDTH, D_MODEL)),
                  _const_spec((D_MODEL, D_FF)),
                  _const_spec((D_FF, D_MODEL))],
        out_specs=tok(D_MODEL),
        out_shape=jax.ShapeDtypeStruct((B, T, D_MODEL), F32),
        compiler_params=_params(("parallel", "parallel")),
        name="post",
    )(x, hm, ha, mod3, wts["g2"], wts["w_out"], wts["w_up"], wts["w_down"])


def _prepare_weights(norm1_g, norm2_g, w_in, mlstm_gate_b, q_lora_g, kv_lora_g, w_q_up, w_kv_up,
                     q_head_g, k_head_g, w_out, w_mlp_up, w_mlp_down):
    o_g = 4 * M_WIDTH
    o_q = o_g + N_GATES
    o_kv = o_q + Q_LORA
    o_kr = o_kv + KV_LORA
    half = A_ROPE // 2
    n_dh = 2 * M_HEADS
    wt = w_in.T.astype(BF16)
    w_gate = wt[o_g:o_q].reshape(2, 2, M_HEADS, D_MODEL)
    bias = mlstm_gate_b.reshape(2, 2, M_HEADS)

    def gate_tile(which, rope_rows):
        return jnp.concatenate([
            w_gate[:, which].reshape(n_dh, D_MODEL), jnp.zeros((A_NOPE - n_dh, D_MODEL), BF16),
            rope_rows, jnp.zeros((LANES - A_QK, D_MODEL), BF16)], axis=0)

    def rot_partner(a):
        z = jnp.zeros(a.shape[:-1] + (A_NOPE,), a.dtype)
        return jnp.concatenate([z, a[..., A_NOPE + half:A_QK], a[..., A_NOPE:A_NOPE + half]], axis=-1)

    pad_tile = lambda a: jnp.pad(a, [(0, 0)] * (a.ndim - 1) + [(0, HEAD_PAD - A_QK)])
    w_kr = wt[o_kr:o_kr + A_ROPE]
    w_kr_partner = jnp.concatenate([w_kr[half:], w_kr[:half]], axis=0)
    w_lat = jnp.concatenate([wt[o_q:o_kr], gate_tile(0, w_kr), gate_tile(1, w_kr_partner)], axis=0)
    gate_bias = jnp.pad(jnp.stack([bias[:, 0, :].reshape(n_dh), bias[:, 1, :].reshape(n_dh)], axis=0),
                        ((0, 0), (0, LANES - n_dh)))
    w_q3 = w_q_up.reshape(Q_LORA, A_HEADS, A_QK)
    w_q = pad_tile(w_q3).reshape(Q_LORA, QK_PAD)
    w_q_partner = pad_tile(rot_partner(w_q3)).reshape(Q_LORA, QK_PAD)
    w_kv3 = w_kv_up.reshape(KV_LORA, A_HEADS, A_NOPE + A_VDIM)
    w_k = jnp.pad(w_kv3[:, :, :A_NOPE], ((0, 0), (0, 0), (0, HEAD_PAD - A_NOPE)))
    w_v = w_kv3[:, :, A_NOPE:]
    w_kv = jnp.concatenate([w_k.reshape(KV_LORA, QK_PAD), w_v.reshape(KV_LORA, A_WIDTH)], axis=1)
    pad_head = lambda g: jnp.stack([pad_tile(g), pad_tile(rot_partner(g))], axis=0)
    return {
        "g1": norm1_g.reshape(1, D_MODEL),
        "g2": norm2_g.reshape(1, D_MODEL),
        "w_main": jnp.concatenate([wt[0:M_WIDTH], wt[2 * M_WIDTH:o_g]], axis=0),
        "w_kt": wt[M_WIDTH:2 * M_WIDTH],
        "w_lat": w_lat,
        "gate_bias": gate_bias,
        "q_lora_g": q_lora_g.reshape(1, Q_LORA),
        "kv_lora_g": kv_lora_g.reshape(1, KV_LORA),
        "w_q": w_q.astype(BF16),
        "w_q_rot": jnp.concatenate([w_q, w_q_partner], axis=1).astype(BF16),
        "w_kv": w_kv.astype(BF16),
        "q_head_g": pad_head(q_head_g),
        "k_head_g": pad_head(k_head_g),
        "w_out": w_out.astype(BF16),
        "w_up": w_mlp_up.astype(BF16),
        "w_down": w_mlp_down.astype(BF16),
    }


def _rope_tables(T):
    rows = T // GRID_W
    row = np.repeat(np.arange(rows, dtype=np.float32), GRID_W)
    col = np.tile(np.arange(GRID_W, dtype=np.float32), rows)
    half = A_ROPE // 2
    inv = (np.float32(ROPE_BASE) ** (-np.arange(0, half, 2, dtype=np.float32) / np.float32(half))).astype(np.float32)
    ang = np.concatenate([row[:, None] * inv, col[:, None] * inv], axis=-1)
    cos, sin = np.cos(ang), np.sin(ang)
    ones = np.ones((T, A_NOPE), np.float32)
    z = lambda w: np.zeros((T, w), np.float32)
    tail = LANES - A_QK
    cos_t = np.concatenate([ones, cos, cos, z(tail)], axis=1)
    sin_t = np.concatenate([z(A_NOPE), -sin, sin, z(tail)], axis=1)
    return jnp.asarray(np.stack([cos_t, sin_t], axis=0).astype(np.float32))


def _layer_pass(x, mod3, mod_row0, mod_per_batch, wts, norm_g, rope_tab, init_state, ctx_kv, is_context):
    pre = _pre(x, mod3, mod_row0, mod_per_batch, wts, rope_tab, emit_cache=is_context)
    mq, mkt, mv, mo, stats, q, k, v = pre[:8]
    ml = _mlstm(mq, mkt, mv, mo, stats, norm_g, init_state, emit_state=is_context,
                heads=M_HEADS if is_context else 1, seqs=MLSTM_CONTEXT_SEQS if is_context else 1)
    ha = _attn(q, k, v, ctx_kv, seqs=ATTN_CONTEXT_SEQS if is_context else 1)
    y = _post(x, ml[0], ha, mod3, mod_row0, mod_per_batch, wts)
    return y, pre[8:], ml[1:]


def kernel(x_prompt, x_sample, cache_mla_ckv, cache_mla_krope, state_mlstm_C, state_mlstm_n, state_mlstm_m,
           c, c_ctx, norm1_g, norm2_g, w_ada, b_ada, w_in, mlstm_gate_b, mlstm_norm_g,
           q_lora_g, kv_lora_g, w_q_up, w_kv_up, q_head_g, k_head_g, w_out, w_mlp_up, w_mlp_down):
    depth = w_in.shape[0]
    Bd = x_sample.shape[0]
    cond8 = jnp.concatenate([c_ctx[None, :], c, jnp.zeros((8 - 1 - Bd, D_MODEL), F32)], axis=0)
    rope_tab = _rope_tables(x_sample.shape[1])

    y, z = x_prompt, x_sample
    ckvs, kropes, Cs, ns, ms = [], [], [], [], []
    for l in range(depth):
        wts = _prepare_weights(norm1_g[l], norm2_g[l], w_in[l], mlstm_gate_b[l], q_lora_g[l], kv_lora_g[l],
                               w_q_up[l], w_kv_up[l], q_head_g[l], k_head_g[l], w_out[l], w_mlp_up[l],
                               w_mlp_down[l])
        norm_g = mlstm_norm_g[l].reshape(M_HEADS, 1, M_HEAD_DIM)
        mod3 = _ada(cond8, w_ada[l], b_ada[l])
        y, (ckv, krope), (C_new, n_new, m_new) = _layer_pass(
            y, mod3, 0, 0, wts, norm_g, None, None, None, True)
        ckvs.append(ckv)
        kropes.append(krope)
        Cs.append(C_new)
        ns.append(n_new[:, :, :, 0, :])
        ms.append(m_new[:, :, :, 0].transpose(0, 2, 1))

        init_state = (state_mlstm_C[:, l],
                      state_mlstm_n[:, l][:, :, :, None, :],
                      jnp.broadcast_to(state_mlstm_m[:, l].transpose(0, 2, 1)[..., None],
                                       (Bd, M_HEADS, 2, LANES)))
        krope_placed = jnp.pad(cache_mla_krope[:, l], ((0, 0), (0, 0), (A_NOPE, LANES - A_QK)))
        ctx_kv = _ctxkv(cache_mla_ckv[:, l], krope_placed, wts)
        z, _, _ = _layer_pass(z, mod3, 1, 1, wts, norm_g, rope_tab, init_state, ctx_kv, False)

    return (y, z, jnp.stack(ckvs, axis=1), jnp.stack(kropes, axis=1), jnp.stack(Cs, axis=1),
            jnp.stack(ns, axis=1), jnp.stack(ms, axis=1))
```

```python
import functools

import jax
import jax.numpy as jnp
import numpy as np
from jax import lax
from jax.experimental import pallas as pl
from jax.experimental.pallas import tpu as pltpu

F32 = jnp.float32
BF16 = jnp.bfloat16

D_MODEL = 1024
GRID_W = 64
M_HEADS = 4
M_HEAD_DIM = 128
M_WIDTH = M_HEADS * M_HEAD_DIM
M_BLOCK = 256
A_HEADS = 8
A_NOPE = 64
A_ROPE = 32
A_QK = A_NOPE + A_ROPE
A_VDIM = 64
A_WIDTH = A_HEADS * A_VDIM
Q_LORA = 384
KV_LORA = 256
ROPE_BASE = 10000.0
D_FF = 4 * D_MODEL
EPS = 1e-6

LANES = 128
SUBLANES = 8
LOG2E = 1.4426950408889634
HEAD_PAD = LANES
QK_PAD = A_HEADS * HEAD_PAD
N_GATES = 4 * M_HEADS
LAT_WIDTH = Q_LORA + KV_LORA + 2 * LANES
VMEM_LIMIT = 56 * 1024 * 1024

TOKEN_TILE = 512
POST_TILE = 512
MLSTM_CONTEXT_SEQS = 2
ATTN_CONTEXT_SEQS = 4
Q_TILE = 512
ADA_TILE_N = 1536
FF_TILE = 1024


def _dot(a, b):
    return jnp.dot(a, b, preferred_element_type=F32)


def _dot_nt(a, b):
    return lax.dot_general(a, b, (((1,), (1,)), ((), ())), preferred_element_type=F32)


def _dot_tn(a, b):
    return lax.dot_general(a, b, (((0,), (0,)), ((), ())), preferred_element_type=F32)


def _rms(x, g):
    y = x * lax.rsqrt(jnp.mean(x * x, axis=-1, keepdims=True) + EPS)
    return y * g


def _params(sem):
    return pltpu.CompilerParams(dimension_semantics=sem, vmem_limit_bytes=VMEM_LIMIT)


def _const_spec(shape):
    zeros = (0,) * len(shape)
    return pl.BlockSpec(shape, lambda *_: zeros, pipeline_mode=pl.Buffered(1))


def _ada_kernel(cond_ref, w_ref, b_ref, o_ref):
    c = cond_ref[...]
    s = (c * jax.nn.sigmoid(c)).astype(BF16)
    o_ref[...] = _dot(s, w_ref[...].astype(BF16)) + b_ref[...]


def _ada(cond8, w_ada, b_ada):
    n = w_ada.shape[1]
    return pl.pallas_call(
        _ada_kernel,
        grid=(n // ADA_TILE_N,),
        in_specs=[
            pl.BlockSpec((8, D_MODEL), lambda j: (0, 0)),
            pl.BlockSpec((D_MODEL, ADA_TILE_N), lambda j: (0, j)),
            pl.BlockSpec((1, ADA_TILE_N), lambda j: (0, j)),
        ],
        out_specs=pl.BlockSpec((8, ADA_TILE_N), lambda j: (0, j)),
        out_shape=jax.ShapeDtypeStruct((8, n), F32),
        compiler_params=_params(("parallel",)),
        name="ada",
    )(cond8, w_ada, b_ada.reshape(1, n))


def _write_heads(src, extra, g_pad, rot, dst_ref):
    for h in range(A_HEADS):
        sl = slice(h * HEAD_PAD, (h + 1) * HEAD_PAD)
        xh = src[:, sl]
        if extra is not None:
            xh = xh + extra
        ss = jnp.sum(xh * xh, axis=-1, keepdims=True) * (1.0 / A_QK)
        r = lax.rsqrt(ss + EPS)
        if rot is None:
            y = xh * r * g_pad
        else:
            partner, cos_g, sin_g = rot
            ph = partner if partner.shape[1] == HEAD_PAD else partner[:, sl]
            y = (xh * cos_g + ph * sin_g) * r
        dst_ref[:, sl] = y.astype(dst_ref.dtype)


def _time_scan(x, op, identity, reverse):
    n = x.shape[0]
    row = lax.broadcasted_iota(jnp.int32, x.shape, 0)
    shift = 1
    while shift < n:
        if shift < SUBLANES:
            if reverse:
                moved = jnp.where(row < n - shift, pltpu.roll(x, n - shift, 0), identity)
            else:
                moved = jnp.where(row >= shift, pltpu.roll(x, shift, 0), identity)
        else:
            fill = jnp.full((shift, x.shape[1]), identity, x.dtype)
            moved = (jnp.concatenate([x[shift:], fill], axis=0) if reverse
                     else jnp.concatenate([fill, x[:n - shift]], axis=0))
        x = op(x, moved)
        shift *= 2
    return x


def _mod_row(mod_ref, row0, per_batch):
    if per_batch:
        return mod_ref[pl.ds(row0 + pl.program_id(0) * per_batch, 1), :]
    return mod_ref[row0:row0 + 1, :]


def _pre_kernel(has_rope, emit_cache, mod_row0, mod_per_batch, *refs):
    (x_ref, mod_ref, g1_ref, wmain_ref, wkt_ref, wlat_ref, gbias_ref, qlg_ref, kvg_ref, wq_ref, wkv_ref,
     qhg_ref, khg_ref) = refs[:13]
    pos = 13
    if has_rope:
        rope_ref = refs[pos]
        pos += 1
    (mq_ref, mkt_ref, mv_ref, mo_ref, stats_ref, q_ref, k_ref, v_ref) = refs[pos:pos + 8]
    pos += 8

    x = x_ref[...]
    mod = _mod_row(mod_ref, mod_row0, mod_per_batch)
    sh1 = mod[:, 0:D_MODEL]
    sc1 = mod[:, D_MODEL:2 * D_MODEL]
    h = _rms(x, g1_ref[...]) * (1.0 + sc1) + sh1
    hb = h.astype(BF16)

    plat = _dot_nt(hb, wlat_ref[...])
    q_lat = plat[:, 0:Q_LORA]
    kv_lat = plat[:, Q_LORA:Q_LORA + KV_LORA]
    tail = plat[:, Q_LORA + KV_LORA:Q_LORA + KV_LORA + LANES]
    tail2 = plat[:, LAT_WIDTH - LANES:LAT_WIDTH]

    lane = lax.broadcasted_iota(jnp.int32, (1, LANES), 1)
    fwd = lane < M_HEADS
    gate_i = tail + gbias_ref[0:1, :]
    gate_f = tail2 + gbias_ref[1:2, :]
    log_f = jnp.minimum(gate_f, 0.0) - jnp.log1p(jnp.exp(-jnp.abs(gate_f)))
    def block_scan(v, op, identity):
        parts = [v[r:r + M_BLOCK] for r in range(0, v.shape[0], M_BLOCK)]
        return jnp.where(fwd, jnp.concatenate([_time_scan(p, op, identity, False) for p in parts], axis=0),
                         jnp.concatenate([_time_scan(p, op, identity, True) for p in parts], axis=0))

    b = block_scan(log_f, jnp.add, 0.0)
    a = gate_i - b
    amax = block_scan(a, jnp.maximum, -jnp.inf)
    stats_ref[0] = b.T[0:2 * M_HEADS, :]
    stats_ref[1] = a.T[0:2 * M_HEADS, :]
    stats_ref[2] = amax.T[0:2 * M_HEADS, :]

    krope_placed = jnp.where((lane >= A_NOPE) & (lane < A_QK), tail, 0.0)
    ckv = _rms(kv_lat, kvg_ref[...])
    qn = _rms(q_lat, qlg_ref[...])
    qf = _dot(qn.astype(BF16), wq_ref[...])
    kvf = _dot(ckv.astype(BF16), wkv_ref[...])
    v_ref[...] = kvf[:, QK_PAD:QK_PAD + A_WIDTH].astype(BF16)
    q_rot = k_rot = None
    qhg = qhg_ref[...] * (A_QK ** -0.5 * LOG2E)
    if has_rope:
        cos_t, sin_t = rope_ref[0], rope_ref[1]
        q_rot = (qf[:, QK_PAD:2 * QK_PAD], cos_t * qhg[0:1, :], sin_t * qhg[1:2, :])
        k_rot = (tail2, cos_t * khg_ref[0:1, :], sin_t * khg_ref[1:2, :])
    _write_heads(qf, None, qhg[0:1, :], q_rot, q_ref)
    _write_heads(kvf, krope_placed, khg_ref[0:1, :], k_rot, k_ref)

    if emit_cache:
        ckv_ref, krope_ref = refs[pos:pos + 2]
        ckv_ref[...] = ckv
        krope_ref[...] = tail[:, A_NOPE:A_QK]

    pm = _dot_nt(hb, wmain_ref[...])
    mq_ref[...] = pm[:, 0:M_WIDTH].astype(BF16)
    mv_ref[...] = pm[:, M_WIDTH:2 * M_WIDTH].astype(BF16)
    mo_ref[...] = pm[:, 2 * M_WIDTH:3 * M_WIDTH]
    mkt_ref[...] = _dot_nt(wkt_ref[...], hb) * (M_HEAD_DIM ** -0.5)


def _pre(x, mod3, mod_row0, mod_per_batch, wts, rope_tab, emit_cache):
    shape = x.shape
    tm = TOKEN_TILE
    if shape[1] < tm:
        assert not mod_per_batch and tm % shape[1] == 0
        x = x.reshape(-1, tm, shape[2])
    B, T, _ = x.shape
    has_rope = rope_tab is not None
    tok = lambda w: pl.BlockSpec((None, tm, w), lambda b, i: (b, i, 0))
    in_specs = [
        tok(D_MODEL),
        _const_spec((8, 6 * D_MODEL)),
        _const_spec((1, D_MODEL)),
        _const_spec((3 * M_WIDTH, D_MODEL)),
        _const_spec((M_WIDTH, D_MODEL)),
        _const_spec((LAT_WIDTH, D_MODEL)),
        _const_spec((2, LANES)),
        _const_spec((1, Q_LORA)),
        _const_spec((1, KV_LORA)),
        _const_spec((Q_LORA, 2 * QK_PAD if has_rope else QK_PAD)),
        _const_spec((KV_LORA, QK_PAD + A_WIDTH)),
        _const_spec((2, HEAD_PAD)),
        _const_spec((2, HEAD_PAD)),
    ]
    assert tm % M_BLOCK == 0
    args = [x, mod3, wts["g1"], wts["w_main"], wts["w_kt"], wts["w_lat"], wts["gate_bias"], wts["q_lora_g"],
            wts["kv_lora_g"], wts["w_q_rot"] if has_rope else wts["w_q"], wts["w_kv"],
            wts["q_head_g"], wts["k_head_g"]]
    if has_rope:
        in_specs.append(pl.BlockSpec((2, tm, HEAD_PAD), lambda b, i: (0, i, 0)))
        args.append(rope_tab)
    out_specs = [tok(M_WIDTH),
                 pl.BlockSpec((None, M_WIDTH, tm), lambda b, i: (b, 0, i)),
                 tok(M_WIDTH), tok(M_WIDTH),
                 pl.BlockSpec((None, 3, 2 * M_HEADS, tm), lambda b, i: (b, 0, 0, i)),
                 tok(QK_PAD), tok(QK_PAD), tok(A_WIDTH)]
    out_shape = [
        jax.ShapeDtypeStruct((B, T, M_WIDTH), BF16),
        jax.ShapeDtypeStruct((B, M_WIDTH, T), F32),
        jax.ShapeDtypeStruct((B, T, M_WIDTH), BF16),
        jax.ShapeDtypeStruct((B, T, M_WIDTH), F32),
        jax.ShapeDtypeStruct((B, 3, 2 * M_HEADS, T), F32),
        jax.ShapeDtypeStruct((B, T, QK_PAD), BF16),
        jax.ShapeDtypeStruct((B, T, QK_PAD), BF16),
        jax.ShapeDtypeStruct((B, T, A_WIDTH), BF16),
    ]
    if emit_cache:
        out_specs += [tok(KV_LORA), tok(A_ROPE)]
        out_shape += [jax.ShapeDtypeStruct((B, T, KV_LORA), F32),
                      jax.ShapeDtypeStruct((B, T, A_ROPE), F32)]
    outs = pl.pallas_call(
        functools.partial(_pre_kernel, has_rope, emit_cache, mod_row0, mod_per_batch),
        grid=(B, T // tm),
        in_specs=in_specs,
        out_specs=out_specs,
        out_shape=out_shape,
        compiler_params=_params(("parallel", "parallel")),
        name="pre_latent" if has_rope else "pre_context",
    )(*args)
    time_on_lanes = (1, 4)
    return [o if n in time_on_lanes else o.reshape(shape[:2] + o.shape[2:]) for n, o in enumerate(outs)]


def _ctxkv_kernel(ckv_ref, krp_ref, wkv_ref, khg_ref, k_ref, v_ref):
    kvf = _dot(ckv_ref[...].astype(BF16), wkv_ref[...])
    v_ref[...] = kvf[:, QK_PAD:QK_PAD + A_WIDTH].astype(BF16)
    _write_heads(kvf, krp_ref[...], khg_ref[0:1, :], None, k_ref)


def _ctxkv(ckv, krope_placed, wts):
    B, P, _ = ckv.shape
    tok = lambda w: pl.BlockSpec((None, P, w), lambda b: (b, 0, 0))
    return pl.pallas_call(
        _ctxkv_kernel,
        grid=(B,),
        in_specs=[tok(KV_LORA), tok(HEAD_PAD), _const_spec((KV_LORA, QK_PAD + A_WIDTH)),
                  _const_spec((2, HEAD_PAD))],
        out_specs=[tok(QK_PAD), tok(A_WIDTH)],
        out_shape=[jax.ShapeDtypeStruct((B, P, QK_PAD), BF16),
                   jax.ShapeDtypeStruct((B, P, A_WIDTH), BF16)],
        compiler_params=_params(("parallel",)),
        name="ctx_kv",
    )(ckv, krope_placed, wts["w_kv"], wts["k_head_g"])


def _rows_to_lane_broadcast(rows, spread):
    x = jnp.concatenate(rows, axis=0)
    p1 = x.astype(BF16)
    r1 = x - p1.astype(F32)
    p2 = r1.astype(BF16)
    p3 = (r1 - p2.astype(F32)).astype(BF16)
    pad = jnp.zeros((spread.shape[0] - 3 * len(rows), x.shape[1]), BF16)
    return _dot_tn(jnp.concatenate([p1, p2, p3, pad], axis=0), spread)


def _mlstm_gate_rows(b_row, a_row, amax_row, forward, m):
    L = b_row.shape[1]
    last = slice(L - 1, L) if forward else slice(0, 1)
    total = b_row[:, last]
    g_row = jnp.maximum(m, amax_row)
    m_new = total + jnp.maximum(m, amax_row[:, last])
    w_key_row = jnp.exp2((a_row + (total - m_new)) * LOG2E)
    decay = jnp.exp(total + m - m_new)
    return g_row * LOG2E, (b_row + g_row) * LOG2E, a_row * LOG2E, w_key_row, decay, m_new


def _mlstm_block(s_raw, q, kt, v_aug, g2, mt2, a2_row, w_key_row, decay, allow, CN, m):
    w_intra = jnp.exp2(jnp.where(allow, a2_row - jnp.concatenate([g2, g2], axis=1), -jnp.inf))
    w_inter = jnp.exp2(m * LOG2E - g2)
    s = (s_raw * w_intra).astype(BF16)
    nd = _dot(s, v_aug) + jnp.concatenate([w_inter, w_inter], axis=1) * _dot(q, CN.astype(BF16))
    num, den = nd[:, 0:M_HEAD_DIM], nd[:, M_HEAD_DIM:2 * M_HEAD_DIM]
    h = num / jnp.maximum(jnp.abs(den), jnp.exp2(-mt2))
    CN_new = decay * CN + _dot((kt * w_key_row).astype(BF16), v_aug)
    return h, CN_new


def _mlstm_kernel(has_init, emit_state, n_blocks, heads, seqs, *refs):
    q_ref, kt_ref, v_ref, mo_ref, stats_ref, ng_ref, spread_ref = refs[:7]
    pos = 7
    if has_init:
        c0_ref, n0_ref, m0_ref = refs[pos:pos + 3]
        pos += 3
    hm_ref = refs[pos]
    pos += 1
    if emit_state:
        c_ref, n_ref, m_ref = refs[pos:pos + 3]
        pos += 3

    L, Dh = M_BLOCK, M_HEAD_DIM
    t_idx = lax.broadcasted_iota(jnp.int32, (L, L), 0)
    s_idx = lax.broadcasted_iota(jnp.int32, (L, L), 1)
    allow = (s_idx <= t_idx, s_idx >= t_idx)
    spread = spread_ref[...]
    ones = jnp.ones((L, Dh), BF16)

    def time_lanes(sq, c):
        start = (sq * n_blocks + c) * L
        return slice(start, start + L)

    def lane_broadcast_n(n_row):
        return jnp.broadcast_to(n_row, (Dh, Dh)).T

    def init_state(sq, j, d):
        if has_init:
            return (jnp.concatenate([c0_ref[sq, d, j], lane_broadcast_n(n0_ref[sq, d, j])], axis=1),
                    m0_ref[sq, j, d:d + 1, 0:1])
        return jnp.zeros((Dh, 2 * Dh), F32), jnp.zeros((1, 1), F32)

    def gate_rows(sq, j, c, d, m):
        head = j if heads == M_HEADS else pl.program_id(1)
        r, lanes = pl.ds(d * M_HEADS + head, 1), time_lanes(sq, c)
        return _mlstm_gate_rows(stats_ref[0, r, lanes], stats_ref[1, r, lanes], stats_ref[2, r, lanes], d == 0, m)

    def blocks(sq, j, jobs, states):
        loaded, rows6, cols_in = {}, [], []
        for (c, d), (CN, m) in zip(jobs, states):
            rows6.append(gate_rows(sq, j, c, d, m))
            cols_in += [rows6[-1][0], rows6[-1][1]]
            if c not in loaded:
                rows, cols = slice(c * L, (c + 1) * L), slice(j * Dh, (j + 1) * Dh)
                q, kt = q_ref[sq, rows, cols], kt_ref[cols, time_lanes(sq, c)]
                v_aug = jnp.concatenate([v_ref[sq, rows, cols], ones], axis=1)
                loaded[c] = (_dot(q, kt.astype(BF16)), q, kt, v_aug)
        cols_out = _rows_to_lane_broadcast(cols_in, spread)
        hs, new_states = [], []
        for idx, ((c, d), (CN, m)) in enumerate(zip(jobs, states)):
            g2 = cols_out[:, (2 * idx) * LANES:(2 * idx + 1) * LANES]
            mt2 = cols_out[:, (2 * idx + 1) * LANES:(2 * idx + 2) * LANES]
            _, _, a2_row, w_key_row, decay, m_new = rows6[idx]
            h, CN_new = _mlstm_block(*loaded[c], g2, mt2, a2_row, w_key_row, decay, allow[d], CN, m)
            hs.append(h)
            new_states.append((CN_new, m_new))
        return hs, new_states

    def finish(sq, j, rows, hs):
        cols = slice(j * Dh, (j + 1) * Dh)
        hn = _rms(hs, ng_ref[j])
        hm_ref[sq, rows, cols] = (hn * jax.nn.sigmoid(mo_ref[sq, rows, cols])).astype(hm_ref.dtype)

    def emit(sq, j, d, state):
        CN, m = state
        c_ref[sq, d, j] = CN[:, 0:Dh]
        n_ref[sq, d, j] = CN[:, Dh:2 * Dh].T[0:1, :]
        m_ref[sq, j, d:d + 1, :] = jnp.broadcast_to(m, (1, LANES))

    if n_blocks > 1:
        hf_scr, hb_scr = refs[pos:pos + 2]

    for sq, j in [(sq, j) for sq in range(seqs) for j in range(heads)]:
        cols = slice(j * Dh, (j + 1) * Dh)
        states = [init_state(sq, j, 0), init_state(sq, j, 1)]
        if n_blocks == 1:
            (hf, hb), states = blocks(sq, j, [(0, 0), (0, 1)], states)
            finish(sq, j, slice(0, L), hf + hb)
        else:
            for step in range(n_blocks):
                cf, cb = step, n_blocks - 1 - step
                (hf, hb), states = blocks(sq, j, [(cf, 0), (cb, 1)], states)
                hf_scr[cf * L:(cf + 1) * L, cols] = hf
                hb_scr[cb * L:(cb + 1) * L, cols] = hb
            finish(sq, j, slice(None), hf_scr[:, cols] + hb_scr[:, cols])
        if emit_state:
            emit(sq, j, 0, states[0])
            emit(sq, j, 1, states[1])


def _mlstm(mq, mkt, mv, mo, stats, norm_g, init_state, emit_state, heads, seqs):
    B, T, _ = mq.shape
    H, Dh, L = M_HEADS, M_HEAD_DIM, M_BLOCK
    nb = T // L
    w = heads * Dh
    has_init = init_state is not None
    n_rows = 4
    spread = jnp.tile(jnp.repeat(jnp.eye(n_rows, dtype=BF16), LANES, axis=1), (3, 1))
    spread = jnp.pad(spread, ((0, 2 * SUBLANES - 3 * n_rows), (0, 0)))
    tok = pl.BlockSpec((seqs, T, w), lambda b, h: (b, 0, h))
    assert mkt.shape == (B // seqs, M_WIDTH, seqs * T) and stats.shape == (B // seqs, 3, 2 * H, seqs * T)
    in_specs = [tok, pl.BlockSpec((None, w, seqs * T), lambda b, h: (b, h, 0)), tok, tok,
                pl.BlockSpec((None, 3, 2 * H, seqs * T), lambda b, h: (b, 0, 0, 0)),
                pl.BlockSpec((heads, 1, Dh), lambda b, h: (h, 0, 0)),
                _const_spec((2 * SUBLANES, n_rows * LANES))]
    args = [mq, mkt, mv, mo, stats, norm_g, spread]
    state_specs = [pl.BlockSpec((seqs, 2, heads, Dh, Dh), lambda b, h: (b, 0, h, 0, 0)),
                   pl.BlockSpec((seqs, 2, heads, 1, Dh), lambda b, h: (b, 0, h, 0, 0)),
                   pl.BlockSpec((seqs, heads, 2, LANES), lambda b, h: (b, h, 0, 0))]
    if has_init:
        in_specs += state_specs
        args += list(init_state)
    out_specs = [tok]
    out_shape = [jax.ShapeDtypeStruct((B, T, M_WIDTH), BF16)]
    if emit_state:
        out_specs += state_specs
        out_shape += [jax.ShapeDtypeStruct((B, 2, H, Dh, Dh), F32),
                      jax.ShapeDtypeStruct((B, 2, H, 1, Dh), F32),
                      jax.ShapeDtypeStruct((B, H, 2, LANES), F32)]
    scratch = [] if nb == 1 else [pltpu.VMEM((T, w), F32), pltpu.VMEM((T, w), F32)]
    return pl.pallas_call(
        functools.partial(_mlstm_kernel, has_init, emit_state, nb, heads, seqs),
        grid=(B // seqs, H // heads),
        in_specs=in_specs,
        out_specs=out_specs,
        out_shape=out_shape,
        scratch_shapes=scratch,
        compiler_params=_params(("parallel", "parallel")),
        name="mlstm_latent" if has_init else "mlstm_context",
    )(*args)


def _attn_kernel(has_ctx, seqs, *refs):
    if has_ctx:
        q_ref, k_ref, v_ref, kc_ref, vc_ref, o_ref = refs
    else:
        q_ref, k_ref, v_ref, o_ref = refs
    lane = lax.broadcasted_iota(jnp.int32, (1, LANES), 1)
    ones = lambda n: jnp.ones((n, LANES), BF16)
    for sq, pair in [(sq, pair) for sq in range(seqs) for pair in range(A_HEADS // 2)]:
        vsl = slice(pair * LANES, (pair + 1) * LANES)
        v_aug = jnp.concatenate([v_ref[sq, :, vsl], ones(v_ref.shape[1])], axis=1)
        if has_ctx:
            vc_aug = jnp.concatenate([vc_ref[sq, :, vsl], ones(vc_ref.shape[1])], axis=1)
        outs = []
        for e in range(2):
            hsl = slice((2 * pair + e) * HEAD_PAD, (2 * pair + e + 1) * HEAD_PAD)
            qh = q_ref[sq, :, hsl]
            s = _dot_nt(qh, k_ref[sq, :, hsl])
            mx = jnp.max(s, axis=1, keepdims=True)
            if has_ctx:
                sc = _dot_nt(qh, kc_ref[sq, :, hsl])
                mx = jnp.maximum(mx, jnp.max(sc, axis=1, keepdims=True))
            od = _dot(jnp.exp2(s - mx).astype(BF16), v_aug)
            if has_ctx:
                od = od + _dot(jnp.exp2(sc - mx).astype(BF16), vc_aug)
            outs.append(od[:, 0:LANES] / od[:, LANES:2 * LANES])
        o_ref[sq, :, vsl] = jnp.where(lane < A_VDIM, outs[0], outs[1]).astype(o_ref.dtype)


def _attn(q, k, v, ctx_kv, seqs):
    B, T, _ = q.shape
    tq = min(Q_TILE, T)
    has_ctx = ctx_kv is not None
    full = lambda n, w: pl.BlockSpec((seqs, n, w), lambda b, i: (b, 0, 0))
    in_specs = [pl.BlockSpec((seqs, tq, QK_PAD), lambda b, i: (b, i, 0)), full(T, QK_PAD), full(T, A_WIDTH)]
    args = [q, k, v]
    if has_ctx:
        P = ctx_kv[0].shape[1]
        in_specs += [full(P, QK_PAD), full(P, A_WIDTH)]
        args += list(ctx_kv)
    return pl.pallas_call(
        functools.partial(_attn_kernel, has_ctx, seqs),
        grid=(B // seqs, T // tq),
        in_specs=in_specs,
        out_specs=pl.BlockSpec((seqs, tq, A_WIDTH), lambda b, i: (b, i, 0)),
        out_shape=jax.ShapeDtypeStruct((B, T, A_WIDTH), BF16),
        compiler_params=_params(("parallel", "parallel")),
        name="attn_latent" if has_ctx else "attn_context",
    )(*args)


def _post_kernel(mod_row0, mod_per_batch, x_ref, hm_ref, ha_ref, mod_ref, g2_ref, wout_ref, wup_ref, wdown_ref,
                 y_ref):
    mod = _mod_row(mod_ref, mod_row0, mod_per_batch)
    gate1 = mod[:, 2 * D_MODEL:3 * D_MODEL]
    sh2 = mod[:, 3 * D_MODEL:4 * D_MODEL]
    sc2 = mod[:, 4 * D_MODEL:5 * D_MODEL]
    gate2 = mod[:, 5 * D_MODEL:6 * D_MODEL]
    mix = jnp.concatenate([hm_ref[...], ha_ref[...]], axis=-1)
    x1 = x_ref[...] + gate1 * _dot(mix, wout_ref[...])
    h2 = (_rms(x1, g2_ref[...]) * (1.0 + sc2) + sh2).astype(BF16)
    acc = jnp.zeros_like(x1)
    for c in range(D_FF // FF_TILE):
        sl = slice(c * FF_TILE, (c + 1) * FF_TILE)
        u = jnp.maximum(_dot(h2, wup_ref[:, sl]), 0.0)
        acc = acc + _dot((u * u).astype(BF16), wdown_ref[sl, :])
    y_ref[...] = x1 + gate2 * acc


def _post(x, hm, ha, mod3, mod_row0, mod_per_batch, wts):
    shape = x.shape
    if not mod_per_batch:
        x, hm, ha = (a.reshape(1, -1, a.shape[-1]) for a in (x, hm, ha))
    B, T, _ = x.shape
    tm = POST_TILE
    tok = lambda w: pl.BlockSpec((None, tm, w), lambda b, i: (b, i, 0))
    return _post_call(x, hm, ha, mod3, mod_row0, mod_per_batch, wts, B, T, tm, tok).reshape(shape)


def _post_call(x, hm, ha, mod3, mod_row0, mod_per_batch, wts, B, T, tm, tok):
    return pl.pallas_call(
        functools.partial(_post_kernel, mod_row0, mod_per_batch),
        grid=(B, T // tm),
        in_specs=[tok(D_MODEL), tok(M_WIDTH), tok(A_WIDTH),
                  _const_spec((8, 6 * D_MODEL)),
                  _const_spec((1, D_MODEL)),
                  _const_spec((M_WIDTH + A_WIDTH, D_MODEL)),
                  _const_spec((D_MODEL, D_FF)),
                  _const_spec((D_FF, D_MODEL))],
        out_specs=tok(D_MODEL),
        out_shape=jax.ShapeDtypeStruct((B, T, D_MODEL), F32),
        compiler_params=_params(("parallel", "parallel")),
        name="post",
    )(x, hm, ha, mod3, wts["g2"], wts["w_out"], wts["w_up"], wts["w_down"])


def _prepare_weights(norm1_g, norm2_g, w_in, mlstm_gate_b, q_lora_g, kv_lora_g, w_q_up, w_kv_up,
                     q_head_g, k_head_g, w_out, w_mlp_up, w_mlp_down):
    o_g = 4 * M_WIDTH
    o_q = o_g + N_GATES
    o_kv = o_q + Q_LORA
    o_kr = o_kv + KV_LORA
    half = A_ROPE // 2
    n_dh = 2 * M_HEADS
    wt = w_in.T.astype(BF16)
    w_gate = wt[o_g:o_q].reshape(2, 2, M_HEADS, D_MODEL)
    bias = mlstm_gate_b.reshape(2, 2, M_HEADS)

    def gate_tile(which, rope_rows):
        return jnp.concatenate([
            w_gate[:, which].reshape(n_dh, D_MODEL), jnp.zeros((A_NOPE - n_dh, D_MODEL), BF16),
            rope_rows, jnp.zeros((LANES - A_QK, D_MODEL), BF16)], axis=0)

    def rot_partner(a):
        z = jnp.zeros(a.shape[:-1] + (A_NOPE,), a.dtype)
        return jnp.concatenate([z, a[..., A_NOPE + half:A_QK], a[..., A_NOPE:A_NOPE + half]], axis=-1)

    pad_tile = lambda a: jnp.pad(a, [(0, 0)] * (a.ndim - 1) + [(0, HEAD_PAD - A_QK)])
    w_kr = wt[o_kr:o_kr + A_ROPE]
    w_kr_partner = jnp.concatenate([w_kr[half:], w_kr[:half]], axis=0)
    w_lat = jnp.concatenate([wt[o_q:o_kr], gate_tile(0, w_kr), gate_tile(1, w_kr_partner)], axis=0)
    gate_bias = jnp.pad(jnp.stack([bias[:, 0, :].reshape(n_dh), bias[:, 1, :].reshape(n_dh)], axis=0),
                        ((0, 0), (0, LANES - n_dh)))
    w_q3 = w_q_up.reshape(Q_LORA, A_HEADS, A_QK)
    w_q = pad_tile(w_q3).reshape(Q_LORA, QK_PAD)
    w_q_partner = pad_tile(rot_partner(w_q3)).reshape(Q_LORA, QK_PAD)
    w_kv3 = w_kv_up.reshape(KV_LORA, A_HEADS, A_NOPE + A_VDIM)
    w_k = jnp.pad(w_kv3[:, :, :A_NOPE], ((0, 0), (0, 0), (0, HEAD_PAD - A_NOPE)))
    w_v = w_kv3[:, :, A_NOPE:]
    w_kv = jnp.concatenate([w_k.reshape(KV_LORA, QK_PAD), w_v.reshape(KV_LORA, A_WIDTH)], axis=1)
    pad_head = lambda g: jnp.stack([pad_tile(g), pad_tile(rot_partner(g))], axis=0)
    return {
        "g1": norm1_g.reshape(1, D_MODEL),
        "g2": norm2_g.reshape(1, D_MODEL),
        "w_main": jnp.concatenate([wt[0:M_WIDTH], wt[2 * M_WIDTH:o_g]], axis=0),
        "w_kt": wt[M_WIDTH:2 * M_WIDTH],
        "w_lat": w_lat,
        "gate_bias": gate_bias,
        "q_lora_g": q_lora_g.reshape(1, Q_LORA),
        "kv_lora_g": kv_lora_g.reshape(1, KV_LORA),
        "w_q": w_q.astype(BF16),
        "w_q_rot": jnp.concatenate([w_q, w_q_partner], axis=1).astype(BF16),
        "w_kv": w_kv.astype(BF16),
        "q_head_g": pad_head(q_head_g),
        "k_head_g": pad_head(k_head_g),
        "w_out": w_out.astype(BF16),
        "w_up": w_mlp_up.astype(BF16),
        "w_down": w_mlp_down.astype(BF16),
    }


def _rope_tables(T):
    rows = T // GRID_W
    row = np.repeat(np.arange(rows, dtype=np.float32), GRID_W)
    col = np.tile(np.arange(GRID_W, dtype=np.float32), rows)
    half = A_ROPE // 2
    inv = (np.float32(ROPE_BASE) ** (-np.arange(0, half, 2, dtype=np.float32) / np.float32(half))).astype(np.float32)
    ang = np.concatenate([row[:, None] * inv, col[:, None] * inv], axis=-1)
    cos, sin = np.cos(ang), np.sin(ang)
    ones = np.ones((T, A_NOPE), np.float32)
    z = lambda w: np.zeros((T, w), np.float32)
    tail = LANES - A_QK
    cos_t = np.concatenate([ones, cos, cos, z(tail)], axis=1)
    sin_t = np.concatenate([z(A_NOPE), -sin, sin, z(tail)], axis=1)
    return jnp.asarray(np.stack([cos_t, sin_t], axis=0).astype(np.float32))


def _layer_pass(x, mod3, mod_row0, mod_per_batch, wts, norm_g, rope_tab, init_state, ctx_kv, is_context):
    pre = _pre(x, mod3, mod_row0, mod_per_batch, wts, rope_tab, emit_cache=is_context)
    mq, mkt, mv, mo, stats, q, k, v = pre[:8]
    ml = _mlstm(mq, mkt, mv, mo, stats, norm_g, init_state, emit_state=is_context,
                heads=M_HEADS if is_context else 1, seqs=MLSTM_CONTEXT_SEQS if is_context else 1)
    ha = _attn(q, k, v, ctx_kv, seqs=ATTN_CONTEXT_SEQS if is_context else 1)
    y = _post(x, ml[0], ha, mod3, mod_row0, mod_per_batch, wts)
    return y, pre[8:], ml[1:]


def kernel(x_prompt, x_sample, cache_mla_ckv, cache_mla_krope, state_mlstm_C, state_mlstm_n, state_mlstm_m,
           c, c_ctx, norm1_g, norm2_g, w_ada, b_ada, w_in, mlstm_gate_b, mlstm_norm_g,
           q_lora_g, kv_lora_g, w_q_up, w_kv_up, q_head_g, k_head_g, w_out, w_mlp_up, w_mlp_down):
    depth = w_in.shape[0]
    Bd = x_sample.shape[0]
    cond8 = jnp.concatenate([c_ctx[None, :], c, jnp.zeros((8 - 1 - Bd, D_MODEL), F32)], axis=0)
    rope_tab = _rope_tables(x_sample.shape[1])

    y, z = x_prompt, x_sample
    ckvs, kropes, Cs, ns, ms = [], [], [], [], []
    for l in range(depth):
        wts = _prepare_weights(norm1_g[l], norm2_g[l], w_in[l], mlstm_gate_b[l], q_lora_g[l], kv_lora_g[l],
                               w_q_up[l], w_kv_up[l], q_head_g[l], k_head_g[l], w_out[l], w_mlp_up[l],
                               w_mlp_down[l])
        norm_g = mlstm_norm_g[l].reshape(M_HEADS, 1, M_HEAD_DIM)
        mod3 = _ada(cond8, w_ada[l], b_ada[l])
        y, (ckv, krope), (C_new, n_new, m_new) = _layer_pass(
            y, mod3, 0, 0, wts, norm_g, None, None, None, True)
        ckvs.append(ckv)
        kropes.append(krope)
        Cs.append(C_new)
        ns.append(n_new[:, :, :, 0, :])
        ms.append(m_new[:, :, :, 0].transpose(0, 2, 1))

        init_state = (state_mlstm_C[:, l],
                      state_mlstm_n[:, l][:, :, :, None, :],
                      jnp.broadcast_to(state_mlstm_m[:, l].transpose(0, 2, 1)[..., None],
                                       (Bd, M_HEADS, 2, LANES)))
        krope_placed = jnp.pad(cache_mla_krope[:, l], ((0, 0), (0, 0), (A_NOPE, LANES - A_QK)))
        ctx_kv = _ctxkv(cache_mla_ckv[:, l], krope_placed, wts)
        z, _, _ = _layer_pass(z, mod3, 1, 1, wts, norm_g, rope_tab, init_state, ctx_kv, False)

    return (y, z, jnp.stack(ckvs, axis=1), jnp.stack(kropes, axis=1), jnp.stack(Cs, axis=1),
            jnp.stack(ns, axis=1), jnp.stack(ms, axis=1))
```

```python
import functools

import jax
import jax.numpy as jnp
import numpy as np
from jax import lax
from jax.experimental import pallas as pl
from jax.experimental.pallas import tpu as pltpu

F32 = jnp.float32
BF16 = jnp.bfloat16

D_MODEL = 1024
GRID_W = 64
M_HEADS = 4
M_HEAD_DIM = 128
M_WIDTH = M_HEADS * M_HEAD_DIM
M_BLOCK = 256
A_HEADS = 8
A_NOPE = 64
A_ROPE = 32
A_QK = A_NOPE + A_ROPE
A_VDIM = 64
A_WIDTH = A_HEADS * A_VDIM
Q_LORA = 384
KV_LORA = 256
ROPE_BASE = 10000.0
D_FF = 4 * D_MODEL
EPS = 1e-6

LANES = 128
SUBLANES = 8
LOG2E = 1.4426950408889634
HEAD_PAD = LANES
QK_PAD = A_HEADS * HEAD_PAD
N_GATES = 4 * M_HEADS
LAT_WIDTH = Q_LORA + KV_LORA + 2 * LANES
VMEM_LIMIT = 56 * 1024 * 1024

TOKEN_TILE = 512
POST_TILE = 512
MLSTM_CONTEXT_SEQS = 2
ATTN_CONTEXT_SEQS = 4
Q_TILE = 1024
Q_SUBTILE = 256
ADA_TILE_N = 1536
FF_TILE = 1024


def _dot(a, b):
    return jnp.dot(a, b, preferred_element_type=F32)


def _dot_nt(a, b):
    return lax.dot_general(a, b, (((1,), (1,)), ((), ())), preferred_element_type=F32)


def _dot_tn(a, b):
    return lax.dot_general(a, b, (((0,), (0,)), ((), ())), preferred_element_type=F32)


def _rms(x, g):
    y = x * lax.rsqrt(jnp.mean(x * x, axis=-1, keepdims=True) + EPS)
    return y * g


def _params(sem):
    return pltpu.CompilerParams(dimension_semantics=sem, vmem_limit_bytes=VMEM_LIMIT)


def _const_spec(shape):
    zeros = (0,) * len(shape)
    return pl.BlockSpec(shape, lambda *_: zeros, pipeline_mode=pl.Buffered(1))


def _ada_kernel(cond_ref, w_ref, b_ref, o_ref):
    c = cond_ref[...]
    s = (c * jax.nn.sigmoid(c)).astype(BF16)
    o_ref[...] = _dot(s, w_ref[...].astype(BF16)) + b_ref[...]


def _ada(cond8, w_ada, b_ada):
    n = w_ada.shape[1]
    return pl.pallas_call(
        _ada_kernel,
        grid=(n // ADA_TILE_N,),
        in_specs=[
            pl.BlockSpec((8, D_MODEL), lambda j: (0, 0)),
            pl.BlockSpec((D_MODEL, ADA_TILE_N), lambda j: (0, j)),
            pl.BlockSpec((1, ADA_TILE_N), lambda j: (0, j)),
        ],
        out_specs=pl.BlockSpec((8, ADA_TILE_N), lambda j: (0, j)),
        out_shape=jax.ShapeDtypeStruct((8, n), F32),
        compiler_params=_params(("parallel",)),
        name="ada",
    )(cond8, w_ada, b_ada.reshape(1, n))


def _write_heads(src, extra, g_pad, rot, dst_ref):
    for h in range(A_HEADS):
        sl = slice(h * HEAD_PAD, (h + 1) * HEAD_PAD)
        xh = src[:, sl]
        if extra is not None:
            xh = xh + extra
        ss = jnp.sum(xh * xh, axis=-1, keepdims=True) * (1.0 / A_QK)
        r = lax.rsqrt(ss + EPS)
        if rot is None:
            y = xh * r * g_pad
        else:
            partner, cos_g, sin_g = rot
            ph = partner if partner.shape[1] == HEAD_PAD else partner[:, sl]
            y = (xh * cos_g + ph * sin_g) * r
        dst_ref[:, sl] = y.astype(dst_ref.dtype)


def _time_scan(x, op, identity, reverse):
    n = x.shape[0]
    row = lax.broadcasted_iota(jnp.int32, x.shape, 0)
    shift = 1
    while shift < n:
        if shift < SUBLANES:
            if reverse:
                moved = jnp.where(row < n - shift, pltpu.roll(x, n - shift, 0), identity)
            else:
                moved = jnp.where(row >= shift, pltpu.roll(x, shift, 0), identity)
        else:
            fill = jnp.full((shift, x.shape[1]), identity, x.dtype)
            moved = (jnp.concatenate([x[shift:], fill], axis=0) if reverse
                     else jnp.concatenate([fill, x[:n - shift]], axis=0))
        x = op(x, moved)
        shift *= 2
    return x


def _mod_row(mod_ref, row0, per_batch):
    if per_batch:
        return mod_ref[pl.ds(row0 + pl.program_id(0) * per_batch, 1), :]
    return mod_ref[row0:row0 + 1, :]


def _pre_kernel(has_rope, emit_cache, mod_row0, mod_per_batch, *refs):
    (x_ref, mod_ref, g1_ref, wmain_ref, wkt_ref, wlat_ref, gbias_ref, qlg_ref, kvg_ref, wq_ref, wkv_ref,
     qhg_ref, khg_ref) = refs[:13]
    pos = 13
    if has_rope:
        rope_ref = refs[pos]
        pos += 1
    (mq_ref, mkt_ref, mv_ref, mo_ref, stats_ref, q_ref, k_ref, v_ref) = refs[pos:pos + 8]
    pos += 8

    x = x_ref[...]
    mod = _mod_row(mod_ref, mod_row0, mod_per_batch)
    sh1 = mod[:, 0:D_MODEL]
    sc1 = mod[:, D_MODEL:2 * D_MODEL]
    h = _rms(x, g1_ref[...]) * (1.0 + sc1) + sh1
    hb = h.astype(BF16)

    plat = _dot_nt(hb, wlat_ref[...])
    q_lat = plat[:, 0:Q_LORA]
    kv_lat = plat[:, Q_LORA:Q_LORA + KV_LORA]
    tail = plat[:, Q_LORA + KV_LORA:Q_LORA + KV_LORA + LANES]
    tail2 = plat[:, LAT_WIDTH - LANES:LAT_WIDTH]

    lane = lax.broadcasted_iota(jnp.int32, (1, LANES), 1)
    fwd = lane < M_HEADS
    gate_i = tail + gbias_ref[0:1, :]
    gate_f = tail2 + gbias_ref[1:2, :]
    log_f = jnp.minimum(gate_f, 0.0) - jnp.log1p(jnp.exp(-jnp.abs(gate_f)))
    def block_scan(v, op, identity):
        parts = [v[r:r + M_BLOCK] for r in range(0, v.shape[0], M_BLOCK)]
        return jnp.where(fwd, jnp.concatenate([_time_scan(p, op, identity, False) for p in parts], axis=0),
                         jnp.concatenate([_time_scan(p, op, identity, True) for p in parts], axis=0))

    b = block_scan(log_f, jnp.add, 0.0)
    a = gate_i - b
    amax = block_scan(a, jnp.maximum, -jnp.inf)
    stats_ref[0] = b.T[0:2 * M_HEADS, :]
    stats_ref[1] = a.T[0:2 * M_HEADS, :]
    stats_ref[2] = amax.T[0:2 * M_HEADS, :]

    krope_placed = jnp.where((lane >= A_NOPE) & (lane < A_QK), tail, 0.0)
    ckv = _rms(kv_lat, kvg_ref[...])
    qn = _rms(q_lat, qlg_ref[...])
    qf = _dot(qn.astype(BF16), wq_ref[...])
    kvf = _dot(ckv.astype(BF16), wkv_ref[...])
    v_ref[...] = kvf[:, QK_PAD:QK_PAD + A_WIDTH].astype(BF16)
    q_rot = k_rot = None
    qhg = qhg_ref[...] * (A_QK ** -0.5 * LOG2E)
    if has_rope:
        cos_t, sin_t = rope_ref[0], rope_ref[1]
        q_rot = (qf[:, QK_PAD:2 * QK_PAD], cos_t * qhg[0:1, :], sin_t * qhg[1:2, :])
        k_rot = (tail2, cos_t * khg_ref[0:1, :], sin_t * khg_ref[1:2, :])
    _write_heads(qf, None, qhg[0:1, :], q_rot, q_ref)
    _write_heads(kvf, krope_placed, khg_ref[0:1, :], k_rot, k_ref)

    if emit_cache:
        ckv_ref, krope_ref = refs[pos:pos + 2]
        ckv_ref[...] = ckv
        krope_ref[...] = tail[:, A_NOPE:A_QK]

    pm = _dot_nt(hb, wmain_ref[...])
    mq_ref[...] = pm[:, 0:M_WIDTH].astype(BF16)
    mv_ref[...] = pm[:, M_WIDTH:2 * M_WIDTH].astype(BF16)
    mo_ref[...] = pm[:, 2 * M_WIDTH:3 * M_WIDTH]
    mkt_ref[...] = _dot_nt(wkt_ref[...], hb) * (M_HEAD_DIM ** -0.5)


def _pre(x, mod3, mod_row0, mod_per_batch, wts, rope_tab, emit_cache):
    shape = x.shape
    tm = TOKEN_TILE
    if shape[1] < tm:
        assert not mod_per_batch and tm % shape[1] == 0
        x = x.reshape(-1, tm, shape[2])
    B, T, _ = x.shape
    has_rope = rope_tab is not None
    tok = lambda w: pl.BlockSpec((None, tm, w), lambda b, i: (b, i, 0))
    in_specs = [
        tok(D_MODEL),
        _const_spec((8, 6 * D_MODEL)),
        _const_spec((1, D_MODEL)),
        _const_spec((3 * M_WIDTH, D_MODEL)),
        _const_spec((M_WIDTH, D_MODEL)),
        _const_spec((LAT_WIDTH, D_MODEL)),
        _const_spec((2, LANES)),
        _const_spec((1, Q_LORA)),
        _const_spec((1, KV_LORA)),
        _const_spec((Q_LORA, 2 * QK_PAD if has_rope else QK_PAD)),
        _const_spec((KV_LORA, QK_PAD + A_WIDTH)),
        _const_spec((2, HEAD_PAD)),
        _const_spec((2, HEAD_PAD)),
    ]
    assert tm % M_BLOCK == 0
    args = [x, mod3, wts["g1"], wts["w_main"], wts["w_kt"], wts["w_lat"], wts["gate_bias"], wts["q_lora_g"],
            wts["kv_lora_g"], wts["w_q_rot"] if has_rope else wts["w_q"], wts["w_kv"],
            wts["q_head_g"], wts["k_head_g"]]
    if has_rope:
        in_specs.append(pl.BlockSpec((2, tm, HEAD_PAD), lambda b, i: (0, i, 0)))
        args.append(rope_tab)
    out_specs = [tok(M_WIDTH),
                 pl.BlockSpec((None, M_WIDTH, tm), lambda b, i: (b, 0, i)),
                 tok(M_WIDTH), tok(M_WIDTH),
                 pl.BlockSpec((None, 3, 2 * M_HEADS, tm), lambda b, i: (b, 0, 0, i)),
                 tok(QK_PAD), tok(QK_PAD), tok(A_WIDTH)]
    out_shape = [
        jax.ShapeDtypeStruct((B, T, M_WIDTH), BF16),
        jax.ShapeDtypeStruct((B, M_WIDTH, T), F32),
        jax.ShapeDtypeStruct((B, T, M_WIDTH), BF16),
        jax.ShapeDtypeStruct((B, T, M_WIDTH), F32),
        jax.ShapeDtypeStruct((B, 3, 2 * M_HEADS, T), F32),
        jax.ShapeDtypeStruct((B, T, QK_PAD), BF16),
        jax.ShapeDtypeStruct((B, T, QK_PAD), BF16),
        jax.ShapeDtypeStruct((B, T, A_WIDTH), BF16),
    ]
    if emit_cache:
        out_specs += [tok(KV_LORA), tok(A_ROPE)]
        out_shape += [jax.ShapeDtypeStruct((B, T, KV_LORA), F32),
                      jax.ShapeDtypeStruct((B, T, A_ROPE), F32)]
    outs = pl.pallas_call(
        functools.partial(_pre_kernel, has_rope, emit_cache, mod_row0, mod_per_batch),
        grid=(B, T // tm),
        in_specs=in_specs,
        out_specs=out_specs,
        out_shape=out_shape,
        compiler_params=_params(("parallel", "parallel")),
        name="pre_latent" if has_rope else "pre_context",
    )(*args)
    time_on_lanes = (1, 4)
    return [o if n in time_on_lanes else o.reshape(shape[:2] + o.shape[2:]) for n, o in enumerate(outs)]


def _ctxkv_kernel(ckv_ref, krp_ref, wkv_ref, khg_ref, k_ref, v_ref):
    kvf = _dot(ckv_ref[...].astype(BF16), wkv_ref[...])
    v_ref[...] = kvf[:, QK_PAD:QK_PAD + A_WIDTH].astype(BF16)
    _write_heads(kvf, krp_ref[...], khg_ref[0:1, :], None, k_ref)


def _ctxkv(ckv, krope_placed, wts):
    B, P, _ = ckv.shape
    tok = lambda w: pl.BlockSpec((None, P, w), lambda b: (b, 0, 0))
    return pl.pallas_call(
        _ctxkv_kernel,
        grid=(B,),
        in_specs=[tok(KV_LORA), tok(HEAD_PAD), _const_spec((KV_LORA, QK_PAD + A_WIDTH)),
                  _const_spec((2, HEAD_PAD))],
        out_specs=[tok(QK_PAD), tok(A_WIDTH)],
        out_shape=[jax.ShapeDtypeStruct((B, P, QK_PAD), BF16),
                   jax.ShapeDtypeStruct((B, P, A_WIDTH), BF16)],
        compiler_params=_params(("parallel",)),
        name="ctx_kv",
    )(ckv, krope_placed, wts["w_kv"], wts["k_head_g"])


def _rows_to_lane_broadcast(rows, spread):
    x = jnp.concatenate(rows, axis=0)
    p1 = x.astype(BF16)
    r1 = x - p1.astype(F32)
    p2 = r1.astype(BF16)
    p3 = (r1 - p2.astype(F32)).astype(BF16)
    pad = jnp.zeros((spread.shape[0] - 3 * len(rows), x.shape[1]), BF16)
    return _dot_tn(jnp.concatenate([p1, p2, p3, pad], axis=0), spread)


def _mlstm_gate_rows(b_row, a_row, amax_row, forward, m):
    L = b_row.shape[1]
    last = slice(L - 1, L) if forward else slice(0, 1)
    total = b_row[:, last]
    g_row = jnp.maximum(m, amax_row)
    m_new = total + jnp.maximum(m, amax_row[:, last])
    w_key_row = jnp.exp2((a_row + (total - m_new)) * LOG2E)
    decay = jnp.exp(total + m - m_new)
    return g_row * LOG2E, (b_row + g_row) * LOG2E, a_row * LOG2E, w_key_row, decay, m_new


def _mlstm_block(s_raw, q, kt, v_aug, g2, mt2, a2_row, w_key_row, decay, allow, CN, m):
    w_intra = jnp.exp2(jnp.where(allow, a2_row - jnp.concatenate([g2, g2], axis=1), -jnp.inf))
    w_inter = jnp.exp2(m * LOG2E - g2)
    s = (s_raw * w_intra).astype(BF16)
    nd = _dot(s, v_aug) + jnp.concatenate([w_inter, w_inter], axis=1) * _dot(q, CN.astype(BF16))
    num, den = nd[:, 0:M_HEAD_DIM], nd[:, M_HEAD_DIM:2 * M_HEAD_DIM]
    h = num / jnp.maximum(jnp.abs(den), jnp.exp2(-mt2))
    CN_new = decay * CN + _dot((kt * w_key_row).astype(BF16), v_aug)
    return h, CN_new


def _mlstm_kernel(has_init, emit_state, n_blocks, heads, seqs, *refs):
    q_ref, kt_ref, v_ref, mo_ref, stats_ref, ng_ref, spread_ref = refs[:7]
    pos = 7
    if has_init:
        c0_ref, n0_ref, m0_ref = refs[pos:pos + 3]
        pos += 3
    hm_ref = refs[pos]
    pos += 1
    if emit_state:
        c_ref, n_ref, m_ref = refs[pos:pos + 3]
        pos += 3

    L, Dh = M_BLOCK, M_HEAD_DIM
    t_idx = lax.broadcasted_iota(jnp.int32, (L, L), 0)
    s_idx = lax.broadcasted_iota(jnp.int32, (L, L), 1)
    allow = (s_idx <= t_idx, s_idx >= t_idx)
    spread = spread_ref[...]
    ones = jnp.ones((L, Dh), BF16)

    def time_lanes(sq, c):
        start = (sq * n_blocks + c) * L
        return slice(start, start + L)

    def lane_broadcast_n(n_row):
        return jnp.broadcast_to(n_row, (Dh, Dh)).T

    def init_state(sq, j, d):
        if has_init:
            return (jnp.concatenate([c0_ref[sq, d, j], lane_broadcast_n(n0_ref[sq, d, j])], axis=1),
                    m0_ref[sq, j, d:d + 1, 0:1])
        return jnp.zeros((Dh, 2 * Dh), F32), jnp.zeros((1, 1), F32)

    def gate_rows(sq, j, c, d, m):
        head = j if heads == M_HEADS else pl.program_id(1)
        r, lanes = pl.ds(d * M_HEADS + head, 1), time_lanes(sq, c)
        return _mlstm_gate_rows(stats_ref[0, r, lanes], stats_ref[1, r, lanes], stats_ref[2, r, lanes], d == 0, m)

    def blocks(sq, j, jobs, states):
        loaded, rows6, cols_in = {}, [], []
        for (c, d), (CN, m) in zip(jobs, states):
            rows6.append(gate_rows(sq, j, c, d, m))
            cols_in += [rows6[-1][0], rows6[-1][1]]
            if c not in loaded:
                rows, cols = slice(c * L, (c + 1) * L), slice(j * Dh, (j + 1) * Dh)
                q, kt = q_ref[sq, rows, cols], kt_ref[cols, time_lanes(sq, c)]
                v_aug = jnp.concatenate([v_ref[sq, rows, cols], ones], axis=1)
                loaded[c] = (_dot(q, kt.astype(BF16)), q, kt, v_aug)
        cols_out = _rows_to_lane_broadcast(cols_in, spread)
        hs, new_states = [], []
        for idx, ((c, d), (CN, m)) in enumerate(zip(jobs, states)):
            g2 = cols_out[:, (2 * idx) * LANES:(2 * idx + 1) * LANES]
            mt2 = cols_out[:, (2 * idx + 1) * LANES:(2 * idx + 2) * LANES]
            _, _, a2_row, w_key_row, decay, m_new = rows6[idx]
            h, CN_new = _mlstm_block(*loaded[c], g2, mt2, a2_row, w_key_row, decay, allow[d], CN, m)
            hs.append(h)
            new_states.append((CN_new, m_new))
        return hs, new_states

    def finish(sq, j, rows, hs):
        cols = slice(j * Dh, (j + 1) * Dh)
        hn = _rms(hs, ng_ref[j])
        hm_ref[sq, rows, cols] = (hn * jax.nn.sigmoid(mo_ref[sq, rows, cols])).astype(hm_ref.dtype)

    def emit(sq, j, d, state):
        CN, m = state
        c_ref[sq, d, j] = CN[:, 0:Dh]
        n_ref[sq, d, j] = CN[:, Dh:2 * Dh].T[0:1, :]
        m_ref[sq, j, d:d + 1, :] = jnp.broadcast_to(m, (1, LANES))

    if n_blocks > 1:
        hf_scr, hb_scr = refs[pos:pos + 2]

    for sq, j in [(sq, j) for sq in range(seqs) for j in range(heads)]:
        cols = slice(j * Dh, (j + 1) * Dh)
        states = [init_state(sq, j, 0), init_state(sq, j, 1)]
        if n_blocks == 1:
            (hf, hb), states = blocks(sq, j, [(0, 0), (0, 1)], states)
            finish(sq, j, slice(0, L), hf + hb)
        else:
            for step in range(n_blocks):
                cf, cb = step, n_blocks - 1 - step
                (hf, hb), states = blocks(sq, j, [(cf, 0), (cb, 1)], states)
                hf_scr[cf * L:(cf + 1) * L, cols] = hf
                hb_scr[cb * L:(cb + 1) * L, cols] = hb
            finish(sq, j, slice(None), hf_scr[:, cols] + hb_scr[:, cols])
        if emit_state:
            emit(sq, j, 0, states[0])
            emit(sq, j, 1, states[1])


def _mlstm(mq, mkt, mv, mo, stats, norm_g, init_state, emit_state, heads, seqs):
    B, T, _ = mq.shape
    H, Dh, L = M_HEADS, M_HEAD_DIM, M_BLOCK
    nb = T // L
    w = heads * Dh
    has_init = init_state is not None
    n_rows = 4
    spread = jnp.tile(jnp.repeat(jnp.eye(n_rows, dtype=BF16), LANES, axis=1), (3, 1))
    spread = jnp.pad(spread, ((0, 2 * SUBLANES - 3 * n_rows), (0, 0)))
    tok = pl.BlockSpec((seqs, T, w), lambda b, h: (b, 0, h))
    assert mkt.shape == (B // seqs, M_WIDTH, seqs * T) and stats.shape == (B // seqs, 3, 2 * H, seqs * T)
    in_specs = [tok, pl.BlockSpec((None, w, seqs * T), lambda b, h: (b, h, 0)), tok, tok,
                pl.BlockSpec((None, 3, 2 * H, seqs * T), lambda b, h: (b, 0, 0, 0)),
                pl.BlockSpec((heads, 1, Dh), lambda b, h: (h, 0, 0)),
                _const_spec((2 * SUBLANES, n_rows * LANES))]
    args = [mq, mkt, mv, mo, stats, norm_g, spread]
    state_specs = [pl.BlockSpec((seqs, 2, heads, Dh, Dh), lambda b, h: (b, 0, h, 0, 0)),
                   pl.BlockSpec((seqs, 2, heads, 1, Dh), lambda b, h: (b, 0, h, 0, 0)),
                   pl.BlockSpec((seqs, heads, 2, LANES), lambda b, h: (b, h, 0, 0))]
    if has_init:
        in_specs += state_specs
        args += list(init_state)
    out_specs = [tok]
    out_shape = [jax.ShapeDtypeStruct((B, T, M_WIDTH), BF16)]
    if emit_state:
        out_specs += state_specs
        out_shape += [jax.ShapeDtypeStruct((B, 2, H, Dh, Dh), F32),
                      jax.ShapeDtypeStruct((B, 2, H, 1, Dh), F32),
                      jax.ShapeDtypeStruct((B, H, 2, LANES), F32)]
    scratch = [] if nb == 1 else [pltpu.VMEM((T, w), F32), pltpu.VMEM((T, w), F32)]
    return pl.pallas_call(
        functools.partial(_mlstm_kernel, has_init, emit_state, nb, heads, seqs),
        grid=(B // seqs, H // heads),
        in_specs=in_specs,
        out_specs=out_specs,
        out_shape=out_shape,
        scratch_shapes=scratch,
        compiler_params=_params(("parallel", "parallel")),
        name="mlstm_latent" if has_init else "mlstm_context",
    )(*args)


def _attn_kernel(has_ctx, seqs, *refs):
    if has_ctx:
        q_ref, k_ref, v_ref, kc_ref, vc_ref, o_ref = refs
    else:
        q_ref, k_ref, v_ref, o_ref = refs
    lane = lax.broadcasted_iota(jnp.int32, (1, LANES), 1)
    ones = lambda n: jnp.ones((n, LANES), BF16)
    tq = q_ref.shape[1]
    sub = min(Q_SUBTILE, tq)
    for sq, r0, pair in [(sq, r0, pair) for sq in range(seqs) for r0 in range(0, tq, sub)
                         for pair in range(A_HEADS // 2)]:
        rows = slice(r0, r0 + sub)
        vsl = slice(pair * LANES, (pair + 1) * LANES)
        v_aug = jnp.concatenate([v_ref[sq, :, vsl], ones(v_ref.shape[1])], axis=1)
        if has_ctx:
            vc_aug = jnp.concatenate([vc_ref[sq, :, vsl], ones(vc_ref.shape[1])], axis=1)
        outs = []
        for e in range(2):
            hsl = slice((2 * pair + e) * HEAD_PAD, (2 * pair + e + 1) * HEAD_PAD)
            qh = q_ref[sq, rows, hsl]
            s = _dot_nt(qh, k_ref[sq, :, hsl])
            mx = jnp.max(s, axis=1, keepdims=True)
            if has_ctx:
                sc = _dot_nt(qh, kc_ref[sq, :, hsl])
                mx = jnp.maximum(mx, jnp.max(sc, axis=1, keepdims=True))
            od = _dot(jnp.exp2(s - mx).astype(BF16), v_aug)
            if has_ctx:
                od = od + _dot(jnp.exp2(sc - mx).astype(BF16), vc_aug)
            outs.append(od[:, 0:LANES] / od[:, LANES:2 * LANES])
        o_ref[sq, rows, vsl] = jnp.where(lane < A_VDIM, outs[0], outs[1]).astype(o_ref.dtype)


def _attn(q, k, v, ctx_kv, seqs):
    B, T, _ = q.shape
    tq = min(Q_TILE, T)
    has_ctx = ctx_kv is not None
    full = lambda n, w: pl.BlockSpec((seqs, n, w), lambda b, i: (b, 0, 0))
    in_specs = [pl.BlockSpec((seqs, tq, QK_PAD), lambda b, i: (b, i, 0)), full(T, QK_PAD), full(T, A_WIDTH)]
    args = [q, k, v]
    if has_ctx:
        P = ctx_kv[0].shape[1]
        in_specs += [full(P, QK_PAD), full(P, A_WIDTH)]
        args += list(ctx_kv)
    return pl.pallas_call(
        functools.partial(_attn_kernel, has_ctx, seqs),
        grid=(B // seqs, T // tq),
        in_specs=in_specs,
        out_specs=pl.BlockSpec((seqs, tq, A_WIDTH), lambda b, i: (b, i, 0)),
        out_shape=jax.ShapeDtypeStruct((B, T, A_WIDTH), BF16),
        compiler_params=_params(("parallel", "parallel")),
        name="attn_latent" if has_ctx else "attn_context",
    )(*args)


def _post_kernel(mod_row0, mod_per_batch, x_ref, hm_ref, ha_ref, mod_ref, g2_ref, wout_ref, wup_ref, wdown_ref,
                 y_ref):
    mod = _mod_row(mod_ref, mod_row0, mod_per_batch)
    gate1 = mod[:, 2 * D_MODEL:3 * D_MODEL]
    sh2 = mod[:, 3 * D_MODEL:4 * D_MODEL]
    sc2 = mod[:, 4 * D_MODEL:5 * D_MODEL]
    gate2 = mod[:, 5 * D_MODEL:6 * D_MODEL]
    mix = jnp.concatenate([hm_ref[...], ha_ref[...]], axis=-1)
    x1 = x_ref[...] + gate1 * _dot(mix, wout_ref[...])
    h2 = (_rms(x1, g2_ref[...]) * (1.0 + sc2) + sh2).astype(BF16)
    acc = jnp.zeros_like(x1)
    for c in range(D_FF // FF_TILE):
        sl = slice(c * FF_TILE, (c + 1) * FF_TILE)
        u = jnp.maximum(_dot(h2, wup_ref[:, sl]), 0.0)
        acc = acc + _dot((u * u).astype(BF16), wdown_ref[sl, :])
    y_ref[...] = x1 + gate2 * acc


def _post(x, hm, ha, mod3, mod_row0, mod_per_batch, wts):
    shape = x.shape
    if not mod_per_batch:
        x, hm, ha = (a.reshape(1, -1, a.shape[-1]) for a in (x, hm, ha))
    B, T, _ = x.shape
    tm = POST_TILE
    tok = lambda w: pl.BlockSpec((None, tm, w), lambda b, i: (b, i, 0))
    return _post_call(x, hm, ha, mod3, mod_row0, mod_per_batch, wts, B, T, tm, tok).reshape(shape)


def _post_call(x, hm, ha, mod3, mod_row0, mod_per_batch, wts, B, T, tm, tok):
    return pl.pallas_call(
        functools.partial(_post_kernel, mod_row0, mod_per_batch),
        grid=(B, T // tm),
        in_specs=[tok(D_MODEL), tok(M_WIDTH), tok(A_WIDTH),
                  _const_spec((8, 6 * D_MODEL)),
                  _const_spec((1, D_MODEL)),
                  _const_spec((M_WIDTH + A_WIDTH, D_MODEL)),
                  _const_spec((D_MODEL, D_FF)),
                  _const_spec((D_FF, D_MODEL))],
        out_specs=tok(D_MODEL),
        out_shape=jax.ShapeDtypeStruct((B, T, D_MODEL), F32),
        compiler_params=_params(("parallel", "parallel")),
        name="post",
    )(x, hm, ha, mod3, wts["g2"], wts["w_out"], wts["w_up"], wts["w_down"])


def _prepare_weights(norm1_g, norm2_g, w_in, mlstm_gate_b, q_lora_g, kv_lora_g, w_q_up, w_kv_up,
                     q_head_g, k_head_g, w_out, w_mlp_up, w_mlp_down):
    o_g = 4 * M_WIDTH
    o_q = o_g + N_GATES
    o_kv = o_q + Q_LORA
    o_kr = o_kv + KV_LORA
    half = A_ROPE // 2
    n_dh = 2 * M_HEADS
    wt = w_in.T.astype(BF16)
    w_gate = wt[o_g:o_q].reshape(2, 2, M_HEADS, D_MODEL)
    bias = mlstm_gate_b.reshape(2, 2, M_HEADS)

    def gate_tile(which, rope_rows):
        return jnp.concatenate([
            w_gate[:, which].reshape(n_dh, D_MODEL), jnp.zeros((A_NOPE - n_dh, D_MODEL), BF16),
            rope_rows, jnp.zeros((LANES - A_QK, D_MODEL), BF16)], axis=0)

    def rot_partner(a):
        z = jnp.zeros(a.shape[:-1] + (A_NOPE,), a.dtype)
        return jnp.concatenate([z, a[..., A_NOPE + half:A_QK], a[..., A_NOPE:A_NOPE + half]], axis=-1)

    pad_tile = lambda a: jnp.pad(a, [(0, 0)] * (a.ndim - 1) + [(0, HEAD_PAD - A_QK)])
    w_kr = wt[o_kr:o_kr + A_ROPE]
    w_kr_partner = jnp.concatenate([w_kr[half:], w_kr[:half]], axis=0)
    w_lat = jnp.concatenate([wt[o_q:o_kr], gate_tile(0, w_kr), gate_tile(1, w_kr_partner)], axis=0)
    gate_bias = jnp.pad(jnp.stack([bias[:, 0, :].reshape(n_dh), bias[:, 1, :].reshape(n_dh)], axis=0),
                        ((0, 0), (0, LANES - n_dh)))
    w_q3 = w_q_up.reshape(Q_LORA, A_HEADS, A_QK)
    w_q = pad_tile(w_q3).reshape(Q_LORA, QK_PAD)
    w_q_partner = pad_tile(rot_partner(w_q3)).reshape(Q_LORA, QK_PAD)
    w_kv3 = w_kv_up.reshape(KV_LORA, A_HEADS, A_NOPE + A_VDIM)
    w_k = jnp.pad(w_kv3[:, :, :A_NOPE], ((0, 0), (0, 0), (0, HEAD_PAD - A_NOPE)))
    w_v = w_kv3[:, :, A_NOPE:]
    w_kv = jnp.concatenate([w_k.reshape(KV_LORA, QK_PAD), w_v.reshape(KV_LORA, A_WIDTH)], axis=1)
    pad_head = lambda g: jnp.stack([pad_tile(g), pad_tile(rot_partner(g))], axis=0)
    return {
        "g1": norm1_g.reshape(1, D_MODEL),
        "g2": norm2_g.reshape(1, D_MODEL),
        "w_main": jnp.concatenate([wt[0:M_WIDTH], wt[2 * M_WIDTH:o_g]], axis=0),
        "w_kt": wt[M_WIDTH:2 * M_WIDTH],
        "w_lat": w_lat,
        "gate_bias": gate_bias,
        "q_lora_g": q_lora_g.reshape(1, Q_LORA),
        "kv_lora_g": kv_lora_g.reshape(1, KV_LORA),
        "w_q": w_q.astype(BF16),
        "w_q_rot": jnp.concatenate([w_q, w_q_partner], axis=1).astype(BF16),
        "w_kv": w_kv.astype(BF16),
        "q_head_g": pad_head(q_head_g),
        "k_head_g": pad_head(k_head_g),
        "w_out": w_out.astype(BF16),
        "w_up": w_mlp_up.astype(BF16),
        "w_down": w_mlp_down.astype(BF16),
    }


def _rope_tables(T):
    rows = T // GRID_W
    row = np.repeat(np.arange(rows, dtype=np.float32), GRID_W)
    col = np.tile(np.arange(GRID_W, dtype=np.float32), rows)
    half = A_ROPE // 2
    inv = (np.float32(ROPE_BASE) ** (-np.arange(0, half, 2, dtype=np.float32) / np.float32(half))).astype(np.float32)
    ang = np.concatenate([row[:, None] * inv, col[:, None] * inv], axis=-1)
    cos, sin = np.cos(ang), np.sin(ang)
    ones = np.ones((T, A_NOPE), np.float32)
    z = lambda w: np.zeros((T, w), np.float32)
    tail = LANES - A_QK
    cos_t = np.concatenate([ones, cos, cos, z(tail)], axis=1)
    sin_t = np.concatenate([z(A_NOPE), -sin, sin, z(tail)], axis=1)
    return jnp.asarray(np.stack([cos_t, sin_t], axis=0).astype(np.float32))


def _layer_pass(x, mod3, mod_row0, mod_per_batch, wts, norm_g, rope_tab, init_state, ctx_kv, is_context):
    pre = _pre(x, mod3, mod_row0, mod_per_batch, wts, rope_tab, emit_cache=is_context)
    mq, mkt, mv, mo, stats, q, k, v = pre[:8]
    ml = _mlstm(mq, mkt, mv, mo, stats, norm_g, init_state, emit_state=is_context,
                heads=M_HEADS if is_context else 1, seqs=MLSTM_CONTEXT_SEQS if is_context else 1)
    ha = _attn(q, k, v, ctx_kv, seqs=ATTN_CONTEXT_SEQS if is_context else 1)
    y = _post(x, ml[0], ha, mod3, mod_row0, mod_per_batch, wts)
    return y, pre[8:], ml[1:]


def kernel(x_prompt, x_sample, cache_mla_ckv, cache_mla_krope, state_mlstm_C, state_mlstm_n, state_mlstm_m,
           c, c_ctx, norm1_g, norm2_g, w_ada, b_ada, w_in, mlstm_gate_b, mlstm_norm_g,
           q_lora_g, kv_lora_g, w_q_up, w_kv_up, q_head_g, k_head_g, w_out, w_mlp_up, w_mlp_down):
    depth = w_in.shape[0]
    Bd = x_sample.shape[0]
    cond8 = jnp.concatenate([c_ctx[None, :], c, jnp.zeros((8 - 1 - Bd, D_MODEL), F32)], axis=0)
    rope_tab = _rope_tables(x_sample.shape[1])

    y, z = x_prompt, x_sample
    ckvs, kropes, Cs, ns, ms = [], [], [], [], []
    for l in range(depth):
        wts = _prepare_weights(norm1_g[l], norm2_g[l], w_in[l], mlstm_gate_b[l], q_lora_g[l], kv_lora_g[l],
                               w_q_up[l], w_kv_up[l], q_head_g[l], k_head_g[l], w_out[l], w_mlp_up[l],
                               w_mlp_down[l])
        norm_g = mlstm_norm_g[l].reshape(M_HEADS, 1, M_HEAD_DIM)
        mod3 = _ada(cond8, w_ada[l], b_ada[l])
        y, (ckv, krope), (C_new, n_new, m_new) = _layer_pass(
            y, mod3, 0, 0, wts, norm_g, None, None, None, True)
        ckvs.append(ckv)
        kropes.append(krope)
        Cs.append(C_new)
        ns.append(n_new[:, :, :, 0, :])
        ms.append(m_new[:, :, :, 0].transpose(0, 2, 1))

        init_state = (state_mlstm_C[:, l],
                      state_mlstm_n[:, l][:, :, :, None, :],
                      jnp.broadcast_to(state_mlstm_m[:, l].transpose(0, 2, 1)[..., None],
                                       (Bd, M_HEADS, 2, LANES)))
        krope_placed = jnp.pad(cache_mla_krope[:, l], ((0, 0), (0, 0), (A_NOPE, LANES - A_QK)))
        ctx_kv = _ctxkv(cache_mla_ckv[:, l], krope_placed, wts)
        z, _, _ = _layer_pass(z, mod3, 1, 1, wts, norm_g, rope_tab, init_state, ctx_kv, False)

    return (y, z, jnp.stack(ckvs, axis=1), jnp.stack(kropes, axis=1), jnp.stack(Cs, axis=1),
            jnp.stack(ns, axis=1), jnp.stack(ms, axis=1))
```

```python
import functools

import jax
import jax.numpy as jnp
import numpy as np
from jax import lax
from jax.experimental import pallas as pl
from jax.experimental.pallas import tpu as pltpu

F32 = jnp.float32
BF16 = jnp.bfloat16

D_MODEL = 1024
GRID_W = 64
M_HEADS = 4
M_HEAD_DIM = 128
M_WIDTH = M_HEADS * M_HEAD_DIM
M_BLOCK = 256
A_HEADS = 8
A_NOPE = 64
A_ROPE = 32
A_QK = A_NOPE + A_ROPE
A_VDIM = 64
A_WIDTH = A_HEADS * A_VDIM
Q_LORA = 384
KV_LORA = 256
ROPE_BASE = 10000.0
D_FF = 4 * D_MODEL
EPS = 1e-6

LANES = 128
SUBLANES = 8
LOG2E = 1.4426950408889634
HEAD_PAD = LANES
QK_PAD = A_HEADS * HEAD_PAD
N_GATES = 4 * M_HEADS
LAT_WIDTH = Q_LORA + KV_LORA + 2 * LANES
VMEM_LIMIT = 56 * 1024 * 1024

TOKEN_TILE = 512
POST_TILE = 512
MLSTM_CONTEXT_SEQS = 2
ATTN_CONTEXT_SEQS = 4
Q_TILE = 1024
Q_SUBTILE = 256
ADA_TILE_N = 1536
FF_TILE = 1024


def _dot(a, b):
    return jnp.dot(a, b, preferred_element_type=F32)


def _dot_nt(a, b):
    return lax.dot_general(a, b, (((1,), (1,)), ((), ())), preferred_element_type=F32)


def _dot_tn(a, b):
    return lax.dot_general(a, b, (((0,), (0,)), ((), ())), preferred_element_type=F32)


def _rms(x, g):
    y = x * lax.rsqrt(jnp.mean(x * x, axis=-1, keepdims=True) + EPS)
    return y * g


def _params(sem):
    return pltpu.CompilerParams(dimension_semantics=sem, vmem_limit_bytes=VMEM_LIMIT)


def _const_spec(shape):
    zeros = (0,) * len(shape)
    return pl.BlockSpec(shape, lambda *_: zeros, pipeline_mode=pl.Buffered(1))


def _ada_kernel(cond_ref, w_ref, b_ref, o_ref):
    c = cond_ref[...]
    s = (c * jax.nn.sigmoid(c)).astype(BF16)
    o_ref[...] = _dot(s, w_ref[...].astype(BF16)) + b_ref[...]


def _ada(cond8, w_ada, b_ada):
    n = w_ada.shape[1]
    return pl.pallas_call(
        _ada_kernel,
        grid=(n // ADA_TILE_N,),
        in_specs=[
            pl.BlockSpec((8, D_MODEL), lambda j: (0, 0)),
            pl.BlockSpec((D_MODEL, ADA_TILE_N), lambda j: (0, j)),
            pl.BlockSpec((1, ADA_TILE_N), lambda j: (0, j)),
        ],
        out_specs=pl.BlockSpec((8, ADA_TILE_N), lambda j: (0, j)),
        out_shape=jax.ShapeDtypeStruct((8, n), F32),
        compiler_params=_params(("parallel",)),
        name="ada",
    )(cond8, w_ada, b_ada.reshape(1, n))


def _write_heads(src, extra, g_pad, rot, dst_ref):
    for h in range(A_HEADS):
        sl = slice(h * HEAD_PAD, (h + 1) * HEAD_PAD)
        xh = src[:, sl]
        if extra is not None:
            xh = xh + extra
        ss = jnp.sum(xh * xh, axis=-1, keepdims=True) * (1.0 / A_QK)
        r = lax.rsqrt(ss + EPS)
        if rot is None:
            y = xh * r * g_pad
        else:
            partner, cos_g, sin_g = rot
            ph = partner if partner.shape[1] == HEAD_PAD else partner[:, sl]
            y = (xh * cos_g + ph * sin_g) * r
        dst_ref[:, sl] = y.astype(dst_ref.dtype)


def _time_scan(x, op, identity, reverse):
    n = x.shape[0]
    row = lax.broadcasted_iota(jnp.int32, x.shape, 0)
    shift = 1
    while shift < n:
        if shift < SUBLANES:
            if reverse:
                moved = jnp.where(row < n - shift, pltpu.roll(x, n - shift, 0), identity)
            else:
                moved = jnp.where(row >= shift, pltpu.roll(x, shift, 0), identity)
        else:
            fill = jnp.full((shift, x.shape[1]), identity, x.dtype)
            moved = (jnp.concatenate([x[shift:], fill], axis=0) if reverse
                     else jnp.concatenate([fill, x[:n - shift]], axis=0))
        x = op(x, moved)
        shift *= 2
    return x


def _mod_row(mod_ref, row0, per_batch):
    if per_batch:
        return mod_ref[pl.ds(row0 + pl.program_id(0) * per_batch, 1), :]
    return mod_ref[row0:row0 + 1, :]


def _pre_kernel(has_rope, emit_cache, mod_row0, mod_per_batch, *refs):
    (x_ref, mod_ref, g1_ref, wmain_ref, wkt_ref, wlat_ref, gbias_ref, qlg_ref, kvg_ref, wq_ref, wkv_ref,
     qhg_ref, khg_ref) = refs[:13]
    pos = 13
    if has_rope:
        rope_ref = refs[pos]
        pos += 1
    if emit_cache:
        cast_in = refs[pos:pos + 3]
        pos += 3
    (mq_ref, mkt_ref, mv_ref, mo_ref, stats_ref, q_ref, k_ref, v_ref) = refs[pos:pos + 8]
    pos += 8

    x = x_ref[...]
    mod = _mod_row(mod_ref, mod_row0, mod_per_batch)
    sh1 = mod[:, 0:D_MODEL]
    sc1 = mod[:, D_MODEL:2 * D_MODEL]
    h = _rms(x, g1_ref[...]) * (1.0 + sc1) + sh1
    hb = h.astype(BF16)

    plat = _dot_nt(hb, wlat_ref[...])
    q_lat = plat[:, 0:Q_LORA]
    kv_lat = plat[:, Q_LORA:Q_LORA + KV_LORA]
    tail = plat[:, Q_LORA + KV_LORA:Q_LORA + KV_LORA + LANES]
    tail2 = plat[:, LAT_WIDTH - LANES:LAT_WIDTH]

    lane = lax.broadcasted_iota(jnp.int32, (1, LANES), 1)
    fwd = lane < M_HEADS
    gate_i = tail + gbias_ref[0:1, :]
    gate_f = tail2 + gbias_ref[1:2, :]
    log_f = jnp.minimum(gate_f, 0.0) - jnp.log1p(jnp.exp(-jnp.abs(gate_f)))
    def block_scan(v, op, identity):
        parts = [v[r:r + M_BLOCK] for r in range(0, v.shape[0], M_BLOCK)]
        return jnp.where(fwd, jnp.concatenate([_time_scan(p, op, identity, False) for p in parts], axis=0),
                         jnp.concatenate([_time_scan(p, op, identity, True) for p in parts], axis=0))

    b = block_scan(log_f, jnp.add, 0.0)
    a = gate_i - b
    amax = block_scan(a, jnp.maximum, -jnp.inf)
    stats_ref[0] = b.T[0:2 * M_HEADS, :]
    stats_ref[1] = a.T[0:2 * M_HEADS, :]
    stats_ref[2] = amax.T[0:2 * M_HEADS, :]

    krope_placed = jnp.where((lane >= A_NOPE) & (lane < A_QK), tail, 0.0)
    ckv = _rms(kv_lat, kvg_ref[...])
    qn = _rms(q_lat, qlg_ref[...])
    qf = _dot(qn.astype(BF16), wq_ref[...])
    kvf = _dot(ckv.astype(BF16), wkv_ref[...])
    v_ref[...] = kvf[:, QK_PAD:QK_PAD + A_WIDTH].astype(BF16)
    q_rot = k_rot = None
    qhg = qhg_ref[...] * (A_QK ** -0.5 * LOG2E)
    if has_rope:
        cos_t, sin_t = rope_ref[0], rope_ref[1]
        q_rot = (qf[:, QK_PAD:2 * QK_PAD], cos_t * qhg[0:1, :], sin_t * qhg[1:2, :])
        k_rot = (tail2, cos_t * khg_ref[0:1, :], sin_t * khg_ref[1:2, :])
    _write_heads(qf, None, qhg[0:1, :], q_rot, q_ref)
    _write_heads(kvf, krope_placed, khg_ref[0:1, :], k_rot, k_ref)

    if emit_cache:
        ckv_ref, krope_ref = refs[pos:pos + 2]
        ckv_ref[...] = ckv
        krope_ref[...] = tail[:, A_NOPE:A_QK]
        for src_ref, dst_ref in zip(cast_in, refs[pos + 2:pos + 5]):
            dst_ref[...] = src_ref[...].astype(BF16)

    pm = _dot_nt(hb, wmain_ref[...])
    mq_ref[...] = pm[:, 0:M_WIDTH].astype(BF16)
    mv_ref[...] = pm[:, M_WIDTH:2 * M_WIDTH].astype(BF16)
    mo_ref[...] = pm[:, 2 * M_WIDTH:3 * M_WIDTH]
    mkt_ref[...] = _dot_nt(wkt_ref[...], hb) * (M_HEAD_DIM ** -0.5)


def _pre(x, mod3, mod_row0, mod_per_batch, wts, rope_tab, emit_cache):
    shape = x.shape
    tm = TOKEN_TILE
    if shape[1] < tm:
        assert not mod_per_batch and tm % shape[1] == 0
        x = x.reshape(-1, tm, shape[2])
    B, T, _ = x.shape
    has_rope = rope_tab is not None
    tok = lambda w: pl.BlockSpec((None, tm, w), lambda b, i: (b, i, 0))
    in_specs = [
        tok(D_MODEL),
        _const_spec((8, 6 * D_MODEL)),
        _const_spec((1, D_MODEL)),
        _const_spec((3 * M_WIDTH, D_MODEL)),
        _const_spec((M_WIDTH, D_MODEL)),
        _const_spec((LAT_WIDTH, D_MODEL)),
        _const_spec((2, LANES)),
        _const_spec((1, Q_LORA)),
        _const_spec((1, KV_LORA)),
        _const_spec((Q_LORA, 2 * QK_PAD if has_rope else QK_PAD)),
        _const_spec((KV_LORA, QK_PAD + A_WIDTH)),
        _const_spec((2, HEAD_PAD)),
        _const_spec((2, HEAD_PAD)),
    ]
    assert tm % M_BLOCK == 0
    args = [x, mod3, wts["g1"], wts["w_main"], wts["w_kt"], wts["w_lat"], wts["gate_bias"], wts["q_lora_g"],
            wts["kv_lora_g"], wts["w_q_rot"] if has_rope else wts["w_q"], wts["w_kv"],
            wts["q_head_g"], wts["k_head_g"]]
    if has_rope:
        in_specs.append(pl.BlockSpec((2, tm, HEAD_PAD), lambda b, i: (0, i, 0)))
        args.append(rope_tab)
    out_specs = [tok(M_WIDTH),
                 pl.BlockSpec((None, M_WIDTH, tm), lambda b, i: (b, 0, i)),
                 tok(M_WIDTH), tok(M_WIDTH),
                 pl.BlockSpec((None, 3, 2 * M_HEADS, tm), lambda b, i: (b, 0, 0, i)),
                 tok(QK_PAD), tok(QK_PAD), tok(A_WIDTH)]
    out_shape = [
        jax.ShapeDtypeStruct((B, T, M_WIDTH), BF16),
        jax.ShapeDtypeStruct((B, M_WIDTH, T), F32),
        jax.ShapeDtypeStruct((B, T, M_WIDTH), BF16),
        jax.ShapeDtypeStruct((B, T, M_WIDTH), F32),
        jax.ShapeDtypeStruct((B, 3, 2 * M_HEADS, T), F32),
        jax.ShapeDtypeStruct((B, T, QK_PAD), BF16),
        jax.ShapeDtypeStruct((B, T, QK_PAD), BF16),
        jax.ShapeDtypeStruct((B, T, A_WIDTH), BF16),
    ]
    if emit_cache:
        out_specs += [tok(KV_LORA), tok(A_ROPE)]
        out_shape += [jax.ShapeDtypeStruct((B, T, KV_LORA), F32),
                      jax.ShapeDtypeStruct((B, T, A_ROPE), F32)]
        steps, per_b = B * (T // tm), T // tm
        for w in (wts["w_out_f32"], wts["w_up_f32"], wts["w_down_f32"]):
            rows = w.shape[0] // steps
            assert rows * steps == w.shape[0] and rows % (2 * SUBLANES) == 0
            spec = pl.BlockSpec((rows, w.shape[1]), lambda b, i: (b * per_b + i, 0))
            in_specs.append(spec)
            args.append(w)
            out_specs.append(spec)
            out_shape.append(jax.ShapeDtypeStruct(w.shape, BF16))
    outs = pl.pallas_call(
        functools.partial(_pre_kernel, has_rope, emit_cache, mod_row0, mod_per_batch),
        grid=(B, T // tm),
        in_specs=in_specs,
        out_specs=out_specs,
        out_shape=out_shape,
        compiler_params=_params(("parallel", "parallel")),
        name="pre_latent" if has_rope else "pre_context",
    )(*args)
    keep = (1, 4, 10, 11, 12)
    return [o if n in keep else o.reshape(shape[:2] + o.shape[2:]) for n, o in enumerate(outs)]


def _ctxkv_kernel(ckv_ref, krp_ref, wkv_ref, khg_ref, k_ref, v_ref):
    kvf = _dot(ckv_ref[...].astype(BF16), wkv_ref[...])
    v_ref[...] = kvf[:, QK_PAD:QK_PAD + A_WIDTH].astype(BF16)
    _write_heads(kvf, krp_ref[...], khg_ref[0:1, :], None, k_ref)


def _ctxkv(ckv, krope_placed, wts):
    B, P, _ = ckv.shape
    tok = lambda w: pl.BlockSpec((None, P, w), lambda b: (b, 0, 0))
    return pl.pallas_call(
        _ctxkv_kernel,
        grid=(B,),
        in_specs=[tok(KV_LORA), tok(HEAD_PAD), _const_spec((KV_LORA, QK_PAD + A_WIDTH)),
                  _const_spec((2, HEAD_PAD))],
        out_specs=[tok(QK_PAD), tok(A_WIDTH)],
        out_shape=[jax.ShapeDtypeStruct((B, P, QK_PAD), BF16),
                   jax.ShapeDtypeStruct((B, P, A_WIDTH), BF16)],
        compiler_params=_params(("parallel",)),
        name="ctx_kv",
    )(ckv, krope_placed, wts["w_kv"], wts["k_head_g"])


def _rows_to_lane_broadcast(rows, spread):
    x = jnp.concatenate(rows, axis=0)
    p1 = x.astype(BF16)
    r1 = x - p1.astype(F32)
    p2 = r1.astype(BF16)
    p3 = (r1 - p2.astype(F32)).astype(BF16)
    pad = jnp.zeros((spread.shape[0] - 3 * len(rows), x.shape[1]), BF16)
    return _dot_tn(jnp.concatenate([p1, p2, p3, pad], axis=0), spread)


def _mlstm_gate_rows(b_row, a_row, amax_row, forward, m):
    L = b_row.shape[1]
    last = slice(L - 1, L) if forward else slice(0, 1)
    total = b_row[:, last]
    g_row = jnp.maximum(m, amax_row)
    m_new = total + jnp.maximum(m, amax_row[:, last])
    w_key_row = jnp.exp2((a_row + (total - m_new)) * LOG2E)
    decay = jnp.exp(total + m - m_new)
    return g_row * LOG2E, (b_row + g_row) * LOG2E, a_row * LOG2E, w_key_row, decay, m_new


def _mlstm_block(s_raw, q, kt, v_aug, g2, mt2, a2_row, w_key_row, decay, allow, CN, m):
    w_intra = jnp.exp2(jnp.where(allow, a2_row - jnp.concatenate([g2, g2], axis=1), -jnp.inf))
    w_inter = jnp.exp2(m * LOG2E - g2)
    s = (s_raw * w_intra).astype(BF16)
    nd = _dot(s, v_aug) + jnp.concatenate([w_inter, w_inter], axis=1) * _dot(q, CN.astype(BF16))
    num, den = nd[:, 0:M_HEAD_DIM], nd[:, M_HEAD_DIM:2 * M_HEAD_DIM]
    h = num / jnp.maximum(jnp.abs(den), jnp.exp2(-mt2))
    CN_new = decay * CN + _dot((kt * w_key_row).astype(BF16), v_aug)
    return h, CN_new


def _mlstm_kernel(has_init, emit_state, n_blocks, heads, seqs, *refs):
    q_ref, kt_ref, v_ref, mo_ref, stats_ref, ng_ref, spread_ref = refs[:7]
    pos = 7
    if has_init:
        c0_ref, n0_ref, m0_ref = refs[pos:pos + 3]
        pos += 3
    hm_ref = refs[pos]
    pos += 1
    if emit_state:
        c_ref, n_ref, m_ref = refs[pos:pos + 3]
        pos += 3

    L, Dh = M_BLOCK, M_HEAD_DIM
    t_idx = lax.broadcasted_iota(jnp.int32, (L, L), 0)
    s_idx = lax.broadcasted_iota(jnp.int32, (L, L), 1)
    allow = (s_idx <= t_idx, s_idx >= t_idx)
    spread = spread_ref[...]
    ones = jnp.ones((L, Dh), BF16)

    def time_lanes(sq, c):
        start = (sq * n_blocks + c) * L
        return slice(start, start + L)

    def lane_broadcast_n(n_row):
        return jnp.broadcast_to(n_row, (Dh, Dh)).T

    def init_state(sq, j, d):
        if has_init:
            return (jnp.concatenate([c0_ref[sq, d, j], lane_broadcast_n(n0_ref[sq, d, j])], axis=1),
                    m0_ref[sq, j, d:d + 1, 0:1])
        return jnp.zeros((Dh, 2 * Dh), F32), jnp.zeros((1, 1), F32)

    def gate_rows(sq, j, c, d, m):
        head = j if heads == M_HEADS else pl.program_id(1)
        r, lanes = pl.ds(d * M_HEADS + head, 1), time_lanes(sq, c)
        return _mlstm_gate_rows(stats_ref[0, r, lanes], stats_ref[1, r, lanes], stats_ref[2, r, lanes], d == 0, m)

    def blocks(sq, j, jobs, states):
        loaded, rows6, cols_in = {}, [], []
        for (c, d), (CN, m) in zip(jobs, states):
            rows6.append(gate_rows(sq, j, c, d, m))
            cols_in += [rows6[-1][0], rows6[-1][1]]
            if c not in loaded:
                rows, cols = slice(c * L, (c + 1) * L), slice(j * Dh, (j + 1) * Dh)
                q, kt = q_ref[sq, rows, cols], kt_ref[cols, time_lanes(sq, c)]
                v_aug = jnp.concatenate([v_ref[sq, rows, cols], ones], axis=1)
                loaded[c] = (_dot(q, kt.astype(BF16)), q, kt, v_aug)
        cols_out = _rows_to_lane_broadcast(cols_in, spread)
        hs, new_states = [], []
        for idx, ((c, d), (CN, m)) in enumerate(zip(jobs, states)):
            g2 = cols_out[:, (2 * idx) * LANES:(2 * idx + 1) * LANES]
            mt2 = cols_out[:, (2 * idx + 1) * LANES:(2 * idx + 2) * LANES]
            _, _, a2_row, w_key_row, decay, m_new = rows6[idx]
            h, CN_new = _mlstm_block(*loaded[c], g2, mt2, a2_row, w_key_row, decay, allow[d], CN, m)
            hs.append(h)
            new_states.append((CN_new, m_new))
        return hs, new_states

    def finish(sq, j, rows, hs):
        cols = slice(j * Dh, (j + 1) * Dh)
        hn = _rms(hs, ng_ref[j])
        hm_ref[sq, rows, cols] = (hn * jax.nn.sigmoid(mo_ref[sq, rows, cols])).astype(hm_ref.dtype)

    def emit(sq, j, d, state):
        CN, m = state
        c_ref[sq, d, j] = CN[:, 0:Dh]
        n_ref[sq, d, j] = CN[:, Dh:2 * Dh].T[0:1, :]
        m_ref[sq, j, d:d + 1, :] = jnp.broadcast_to(m, (1, LANES))

    if n_blocks > 1:
        hf_scr, hb_scr = refs[pos:pos + 2]

    for sq, j in [(sq, j) for sq in range(seqs) for j in range(heads)]:
        cols = slice(j * Dh, (j + 1) * Dh)
        states = [init_state(sq, j, 0), init_state(sq, j, 1)]
        if n_blocks == 1:
            (hf, hb), states = blocks(sq, j, [(0, 0), (0, 1)], states)
            finish(sq, j, slice(0, L), hf + hb)
        else:
            for step in range(n_blocks):
                cf, cb = step, n_blocks - 1 - step
                (hf, hb), states = blocks(sq, j, [(cf, 0), (cb, 1)], states)
                hf_scr[cf * L:(cf + 1) * L, cols] = hf
                hb_scr[cb * L:(cb + 1) * L, cols] = hb
            finish(sq, j, slice(None), hf_scr[:, cols] + hb_scr[:, cols])
        if emit_state:
            emit(sq, j, 0, states[0])
            emit(sq, j, 1, states[1])


def _mlstm(mq, mkt, mv, mo, stats, norm_g, init_state, emit_state, heads, seqs):
    B, T, _ = mq.shape
    H, Dh, L = M_HEADS, M_HEAD_DIM, M_BLOCK
    nb = T // L
    w = heads * Dh
    has_init = init_state is not None
    n_rows = 4
    spread = jnp.tile(jnp.repeat(jnp.eye(n_rows, dtype=BF16), LANES, axis=1), (3, 1))
    spread = jnp.pad(spread, ((0, 2 * SUBLANES - 3 * n_rows), (0, 0)))
    tok = pl.BlockSpec((seqs, T, w), lambda b, h: (b, 0, h))
    assert mkt.shape == (B // seqs, M_WIDTH, seqs * T) and stats.shape == (B // seqs, 3, 2 * H, seqs * T)
    in_specs = [tok, pl.BlockSpec((None, w, seqs * T), lambda b, h: (b, h, 0)), tok, tok,
                pl.BlockSpec((None, 3, 2 * H, seqs * T), lambda b, h: (b, 0, 0, 0)),
                pl.BlockSpec((heads, 1, Dh), lambda b, h: (h, 0, 0)),
                _const_spec((2 * SUBLANES, n_rows * LANES))]
    args = [mq, mkt, mv, mo, stats, norm_g, spread]
    state_specs = [pl.BlockSpec((seqs, 2, heads, Dh, Dh), lambda b, h: (b, 0, h, 0, 0)),
                   pl.BlockSpec((seqs, 2, heads, 1, Dh), lambda b, h: (b, 0, h, 0, 0)),
                   pl.BlockSpec((seqs, heads, 2, LANES), lambda b, h: (b, h, 0, 0))]
    if has_init:
        in_specs += state_specs
        args += list(init_state)
    out_specs = [tok]
    out_shape = [jax.ShapeDtypeStruct((B, T, M_WIDTH), BF16)]
    if emit_state:
        out_specs += state_specs
        out_shape += [jax.ShapeDtypeStruct((B, 2, H, Dh, Dh), F32),
                      jax.ShapeDtypeStruct((B, 2, H, 1, Dh), F32),
                      jax.ShapeDtypeStruct((B, H, 2, LANES), F32)]
    scratch = [] if nb == 1 else [pltpu.VMEM((T, w), F32), pltpu.VMEM((T, w), F32)]
    return pl.pallas_call(
        functools.partial(_mlstm_kernel, has_init, emit_state, nb, heads, seqs),
        grid=(B // seqs, H // heads),
        in_specs=in_specs,
        out_specs=out_specs,
        out_shape=out_shape,
        scratch_shapes=scratch,
        compiler_params=_params(("parallel", "parallel")),
        name="mlstm_latent" if has_init else "mlstm_context",
    )(*args)


def _attn_kernel(has_ctx, seqs, *refs):
    if has_ctx:
        q_ref, k_ref, v_ref, kc_ref, vc_ref, o_ref = refs
    else:
        q_ref, k_ref, v_ref, o_ref = refs
    lane = lax.broadcasted_iota(jnp.int32, (1, LANES), 1)
    ones = lambda n: jnp.ones((n, LANES), BF16)
    tq = q_ref.shape[1]
    sub = min(Q_SUBTILE, tq)
    for sq, r0, pair in [(sq, r0, pair) for sq in range(seqs) for r0 in range(0, tq, sub)
                         for pair in range(A_HEADS // 2)]:
        rows = slice(r0, r0 + sub)
        vsl = slice(pair * LANES, (pair + 1) * LANES)
        v_aug = jnp.concatenate([v_ref[sq, :, vsl], ones(v_ref.shape[1])], axis=1)
        if has_ctx:
            vc_aug = jnp.concatenate([vc_ref[sq, :, vsl], ones(vc_ref.shape[1])], axis=1)
        outs = []
        for e in range(2):
            hsl = slice((2 * pair + e) * HEAD_PAD, (2 * pair + e + 1) * HEAD_PAD)
            qh = q_ref[sq, rows, hsl]
            s = _dot_nt(qh, k_ref[sq, :, hsl])
            mx = jnp.max(s, axis=1, keepdims=True)
            if has_ctx:
                sc = _dot_nt(qh, kc_ref[sq, :, hsl])
                mx = jnp.maximum(mx, jnp.max(sc, axis=1, keepdims=True))
            od = _dot(jnp.exp2(s - mx).astype(BF16), v_aug)
            if has_ctx:
                od = od + _dot(jnp.exp2(sc - mx).astype(BF16), vc_aug)
            outs.append(od[:, 0:LANES] / od[:, LANES:2 * LANES])
        o_ref[sq, rows, vsl] = jnp.where(lane < A_VDIM, outs[0], outs[1]).astype(o_ref.dtype)


def _attn(q, k, v, ctx_kv, seqs):
    B, T, _ = q.shape
    tq = min(Q_TILE, T)
    has_ctx = ctx_kv is not None
    full = lambda n, w: pl.BlockSpec((seqs, n, w), lambda b, i: (b, 0, 0))
    in_specs = [pl.BlockSpec((seqs, tq, QK_PAD), lambda b, i: (b, i, 0)), full(T, QK_PAD), full(T, A_WIDTH)]
    args = [q, k, v]
    if has_ctx:
        P = ctx_kv[0].shape[1]
        in_specs += [full(P, QK_PAD), full(P, A_WIDTH)]
        args += list(ctx_kv)
    return pl.pallas_call(
        functools.partial(_attn_kernel, has_ctx, seqs),
        grid=(B // seqs, T // tq),
        in_specs=in_specs,
        out_specs=pl.BlockSpec((seqs, tq, A_WIDTH), lambda b, i: (b, i, 0)),
        out_shape=jax.ShapeDtypeStruct((B, T, A_WIDTH), BF16),
        compiler_params=_params(("parallel", "parallel")),
        name="attn_latent" if has_ctx else "attn_context",
    )(*args)


def _post_kernel(mod_row0, mod_per_batch, x_ref, hm_ref, ha_ref, mod_ref, g2_ref, wout_ref, wup_ref, wdown_ref,
                 y_ref):
    mod = _mod_row(mod_ref, mod_row0, mod_per_batch)
    gate1 = mod[:, 2 * D_MODEL:3 * D_MODEL]
    sh2 = mod[:, 3 * D_MODEL:4 * D_MODEL]
    sc2 = mod[:, 4 * D_MODEL:5 * D_MODEL]
    gate2 = mod[:, 5 * D_MODEL:6 * D_MODEL]
    mix = jnp.concatenate([hm_ref[...], ha_ref[...]], axis=-1)
    x1 = x_ref[...] + gate1 * _dot(mix, wout_ref[...])
    h2 = (_rms(x1, g2_ref[...]) * (1.0 + sc2) + sh2).astype(BF16)
    acc = jnp.zeros_like(x1)
    for c in range(D_FF // FF_TILE):
        sl = slice(c * FF_TILE, (c + 1) * FF_TILE)
        u = jnp.maximum(_dot(h2, wup_ref[:, sl]), 0.0)
        acc = acc + _dot((u * u).astype(BF16), wdown_ref[sl, :])
    y_ref[...] = x1 + gate2 * acc


def _post(x, hm, ha, mod3, mod_row0, mod_per_batch, wts):
    shape = x.shape
    if not mod_per_batch:
        x, hm, ha = (a.reshape(1, -1, a.shape[-1]) for a in (x, hm, ha))
    B, T, _ = x.shape
    tm = POST_TILE
    tok = lambda w: pl.BlockSpec((None, tm, w), lambda b, i: (b, i, 0))
    return _post_call(x, hm, ha, mod3, mod_row0, mod_per_batch, wts, B, T, tm, tok).reshape(shape)


def _post_call(x, hm, ha, mod3, mod_row0, mod_per_batch, wts, B, T, tm, tok):
    return pl.pallas_call(
        functools.partial(_post_kernel, mod_row0, mod_per_batch),
        grid=(B, T // tm),
        in_specs=[tok(D_MODEL), tok(M_WIDTH), tok(A_WIDTH),
                  _const_spec((8, 6 * D_MODEL)),
                  _const_spec((1, D_MODEL)),
                  _const_spec((M_WIDTH + A_WIDTH, D_MODEL)),
                  _const_spec((D_MODEL, D_FF)),
                  _const_spec((D_FF, D_MODEL))],
        out_specs=tok(D_MODEL),
        out_shape=jax.ShapeDtypeStruct((B, T, D_MODEL), F32),
        compiler_params=_params(("parallel", "parallel")),
        name="post",
    )(x, hm, ha, mod3, wts["g2"], wts["w_out"], wts["w_up"], wts["w_down"])


def _prepare_weights(norm1_g, norm2_g, w_in, mlstm_gate_b, q_lora_g, kv_lora_g, w_q_up, w_kv_up,
                     q_head_g, k_head_g, w_out, w_mlp_up, w_mlp_down):
    o_g = 4 * M_WIDTH
    o_q = o_g + N_GATES
    o_kv = o_q + Q_LORA
    o_kr = o_kv + KV_LORA
    half = A_ROPE // 2
    n_dh = 2 * M_HEADS
    wt = w_in.T.astype(BF16)
    w_gate = wt[o_g:o_q].reshape(2, 2, M_HEADS, D_MODEL)
    bias = mlstm_gate_b.reshape(2, 2, M_HEADS)

    def gate_tile(which, rope_rows):
        return jnp.concatenate([
            w_gate[:, which].reshape(n_dh, D_MODEL), jnp.zeros((A_NOPE - n_dh, D_MODEL), BF16),
            rope_rows, jnp.zeros((LANES - A_QK, D_MODEL), BF16)], axis=0)

    def rot_partner(a):
        z = jnp.zeros(a.shape[:-1] + (A_NOPE,), a.dtype)
        return jnp.concatenate([z, a[..., A_NOPE + half:A_QK], a[..., A_NOPE:A_NOPE + half]], axis=-1)

    pad_tile = lambda a: jnp.pad(a, [(0, 0)] * (a.ndim - 1) + [(0, HEAD_PAD - A_QK)])
    w_kr = wt[o_kr:o_kr + A_ROPE]
    w_kr_partner = jnp.concatenate([w_kr[half:], w_kr[:half]], axis=0)
    w_lat = jnp.concatenate([wt[o_q:o_kr], gate_tile(0, w_kr), gate_tile(1, w_kr_partner)], axis=0)
    gate_bias = jnp.pad(jnp.stack([bias[:, 0, :].reshape(n_dh), bias[:, 1, :].reshape(n_dh)], axis=0),
                        ((0, 0), (0, LANES - n_dh)))
    w_q3 = w_q_up.reshape(Q_LORA, A_HEADS, A_QK)
    w_q = pad_tile(w_q3).reshape(Q_LORA, QK_PAD)
    w_q_partner = pad_tile(rot_partner(w_q3)).reshape(Q_LORA, QK_PAD)
    w_kv3 = w_kv_up.reshape(KV_LORA, A_HEADS, A_NOPE + A_VDIM)
    w_k = jnp.pad(w_kv3[:, :, :A_NOPE], ((0, 0), (0, 0), (0, HEAD_PAD - A_NOPE)))
    w_v = w_kv3[:, :, A_NOPE:]
    w_kv = jnp.concatenate([w_k.reshape(KV_LORA, QK_PAD), w_v.reshape(KV_LORA, A_WIDTH)], axis=1)
    pad_head = lambda g: jnp.stack([pad_tile(g), pad_tile(rot_partner(g))], axis=0)
    return {
        "g1": norm1_g.reshape(1, D_MODEL),
        "g2": norm2_g.reshape(1, D_MODEL),
        "w_main": jnp.concatenate([wt[0:M_WIDTH], wt[2 * M_WIDTH:o_g]], axis=0),
        "w_kt": wt[M_WIDTH:2 * M_WIDTH],
        "w_lat": w_lat,
        "gate_bias": gate_bias,
        "q_lora_g": q_lora_g.reshape(1, Q_LORA),
        "kv_lora_g": kv_lora_g.reshape(1, KV_LORA),
        "w_q": w_q.astype(BF16),
        "w_q_rot": jnp.concatenate([w_q, w_q_partner], axis=1).astype(BF16),
        "w_kv": w_kv.astype(BF16),
        "q_head_g": pad_head(q_head_g),
        "k_head_g": pad_head(k_head_g),
        "w_out_f32": w_out,
        "w_up_f32": w_mlp_up,
        "w_down_f32": w_mlp_down,
    }


def _rope_tables(T):
    rows = T // GRID_W
    row = np.repeat(np.arange(rows, dtype=np.float32), GRID_W)
    col = np.tile(np.arange(GRID_W, dtype=np.float32), rows)
    half = A_ROPE // 2
    inv = (np.float32(ROPE_BASE) ** (-np.arange(0, half, 2, dtype=np.float32) / np.float32(half))).astype(np.float32)
    ang = np.concatenate([row[:, None] * inv, col[:, None] * inv], axis=-1)
    cos, sin = np.cos(ang), np.sin(ang)
    ones = np.ones((T, A_NOPE), np.float32)
    z = lambda w: np.zeros((T, w), np.float32)
    tail = LANES - A_QK
    cos_t = np.concatenate([ones, cos, cos, z(tail)], axis=1)
    sin_t = np.concatenate([z(A_NOPE), -sin, sin, z(tail)], axis=1)
    return jnp.asarray(np.stack([cos_t, sin_t], axis=0).astype(np.float32))


def _layer_pass(x, mod3, mod_row0, mod_per_batch, wts, norm_g, rope_tab, init_state, ctx_kv, is_context):
    pre = _pre(x, mod3, mod_row0, mod_per_batch, wts, rope_tab, emit_cache=is_context)
    mq, mkt, mv, mo, stats, q, k, v = pre[:8]
    if is_context:
        wts = dict(wts, w_out=pre[10], w_up=pre[11], w_down=pre[12])
    ml = _mlstm(mq, mkt, mv, mo, stats, norm_g, init_state, emit_state=is_context,
                heads=M_HEADS if is_context else 1, seqs=MLSTM_CONTEXT_SEQS if is_context else 1)
    ha = _attn(q, k, v, ctx_kv, seqs=ATTN_CONTEXT_SEQS if is_context else 1)
    y = _post(x, ml[0], ha, mod3, mod_row0, mod_per_batch, wts)
    return y, pre[8:10], ml[1:], wts


def kernel(x_prompt, x_sample, cache_mla_ckv, cache_mla_krope, state_mlstm_C, state_mlstm_n, state_mlstm_m,
           c, c_ctx, norm1_g, norm2_g, w_ada, b_ada, w_in, mlstm_gate_b, mlstm_norm_g,
           q_lora_g, kv_lora_g, w_q_up, w_kv_up, q_head_g, k_head_g, w_out, w_mlp_up, w_mlp_down):
    depth = w_in.shape[0]
    Bd = x_sample.shape[0]
    cond8 = jnp.concatenate([c_ctx[None, :], c, jnp.zeros((8 - 1 - Bd, D_MODEL), F32)], axis=0)
    rope_tab = _rope_tables(x_sample.shape[1])

    y, z = x_prompt, x_sample
    ckvs, kropes, Cs, ns, ms = [], [], [], [], []
    for l in range(depth):
        wts = _prepare_weights(norm1_g[l], norm2_g[l], w_in[l], mlstm_gate_b[l], q_lora_g[l], kv_lora_g[l],
                               w_q_up[l], w_kv_up[l], q_head_g[l], k_head_g[l], w_out[l], w_mlp_up[l],
                               w_mlp_down[l])
        norm_g = mlstm_norm_g[l].reshape(M_HEADS, 1, M_HEAD_DIM)
        mod3 = _ada(cond8, w_ada[l], b_ada[l])
        y, (ckv, krope), (C_new, n_new, m_new), wts = _layer_pass(
            y, mod3, 0, 0, wts, norm_g, None, None, None, True)
        ckvs.append(ckv)
        kropes.append(krope)
        Cs.append(C_new)
        ns.append(n_new[:, :, :, 0, :])
        ms.append(m_new[:, :, :, 0].transpose(0, 2, 1))

        init_state = (state_mlstm_C[:, l],
                      state_mlstm_n[:, l][:, :, :, None, :],
                      jnp.broadcast_to(state_mlstm_m[:, l].transpose(0, 2, 1)[..., None],
                                       (Bd, M_HEADS, 2, LANES)))
        krope_placed = jnp.pad(cache_mla_krope[:, l], ((0, 0), (0, 0), (A_NOPE, LANES - A_QK)))
        ctx_kv = _ctxkv(cache_mla_ckv[:, l], krope_placed, wts)
        z, _, _, _ = _layer_pass(z, mod3, 1, 1, wts, norm_g, rope_tab, init_state, ctx_kv, False)

    return (y, z, jnp.stack(ckvs, axis=1), jnp.stack(kropes, axis=1), jnp.stack(Cs, axis=1),
            jnp.stack(ns, axis=1), jnp.stack(ms, axis=1))
```

```python
import collections
import functools

import jax
import jax.numpy as jnp
import numpy as np
from jax import lax
from jax.experimental import pallas as pl
from jax.experimental.pallas import tpu as pltpu

F32 = jnp.float32
BF16 = jnp.bfloat16

D_MODEL = 1024
GRID_W = 64
M_HEADS = 4
M_HEAD_DIM = 128
M_WIDTH = M_HEADS * M_HEAD_DIM
M_BLOCK = 256
A_HEADS = 8
A_NOPE = 64
A_ROPE = 32
A_QK = A_NOPE + A_ROPE
A_VDIM = 64
A_WIDTH = A_HEADS * A_VDIM
Q_LORA = 384
KV_LORA = 256
ROPE_BASE = 10000.0
D_FF = 4 * D_MODEL
EPS = 1e-6

LANES = 128
SUBLANES = 8
LOG2E = 1.4426950408889634
HEAD_PAD = LANES
QK_PAD = A_HEADS * HEAD_PAD
N_GATES = 4 * M_HEADS
LAT_WIDTH = Q_LORA + KV_LORA + 2 * LANES
VMEM_LIMIT = 56 * 1024 * 1024

TOKEN_TILE = 512
POST_TILE = 512
CONTEXT_SEQS = 2
LATENT_MLSTM_HEADS = 2
Q_TILE = 1024
Q_SUBTILE = 256
ADA_TILE_N = 1536
FF_TILE = 1024


def _dot(a, b):
    return jnp.dot(a, b, preferred_element_type=F32)


def _dot_nt(a, b):
    return lax.dot_general(a, b, (((1,), (1,)), ((), ())), preferred_element_type=F32)


def _dot_tn(a, b):
    return lax.dot_general(a, b, (((0,), (0,)), ((), ())), preferred_element_type=F32)


def _rms(x, g):
    y = x * lax.rsqrt(jnp.mean(x * x, axis=-1, keepdims=True) + EPS)
    return y * g


def _params(sem):
    return pltpu.CompilerParams(dimension_semantics=sem, vmem_limit_bytes=VMEM_LIMIT)


def _const_spec(shape):
    zeros = (0,) * len(shape)
    return pl.BlockSpec(shape, lambda *_: zeros, pipeline_mode=pl.Buffered(1))


def _ada_kernel(cond_ref, w_ref, b_ref, o_ref):
    c = cond_ref[...]
    s = (c * jax.nn.sigmoid(c)).astype(BF16)
    o_ref[...] = _dot(s, w_ref[...].astype(BF16)) + b_ref[...]


def _ada(cond8, w_ada, b_ada):
    n = w_ada.shape[1]
    return pl.pallas_call(
        _ada_kernel,
        grid=(n // ADA_TILE_N,),
        in_specs=[
            pl.BlockSpec((8, D_MODEL), lambda j: (0, 0)),
            pl.BlockSpec((D_MODEL, ADA_TILE_N), lambda j: (0, j)),
            pl.BlockSpec((1, ADA_TILE_N), lambda j: (0, j)),
        ],
        out_specs=pl.BlockSpec((8, ADA_TILE_N), lambda j: (0, j)),
        out_shape=jax.ShapeDtypeStruct((8, n), F32),
        compiler_params=_params(("parallel",)),
        name="ada",
    )(cond8, w_ada, b_ada.reshape(1, n))


def _write_heads(src, extra, g_pad, rot, dst_ref):
    for h in range(A_HEADS):
        sl = slice(h * HEAD_PAD, (h + 1) * HEAD_PAD)
        xh = src[:, sl]
        if extra is not None:
            xh = xh + extra
        ss = jnp.sum(xh * xh, axis=-1, keepdims=True) * (1.0 / A_QK)
        r = lax.rsqrt(ss + EPS)
        if rot is None:
            y = xh * r * g_pad
        else:
            partner, cos_g, sin_g = rot
            ph = partner if partner.shape[1] == HEAD_PAD else partner[:, sl]
            y = (xh * cos_g + ph * sin_g) * r
        dst_ref[:, sl] = y.astype(dst_ref.dtype)


def _time_scan(x, op, identity, reverse):
    n = x.shape[0]
    row = lax.broadcasted_iota(jnp.int32, x.shape, 0)
    shift = 1
    while shift < n:
        if shift < SUBLANES:
            if reverse:
                moved = jnp.where(row < n - shift, pltpu.roll(x, n - shift, 0), identity)
            else:
                moved = jnp.where(row >= shift, pltpu.roll(x, shift, 0), identity)
        else:
            fill = jnp.full((shift, x.shape[1]), identity, x.dtype)
            moved = (jnp.concatenate([x[shift:], fill], axis=0) if reverse
                     else jnp.concatenate([fill, x[:n - shift]], axis=0))
        x = op(x, moved)
        shift *= 2
    return x


def _mod_row(mod_ref, row0, per_batch):
    if per_batch:
        return mod_ref[pl.ds(row0 + pl.program_id(0) * per_batch, 1), :]
    return mod_ref[row0:row0 + 1, :]


def _pre_kernel(has_rope, emit_cache, mod_row0, mod_per_batch, *refs):
    (x_ref, mod_ref, g1_ref, wmain_ref, wkt_ref, wlat_ref, gbias_ref, qlg_ref, kvg_ref, wq_ref, wkv_ref,
     qhg_ref, khg_ref) = refs[:13]
    pos = 13
    if has_rope:
        rope_ref = refs[pos]
        pos += 1
    if emit_cache:
        cast_in = refs[pos:pos + 3]
        pos += 3
    (mq_ref, mkt_ref, mv_ref, mo_ref, stats_ref, q_ref, k_ref, v_ref) = refs[pos:pos + 8]
    pos += 8

    x = x_ref[...]
    mod = _mod_row(mod_ref, mod_row0, mod_per_batch)
    sh1 = mod[:, 0:D_MODEL]
    sc1 = mod[:, D_MODEL:2 * D_MODEL]
    h = _rms(x, g1_ref[...]) * (1.0 + sc1) + sh1
    hb = h.astype(BF16)

    plat = _dot_nt(hb, wlat_ref[...])
    q_lat = plat[:, 0:Q_LORA]
    kv_lat = plat[:, Q_LORA:Q_LORA + KV_LORA]
    tail = plat[:, Q_LORA + KV_LORA:Q_LORA + KV_LORA + LANES]
    tail2 = plat[:, LAT_WIDTH - LANES:LAT_WIDTH]

    lane = lax.broadcasted_iota(jnp.int32, (1, LANES), 1)
    fwd = lane < M_HEADS
    gate_i = tail + gbias_ref[0:1, :]
    gate_f = tail2 + gbias_ref[1:2, :]
    log_f = jnp.minimum(gate_f, 0.0) - jnp.log1p(jnp.exp(-jnp.abs(gate_f)))
    def block_scan(v, op, identity):
        parts = [v[r:r + M_BLOCK] for r in range(0, v.shape[0], M_BLOCK)]
        return jnp.where(fwd, jnp.concatenate([_time_scan(p, op, identity, False) for p in parts], axis=0),
                         jnp.concatenate([_time_scan(p, op, identity, True) for p in parts], axis=0))

    b = block_scan(log_f, jnp.add, 0.0)
    a = gate_i - b
    amax = block_scan(a, jnp.maximum, -jnp.inf)
    stats_ref[0] = b.T[0:2 * M_HEADS, :]
    stats_ref[1] = a.T[0:2 * M_HEADS, :]
    stats_ref[2] = amax.T[0:2 * M_HEADS, :]

    krope_placed = jnp.where((lane >= A_NOPE) & (lane < A_QK), tail, 0.0)
    ckv = _rms(kv_lat, kvg_ref[...])
    qn = _rms(q_lat, qlg_ref[...])
    qf = _dot(qn.astype(BF16), wq_ref[...])
    kvf = _dot(ckv.astype(BF16), wkv_ref[...])
    v_ref[...] = kvf[:, QK_PAD:QK_PAD + A_WIDTH].astype(BF16)
    q_rot = k_rot = None
    qhg = qhg_ref[...] * (A_QK ** -0.5 * LOG2E)
    if has_rope:
        cos_t, sin_t = rope_ref[0], rope_ref[1]
        q_rot = (qf[:, QK_PAD:2 * QK_PAD], cos_t * qhg[0:1, :], sin_t * qhg[1:2, :])
        k_rot = (tail2, cos_t * khg_ref[0:1, :], sin_t * khg_ref[1:2, :])
    _write_heads(qf, None, qhg[0:1, :], q_rot, q_ref)
    _write_heads(kvf, krope_placed, khg_ref[0:1, :], k_rot, k_ref)

    if emit_cache:
        ckv_ref, krope_ref = refs[pos:pos + 2]
        ckv_ref[...] = ckv
        krope_ref[...] = tail[:, A_NOPE:A_QK]
        for src_ref, dst_ref in zip(cast_in, refs[pos + 2:pos + 5]):
            dst_ref[...] = src_ref[...].astype(BF16)

    pm = _dot_nt(hb, wmain_ref[...])
    mq_ref[...] = pm[:, 0:M_WIDTH].astype(BF16)
    mv_ref[...] = pm[:, M_WIDTH:2 * M_WIDTH].astype(BF16)
    mo_ref[...] = pm[:, 2 * M_WIDTH:3 * M_WIDTH]
    mkt_ref[...] = _dot_nt(wkt_ref[...], hb) * (M_HEAD_DIM ** -0.5)


def _pre(x, mod3, mod_row0, mod_per_batch, wts, rope_tab, emit_cache):
    shape = x.shape
    tm = TOKEN_TILE
    if shape[1] < tm:
        assert not mod_per_batch and tm % shape[1] == 0
        x = x.reshape(-1, tm, shape[2])
    B, T, _ = x.shape
    has_rope = rope_tab is not None
    tok = lambda w: pl.BlockSpec((None, tm, w), lambda b, i: (b, i, 0))
    in_specs = [
        tok(D_MODEL),
        _const_spec((8, 6 * D_MODEL)),
        _const_spec((1, D_MODEL)),
        _const_spec((3 * M_WIDTH, D_MODEL)),
        _const_spec((M_WIDTH, D_MODEL)),
        _const_spec((LAT_WIDTH, D_MODEL)),
        _const_spec((2, LANES)),
        _const_spec((1, Q_LORA)),
        _const_spec((1, KV_LORA)),
        _const_spec((Q_LORA, 2 * QK_PAD if has_rope else QK_PAD)),
        _const_spec((KV_LORA, QK_PAD + A_WIDTH)),
        _const_spec((2, HEAD_PAD)),
        _const_spec((2, HEAD_PAD)),
    ]
    assert tm % M_BLOCK == 0
    args = [x, mod3, wts["g1"], wts["w_main"], wts["w_kt"], wts["w_lat"], wts["gate_bias"], wts["q_lora_g"],
            wts["kv_lora_g"], wts["w_q_rot"] if has_rope else wts["w_q"], wts["w_kv"],
            wts["q_head_g"], wts["k_head_g"]]
    if has_rope:
        in_specs.append(pl.BlockSpec((2, tm, HEAD_PAD), lambda b, i: (0, i, 0)))
        args.append(rope_tab)
    out_specs = [tok(M_WIDTH),
                 pl.BlockSpec((None, M_WIDTH, tm), lambda b, i: (b, 0, i)),
                 tok(M_WIDTH), tok(M_WIDTH),
                 pl.BlockSpec((None, 3, 2 * M_HEADS, tm), lambda b, i: (b, 0, 0, i)),
                 tok(QK_PAD), tok(QK_PAD), tok(A_WIDTH)]
    out_shape = [
        jax.ShapeDtypeStruct((B, T, M_WIDTH), BF16),
        jax.ShapeDtypeStruct((B, M_WIDTH, T), F32),
        jax.ShapeDtypeStruct((B, T, M_WIDTH), BF16),
        jax.ShapeDtypeStruct((B, T, M_WIDTH), F32),
        jax.ShapeDtypeStruct((B, 3, 2 * M_HEADS, T), F32),
        jax.ShapeDtypeStruct((B, T, QK_PAD), BF16),
        jax.ShapeDtypeStruct((B, T, QK_PAD), BF16),
        jax.ShapeDtypeStruct((B, T, A_WIDTH), BF16),
    ]
    if emit_cache:
        out_specs += [tok(KV_LORA), tok(A_ROPE)]
        out_shape += [jax.ShapeDtypeStruct((B, T, KV_LORA), F32),
                      jax.ShapeDtypeStruct((B, T, A_ROPE), F32)]
        steps, per_b = B * (T // tm), T // tm
        for w in (wts["w_out_f32"], wts["w_up_f32"], wts["w_down_f32"]):
            rows = w.shape[0] // steps
            assert rows * steps == w.shape[0] and rows % (2 * SUBLANES) == 0
            spec = pl.BlockSpec((rows, w.shape[1]), lambda b, i: (b * per_b + i, 0))
            in_specs.append(spec)
            args.append(w)
            out_specs.append(spec)
            out_shape.append(jax.ShapeDtypeStruct(w.shape, BF16))
    outs = pl.pallas_call(
        functools.partial(_pre_kernel, has_rope, emit_cache, mod_row0, mod_per_batch),
        grid=(B, T // tm),
        in_specs=in_specs,
        out_specs=out_specs,
        out_shape=out_shape,
        compiler_params=_params(("parallel", "parallel")),
        name="pre_latent" if has_rope else "pre_context",
    )(*args)
    keep = (1, 4, 10, 11, 12)
    return [o if n in keep else o.reshape(shape[:2] + o.shape[2:]) for n, o in enumerate(outs)]


def _ctxkv_kernel(ckv_ref, krp_ref, wkv_ref, khg_ref, k_ref, v_ref):
    kvf = _dot(ckv_ref[...].astype(BF16), wkv_ref[...])
    v_ref[...] = kvf[:, QK_PAD:QK_PAD + A_WIDTH].astype(BF16)
    _write_heads(kvf, krp_ref[...], khg_ref[0:1, :], None, k_ref)


def _ctxkv(ckv, krope_placed, wts):
    B, P, _ = ckv.shape
    tok = lambda w: pl.BlockSpec((None, P, w), lambda b: (b, 0, 0))
    return pl.pallas_call(
        _ctxkv_kernel,
        grid=(B,),
        in_specs=[tok(KV_LORA), tok(HEAD_PAD), _const_spec((KV_LORA, QK_PAD + A_WIDTH)),
                  _const_spec((2, HEAD_PAD))],
        out_specs=[tok(QK_PAD), tok(A_WIDTH)],
        out_shape=[jax.ShapeDtypeStruct((B, P, QK_PAD), BF16),
                   jax.ShapeDtypeStruct((B, P, A_WIDTH), BF16)],
        compiler_params=_params(("parallel",)),
        name="ctx_kv",
    )(ckv, krope_placed, wts["w_kv"], wts["k_head_g"])


def _rows_to_lane_broadcast(rows, spread):
    x = jnp.concatenate(rows, axis=0)
    p1 = x.astype(BF16)
    r1 = x - p1.astype(F32)
    p2 = r1.astype(BF16)
    p3 = (r1 - p2.astype(F32)).astype(BF16)
    pad = jnp.zeros((spread.shape[0] - 3 * len(rows), x.shape[1]), BF16)
    return _dot_tn(jnp.concatenate([p1, p2, p3, pad], axis=0), spread)


def _mlstm_gate_rows(b_row, a_row, amax_row, forward, m):
    L = b_row.shape[1]
    last = slice(L - 1, L) if forward else slice(0, 1)
    total = b_row[:, last]
    g_row = jnp.maximum(m, amax_row)
    m_new = total + jnp.maximum(m, amax_row[:, last])
    w_key_row = jnp.exp2((a_row + (total - m_new)) * LOG2E)
    decay = jnp.exp(total + m - m_new)
    return g_row * LOG2E, (b_row + g_row) * LOG2E, a_row * LOG2E, w_key_row, decay, m_new


def _mlstm_block(s_raw, q, kt, v_aug, g2, mt2, a2_row, w_key_row, decay, allow, CN, m):
    w_intra = jnp.exp2(jnp.where(allow, a2_row - jnp.concatenate([g2, g2], axis=1), -jnp.inf))
    w_inter = jnp.exp2(m * LOG2E - g2)
    s = (s_raw * w_intra).astype(BF16)
    nd = _dot(s, v_aug) + jnp.concatenate([w_inter, w_inter], axis=1) * _dot(q, CN.astype(BF16))
    num, den = nd[:, 0:M_HEAD_DIM], nd[:, M_HEAD_DIM:2 * M_HEAD_DIM]
    h = num / jnp.maximum(jnp.abs(den), jnp.exp2(-mt2))
    CN_new = decay * CN + _dot((kt * w_key_row).astype(BF16), v_aug)
    return h, CN_new


def _mlstm_kernel(has_init, emit_state, n_blocks, heads, seqs, *refs):
    q_ref, kt_ref, v_ref, mo_ref, stats_ref, ng_ref, spread_ref = refs[:7]
    pos = 7
    if has_init:
        c0_ref, n0_ref, m0_ref = refs[pos:pos + 3]
        pos += 3
    hm_ref = refs[pos]
    pos += 1
    if emit_state:
        c_ref, n_ref, m_ref = refs[pos:pos + 3]
        pos += 3

    L, Dh = M_BLOCK, M_HEAD_DIM
    t_idx = lax.broadcasted_iota(jnp.int32, (L, L), 0)
    s_idx = lax.broadcasted_iota(jnp.int32, (L, L), 1)
    allow = (s_idx <= t_idx, s_idx >= t_idx)
    spread = spread_ref[...]
    ones = jnp.ones((L, Dh), BF16)

    def time_lanes(sq, c):
        start = (sq * n_blocks + c) * L
        return slice(start, start + L)

    def lane_broadcast_n(n_row):
        return jnp.broadcast_to(n_row, (Dh, Dh)).T

    def init_state(sq, j, d):
        if has_init:
            return (jnp.concatenate([c0_ref[sq, d, j], lane_broadcast_n(n0_ref[sq, d, j])], axis=1),
                    m0_ref[sq, j, d:d + 1, 0:1])
        return jnp.zeros((Dh, 2 * Dh), F32), jnp.zeros((1, 1), F32)

    def gate_rows(sq, j, c, d, m):
        head = j if heads == M_HEADS else pl.program_id(1) * heads + j
        r, lanes = pl.ds(d * M_HEADS + head, 1), time_lanes(sq, c)
        return _mlstm_gate_rows(stats_ref[0, r, lanes], stats_ref[1, r, lanes], stats_ref[2, r, lanes], d == 0, m)

    def blocks(sq, j, jobs, states):
        loaded, rows6, cols_in = {}, [], []
        for (c, d), (CN, m) in zip(jobs, states):
            rows6.append(gate_rows(sq, j, c, d, m))
            cols_in += [rows6[-1][0], rows6[-1][1]]
            if c not in loaded:
                rows, cols = slice(c * L, (c + 1) * L), slice(j * Dh, (j + 1) * Dh)
                q, kt = q_ref[sq, rows, cols], kt_ref[cols, time_lanes(sq, c)]
                v_aug = jnp.concatenate([v_ref[sq, rows, cols], ones], axis=1)
                loaded[c] = (_dot(q, kt.astype(BF16)), q, kt, v_aug)
        cols_out = _rows_to_lane_broadcast(cols_in, spread)
        hs, new_states = [], []
        for idx, ((c, d), (CN, m)) in enumerate(zip(jobs, states)):
            g2 = cols_out[:, (2 * idx) * LANES:(2 * idx + 1) * LANES]
            mt2 = cols_out[:, (2 * idx + 1) * LANES:(2 * idx + 2) * LANES]
            _, _, a2_row, w_key_row, decay, m_new = rows6[idx]
            h, CN_new = _mlstm_block(*loaded[c], g2, mt2, a2_row, w_key_row, decay, allow[d], CN, m)
            hs.append(h)
            new_states.append((CN_new, m_new))
        return hs, new_states

    def finish(sq, j, rows, hs):
        cols = slice(j * Dh, (j + 1) * Dh)
        hn = _rms(hs, ng_ref[j])
        hm_ref[sq, rows, cols] = (hn * jax.nn.sigmoid(mo_ref[sq, rows, cols])).astype(hm_ref.dtype)

    def emit(sq, j, d, state):
        CN, m = state
        c_ref[sq, d, j] = CN[:, 0:Dh]
        n_ref[sq, d, j] = CN[:, Dh:2 * Dh].T[0:1, :]
        m_ref[sq, j, d:d + 1, :] = jnp.broadcast_to(m, (1, LANES))

    if n_blocks > 1:
        hf_scr, hb_scr = refs[pos:pos + 2]

    for sq, j in [(sq, j) for sq in range(seqs) for j in range(heads)]:
        cols = slice(j * Dh, (j + 1) * Dh)
        states = [init_state(sq, j, 0), init_state(sq, j, 1)]
        if n_blocks == 1:
            (hf, hb), states = blocks(sq, j, [(0, 0), (0, 1)], states)
            finish(sq, j, slice(0, L), hf + hb)
        else:
            for step in range(n_blocks):
                cf, cb = step, n_blocks - 1 - step
                (hf, hb), states = blocks(sq, j, [(cf, 0), (cb, 1)], states)
                hf_scr[cf * L:(cf + 1) * L, cols] = hf
                hb_scr[cb * L:(cb + 1) * L, cols] = hb
            finish(sq, j, slice(None), hf_scr[:, cols] + hb_scr[:, cols])
        if emit_state:
            emit(sq, j, 0, states[0])
            emit(sq, j, 1, states[1])


def _mlstm(mq, mkt, mv, mo, stats, norm_g, init_state, emit_state, heads, seqs):
    B, T, _ = mq.shape
    H, Dh, L = M_HEADS, M_HEAD_DIM, M_BLOCK
    nb = T // L
    w = heads * Dh
    has_init = init_state is not None
    n_rows = 4
    spread = jnp.tile(jnp.repeat(jnp.eye(n_rows, dtype=BF16), LANES, axis=1), (3, 1))
    spread = jnp.pad(spread, ((0, 2 * SUBLANES - 3 * n_rows), (0, 0)))
    tok = pl.BlockSpec((seqs, T, w), lambda b, h: (b, 0, h))
    assert mkt.shape == (B // seqs, M_WIDTH, seqs * T) and stats.shape == (B // seqs, 3, 2 * H, seqs * T)
    in_specs = [tok, pl.BlockSpec((None, w, seqs * T), lambda b, h: (b, h, 0)), tok, tok,
                pl.BlockSpec((None, 3, 2 * H, seqs * T), lambda b, h: (b, 0, 0, 0)),
                pl.BlockSpec((heads, 1, Dh), lambda b, h: (h, 0, 0)),
                _const_spec((2 * SUBLANES, n_rows * LANES))]
    args = [mq, mkt, mv, mo, stats, norm_g, spread]
    state_specs = [pl.BlockSpec((seqs, 2, heads, Dh, Dh), lambda b, h: (b, 0, h, 0, 0)),
                   pl.BlockSpec((seqs, 2, heads, 1, Dh), lambda b, h: (b, 0, h, 0, 0)),
                   pl.BlockSpec((seqs, heads, 2, LANES), lambda b, h: (b, h, 0, 0))]
    if has_init:
        in_specs += state_specs
        args += list(init_state)
    out_specs = [tok]
    out_shape = [jax.ShapeDtypeStruct((B, T, M_WIDTH), BF16)]
    if emit_state:
        out_specs += state_specs
        out_shape += [jax.ShapeDtypeStruct((B, 2, H, Dh, Dh), F32),
                      jax.ShapeDtypeStruct((B, 2, H, 1, Dh), F32),
                      jax.ShapeDtypeStruct((B, H, 2, LANES), F32)]
    scratch = [] if nb == 1 else [pltpu.VMEM((T, w), F32), pltpu.VMEM((T, w), F32)]
    return _Part(functools.partial(_mlstm_kernel, has_init, emit_state, nb, heads, seqs),
                 (B // seqs, H // heads), in_specs, args, out_specs, out_shape, scratch)


def _attn_kernel(has_ctx, seqs, *refs):
    if has_ctx:
        q_ref, k_ref, v_ref, kc_ref, vc_ref, o_ref = refs
    else:
        q_ref, k_ref, v_ref, o_ref = refs
    lane = lax.broadcasted_iota(jnp.int32, (1, LANES), 1)
    ones = lambda n: jnp.ones((n, LANES), BF16)
    tq = q_ref.shape[1]
    sub = min(Q_SUBTILE, tq)
    for sq, r0, pair in [(sq, r0, pair) for sq in range(seqs) for r0 in range(0, tq, sub)
                         for pair in range(A_HEADS // 2)]:
        rows = slice(r0, r0 + sub)
        vsl = slice(pair * LANES, (pair + 1) * LANES)
        v_aug = jnp.concatenate([v_ref[sq, :, vsl], ones(v_ref.shape[1])], axis=1)
        if has_ctx:
            vc_aug = jnp.concatenate([vc_ref[sq, :, vsl], ones(vc_ref.shape[1])], axis=1)
        outs = []
        for e in range(2):
            hsl = slice((2 * pair + e) * HEAD_PAD, (2 * pair + e + 1) * HEAD_PAD)
            qh = q_ref[sq, rows, hsl]
            s = _dot_nt(qh, k_ref[sq, :, hsl])
            mx = jnp.max(s, axis=1, keepdims=True)
            if has_ctx:
                sc = _dot_nt(qh, kc_ref[sq, :, hsl])
                mx = jnp.maximum(mx, jnp.max(sc, axis=1, keepdims=True))
            od = _dot(jnp.exp2(s - mx).astype(BF16), v_aug)
            if has_ctx:
                od = od + _dot(jnp.exp2(sc - mx).astype(BF16), vc_aug)
            outs.append(od[:, 0:LANES] / od[:, LANES:2 * LANES])
        o_ref[sq, rows, vsl] = jnp.where(lane < A_VDIM, outs[0], outs[1]).astype(o_ref.dtype)


def _attn(q, k, v, ctx_kv, seqs):
    B, T, _ = q.shape
    tq = min(Q_TILE, T)
    has_ctx = ctx_kv is not None
    full = lambda n, w: pl.BlockSpec((seqs, n, w), lambda b, i: (b, 0, 0))
    in_specs = [pl.BlockSpec((seqs, tq, QK_PAD), lambda b, i: (b, i, 0)), full(T, QK_PAD), full(T, A_WIDTH)]
    args = [q, k, v]
    if has_ctx:
        P = ctx_kv[0].shape[1]
        in_specs += [full(P, QK_PAD), full(P, A_WIDTH)]
        args += list(ctx_kv)
    return _Part(functools.partial(_attn_kernel, has_ctx, seqs), (B // seqs, T // tq), in_specs, args,
                 [pl.BlockSpec((seqs, tq, A_WIDTH), lambda b, i: (b, i, 0))],
                 [jax.ShapeDtypeStruct((B, T, A_WIDTH), BF16)], [])


_Part = collections.namedtuple("_Part", "kernel grid in_specs args out_specs out_shape scratch")


def _mixers_kernel(parts_meta, *refs):
    n_in = sum(m[1] for m in parts_meta)
    n_out = sum(m[2] for m in parts_meta)
    ins, outs, scr = refs[:n_in], refs[n_in:n_in + n_out], refs[n_in + n_out:]
    i = o = c = 0
    for kernel, ni, no, nc in parts_meta:
        kernel(*ins[i:i + ni], *outs[o:o + no], *scr[c:c + nc])
        i, o, c = i + ni, o + no, c + nc


def _mixers(parts, name):
    grid = parts[0].grid
    assert all(p.grid == grid for p in parts)
    meta = tuple((p.kernel, len(p.in_specs), len(p.out_specs), len(p.scratch)) for p in parts)
    outs = pl.pallas_call(
        functools.partial(_mixers_kernel, meta),
        grid=grid,
        in_specs=[s for p in parts for s in p.in_specs],
        out_specs=[s for p in parts for s in p.out_specs],
        out_shape=[s for p in parts for s in p.out_shape],
        scratch_shapes=[s for p in parts for s in p.scratch],
        compiler_params=_params(("parallel", "parallel")),
        name=name,
    )(*[a for p in parts for a in p.args])
    split, o = [], 0
    for p in parts:
        split.append(outs[o:o + len(p.out_specs)])
        o += len(p.out_specs)
    return split


def _post_kernel(mod_row0, mod_per_batch, x_ref, hm_ref, ha_ref, mod_ref, g2_ref, wout_ref, wup_ref, wdown_ref,
                 y_ref):
    mod = _mod_row(mod_ref, mod_row0, mod_per_batch)
    gate1 = mod[:, 2 * D_MODEL:3 * D_MODEL]
    sh2 = mod[:, 3 * D_MODEL:4 * D_MODEL]
    sc2 = mod[:, 4 * D_MODEL:5 * D_MODEL]
    gate2 = mod[:, 5 * D_MODEL:6 * D_MODEL]
    mix = jnp.concatenate([hm_ref[...], ha_ref[...]], axis=-1)
    x1 = x_ref[...] + gate1 * _dot(mix, wout_ref[...])
    h2 = (_rms(x1, g2_ref[...]) * (1.0 + sc2) + sh2).astype(BF16)
    acc = jnp.zeros_like(x1)
    for c in range(D_FF // FF_TILE):
        sl = slice(c * FF_TILE, (c + 1) * FF_TILE)
        u = jnp.maximum(_dot(h2, wup_ref[:, sl]), 0.0)
        acc = acc + _dot((u * u).astype(BF16), wdown_ref[sl, :])
    y_ref[...] = x1 + gate2 * acc


def _post(x, hm, ha, mod3, mod_row0, mod_per_batch, wts):
    shape = x.shape
    if not mod_per_batch:
        x, hm, ha = (a.reshape(1, -1, a.shape[-1]) for a in (x, hm, ha))
    B, T, _ = x.shape
    tm = POST_TILE
    tok = lambda w: pl.BlockSpec((None, tm, w), lambda b, i: (b, i, 0))
    return _post_call(x, hm, ha, mod3, mod_row0, mod_per_batch, wts, B, T, tm, tok).reshape(shape)


def _post_call(x, hm, ha, mod3, mod_row0, mod_per_batch, wts, B, T, tm, tok):
    return pl.pallas_call(
        functools.partial(_post_kernel, mod_row0, mod_per_batch),
        grid=(B, T // tm),
        in_specs=[tok(D_MODEL), tok(M_WIDTH), tok(A_WIDTH),
                  _const_spec((8, 6 * D_MODEL)),
                  _const_spec((1, D_MODEL)),
                  _const_spec((M_WIDTH + A_WIDTH, D_MODEL)),
                  _const_spec((D_MODEL, D_FF)),
                  _const_spec((D_FF, D_MODEL))],
        out_specs=tok(D_MODEL),
        out_shape=jax.ShapeDtypeStruct((B, T, D_MODEL), F32),
        compiler_params=_params(("parallel", "parallel")),
        name="post",
    )(x, hm, ha, mod3, wts["g2"], wts["w_out"], wts["w_up"], wts["w_down"])


def _prepare_weights(norm1_g, norm2_g, w_in, mlstm_gate_b, q_lora_g, kv_lora_g, w_q_up, w_kv_up,
                     q_head_g, k_head_g, w_out, w_mlp_up, w_mlp_down):
    o_g = 4 * M_WIDTH
    o_q = o_g + N_GATES
    o_kv = o_q + Q_LORA
    o_kr = o_kv + KV_LORA
    half = A_ROPE // 2
    n_dh = 2 * M_HEADS
    wt = w_in.T.astype(BF16)
    w_gate = wt[o_g:o_q].reshape(2, 2, M_HEADS, D_MODEL)
    bias = mlstm_gate_b.reshape(2, 2, M_HEADS)

    def gate_tile(which, rope_rows):
        return jnp.concatenate([
            w_gate[:, which].reshape(n_dh, D_MODEL), jnp.zeros((A_NOPE - n_dh, D_MODEL), BF16),
            rope_rows, jnp.zeros((LANES - A_QK, D_MODEL), BF16)], axis=0)

    def rot_partner(a):
        z = jnp.zeros(a.shape[:-1] + (A_NOPE,), a.dtype)
        return jnp.concatenate([z, a[..., A_NOPE + half:A_QK], a[..., A_NOPE:A_NOPE + half]], axis=-1)

    pad_tile = lambda a: jnp.pad(a, [(0, 0)] * (a.ndim - 1) + [(0, HEAD_PAD - A_QK)])
    w_kr = wt[o_kr:o_kr + A_ROPE]
    w_kr_partner = jnp.concatenate([w_kr[half:], w_kr[:half]], axis=0)
    w_lat = jnp.concatenate([wt[o_q:o_kr], gate_tile(0, w_kr), gate_tile(1, w_kr_partner)], axis=0)
    gate_bias = jnp.pad(jnp.stack([bias[:, 0, :].reshape(n_dh), bias[:, 1, :].reshape(n_dh)], axis=0),
                        ((0, 0), (0, LANES - n_dh)))
    w_q3 = w_q_up.reshape(Q_LORA, A_HEADS, A_QK)
    w_q = pad_tile(w_q3).reshape(Q_LORA, QK_PAD)
    w_q_partner = pad_tile(rot_partner(w_q3)).reshape(Q_LORA, QK_PAD)
    w_kv3 = w_kv_up.reshape(KV_LORA, A_HEADS, A_NOPE + A_VDIM)
    w_k = jnp.pad(w_kv3[:, :, :A_NOPE], ((0, 0), (0, 0), (0, HEAD_PAD - A_NOPE)))
    w_v = w_kv3[:, :, A_NOPE:]
    w_kv = jnp.concatenate([w_k.reshape(KV_LORA, QK_PAD), w_v.reshape(KV_LORA, A_WIDTH)], axis=1)
    pad_head = lambda g: jnp.stack([pad_tile(g), pad_tile(rot_partner(g))], axis=0)
    return {
        "g1": norm1_g.reshape(1, D_MODEL),
        "g2": norm2_g.reshape(1, D_MODEL),
        "w_main": jnp.concatenate([wt[0:M_WIDTH], wt[2 * M_WIDTH:o_g]], axis=0),
        "w_kt": wt[M_WIDTH:2 * M_WIDTH],
        "w_lat": w_lat,
        "gate_bias": gate_bias,
        "q_lora_g": q_lora_g.reshape(1, Q_LORA),
        "kv_lora_g": kv_lora_g.reshape(1, KV_LORA),
        "w_q": w_q.astype(BF16),
        "w_q_rot": jnp.concatenate([w_q, w_q_partner], axis=1).astype(BF16),
        "w_kv": w_kv.astype(BF16),
        "q_head_g": pad_head(q_head_g),
        "k_head_g": pad_head(k_head_g),
        "w_out_f32": w_out,
        "w_up_f32": w_mlp_up,
        "w_down_f32": w_mlp_down,
    }


def _rope_tables(T):
    rows = T // GRID_W
    row = np.repeat(np.arange(rows, dtype=np.float32), GRID_W)
    col = np.tile(np.arange(GRID_W, dtype=np.float32), rows)
    half = A_ROPE // 2
    inv = (np.float32(ROPE_BASE) ** (-np.arange(0, half, 2, dtype=np.float32) / np.float32(half))).astype(np.float32)
    ang = np.concatenate([row[:, None] * inv, col[:, None] * inv], axis=-1)
    cos, sin = np.cos(ang), np.sin(ang)
    ones = np.ones((T, A_NOPE), np.float32)
    z = lambda w: np.zeros((T, w), np.float32)
    tail = LANES - A_QK
    cos_t = np.concatenate([ones, cos, cos, z(tail)], axis=1)
    sin_t = np.concatenate([z(A_NOPE), -sin, sin, z(tail)], axis=1)
    return jnp.asarray(np.stack([cos_t, sin_t], axis=0).astype(np.float32))


def _layer_pass(x, mod3, mod_row0, mod_per_batch, wts, norm_g, rope_tab, init_state, ctx_kv, is_context):
    pre = _pre(x, mod3, mod_row0, mod_per_batch, wts, rope_tab, emit_cache=is_context)
    mq, mkt, mv, mo, stats, q, k, v = pre[:8]
    if is_context:
        wts = dict(wts, w_out=pre[10], w_up=pre[11], w_down=pre[12])
    (ha,), ml = _mixers([
        _attn(q, k, v, ctx_kv, seqs=CONTEXT_SEQS if is_context else 1),
        _mlstm(mq, mkt, mv, mo, stats, norm_g, init_state, emit_state=is_context,
               heads=M_HEADS if is_context else LATENT_MLSTM_HEADS, seqs=CONTEXT_SEQS if is_context else 1),
    ], "mixers_context" if is_context else "mixers_latent")
    y = _post(x, ml[0], ha, mod3, mod_row0, mod_per_batch, wts)
    return y, pre[8:10], ml[1:], wts


def kernel(x_prompt, x_sample, cache_mla_ckv, cache_mla_krope, state_mlstm_C, state_mlstm_n, state_mlstm_m,
           c, c_ctx, norm1_g, norm2_g, w_ada, b_ada, w_in, mlstm_gate_b, mlstm_norm_g,
           q_lora_g, kv_lora_g, w_q_up, w_kv_up, q_head_g, k_head_g, w_out, w_mlp_up, w_mlp_down):
    depth = w_in.shape[0]
    Bd = x_sample.shape[0]
    cond8 = jnp.concatenate([c_ctx[None, :], c, jnp.zeros((8 - 1 - Bd, D_MODEL), F32)], axis=0)
    rope_tab = _rope_tables(x_sample.shape[1])

    y, z = x_prompt, x_sample
    ckvs, kropes, Cs, ns, ms = [], [], [], [], []
    for l in range(depth):
        wts = _prepare_weights(norm1_g[l], norm2_g[l], w_in[l], mlstm_gate_b[l], q_lora_g[l], kv_lora_g[l],
                               w_q_up[l], w_kv_up[l], q_head_g[l], k_head_g[l], w_out[l], w_mlp_up[l],
                               w_mlp_down[l])
        norm_g = mlstm_norm_g[l].reshape(M_HEADS, 1, M_HEAD_DIM)
        mod3 = _ada(cond8, w_ada[l], b_ada[l])
        y, (ckv, krope), (C_new, n_new, m_new), wts = _layer_pass(
            y, mod3, 0, 0, wts, norm_g, None, None, None, True)
        ckvs.append(ckv)
        kropes.append(krope)
        Cs.append(C_new)
        ns.append(n_new[:, :, :, 0, :])
        ms.append(m_new[:, :, :, 0].transpose(0, 2, 1))

        init_state = (state_mlstm_C[:, l],
                      state_mlstm_n[:, l][:, :, :, None, :],
                      jnp.broadcast_to(state_mlstm_m[:, l].transpose(0, 2, 1)[..., None],
                                       (Bd, M_HEADS, 2, LANES)))
        krope_placed = jnp.pad(cache_mla_krope[:, l], ((0, 0), (0, 0), (A_NOPE, LANES - A_QK)))
        ctx_kv = _ctxkv(cache_mla_ckv[:, l], krope_placed, wts)
        z, _, _, _ = _layer_pass(z, mod3, 1, 1, wts, norm_g, rope_tab, init_state, ctx_kv, False)

    return (y, z, jnp.stack(ckvs, axis=1), jnp.stack(kropes, axis=1), jnp.stack(Cs, axis=1),
            jnp.stack(ns, axis=1), jnp.stack(ms, axis=1))
```

```python
import collections
import functools

import jax
import jax.numpy as jnp
import numpy as np
from jax import lax
from jax.experimental import pallas as pl
from jax.experimental.pallas import tpu as pltpu

F32 = jnp.float32
BF16 = jnp.bfloat16

D_MODEL = 1024
GRID_W = 64
M_HEADS = 4
M_HEAD_DIM = 128
M_WIDTH = M_HEADS * M_HEAD_DIM
M_BLOCK = 256
A_HEADS = 8
A_NOPE = 64
A_ROPE = 32
A_QK = A_NOPE + A_ROPE
A_VDIM = 64
A_WIDTH = A_HEADS * A_VDIM
Q_LORA = 384
KV_LORA = 256
ROPE_BASE = 10000.0
D_FF = 4 * D_MODEL
EPS = 1e-6

LANES = 128
SUBLANES = 8
LOG2E = 1.4426950408889634
HEAD_PAD = LANES
QK_PAD = A_HEADS * HEAD_PAD
N_GATES = 4 * M_HEADS
LAT_WIDTH = Q_LORA + KV_LORA + 2 * LANES
VMEM_LIMIT = 56 * 1024 * 1024

TOKEN_TILE = 512
POST_TILE = 512
CONTEXT_SEQS = 2
LATENT_MLSTM_HEADS = 2
Q_TILE = 1024
Q_SUBTILE = 256
ADA_TILE_N = 1536
FF_TILE = 1024


def _dot(a, b):
    return jnp.dot(a, b, preferred_element_type=F32)


def _dot_nt(a, b):
    return lax.dot_general(a, b, (((1,), (1,)), ((), ())), preferred_element_type=F32)


def _dot_tn(a, b):
    return lax.dot_general(a, b, (((0,), (0,)), ((), ())), preferred_element_type=F32)


def _rms(x, g):
    y = x * lax.rsqrt(jnp.mean(x * x, axis=-1, keepdims=True) + EPS)
    return y * g


def _params(sem):
    return pltpu.CompilerParams(dimension_semantics=sem, vmem_limit_bytes=VMEM_LIMIT)


def _const_spec(shape):
    zeros = (0,) * len(shape)
    return pl.BlockSpec(shape, lambda *_: zeros, pipeline_mode=pl.Buffered(1))


def _ada_kernel(cond_ref, w_ref, b_ref, o_ref):
    c = cond_ref[...]
    s = (c * jax.nn.sigmoid(c)).astype(BF16)
    o_ref[...] = _dot(s, w_ref[...].astype(BF16)) + b_ref[...]


def _ada(cond8, w_ada, b_ada):
    n = w_ada.shape[1]
    return pl.pallas_call(
        _ada_kernel,
        grid=(n // ADA_TILE_N,),
        in_specs=[
            pl.BlockSpec((8, D_MODEL), lambda j: (0, 0)),
            pl.BlockSpec((D_MODEL, ADA_TILE_N), lambda j: (0, j)),
            pl.BlockSpec((1, ADA_TILE_N), lambda j: (0, j)),
        ],
        out_specs=pl.BlockSpec((8, ADA_TILE_N), lambda j: (0, j)),
        out_shape=jax.ShapeDtypeStruct((8, n), F32),
        compiler_params=_params(("parallel",)),
        name="ada",
    )(cond8, w_ada, b_ada.reshape(1, n))


def _write_heads(src, extra, g_pad, rot, dst_ref):
    for h in range(A_HEADS):
        sl = slice(h * HEAD_PAD, (h + 1) * HEAD_PAD)
        xh = src[:, sl]
        if extra is not None:
            xh = xh + extra
        ss = jnp.sum(xh * xh, axis=-1, keepdims=True) * (1.0 / A_QK)
        r = lax.rsqrt(ss + EPS)
        if rot is None:
            y = xh * r * g_pad
        else:
            partner, cos_g, sin_g = rot
            ph = partner if partner.shape[1] == HEAD_PAD else partner[:, sl]
            y = (xh * cos_g + ph * sin_g) * r
        dst_ref[:, sl] = y.astype(dst_ref.dtype)


def _time_scan(x, op, identity, reverse):
    n = x.shape[0]
    row = lax.broadcasted_iota(jnp.int32, x.shape, 0)
    shift = 1
    while shift < n:
        if shift < SUBLANES:
            if reverse:
                moved = jnp.where(row < n - shift, pltpu.roll(x, n - shift, 0), identity)
            else:
                moved = jnp.where(row >= shift, pltpu.roll(x, shift, 0), identity)
        else:
            fill = jnp.full((shift, x.shape[1]), identity, x.dtype)
            moved = (jnp.concatenate([x[shift:], fill], axis=0) if reverse
                     else jnp.concatenate([fill, x[:n - shift]], axis=0))
        x = op(x, moved)
        shift *= 2
    return x


def _mod_row(mod_ref, row0, per_batch):
    if per_batch:
        return mod_ref[pl.ds(row0 + pl.program_id(0) * per_batch, 1), :]
    return mod_ref[row0:row0 + 1, :]


def _pre_kernel(has_rope, emit_cache, mod_row0, mod_per_batch, *refs):
    (x_ref, mod_ref, g1_ref, wmain_ref, wkt_ref, wlat_ref, gbias_ref, qlg_ref, kvg_ref, wq_ref, wkv_ref,
     qhg_ref, khg_ref) = refs[:13]
    pos = 13
    if has_rope:
        rope_ref = refs[pos]
        pos += 1
    if emit_cache:
        cast_in = refs[pos:pos + 3]
        pos += 3
    (mq_ref, mkt_ref, mv_ref, mo_ref, stats_ref, q_ref, k_ref, v_ref) = refs[pos:pos + 8]
    pos += 8

    x = x_ref[...]
    mod = _mod_row(mod_ref, mod_row0, mod_per_batch)
    sh1 = mod[:, 0:D_MODEL]
    sc1 = mod[:, D_MODEL:2 * D_MODEL]
    h = _rms(x, g1_ref[...]) * (1.0 + sc1) + sh1
    hb = h.astype(BF16)

    plat = _dot_nt(hb, wlat_ref[...])
    q_lat = plat[:, 0:Q_LORA]
    kv_lat = plat[:, Q_LORA:Q_LORA + KV_LORA]
    tail = plat[:, Q_LORA + KV_LORA:Q_LORA + KV_LORA + LANES]
    tail2 = plat[:, LAT_WIDTH - LANES:LAT_WIDTH]

    lane = lax.broadcasted_iota(jnp.int32, (1, LANES), 1)
    fwd = lane < M_HEADS
    gate_i = tail + gbias_ref[0:1, :]
    gate_f = tail2 + gbias_ref[1:2, :]
    log_f = jnp.minimum(gate_f, 0.0) - jnp.log1p(jnp.exp(-jnp.abs(gate_f)))
    def block_scan(v, op, identity):
        parts = [v[r:r + M_BLOCK] for r in range(0, v.shape[0], M_BLOCK)]
        return jnp.where(fwd, jnp.concatenate([_time_scan(p, op, identity, False) for p in parts], axis=0),
                         jnp.concatenate([_time_scan(p, op, identity, True) for p in parts], axis=0))

    b = block_scan(log_f, jnp.add, 0.0)
    a = gate_i - b
    amax = block_scan(a, jnp.maximum, -jnp.inf)
    stats_ref[0] = b.T[0:2 * M_HEADS, :]
    stats_ref[1] = a.T[0:2 * M_HEADS, :]
    stats_ref[2] = amax.T[0:2 * M_HEADS, :]

    krope_placed = jnp.where((lane >= A_NOPE) & (lane < A_QK), tail, 0.0)
    ckv = _rms(kv_lat, kvg_ref[...])
    qn = _rms(q_lat, qlg_ref[...])
    qf = _dot(qn.astype(BF16), wq_ref[...])
    kvf = _dot(ckv.astype(BF16), wkv_ref[...])
    v_ref[...] = kvf[:, QK_PAD:QK_PAD + A_WIDTH].astype(BF16)
    q_rot = k_rot = None
    qhg = qhg_ref[...] * (A_QK ** -0.5 * LOG2E)
    if has_rope:
        cos_t, sin_t = rope_ref[0], rope_ref[1]
        q_rot = (qf[:, QK_PAD:2 * QK_PAD], cos_t * qhg[0:1, :], sin_t * qhg[1:2, :])
        k_rot = (tail2, cos_t * khg_ref[0:1, :], sin_t * khg_ref[1:2, :])
    _write_heads(qf, None, qhg[0:1, :], q_rot, q_ref)
    _write_heads(kvf, krope_placed, khg_ref[0:1, :], k_rot, k_ref)

    if emit_cache:
        ckv_ref, krope_ref = refs[pos:pos + 2]
        ckv_ref[...] = ckv
        krope_ref[...] = tail[:, A_NOPE:A_QK]
        for src_ref, dst_ref in zip(cast_in, refs[pos + 2:pos + 5]):
            dst_ref[...] = src_ref[...].astype(BF16)

    pm = _dot_nt(hb, wmain_ref[...])
    mq_ref[...] = pm[:, 0:M_WIDTH].astype(BF16)
    mv_ref[...] = pm[:, M_WIDTH:2 * M_WIDTH].astype(BF16)
    mo_ref[...] = pm[:, 2 * M_WIDTH:3 * M_WIDTH]
    mkt_ref[...] = _dot_nt(wkt_ref[...], hb) * (M_HEAD_DIM ** -0.5)


def _pre(x, mod3, mod_row0, mod_per_batch, wts, rope_tab, emit_cache):
    shape = x.shape
    tm = TOKEN_TILE
    if shape[1] < tm:
        assert not mod_per_batch and tm % shape[1] == 0
        x = x.reshape(-1, tm, shape[2])
    B, T, _ = x.shape
    has_rope = rope_tab is not None
    tok = lambda w: pl.BlockSpec((None, tm, w), lambda b, i: (b, i, 0))
    in_specs = [
        tok(D_MODEL),
        _const_spec((8, 6 * D_MODEL)),
        _const_spec((1, D_MODEL)),
        _const_spec((3 * M_WIDTH, D_MODEL)),
        _const_spec((M_WIDTH, D_MODEL)),
        _const_spec((LAT_WIDTH, D_MODEL)),
        _const_spec((2, LANES)),
        _const_spec((1, Q_LORA)),
        _const_spec((1, KV_LORA)),
        _const_spec((Q_LORA, 2 * QK_PAD if has_rope else QK_PAD)),
        _const_spec((KV_LORA, QK_PAD + A_WIDTH)),
        _const_spec((2, HEAD_PAD)),
        _const_spec((2, HEAD_PAD)),
    ]
    assert tm % M_BLOCK == 0
    args = [x, mod3, wts["g1"], wts["w_main"], wts["w_kt"], wts["w_lat"], wts["gate_bias"], wts["q_lora_g"],
            wts["kv_lora_g"], wts["w_q_rot"] if has_rope else wts["w_q"], wts["w_kv"],
            wts["q_head_g"], wts["k_head_g"]]
    if has_rope:
        in_specs.append(pl.BlockSpec((2, tm, HEAD_PAD), lambda b, i: (0, i, 0)))
        args.append(rope_tab)
    out_specs = [tok(M_WIDTH),
                 pl.BlockSpec((None, M_WIDTH, tm), lambda b, i: (b, 0, i)),
                 tok(M_WIDTH), tok(M_WIDTH),
                 pl.BlockSpec((None, 3, 2 * M_HEADS, tm), lambda b, i: (b, 0, 0, i)),
                 tok(QK_PAD), tok(QK_PAD), tok(A_WIDTH)]
    out_shape = [
        jax.ShapeDtypeStruct((B, T, M_WIDTH), BF16),
        jax.ShapeDtypeStruct((B, M_WIDTH, T), F32),
        jax.ShapeDtypeStruct((B, T, M_WIDTH), BF16),
        jax.ShapeDtypeStruct((B, T, M_WIDTH), F32),
        jax.ShapeDtypeStruct((B, 3, 2 * M_HEADS, T), F32),
        jax.ShapeDtypeStruct((B, T, QK_PAD), BF16),
        jax.ShapeDtypeStruct((B, T, QK_PAD), BF16),
        jax.ShapeDtypeStruct((B, T, A_WIDTH), BF16),
    ]
    if emit_cache:
        out_specs += [tok(KV_LORA), tok(A_ROPE)]
        out_shape += [jax.ShapeDtypeStruct((B, T, KV_LORA), F32),
                      jax.ShapeDtypeStruct((B, T, A_ROPE), F32)]
        steps, per_b = B * (T // tm), T // tm
        for w in (wts["w_out_f32"], wts["w_up_f32"], wts["w_down_f32"]):
            rows = w.shape[0] // steps
            assert rows * steps == w.shape[0] and rows % (2 * SUBLANES) == 0
            spec = pl.BlockSpec((rows, w.shape[1]), lambda b, i: (b * per_b + i, 0))
            in_specs.append(spec)
            args.append(w)
            out_specs.append(spec)
            out_shape.append(jax.ShapeDtypeStruct(w.shape, BF16))
    outs = pl.pallas_call(
        functools.partial(_pre_kernel, has_rope, emit_cache, mod_row0, mod_per_batch),
        grid=(B, T // tm),
        in_specs=in_specs,
        out_specs=out_specs,
        out_shape=out_shape,
        compiler_params=_params(("parallel", "parallel")),
        name="pre_latent" if has_rope else "pre_context",
    )(*args)
    keep = (1, 4, 10, 11, 12)
    return [o if n in keep else o.reshape(shape[:2] + o.shape[2:]) for n, o in enumerate(outs)]


def _ctxkv_kernel(ckv_ref, krp_ref, wkv_ref, khg_ref, k_ref, v_ref):
    kvf = _dot(ckv_ref[...].astype(BF16), wkv_ref[...])
    v_ref[...] = kvf[:, QK_PAD:QK_PAD + A_WIDTH].astype(BF16)
    _write_heads(kvf, krp_ref[...], khg_ref[0:1, :], None, k_ref)


def _ctxkv(ckv, krope_placed, wts):
    B, P, _ = ckv.shape
    tok = lambda w: pl.BlockSpec((None, P, w), lambda b: (b, 0, 0))
    return pl.pallas_call(
        _ctxkv_kernel,
        grid=(B,),
        in_specs=[tok(KV_LORA), tok(HEAD_PAD), _const_spec((KV_LORA, QK_PAD + A_WIDTH)),
                  _const_spec((2, HEAD_PAD))],
        out_specs=[tok(QK_PAD), tok(A_WIDTH)],
        out_shape=[jax.ShapeDtypeStruct((B, P, QK_PAD), BF16),
                   jax.ShapeDtypeStruct((B, P, A_WIDTH), BF16)],
        compiler_params=_params(("parallel",)),
        name="ctx_kv",
    )(ckv, krope_placed, wts["w_kv"], wts["k_head_g"])


def _rows_to_lane_broadcast(rows, spread):
    x = jnp.concatenate(rows, axis=0)
    p1 = x.astype(BF16)
    r1 = x - p1.astype(F32)
    p2 = r1.astype(BF16)
    p3 = (r1 - p2.astype(F32)).astype(BF16)
    pad = jnp.zeros((spread.shape[0] - 3 * len(rows), x.shape[1]), BF16)
    return _dot_tn(jnp.concatenate([p1, p2, p3, pad], axis=0), spread)


def _mlstm_gate_rows(b_row, a_row, amax_row, forward, m):
    L = b_row.shape[1]
    last = slice(L - 1, L) if forward else slice(0, 1)
    total = b_row[:, last]
    g_row = jnp.maximum(m, amax_row)
    m_new = total + jnp.maximum(m, amax_row[:, last])
    w_key_row = jnp.exp2((a_row + (total - m_new)) * LOG2E)
    decay = jnp.exp(total + m - m_new)
    return g_row * LOG2E, (b_row + g_row) * LOG2E, a_row * LOG2E, w_key_row, decay, m_new


def _mlstm_block(s_raw, q, kt, v_aug, g2, mt2, a2_row, w_key_row, decay, allow, CN, m):
    w_intra = jnp.exp2(jnp.where(allow, a2_row - jnp.concatenate([g2, g2], axis=1), -jnp.inf))
    w_inter = jnp.exp2(m * LOG2E - g2)
    s = (s_raw * w_intra).astype(BF16)
    nd = _dot(s, v_aug) + jnp.concatenate([w_inter, w_inter], axis=1) * _dot(q, CN.astype(BF16))
    num, den = nd[:, 0:M_HEAD_DIM], nd[:, M_HEAD_DIM:2 * M_HEAD_DIM]
    h = num / jnp.maximum(jnp.abs(den), jnp.exp2(-mt2))
    CN_new = decay * CN + _dot((kt * w_key_row).astype(BF16), v_aug)
    return h, CN_new


def _mlstm_kernel(has_init, emit_state, n_blocks, heads, seqs, *refs):
    q_ref, kt_ref, v_ref, mo_ref, stats_ref, ng_ref, spread_ref = refs[:7]
    pos = 7
    if has_init:
        c0_ref, n0_ref, m0_ref = refs[pos:pos + 3]
        pos += 3
    hm_ref = refs[pos]
    pos += 1
    if emit_state:
        c_ref, n_ref, m_ref = refs[pos:pos + 3]
        pos += 3

    L, Dh = M_BLOCK, M_HEAD_DIM
    t_idx = lax.broadcasted_iota(jnp.int32, (L, L), 0)
    s_idx = lax.broadcasted_iota(jnp.int32, (L, L), 1)
    allow = (s_idx <= t_idx, s_idx >= t_idx)
    spread = spread_ref[...]
    ones = jnp.ones((L, Dh), BF16)

    def time_lanes(sq, c):
        start = (sq * n_blocks + c) * L
        return slice(start, start + L)

    def lane_broadcast_n(n_row):
        return jnp.broadcast_to(n_row, (Dh, Dh)).T

    def init_state(sq, j, d):
        if has_init:
            return (jnp.concatenate([c0_ref[sq, d, j], lane_broadcast_n(n0_ref[sq, d, j])], axis=1),
                    m0_ref[sq, j, d:d + 1, 0:1])
        return jnp.zeros((Dh, 2 * Dh), F32), jnp.zeros((1, 1), F32)

    def gate_rows(sq, j, c, d, m):
        head = j if heads == M_HEADS else pl.program_id(1) * heads + j
        r, lanes = pl.ds(d * M_HEADS + head, 1), time_lanes(sq, c)
        return _mlstm_gate_rows(stats_ref[0, r, lanes], stats_ref[1, r, lanes], stats_ref[2, r, lanes], d == 0, m)

    def blocks(sq, j, jobs, states):
        loaded, rows6, cols_in = {}, [], []
        for (c, d), (CN, m) in zip(jobs, states):
            rows6.append(gate_rows(sq, j, c, d, m))
            cols_in += [rows6[-1][0], rows6[-1][1]]
            if c not in loaded:
                rows, cols = slice(c * L, (c + 1) * L), slice(j * Dh, (j + 1) * Dh)
                q, kt = q_ref[sq, rows, cols], kt_ref[cols, time_lanes(sq, c)]
                v_aug = jnp.concatenate([v_ref[sq, rows, cols], ones], axis=1)
                loaded[c] = (_dot(q, kt.astype(BF16)), q, kt, v_aug)
        cols_out = _rows_to_lane_broadcast(cols_in, spread)
        hs, new_states = [], []
        for idx, ((c, d), (CN, m)) in enumerate(zip(jobs, states)):
            g2 = cols_out[:, (2 * idx) * LANES:(2 * idx + 1) * LANES]
            mt2 = cols_out[:, (2 * idx + 1) * LANES:(2 * idx + 2) * LANES]
            _, _, a2_row, w_key_row, decay, m_new = rows6[idx]
            h, CN_new = _mlstm_block(*loaded[c], g2, mt2, a2_row, w_key_row, decay, allow[d], CN, m)
            hs.append(h)
            new_states.append((CN_new, m_new))
        return hs, new_states

    def finish(sq, j, rows, hs):
        cols = slice(j * Dh, (j + 1) * Dh)
        hn = _rms(hs, ng_ref[j])
        hm_ref[sq, rows, cols] = (hn * jax.nn.sigmoid(mo_ref[sq, rows, cols])).astype(hm_ref.dtype)

    def emit(sq, j, d, state):
        CN, m = state
        c_ref[sq, d, j] = CN[:, 0:Dh]
        n_ref[sq, d, j] = CN[:, Dh:2 * Dh].T[0:1, :]
        m_ref[sq, j, d:d + 1, :] = jnp.broadcast_to(m, (1, LANES))

    if n_blocks > 1:
        hf_scr, hb_scr = refs[pos:pos + 2]

    for sq, j in [(sq, j) for sq in range(seqs) for j in range(heads)]:
        cols = slice(j * Dh, (j + 1) * Dh)
        states = [init_state(sq, j, 0), init_state(sq, j, 1)]
        if n_blocks == 1:
            (hf, hb), states = blocks(sq, j, [(0, 0), (0, 1)], states)
            finish(sq, j, slice(0, L), hf + hb)
        else:
            for step in range(n_blocks):
                cf, cb = step, n_blocks - 1 - step
                (hf, hb), states = blocks(sq, j, [(cf, 0), (cb, 1)], states)
                hf_scr[cf * L:(cf + 1) * L, cols] = hf
                hb_scr[cb * L:(cb + 1) * L, cols] = hb
                yield
            finish(sq, j, slice(None), hf_scr[:, cols] + hb_scr[:, cols])
        if emit_state:
            emit(sq, j, 0, states[0])
            emit(sq, j, 1, states[1])
        yield


def _mlstm(mq, mkt, mv, mo, stats, norm_g, init_state, emit_state, heads, seqs):
    B, T, _ = mq.shape
    H, Dh, L = M_HEADS, M_HEAD_DIM, M_BLOCK
    nb = T // L
    w = heads * Dh
    has_init = init_state is not None
    n_rows = 4
    spread = jnp.tile(jnp.repeat(jnp.eye(n_rows, dtype=BF16), LANES, axis=1), (3, 1))
    spread = jnp.pad(spread, ((0, 2 * SUBLANES - 3 * n_rows), (0, 0)))
    tok = pl.BlockSpec((seqs, T, w), lambda b, h: (b, 0, h))
    assert mkt.shape == (B // seqs, M_WIDTH, seqs * T) and stats.shape == (B // seqs, 3, 2 * H, seqs * T)
    in_specs = [tok, pl.BlockSpec((None, w, seqs * T), lambda b, h: (b, h, 0)), tok, tok,
                pl.BlockSpec((None, 3, 2 * H, seqs * T), lambda b, h: (b, 0, 0, 0)),
                pl.BlockSpec((heads, 1, Dh), lambda b, h: (h, 0, 0)),
                _const_spec((2 * SUBLANES, n_rows * LANES))]
    args = [mq, mkt, mv, mo, stats, norm_g, spread]
    state_specs = [pl.BlockSpec((seqs, 2, heads, Dh, Dh), lambda b, h: (b, 0, h, 0, 0)),
                   pl.BlockSpec((seqs, 2, heads, 1, Dh), lambda b, h: (b, 0, h, 0, 0)),
                   pl.BlockSpec((seqs, heads, 2, LANES), lambda b, h: (b, h, 0, 0))]
    if has_init:
        in_specs += state_specs
        args += list(init_state)
    out_specs = [tok]
    out_shape = [jax.ShapeDtypeStruct((B, T, M_WIDTH), BF16)]
    if emit_state:
        out_specs += state_specs
        out_shape += [jax.ShapeDtypeStruct((B, 2, H, Dh, Dh), F32),
                      jax.ShapeDtypeStruct((B, 2, H, 1, Dh), F32),
                      jax.ShapeDtypeStruct((B, H, 2, LANES), F32)]
    scratch = [] if nb == 1 else [pltpu.VMEM((T, w), F32), pltpu.VMEM((T, w), F32)]
    return _Part(functools.partial(_mlstm_kernel, has_init, emit_state, nb, heads, seqs),
                 (B // seqs, H // heads), in_specs, args, out_specs, out_shape, scratch)


def _attn_kernel(has_ctx, seqs, *refs):
    if has_ctx:
        q_ref, k_ref, v_ref, kc_ref, vc_ref, o_ref = refs
    else:
        q_ref, k_ref, v_ref, o_ref = refs
    lane = lax.broadcasted_iota(jnp.int32, (1, LANES), 1)
    ones = lambda n: jnp.ones((n, LANES), BF16)
    tq = q_ref.shape[1]
    sub = min(Q_SUBTILE, tq)
    for sq, r0, pair in [(sq, r0, pair) for sq in range(seqs) for r0 in range(0, tq, sub)
                         for pair in range(A_HEADS // 2)]:
        rows = slice(r0, r0 + sub)
        vsl = slice(pair * LANES, (pair + 1) * LANES)
        v_aug = jnp.concatenate([v_ref[sq, :, vsl], ones(v_ref.shape[1])], axis=1)
        if has_ctx:
            vc_aug = jnp.concatenate([vc_ref[sq, :, vsl], ones(vc_ref.shape[1])], axis=1)
        outs = []
        for e in range(2):
            hsl = slice((2 * pair + e) * HEAD_PAD, (2 * pair + e + 1) * HEAD_PAD)
            qh = q_ref[sq, rows, hsl]
            s = _dot_nt(qh, k_ref[sq, :, hsl])
            mx = jnp.max(s, axis=1, keepdims=True)
            if has_ctx:
                sc = _dot_nt(qh, kc_ref[sq, :, hsl])
                mx = jnp.maximum(mx, jnp.max(sc, axis=1, keepdims=True))
            od = _dot(jnp.exp2(s - mx).astype(BF16), v_aug)
            if has_ctx:
                od = od + _dot(jnp.exp2(sc - mx).astype(BF16), vc_aug)
            outs.append(od[:, 0:LANES] / od[:, LANES:2 * LANES])
        o_ref[sq, rows, vsl] = jnp.where(lane < A_VDIM, outs[0], outs[1]).astype(o_ref.dtype)
        yield


def _attn(q, k, v, ctx_kv, seqs):
    B, T, _ = q.shape
    tq = min(Q_TILE, T)
    has_ctx = ctx_kv is not None
    full = lambda n, w: pl.BlockSpec((seqs, n, w), lambda b, i: (b, 0, 0))
    in_specs = [pl.BlockSpec((seqs, tq, QK_PAD), lambda b, i: (b, i, 0)), full(T, QK_PAD), full(T, A_WIDTH)]
    args = [q, k, v]
    if has_ctx:
        P = ctx_kv[0].shape[1]
        in_specs += [full(P, QK_PAD), full(P, A_WIDTH)]
        args += list(ctx_kv)
    return _Part(functools.partial(_attn_kernel, has_ctx, seqs), (B // seqs, T // tq), in_specs, args,
                 [pl.BlockSpec((seqs, tq, A_WIDTH), lambda b, i: (b, i, 0))],
                 [jax.ShapeDtypeStruct((B, T, A_WIDTH), BF16)], [])


_Part = collections.namedtuple("_Part", "kernel grid in_specs args out_specs out_shape scratch")
_DONE = object()


def _mixers_kernel(parts_meta, *refs):
    n_in = sum(m[1] for m in parts_meta)
    n_out = sum(m[2] for m in parts_meta)
    ins, outs, scr = refs[:n_in], refs[n_in:n_in + n_out], refs[n_in + n_out:]
    i = o = c = 0
    running = []
    for kernel, ni, no, nc in parts_meta:
        running.append(kernel(*ins[i:i + ni], *outs[o:o + no], *scr[c:c + nc]))
        i, o, c = i + ni, o + no, c + nc
    while running:
        running = [g for g in running if next(g, _DONE) is not _DONE]


def _mixers(parts, name):
    grid = parts[0].grid
    assert all(p.grid == grid for p in parts)
    meta = tuple((p.kernel, len(p.in_specs), len(p.out_specs), len(p.scratch)) for p in parts)
    outs = pl.pallas_call(
        functools.partial(_mixers_kernel, meta),
        grid=grid,
        in_specs=[s for p in parts for s in p.in_specs],
        out_specs=[s for p in parts for s in p.out_specs],
        out_shape=[s for p in parts for s in p.out_shape],
        scratch_shapes=[s for p in parts for s in p.scratch],
        compiler_params=_params(("parallel", "parallel")),
        name=name,
    )(*[a for p in parts for a in p.args])
    split, o = [], 0
    for p in parts:
        split.append(outs[o:o + len(p.out_specs)])
        o += len(p.out_specs)
    return split


def _post_kernel(mod_row0, mod_per_batch, x_ref, hm_ref, ha_ref, mod_ref, g2_ref, wout_ref, wup_ref, wdown_ref,
                 y_ref):
    mod = _mod_row(mod_ref, mod_row0, mod_per_batch)
    gate1 = mod[:, 2 * D_MODEL:3 * D_MODEL]
    sh2 = mod[:, 3 * D_MODEL:4 * D_MODEL]
    sc2 = mod[:, 4 * D_MODEL:5 * D_MODEL]
    gate2 = mod[:, 5 * D_MODEL:6 * D_MODEL]
    mix = jnp.concatenate([hm_ref[...], ha_ref[...]], axis=-1)
    x1 = x_ref[...] + gate1 * _dot(mix, wout_ref[...])
    h2 = (_rms(x1, g2_ref[...]) * (1.0 + sc2) + sh2).astype(BF16)
    acc = jnp.zeros_like(x1)
    for c in range(D_FF // FF_TILE):
        sl = slice(c * FF_TILE, (c + 1) * FF_TILE)
        u = jnp.maximum(_dot(h2, wup_ref[:, sl]), 0.0)
        acc = acc + _dot((u * u).astype(BF16), wdown_ref[sl, :])
    y_ref[...] = x1 + gate2 * acc


def _post(x, hm, ha, mod3, mod_row0, mod_per_batch, wts):
    shape = x.shape
    if not mod_per_batch:
        x, hm, ha = (a.reshape(1, -1, a.shape[-1]) for a in (x, hm, ha))
    B, T, _ = x.shape
    tm = POST_TILE
    tok = lambda w: pl.BlockSpec((None, tm, w), lambda b, i: (b, i, 0))
    return _post_call(x, hm, ha, mod3, mod_row0, mod_per_batch, wts, B, T, tm, tok).reshape(shape)


def _post_call(x, hm, ha, mod3, mod_row0, mod_per_batch, wts, B, T, tm, tok):
    return pl.pallas_call(
        functools.partial(_post_kernel, mod_row0, mod_per_batch),
        grid=(B, T // tm),
        in_specs=[tok(D_MODEL), tok(M_WIDTH), tok(A_WIDTH),
                  _const_spec((8, 6 * D_MODEL)),
                  _const_spec((1, D_MODEL)),
                  _const_spec((M_WIDTH + A_WIDTH, D_MODEL)),
                  _const_spec((D_MODEL, D_FF)),
                  _const_spec((D_FF, D_MODEL))],
        out_specs=tok(D_MODEL),
        out_shape=jax.ShapeDtypeStruct((B, T, D_MODEL), F32),
        compiler_params=_params(("parallel", "parallel")),
        name="post",
    )(x, hm, ha, mod3, wts["g2"], wts["w_out"], wts["w_up"], wts["w_down"])


def _prepare_weights(norm1_g, norm2_g, w_in, mlstm_gate_b, q_lora_g, kv_lora_g, w_q_up, w_kv_up,
                     q_head_g, k_head_g, w_out, w_mlp_up, w_mlp_down):
    o_g = 4 * M_WIDTH
    o_q = o_g + N_GATES
    o_kv = o_q + Q_LORA
    o_kr = o_kv + KV_LORA
    half = A_ROPE // 2
    n_dh = 2 * M_HEADS
    wt = w_in.T.astype(BF16)
    w_gate = wt[o_g:o_q].reshape(2, 2, M_HEADS, D_MODEL)
    bias = mlstm_gate_b.reshape(2, 2, M_HEADS)

    def gate_tile(which, rope_rows):
        return jnp.concatenate([
            w_gate[:, which].reshape(n_dh, D_MODEL), jnp.zeros((A_NOPE - n_dh, D_MODEL), BF16),
            rope_rows, jnp.zeros((LANES - A_QK, D_MODEL), BF16)], axis=0)

    def rot_partner(a):
        z = jnp.zeros(a.shape[:-1] + (A_NOPE,), a.dtype)
        return jnp.concatenate([z, a[..., A_NOPE + half:A_QK], a[..., A_NOPE:A_NOPE + half]], axis=-1)

    pad_tile = lambda a: jnp.pad(a, [(0, 0)] * (a.ndim - 1) + [(0, HEAD_PAD - A_QK)])
    w_kr = wt[o_kr:o_kr + A_ROPE]
    w_kr_partner = jnp.concatenate([w_kr[half:], w_kr[:half]], axis=0)
    w_lat = jnp.concatenate([wt[o_q:o_kr], gate_tile(0, w_kr), gate_tile(1, w_kr_partner)], axis=0)
    gate_bias = jnp.pad(jnp.stack([bias[:, 0, :].reshape(n_dh), bias[:, 1, :].reshape(n_dh)], axis=0),
                        ((0, 0), (0, LANES - n_dh)))
    w_q3 = w_q_up.reshape(Q_LORA, A_HEADS, A_QK)
    w_q = pad_tile(w_q3).reshape(Q_LORA, QK_PAD)
    w_q_partner = pad_tile(rot_partner(w_q3)).reshape(Q_LORA, QK_PAD)
    w_kv3 = w_kv_up.reshape(KV_LORA, A_HEADS, A_NOPE + A_VDIM)
    w_k = jnp.pad(w_kv3[:, :, :A_NOPE], ((0, 0), (0, 0), (0, HEAD_PAD - A_NOPE)))
    w_v = w_kv3[:, :, A_NOPE:]
    w_kv = jnp.concatenate([w_k.reshape(KV_LORA, QK_PAD), w_v.reshape(KV_LORA, A_WIDTH)], axis=1)
    pad_head = lambda g: jnp.stack([pad_tile(g), pad_tile(rot_partner(g))], axis=0)
    return {
        "g1": norm1_g.reshape(1, D_MODEL),
        "g2": norm2_g.reshape(1, D_MODEL),
        "w_main": jnp.concatenate([wt[0:M_WIDTH], wt[2 * M_WIDTH:o_g]], axis=0),
        "w_kt": wt[M_WIDTH:2 * M_WIDTH],
        "w_lat": w_lat,
        "gate_bias": gate_bias,
        "q_lora_g": q_lora_g.reshape(1, Q_LORA),
        "kv_lora_g": kv_lora_g.reshape(1, KV_LORA),
        "w_q": w_q.astype(BF16),
        "w_q_rot": jnp.concatenate([w_q, w_q_partner], axis=1).astype(BF16),
        "w_kv": w_kv.astype(BF16),
        "q_head_g": pad_head(q_head_g),
        "k_head_g": pad_head(k_head_g),
        "w_out_f32": w_out,
        "w_up_f32": w_mlp_up,
        "w_down_f32": w_mlp_down,
    }


def _rope_tables(T):
    rows = T // GRID_W
    row = np.repeat(np.arange(rows, dtype=np.float32), GRID_W)
    col = np.tile(np.arange(GRID_W, dtype=np.float32), rows)
    half = A_ROPE // 2
    inv = (np.float32(ROPE_BASE) ** (-np.arange(0, half, 2, dtype=np.float32) / np.float32(half))).astype(np.float32)
    ang = np.concatenate([row[:, None] * inv, col[:, None] * inv], axis=-1)
    cos, sin = np.cos(ang), np.sin(ang)
    ones = np.ones((T, A_NOPE), np.float32)
    z = lambda w: np.zeros((T, w), np.float32)
    tail = LANES - A_QK
    cos_t = np.concatenate([ones, cos, cos, z(tail)], axis=1)
    sin_t = np.concatenate([z(A_NOPE), -sin, sin, z(tail)], axis=1)
    return jnp.asarray(np.stack([cos_t, sin_t], axis=0).astype(np.float32))


def _layer_pass(x, mod3, mod_row0, mod_per_batch, wts, norm_g, rope_tab, init_state, ctx_kv, is_context):
    pre = _pre(x, mod3, mod_row0, mod_per_batch, wts, rope_tab, emit_cache=is_context)
    mq, mkt, mv, mo, stats, q, k, v = pre[:8]
    if is_context:
        wts = dict(wts, w_out=pre[10], w_up=pre[11], w_down=pre[12])
    (ha,), ml = _mixers([
        _attn(q, k, v, ctx_kv, seqs=CONTEXT_SEQS if is_context else 1),
        _mlstm(mq, mkt, mv, mo, stats, norm_g, init_state, emit_state=is_context,
               heads=M_HEADS if is_context else LATENT_MLSTM_HEADS, seqs=CONTEXT_SEQS if is_context else 1),
    ], "mixers_context" if is_context else "mixers_latent")
    y = _post(x, ml[0], ha, mod3, mod_row0, mod_per_batch, wts)
    return y, pre[8:10], ml[1:], wts


def kernel(x_prompt, x_sample, cache_mla_ckv, cache_mla_krope, state_mlstm_C, state_mlstm_n, state_mlstm_m,
           c, c_ctx, norm1_g, norm2_g, w_ada, b_ada, w_in, mlstm_gate_b, mlstm_norm_g,
           q_lora_g, kv_lora_g, w_q_up, w_kv_up, q_head_g, k_head_g, w_out, w_mlp_up, w_mlp_down):
    depth = w_in.shape[0]
    Bd = x_sample.shape[0]
    cond8 = jnp.concatenate([c_ctx[None, :], c, jnp.zeros((8 - 1 - Bd, D_MODEL), F32)], axis=0)
    rope_tab = _rope_tables(x_sample.shape[1])

    y, z = x_prompt, x_sample
    ckvs, kropes, Cs, ns, ms = [], [], [], [], []
    for l in range(depth):
        wts = _prepare_weights(norm1_g[l], norm2_g[l], w_in[l], mlstm_gate_b[l], q_lora_g[l], kv_lora_g[l],
                               w_q_up[l], w_kv_up[l], q_head_g[l], k_head_g[l], w_out[l], w_mlp_up[l],
                               w_mlp_down[l])
        norm_g = mlstm_norm_g[l].reshape(M_HEADS, 1, M_HEAD_DIM)
        mod3 = _ada(cond8, w_ada[l], b_ada[l])
        y, (ckv, krope), (C_new, n_new, m_new), wts = _layer_pass(
            y, mod3, 0, 0, wts, norm_g, None, None, None, True)
        ckvs.append(ckv)
        kropes.append(krope)
        Cs.append(C_new)
        ns.append(n_new[:, :, :, 0, :])
        ms.append(m_new[:, :, :, 0].transpose(0, 2, 1))

        init_state = (state_mlstm_C[:, l],
                      state_mlstm_n[:, l][:, :, :, None, :],
                      jnp.broadcast_to(state_mlstm_m[:, l].transpose(0, 2, 1)[..., None],
                                       (Bd, M_HEADS, 2, LANES)))
        krope_placed = jnp.pad(cache_mla_krope[:, l], ((0, 0), (0, 0), (A_NOPE, LANES - A_QK)))
        ctx_kv = _ctxkv(cache_mla_ckv[:, l], krope_placed, wts)
        z, _, _, _ = _layer_pass(z, mod3, 1, 1, wts, norm_g, rope_tab, init_state, ctx_kv, False)

    return (y, z, jnp.stack(ckvs, axis=1), jnp.stack(kropes, axis=1), jnp.stack(Cs, axis=1),
            jnp.stack(ns, axis=1), jnp.stack(ms, axis=1))
```

```python
import functools

import jax
import jax.numpy as jnp
import numpy as np
from jax import lax
from jax.experimental import pallas as pl
from jax.experimental.pallas import tpu as pltpu

F32 = jnp.float32
BF16 = jnp.bfloat16

D_MODEL = 1024
GRID_W = 64
M_HEADS = 4
M_HEAD_DIM = 128
M_WIDTH = M_HEADS * M_HEAD_DIM
M_BLOCK = 256
A_HEADS = 8
A_NOPE = 64
A_ROPE = 32
A_QK = A_NOPE + A_ROPE
A_VDIM = 64
A_WIDTH = A_HEADS * A_VDIM
Q_LORA = 384
KV_LORA = 256
ROPE_BASE = 10000.0
D_FF = 4 * D_MODEL
EPS = 1e-6

LANES = 128
SUBLANES = 8
LOG2E = 1.4426950408889634
HEAD_PAD = LANES
QK_PAD = A_HEADS * HEAD_PAD
N_GATES = 4 * M_HEADS
LAT_WIDTH = Q_LORA + KV_LORA + LANES
VMEM_LIMIT = 56 * 1024 * 1024

TOKEN_TILE = 512
POST_TILE = 512
MLSTM_CONTEXT_SEQS = 2
ATTN_CONTEXT_SEQS = 4
Q_TILE = 1024
Q_SUBTILE = 256
ADA_TILE_N = 1536
FF_TILE = 1024


def _dot(a, b):
    return jnp.dot(a, b, preferred_element_type=F32)


def _dot_nt(a, b):
    return lax.dot_general(a, b, (((1,), (1,)), ((), ())), preferred_element_type=F32)


def _dot_tn(a, b):
    return lax.dot_general(a, b, (((0,), (0,)), ((), ())), preferred_element_type=F32)


def _rms(x, g):
    y = x * lax.rsqrt(jnp.mean(x * x, axis=-1, keepdims=True) + EPS)
    return y * g


def _params(sem):
    return pltpu.CompilerParams(dimension_semantics=sem, vmem_limit_bytes=VMEM_LIMIT)


def _const_spec(shape):
    zeros = (0,) * len(shape)
    return pl.BlockSpec(shape, lambda *_: zeros, pipeline_mode=pl.Buffered(1))


def _ada_kernel(cond_ref, w_ref, b_ref, o_ref):
    c = cond_ref[...]
    s = (c * jax.nn.sigmoid(c)).astype(BF16)
    o_ref[...] = _dot(s, w_ref[...].astype(BF16)) + b_ref[...]


def _ada(cond8, w_ada, b_ada):
    n = w_ada.shape[1]
    return pl.pallas_call(
        _ada_kernel,
        grid=(n // ADA_TILE_N,),
        in_specs=[
            pl.BlockSpec((8, D_MODEL), lambda j: (0, 0)),
            pl.BlockSpec((D_MODEL, ADA_TILE_N), lambda j: (0, j)),
            pl.BlockSpec((1, ADA_TILE_N), lambda j: (0, j)),
        ],
        out_specs=pl.BlockSpec((8, ADA_TILE_N), lambda j: (0, j)),
        out_shape=jax.ShapeDtypeStruct((8, n), F32),
        compiler_params=_params(("parallel",)),
        name="ada",
    )(cond8, w_ada, b_ada.reshape(1, n))


def _write_heads(src, extra, g_pad, rot, dst_ref):
    for h in range(A_HEADS):
        sl = slice(h * HEAD_PAD, (h + 1) * HEAD_PAD)
        xh = src[:, sl]
        if extra is not None:
            xh = xh + extra
        ss = jnp.sum(xh * xh, axis=-1, keepdims=True) * (1.0 / A_QK)
        r = lax.rsqrt(ss + EPS)
        if rot is None:
            y = xh * r * g_pad
        else:
            partner, cos_g, sin_g = rot
            ph = partner if partner.shape[1] == HEAD_PAD else partner[:, sl]
            y = (xh * cos_g + ph * sin_g) * r
        dst_ref[:, sl] = y.astype(dst_ref.dtype)


def _time_scan(x, op, identity, reverse):
    n = x.shape[0]
    row = lax.broadcasted_iota(jnp.int32, x.shape, 0)
    shift = 1
    while shift < n:
        if shift < SUBLANES:
            if reverse:
                moved = jnp.where(row < n - shift, pltpu.roll(x, n - shift, 0), identity)
            else:
                moved = jnp.where(row >= shift, pltpu.roll(x, shift, 0), identity)
        else:
            fill = jnp.full((shift, x.shape[1]), identity, x.dtype)
            moved = (jnp.concatenate([x[shift:], fill], axis=0) if reverse
                     else jnp.concatenate([fill, x[:n - shift]], axis=0))
        x = op(x, moved)
        shift *= 2
    return x


def _mod_row(mod_ref, row0, per_batch):
    if per_batch:
        return mod_ref[pl.ds(row0 + pl.program_id(0) * per_batch, 1), :]
    return mod_ref[row0:row0 + 1, :]


def _pre_kernel(has_rope, emit_cache, mod_row0, mod_per_batch, *refs):
    (x_ref, mod_ref, g1_ref, wmain_ref, wkt_ref, wlat_ref, gbias_ref, qlg_ref, kvg_ref, wq_ref, wkv_ref,
     qhg_ref, khg_ref) = refs[:13]
    pos = 13
    if has_rope:
        rope_ref = refs[pos]
        pos += 1
    if emit_cache:
        cast_in = refs[pos:pos + 3]
        pos += 3
    (mq_ref, mkt_ref, mv_ref, mo_ref, stats_ref, q_ref, k_ref, v_ref) = refs[pos:pos + 8]
    pos += 8

    x = x_ref[...]
    mod = _mod_row(mod_ref, mod_row0, mod_per_batch)
    sh1 = mod[:, 0:D_MODEL]
    sc1 = mod[:, D_MODEL:2 * D_MODEL]
    h = _rms(x, g1_ref[...]) * (1.0 + sc1) + sh1
    hb = h.astype(BF16)

    plat = _dot_nt(hb, wlat_ref[...])
    q_lat = plat[:, 0:Q_LORA]
    kv_lat = plat[:, Q_LORA:Q_LORA + KV_LORA]
    tail = plat[:, Q_LORA + KV_LORA:LAT_WIDTH]
    tail_f = pltpu.roll(tail, LANES - 2 * M_HEADS, 1)
    tail2 = pltpu.roll(tail, LANES - A_ROPE, 1)

    lane = lax.broadcasted_iota(jnp.int32, (1, LANES), 1)
    fwd = lane < M_HEADS
    gate_i = tail + gbias_ref[0:1, :]
    gate_f = tail_f + gbias_ref[1:2, :]
    log_f = jnp.minimum(gate_f, 0.0) - jnp.log1p(jnp.exp(-jnp.abs(gate_f)))
    def block_scan(v, op, identity):
        parts = [v[r:r + M_BLOCK] for r in range(0, v.shape[0], M_BLOCK)]
        return jnp.where(fwd, jnp.concatenate([_time_scan(p, op, identity, False) for p in parts], axis=0),
                         jnp.concatenate([_time_scan(p, op, identity, True) for p in parts], axis=0))

    b = block_scan(log_f, jnp.add, 0.0)
    a = gate_i - b
    amax = block_scan(a, jnp.maximum, -jnp.inf)
    stats_ref[0] = b.T[0:2 * M_HEADS, :]
    stats_ref[1] = a.T[0:2 * M_HEADS, :]
    stats_ref[2] = amax.T[0:2 * M_HEADS, :]

    krope_placed = jnp.where((lane >= A_NOPE) & (lane < A_QK), tail, 0.0)
    ckv = _rms(kv_lat, kvg_ref[...])
    qn = _rms(q_lat, qlg_ref[...])
    qf = _dot(qn.astype(BF16), wq_ref[...])
    kvf = _dot(ckv.astype(BF16), wkv_ref[...])
    v_ref[...] = kvf[:, QK_PAD:QK_PAD + A_WIDTH].astype(BF16)
    q_rot = k_rot = None
    qhg = qhg_ref[...] * (A_QK ** -0.5 * LOG2E)
    if has_rope:
        cos_t, sin_t = rope_ref[0], rope_ref[1]
        q_rot = (qf[:, QK_PAD:2 * QK_PAD], cos_t * qhg[0:1, :], sin_t * qhg[1:2, :])
        k_rot = (tail2, cos_t * khg_ref[0:1, :], sin_t * khg_ref[1:2, :])
    _write_heads(qf, None, qhg[0:1, :], q_rot, q_ref)
    _write_heads(kvf, krope_placed, khg_ref[0:1, :], k_rot, k_ref)

    if emit_cache:
        ckv_ref, krope_ref = refs[pos:pos + 2]
        ckv_ref[...] = ckv
        krope_ref[...] = tail[:, A_NOPE:A_QK]
        for src_ref, dst_ref in zip(cast_in, refs[pos + 2:pos + 5]):
            dst_ref[...] = src_ref[...].astype(BF16)

    pm = _dot_nt(hb, wmain_ref[...])
    mq_ref[...] = pm[:, 0:M_WIDTH].astype(BF16)
    mv_ref[...] = pm[:, M_WIDTH:2 * M_WIDTH].astype(BF16)
    mo_ref[...] = pm[:, 2 * M_WIDTH:3 * M_WIDTH]
    mkt_ref[...] = _dot_nt(wkt_ref[...], hb) * (M_HEAD_DIM ** -0.5)


def _pre(x, mod3, mod_row0, mod_per_batch, wts, rope_tab, emit_cache):
    shape = x.shape
    tm = TOKEN_TILE
    if shape[1] < tm:
        assert not mod_per_batch and tm % shape[1] == 0
        x = x.reshape(-1, tm, shape[2])
    B, T, _ = x.shape
    has_rope = rope_tab is not None
    tok = lambda w: pl.BlockSpec((None, tm, w), lambda b, i: (b, i, 0))
    in_specs = [
        tok(D_MODEL),
        _const_spec((8, 6 * D_MODEL)),
        _const_spec((1, D_MODEL)),
        _const_spec((3 * M_WIDTH, D_MODEL)),
        _const_spec((M_WIDTH, D_MODEL)),
        _const_spec((LAT_WIDTH, D_MODEL)),
        _const_spec((2, LANES)),
        _const_spec((1, Q_LORA)),
        _const_spec((1, KV_LORA)),
        _const_spec((Q_LORA, 2 * QK_PAD if has_rope else QK_PAD)),
        _const_spec((KV_LORA, QK_PAD + A_WIDTH)),
        _const_spec((2, HEAD_PAD)),
        _const_spec((2, HEAD_PAD)),
    ]
    assert tm % M_BLOCK == 0
    args = [x, mod3, wts["g1"], wts["w_main"], wts["w_kt"], wts["w_lat"], wts["gate_bias"], wts["q_lora_g"],
            wts["kv_lora_g"], wts["w_q_rot"] if has_rope else wts["w_q"], wts["w_kv"],
            wts["q_head_g"], wts["k_head_g"]]
    if has_rope:
        in_specs.append(pl.BlockSpec((2, tm, HEAD_PAD), lambda b, i: (0, i, 0)))
        args.append(rope_tab)
    out_specs = [tok(M_WIDTH),
                 pl.BlockSpec((None, M_WIDTH, tm), lambda b, i: (b, 0, i)),
                 tok(M_WIDTH), tok(M_WIDTH),
                 pl.BlockSpec((None, 3, 2 * M_HEADS, tm), lambda b, i: (b, 0, 0, i)),
                 tok(QK_PAD), tok(QK_PAD), tok(A_WIDTH)]
    out_shape = [
        jax.ShapeDtypeStruct((B, T, M_WIDTH), BF16),
        jax.ShapeDtypeStruct((B, M_WIDTH, T), F32),
        jax.ShapeDtypeStruct((B, T, M_WIDTH), BF16),
        jax.ShapeDtypeStruct((B, T, M_WIDTH), F32),
        jax.ShapeDtypeStruct((B, 3, 2 * M_HEADS, T), F32),
        jax.ShapeDtypeStruct((B, T, QK_PAD), BF16),
        jax.ShapeDtypeStruct((B, T, QK_PAD), BF16),
        jax.ShapeDtypeStruct((B, T, A_WIDTH), BF16),
    ]
    if emit_cache:
        out_specs += [tok(KV_LORA), tok(A_ROPE)]
        out_shape += [jax.ShapeDtypeStruct((B, T, KV_LORA), F32),
                      jax.ShapeDtypeStruct((B, T, A_ROPE), F32)]
        steps, per_b = B * (T // tm), T // tm
        for w in (wts["w_out_f32"], wts["w_up_f32"], wts["w_down_f32"]):
            rows = w.shape[0] // steps
            assert rows * steps == w.shape[0] and rows % (2 * SUBLANES) == 0
            spec = pl.BlockSpec((rows, w.shape[1]), lambda b, i: (b * per_b + i, 0))
            in_specs.append(spec)
            args.append(w)
            out_specs.append(spec)
            out_shape.append(jax.ShapeDtypeStruct(w.shape, BF16))
    outs = pl.pallas_call(
        functools.partial(_pre_kernel, has_rope, emit_cache, mod_row0, mod_per_batch),
        grid=(B, T // tm),
        in_specs=in_specs,
        out_specs=out_specs,
        out_shape=out_shape,
        compiler_params=_params(("parallel", "parallel")),
        name="pre_latent" if has_rope else "pre_context",
    )(*args)
    keep = (1, 4, 10, 11, 12)
    return [o if n in keep else o.reshape(shape[:2] + o.shape[2:]) for n, o in enumerate(outs)]


def _ctxkv_kernel(ckv_ref, krp_ref, wkv_ref, khg_ref, k_ref, v_ref):
    kvf = _dot(ckv_ref[...].astype(BF16), wkv_ref[...])
    v_ref[...] = kvf[:, QK_PAD:QK_PAD + A_WIDTH].astype(BF16)
    _write_heads(kvf, krp_ref[...], khg_ref[0:1, :], None, k_ref)


def _ctxkv(ckv, krope_placed, wts):
    B, P, _ = ckv.shape
    tok = lambda w: pl.BlockSpec((None, P, w), lambda b: (b, 0, 0))
    return pl.pallas_call(
        _ctxkv_kernel,
        grid=(B,),
        in_specs=[tok(KV_LORA), tok(HEAD_PAD), _const_spec((KV_LORA, QK_PAD + A_WIDTH)),
                  _const_spec((2, HEAD_PAD))],
        out_specs=[tok(QK_PAD), tok(A_WIDTH)],
        out_shape=[jax.ShapeDtypeStruct((B, P, QK_PAD), BF16),
                   jax.ShapeDtypeStruct((B, P, A_WIDTH), BF16)],
        compiler_params=_params(("parallel",)),
        name="ctx_kv",
    )(ckv, krope_placed, wts["w_kv"], wts["k_head_g"])


def _rows_to_lane_broadcast(rows, spread):
    x = jnp.concatenate(rows, axis=0)
    p1 = x.astype(BF16)
    r1 = x - p1.astype(F32)
    p2 = r1.astype(BF16)
    p3 = (r1 - p2.astype(F32)).astype(BF16)
    pad = jnp.zeros((spread.shape[0] - 3 * len(rows), x.shape[1]), BF16)
    return _dot_tn(jnp.concatenate([p1, p2, p3, pad], axis=0), spread)


def _mlstm_gate_rows(b_row, a_row, amax_row, forward, m):
    L = b_row.shape[1]
    last = slice(L - 1, L) if forward else slice(0, 1)
    total = b_row[:, last]
    g_row = jnp.maximum(m, amax_row)
    m_new = total + jnp.maximum(m, amax_row[:, last])
    w_key_row = jnp.exp2((a_row + (total - m_new)) * LOG2E)
    decay = jnp.exp(total + m - m_new)
    return g_row * LOG2E, (b_row + g_row) * LOG2E, a_row * LOG2E, w_key_row, decay, m_new


def _mlstm_block(s_raw, q, kt, v_aug, g2, mt2, a2_row, w_key_row, decay, allow, CN, m):
    w_intra = jnp.exp2(jnp.where(allow, a2_row - jnp.concatenate([g2, g2], axis=1), -jnp.inf))
    w_inter = jnp.exp2(m * LOG2E - g2)
    s = (s_raw * w_intra).astype(BF16)
    nd = _dot(s, v_aug) + jnp.concatenate([w_inter, w_inter], axis=1) * _dot(q, CN.astype(BF16))
    num, den = nd[:, 0:M_HEAD_DIM], nd[:, M_HEAD_DIM:2 * M_HEAD_DIM]
    h = num / jnp.maximum(jnp.abs(den), jnp.exp2(-mt2))
    CN_new = decay * CN + _dot((kt * w_key_row).astype(BF16), v_aug)
    return h, CN_new


def _mlstm_kernel(has_init, emit_state, n_blocks, heads, seqs, *refs):
    q_ref, kt_ref, v_ref, mo_ref, stats_ref, ng_ref, spread_ref = refs[:7]
    pos = 7
    if has_init:
        c0_ref, n0_ref, m0_ref = refs[pos:pos + 3]
        pos += 3
    hm_ref = refs[pos]
    pos += 1
    if emit_state:
        c_ref, n_ref, m_ref = refs[pos:pos + 3]
        pos += 3

    L, Dh = M_BLOCK, M_HEAD_DIM
    t_idx = lax.broadcasted_iota(jnp.int32, (L, L), 0)
    s_idx = lax.broadcasted_iota(jnp.int32, (L, L), 1)
    allow = (s_idx <= t_idx, s_idx >= t_idx)
    spread = spread_ref[...]
    ones = jnp.ones((L, Dh), BF16)

    def time_lanes(sq, c):
        start = (sq * n_blocks + c) * L
        return slice(start, start + L)

    def lane_broadcast_n(n_row):
        return jnp.broadcast_to(n_row, (Dh, Dh)).T

    def init_state(sq, j, d):
        if has_init:
            return (jnp.concatenate([c0_ref[sq, d, j], lane_broadcast_n(n0_ref[sq, d, j])], axis=1),
                    m0_ref[sq, j, d:d + 1, 0:1])
        return jnp.zeros((Dh, 2 * Dh), F32), jnp.zeros((1, 1), F32)

    def gate_rows(sq, j, c, d, m):
        head = j if heads == M_HEADS else pl.program_id(1) * heads + j
        r, lanes = pl.ds(d * M_HEADS + head, 1), time_lanes(sq, c)
        return _mlstm_gate_rows(stats_ref[0, r, lanes], stats_ref[1, r, lanes], stats_ref[2, r, lanes], d == 0, m)

    def blocks(sq, j, jobs, states):
        loaded, rows6, cols_in = {}, [], []
        for (c, d), (CN, m) in zip(jobs, states):
            rows6.append(gate_rows(sq, j, c, d, m))
            cols_in += [rows6[-1][0], rows6[-1][1]]
            if c not in loaded:
                rows, cols = slice(c * L, (c + 1) * L), slice(j * Dh, (j + 1) * Dh)
                q, kt = q_ref[sq, rows, cols], kt_ref[cols, time_lanes(sq, c)]
                v_aug = jnp.concatenate([v_ref[sq, rows, cols], ones], axis=1)
                loaded[c] = (_dot(q, kt.astype(BF16)), q, kt, v_aug)
        cols_out = _rows_to_lane_broadcast(cols_in, spread)
        hs, new_states = [], []
        for idx, ((c, d), (CN, m)) in enumerate(zip(jobs, states)):
            g2 = cols_out[:, (2 * idx) * LANES:(2 * idx + 1) * LANES]
            mt2 = cols_out[:, (2 * idx + 1) * LANES:(2 * idx + 2) * LANES]
            _, _, a2_row, w_key_row, decay, m_new = rows6[idx]
            h, CN_new = _mlstm_block(*loaded[c], g2, mt2, a2_row, w_key_row, decay, allow[d], CN, m)
            hs.append(h)
            new_states.append((CN_new, m_new))
        return hs, new_states

    def finish(sq, j, rows, hs):
        cols = slice(j * Dh, (j + 1) * Dh)
        hn = _rms(hs, ng_ref[j])
        hm_ref[sq, rows, cols] = (hn * jax.nn.sigmoid(mo_ref[sq, rows, cols])).astype(hm_ref.dtype)

    def emit(sq, j, d, state):
        CN, m = state
        c_ref[sq, d, j] = CN[:, 0:Dh]
        n_ref[sq, d, j] = CN[:, Dh:2 * Dh].T[0:1, :]
        m_ref[sq, j, d:d + 1, :] = jnp.broadcast_to(m, (1, LANES))

    if n_blocks > 1:
        hf_scr, hb_scr = refs[pos:pos + 2]

    for sq, j in [(sq, j) for sq in range(seqs) for j in range(heads)]:
        cols = slice(j * Dh, (j + 1) * Dh)
        states = [init_state(sq, j, 0), init_state(sq, j, 1)]
        if n_blocks == 1:
            (hf, hb), states = blocks(sq, j, [(0, 0), (0, 1)], states)
            finish(sq, j, slice(0, L), hf + hb)
        else:
            for step in range(n_blocks):
                cf, cb = step, n_blocks - 1 - step
                (hf, hb), states = blocks(sq, j, [(cf, 0), (cb, 1)], states)
                hf_scr[cf * L:(cf + 1) * L, cols] = hf
                hb_scr[cb * L:(cb + 1) * L, cols] = hb
            finish(sq, j, slice(None), hf_scr[:, cols] + hb_scr[:, cols])
        if emit_state:
            emit(sq, j, 0, states[0])
            emit(sq, j, 1, states[1])


def _mlstm(mq, mkt, mv, mo, stats, norm_g, init_state, emit_state, heads, seqs):
    B, T, _ = mq.shape
    H, Dh, L = M_HEADS, M_HEAD_DIM, M_BLOCK
    nb = T // L
    w = heads * Dh
    has_init = init_state is not None
    n_rows = 4
    spread = jnp.tile(jnp.repeat(jnp.eye(n_rows, dtype=BF16), LANES, axis=1), (3, 1))
    spread = jnp.pad(spread, ((0, 2 * SUBLANES - 3 * n_rows), (0, 0)))
    tok = pl.BlockSpec((seqs, T, w), lambda b, h: (b, 0, h))
    assert mkt.shape == (B // seqs, M_WIDTH, seqs * T) and stats.shape == (B // seqs, 3, 2 * H, seqs * T)
    in_specs = [tok, pl.BlockSpec((None, w, seqs * T), lambda b, h: (b, h, 0)), tok, tok,
                pl.BlockSpec((None, 3, 2 * H, seqs * T), lambda b, h: (b, 0, 0, 0)),
                pl.BlockSpec((heads, 1, Dh), lambda b, h: (h, 0, 0)),
                _const_spec((2 * SUBLANES, n_rows * LANES))]
    args = [mq, mkt, mv, mo, stats, norm_g, spread]
    state_specs = [pl.BlockSpec((seqs, 2, heads, Dh, Dh), lambda b, h: (b, 0, h, 0, 0)),
                   pl.BlockSpec((seqs, 2, heads, 1, Dh), lambda b, h: (b, 0, h, 0, 0)),
                   pl.BlockSpec((seqs, heads, 2, LANES), lambda b, h: (b, h, 0, 0))]
    if has_init:
        in_specs += state_specs
        args += list(init_state)
    out_specs = [tok]
    out_shape = [jax.ShapeDtypeStruct((B, T, M_WIDTH), BF16)]
    if emit_state:
        out_specs += state_specs
        out_shape += [jax.ShapeDtypeStruct((B, 2, H, Dh, Dh), F32),
                      jax.ShapeDtypeStruct((B, 2, H, 1, Dh), F32),
                      jax.ShapeDtypeStruct((B, H, 2, LANES), F32)]
    scratch = [] if nb == 1 else [pltpu.VMEM((T, w), F32), pltpu.VMEM((T, w), F32)]
    return pl.pallas_call(
        functools.partial(_mlstm_kernel, has_init, emit_state, nb, heads, seqs),
        grid=(B // seqs, H // heads),
        in_specs=in_specs,
        out_specs=out_specs,
        out_shape=out_shape,
        scratch_shapes=scratch,
        compiler_params=_params(("parallel", "parallel")),
        name="mlstm_latent" if has_init else "mlstm_context",
    )(*args)


def _attn_kernel(has_ctx, seqs, *refs):
    if has_ctx:
        q_ref, k_ref, v_ref, kc_ref, vc_ref, o_ref = refs
    else:
        q_ref, k_ref, v_ref, o_ref = refs
    lane = lax.broadcasted_iota(jnp.int32, (1, LANES), 1)
    ones = lambda n: jnp.ones((n, LANES), BF16)
    tq = q_ref.shape[1]
    sub = min(Q_SUBTILE, tq)
    for sq, r0, pair in [(sq, r0, pair) for sq in range(seqs) for r0 in range(0, tq, sub)
                         for pair in range(A_HEADS // 2)]:
        rows = slice(r0, r0 + sub)
        vsl = slice(pair * LANES, (pair + 1) * LANES)
        v_aug = jnp.concatenate([v_ref[sq, :, vsl], ones(v_ref.shape[1])], axis=1)
        if has_ctx:
            vc_aug = jnp.concatenate([vc_ref[sq, :, vsl], ones(vc_ref.shape[1])], axis=1)
        outs = []
        for e in range(2):
            hsl = slice((2 * pair + e) * HEAD_PAD, (2 * pair + e + 1) * HEAD_PAD)
            qh = q_ref[sq, rows, hsl]
            s = _dot_nt(qh, k_ref[sq, :, hsl])
            mx = jnp.max(s, axis=1, keepdims=True)
            if has_ctx:
                sc = _dot_nt(qh, kc_ref[sq, :, hsl])
                mx = jnp.maximum(mx, jnp.max(sc, axis=1, keepdims=True))
            od = _dot(jnp.exp2(s - mx).astype(BF16), v_aug)
            if has_ctx:
                od = od + _dot(jnp.exp2(sc - mx).astype(BF16), vc_aug)
            outs.append(od[:, 0:LANES] / od[:, LANES:2 * LANES])
        o_ref[sq, rows, vsl] = jnp.where(lane < A_VDIM, outs[0], outs[1]).astype(o_ref.dtype)


def _attn(q, k, v, ctx_kv, seqs):
    B, T, _ = q.shape
    tq = min(Q_TILE, T)
    has_ctx = ctx_kv is not None
    full = lambda n, w: pl.BlockSpec((seqs, n, w), lambda b, i: (b, 0, 0))
    in_specs = [pl.BlockSpec((seqs, tq, QK_PAD), lambda b, i: (b, i, 0)), full(T, QK_PAD), full(T, A_WIDTH)]
    args = [q, k, v]
    if has_ctx:
        P = ctx_kv[0].shape[1]
        in_specs += [full(P, QK_PAD), full(P, A_WIDTH)]
        args += list(ctx_kv)
    return pl.pallas_call(
        functools.partial(_attn_kernel, has_ctx, seqs),
        grid=(B // seqs, T // tq),
        in_specs=in_specs,
        out_specs=pl.BlockSpec((seqs, tq, A_WIDTH), lambda b, i: (b, i, 0)),
        out_shape=jax.ShapeDtypeStruct((B, T, A_WIDTH), BF16),
        compiler_params=_params(("parallel", "parallel")),
        name="attn_latent" if has_ctx else "attn_context",
    )(*args)


def _post_kernel(mod_row0, mod_per_batch, x_ref, hm_ref, ha_ref, mod_ref, g2_ref, wout_ref, wup_ref, wdown_ref,
                 y_ref):
    mod = _mod_row(mod_ref, mod_row0, mod_per_batch)
    gate1 = mod[:, 2 * D_MODEL:3 * D_MODEL]
    sh2 = mod[:, 3 * D_MODEL:4 * D_MODEL]
    sc2 = mod[:, 4 * D_MODEL:5 * D_MODEL]
    gate2 = mod[:, 5 * D_MODEL:6 * D_MODEL]
    mix = jnp.concatenate([hm_ref[...], ha_ref[...]], axis=-1)
    x1 = x_ref[...] + gate1 * _dot(mix, wout_ref[...])
    h2 = (_rms(x1, g2_ref[...]) * (1.0 + sc2) + sh2).astype(BF16)
    acc = jnp.zeros_like(x1)
    for c in range(D_FF // FF_TILE):
        sl = slice(c * FF_TILE, (c + 1) * FF_TILE)
        u = jnp.maximum(_dot(h2, wup_ref[:, sl]), 0.0)
        acc = acc + _dot((u * u).astype(BF16), wdown_ref[sl, :])
    y_ref[...] = x1 + gate2 * acc


def _post(x, hm, ha, mod3, mod_row0, mod_per_batch, wts):
    shape = x.shape
    if not mod_per_batch:
        x, hm, ha = (a.reshape(1, -1, a.shape[-1]) for a in (x, hm, ha))
    B, T, _ = x.shape
    tm = POST_TILE
    tok = lambda w: pl.BlockSpec((None, tm, w), lambda b, i: (b, i, 0))
    return _post_call(x, hm, ha, mod3, mod_row0, mod_per_batch, wts, B, T, tm, tok).reshape(shape)


def _post_call(x, hm, ha, mod3, mod_row0, mod_per_batch, wts, B, T, tm, tok):
    return pl.pallas_call(
        functools.partial(_post_kernel, mod_row0, mod_per_batch),
        grid=(B, T // tm),
        in_specs=[tok(D_MODEL), tok(M_WIDTH), tok(A_WIDTH),
                  _const_spec((8, 6 * D_MODEL)),
                  _const_spec((1, D_MODEL)),
                  _const_spec((M_WIDTH + A_WIDTH, D_MODEL)),
                  _const_spec((D_MODEL, D_FF)),
                  _const_spec((D_FF, D_MODEL))],
        out_specs=tok(D_MODEL),
        out_shape=jax.ShapeDtypeStruct((B, T, D_MODEL), F32),
        compiler_params=_params(("parallel", "parallel")),
        name="post",
    )(x, hm, ha, mod3, wts["g2"], wts["w_out"], wts["w_up"], wts["w_down"])


def _prepare_weights(norm1_g, norm2_g, w_in, mlstm_gate_b, q_lora_g, kv_lora_g, w_q_up, w_kv_up,
                     q_head_g, k_head_g, w_out, w_mlp_up, w_mlp_down):
    o_g = 4 * M_WIDTH
    o_q = o_g + N_GATES
    o_kv = o_q + Q_LORA
    o_kr = o_kv + KV_LORA
    half = A_ROPE // 2
    n_dh = 2 * M_HEADS
    wt = w_in.T.astype(BF16)
    w_gate = wt[o_g:o_q].reshape(2, 2, M_HEADS, D_MODEL)
    bias = mlstm_gate_b.reshape(2, 2, M_HEADS)

    def rot_partner(a):
        z = jnp.zeros(a.shape[:-1] + (A_NOPE,), a.dtype)
        return jnp.concatenate([z, a[..., A_NOPE + half:A_QK], a[..., A_NOPE:A_NOPE + half]], axis=-1)

    pad_tile = lambda a: jnp.pad(a, [(0, 0)] * (a.ndim - 1) + [(0, HEAD_PAD - A_QK)])
    w_kr = wt[o_kr:o_kr + A_ROPE]
    w_kr_partner = jnp.concatenate([w_kr[half:], w_kr[:half]], axis=0)
    w_lat = jnp.concatenate([wt[o_q:o_kr], w_gate[:, 0].reshape(n_dh, D_MODEL), w_gate[:, 1].reshape(n_dh, D_MODEL),
                             jnp.zeros((A_NOPE - 2 * n_dh, D_MODEL), BF16), w_kr, w_kr_partner], axis=0)
    gate_bias = jnp.pad(jnp.stack([bias[:, 0, :].reshape(n_dh), bias[:, 1, :].reshape(n_dh)], axis=0),
                        ((0, 0), (0, LANES - n_dh)))
    w_q3 = w_q_up.reshape(Q_LORA, A_HEADS, A_QK)
    w_q = pad_tile(w_q3).reshape(Q_LORA, QK_PAD)
    w_q_partner = pad_tile(rot_partner(w_q3)).reshape(Q_LORA, QK_PAD)
    w_kv3 = w_kv_up.reshape(KV_LORA, A_HEADS, A_NOPE + A_VDIM)
    w_k = jnp.pad(w_kv3[:, :, :A_NOPE], ((0, 0), (0, 0), (0, HEAD_PAD - A_NOPE)))
    w_v = w_kv3[:, :, A_NOPE:]
    w_kv = jnp.concatenate([w_k.reshape(KV_LORA, QK_PAD), w_v.reshape(KV_LORA, A_WIDTH)], axis=1)
    pad_head = lambda g: jnp.stack([pad_tile(g), pad_tile(rot_partner(g))], axis=0)
    return {
        "g1": norm1_g.reshape(1, D_MODEL),
        "g2": norm2_g.reshape(1, D_MODEL),
        "w_main": jnp.concatenate([wt[0:M_WIDTH], wt[2 * M_WIDTH:o_g]], axis=0),
        "w_kt": wt[M_WIDTH:2 * M_WIDTH],
        "w_lat": w_lat,
        "gate_bias": gate_bias,
        "q_lora_g": q_lora_g.reshape(1, Q_LORA),
        "kv_lora_g": kv_lora_g.reshape(1, KV_LORA),
        "w_q": w_q.astype(BF16),
        "w_q_rot": jnp.concatenate([w_q, w_q_partner], axis=1).astype(BF16),
        "w_kv": w_kv.astype(BF16),
        "q_head_g": pad_head(q_head_g),
        "k_head_g": pad_head(k_head_g),
        "w_out_f32": w_out,
        "w_up_f32": w_mlp_up,
        "w_down_f32": w_mlp_down,
    }


def _rope_tables(T):
    rows = T // GRID_W
    row = np.repeat(np.arange(rows, dtype=np.float32), GRID_W)
    col = np.tile(np.arange(GRID_W, dtype=np.float32), rows)
    half = A_ROPE // 2
    inv = (np.float32(ROPE_BASE) ** (-np.arange(0, half, 2, dtype=np.float32) / np.float32(half))).astype(np.float32)
    ang = np.concatenate([row[:, None] * inv, col[:, None] * inv], axis=-1)
    cos, sin = np.cos(ang), np.sin(ang)
    ones = np.ones((T, A_NOPE), np.float32)
    z = lambda w: np.zeros((T, w), np.float32)
    tail = LANES - A_QK
    cos_t = np.concatenate([ones, cos, cos, z(tail)], axis=1)
    sin_t = np.concatenate([z(A_NOPE), -sin, sin, z(tail)], axis=1)
    return jnp.asarray(np.stack([cos_t, sin_t], axis=0).astype(np.float32))


def _layer_pass(x, mod3, mod_row0, mod_per_batch, wts, norm_g, rope_tab, init_state, ctx_kv, is_context):
    pre = _pre(x, mod3, mod_row0, mod_per_batch, wts, rope_tab, emit_cache=is_context)
    mq, mkt, mv, mo, stats, q, k, v = pre[:8]
    if is_context:
        wts = dict(wts, w_out=pre[10], w_up=pre[11], w_down=pre[12])
    ml = _mlstm(mq, mkt, mv, mo, stats, norm_g, init_state, emit_state=is_context,
                heads=M_HEADS if is_context else 1, seqs=MLSTM_CONTEXT_SEQS if is_context else 1)
    ha = _attn(q, k, v, ctx_kv, seqs=ATTN_CONTEXT_SEQS if is_context else 1)
    y = _post(x, ml[0], ha, mod3, mod_row0, mod_per_batch, wts)
    return y, pre[8:10], ml[1:], wts


def kernel(x_prompt, x_sample, cache_mla_ckv, cache_mla_krope, state_mlstm_C, state_mlstm_n, state_mlstm_m,
           c, c_ctx, norm1_g, norm2_g, w_ada, b_ada, w_in, mlstm_gate_b, mlstm_norm_g,
           q_lora_g, kv_lora_g, w_q_up, w_kv_up, q_head_g, k_head_g, w_out, w_mlp_up, w_mlp_down):
    depth = w_in.shape[0]
    Bd = x_sample.shape[0]
    cond8 = jnp.concatenate([c_ctx[None, :], c, jnp.zeros((8 - 1 - Bd, D_MODEL), F32)], axis=0)
    rope_tab = _rope_tables(x_sample.shape[1])

    y, z = x_prompt, x_sample
    ckvs, kropes, Cs, ns, ms = [], [], [], [], []
    for l in range(depth):
        wts = _prepare_weights(norm1_g[l], norm2_g[l], w_in[l], mlstm_gate_b[l], q_lora_g[l], kv_lora_g[l],
                               w_q_up[l], w_kv_up[l], q_head_g[l], k_head_g[l], w_out[l], w_mlp_up[l],
                               w_mlp_down[l])
        norm_g = mlstm_norm_g[l].reshape(M_HEADS, 1, M_HEAD_DIM)
        mod3 = _ada(cond8, w_ada[l], b_ada[l])
        y, (ckv, krope), (C_new, n_new, m_new), wts = _layer_pass(
            y, mod3, 0, 0, wts, norm_g, None, None, None, True)
        ckvs.append(ckv)
        kropes.append(krope)
        Cs.append(C_new)
        ns.append(n_new[:, :, :, 0, :])
        ms.append(m_new[:, :, :, 0].transpose(0, 2, 1))

        init_state = (state_mlstm_C[:, l],
                      state_mlstm_n[:, l][:, :, :, None, :],
                      jnp.broadcast_to(state_mlstm_m[:, l].transpose(0, 2, 1)[..., None],
                                       (Bd, M_HEADS, 2, LANES)))
        krope_placed = jnp.pad(cache_mla_krope[:, l], ((0, 0), (0, 0), (A_NOPE, LANES - A_QK)))
        ctx_kv = _ctxkv(cache_mla_ckv[:, l], krope_placed, wts)
        z, _, _, _ = _layer_pass(z, mod3, 1, 1, wts, norm_g, rope_tab, init_state, ctx_kv, False)

    return (y, z, jnp.stack(ckvs, axis=1), jnp.stack(kropes, axis=1), jnp.stack(Cs, axis=1),
            jnp.stack(ns, axis=1), jnp.stack(ms, axis=1))
```

```python
import functools

import jax
import jax.numpy as jnp
import numpy as np
from jax import lax
from jax.experimental import pallas as pl
from jax.experimental.pallas import tpu as pltpu

F32 = jnp.float32
BF16 = jnp.bfloat16

D_MODEL = 1024
GRID_W = 64
M_HEADS = 4
M_HEAD_DIM = 128
M_WIDTH = M_HEADS * M_HEAD_DIM
M_BLOCK = 256
A_HEADS = 8
A_NOPE = 64
A_ROPE = 32
A_QK = A_NOPE + A_ROPE
A_VDIM = 64
A_WIDTH = A_HEADS * A_VDIM
Q_LORA = 384
KV_LORA = 256
ROPE_BASE = 10000.0
D_FF = 4 * D_MODEL
EPS = 1e-6

LANES = 128
SUBLANES = 8
LOG2E = 1.4426950408889634
HEAD_PAD = LANES
QK_PAD = A_HEADS * HEAD_PAD
N_GATES = 4 * M_HEADS
LAT_WIDTH = Q_LORA + KV_LORA + LANES
VMEM_LIMIT = 56 * 1024 * 1024

TOKEN_TILE = 512
POST_TILE = 512
MLSTM_CONTEXT_SEQS = 2
ATTN_CONTEXT_SEQS = 4
Q_TILE = 1024
Q_SUBTILE = 256
ADA_TILE_N = 1536
FF_TILE = 1024


def _dot(a, b):
    return jnp.dot(a, b, preferred_element_type=F32)


def _dot_nt(a, b):
    return lax.dot_general(a, b, (((1,), (1,)), ((), ())), preferred_element_type=F32)


def _dot_tn(a, b):
    return lax.dot_general(a, b, (((0,), (0,)), ((), ())), preferred_element_type=F32)


def _rms(x, g):
    y = x * lax.rsqrt(jnp.mean(x * x, axis=-1, keepdims=True) + EPS)
    return y * g


def _params(sem):
    return pltpu.CompilerParams(dimension_semantics=sem, vmem_limit_bytes=VMEM_LIMIT)


def _const_spec(shape):
    zeros = (0,) * len(shape)
    return pl.BlockSpec(shape, lambda *_: zeros, pipeline_mode=pl.Buffered(1))


def _ada_kernel(cond_ref, w_ref, b_ref, o_ref):
    c = cond_ref[...]
    s = (c * jax.nn.sigmoid(c)).astype(BF16)
    o_ref[...] = _dot(s, w_ref[...].astype(BF16)) + b_ref[...]


def _ada(cond8, w_ada, b_ada):
    n = w_ada.shape[1]
    return pl.pallas_call(
        _ada_kernel,
        grid=(n // ADA_TILE_N,),
        in_specs=[
            pl.BlockSpec((8, D_MODEL), lambda j: (0, 0)),
            pl.BlockSpec((D_MODEL, ADA_TILE_N), lambda j: (0, j)),
            pl.BlockSpec((1, ADA_TILE_N), lambda j: (0, j)),
        ],
        out_specs=pl.BlockSpec((8, ADA_TILE_N), lambda j: (0, j)),
        out_shape=jax.ShapeDtypeStruct((8, n), F32),
        compiler_params=_params(("parallel",)),
        name="ada",
    )(cond8, w_ada, b_ada.reshape(1, n))


def _write_heads(src, extra, g_pad, rot, dst_ref):
    for h in range(A_HEADS):
        sl = slice(h * HEAD_PAD, (h + 1) * HEAD_PAD)
        xh = src[:, sl]
        if extra is not None:
            xh = xh + extra
        ss = jnp.sum(xh * xh, axis=-1, keepdims=True) * (1.0 / A_QK)
        r = lax.rsqrt(ss + EPS)
        if rot is None:
            y = xh * r * g_pad
        else:
            partner, cos_g, sin_g = rot
            ph = partner if partner.shape[1] == HEAD_PAD else partner[:, sl]
            y = (xh * cos_g + ph * sin_g) * r
        dst_ref[:, sl] = y.astype(dst_ref.dtype)


def _time_scan(x, op, identity, reverse):
    n = x.shape[0]
    row = lax.broadcasted_iota(jnp.int32, x.shape, 0)
    shift = 1
    while shift < n:
        if shift < SUBLANES:
            if reverse:
                moved = jnp.where(row < n - shift, pltpu.roll(x, n - shift, 0), identity)
            else:
                moved = jnp.where(row >= shift, pltpu.roll(x, shift, 0), identity)
        else:
            fill = jnp.full((shift, x.shape[1]), identity, x.dtype)
            moved = (jnp.concatenate([x[shift:], fill], axis=0) if reverse
                     else jnp.concatenate([fill, x[:n - shift]], axis=0))
        x = op(x, moved)
        shift *= 2
    return x


def _mod_row(mod_ref, row0, per_batch):
    if per_batch:
        return mod_ref[pl.ds(row0 + pl.program_id(0) * per_batch, 1), :]
    return mod_ref[row0:row0 + 1, :]


def _pre_kernel(has_rope, emit_cache, mod_row0, mod_per_batch, *refs):
    (x_ref, mod_ref, g1_ref, wmain_ref, wkt_ref, wlat_ref, gbias_ref, qlg_ref, kvg_ref, wq_ref, wkv_ref,
     qhg_ref, khg_ref) = refs[:13]
    pos = 13
    if has_rope:
        rope_ref = refs[pos]
        pos += 1
    if emit_cache:
        cast_in = refs[pos:pos + 3]
        pos += 3
    (mq_ref, mkt_ref, mv_ref, mo_ref, stats_ref, q_ref, k_ref, v_ref) = refs[pos:pos + 8]
    pos += 8

    x = x_ref[...]
    mod = _mod_row(mod_ref, mod_row0, mod_per_batch)
    sh1 = mod[:, 0:D_MODEL]
    sc1 = mod[:, D_MODEL:2 * D_MODEL]
    h = _rms(x, g1_ref[...]) * (1.0 + sc1) + sh1
    hb = h.astype(BF16)

    plat = _dot_nt(hb, wlat_ref[...])
    q_lat = plat[:, 0:Q_LORA]
    kv_lat = plat[:, Q_LORA:Q_LORA + KV_LORA]
    tail = plat[:, Q_LORA + KV_LORA:LAT_WIDTH]
    tail_f = pltpu.roll(tail, LANES - 2 * M_HEADS, 1)
    tail2 = pltpu.roll(tail, LANES - A_ROPE, 1)

    lane = lax.broadcasted_iota(jnp.int32, (1, LANES), 1)
    fwd = lane < M_HEADS
    gate_i = tail + gbias_ref[0:1, :]
    gate_f = tail_f + gbias_ref[1:2, :]
    log_f = jnp.minimum(gate_f, 0.0) - jnp.log1p(jnp.exp(-jnp.abs(gate_f)))
    def block_scan(v, op, identity):
        parts = [v[r:r + M_BLOCK] for r in range(0, v.shape[0], M_BLOCK)]
        return jnp.where(fwd, jnp.concatenate([_time_scan(p, op, identity, False) for p in parts], axis=0),
                         jnp.concatenate([_time_scan(p, op, identity, True) for p in parts], axis=0))

    b = block_scan(log_f, jnp.add, 0.0)
    a = gate_i - b
    amax = block_scan(a, jnp.maximum, -jnp.inf)
    stats_ref[0] = b.T[0:2 * M_HEADS, :]
    stats_ref[1] = a.T[0:2 * M_HEADS, :]
    stats_ref[2] = amax.T[0:2 * M_HEADS, :]

    krope_placed = jnp.where((lane >= A_NOPE) & (lane < A_QK), tail, 0.0)
    ckv = _rms(kv_lat, kvg_ref[...])
    qn = _rms(q_lat, qlg_ref[...])
    qf = _dot(qn.astype(BF16), wq_ref[...])
    kvf = _dot(ckv.astype(BF16), wkv_ref[...])
    v_ref[...] = kvf[:, QK_PAD:QK_PAD + A_WIDTH].astype(BF16)
    q_rot = k_rot = None
    qhg = qhg_ref[...] * (A_QK ** -0.5 * LOG2E)
    if has_rope:
        cos_t, sin_t = rope_ref[0], rope_ref[1]
        q_rot = (qf[:, QK_PAD:2 * QK_PAD], cos_t * qhg[0:1, :], sin_t * qhg[1:2, :])
        k_rot = (tail2, cos_t * khg_ref[0:1, :], sin_t * khg_ref[1:2, :])
    _write_heads(qf, None, qhg[0:1, :], q_rot, q_ref)
    _write_heads(kvf, krope_placed, khg_ref[0:1, :], k_rot, k_ref)

    if emit_cache:
        ckv_ref, krope_ref = refs[pos:pos + 2]
        ckv_ref[...] = ckv
        krope_ref[...] = tail[:, A_NOPE:A_QK]
        for src_ref, dst_ref in zip(cast_in, refs[pos + 2:pos + 5]):
            dst_ref[...] = src_ref[...].astype(BF16)

    pm = _dot_nt(hb, wmain_ref[...])
    mq_ref[...] = pm[:, 0:M_WIDTH].astype(BF16)
    mv_ref[...] = pm[:, M_WIDTH:2 * M_WIDTH].astype(BF16)
    mo_ref[...] = pm[:, 2 * M_WIDTH:3 * M_WIDTH]
    mkt_ref[...] = _dot_nt(wkt_ref[...], hb) * (M_HEAD_DIM ** -0.5)


def _pre(x, mod3, mod_row0, mod_per_batch, wts, rope_tab, emit_cache):
    shape = x.shape
    tm = TOKEN_TILE
    if shape[1] < tm:
        assert not mod_per_batch and tm % shape[1] == 0
        x = x.reshape(-1, tm, shape[2])
    B, T, _ = x.shape
    has_rope = rope_tab is not None
    tok = lambda w: pl.BlockSpec((None, tm, w), lambda b, i: (b, i, 0))
    in_specs = [
        tok(D_MODEL),
        _const_spec((8, 6 * D_MODEL)),
        _const_spec((1, D_MODEL)),
        _const_spec((3 * M_WIDTH, D_MODEL)),
        _const_spec((M_WIDTH, D_MODEL)),
        _const_spec((LAT_WIDTH, D_MODEL)),
        _const_spec((2, LANES)),
        _const_spec((1, Q_LORA)),
        _const_spec((1, KV_LORA)),
        _const_spec((Q_LORA, 2 * QK_PAD if has_rope else QK_PAD)),
        _const_spec((KV_LORA, QK_PAD + A_WIDTH)),
        _const_spec((2, HEAD_PAD)),
        _const_spec((2, HEAD_PAD)),
    ]
    assert tm % M_BLOCK == 0
    args = [x, mod3, wts["g1"], wts["w_main"], wts["w_kt"], wts["w_lat"], wts["gate_bias"], wts["q_lora_g"],
            wts["kv_lora_g"], wts["w_q_rot"] if has_rope else wts["w_q"], wts["w_kv"],
            wts["q_head_g"], wts["k_head_g"]]
    if has_rope:
        in_specs.append(pl.BlockSpec((2, tm, HEAD_PAD), lambda b, i: (0, i, 0)))
        args.append(rope_tab)
    out_specs = [tok(M_WIDTH),
                 pl.BlockSpec((None, M_WIDTH, tm), lambda b, i: (b, 0, i)),
                 tok(M_WIDTH), tok(M_WIDTH),
                 pl.BlockSpec((None, 3, 2 * M_HEADS, tm), lambda b, i: (b, 0, 0, i)),
                 tok(QK_PAD), tok(QK_PAD), tok(A_WIDTH)]
    out_shape = [
        jax.ShapeDtypeStruct((B, T, M_WIDTH), BF16),
        jax.ShapeDtypeStruct((B, M_WIDTH, T), F32),
        jax.ShapeDtypeStruct((B, T, M_WIDTH), BF16),
        jax.ShapeDtypeStruct((B, T, M_WIDTH), F32),
        jax.ShapeDtypeStruct((B, 3, 2 * M_HEADS, T), F32),
        jax.ShapeDtypeStruct((B, T, QK_PAD), BF16),
        jax.ShapeDtypeStruct((B, T, QK_PAD), BF16),
        jax.ShapeDtypeStruct((B, T, A_WIDTH), BF16),
    ]
    if emit_cache:
        out_specs += [tok(KV_LORA), tok(A_ROPE)]
        out_shape += [jax.ShapeDtypeStruct((B, T, KV_LORA), F32),
                      jax.ShapeDtypeStruct((B, T, A_ROPE), F32)]
        steps, per_b = B * (T // tm), T // tm
        for w in (wts["w_out_f32"], wts["w_up_f32"], wts["w_down_f32"]):
            rows = w.shape[0] // steps
            assert rows * steps == w.shape[0] and rows % (2 * SUBLANES) == 0
            spec = pl.BlockSpec((rows, w.shape[1]), lambda b, i: (b * per_b + i, 0))
            in_specs.append(spec)
            args.append(w)
            out_specs.append(spec)
            out_shape.append(jax.ShapeDtypeStruct(w.shape, BF16))
    outs = pl.pallas_call(
        functools.partial(_pre_kernel, has_rope, emit_cache, mod_row0, mod_per_batch),
        grid=(B, T // tm),
        in_specs=in_specs,
        out_specs=out_specs,
        out_shape=out_shape,
        compiler_params=_params(("parallel", "parallel")),
        name="pre_latent" if has_rope else "pre_context",
    )(*args)
    keep = (1, 4, 10, 11, 12)
    return [o if n in keep else o.reshape(shape[:2] + o.shape[2:]) for n, o in enumerate(outs)]


def _rows_to_lane_broadcast(rows, spread):
    x = jnp.concatenate(rows, axis=0)
    p1 = x.astype(BF16)
    r1 = x - p1.astype(F32)
    p2 = r1.astype(BF16)
    p3 = (r1 - p2.astype(F32)).astype(BF16)
    pad = jnp.zeros((spread.shape[0] - 3 * len(rows), x.shape[1]), BF16)
    return _dot_tn(jnp.concatenate([p1, p2, p3, pad], axis=0), spread)


def _mlstm_gate_rows(b_row, a_row, amax_row, forward, m):
    L = b_row.shape[1]
    last = slice(L - 1, L) if forward else slice(0, 1)
    total = b_row[:, last]
    g_row = jnp.maximum(m, amax_row)
    m_new = total + jnp.maximum(m, amax_row[:, last])
    w_key_row = jnp.exp2((a_row + (total - m_new)) * LOG2E)
    decay = jnp.exp(total + m - m_new)
    return g_row * LOG2E, (b_row + g_row) * LOG2E, a_row * LOG2E, w_key_row, decay, m_new


def _mlstm_block(s_raw, q, kt, v_aug, g2, mt2, a2_row, w_key_row, decay, allow, CN, m):
    w_intra = jnp.exp2(jnp.where(allow, a2_row - jnp.concatenate([g2, g2], axis=1), -jnp.inf))
    w_inter = jnp.exp2(m * LOG2E - g2)
    s = (s_raw * w_intra).astype(BF16)
    nd = _dot(s, v_aug) + jnp.concatenate([w_inter, w_inter], axis=1) * _dot(q, CN.astype(BF16))
    num, den = nd[:, 0:M_HEAD_DIM], nd[:, M_HEAD_DIM:2 * M_HEAD_DIM]
    h = num / jnp.maximum(jnp.abs(den), jnp.exp2(-mt2))
    CN_new = decay * CN + _dot((kt * w_key_row).astype(BF16), v_aug)
    return h, CN_new


def _mlstm_kernel(has_init, emit_state, n_blocks, heads, seqs, *refs):
    q_ref, kt_ref, v_ref, mo_ref, stats_ref, ng_ref, spread_ref = refs[:7]
    pos = 7
    if has_init:
        c0_ref, n0_ref, m0_ref = refs[pos:pos + 3]
        pos += 3
    hm_ref = refs[pos]
    pos += 1
    if emit_state:
        c_ref, n_ref, m_ref = refs[pos:pos + 3]
        pos += 3

    L, Dh = M_BLOCK, M_HEAD_DIM
    t_idx = lax.broadcasted_iota(jnp.int32, (L, L), 0)
    s_idx = lax.broadcasted_iota(jnp.int32, (L, L), 1)
    allow = (s_idx <= t_idx, s_idx >= t_idx)
    spread = spread_ref[...]
    ones = jnp.ones((L, Dh), BF16)

    def time_lanes(sq, c):
        start = (sq * n_blocks + c) * L
        return slice(start, start + L)

    def lane_broadcast_n(n_row):
        return jnp.broadcast_to(n_row, (Dh, Dh)).T

    def init_state(sq, j, d):
        if has_init:
            return (jnp.concatenate([c0_ref[sq, d, j], lane_broadcast_n(n0_ref[sq, d, j])], axis=1),
                    m0_ref[sq, j, d:d + 1, 0:1])
        return jnp.zeros((Dh, 2 * Dh), F32), jnp.zeros((1, 1), F32)

    def gate_rows(sq, j, c, d, m):
        head = j if heads == M_HEADS else pl.program_id(1) * heads + j
        r, lanes = pl.ds(d * M_HEADS + head, 1), time_lanes(sq, c)
        return _mlstm_gate_rows(stats_ref[0, r, lanes], stats_ref[1, r, lanes], stats_ref[2, r, lanes], d == 0, m)

    def blocks(sq, j, jobs, states):
        loaded, rows6, cols_in = {}, [], []
        for (c, d), (CN, m) in zip(jobs, states):
            rows6.append(gate_rows(sq, j, c, d, m))
            cols_in += [rows6[-1][0], rows6[-1][1]]
            if c not in loaded:
                rows, cols = slice(c * L, (c + 1) * L), slice(j * Dh, (j + 1) * Dh)
                q, kt = q_ref[sq, rows, cols], kt_ref[cols, time_lanes(sq, c)]
                v_aug = jnp.concatenate([v_ref[sq, rows, cols], ones], axis=1)
                loaded[c] = (_dot(q, kt.astype(BF16)), q, kt, v_aug)
        cols_out = _rows_to_lane_broadcast(cols_in, spread)
        hs, new_states = [], []
        for idx, ((c, d), (CN, m)) in enumerate(zip(jobs, states)):
            g2 = cols_out[:, (2 * idx) * LANES:(2 * idx + 1) * LANES]
            mt2 = cols_out[:, (2 * idx + 1) * LANES:(2 * idx + 2) * LANES]
            _, _, a2_row, w_key_row, decay, m_new = rows6[idx]
            h, CN_new = _mlstm_block(*loaded[c], g2, mt2, a2_row, w_key_row, decay, allow[d], CN, m)
            hs.append(h)
            new_states.append((CN_new, m_new))
        return hs, new_states

    def finish(sq, j, rows, hs):
        cols = slice(j * Dh, (j + 1) * Dh)
        hn = _rms(hs, ng_ref[j])
        hm_ref[sq, rows, cols] = (hn * jax.nn.sigmoid(mo_ref[sq, rows, cols])).astype(hm_ref.dtype)

    def emit(sq, j, d, state):
        CN, m = state
        c_ref[sq, d, j] = CN[:, 0:Dh]
        n_ref[sq, d, j] = CN[:, Dh:2 * Dh].T[0:1, :]
        m_ref[sq, j, d:d + 1, :] = jnp.broadcast_to(m, (1, LANES))

    if n_blocks > 1:
        hf_scr, hb_scr = refs[pos:pos + 2]

    for sq, j in [(sq, j) for sq in range(seqs) for j in range(heads)]:
        cols = slice(j * Dh, (j + 1) * Dh)
        states = [init_state(sq, j, 0), init_state(sq, j, 1)]
        if n_blocks == 1:
            (hf, hb), states = blocks(sq, j, [(0, 0), (0, 1)], states)
            finish(sq, j, slice(0, L), hf + hb)
        else:
            for step in range(n_blocks):
                cf, cb = step, n_blocks - 1 - step
                (hf, hb), states = blocks(sq, j, [(cf, 0), (cb, 1)], states)
                hf_scr[cf * L:(cf + 1) * L, cols] = hf
                hb_scr[cb * L:(cb + 1) * L, cols] = hb
            finish(sq, j, slice(None), hf_scr[:, cols] + hb_scr[:, cols])
        if emit_state:
            emit(sq, j, 0, states[0])
            emit(sq, j, 1, states[1])


def _mlstm(mq, mkt, mv, mo, stats, norm_g, init_state, emit_state, heads, seqs):
    B, T, _ = mq.shape
    H, Dh, L = M_HEADS, M_HEAD_DIM, M_BLOCK
    nb = T // L
    w = heads * Dh
    has_init = init_state is not None
    n_rows = 4
    spread = jnp.tile(jnp.repeat(jnp.eye(n_rows, dtype=BF16), LANES, axis=1), (3, 1))
    spread = jnp.pad(spread, ((0, 2 * SUBLANES - 3 * n_rows), (0, 0)))
    tok = pl.BlockSpec((seqs, T, w), lambda b, h: (b, 0, h))
    assert mkt.shape == (B // seqs, M_WIDTH, seqs * T) and stats.shape == (B // seqs, 3, 2 * H, seqs * T)
    in_specs = [tok, pl.BlockSpec((None, w, seqs * T), lambda b, h: (b, h, 0)), tok, tok,
                pl.BlockSpec((None, 3, 2 * H, seqs * T), lambda b, h: (b, 0, 0, 0)),
                pl.BlockSpec((heads, 1, Dh), lambda b, h: (h, 0, 0)),
                _const_spec((2 * SUBLANES, n_rows * LANES))]
    args = [mq, mkt, mv, mo, stats, norm_g, spread]
    state_specs = [pl.BlockSpec((seqs, 2, heads, Dh, Dh), lambda b, h: (b, 0, h, 0, 0)),
                   pl.BlockSpec((seqs, 2, heads, 1, Dh), lambda b, h: (b, 0, h, 0, 0)),
                   pl.BlockSpec((seqs, heads, 2, LANES), lambda b, h: (b, h, 0, 0))]
    if has_init:
        in_specs += state_specs
        args += list(init_state)
    out_specs = [tok]
    out_shape = [jax.ShapeDtypeStruct((B, T, M_WIDTH), BF16)]
    if emit_state:
        out_specs += state_specs
        out_shape += [jax.ShapeDtypeStruct((B, 2, H, Dh, Dh), F32),
                      jax.ShapeDtypeStruct((B, 2, H, 1, Dh), F32),
                      jax.ShapeDtypeStruct((B, H, 2, LANES), F32)]
    scratch = [] if nb == 1 else [pltpu.VMEM((T, w), F32), pltpu.VMEM((T, w), F32)]
    return pl.pallas_call(
        functools.partial(_mlstm_kernel, has_init, emit_state, nb, heads, seqs),
        grid=(B // seqs, H // heads),
        in_specs=in_specs,
        out_specs=out_specs,
        out_shape=out_shape,
        scratch_shapes=scratch,
        compiler_params=_params(("parallel", "parallel")),
        name="mlstm_latent" if has_init else "mlstm_context",
    )(*args)


def _attn_kernel(has_ctx, seqs, *refs):
    if has_ctx:
        q_ref, k_ref, v_ref, ckv_ref, krp_ref, wkv_ref, khg_ref, o_ref, kc_ref, vc_ref = refs

        @pl.when(pl.program_id(1) == 0)
        def _():
            kvf = _dot(ckv_ref[0].astype(BF16), wkv_ref[...])
            vc_ref[...] = kvf[:, QK_PAD:QK_PAD + A_WIDTH].astype(BF16)
            _write_heads(kvf, krp_ref[0], khg_ref[0:1, :], None, kc_ref)
    else:
        q_ref, k_ref, v_ref, o_ref = refs
    lane = lax.broadcasted_iota(jnp.int32, (1, LANES), 1)
    ones = lambda n: jnp.ones((n, LANES), BF16)
    tq = q_ref.shape[1]
    sub = min(Q_SUBTILE, tq)
    for sq, r0, pair in [(sq, r0, pair) for sq in range(seqs) for r0 in range(0, tq, sub)
                         for pair in range(A_HEADS // 2)]:
        rows = slice(r0, r0 + sub)
        vsl = slice(pair * LANES, (pair + 1) * LANES)
        v_aug = jnp.concatenate([v_ref[sq, :, vsl], ones(v_ref.shape[1])], axis=1)
        if has_ctx:
            vc_aug = jnp.concatenate([vc_ref[:, vsl], ones(vc_ref.shape[0])], axis=1)
        outs = []
        for e in range(2):
            hsl = slice((2 * pair + e) * HEAD_PAD, (2 * pair + e + 1) * HEAD_PAD)
            qh = q_ref[sq, rows, hsl]
            s = _dot_nt(qh, k_ref[sq, :, hsl])
            mx = jnp.max(s, axis=1, keepdims=True)
            if has_ctx:
                sc = _dot_nt(qh, kc_ref[:, hsl])
                mx = jnp.maximum(mx, jnp.max(sc, axis=1, keepdims=True))
            od = _dot(jnp.exp2(s - mx).astype(BF16), v_aug)
            if has_ctx:
                od = od + _dot(jnp.exp2(sc - mx).astype(BF16), vc_aug)
            outs.append(od[:, 0:LANES] / od[:, LANES:2 * LANES])
        o_ref[sq, rows, vsl] = jnp.where(lane < A_VDIM, outs[0], outs[1]).astype(o_ref.dtype)


def _attn(q, k, v, ctx, seqs):
    B, T, _ = q.shape
    tq = min(Q_TILE, T)
    has_ctx = ctx is not None
    full = lambda n, w: pl.BlockSpec((seqs, n, w), lambda b, i: (b, 0, 0))
    in_specs = [pl.BlockSpec((seqs, tq, QK_PAD), lambda b, i: (b, i, 0)), full(T, QK_PAD), full(T, A_WIDTH)]
    args = [q, k, v]
    scratch = []
    if has_ctx:
        assert seqs == 1
        P = ctx[0].shape[1]
        in_specs += [full(P, KV_LORA), full(P, HEAD_PAD), _const_spec((KV_LORA, QK_PAD + A_WIDTH)),
                     _const_spec((2, HEAD_PAD))]
        args += list(ctx)
        scratch = [pltpu.VMEM((P, QK_PAD), BF16), pltpu.VMEM((P, A_WIDTH), BF16)]
    return pl.pallas_call(
        functools.partial(_attn_kernel, has_ctx, seqs),
        grid=(B // seqs, T // tq),
        in_specs=in_specs,
        out_specs=pl.BlockSpec((seqs, tq, A_WIDTH), lambda b, i: (b, i, 0)),
        out_shape=jax.ShapeDtypeStruct((B, T, A_WIDTH), BF16),
        scratch_shapes=scratch,
        compiler_params=_params(("parallel", "arbitrary" if has_ctx else "parallel")),
        name="attn_latent" if has_ctx else "attn_context",
    )(*args)


def _post_kernel(mod_row0, mod_per_batch, x_ref, hm_ref, ha_ref, mod_ref, g2_ref, wout_ref, wup_ref, wdown_ref,
                 y_ref):
    mod = _mod_row(mod_ref, mod_row0, mod_per_batch)
    gate1 = mod[:, 2 * D_MODEL:3 * D_MODEL]
    sh2 = mod[:, 3 * D_MODEL:4 * D_MODEL]
    sc2 = mod[:, 4 * D_MODEL:5 * D_MODEL]
    gate2 = mod[:, 5 * D_MODEL:6 * D_MODEL]
    mix = jnp.concatenate([hm_ref[...], ha_ref[...]], axis=-1)
    x1 = x_ref[...] + gate1 * _dot(mix, wout_ref[...])
    h2 = (_rms(x1, g2_ref[...]) * (1.0 + sc2) + sh2).astype(BF16)
    acc = jnp.zeros_like(x1)
    for c in range(D_FF // FF_TILE):
        sl = slice(c * FF_TILE, (c + 1) * FF_TILE)
        u = jnp.maximum(_dot(h2, wup_ref[:, sl]), 0.0)
        acc = acc + _dot((u * u).astype(BF16), wdown_ref[sl, :])
    y_ref[...] = x1 + gate2 * acc


def _post(x, hm, ha, mod3, mod_row0, mod_per_batch, wts):
    shape = x.shape
    if not mod_per_batch:
        x, hm, ha = (a.reshape(1, -1, a.shape[-1]) for a in (x, hm, ha))
    B, T, _ = x.shape
    tm = POST_TILE
    tok = lambda w: pl.BlockSpec((None, tm, w), lambda b, i: (b, i, 0))
    return _post_call(x, hm, ha, mod3, mod_row0, mod_per_batch, wts, B, T, tm, tok).reshape(shape)


def _post_call(x, hm, ha, mod3, mod_row0, mod_per_batch, wts, B, T, tm, tok):
    return pl.pallas_call(
        functools.partial(_post_kernel, mod_row0, mod_per_batch),
        grid=(B, T // tm),
        in_specs=[tok(D_MODEL), tok(M_WIDTH), tok(A_WIDTH),
                  _const_spec((8, 6 * D_MODEL)),
                  _const_spec((1, D_MODEL)),
                  _const_spec((M_WIDTH + A_WIDTH, D_MODEL)),
                  _const_spec((D_MODEL, D_FF)),
                  _const_spec((D_FF, D_MODEL))],
        out_specs=tok(D_MODEL),
        out_shape=jax.ShapeDtypeStruct((B, T, D_MODEL), F32),
        compiler_params=_params(("parallel", "parallel")),
        name="post",
    )(x, hm, ha, mod3, wts["g2"], wts["w_out"], wts["w_up"], wts["w_down"])


def _prepare_weights(norm1_g, norm2_g, w_in, mlstm_gate_b, q_lora_g, kv_lora_g, w_q_up, w_kv_up,
                     q_head_g, k_head_g, w_out, w_mlp_up, w_mlp_down):
    o_g = 4 * M_WIDTH
    o_q = o_g + N_GATES
    o_kv = o_q + Q_LORA
    o_kr = o_kv + KV_LORA
    half = A_ROPE // 2
    n_dh = 2 * M_HEADS
    wt = w_in.T.astype(BF16)
    w_gate = wt[o_g:o_q].reshape(2, 2, M_HEADS, D_MODEL)
    bias = mlstm_gate_b.reshape(2, 2, M_HEADS)

    def rot_partner(a):
        z = jnp.zeros(a.shape[:-1] + (A_NOPE,), a.dtype)
        return jnp.concatenate([z, a[..., A_NOPE + half:A_QK], a[..., A_NOPE:A_NOPE + half]], axis=-1)

    pad_tile = lambda a: jnp.pad(a, [(0, 0)] * (a.ndim - 1) + [(0, HEAD_PAD - A_QK)])
    w_kr = wt[o_kr:o_kr + A_ROPE]
    w_kr_partner = jnp.concatenate([w_kr[half:], w_kr[:half]], axis=0)
    w_lat = jnp.concatenate([wt[o_q:o_kr], w_gate[:, 0].reshape(n_dh, D_MODEL), w_gate[:, 1].reshape(n_dh, D_MODEL),
                             jnp.zeros((A_NOPE - 2 * n_dh, D_MODEL), BF16), w_kr, w_kr_partner], axis=0)
    gate_bias = jnp.pad(jnp.stack([bias[:, 0, :].reshape(n_dh), bias[:, 1, :].reshape(n_dh)], axis=0),
                        ((0, 0), (0, LANES - n_dh)))
    w_q3 = w_q_up.reshape(Q_LORA, A_HEADS, A_QK)
    w_q = pad_tile(w_q3).reshape(Q_LORA, QK_PAD)
    w_q_partner = pad_tile(rot_partner(w_q3)).reshape(Q_LORA, QK_PAD)
    w_kv3 = w_kv_up.reshape(KV_LORA, A_HEADS, A_NOPE + A_VDIM)
    w_k = jnp.pad(w_kv3[:, :, :A_NOPE], ((0, 0), (0, 0), (0, HEAD_PAD - A_NOPE)))
    w_v = w_kv3[:, :, A_NOPE:]
    w_kv = jnp.concatenate([w_k.reshape(KV_LORA, QK_PAD), w_v.reshape(KV_LORA, A_WIDTH)], axis=1)
    pad_head = lambda g: jnp.stack([pad_tile(g), pad_tile(rot_partner(g))], axis=0)
    return {
        "g1": norm1_g.reshape(1, D_MODEL),
        "g2": norm2_g.reshape(1, D_MODEL),
        "w_main": jnp.concatenate([wt[0:M_WIDTH], wt[2 * M_WIDTH:o_g]], axis=0),
        "w_kt": wt[M_WIDTH:2 * M_WIDTH],
        "w_lat": w_lat,
        "gate_bias": gate_bias,
        "q_lora_g": q_lora_g.reshape(1, Q_LORA),
        "kv_lora_g": kv_lora_g.reshape(1, KV_LORA),
        "w_q": w_q.astype(BF16),
        "w_q_rot": jnp.concatenate([w_q, w_q_partner], axis=1).astype(BF16),
        "w_kv": w_kv.astype(BF16),
        "q_head_g": pad_head(q_head_g),
        "k_head_g": pad_head(k_head_g),
        "w_out_f32": w_out,
        "w_up_f32": w_mlp_up,
        "w_down_f32": w_mlp_down,
    }


def _rope_tables(T):
    rows = T // GRID_W
    row = np.repeat(np.arange(rows, dtype=np.float32), GRID_W)
    col = np.tile(np.arange(GRID_W, dtype=np.float32), rows)
    half = A_ROPE // 2
    inv = (np.float32(ROPE_BASE) ** (-np.arange(0, half, 2, dtype=np.float32) / np.float32(half))).astype(np.float32)
    ang = np.concatenate([row[:, None] * inv, col[:, None] * inv], axis=-1)
    cos, sin = np.cos(ang), np.sin(ang)
    ones = np.ones((T, A_NOPE), np.float32)
    z = lambda w: np.zeros((T, w), np.float32)
    tail = LANES - A_QK
    cos_t = np.concatenate([ones, cos, cos, z(tail)], axis=1)
    sin_t = np.concatenate([z(A_NOPE), -sin, sin, z(tail)], axis=1)
    return jnp.asarray(np.stack([cos_t, sin_t], axis=0).astype(np.float32))


def _layer_pass(x, mod3, mod_row0, mod_per_batch, wts, norm_g, rope_tab, init_state, ctx_kv, is_context):
    pre = _pre(x, mod3, mod_row0, mod_per_batch, wts, rope_tab, emit_cache=is_context)
    mq, mkt, mv, mo, stats, q, k, v = pre[:8]
    if is_context:
        wts = dict(wts, w_out=pre[10], w_up=pre[11], w_down=pre[12])
    ml = _mlstm(mq, mkt, mv, mo, stats, norm_g, init_state, emit_state=is_context,
                heads=M_HEADS if is_context else 1, seqs=MLSTM_CONTEXT_SEQS if is_context else 1)
    ha = _attn(q, k, v, ctx_kv, seqs=ATTN_CONTEXT_SEQS if is_context else 1)
    y = _post(x, ml[0], ha, mod3, mod_row0, mod_per_batch, wts)
    return y, pre[8:10], ml[1:], wts


def kernel(x_prompt, x_sample, cache_mla_ckv, cache_mla_krope, state_mlstm_C, state_mlstm_n, state_mlstm_m,
           c, c_ctx, norm1_g, norm2_g, w_ada, b_ada, w_in, mlstm_gate_b, mlstm_norm_g,
           q_lora_g, kv_lora_g, w_q_up, w_kv_up, q_head_g, k_head_g, w_out, w_mlp_up, w_mlp_down):
    depth = w_in.shape[0]
    Bd = x_sample.shape[0]
    cond8 = jnp.concatenate([c_ctx[None, :], c, jnp.zeros((8 - 1 - Bd, D_MODEL), F32)], axis=0)
    rope_tab = _rope_tables(x_sample.shape[1])

    y, z = x_prompt, x_sample
    ckvs, kropes, Cs, ns, ms = [], [], [], [], []
    for l in range(depth):
        wts = _prepare_weights(norm1_g[l], norm2_g[l], w_in[l], mlstm_gate_b[l], q_lora_g[l], kv_lora_g[l],
                               w_q_up[l], w_kv_up[l], q_head_g[l], k_head_g[l], w_out[l], w_mlp_up[l],
                               w_mlp_down[l])
        norm_g = mlstm_norm_g[l].reshape(M_HEADS, 1, M_HEAD_DIM)
        mod3 = _ada(cond8, w_ada[l], b_ada[l])
        y, (ckv, krope), (C_new, n_new, m_new), wts = _layer_pass(
            y, mod3, 0, 0, wts, norm_g, None, None, None, True)
        ckvs.append(ckv)
        kropes.append(krope)
        Cs.append(C_new)
        ns.append(n_new[:, :, :, 0, :])
        ms.append(m_new[:, :, :, 0].transpose(0, 2, 1))

        init_state = (state_mlstm_C[:, l],
                      state_mlstm_n[:, l][:, :, :, None, :],
                      jnp.broadcast_to(state_mlstm_m[:, l].transpose(0, 2, 1)[..., None],
                                       (Bd, M_HEADS, 2, LANES)))
        krope_placed = jnp.pad(cache_mla_krope[:, l], ((0, 0), (0, 0), (A_NOPE, LANES - A_QK)))
        ctx = (cache_mla_ckv[:, l], krope_placed, wts["w_kv"], wts["k_head_g"])
        z, _, _, _ = _layer_pass(z, mod3, 1, 1, wts, norm_g, rope_tab, init_state, ctx, False)

    return (y, z, jnp.stack(ckvs, axis=1), jnp.stack(kropes, axis=1), jnp.stack(Cs, axis=1),
            jnp.stack(ns, axis=1), jnp.stack(ms, axis=1))
```

```python
import functools

import jax
import jax.numpy as jnp
import numpy as np
from jax import lax
from jax.experimental import pallas as pl
from jax.experimental.pallas import tpu as pltpu

F32 = jnp.float32
BF16 = jnp.bfloat16

D_MODEL = 1024
GRID_W = 64
M_HEADS = 4
M_HEAD_DIM = 128
M_WIDTH = M_HEADS * M_HEAD_DIM
M_BLOCK = 256
A_HEADS = 8
A_NOPE = 64
A_ROPE = 32
A_QK = A_NOPE + A_ROPE
A_VDIM = 64
A_WIDTH = A_HEADS * A_VDIM
Q_LORA = 384
KV_LORA = 256
ROPE_BASE = 10000.0
D_FF = 4 * D_MODEL
EPS = 1e-6

LANES = 128
SUBLANES = 8
LOG2E = 1.4426950408889634
HEAD_PAD = LANES
QK_PAD = A_HEADS * HEAD_PAD
N_GATES = 4 * M_HEADS
LAT_WIDTH = Q_LORA + KV_LORA + LANES
VMEM_LIMIT = 56 * 1024 * 1024

TOKEN_TILE = 512
POST_TILE = 1024
MLSTM_CONTEXT_SEQS = 2
MLSTM_LATENT_HEADS = 2
ATTN_CONTEXT_SEQS = 4
Q_TILE = 1024
Q_SUBTILE = 256
ADA_TILE_N = 1536
FF_TILE = 1024


def _dot(a, b):
    return jnp.dot(a, b, preferred_element_type=F32)


def _dot_nt(a, b):
    return lax.dot_general(a, b, (((1,), (1,)), ((), ())), preferred_element_type=F32)


def _dot_tn(a, b):
    return lax.dot_general(a, b, (((0,), (0,)), ((), ())), preferred_element_type=F32)


def _rms(x, g):
    y = x * lax.rsqrt(jnp.mean(x * x, axis=-1, keepdims=True) + EPS)
    return y * g


def _params(sem):
    return pltpu.CompilerParams(dimension_semantics=sem, vmem_limit_bytes=VMEM_LIMIT)


def _const_spec(shape):
    zeros = (0,) * len(shape)
    return pl.BlockSpec(shape, lambda *_: zeros, pipeline_mode=pl.Buffered(1))


def _ada_kernel(cond_ref, w_ref, b_ref, o_ref):
    c = cond_ref[...]
    s = (c * jax.nn.sigmoid(c)).astype(BF16)
    o_ref[...] = _dot(s, w_ref[...].astype(BF16)) + b_ref[...]


def _ada(cond8, w_ada, b_ada):
    n = w_ada.shape[1]
    return pl.pallas_call(
        _ada_kernel,
        grid=(n // ADA_TILE_N,),
        in_specs=[
            pl.BlockSpec((8, D_MODEL), lambda j: (0, 0)),
            pl.BlockSpec((D_MODEL, ADA_TILE_N), lambda j: (0, j)),
            pl.BlockSpec((1, ADA_TILE_N), lambda j: (0, j)),
        ],
        out_specs=pl.BlockSpec((8, ADA_TILE_N), lambda j: (0, j)),
        out_shape=jax.ShapeDtypeStruct((8, n), F32),
        compiler_params=_params(("parallel",)),
        name="ada",
    )(cond8, w_ada, b_ada.reshape(1, n))


def _write_heads(src, extra, g_pad, rot, dst_ref):
    for h in range(A_HEADS):
        sl = slice(h * HEAD_PAD, (h + 1) * HEAD_PAD)
        xh = src[:, sl]
        if extra is not None:
            xh = xh + extra
        ss = jnp.sum(xh * xh, axis=-1, keepdims=True) * (1.0 / A_QK)
        r = lax.rsqrt(ss + EPS)
        if rot is None:
            y = xh * r * g_pad
        else:
            partner, cos_g, sin_g = rot
            ph = partner if partner.shape[1] == HEAD_PAD else partner[:, sl]
            y = (xh * cos_g + ph * sin_g) * r
        dst_ref[:, sl] = y.astype(dst_ref.dtype)


def _time_scan(x, op, identity, reverse):
    n = x.shape[0]
    row = lax.broadcasted_iota(jnp.int32, x.shape, 0)
    shift = 1
    while shift < n:
        if shift < SUBLANES:
            if reverse:
                moved = jnp.where(row < n - shift, pltpu.roll(x, n - shift, 0), identity)
            else:
                moved = jnp.where(row >= shift, pltpu.roll(x, shift, 0), identity)
        else:
            fill = jnp.full((shift, x.shape[1]), identity, x.dtype)
            moved = (jnp.concatenate([x[shift:], fill], axis=0) if reverse
                     else jnp.concatenate([fill, x[:n - shift]], axis=0))
        x = op(x, moved)
        shift *= 2
    return x


def _mod_row(mod_ref, row0, per_batch):
    if per_batch:
        return mod_ref[pl.ds(row0 + pl.program_id(0) * per_batch, 1), :]
    return mod_ref[row0:row0 + 1, :]


def _pre_kernel(has_rope, emit_cache, mod_row0, mod_per_batch, *refs):
    (x_ref, mod_ref, g1_ref, wmain_ref, wkt_ref, wlat_ref, gbias_ref, qlg_ref, kvg_ref, wq_ref, wkv_ref,
     qhg_ref, khg_ref) = refs[:13]
    pos = 13
    if has_rope:
        rope_ref = refs[pos]
        pos += 1
    if emit_cache:
        cast_in = refs[pos:pos + 3]
        pos += 3
    (mq_ref, mkt_ref, mv_ref, mo_ref, stats_ref, q_ref, k_ref, v_ref) = refs[pos:pos + 8]
    pos += 8

    x = x_ref[...]
    mod = _mod_row(mod_ref, mod_row0, mod_per_batch)
    sh1 = mod[:, 0:D_MODEL]
    sc1 = mod[:, D_MODEL:2 * D_MODEL]
    h = _rms(x, g1_ref[...]) * (1.0 + sc1) + sh1
    hb = h.astype(BF16)

    plat = _dot_nt(hb, wlat_ref[...])
    q_lat = plat[:, 0:Q_LORA]
    kv_lat = plat[:, Q_LORA:Q_LORA + KV_LORA]
    tail = plat[:, Q_LORA + KV_LORA:LAT_WIDTH]
    tail_f = pltpu.roll(tail, LANES - 2 * M_HEADS, 1)
    tail2 = pltpu.roll(tail, LANES - A_ROPE, 1)

    lane = lax.broadcasted_iota(jnp.int32, (1, LANES), 1)
    fwd = lane < M_HEADS
    gate_i = tail + gbias_ref[0:1, :]
    gate_f = tail_f + gbias_ref[1:2, :]
    log_f = jnp.minimum(gate_f, 0.0) - jnp.log1p(jnp.exp(-jnp.abs(gate_f)))
    def block_scan(v, op, identity):
        parts = [v[r:r + M_BLOCK] for r in range(0, v.shape[0], M_BLOCK)]
        return jnp.where(fwd, jnp.concatenate([_time_scan(p, op, identity, False) for p in parts], axis=0),
                         jnp.concatenate([_time_scan(p, op, identity, True) for p in parts], axis=0))

    b = block_scan(log_f, jnp.add, 0.0)
    a = gate_i - b
    amax = block_scan(a, jnp.maximum, -jnp.inf)
    stats_ref[0] = b.T[0:2 * M_HEADS, :]
    stats_ref[1] = a.T[0:2 * M_HEADS, :]
    stats_ref[2] = amax.T[0:2 * M_HEADS, :]

    krope_placed = jnp.where((lane >= A_NOPE) & (lane < A_QK), tail, 0.0)
    ckv = _rms(kv_lat, kvg_ref[...])
    qn = _rms(q_lat, qlg_ref[...])
    qf = _dot(qn.astype(BF16), wq_ref[...])
    kvf = _dot(ckv.astype(BF16), wkv_ref[...])
    v_ref[...] = kvf[:, QK_PAD:QK_PAD + A_WIDTH].astype(BF16)
    q_rot = k_rot = None
    qhg = qhg_ref[...] * (A_QK ** -0.5 * LOG2E)
    if has_rope:
        cos_t, sin_t = rope_ref[0], rope_ref[1]
        q_rot = (qf[:, QK_PAD:2 * QK_PAD], cos_t * qhg[0:1, :], sin_t * qhg[1:2, :])
        k_rot = (tail2, cos_t * khg_ref[0:1, :], sin_t * khg_ref[1:2, :])
    _write_heads(qf, None, qhg[0:1, :], q_rot, q_ref)
    _write_heads(kvf, krope_placed, khg_ref[0:1, :], k_rot, k_ref)

    if emit_cache:
        ckv_ref, krope_ref = refs[pos:pos + 2]
        ckv_ref[...] = ckv
        krope_ref[...] = tail[:, A_NOPE:A_QK]
        for src_ref, dst_ref in zip(cast_in, refs[pos + 2:pos + 5]):
            dst_ref[...] = src_ref[...].astype(BF16)

    pm = _dot_nt(hb, wmain_ref[...])
    mq_ref[...] = pm[:, 0:M_WIDTH].astype(BF16)
    mv_ref[...] = pm[:, M_WIDTH:2 * M_WIDTH].astype(BF16)
    mo_ref[...] = pm[:, 2 * M_WIDTH:3 * M_WIDTH]
    mkt_ref[...] = _dot_nt(wkt_ref[...], hb) * (M_HEAD_DIM ** -0.5)


def _pre(x, mod3, mod_row0, mod_per_batch, wts, rope_tab, emit_cache):
    shape = x.shape
    tm = TOKEN_TILE
    if shape[1] < tm:
        assert not mod_per_batch and tm % shape[1] == 0
        x = x.reshape(-1, tm, shape[2])
    B, T, _ = x.shape
    has_rope = rope_tab is not None
    tok = lambda w: pl.BlockSpec((None, tm, w), lambda b, i: (b, i, 0))
    in_specs = [
        tok(D_MODEL),
        _const_spec((8, 6 * D_MODEL)),
        _const_spec((1, D_MODEL)),
        _const_spec((3 * M_WIDTH, D_MODEL)),
        _const_spec((M_WIDTH, D_MODEL)),
        _const_spec((LAT_WIDTH, D_MODEL)),
        _const_spec((2, LANES)),
        _const_spec((1, Q_LORA)),
        _const_spec((1, KV_LORA)),
        _const_spec((Q_LORA, 2 * QK_PAD if has_rope else QK_PAD)),
        _const_spec((KV_LORA, QK_PAD + A_WIDTH)),
        _const_spec((2, HEAD_PAD)),
        _const_spec((2, HEAD_PAD)),
    ]
    assert tm % M_BLOCK == 0
    args = [x, mod3, wts["g1"], wts["w_main"], wts["w_kt"], wts["w_lat"], wts["gate_bias"], wts["q_lora_g"],
            wts["kv_lora_g"], wts["w_q_rot"] if has_rope else wts["w_q"], wts["w_kv"],
            wts["q_head_g"], wts["k_head_g"]]
    if has_rope:
        in_specs.append(pl.BlockSpec((2, tm, HEAD_PAD), lambda b, i: (0, i, 0)))
        args.append(rope_tab)
    out_specs = [tok(M_WIDTH),
                 pl.BlockSpec((None, M_WIDTH, tm), lambda b, i: (b, 0, i)),
                 tok(M_WIDTH), tok(M_WIDTH),
                 pl.BlockSpec((None, 3, 2 * M_HEADS, tm), lambda b, i: (b, 0, 0, i)),
                 tok(QK_PAD), tok(QK_PAD), tok(A_WIDTH)]
    out_shape = [
        jax.ShapeDtypeStruct((B, T, M_WIDTH), BF16),
        jax.ShapeDtypeStruct((B, M_WIDTH, T), F32),
        jax.ShapeDtypeStruct((B, T, M_WIDTH), BF16),
        jax.ShapeDtypeStruct((B, T, M_WIDTH), F32),
        jax.ShapeDtypeStruct((B, 3, 2 * M_HEADS, T), F32),
        jax.ShapeDtypeStruct((B, T, QK_PAD), BF16),
        jax.ShapeDtypeStruct((B, T, QK_PAD), BF16),
        jax.ShapeDtypeStruct((B, T, A_WIDTH), BF16),
    ]
    if emit_cache:
        out_specs += [tok(KV_LORA), tok(A_ROPE)]
        out_shape += [jax.ShapeDtypeStruct((B, T, KV_LORA), F32),
                      jax.ShapeDtypeStruct((B, T, A_ROPE), F32)]
        steps, per_b = B * (T // tm), T // tm
        for w in (wts["w_out_f32"], wts["w_up_f32"], wts["w_down_f32"]):
            rows = w.shape[0] // steps
            assert rows * steps == w.shape[0] and rows % (2 * SUBLANES) == 0
            spec = pl.BlockSpec((rows, w.shape[1]), lambda b, i: (b * per_b + i, 0))
            in_specs.append(spec)
            args.append(w)
            out_specs.append(spec)
            out_shape.append(jax.ShapeDtypeStruct(w.shape, BF16))
    outs = pl.pallas_call(
        functools.partial(_pre_kernel, has_rope, emit_cache, mod_row0, mod_per_batch),
        grid=(B, T // tm),
        in_specs=in_specs,
        out_specs=out_specs,
        out_shape=out_shape,
        compiler_params=_params(("parallel", "parallel")),
        name="pre_latent" if has_rope else "pre_context",
    )(*args)
    keep = (1, 4, 10, 11, 12)
    return [o if n in keep else o.reshape(shape[:2] + o.shape[2:]) for n, o in enumerate(outs)]


def _rows_to_lane_broadcast(rows, spread):
    x = jnp.concatenate(rows, axis=0)
    p1 = x.astype(BF16)
    r1 = x - p1.astype(F32)
    p2 = r1.astype(BF16)
    p3 = (r1 - p2.astype(F32)).astype(BF16)
    pad = jnp.zeros((spread.shape[0] - 3 * len(rows), x.shape[1]), BF16)
    return _dot_tn(jnp.concatenate([p1, p2, p3, pad], axis=0), spread)


def _mlstm_gate_rows(b_row, a_row, amax_row, forward, m):
    L = b_row.shape[1]
    last = slice(L - 1, L) if forward else slice(0, 1)
    total = b_row[:, last]
    g_row = jnp.maximum(m, amax_row)
    m_new = total + jnp.maximum(m, amax_row[:, last])
    w_key_row = jnp.exp2((a_row + (total - m_new)) * LOG2E)
    decay = jnp.exp(total + m - m_new)
    return g_row * LOG2E, (b_row + g_row) * LOG2E, a_row * LOG2E, w_key_row, decay, m_new


def _mlstm_block(s_raw, q, kt, v_aug, g2, mt2, a2_row, w_key_row, decay, allow, CN, m):
    w_intra = jnp.exp2(jnp.where(allow, a2_row - jnp.concatenate([g2, g2], axis=1), -jnp.inf))
    w_inter = jnp.exp2(m * LOG2E - g2)
    s = (s_raw * w_intra).astype(BF16)
    nd = _dot(s, v_aug) + jnp.concatenate([w_inter, w_inter], axis=1) * _dot(q, CN.astype(BF16))
    num, den = nd[:, 0:M_HEAD_DIM], nd[:, M_HEAD_DIM:2 * M_HEAD_DIM]
    h = num / jnp.maximum(jnp.abs(den), jnp.exp2(-mt2))
    CN_new = decay * CN + _dot((kt * w_key_row).astype(BF16), v_aug)
    return h, CN_new


def _mlstm_kernel(has_init, emit_state, n_blocks, heads, seqs, *refs):
    q_ref, kt_ref, v_ref, mo_ref, stats_ref, ng_ref, spread_ref = refs[:7]
    pos = 7
    if has_init:
        c0_ref, n0_ref, m0_ref = refs[pos:pos + 3]
        pos += 3
    hm_ref = refs[pos]
    pos += 1
    if emit_state:
        c_ref, n_ref, m_ref = refs[pos:pos + 3]
        pos += 3

    L, Dh = M_BLOCK, M_HEAD_DIM
    t_idx = lax.broadcasted_iota(jnp.int32, (L, L), 0)
    s_idx = lax.broadcasted_iota(jnp.int32, (L, L), 1)
    allow = (s_idx <= t_idx, s_idx >= t_idx)
    spread = spread_ref[...]
    ones = jnp.ones((L, Dh), BF16)

    def time_lanes(sq, c):
        start = (sq * n_blocks + c) * L
        return slice(start, start + L)

    def lane_broadcast_n(n_row):
        return jnp.broadcast_to(n_row, (Dh, Dh)).T

    def init_state(sq, j, d):
        if has_init:
            return (jnp.concatenate([c0_ref[sq, d, j], lane_broadcast_n(n0_ref[sq, d, j])], axis=1),
                    m0_ref[sq, j, d:d + 1, 0:1])
        return jnp.zeros((Dh, 2 * Dh), F32), jnp.zeros((1, 1), F32)

    def gate_rows(sq, j, c, d, m):
        head = j if heads == M_HEADS else pl.program_id(1) * heads + j
        r, lanes = pl.ds(d * M_HEADS + head, 1), time_lanes(sq, c)
        return _mlstm_gate_rows(stats_ref[0, r, lanes], stats_ref[1, r, lanes], stats_ref[2, r, lanes], d == 0, m)

    def blocks(sq, j, jobs, states):
        loaded, rows6, cols_in = {}, [], []
        for (c, d), (CN, m) in zip(jobs, states):
            rows6.append(gate_rows(sq, j, c, d, m))
            cols_in += [rows6[-1][0], rows6[-1][1]]
            if c not in loaded:
                rows, cols = slice(c * L, (c + 1) * L), slice(j * Dh, (j + 1) * Dh)
                q, kt = q_ref[sq, rows, cols], kt_ref[cols, time_lanes(sq, c)]
                v_aug = jnp.concatenate([v_ref[sq, rows, cols], ones], axis=1)
                loaded[c] = (_dot(q, kt.astype(BF16)), q, kt, v_aug)
        cols_out = _rows_to_lane_broadcast(cols_in, spread)
        hs, new_states = [], []
        for idx, ((c, d), (CN, m)) in enumerate(zip(jobs, states)):
            g2 = cols_out[:, (2 * idx) * LANES:(2 * idx + 1) * LANES]
            mt2 = cols_out[:, (2 * idx + 1) * LANES:(2 * idx + 2) * LANES]
            _, _, a2_row, w_key_row, decay, m_new = rows6[idx]
            h, CN_new = _mlstm_block(*loaded[c], g2, mt2, a2_row, w_key_row, decay, allow[d], CN, m)
            hs.append(h)
            new_states.append((CN_new, m_new))
        return hs, new_states

    def finish(sq, j, rows, hs):
        cols = slice(j * Dh, (j + 1) * Dh)
        hn = _rms(hs, ng_ref[j])
        hm_ref[sq, rows, cols] = (hn * jax.nn.sigmoid(mo_ref[sq, rows, cols])).astype(hm_ref.dtype)

    def emit(sq, j, d, state):
        CN, m = state
        c_ref[sq, d, j] = CN[:, 0:Dh]
        n_ref[sq, d, j] = CN[:, Dh:2 * Dh].T[0:1, :]
        m_ref[sq, j, d:d + 1, :] = jnp.broadcast_to(m, (1, LANES))

    if n_blocks > 1:
        hf_scr, hb_scr = refs[pos:pos + 2]

    for sq, j in [(sq, j) for sq in range(seqs) for j in range(heads)]:
        cols = slice(j * Dh, (j + 1) * Dh)
        states = [init_state(sq, j, 0), init_state(sq, j, 1)]
        if n_blocks == 1:
            (hf, hb), states = blocks(sq, j, [(0, 0), (0, 1)], states)
            finish(sq, j, slice(0, L), hf + hb)
        else:
            for step in range(n_blocks):
                cf, cb = step, n_blocks - 1 - step
                (hf, hb), states = blocks(sq, j, [(cf, 0), (cb, 1)], states)
                hf_scr[cf * L:(cf + 1) * L, cols] = hf
                hb_scr[cb * L:(cb + 1) * L, cols] = hb
            finish(sq, j, slice(None), hf_scr[:, cols] + hb_scr[:, cols])
        if emit_state:
            emit(sq, j, 0, states[0])
            emit(sq, j, 1, states[1])


def _mlstm(mq, mkt, mv, mo, stats, norm_g, init_state, emit_state, heads, seqs):
    B, T, _ = mq.shape
    H, Dh, L = M_HEADS, M_HEAD_DIM, M_BLOCK
    nb = T // L
    w = heads * Dh
    has_init = init_state is not None
    n_rows = 4
    spread = jnp.tile(jnp.repeat(jnp.eye(n_rows, dtype=BF16), LANES, axis=1), (3, 1))
    spread = jnp.pad(spread, ((0, 2 * SUBLANES - 3 * n_rows), (0, 0)))
    tok = pl.BlockSpec((seqs, T, w), lambda b, h: (b, 0, h))
    assert mkt.shape == (B // seqs, M_WIDTH, seqs * T) and stats.shape == (B // seqs, 3, 2 * H, seqs * T)
    in_specs = [tok, pl.BlockSpec((None, w, seqs * T), lambda b, h: (b, h, 0)), tok, tok,
                pl.BlockSpec((None, 3, 2 * H, seqs * T), lambda b, h: (b, 0, 0, 0)),
                pl.BlockSpec((heads, 1, Dh), lambda b, h: (h, 0, 0)),
                _const_spec((2 * SUBLANES, n_rows * LANES))]
    args = [mq, mkt, mv, mo, stats, norm_g, spread]
    state_specs = [pl.BlockSpec((seqs, 2, heads, Dh, Dh), lambda b, h: (b, 0, h, 0, 0)),
                   pl.BlockSpec((seqs, 2, heads, 1, Dh), lambda b, h: (b, 0, h, 0, 0)),
                   pl.BlockSpec((seqs, heads, 2, LANES), lambda b, h: (b, h, 0, 0))]
    if has_init:
        in_specs += state_specs
        args += list(init_state)
    out_specs = [tok]
    out_shape = [jax.ShapeDtypeStruct((B, T, M_WIDTH), BF16)]
    if emit_state:
        out_specs += state_specs
        out_shape += [jax.ShapeDtypeStruct((B, 2, H, Dh, Dh), F32),
                      jax.ShapeDtypeStruct((B, 2, H, 1, Dh), F32),
                      jax.ShapeDtypeStruct((B, H, 2, LANES), F32)]
    scratch = [] if nb == 1 else [pltpu.VMEM((T, w), F32), pltpu.VMEM((T, w), F32)]
    return pl.pallas_call(
        functools.partial(_mlstm_kernel, has_init, emit_state, nb, heads, seqs),
        grid=(B // seqs, H // heads),
        in_specs=in_specs,
        out_specs=out_specs,
        out_shape=out_shape,
        scratch_shapes=scratch,
        compiler_params=_params(("parallel", "parallel")),
        name="mlstm_latent" if has_init else "mlstm_context",
    )(*args)


def _attn_kernel(has_ctx, seqs, *refs):
    if has_ctx:
        q_ref, k_ref, v_ref, ckv_ref, krp_ref, wkv_ref, khg_ref, o_ref, kc_ref, vc_ref = refs

        @pl.when(pl.program_id(1) == 0)
        def _():
            kvf = _dot(ckv_ref[0].astype(BF16), wkv_ref[...])
            vc_ref[...] = kvf[:, QK_PAD:QK_PAD + A_WIDTH].astype(BF16)
            _write_heads(kvf, krp_ref[0], khg_ref[0:1, :], None, kc_ref)
    else:
        q_ref, k_ref, v_ref, o_ref = refs
    lane = lax.broadcasted_iota(jnp.int32, (1, LANES), 1)
    ones = lambda n: jnp.ones((n, LANES), BF16)
    tq = q_ref.shape[1]
    sub = min(Q_SUBTILE, tq)
    for sq, r0, pair in [(sq, r0, pair) for sq in range(seqs) for r0 in range(0, tq, sub)
                         for pair in range(A_HEADS // 2)]:
        rows = slice(r0, r0 + sub)
        vsl = slice(pair * LANES, (pair + 1) * LANES)
        v_aug = jnp.concatenate([v_ref[sq, :, vsl], ones(v_ref.shape[1])], axis=1)
        if has_ctx:
            vc_aug = jnp.concatenate([vc_ref[:, vsl], ones(vc_ref.shape[0])], axis=1)
        outs = []
        for e in range(2):
            hsl = slice((2 * pair + e) * HEAD_PAD, (2 * pair + e + 1) * HEAD_PAD)
            qh = q_ref[sq, rows, hsl]
            s = _dot_nt(qh, k_ref[sq, :, hsl])
            mx = jnp.max(s, axis=1, keepdims=True)
            if has_ctx:
                sc = _dot_nt(qh, kc_ref[:, hsl])
                mx = jnp.maximum(mx, jnp.max(sc, axis=1, keepdims=True))
            od = _dot(jnp.exp2(s - mx).astype(BF16), v_aug)
            if has_ctx:
                od = od + _dot(jnp.exp2(sc - mx).astype(BF16), vc_aug)
            outs.append(od[:, 0:LANES] / od[:, LANES:2 * LANES])
        o_ref[sq, rows, vsl] = jnp.where(lane < A_VDIM, outs[0], outs[1]).astype(o_ref.dtype)


def _attn(q, k, v, ctx, seqs):
    B, T, _ = q.shape
    tq = min(Q_TILE, T)
    has_ctx = ctx is not None
    full = lambda n, w: pl.BlockSpec((seqs, n, w), lambda b, i: (b, 0, 0))
    in_specs = [pl.BlockSpec((seqs, tq, QK_PAD), lambda b, i: (b, i, 0)), full(T, QK_PAD), full(T, A_WIDTH)]
    args = [q, k, v]
    scratch = []
    if has_ctx:
        assert seqs == 1
        P = ctx[0].shape[1]
        in_specs += [full(P, KV_LORA), full(P, HEAD_PAD), _const_spec((KV_LORA, QK_PAD + A_WIDTH)),
                     _const_spec((2, HEAD_PAD))]
        args += list(ctx)
        scratch = [pltpu.VMEM((P, QK_PAD), BF16), pltpu.VMEM((P, A_WIDTH), BF16)]
    return pl.pallas_call(
        functools.partial(_attn_kernel, has_ctx, seqs),
        grid=(B // seqs, T // tq),
        in_specs=in_specs,
        out_specs=pl.BlockSpec((seqs, tq, A_WIDTH), lambda b, i: (b, i, 0)),
        out_shape=jax.ShapeDtypeStruct((B, T, A_WIDTH), BF16),
        scratch_shapes=scratch,
        compiler_params=_params(("parallel", "arbitrary" if has_ctx else "parallel")),
        name="attn_latent" if has_ctx else "attn_context",
    )(*args)


def _post_kernel(mod_row0, mod_per_batch, x_ref, hm_ref, ha_ref, mod_ref, g2_ref, wout_ref, wup_ref, wdown_ref,
                 y_ref):
    mod = _mod_row(mod_ref, mod_row0, mod_per_batch)
    gate1 = mod[:, 2 * D_MODEL:3 * D_MODEL]
    sh2 = mod[:, 3 * D_MODEL:4 * D_MODEL]
    sc2 = mod[:, 4 * D_MODEL:5 * D_MODEL]
    gate2 = mod[:, 5 * D_MODEL:6 * D_MODEL]
    mix = jnp.concatenate([hm_ref[...], ha_ref[...]], axis=-1)
    x1 = x_ref[...] + gate1 * _dot(mix, wout_ref[...])
    h2 = (_rms(x1, g2_ref[...]) * (1.0 + sc2) + sh2).astype(BF16)
    acc = jnp.zeros_like(x1)
    for c in range(D_FF // FF_TILE):
        sl = slice(c * FF_TILE, (c + 1) * FF_TILE)
        u = jnp.maximum(_dot(h2, wup_ref[:, sl]), 0.0)
        acc = acc + _dot((u * u).astype(BF16), wdown_ref[sl, :])
    y_ref[...] = x1 + gate2 * acc


def _post(x, hm, ha, mod3, mod_row0, mod_per_batch, wts):
    shape = x.shape
    if not mod_per_batch:
        x, hm, ha = (a.reshape(1, -1, a.shape[-1]) for a in (x, hm, ha))
    B, T, _ = x.shape
    tm = POST_TILE
    tok = lambda w: pl.BlockSpec((None, tm, w), lambda b, i: (b, i, 0))
    return _post_call(x, hm, ha, mod3, mod_row0, mod_per_batch, wts, B, T, tm, tok).reshape(shape)


def _post_call(x, hm, ha, mod3, mod_row0, mod_per_batch, wts, B, T, tm, tok):
    return pl.pallas_call(
        functools.partial(_post_kernel, mod_row0, mod_per_batch),
        grid=(B, T // tm),
        in_specs=[tok(D_MODEL), tok(M_WIDTH), tok(A_WIDTH),
                  _const_spec((8, 6 * D_MODEL)),
                  _const_spec((1, D_MODEL)),
                  _const_spec((M_WIDTH + A_WIDTH, D_MODEL)),
                  _const_spec((D_MODEL, D_FF)),
                  _const_spec((D_FF, D_MODEL))],
        out_specs=tok(D_MODEL),
        out_shape=jax.ShapeDtypeStruct((B, T, D_MODEL), F32),
        compiler_params=_params(("parallel", "parallel")),
        name="post",
    )(x, hm, ha, mod3, wts["g2"], wts["w_out"], wts["w_up"], wts["w_down"])


def _prepare_weights(norm1_g, norm2_g, w_in, mlstm_gate_b, q_lora_g, kv_lora_g, w_q_up, w_kv_up,
                     q_head_g, k_head_g, w_out, w_mlp_up, w_mlp_down):
    o_g = 4 * M_WIDTH
    o_q = o_g + N_GATES
    o_kv = o_q + Q_LORA
    o_kr = o_kv + KV_LORA
    half = A_ROPE // 2
    n_dh = 2 * M_HEADS
    wt = w_in.T.astype(BF16)
    w_gate = wt[o_g:o_q].reshape(2, 2, M_HEADS, D_MODEL)
    bias = mlstm_gate_b.reshape(2, 2, M_HEADS)

    def rot_partner(a):
        z = jnp.zeros(a.shape[:-1] + (A_NOPE,), a.dtype)
        return jnp.concatenate([z, a[..., A_NOPE + half:A_QK], a[..., A_NOPE:A_NOPE + half]], axis=-1)

    pad_tile = lambda a: jnp.pad(a, [(0, 0)] * (a.ndim - 1) + [(0, HEAD_PAD - A_QK)])
    w_kr = wt[o_kr:o_kr + A_ROPE]
    w_kr_partner = jnp.concatenate([w_kr[half:], w_kr[:half]], axis=0)
    w_lat = jnp.concatenate([wt[o_q:o_kr], w_gate[:, 0].reshape(n_dh, D_MODEL), w_gate[:, 1].reshape(n_dh, D_MODEL),
                             jnp.zeros((A_NOPE - 2 * n_dh, D_MODEL), BF16), w_kr, w_kr_partner], axis=0)
    gate_bias = jnp.pad(jnp.stack([bias[:, 0, :].reshape(n_dh), bias[:, 1, :].reshape(n_dh)], axis=0),
                        ((0, 0), (0, LANES - n_dh)))
    w_q3 = w_q_up.reshape(Q_LORA, A_HEADS, A_QK)
    w_q = pad_tile(w_q3).reshape(Q_LORA, QK_PAD)
    w_q_partner = pad_tile(rot_partner(w_q3)).reshape(Q_LORA, QK_PAD)
    w_kv3 = w_kv_up.reshape(KV_LORA, A_HEADS, A_NOPE + A_VDIM)
    w_k = jnp.pad(w_kv3[:, :, :A_NOPE], ((0, 0), (0, 0), (0, HEAD_PAD - A_NOPE)))
    w_v = w_kv3[:, :, A_NOPE:]
    w_kv = jnp.concatenate([w_k.reshape(KV_LORA, QK_PAD), w_v.reshape(KV_LORA, A_WIDTH)], axis=1)
    pad_head = lambda g: jnp.stack([pad_tile(g), pad_tile(rot_partner(g))], axis=0)
    return {
        "g1": norm1_g.reshape(1, D_MODEL),
        "g2": norm2_g.reshape(1, D_MODEL),
        "w_main": jnp.concatenate([wt[0:M_WIDTH], wt[2 * M_WIDTH:o_g]], axis=0),
        "w_kt": wt[M_WIDTH:2 * M_WIDTH],
        "w_lat": w_lat,
        "gate_bias": gate_bias,
        "q_lora_g": q_lora_g.reshape(1, Q_LORA),
        "kv_lora_g": kv_lora_g.reshape(1, KV_LORA),
        "w_q": w_q.astype(BF16),
        "w_q_rot": jnp.concatenate([w_q, w_q_partner], axis=1).astype(BF16),
        "w_kv": w_kv.astype(BF16),
        "q_head_g": pad_head(q_head_g),
        "k_head_g": pad_head(k_head_g),
        "w_out_f32": w_out,
        "w_up_f32": w_mlp_up,
        "w_down_f32": w_mlp_down,
    }


def _rope_tables(T):
    rows = T // GRID_W
    row = np.repeat(np.arange(rows, dtype=np.float32), GRID_W)
    col = np.tile(np.arange(GRID_W, dtype=np.float32), rows)
    half = A_ROPE // 2
    inv = (np.float32(ROPE_BASE) ** (-np.arange(0, half, 2, dtype=np.float32) / np.float32(half))).astype(np.float32)
    ang = np.concatenate([row[:, None] * inv, col[:, None] * inv], axis=-1)
    cos, sin = np.cos(ang), np.sin(ang)
    ones = np.ones((T, A_NOPE), np.float32)
    z = lambda w: np.zeros((T, w), np.float32)
    tail = LANES - A_QK
    cos_t = np.concatenate([ones, cos, cos, z(tail)], axis=1)
    sin_t = np.concatenate([z(A_NOPE), -sin, sin, z(tail)], axis=1)
    return jnp.asarray(np.stack([cos_t, sin_t], axis=0).astype(np.float32))


def _layer_pass(x, mod3, mod_row0, mod_per_batch, wts, norm_g, rope_tab, init_state, ctx_kv, is_context):
    pre = _pre(x, mod3, mod_row0, mod_per_batch, wts, rope_tab, emit_cache=is_context)
    mq, mkt, mv, mo, stats, q, k, v = pre[:8]
    if is_context:
        wts = dict(wts, w_out=pre[10], w_up=pre[11], w_down=pre[12])
    ml = _mlstm(mq, mkt, mv, mo, stats, norm_g, init_state, emit_state=is_context,
                heads=M_HEADS if is_context else MLSTM_LATENT_HEADS,
                seqs=MLSTM_CONTEXT_SEQS if is_context else 1)
    ha = _attn(q, k, v, ctx_kv, seqs=ATTN_CONTEXT_SEQS if is_context else 1)
    y = _post(x, ml[0], ha, mod3, mod_row0, mod_per_batch, wts)
    return y, pre[8:10], ml[1:], wts


def kernel(x_prompt, x_sample, cache_mla_ckv, cache_mla_krope, state_mlstm_C, state_mlstm_n, state_mlstm_m,
           c, c_ctx, norm1_g, norm2_g, w_ada, b_ada, w_in, mlstm_gate_b, mlstm_norm_g,
           q_lora_g, kv_lora_g, w_q_up, w_kv_up, q_head_g, k_head_g, w_out, w_mlp_up, w_mlp_down):
    depth = w_in.shape[0]
    Bd = x_sample.shape[0]
    cond8 = jnp.concatenate([c_ctx[None, :], c, jnp.zeros((8 - 1 - Bd, D_MODEL), F32)], axis=0)
    rope_tab = _rope_tables(x_sample.shape[1])

    y, z = x_prompt, x_sample
    ckvs, kropes, Cs, ns, ms = [], [], [], [], []
    for l in range(depth):
        wts = _prepare_weights(norm1_g[l], norm2_g[l], w_in[l], mlstm_gate_b[l], q_lora_g[l], kv_lora_g[l],
                               w_q_up[l], w_kv_up[l], q_head_g[l], k_head_g[l], w_out[l], w_mlp_up[l],
                               w_mlp_down[l])
        norm_g = mlstm_norm_g[l].reshape(M_HEADS, 1, M_HEAD_DIM)
        mod3 = _ada(cond8, w_ada[l], b_ada[l])
        y, (ckv, krope), (C_new, n_new, m_new), wts = _layer_pass(
            y, mod3, 0, 0, wts, norm_g, None, None, None, True)
        ckvs.append(ckv)
        kropes.append(krope)
        Cs.append(C_new)
        ns.append(n_new[:, :, :, 0, :])
        ms.append(m_new[:, :, :, 0].transpose(0, 2, 1))

        init_state = (state_mlstm_C[:, l],
                      state_mlstm_n[:, l][:, :, :, None, :],
                      jnp.broadcast_to(state_mlstm_m[:, l].transpose(0, 2, 1)[..., None],
                                       (Bd, M_HEADS, 2, LANES)))
        krope_placed = jnp.pad(cache_mla_krope[:, l], ((0, 0), (0, 0), (A_NOPE, LANES - A_QK)))
        ctx = (cache_mla_ckv[:, l], krope_placed, wts["w_kv"], wts["k_head_g"])
        z, _, _, _ = _layer_pass(z, mod3, 1, 1, wts, norm_g, rope_tab, init_state, ctx, False)

    return (y, z, jnp.stack(ckvs, axis=1), jnp.stack(kropes, axis=1), jnp.stack(Cs, axis=1),
            jnp.stack(ns, axis=1), jnp.stack(ms, axis=1))
```

```python
import functools

import jax
import jax.numpy as jnp
import numpy as np
from jax import lax
from jax.experimental import pallas as pl
from jax.experimental.pallas import tpu as pltpu

F32 = jnp.float32
BF16 = jnp.bfloat16

D_MODEL = 1024
GRID_W = 64
M_HEADS = 4
M_HEAD_DIM = 128
M_WIDTH = M_HEADS * M_HEAD_DIM
M_BLOCK = 256
A_HEADS = 8
A_NOPE = 64
A_ROPE = 32
A_QK = A_NOPE + A_ROPE
A_VDIM = 64
A_WIDTH = A_HEADS * A_VDIM
Q_LORA = 384
KV_LORA = 256
ROPE_BASE = 10000.0
D_FF = 4 * D_MODEL
EPS = 1e-6

LANES = 128
SUBLANES = 8
LOG2E = 1.4426950408889634
HEAD_PAD = LANES
QK_PAD = A_HEADS * HEAD_PAD
N_GATES = 4 * M_HEADS
LAT_WIDTH = Q_LORA + KV_LORA + LANES
VMEM_LIMIT = 56 * 1024 * 1024

TOKEN_TILE = 512
POST_TILE = 1024
MLSTM_CONTEXT_SEQS = 2
MLSTM_LATENT_HEADS = 1
ATTN_CONTEXT_SEQS = 4
Q_TILE = 1024
Q_SUBTILE = 256
ADA_TILE_N = 1536
FF_TILE = 1024


def _dot(a, b):
    return jnp.dot(a, b, preferred_element_type=F32)


def _dot_nt(a, b):
    return lax.dot_general(a, b, (((1,), (1,)), ((), ())), preferred_element_type=F32)


def _dot_tn(a, b):
    return lax.dot_general(a, b, (((0,), (0,)), ((), ())), preferred_element_type=F32)


def _rms(x, g):
    y = x * lax.rsqrt(jnp.mean(x * x, axis=-1, keepdims=True) + EPS)
    return y * g


def _params(sem):
    return pltpu.CompilerParams(dimension_semantics=sem, vmem_limit_bytes=VMEM_LIMIT)


def _const_spec(shape):
    zeros = (0,) * len(shape)
    return pl.BlockSpec(shape, lambda *_: zeros, pipeline_mode=pl.Buffered(1))


def _ada_kernel(cond_ref, w_ref, b_ref, o_ref):
    c = cond_ref[...]
    s = (c * jax.nn.sigmoid(c)).astype(BF16)
    o_ref[...] = _dot(s, w_ref[...].astype(BF16)) + b_ref[...]


def _ada(cond8, w_ada, b_ada):
    n = w_ada.shape[1]
    return pl.pallas_call(
        _ada_kernel,
        grid=(n // ADA_TILE_N,),
        in_specs=[
            pl.BlockSpec((8, D_MODEL), lambda j: (0, 0)),
            pl.BlockSpec((D_MODEL, ADA_TILE_N), lambda j: (0, j)),
            pl.BlockSpec((1, ADA_TILE_N), lambda j: (0, j)),
        ],
        out_specs=pl.BlockSpec((8, ADA_TILE_N), lambda j: (0, j)),
        out_shape=jax.ShapeDtypeStruct((8, n), F32),
        compiler_params=_params(("parallel",)),
        name="ada",
    )(cond8, w_ada, b_ada.reshape(1, n))


def _write_heads(src, extra, g_pad, rot, dst_ref):
    for h in range(A_HEADS):
        sl = slice(h * HEAD_PAD, (h + 1) * HEAD_PAD)
        xh = src[:, sl]
        if extra is not None:
            xh = xh + extra
        ss = jnp.sum(xh * xh, axis=-1, keepdims=True) * (1.0 / A_QK)
        r = lax.rsqrt(ss + EPS)
        if rot is None:
            y = xh * r * g_pad
        else:
            partner, cos_g, sin_g = rot
            ph = partner if partner.shape[1] == HEAD_PAD else partner[:, sl]
            y = (xh * cos_g + ph * sin_g) * r
        dst_ref[:, sl] = y.astype(dst_ref.dtype)


def _time_scan(x, op, identity, reverse):
    n = x.shape[0]
    row = lax.broadcasted_iota(jnp.int32, x.shape, 0)
    shift = 1
    while shift < n:
        if shift < SUBLANES:
            if reverse:
                moved = jnp.where(row < n - shift, pltpu.roll(x, n - shift, 0), identity)
            else:
                moved = jnp.where(row >= shift, pltpu.roll(x, shift, 0), identity)
        else:
            fill = jnp.full((shift, x.shape[1]), identity, x.dtype)
            moved = (jnp.concatenate([x[shift:], fill], axis=0) if reverse
                     else jnp.concatenate([fill, x[:n - shift]], axis=0))
        x = op(x, moved)
        shift *= 2
    return x


def _mod_row(mod_ref, row0, per_batch):
    if per_batch:
        return mod_ref[pl.ds(row0 + pl.program_id(0) * per_batch, 1), :]
    return mod_ref[row0:row0 + 1, :]


def _pre_kernel(has_rope, emit_cache, mod_row0, mod_per_batch, *refs):
    (x_ref, mod_ref, g1_ref, wmain_ref, wkt_ref, wlat_ref, gbias_ref, qlg_ref, kvg_ref, wq_ref, wkv_ref,
     qhg_ref, khg_ref) = refs[:13]
    pos = 13
    if has_rope:
        rope_ref = refs[pos]
        pos += 1
    if emit_cache:
        cast_in = refs[pos:pos + 3]
        pos += 3
    (mq_ref, mkt_ref, mv_ref, mo_ref, stats_ref, q_ref, k_ref, v_ref) = refs[pos:pos + 8]
    pos += 8

    x = x_ref[...]
    mod = _mod_row(mod_ref, mod_row0, mod_per_batch)
    sh1 = mod[:, 0:D_MODEL]
    sc1 = mod[:, D_MODEL:2 * D_MODEL]
    h = _rms(x, g1_ref[...]) * (1.0 + sc1) + sh1
    hb = h.astype(BF16)

    plat = _dot_nt(hb, wlat_ref[...])
    q_lat = plat[:, 0:Q_LORA]
    kv_lat = plat[:, Q_LORA:Q_LORA + KV_LORA]
    tail = plat[:, Q_LORA + KV_LORA:LAT_WIDTH]
    tail_f = pltpu.roll(tail, LANES - 2 * M_HEADS, 1)
    tail2 = pltpu.roll(tail, LANES - A_ROPE, 1)

    lane = lax.broadcasted_iota(jnp.int32, (1, LANES), 1)
    fwd = lane < M_HEADS
    gate_i = tail + gbias_ref[0:1, :]
    gate_f = tail_f + gbias_ref[1:2, :]
    log_f = jnp.minimum(gate_f, 0.0) - jnp.log1p(jnp.exp(-jnp.abs(gate_f)))
    def block_scan(v, op, identity):
        parts = [v[r:r + M_BLOCK] for r in range(0, v.shape[0], M_BLOCK)]
        return jnp.where(fwd, jnp.concatenate([_time_scan(p, op, identity, False) for p in parts], axis=0),
                         jnp.concatenate([_time_scan(p, op, identity, True) for p in parts], axis=0))

    b = block_scan(log_f, jnp.add, 0.0)
    a = gate_i - b
    amax = block_scan(a, jnp.maximum, -jnp.inf)
    stats_ref[0] = b.T[0:2 * M_HEADS, :]
    stats_ref[1] = a.T[0:2 * M_HEADS, :]
    stats_ref[2] = amax.T[0:2 * M_HEADS, :]

    krope_placed = jnp.where((lane >= A_NOPE) & (lane < A_QK), tail, 0.0)
    ckv = _rms(kv_lat, kvg_ref[...])
    qn = _rms(q_lat, qlg_ref[...])
    qf = _dot(qn.astype(BF16), wq_ref[...])
    kvf = _dot(ckv.astype(BF16), wkv_ref[...])
    v_ref[...] = kvf[:, QK_PAD:QK_PAD + A_WIDTH].astype(BF16)
    q_rot = k_rot = None
    qhg = qhg_ref[...] * (A_QK ** -0.5 * LOG2E)
    if has_rope:
        cos_t, sin_t = rope_ref[0], rope_ref[1]
        q_rot = (qf[:, QK_PAD:2 * QK_PAD], cos_t * qhg[0:1, :], sin_t * qhg[1:2, :])
        k_rot = (tail2, cos_t * khg_ref[0:1, :], sin_t * khg_ref[1:2, :])
    _write_heads(qf, None, qhg[0:1, :], q_rot, q_ref)
    _write_heads(kvf, krope_placed, khg_ref[0:1, :], k_rot, k_ref)

    if emit_cache:
        ckv_ref, krope_ref = refs[pos:pos + 2]
        ckv_ref[...] = ckv
        krope_ref[...] = tail[:, A_NOPE:A_QK]
        for src_ref, dst_ref in zip(cast_in, refs[pos + 2:pos + 5]):
            dst_ref[...] = src_ref[...].astype(BF16)

    pm = _dot_nt(hb, wmain_ref[...])
    mq_ref[...] = pm[:, 0:M_WIDTH].astype(BF16)
    mv_ref[...] = pm[:, M_WIDTH:2 * M_WIDTH].astype(BF16)
    mo_ref[...] = pm[:, 2 * M_WIDTH:3 * M_WIDTH]
    mkt_ref[...] = _dot_nt(wkt_ref[...], hb) * (M_HEAD_DIM ** -0.5)


def _pre(x, mod3, mod_row0, mod_per_batch, wts, rope_tab, emit_cache):
    shape = x.shape
    tm = TOKEN_TILE
    if shape[1] < tm:
        assert not mod_per_batch and tm % shape[1] == 0
        x = x.reshape(-1, tm, shape[2])
    B, T, _ = x.shape
    has_rope = rope_tab is not None
    tok = lambda w: pl.BlockSpec((None, tm, w), lambda b, i: (b, i, 0))
    in_specs = [
        tok(D_MODEL),
        _const_spec((8, 6 * D_MODEL)),
        _const_spec((1, D_MODEL)),
        _const_spec((3 * M_WIDTH, D_MODEL)),
        _const_spec((M_WIDTH, D_MODEL)),
        _const_spec((LAT_WIDTH, D_MODEL)),
        _const_spec((2, LANES)),
        _const_spec((1, Q_LORA)),
        _const_spec((1, KV_LORA)),
        _const_spec((Q_LORA, 2 * QK_PAD if has_rope else QK_PAD)),
        _const_spec((KV_LORA, QK_PAD + A_WIDTH)),
        _const_spec((2, HEAD_PAD)),
        _const_spec((2, HEAD_PAD)),
    ]
    assert tm % M_BLOCK == 0
    args = [x, mod3, wts["g1"], wts["w_main"], wts["w_kt"], wts["w_lat"], wts["gate_bias"], wts["q_lora_g"],
            wts["kv_lora_g"], wts["w_q_rot"] if has_rope else wts["w_q"], wts["w_kv"],
            wts["q_head_g"], wts["k_head_g"]]
    if has_rope:
        in_specs.append(pl.BlockSpec((2, tm, HEAD_PAD), lambda b, i: (0, i, 0)))
        args.append(rope_tab)
    out_specs = [tok(M_WIDTH),
                 pl.BlockSpec((None, M_WIDTH, tm), lambda b, i: (b, 0, i)),
                 tok(M_WIDTH), tok(M_WIDTH),
                 pl.BlockSpec((None, 3, 2 * M_HEADS, tm), lambda b, i: (b, 0, 0, i)),
                 tok(QK_PAD), tok(QK_PAD), tok(A_WIDTH)]
    out_shape = [
        jax.ShapeDtypeStruct((B, T, M_WIDTH), BF16),
        jax.ShapeDtypeStruct((B, M_WIDTH, T), F32),
        jax.ShapeDtypeStruct((B, T, M_WIDTH), BF16),
        jax.ShapeDtypeStruct((B, T, M_WIDTH), F32),
        jax.ShapeDtypeStruct((B, 3, 2 * M_HEADS, T), F32),
        jax.ShapeDtypeStruct((B, T, QK_PAD), BF16),
        jax.ShapeDtypeStruct((B, T, QK_PAD), BF16),
        jax.ShapeDtypeStruct((B, T, A_WIDTH), BF16),
    ]
    if emit_cache:
        out_specs += [tok(KV_LORA), tok(A_ROPE)]
        out_shape += [jax.ShapeDtypeStruct((B, T, KV_LORA), F32),
                      jax.ShapeDtypeStruct((B, T, A_ROPE), F32)]
        steps, per_b = B * (T // tm), T // tm
        for w in (wts["w_out_f32"], wts["w_up_f32"], wts["w_down_f32"]):
            rows = w.shape[0] // steps
            assert rows * steps == w.shape[0] and rows % (2 * SUBLANES) == 0
            spec = pl.BlockSpec((rows, w.shape[1]), lambda b, i: (b * per_b + i, 0))
            in_specs.append(spec)
            args.append(w)
            out_specs.append(spec)
            out_shape.append(jax.ShapeDtypeStruct(w.shape, BF16))
    outs = pl.pallas_call(
        functools.partial(_pre_kernel, has_rope, emit_cache, mod_row0, mod_per_batch),
        grid=(B, T // tm),
        in_specs=in_specs,
        out_specs=out_specs,
        out_shape=out_shape,
        compiler_params=_params(("parallel", "parallel")),
        name="pre_latent" if has_rope else "pre_context",
    )(*args)
    keep = (1, 4, 10, 11, 12)
    return [o if n in keep else o.reshape(shape[:2] + o.shape[2:]) for n, o in enumerate(outs)]


def _rows_to_lane_broadcast(rows, spread):
    x = jnp.concatenate(rows, axis=0)
    p1 = x.astype(BF16)
    r1 = x - p1.astype(F32)
    p2 = r1.astype(BF16)
    p3 = (r1 - p2.astype(F32)).astype(BF16)
    pad = jnp.zeros((spread.shape[0] - 3 * len(rows), x.shape[1]), BF16)
    return _dot_tn(jnp.concatenate([p1, p2, p3, pad], axis=0), spread)


def _mlstm_gate_rows(b_row, a_row, amax_row, forward, m):
    L = b_row.shape[1]
    last = slice(L - 1, L) if forward else slice(0, 1)
    total = b_row[:, last]
    g_row = jnp.maximum(m, amax_row)
    m_new = total + jnp.maximum(m, amax_row[:, last])
    w_key_row = jnp.exp2((a_row + (total - m_new)) * LOG2E)
    decay = jnp.exp(total + m - m_new)
    return g_row * LOG2E, (b_row + g_row) * LOG2E, a_row * LOG2E, w_key_row, decay, m_new


def _mlstm_block(s_raw, q, kt, v_aug, g2, mt2, a2_row, w_key_row, decay, allow, CN, m):
    w_intra = jnp.exp2(jnp.where(allow, a2_row - jnp.concatenate([g2, g2], axis=1), -jnp.inf))
    w_inter = jnp.exp2(m * LOG2E - g2)
    s = (s_raw * w_intra).astype(BF16)
    nd = _dot(s, v_aug) + jnp.concatenate([w_inter, w_inter], axis=1) * _dot(q, CN.astype(BF16))
    num, den = nd[:, 0:M_HEAD_DIM], nd[:, M_HEAD_DIM:2 * M_HEAD_DIM]
    h = num / jnp.maximum(jnp.abs(den), jnp.exp2(-mt2))
    CN_new = decay * CN + _dot((kt * w_key_row).astype(BF16), v_aug)
    return h, CN_new


def _mlstm_kernel(has_init, emit_state, n_blocks, heads, seqs, *refs):
    q_ref, kt_ref, v_ref, mo_ref, stats_ref, ng_ref, spread_ref = refs[:7]
    pos = 7
    if has_init:
        c0_ref, n0_ref, m0_ref = refs[pos:pos + 3]
        pos += 3
    hm_ref = refs[pos]
    pos += 1
    if emit_state:
        c_ref, n_ref, m_ref = refs[pos:pos + 3]
        pos += 3

    L, Dh = M_BLOCK, M_HEAD_DIM
    t_idx = lax.broadcasted_iota(jnp.int32, (L, L), 0)
    s_idx = lax.broadcasted_iota(jnp.int32, (L, L), 1)
    allow = (s_idx <= t_idx, s_idx >= t_idx)
    spread = spread_ref[...]
    ones = jnp.ones((L, Dh), BF16)

    def time_lanes(sq, c):
        start = (sq * n_blocks + c) * L
        return slice(start, start + L)

    def lane_broadcast_n(n_row):
        return jnp.broadcast_to(n_row, (Dh, Dh)).T

    def init_state(sq, j, d):
        if has_init:
            return (jnp.concatenate([c0_ref[sq, d, j], lane_broadcast_n(n0_ref[sq, d, j])], axis=1),
                    m0_ref[sq, j, d:d + 1, 0:1])
        return jnp.zeros((Dh, 2 * Dh), F32), jnp.zeros((1, 1), F32)

    def gate_rows(sq, j, c, d, m):
        head = j if heads == M_HEADS else pl.program_id(1) * heads + j
        r, lanes = pl.ds(d * M_HEADS + head, 1), time_lanes(sq, c)
        return _mlstm_gate_rows(stats_ref[0, r, lanes], stats_ref[1, r, lanes], stats_ref[2, r, lanes], d == 0, m)

    def blocks(sq, j, jobs, states):
        loaded, rows6, cols_in = {}, [], []
        for (c, d), (CN, m) in zip(jobs, states):
            rows6.append(gate_rows(sq, j, c, d, m))
            cols_in += [rows6[-1][0], rows6[-1][1]]
            if c not in loaded:
                rows, cols = slice(c * L, (c + 1) * L), slice(j * Dh, (j + 1) * Dh)
                q, kt = q_ref[sq, rows, cols], kt_ref[cols, time_lanes(sq, c)]
                v_aug = jnp.concatenate([v_ref[sq, rows, cols], ones], axis=1)
                loaded[c] = (_dot(q, kt.astype(BF16)), q, kt, v_aug)
        cols_out = _rows_to_lane_broadcast(cols_in, spread)
        hs, new_states = [], []
        for idx, ((c, d), (CN, m)) in enumerate(zip(jobs, states)):
            g2 = cols_out[:, (2 * idx) * LANES:(2 * idx + 1) * LANES]
            mt2 = cols_out[:, (2 * idx + 1) * LANES:(2 * idx + 2) * LANES]
            _, _, a2_row, w_key_row, decay, m_new = rows6[idx]
            h, CN_new = _mlstm_block(*loaded[c], g2, mt2, a2_row, w_key_row, decay, allow[d], CN, m)
            hs.append(h)
            new_states.append((CN_new, m_new))
        return hs, new_states

    def finish(sq, j, rows, hs):
        cols = slice(j * Dh, (j + 1) * Dh)
        hn = _rms(hs, ng_ref[j])
        hm_ref[sq, rows, cols] = (hn * jax.nn.sigmoid(mo_ref[sq, rows, cols])).astype(hm_ref.dtype)

    def emit(sq, j, d, state):
        CN, m = state
        c_ref[sq, d, j] = CN[:, 0:Dh]
        n_ref[sq, d, j] = CN[:, Dh:2 * Dh].T[0:1, :]
        m_ref[sq, j, d:d + 1, :] = jnp.broadcast_to(m, (1, LANES))

    if n_blocks > 1:
        hf_scr, hb_scr = refs[pos:pos + 2]

    for sq, j in [(sq, j) for sq in range(seqs) for j in range(heads)]:
        cols = slice(j * Dh, (j + 1) * Dh)
        states = [init_state(sq, j, 0), init_state(sq, j, 1)]
        if n_blocks == 1:
            (hf, hb), states = blocks(sq, j, [(0, 0), (0, 1)], states)
            finish(sq, j, slice(0, L), hf + hb)
        else:
            for step in range(n_blocks):
                cf, cb = step, n_blocks - 1 - step
                (hf, hb), states = blocks(sq, j, [(cf, 0), (cb, 1)], states)
                hf_scr[cf * L:(cf + 1) * L, cols] = hf
                hb_scr[cb * L:(cb + 1) * L, cols] = hb
            finish(sq, j, slice(None), hf_scr[:, cols] + hb_scr[:, cols])
        if emit_state:
            emit(sq, j, 0, states[0])
            emit(sq, j, 1, states[1])


def _mlstm(mq, mkt, mv, mo, stats, norm_g, init_state, emit_state, heads, seqs):
    B, T, _ = mq.shape
    H, Dh, L = M_HEADS, M_HEAD_DIM, M_BLOCK
    nb = T // L
    w = heads * Dh
    has_init = init_state is not None
    n_rows = 4
    spread = jnp.tile(jnp.repeat(jnp.eye(n_rows, dtype=BF16), LANES, axis=1), (3, 1))
    spread = jnp.pad(spread, ((0, 2 * SUBLANES - 3 * n_rows), (0, 0)))
    tok = pl.BlockSpec((seqs, T, w), lambda b, h: (b, 0, h))
    assert mkt.shape == (B // seqs, M_WIDTH, seqs * T) and stats.shape == (B // seqs, 3, 2 * H, seqs * T)
    in_specs = [tok, pl.BlockSpec((None, w, seqs * T), lambda b, h: (b, h, 0)), tok, tok,
                pl.BlockSpec((None, 3, 2 * H, seqs * T), lambda b, h: (b, 0, 0, 0)),
                pl.BlockSpec((heads, 1, Dh), lambda b, h: (h, 0, 0)),
                _const_spec((2 * SUBLANES, n_rows * LANES))]
    args = [mq, mkt, mv, mo, stats, norm_g, spread]
    state_specs = [pl.BlockSpec((seqs, 2, heads, Dh, Dh), lambda b, h: (b, 0, h, 0, 0)),
                   pl.BlockSpec((seqs, 2, heads, 1, Dh), lambda b, h: (b, 0, h, 0, 0)),
                   pl.BlockSpec((seqs, heads, 2, LANES), lambda b, h: (b, h, 0, 0))]
    if has_init:
        in_specs += state_specs
        args += list(init_state)
    out_specs = [tok]
    out_shape = [jax.ShapeDtypeStruct((B, T, M_WIDTH), BF16)]
    if emit_state:
        out_specs += state_specs
        out_shape += [jax.ShapeDtypeStruct((B, 2, H, Dh, Dh), F32),
                      jax.ShapeDtypeStruct((B, 2, H, 1, Dh), F32),
                      jax.ShapeDtypeStruct((B, H, 2, LANES), F32)]
    scratch = [] if nb == 1 else [pltpu.VMEM((T, w), F32), pltpu.VMEM((T, w), F32)]
    return pl.pallas_call(
        functools.partial(_mlstm_kernel, has_init, emit_state, nb, heads, seqs),
        grid=(B // seqs, H // heads),
        in_specs=in_specs,
        out_specs=out_specs,
        out_shape=out_shape,
        scratch_shapes=scratch,
        compiler_params=_params(("parallel", "parallel")),
        name="mlstm_latent" if has_init else "mlstm_context",
    )(*args)


def _attn_kernel(has_ctx, seqs, *refs):
    if has_ctx:
        q_ref, k_ref, v_ref, ckv_ref, krp_ref, wkv_ref, khg_ref, o_ref, kc_ref, vc_ref = refs

        @pl.when(pl.program_id(1) == 0)
        def _():
            kvf = _dot(ckv_ref[0].astype(BF16), wkv_ref[...])
            vc_ref[...] = kvf[:, QK_PAD:QK_PAD + A_WIDTH].astype(BF16)
            _write_heads(kvf, krp_ref[0], khg_ref[0:1, :], None, kc_ref)
    else:
        q_ref, k_ref, v_ref, o_ref = refs
    lane = lax.broadcasted_iota(jnp.int32, (1, LANES), 1)
    ones = lambda n: jnp.ones((n, LANES), BF16)
    tq = q_ref.shape[1]
    sub = min(Q_SUBTILE, tq)
    for sq, r0, pair in [(sq, r0, pair) for sq in range(seqs) for r0 in range(0, tq, sub)
                         for pair in range(A_HEADS // 2)]:
        rows = slice(r0, r0 + sub)
        vsl = slice(pair * LANES, (pair + 1) * LANES)
        v_aug = jnp.concatenate([v_ref[sq, :, vsl], ones(v_ref.shape[1])], axis=1)
        if has_ctx:
            vc_aug = jnp.concatenate([vc_ref[:, vsl], ones(vc_ref.shape[0])], axis=1)
        outs = []
        for e in range(2):
            hsl = slice((2 * pair + e) * HEAD_PAD, (2 * pair + e + 1) * HEAD_PAD)
            qh = q_ref[sq, rows, hsl]
            s = _dot_nt(qh, k_ref[sq, :, hsl])
            mx = jnp.max(s, axis=1, keepdims=True)
            if has_ctx:
                sc = _dot_nt(qh, kc_ref[:, hsl])
                mx = jnp.maximum(mx, jnp.max(sc, axis=1, keepdims=True))
            od = _dot(jnp.exp2(s - mx).astype(BF16), v_aug)
            if has_ctx:
                od = od + _dot(jnp.exp2(sc - mx).astype(BF16), vc_aug)
            outs.append(od[:, 0:LANES] / od[:, LANES:2 * LANES])
        o_ref[sq, rows, vsl] = jnp.where(lane < A_VDIM, outs[0], outs[1]).astype(o_ref.dtype)


def _attn(q, k, v, ctx, seqs):
    B, T, _ = q.shape
    tq = min(Q_TILE, T)
    has_ctx = ctx is not None
    full = lambda n, w: pl.BlockSpec((seqs, n, w), lambda b, i: (b, 0, 0))
    in_specs = [pl.BlockSpec((seqs, tq, QK_PAD), lambda b, i: (b, i, 0)), full(T, QK_PAD), full(T, A_WIDTH)]
    args = [q, k, v]
    scratch = []
    if has_ctx:
        assert seqs == 1
        P = ctx[0].shape[1]
        in_specs += [full(P, KV_LORA), full(P, HEAD_PAD), _const_spec((KV_LORA, QK_PAD + A_WIDTH)),
                     _const_spec((2, HEAD_PAD))]
        args += list(ctx)
        scratch = [pltpu.VMEM((P, QK_PAD), BF16), pltpu.VMEM((P, A_WIDTH), BF16)]
    return pl.pallas_call(
        functools.partial(_attn_kernel, has_ctx, seqs),
        grid=(B // seqs, T // tq),
        in_specs=in_specs,
        out_specs=pl.BlockSpec((seqs, tq, A_WIDTH), lambda b, i: (b, i, 0)),
        out_shape=jax.ShapeDtypeStruct((B, T, A_WIDTH), BF16),
        scratch_shapes=scratch,
        compiler_params=_params(("parallel", "arbitrary" if has_ctx else "parallel")),
        name="attn_latent" if has_ctx else "attn_context",
    )(*args)


def _post_kernel(mod_row0, mod_per_batch, x_ref, hm_ref, ha_ref, mod_ref, g2_ref, wout_ref, wup_ref, wdown_ref,
                 y_ref):
    mod = _mod_row(mod_ref, mod_row0, mod_per_batch)
    gate1 = mod[:, 2 * D_MODEL:3 * D_MODEL]
    sh2 = mod[:, 3 * D_MODEL:4 * D_MODEL]
    sc2 = mod[:, 4 * D_MODEL:5 * D_MODEL]
    gate2 = mod[:, 5 * D_MODEL:6 * D_MODEL]
    mix = jnp.concatenate([hm_ref[...], ha_ref[...]], axis=-1)
    x1 = x_ref[...] + gate1 * _dot(mix, wout_ref[...])
    h2 = (_rms(x1, g2_ref[...]) * (1.0 + sc2) + sh2).astype(BF16)
    acc = jnp.zeros_like(x1)
    for c in range(D_FF // FF_TILE):
        sl = slice(c * FF_TILE, (c + 1) * FF_TILE)
        u = jnp.maximum(_dot(h2, wup_ref[:, sl]), 0.0)
        acc = acc + _dot((u * u).astype(BF16), wdown_ref[sl, :])
    y_ref[...] = x1 + gate2 * acc


def _post(x, hm, ha, mod3, mod_row0, mod_per_batch, wts):
    shape = x.shape
    if not mod_per_batch:
        x, hm, ha = (a.reshape(1, -1, a.shape[-1]) for a in (x, hm, ha))
    B, T, _ = x.shape
    tm = POST_TILE
    tok = lambda w: pl.BlockSpec((None, tm, w), lambda b, i: (b, i, 0))
    return _post_call(x, hm, ha, mod3, mod_row0, mod_per_batch, wts, B, T, tm, tok).reshape(shape)


def _post_call(x, hm, ha, mod3, mod_row0, mod_per_batch, wts, B, T, tm, tok):
    return pl.pallas_call(
        functools.partial(_post_kernel, mod_row0, mod_per_batch),
        grid=(B, T // tm),
        in_specs=[tok(D_MODEL), tok(M_WIDTH), tok(A_WIDTH),
                  _const_spec((8, 6 * D_MODEL)),
                  _const_spec((1, D_MODEL)),
                  _const_spec((M_WIDTH + A_WIDTH, D_MODEL)),
                  _const_spec((D_MODEL, D_FF)),
                  _const_spec((D_FF, D_MODEL))],
        out_specs=tok(D_MODEL),
        out_shape=jax.ShapeDtypeStruct((B, T, D_MODEL), F32),
        compiler_params=_params(("parallel", "parallel")),
        name="post",
    )(x, hm, ha, mod3, wts["g2"], wts["w_out"], wts["w_up"], wts["w_down"])


def _prepare_weights(norm1_g, norm2_g, w_in, mlstm_gate_b, q_lora_g, kv_lora_g, w_q_up, w_kv_up,
                     q_head_g, k_head_g, w_out, w_mlp_up, w_mlp_down):
    o_g = 4 * M_WIDTH
    o_q = o_g + N_GATES
    o_kv = o_q + Q_LORA
    o_kr = o_kv + KV_LORA
    half = A_ROPE // 2
    n_dh = 2 * M_HEADS
    wt = w_in.T.astype(BF16)
    w_gate = wt[o_g:o_q].reshape(2, 2, M_HEADS, D_MODEL)
    bias = mlstm_gate_b.reshape(2, 2, M_HEADS)

    def rot_partner(a):
        z = jnp.zeros(a.shape[:-1] + (A_NOPE,), a.dtype)
        return jnp.concatenate([z, a[..., A_NOPE + half:A_QK], a[..., A_NOPE:A_NOPE + half]], axis=-1)

    pad_tile = lambda a: jnp.pad(a, [(0, 0)] * (a.ndim - 1) + [(0, HEAD_PAD - A_QK)])
    w_kr = wt[o_kr:o_kr + A_ROPE]
    w_kr_partner = jnp.concatenate([w_kr[half:], w_kr[:half]], axis=0)
    w_lat = jnp.concatenate([wt[o_q:o_kr], w_gate[:, 0].reshape(n_dh, D_MODEL), w_gate[:, 1].reshape(n_dh, D_MODEL),
                             jnp.zeros((A_NOPE - 2 * n_dh, D_MODEL), BF16), w_kr, w_kr_partner], axis=0)
    gate_bias = jnp.pad(jnp.stack([bias[:, 0, :].reshape(n_dh), bias[:, 1, :].reshape(n_dh)], axis=0),
                        ((0, 0), (0, LANES - n_dh)))
    w_q3 = w_q_up.reshape(Q_LORA, A_HEADS, A_QK)
    w_q = pad_tile(w_q3).reshape(Q_LORA, QK_PAD)
    w_q_partner = pad_tile(rot_partner(w_q3)).reshape(Q_LORA, QK_PAD)
    w_kv3 = w_kv_up.reshape(KV_LORA, A_HEADS, A_NOPE + A_VDIM)
    w_k = jnp.pad(w_kv3[:, :, :A_NOPE], ((0, 0), (0, 0), (0, HEAD_PAD - A_NOPE)))
    w_v = w_kv3[:, :, A_NOPE:]
    w_kv = jnp.concatenate([w_k.reshape(KV_LORA, QK_PAD), w_v.reshape(KV_LORA, A_WIDTH)], axis=1)
    pad_head = lambda g: jnp.stack([pad_tile(g), pad_tile(rot_partner(g))], axis=0)
    return {
        "g1": norm1_g.reshape(1, D_MODEL),
        "g2": norm2_g.reshape(1, D_MODEL),
        "w_main": jnp.concatenate([wt[0:M_WIDTH], wt[2 * M_WIDTH:o_g]], axis=0),
        "w_kt": wt[M_WIDTH:2 * M_WIDTH],
        "w_lat": w_lat,
        "gate_bias": gate_bias,
        "q_lora_g": q_lora_g.reshape(1, Q_LORA),
        "kv_lora_g": kv_lora_g.reshape(1, KV_LORA),
        "w_q": w_q.astype(BF16),
        "w_q_rot": jnp.concatenate([w_q, w_q_partner], axis=1).astype(BF16),
        "w_kv": w_kv.astype(BF16),
        "q_head_g": pad_head(q_head_g),
        "k_head_g": pad_head(k_head_g),
        "w_out_f32": w_out,
        "w_up_f32": w_mlp_up,
        "w_down_f32": w_mlp_down,
    }


def _rope_tables(T):
    rows = T // GRID_W
    row = np.repeat(np.arange(rows, dtype=np.float32), GRID_W)
    col = np.tile(np.arange(GRID_W, dtype=np.float32), rows)
    half = A_ROPE // 2
    inv = (np.float32(ROPE_BASE) ** (-np.arange(0, half, 2, dtype=np.float32) / np.float32(half))).astype(np.float32)
    ang = np.concatenate([row[:, None] * inv, col[:, None] * inv], axis=-1)
    cos, sin = np.cos(ang), np.sin(ang)
    ones = np.ones((T, A_NOPE), np.float32)
    z = lambda w: np.zeros((T, w), np.float32)
    tail = LANES - A_QK
    cos_t = np.concatenate([ones, cos, cos, z(tail)], axis=1)
    sin_t = np.concatenate([z(A_NOPE), -sin, sin, z(tail)], axis=1)
    return jnp.asarray(np.stack([cos_t, sin_t], axis=0).astype(np.float32))


def _layer_pass(x, mod3, mod_row0, mod_per_batch, wts, norm_g, rope_tab, init_state, ctx_kv, is_context):
    pre = _pre(x, mod3, mod_row0, mod_per_batch, wts, rope_tab, emit_cache=is_context)
    mq, mkt, mv, mo, stats, q, k, v = pre[:8]
    if is_context:
        wts = dict(wts, w_out=pre[10], w_up=pre[11], w_down=pre[12])
    ml = _mlstm(mq, mkt, mv, mo, stats, norm_g, init_state, emit_state=is_context,
                heads=M_HEADS if is_context else MLSTM_LATENT_HEADS,
                seqs=MLSTM_CONTEXT_SEQS if is_context else 1)
    ha = _attn(q, k, v, ctx_kv, seqs=ATTN_CONTEXT_SEQS if is_context else 1)
    y = _post(x, ml[0], ha, mod3, mod_row0, mod_per_batch, wts)
    return y, pre[8:10], ml[1:], wts


def kernel(x_prompt, x_sample, cache_mla_ckv, cache_mla_krope, state_mlstm_C, state_mlstm_n, state_mlstm_m,
           c, c_ctx, norm1_g, norm2_g, w_ada, b_ada, w_in, mlstm_gate_b, mlstm_norm_g,
           q_lora_g, kv_lora_g, w_q_up, w_kv_up, q_head_g, k_head_g, w_out, w_mlp_up, w_mlp_down):
    depth = w_in.shape[0]
    Bd = x_sample.shape[0]
    cond8 = jnp.concatenate([c_ctx[None, :], c, jnp.zeros((8 - 1 - Bd, D_MODEL), F32)], axis=0)
    rope_tab = _rope_tables(x_sample.shape[1])

    y, z = x_prompt, x_sample
    ckvs, kropes, Cs, ns, ms = [], [], [], [], []
    for l in range(depth):
        wts = _prepare_weights(norm1_g[l], norm2_g[l], w_in[l], mlstm_gate_b[l], q_lora_g[l], kv_lora_g[l],
                               w_q_up[l], w_kv_up[l], q_head_g[l], k_head_g[l], w_out[l], w_mlp_up[l],
                               w_mlp_down[l])
        norm_g = mlstm_norm_g[l].reshape(M_HEADS, 1, M_HEAD_DIM)
        mod3 = _ada(cond8, w_ada[l], b_ada[l])
        y, (ckv, krope), (C_new, n_new, m_new), wts = _layer_pass(
            y, mod3, 0, 0, wts, norm_g, None, None, None, True)
        ckvs.append(ckv)
        kropes.append(krope)
        Cs.append(C_new)
        ns.append(n_new[:, :, :, 0, :])
        ms.append(m_new[:, :, :, 0].transpose(0, 2, 1))

        init_state = (state_mlstm_C[:, l],
                      state_mlstm_n[:, l][:, :, :, None, :],
                      jnp.broadcast_to(state_mlstm_m[:, l].transpose(0, 2, 1)[..., None],
                                       (Bd, M_HEADS, 2, LANES)))
        krope_placed = jnp.pad(cache_mla_krope[:, l], ((0, 0), (0, 0), (A_NOPE, LANES - A_QK)))
        ctx = (cache_mla_ckv[:, l], krope_placed, wts["w_kv"], wts["k_head_g"])
        z, _, _, _ = _layer_pass(z, mod3, 1, 1, wts, norm_g, rope_tab, init_state, ctx, False)

    return (y, z, jnp.stack(ckvs, axis=1), jnp.stack(kropes, axis=1), jnp.stack(Cs, axis=1),
            jnp.stack(ns, axis=1), jnp.stack(ms, axis=1))
```

```python
import functools

import jax
import jax.numpy as jnp
import numpy as np
from jax import lax
from jax.experimental import pallas as pl
from jax.experimental.pallas import tpu as pltpu

F32 = jnp.float32
BF16 = jnp.bfloat16

D_MODEL = 1024
GRID_W = 64
M_HEADS = 4
M_HEAD_DIM = 128
M_WIDTH = M_HEADS * M_HEAD_DIM
M_BLOCK = 256
A_HEADS = 8
A_NOPE = 64
A_ROPE = 32
A_QK = A_NOPE + A_ROPE
A_VDIM = 64
A_WIDTH = A_HEADS * A_VDIM
Q_LORA = 384
KV_LORA = 256
ROPE_BASE = 10000.0
D_FF = 4 * D_MODEL
EPS = 1e-6

LANES = 128
SUBLANES = 8
LOG2E = 1.4426950408889634
HEAD_PAD = LANES
QK_PAD = A_HEADS * HEAD_PAD
N_GATES = 4 * M_HEADS
LAT_WIDTH = Q_LORA + KV_LORA + LANES
VMEM_LIMIT = 56 * 1024 * 1024

TOKEN_TILE = 512
POST_TILE = 512
MLSTM_CONTEXT_SEQS = 2
ATTN_CONTEXT_SEQS = 4
Q_TILE = 1024
Q_SUBTILE = 256
ADA_TILE_N = 768
FF_TILE = 1024


def _dot(a, b):
    return jnp.dot(a, b, preferred_element_type=F32)


def _dot_nt(a, b):
    return lax.dot_general(a, b, (((1,), (1,)), ((), ())), preferred_element_type=F32)


def _dot_tn(a, b):
    return lax.dot_general(a, b, (((0,), (0,)), ((), ())), preferred_element_type=F32)


def _rms(x, g):
    y = x * lax.rsqrt(jnp.mean(x * x, axis=-1, keepdims=True) + EPS)
    return y * g


def _params(sem):
    return pltpu.CompilerParams(dimension_semantics=sem, vmem_limit_bytes=VMEM_LIMIT)


def _const_spec(shape):
    zeros = (0,) * len(shape)
    return pl.BlockSpec(shape, lambda *_: zeros, pipeline_mode=pl.Buffered(1))


def _ada_kernel(cond_ref, w_ref, b_ref, o_ref):
    c = cond_ref[...]
    s = (c * jax.nn.sigmoid(c)).astype(BF16)
    o_ref[...] = _dot(s, w_ref[...].astype(BF16)) + b_ref[...]


def _ada(cond8, w_ada, b_ada):
    n = w_ada.shape[1]
    return pl.pallas_call(
        _ada_kernel,
        grid=(n // ADA_TILE_N,),
        in_specs=[
            pl.BlockSpec((8, D_MODEL), lambda j: (0, 0)),
            pl.BlockSpec((D_MODEL, ADA_TILE_N), lambda j: (0, j)),
            pl.BlockSpec((1, ADA_TILE_N), lambda j: (0, j)),
        ],
        out_specs=pl.BlockSpec((8, ADA_TILE_N), lambda j: (0, j)),
        out_shape=jax.ShapeDtypeStruct((8, n), F32),
        compiler_params=_params(("parallel",)),
        name="ada",
    )(cond8, w_ada, b_ada.reshape(1, n))


def _write_heads(src, extra, g_pad, rot, dst_ref):
    for h in range(A_HEADS):
        sl = slice(h * HEAD_PAD, (h + 1) * HEAD_PAD)
        xh = src[:, sl]
        if extra is not None:
            xh = xh + extra
        ss = jnp.sum(xh * xh, axis=-1, keepdims=True) * (1.0 / A_QK)
        r = lax.rsqrt(ss + EPS)
        if rot is None:
            y = xh * r * g_pad
        else:
            partner, cos_g, sin_g = rot
            ph = partner if partner.shape[1] == HEAD_PAD else partner[:, sl]
            y = (xh * cos_g + ph * sin_g) * r
        dst_ref[:, sl] = y.astype(dst_ref.dtype)


def _time_scan(x, op, identity, reverse):
    n = x.shape[0]
    row = lax.broadcasted_iota(jnp.int32, x.shape, 0)
    shift = 1
    while shift < n:
        if shift < SUBLANES:
            if reverse:
                moved = jnp.where(row < n - shift, pltpu.roll(x, n - shift, 0), identity)
            else:
                moved = jnp.where(row >= shift, pltpu.roll(x, shift, 0), identity)
        else:
            fill = jnp.full((shift, x.shape[1]), identity, x.dtype)
            moved = (jnp.concatenate([x[shift:], fill], axis=0) if reverse
                     else jnp.concatenate([fill, x[:n - shift]], axis=0))
        x = op(x, moved)
        shift *= 2
    return x


def _mod_row(mod_ref, row0, per_batch):
    if per_batch:
        return mod_ref[pl.ds(row0 + pl.program_id(0) * per_batch, 1), :]
    return mod_ref[row0:row0 + 1, :]


def _pre_kernel(has_rope, emit_cache, mod_row0, mod_per_batch, *refs):
    (x_ref, mod_ref, g1_ref, wmain_ref, wkt_ref, wlat_ref, gbias_ref, qlg_ref, kvg_ref, wq_ref, wkv_ref,
     qhg_ref, khg_ref) = refs[:13]
    pos = 13
    if has_rope:
        rope_ref = refs[pos]
        pos += 1
    if emit_cache:
        cast_in = refs[pos:pos + 3]
        pos += 3
    (mq_ref, mkt_ref, mv_ref, mo_ref, stats_ref, q_ref, k_ref, v_ref) = refs[pos:pos + 8]
    pos += 8

    x = x_ref[...]
    mod = _mod_row(mod_ref, mod_row0, mod_per_batch)
    sh1 = mod[:, 0:D_MODEL]
    sc1 = mod[:, D_MODEL:2 * D_MODEL]
    h = _rms(x, g1_ref[...]) * (1.0 + sc1) + sh1
    hb = h.astype(BF16)

    plat = _dot_nt(hb, wlat_ref[...])
    q_lat = plat[:, 0:Q_LORA]
    kv_lat = plat[:, Q_LORA:Q_LORA + KV_LORA]
    tail = plat[:, Q_LORA + KV_LORA:LAT_WIDTH]
    tail_f = pltpu.roll(tail, LANES - 2 * M_HEADS, 1)
    tail2 = pltpu.roll(tail, LANES - A_ROPE, 1)

    lane = lax.broadcasted_iota(jnp.int32, (1, LANES), 1)
    fwd = lane < M_HEADS
    gate_i = tail + gbias_ref[0:1, :]
    gate_f = tail_f + gbias_ref[1:2, :]
    log_f = jnp.minimum(gate_f, 0.0) - jnp.log1p(jnp.exp(-jnp.abs(gate_f)))
    def block_scan(v, op, identity):
        parts = [v[r:r + M_BLOCK] for r in range(0, v.shape[0], M_BLOCK)]
        return jnp.where(fwd, jnp.concatenate([_time_scan(p, op, identity, False) for p in parts], axis=0),
                         jnp.concatenate([_time_scan(p, op, identity, True) for p in parts], axis=0))

    b = block_scan(log_f, jnp.add, 0.0)
    a = gate_i - b
    amax = block_scan(a, jnp.maximum, -jnp.inf)
    stats_ref[0] = b.T[0:2 * M_HEADS, :]
    stats_ref[1] = a.T[0:2 * M_HEADS, :]
    stats_ref[2] = amax.T[0:2 * M_HEADS, :]

    krope_placed = jnp.where((lane >= A_NOPE) & (lane < A_QK), tail, 0.0)
    ckv = _rms(kv_lat, kvg_ref[...])
    qn = _rms(q_lat, qlg_ref[...])
    qf = _dot(qn.astype(BF16), wq_ref[...])
    kvf = _dot(ckv.astype(BF16), wkv_ref[...])
    v_ref[...] = kvf[:, QK_PAD:QK_PAD + A_WIDTH].astype(BF16)
    q_rot = k_rot = None
    qhg = qhg_ref[...] * (A_QK ** -0.5 * LOG2E)
    if has_rope:
        cos_t, sin_t = rope_ref[0], rope_ref[1]
        q_rot = (qf[:, QK_PAD:2 * QK_PAD], cos_t * qhg[0:1, :], sin_t * qhg[1:2, :])
        k_rot = (tail2, cos_t * khg_ref[0:1, :], sin_t * khg_ref[1:2, :])
    _write_heads(qf, None, qhg[0:1, :], q_rot, q_ref)
    _write_heads(kvf, krope_placed, khg_ref[0:1, :], k_rot, k_ref)

    if emit_cache:
        ckv_ref, krope_ref = refs[pos:pos + 2]
        ckv_ref[...] = ckv
        krope_ref[...] = tail[:, A_NOPE:A_QK]
        for src_ref, dst_ref in zip(cast_in, refs[pos + 2:pos + 5]):
            dst_ref[...] = src_ref[...].astype(BF16)

    pm = _dot_nt(hb, wmain_ref[...])
    mq_ref[...] = pm[:, 0:M_WIDTH].astype(BF16)
    mv_ref[...] = pm[:, M_WIDTH:2 * M_WIDTH].astype(BF16)
    mo_ref[...] = pm[:, 2 * M_WIDTH:3 * M_WIDTH]
    mkt_ref[...] = _dot_nt(wkt_ref[...], hb) * (M_HEAD_DIM ** -0.5)


def _pre(x, mod3, mod_row0, mod_per_batch, wts, rope_tab, emit_cache):
    shape = x.shape
    tm = TOKEN_TILE
    if shape[1] < tm:
        assert not mod_per_batch and tm % shape[1] == 0
        x = x.reshape(-1, tm, shape[2])
    B, T, _ = x.shape
    has_rope = rope_tab is not None
    tok = lambda w: pl.BlockSpec((None, tm, w), lambda b, i: (b, i, 0))
    in_specs = [
        tok(D_MODEL),
        _const_spec((8, 6 * D_MODEL)),
        _const_spec((1, D_MODEL)),
        _const_spec((3 * M_WIDTH, D_MODEL)),
        _const_spec((M_WIDTH, D_MODEL)),
        _const_spec((LAT_WIDTH, D_MODEL)),
        _const_spec((2, LANES)),
        _const_spec((1, Q_LORA)),
        _const_spec((1, KV_LORA)),
        _const_spec((Q_LORA, 2 * QK_PAD if has_rope else QK_PAD)),
        _const_spec((KV_LORA, QK_PAD + A_WIDTH)),
        _const_spec((2, HEAD_PAD)),
        _const_spec((2, HEAD_PAD)),
    ]
    assert tm % M_BLOCK == 0
    args = [x, mod3, wts["g1"], wts["w_main"], wts["w_kt"], wts["w_lat"], wts["gate_bias"], wts["q_lora_g"],
            wts["kv_lora_g"], wts["w_q_rot"] if has_rope else wts["w_q"], wts["w_kv"],
            wts["q_head_g"], wts["k_head_g"]]
    if has_rope:
        in_specs.append(pl.BlockSpec((2, tm, HEAD_PAD), lambda b, i: (0, i, 0)))
        args.append(rope_tab)
    out_specs = [tok(M_WIDTH),
                 pl.BlockSpec((None, M_WIDTH, tm), lambda b, i: (b, 0, i)),
                 tok(M_WIDTH), tok(M_WIDTH),
                 pl.BlockSpec((None, 3, 2 * M_HEADS, tm), lambda b, i: (b, 0, 0, i)),
                 tok(QK_PAD), tok(QK_PAD), tok(A_WIDTH)]
    out_shape = [
        jax.ShapeDtypeStruct((B, T, M_WIDTH), BF16),
        jax.ShapeDtypeStruct((B, M_WIDTH, T), F32),
        jax.ShapeDtypeStruct((B, T, M_WIDTH), BF16),
        jax.ShapeDtypeStruct((B, T, M_WIDTH), F32),
        jax.ShapeDtypeStruct((B, 3, 2 * M_HEADS, T), F32),
        jax.ShapeDtypeStruct((B, T, QK_PAD), BF16),
        jax.ShapeDtypeStruct((B, T, QK_PAD), BF16),
        jax.ShapeDtypeStruct((B, T, A_WIDTH), BF16),
    ]
    if emit_cache:
        out_specs += [tok(KV_LORA), tok(A_ROPE)]
        out_shape += [jax.ShapeDtypeStruct((B, T, KV_LORA), F32),
                      jax.ShapeDtypeStruct((B, T, A_ROPE), F32)]
        steps, per_b = B * (T // tm), T // tm
        for w in (wts["w_out_f32"], wts["w_up_f32"], wts["w_down_f32"]):
            rows = w.shape[0] // steps
            assert rows * steps == w.shape[0] and rows % (2 * SUBLANES) == 0
            spec = pl.BlockSpec((rows, w.shape[1]), lambda b, i: (b * per_b + i, 0))
            in_specs.append(spec)
            args.append(w)
            out_specs.append(spec)
            out_shape.append(jax.ShapeDtypeStruct(w.shape, BF16))
    outs = pl.pallas_call(
        functools.partial(_pre_kernel, has_rope, emit_cache, mod_row0, mod_per_batch),
        grid=(B, T // tm),
        in_specs=in_specs,
        out_specs=out_specs,
        out_shape=out_shape,
        compiler_params=_params(("parallel", "parallel")),
        name="pre_latent" if has_rope else "pre_context",
    )(*args)
    keep = (1, 4, 10, 11, 12)
    return [o if n in keep else o.reshape(shape[:2] + o.shape[2:]) for n, o in enumerate(outs)]


def _rows_to_lane_broadcast(rows, spread):
    x = jnp.concatenate(rows, axis=0)
    p1 = x.astype(BF16)
    r1 = x - p1.astype(F32)
    p2 = r1.astype(BF16)
    p3 = (r1 - p2.astype(F32)).astype(BF16)
    pad = jnp.zeros((spread.shape[0] - 3 * len(rows), x.shape[1]), BF16)
    return _dot_tn(jnp.concatenate([p1, p2, p3, pad], axis=0), spread)


def _mlstm_gate_rows(b_row, a_row, amax_row, forward, m):
    L = b_row.shape[1]
    last = slice(L - 1, L) if forward else slice(0, 1)
    total = b_row[:, last]
    g_row = jnp.maximum(m, amax_row)
    m_new = total + jnp.maximum(m, amax_row[:, last])
    w_key_row = jnp.exp2((a_row + (total - m_new)) * LOG2E)
    decay = jnp.exp(total + m - m_new)
    return g_row * LOG2E, (b_row + g_row) * LOG2E, a_row * LOG2E, w_key_row, decay, m_new


def _mlstm_block(s_raw, q, kt, v_aug, g2, mt2, a2_row, w_key_row, decay, allow, CN, m):
    w_intra = jnp.exp2(jnp.where(allow, a2_row - jnp.concatenate([g2, g2], axis=1), -jnp.inf))
    w_inter = jnp.exp2(m * LOG2E - g2)
    s = (s_raw * w_intra).astype(BF16)
    nd = _dot(s, v_aug) + jnp.concatenate([w_inter, w_inter], axis=1) * _dot(q, CN.astype(BF16))
    num, den = nd[:, 0:M_HEAD_DIM], nd[:, M_HEAD_DIM:2 * M_HEAD_DIM]
    h = num / jnp.maximum(jnp.abs(den), jnp.exp2(-mt2))
    CN_new = decay * CN + _dot((kt * w_key_row).astype(BF16), v_aug)
    return h, CN_new


def _mlstm_kernel(has_init, emit_state, n_blocks, heads, seqs, *refs):
    q_ref, kt_ref, v_ref, mo_ref, stats_ref, ng_ref, spread_ref = refs[:7]
    pos = 7
    if has_init:
        c0_ref, n0_ref, m0_ref = refs[pos:pos + 3]
        pos += 3
    hm_ref = refs[pos]
    pos += 1
    if emit_state:
        c_ref, n_ref, m_ref = refs[pos:pos + 3]
        pos += 3

    L, Dh = M_BLOCK, M_HEAD_DIM
    t_idx = lax.broadcasted_iota(jnp.int32, (L, L), 0)
    s_idx = lax.broadcasted_iota(jnp.int32, (L, L), 1)
    allow = (s_idx <= t_idx, s_idx >= t_idx)
    spread = spread_ref[...]
    ones = jnp.ones((L, Dh), BF16)

    def time_lanes(sq, c):
        start = (sq * n_blocks + c) * L
        return slice(start, start + L)

    def lane_broadcast_n(n_row):
        return jnp.broadcast_to(n_row, (Dh, Dh)).T

    def init_state(sq, j, d):
        if has_init:
            return (jnp.concatenate([c0_ref[sq, d, j], lane_broadcast_n(n0_ref[sq, d, j])], axis=1),
                    m0_ref[sq, j, d:d + 1, 0:1])
        return jnp.zeros((Dh, 2 * Dh), F32), jnp.zeros((1, 1), F32)

    def gate_rows(sq, j, c, d, m):
        head = j if heads == M_HEADS else pl.program_id(1) * heads + j
        r, lanes = pl.ds(d * M_HEADS + head, 1), time_lanes(sq, c)
        return _mlstm_gate_rows(stats_ref[0, r, lanes], stats_ref[1, r, lanes], stats_ref[2, r, lanes], d == 0, m)

    def blocks(sq, j, jobs, states):
        loaded, rows6, cols_in = {}, [], []
        for (c, d), (CN, m) in zip(jobs, states):
            rows6.append(gate_rows(sq, j, c, d, m))
            cols_in += [rows6[-1][0], rows6[-1][1]]
            if c not in loaded:
                rows, cols = slice(c * L, (c + 1) * L), slice(j * Dh, (j + 1) * Dh)
                q, kt = q_ref[sq, rows, cols], kt_ref[cols, time_lanes(sq, c)]
                v_aug = jnp.concatenate([v_ref[sq, rows, cols], ones], axis=1)
                loaded[c] = (_dot(q, kt.astype(BF16)), q, kt, v_aug)
        cols_out = _rows_to_lane_broadcast(cols_in, spread)
        hs, new_states = [], []
        for idx, ((c, d), (CN, m)) in enumerate(zip(jobs, states)):
            g2 = cols_out[:, (2 * idx) * LANES:(2 * idx + 1) * LANES]
            mt2 = cols_out[:, (2 * idx + 1) * LANES:(2 * idx + 2) * LANES]
            _, _, a2_row, w_key_row, decay, m_new = rows6[idx]
            h, CN_new = _mlstm_block(*loaded[c], g2, mt2, a2_row, w_key_row, decay, allow[d], CN, m)
            hs.append(h)
            new_states.append((CN_new, m_new))
        return hs, new_states

    def finish(sq, j, rows, hs):
        cols = slice(j * Dh, (j + 1) * Dh)
        hn = _rms(hs, ng_ref[j])
        hm_ref[sq, rows, cols] = (hn * jax.nn.sigmoid(mo_ref[sq, rows, cols])).astype(hm_ref.dtype)

    def emit(sq, j, d, state):
        CN, m = state
        c_ref[sq, d, j] = CN[:, 0:Dh]
        n_ref[sq, d, j] = CN[:, Dh:2 * Dh].T[0:1, :]
        m_ref[sq, j, d:d + 1, :] = jnp.broadcast_to(m, (1, LANES))

    if n_blocks > 1:
        hf_scr, hb_scr = refs[pos:pos + 2]

    for sq, j in [(sq, j) for sq in range(seqs) for j in range(heads)]:
        cols = slice(j * Dh, (j + 1) * Dh)
        states = [init_state(sq, j, 0), init_state(sq, j, 1)]
        if n_blocks == 1:
            (hf, hb), states = blocks(sq, j, [(0, 0), (0, 1)], states)
            finish(sq, j, slice(0, L), hf + hb)
        else:
            for step in range(n_blocks):
                cf, cb = step, n_blocks - 1 - step
                (hf, hb), states = blocks(sq, j, [(cf, 0), (cb, 1)], states)
                hf_scr[cf * L:(cf + 1) * L, cols] = hf
                hb_scr[cb * L:(cb + 1) * L, cols] = hb
            finish(sq, j, slice(None), hf_scr[:, cols] + hb_scr[:, cols])
        if emit_state:
            emit(sq, j, 0, states[0])
            emit(sq, j, 1, states[1])


def _mlstm(mq, mkt, mv, mo, stats, norm_g, init_state, emit_state, heads, seqs):
    B, T, _ = mq.shape
    H, Dh, L = M_HEADS, M_HEAD_DIM, M_BLOCK
    nb = T // L
    w = heads * Dh
    has_init = init_state is not None
    n_rows = 4
    spread = np.tile(np.repeat(np.eye(n_rows, dtype=np.float32), LANES, axis=1), (3, 1))
    spread = jnp.asarray(np.pad(spread, ((0, 2 * SUBLANES - 3 * n_rows), (0, 0))), BF16)
    tok = pl.BlockSpec((seqs, T, w), lambda b, h: (b, 0, h))
    assert mkt.shape == (B // seqs, M_WIDTH, seqs * T) and stats.shape == (B // seqs, 3, 2 * H, seqs * T)
    in_specs = [tok, pl.BlockSpec((None, w, seqs * T), lambda b, h: (b, h, 0)), tok, tok,
                pl.BlockSpec((None, 3, 2 * H, seqs * T), lambda b, h: (b, 0, 0, 0)),
                pl.BlockSpec((heads, 1, Dh), lambda b, h: (h, 0, 0)),
                _const_spec((2 * SUBLANES, n_rows * LANES))]
    args = [mq, mkt, mv, mo, stats, norm_g, spread]
    state_specs = [pl.BlockSpec((seqs, 2, heads, Dh, Dh), lambda b, h: (b, 0, h, 0, 0)),
                   pl.BlockSpec((seqs, 2, heads, 1, Dh), lambda b, h: (b, 0, h, 0, 0)),
                   pl.BlockSpec((seqs, heads, 2, LANES), lambda b, h: (b, h, 0, 0))]
    if has_init:
        in_specs += state_specs
        args += list(init_state)
    out_specs = [tok]
    out_shape = [jax.ShapeDtypeStruct((B, T, M_WIDTH), BF16)]
    if emit_state:
        out_specs += state_specs
        out_shape += [jax.ShapeDtypeStruct((B, 2, H, Dh, Dh), F32),
                      jax.ShapeDtypeStruct((B, 2, H, 1, Dh), F32),
                      jax.ShapeDtypeStruct((B, H, 2, LANES), F32)]
    scratch = [] if nb == 1 else [pltpu.VMEM((T, w), F32), pltpu.VMEM((T, w), F32)]
    return pl.pallas_call(
        functools.partial(_mlstm_kernel, has_init, emit_state, nb, heads, seqs),
        grid=(B // seqs, H // heads),
        in_specs=in_specs,
        out_specs=out_specs,
        out_shape=out_shape,
        scratch_shapes=scratch,
        compiler_params=_params(("parallel", "parallel")),
        name="mlstm_latent" if has_init else "mlstm_context",
    )(*args)


def _attn_kernel(has_ctx, seqs, *refs):
    if has_ctx:
        q_ref, k_ref, v_ref, ckv_ref, krp_ref, wkv_ref, khg_ref, o_ref, kc_ref, vc_ref = refs

        @pl.when(pl.program_id(1) == 0)
        def _():
            kvf = _dot(ckv_ref[0].astype(BF16), wkv_ref[...])
            vc_ref[...] = kvf[:, QK_PAD:QK_PAD + A_WIDTH].astype(BF16)
            _write_heads(kvf, krp_ref[0], khg_ref[0:1, :], None, kc_ref)
    else:
        q_ref, k_ref, v_ref, o_ref = refs
    lane = lax.broadcasted_iota(jnp.int32, (1, LANES), 1)
    ones = lambda n: jnp.ones((n, LANES), BF16)
    tq = q_ref.shape[1]
    sub = min(Q_SUBTILE, tq)
    for sq, r0, pair in [(sq, r0, pair) for sq in range(seqs) for r0 in range(0, tq, sub)
                         for pair in range(A_HEADS // 2)]:
        rows = slice(r0, r0 + sub)
        vsl = slice(pair * LANES, (pair + 1) * LANES)
        v_aug = jnp.concatenate([v_ref[sq, :, vsl], ones(v_ref.shape[1])], axis=1)
        if has_ctx:
            vc_aug = jnp.concatenate([vc_ref[:, vsl], ones(vc_ref.shape[0])], axis=1)
        outs = []
        for e in range(2):
            hsl = slice((2 * pair + e) * HEAD_PAD, (2 * pair + e + 1) * HEAD_PAD)
            qh = q_ref[sq, rows, hsl]
            s = _dot_nt(qh, k_ref[sq, :, hsl])
            mx = jnp.max(s, axis=1, keepdims=True)
            if has_ctx:
                sc = _dot_nt(qh, kc_ref[:, hsl])
                mx = jnp.maximum(mx, jnp.max(sc, axis=1, keepdims=True))
            od = _dot(jnp.exp2(s - mx).astype(BF16), v_aug)
            if has_ctx:
                od = od + _dot(jnp.exp2(sc - mx).astype(BF16), vc_aug)
            outs.append(od[:, 0:LANES] / od[:, LANES:2 * LANES])
        o_ref[sq, rows, vsl] = jnp.where(lane < A_VDIM, outs[0], outs[1]).astype(o_ref.dtype)


def _attn(q, k, v, ctx, seqs):
    B, T, _ = q.shape
    tq = min(Q_TILE, T)
    has_ctx = ctx is not None
    full = lambda n, w: pl.BlockSpec((seqs, n, w), lambda b, i: (b, 0, 0))
    in_specs = [pl.BlockSpec((seqs, tq, QK_PAD), lambda b, i: (b, i, 0)), full(T, QK_PAD), full(T, A_WIDTH)]
    args = [q, k, v]
    scratch = []
    if has_ctx:
        assert seqs == 1
        P = ctx[0].shape[1]
        in_specs += [full(P, KV_LORA), full(P, HEAD_PAD), _const_spec((KV_LORA, QK_PAD + A_WIDTH)),
                     _const_spec((2, HEAD_PAD))]
        args += list(ctx)
        scratch = [pltpu.VMEM((P, QK_PAD), BF16), pltpu.VMEM((P, A_WIDTH), BF16)]
    return pl.pallas_call(
        functools.partial(_attn_kernel, has_ctx, seqs),
        grid=(B // seqs, T // tq),
        in_specs=in_specs,
        out_specs=pl.BlockSpec((seqs, tq, A_WIDTH), lambda b, i: (b, i, 0)),
        out_shape=jax.ShapeDtypeStruct((B, T, A_WIDTH), BF16),
        scratch_shapes=scratch,
        compiler_params=_params(("parallel", "arbitrary" if has_ctx else "parallel")),
        name="attn_latent" if has_ctx else "attn_context",
    )(*args)


def _post_kernel(mod_row0, mod_per_batch, x_ref, hm_ref, ha_ref, mod_ref, g2_ref, wout_ref, wup_ref, wdown_ref,
                 y_ref):
    mod = _mod_row(mod_ref, mod_row0, mod_per_batch)
    gate1 = mod[:, 2 * D_MODEL:3 * D_MODEL]
    sh2 = mod[:, 3 * D_MODEL:4 * D_MODEL]
    sc2 = mod[:, 4 * D_MODEL:5 * D_MODEL]
    gate2 = mod[:, 5 * D_MODEL:6 * D_MODEL]
    mix = jnp.concatenate([hm_ref[...], ha_ref[...]], axis=-1)
    x1 = x_ref[...] + gate1 * _dot(mix, wout_ref[...])
    h2 = (_rms(x1, g2_ref[...]) * (1.0 + sc2) + sh2).astype(BF16)
    acc = jnp.zeros_like(x1)
    for c in range(D_FF // FF_TILE):
        sl = slice(c * FF_TILE, (c + 1) * FF_TILE)
        u = jnp.maximum(_dot(h2, wup_ref[:, sl]), 0.0)
        acc = acc + _dot((u * u).astype(BF16), wdown_ref[sl, :])
    y_ref[...] = x1 + gate2 * acc


def _post(x, hm, ha, mod3, mod_row0, mod_per_batch, wts):
    shape = x.shape
    if not mod_per_batch:
        x, hm, ha = (a.reshape(1, -1, a.shape[-1]) for a in (x, hm, ha))
    B, T, _ = x.shape
    tm = POST_TILE
    tok = lambda w: pl.BlockSpec((None, tm, w), lambda b, i: (b, i, 0))
    return _post_call(x, hm, ha, mod3, mod_row0, mod_per_batch, wts, B, T, tm, tok).reshape(shape)


def _post_call(x, hm, ha, mod3, mod_row0, mod_per_batch, wts, B, T, tm, tok):
    return pl.pallas_call(
        functools.partial(_post_kernel, mod_row0, mod_per_batch),
        grid=(B, T // tm),
        in_specs=[tok(D_MODEL), tok(M_WIDTH), tok(A_WIDTH),
                  _const_spec((8, 6 * D_MODEL)),
                  _const_spec((1, D_MODEL)),
                  _const_spec((M_WIDTH + A_WIDTH, D_MODEL)),
                  _const_spec((D_MODEL, D_FF)),
                  _const_spec((D_FF, D_MODEL))],
        out_specs=tok(D_MODEL),
        out_shape=jax.ShapeDtypeStruct((B, T, D_MODEL), F32),
        compiler_params=_params(("parallel", "parallel")),
        name="post",
    )(x, hm, ha, mod3, wts["g2"], wts["w_out"], wts["w_up"], wts["w_down"])


def _prepare_weights(norm1_g, norm2_g, w_in, mlstm_gate_b, q_lora_g, kv_lora_g, w_q_up, w_kv_up,
                     q_head_g, k_head_g, w_out, w_mlp_up, w_mlp_down):
    o_g = 4 * M_WIDTH
    o_q = o_g + N_GATES
    o_kv = o_q + Q_LORA
    o_kr = o_kv + KV_LORA
    half = A_ROPE // 2
    n_dh = 2 * M_HEADS
    wt = w_in.T.astype(BF16)
    w_gate = wt[o_g:o_q].reshape(2, 2, M_HEADS, D_MODEL)

    def rot_partner(a):
        z = jnp.zeros(a.shape[:-1] + (A_NOPE,), a.dtype)
        return jnp.concatenate([z, a[..., A_NOPE + half:A_QK], a[..., A_NOPE:A_NOPE + half]], axis=-1)

    pad_tile = lambda a: jnp.pad(a, [(0, 0)] * (a.ndim - 1) + [(0, HEAD_PAD - A_QK)])
    w_kr = wt[o_kr:o_kr + A_ROPE]
    w_kr_partner = jnp.concatenate([w_kr[half:], w_kr[:half]], axis=0)
    w_lat = jnp.concatenate([wt[o_q:o_kr], w_gate[:, 0].reshape(n_dh, D_MODEL), w_gate[:, 1].reshape(n_dh, D_MODEL),
                             jnp.zeros((A_NOPE - 2 * n_dh, D_MODEL), BF16), w_kr, w_kr_partner], axis=0)
    def lane_table(vec, src):
        src = np.array([list(r) + [-1] * (LANES - len(r)) for r in src])
        return jnp.where(src >= 0, vec[np.maximum(src, 0)], 0.0)

    dir_head = [d * 2 * M_HEADS + h for d in range(2) for h in range(M_HEADS)]
    gate_bias = lane_table(mlstm_gate_b, [dir_head, [M_HEADS + j for j in dir_head]])
    w_q3 = w_q_up.reshape(Q_LORA, A_HEADS, A_QK)
    w_q = pad_tile(w_q3).reshape(Q_LORA, QK_PAD)
    w_q_partner = pad_tile(rot_partner(w_q3)).reshape(Q_LORA, QK_PAD)
    w_kv3 = w_kv_up.reshape(KV_LORA, A_HEADS, A_NOPE + A_VDIM)
    w_k = jnp.pad(w_kv3[:, :, :A_NOPE], ((0, 0), (0, 0), (0, HEAD_PAD - A_NOPE)))
    w_v = w_kv3[:, :, A_NOPE:]
    w_kv = jnp.concatenate([w_k.reshape(KV_LORA, QK_PAD), w_v.reshape(KV_LORA, A_WIDTH)], axis=1)
    own = list(range(A_QK))
    partner = [-1] * A_NOPE + own[A_NOPE + half:] + own[A_NOPE:A_NOPE + half]
    pad_head = lambda g: lane_table(g, [own, partner])
    return {
        "g1": norm1_g.reshape(1, D_MODEL),
        "g2": norm2_g.reshape(1, D_MODEL),
        "w_main": jnp.concatenate([wt[0:M_WIDTH], wt[2 * M_WIDTH:o_g]], axis=0),
        "w_kt": wt[M_WIDTH:2 * M_WIDTH],
        "w_lat": w_lat,
        "gate_bias": gate_bias,
        "q_lora_g": q_lora_g.reshape(1, Q_LORA),
        "kv_lora_g": kv_lora_g.reshape(1, KV_LORA),
        "w_q": w_q.astype(BF16),
        "w_q_rot": jnp.concatenate([w_q, w_q_partner], axis=1).astype(BF16),
        "w_kv": w_kv.astype(BF16),
        "q_head_g": pad_head(q_head_g),
        "k_head_g": pad_head(k_head_g),
        "w_out_f32": w_out,
        "w_up_f32": w_mlp_up,
        "w_down_f32": w_mlp_down,
    }


def _rope_tables(T):
    rows = T // GRID_W
    row = np.repeat(np.arange(rows, dtype=np.float32), GRID_W)
    col = np.tile(np.arange(GRID_W, dtype=np.float32), rows)
    half = A_ROPE // 2
    inv = (np.float32(ROPE_BASE) ** (-np.arange(0, half, 2, dtype=np.float32) / np.float32(half))).astype(np.float32)
    ang = np.concatenate([row[:, None] * inv, col[:, None] * inv], axis=-1)
    cos, sin = np.cos(ang), np.sin(ang)
    ones = np.ones((T, A_NOPE), np.float32)
    z = lambda w: np.zeros((T, w), np.float32)
    tail = LANES - A_QK
    cos_t = np.concatenate([ones, cos, cos, z(tail)], axis=1)
    sin_t = np.concatenate([z(A_NOPE), -sin, sin, z(tail)], axis=1)
    return jnp.asarray(np.stack([cos_t, sin_t], axis=0).astype(np.float32))


def _layer_pass(x, mod3, mod_row0, mod_per_batch, wts, norm_g, rope_tab, init_state, ctx_kv, is_context):
    pre = _pre(x, mod3, mod_row0, mod_per_batch, wts, rope_tab, emit_cache=is_context)
    mq, mkt, mv, mo, stats, q, k, v = pre[:8]
    if is_context:
        wts = dict(wts, w_out=pre[10], w_up=pre[11], w_down=pre[12])
    ml = _mlstm(mq, mkt, mv, mo, stats, norm_g, init_state, emit_state=is_context,
                heads=M_HEADS if is_context else 1, seqs=MLSTM_CONTEXT_SEQS if is_context else 1)
    ha = _attn(q, k, v, ctx_kv, seqs=ATTN_CONTEXT_SEQS if is_context else 1)
    y = _post(x, ml[0], ha, mod3, mod_row0, mod_per_batch, wts)
    return y, pre[8:10], ml[1:], wts


def kernel(x_prompt, x_sample, cache_mla_ckv, cache_mla_krope, state_mlstm_C, state_mlstm_n, state_mlstm_m,
           c, c_ctx, norm1_g, norm2_g, w_ada, b_ada, w_in, mlstm_gate_b, mlstm_norm_g,
           q_lora_g, kv_lora_g, w_q_up, w_kv_up, q_head_g, k_head_g, w_out, w_mlp_up, w_mlp_down):
    depth = w_in.shape[0]
    Bd = x_sample.shape[0]
    cond8 = jnp.concatenate([c_ctx[None, :], c, jnp.zeros((8 - 1 - Bd, D_MODEL), F32)], axis=0)
    rope_tab = _rope_tables(x_sample.shape[1])

    y, z = x_prompt, x_sample
    ckvs, kropes, Cs, ns, ms = [], [], [], [], []
    for l in range(depth):
        wts = _prepare_weights(norm1_g[l], norm2_g[l], w_in[l], mlstm_gate_b[l], q_lora_g[l], kv_lora_g[l],
                               w_q_up[l], w_kv_up[l], q_head_g[l], k_head_g[l], w_out[l], w_mlp_up[l],
                               w_mlp_down[l])
        norm_g = mlstm_norm_g[l].reshape(M_HEADS, 1, M_HEAD_DIM)
        mod3 = _ada(cond8, w_ada[l], b_ada[l])
        y, (ckv, krope), (C_new, n_new, m_new), wts = _layer_pass(
            y, mod3, 0, 0, wts, norm_g, None, None, None, True)
        ckvs.append(ckv)
        kropes.append(krope)
        Cs.append(C_new)
        ns.append(n_new[:, :, :, 0, :])
        ms.append(m_new[:, :, :, 0].transpose(0, 2, 1))

        init_state = (state_mlstm_C[:, l],
                      state_mlstm_n[:, l][:, :, :, None, :],
                      jnp.broadcast_to(state_mlstm_m[:, l].transpose(0, 2, 1)[..., None],
                                       (Bd, M_HEADS, 2, LANES)))
        krope_placed = jnp.pad(cache_mla_krope[:, l], ((0, 0), (0, 0), (A_NOPE, LANES - A_QK)))
        ctx = (cache_mla_ckv[:, l], krope_placed, wts["w_kv"], wts["k_head_g"])
        z, _, _, _ = _layer_pass(z, mod3, 1, 1, wts, norm_g, rope_tab, init_state, ctx, False)

    return (y, z, jnp.stack(ckvs, axis=1), jnp.stack(kropes, axis=1), jnp.stack(Cs, axis=1),
            jnp.stack(ns, axis=1), jnp.stack(ms, axis=1))
```

```python
import functools

import jax
import jax.numpy as jnp
import numpy as np
from jax import lax
from jax.experimental import pallas as pl
from jax.experimental.pallas import tpu as pltpu

F32 = jnp.float32
BF16 = jnp.bfloat16

D_MODEL = 1024
GRID_W = 64
M_HEADS = 4
M_HEAD_DIM = 128
M_WIDTH = M_HEADS * M_HEAD_DIM
M_BLOCK = 256
A_HEADS = 8
A_NOPE = 64
A_ROPE = 32
A_QK = A_NOPE + A_ROPE
A_VDIM = 64
A_WIDTH = A_HEADS * A_VDIM
Q_LORA = 384
KV_LORA = 256
ROPE_BASE = 10000.0
D_FF = 4 * D_MODEL
EPS = 1e-6

LANES = 128
SUBLANES = 8
LOG2E = 1.4426950408889634
HEAD_PAD = LANES
QK_PAD = A_HEADS * HEAD_PAD
N_GATES = 4 * M_HEADS
LAT_WIDTH = Q_LORA + KV_LORA + LANES
VMEM_LIMIT = 56 * 1024 * 1024

TOKEN_TILE = 512
POST_TILE = 512
MLSTM_CONTEXT_SEQS = 2
ATTN_CONTEXT_SEQS = 4
Q_TILE = 2048
Q_SUBTILE = 256
ADA_TILE_N = 1536
FF_TILE = 1024


def _dot(a, b):
    return jnp.dot(a, b, preferred_element_type=F32)


def _dot_nt(a, b):
    return lax.dot_general(a, b, (((1,), (1,)), ((), ())), preferred_element_type=F32)


def _dot_tn(a, b):
    return lax.dot_general(a, b, (((0,), (0,)), ((), ())), preferred_element_type=F32)


def _rms(x, g):
    y = x * lax.rsqrt(jnp.mean(x * x, axis=-1, keepdims=True) + EPS)
    return y * g


def _params(sem):
    return pltpu.CompilerParams(dimension_semantics=sem, vmem_limit_bytes=VMEM_LIMIT)


def _const_spec(shape):
    zeros = (0,) * len(shape)
    return pl.BlockSpec(shape, lambda *_: zeros, pipeline_mode=pl.Buffered(1))


def _ada_kernel(cond_ref, w_ref, b_ref, o_ref):
    c = cond_ref[...]
    s = (c * jax.nn.sigmoid(c)).astype(BF16)
    o_ref[...] = _dot(s, w_ref[...].astype(BF16)) + b_ref[...]


def _ada(cond8, w_ada, b_ada):
    n = w_ada.shape[1]
    return pl.pallas_call(
        _ada_kernel,
        grid=(n // ADA_TILE_N,),
        in_specs=[
            pl.BlockSpec((8, D_MODEL), lambda j: (0, 0)),
            pl.BlockSpec((D_MODEL, ADA_TILE_N), lambda j: (0, j)),
            pl.BlockSpec((1, ADA_TILE_N), lambda j: (0, j)),
        ],
        out_specs=pl.BlockSpec((8, ADA_TILE_N), lambda j: (0, j)),
        out_shape=jax.ShapeDtypeStruct((8, n), F32),
        compiler_params=_params(("parallel",)),
        name="ada",
    )(cond8, w_ada, b_ada.reshape(1, n))


def _write_heads(src, extra, g_pad, rot, dst_ref):
    for h in range(A_HEADS):
        sl = slice(h * HEAD_PAD, (h + 1) * HEAD_PAD)
        xh = src[:, sl]
        if extra is not None:
            xh = xh + extra
        ss = jnp.sum(xh * xh, axis=-1, keepdims=True) * (1.0 / A_QK)
        r = lax.rsqrt(ss + EPS)
        if rot is None:
            y = xh * r * g_pad
        else:
            partner, cos_g, sin_g = rot
            ph = partner if partner.shape[1] == HEAD_PAD else partner[:, sl]
            y = (xh * cos_g + ph * sin_g) * r
        dst_ref[:, sl] = y.astype(dst_ref.dtype)


def _time_scan(x, op, identity, reverse):
    n = x.shape[0]
    row = lax.broadcasted_iota(jnp.int32, x.shape, 0)
    shift = 1
    while shift < n:
        if shift < SUBLANES:
            if reverse:
                moved = jnp.where(row < n - shift, pltpu.roll(x, n - shift, 0), identity)
            else:
                moved = jnp.where(row >= shift, pltpu.roll(x, shift, 0), identity)
        else:
            fill = jnp.full((shift, x.shape[1]), identity, x.dtype)
            moved = (jnp.concatenate([x[shift:], fill], axis=0) if reverse
                     else jnp.concatenate([fill, x[:n - shift]], axis=0))
        x = op(x, moved)
        shift *= 2
    return x


def _mod_row(mod_ref, row0, per_batch):
    if per_batch:
        return mod_ref[pl.ds(row0 + pl.program_id(0) * per_batch, 1), :]
    return mod_ref[row0:row0 + 1, :]


def _pre_kernel(has_rope, emit_cache, mod_row0, mod_per_batch, *refs):
    (x_ref, mod_ref, g1_ref, wmain_ref, wkt_ref, wlat_ref, gbias_ref, qlg_ref, kvg_ref, wq_ref, wkv_ref,
     qhg_ref, khg_ref) = refs[:13]
    pos = 13
    if has_rope:
        rope_ref = refs[pos]
        pos += 1
    if emit_cache:
        cast_in = refs[pos:pos + 3]
        pos += 3
    (mq_ref, mkt_ref, mv_ref, mo_ref, stats_ref, q_ref, k_ref, v_ref) = refs[pos:pos + 8]
    pos += 8

    x = x_ref[...]
    mod = _mod_row(mod_ref, mod_row0, mod_per_batch)
    sh1 = mod[:, 0:D_MODEL]
    sc1 = mod[:, D_MODEL:2 * D_MODEL]
    h = _rms(x, g1_ref[...]) * (1.0 + sc1) + sh1
    hb = h.astype(BF16)

    plat = _dot_nt(hb, wlat_ref[...])
    q_lat = plat[:, 0:Q_LORA]
    kv_lat = plat[:, Q_LORA:Q_LORA + KV_LORA]
    tail = plat[:, Q_LORA + KV_LORA:LAT_WIDTH]
    tail_f = pltpu.roll(tail, LANES - 2 * M_HEADS, 1)
    tail2 = pltpu.roll(tail, LANES - A_ROPE, 1)

    lane = lax.broadcasted_iota(jnp.int32, (1, LANES), 1)
    fwd = lane < M_HEADS
    gate_i = tail + gbias_ref[0:1, :]
    gate_f = tail_f + gbias_ref[1:2, :]
    log_f = jnp.minimum(gate_f, 0.0) - jnp.log1p(jnp.exp(-jnp.abs(gate_f)))
    def block_scan(v, op, identity):
        parts = [v[r:r + M_BLOCK] for r in range(0, v.shape[0], M_BLOCK)]
        return jnp.where(fwd, jnp.concatenate([_time_scan(p, op, identity, False) for p in parts], axis=0),
                         jnp.concatenate([_time_scan(p, op, identity, True) for p in parts], axis=0))

    b = block_scan(log_f, jnp.add, 0.0)
    a = gate_i - b
    amax = block_scan(a, jnp.maximum, -jnp.inf)
    stats_ref[0] = b.T[0:2 * M_HEADS, :]
    stats_ref[1] = a.T[0:2 * M_HEADS, :]
    stats_ref[2] = amax.T[0:2 * M_HEADS, :]

    krope_placed = jnp.where((lane >= A_NOPE) & (lane < A_QK), tail, 0.0)
    ckv = _rms(kv_lat, kvg_ref[...])
    qn = _rms(q_lat, qlg_ref[...])
    qf = _dot(qn.astype(BF16), wq_ref[...])
    kvf = _dot(ckv.astype(BF16), wkv_ref[...])
    v_ref[...] = kvf[:, QK_PAD:QK_PAD + A_WIDTH].astype(BF16)
    q_rot = k_rot = None
    qhg = qhg_ref[...] * (A_QK ** -0.5 * LOG2E)
    if has_rope:
        cos_t, sin_t = rope_ref[0], rope_ref[1]
        q_rot = (qf[:, QK_PAD:2 * QK_PAD], cos_t * qhg[0:1, :], sin_t * qhg[1:2, :])
        k_rot = (tail2, cos_t * khg_ref[0:1, :], sin_t * khg_ref[1:2, :])
    _write_heads(qf, None, qhg[0:1, :], q_rot, q_ref)
    _write_heads(kvf, krope_placed, khg_ref[0:1, :], k_rot, k_ref)

    if emit_cache:
        ckv_ref, krope_ref = refs[pos:pos + 2]
        ckv_ref[...] = ckv
        krope_ref[...] = tail[:, A_NOPE:A_QK]
        for src_ref, dst_ref in zip(cast_in, refs[pos + 2:pos + 5]):
            dst_ref[...] = src_ref[...].astype(BF16)

    pm = _dot_nt(hb, wmain_ref[...])
    mq_ref[...] = pm[:, 0:M_WIDTH].astype(BF16)
    mv_ref[...] = pm[:, M_WIDTH:2 * M_WIDTH].astype(BF16)
    mo_ref[...] = pm[:, 2 * M_WIDTH:3 * M_WIDTH]
    mkt_ref[...] = _dot_nt(wkt_ref[...], hb) * (M_HEAD_DIM ** -0.5)


def _pre(x, mod3, mod_row0, mod_per_batch, wts, rope_tab, emit_cache):
    shape = x.shape
    tm = TOKEN_TILE
    if shape[1] < tm:
        assert not mod_per_batch and tm % shape[1] == 0
        x = x.reshape(-1, tm, shape[2])
    B, T, _ = x.shape
    has_rope = rope_tab is not None
    tok = lambda w: pl.BlockSpec((None, tm, w), lambda b, i: (b, i, 0))
    in_specs = [
        tok(D_MODEL),
        _const_spec((8, 6 * D_MODEL)),
        _const_spec((1, D_MODEL)),
        _const_spec((3 * M_WIDTH, D_MODEL)),
        _const_spec((M_WIDTH, D_MODEL)),
        _const_spec((LAT_WIDTH, D_MODEL)),
        _const_spec((2, LANES)),
        _const_spec((1, Q_LORA)),
        _const_spec((1, KV_LORA)),
        _const_spec((Q_LORA, 2 * QK_PAD if has_rope else QK_PAD)),
        _const_spec((KV_LORA, QK_PAD + A_WIDTH)),
        _const_spec((2, HEAD_PAD)),
        _const_spec((2, HEAD_PAD)),
    ]
    assert tm % M_BLOCK == 0
    args = [x, mod3, wts["g1"], wts["w_main"], wts["w_kt"], wts["w_lat"], wts["gate_bias"], wts["q_lora_g"],
            wts["kv_lora_g"], wts["w_q_rot"] if has_rope else wts["w_q"], wts["w_kv"],
            wts["q_head_g"], wts["k_head_g"]]
    if has_rope:
        in_specs.append(pl.BlockSpec((2, tm, HEAD_PAD), lambda b, i: (0, i, 0)))
        args.append(rope_tab)
    out_specs = [tok(M_WIDTH),
                 pl.BlockSpec((None, M_WIDTH, tm), lambda b, i: (b, 0, i)),
                 tok(M_WIDTH), tok(M_WIDTH),
                 pl.BlockSpec((None, 3, 2 * M_HEADS, tm), lambda b, i: (b, 0, 0, i)),
                 tok(QK_PAD), tok(QK_PAD), tok(A_WIDTH)]
    out_shape = [
        jax.ShapeDtypeStruct((B, T, M_WIDTH), BF16),
        jax.ShapeDtypeStruct((B, M_WIDTH, T), F32),
        jax.ShapeDtypeStruct((B, T, M_WIDTH), BF16),
        jax.ShapeDtypeStruct((B, T, M_WIDTH), F32),
        jax.ShapeDtypeStruct((B, 3, 2 * M_HEADS, T), F32),
        jax.ShapeDtypeStruct((B, T, QK_PAD), BF16),
        jax.ShapeDtypeStruct((B, T, QK_PAD), BF16),
        jax.ShapeDtypeStruct((B, T, A_WIDTH), BF16),
    ]
    if emit_cache:
        out_specs += [tok(KV_LORA), tok(A_ROPE)]
        out_shape += [jax.ShapeDtypeStruct((B, T, KV_LORA), F32),
                      jax.ShapeDtypeStruct((B, T, A_ROPE), F32)]
        steps, per_b = B * (T // tm), T // tm
        for w in (wts["w_out_f32"], wts["w_up_f32"], wts["w_down_f32"]):
            rows = w.shape[0] // steps
            assert rows * steps == w.shape[0] and rows % (2 * SUBLANES) == 0
            spec = pl.BlockSpec((rows, w.shape[1]), lambda b, i: (b * per_b + i, 0))
            in_specs.append(spec)
            args.append(w)
            out_specs.append(spec)
            out_shape.append(jax.ShapeDtypeStruct(w.shape, BF16))
    outs = pl.pallas_call(
        functools.partial(_pre_kernel, has_rope, emit_cache, mod_row0, mod_per_batch),
        grid=(B, T // tm),
        in_specs=in_specs,
        out_specs=out_specs,
        out_shape=out_shape,
        compiler_params=_params(("parallel", "parallel")),
        name="pre_latent" if has_rope else "pre_context",
    )(*args)
    keep = (1, 4, 10, 11, 12)
    return [o if n in keep else o.reshape(shape[:2] + o.shape[2:]) for n, o in enumerate(outs)]


def _rows_to_lane_broadcast(rows, spread):
    x = jnp.concatenate(rows, axis=0)
    p1 = x.astype(BF16)
    r1 = x - p1.astype(F32)
    p2 = r1.astype(BF16)
    p3 = (r1 - p2.astype(F32)).astype(BF16)
    pad = jnp.zeros((spread.shape[0] - 3 * len(rows), x.shape[1]), BF16)
    return _dot_tn(jnp.concatenate([p1, p2, p3, pad], axis=0), spread)


def _mlstm_gate_rows(b_row, a_row, amax_row, forward, m):
    L = b_row.shape[1]
    last = slice(L - 1, L) if forward else slice(0, 1)
    total = b_row[:, last]
    g_row = jnp.maximum(m, amax_row)
    m_new = total + jnp.maximum(m, amax_row[:, last])
    w_key_row = jnp.exp2((a_row + (total - m_new)) * LOG2E)
    decay = jnp.exp(total + m - m_new)
    return g_row * LOG2E, (b_row + g_row) * LOG2E, a_row * LOG2E, w_key_row, decay, m_new


def _mlstm_block(s_raw, q, kt, v_aug, g2, mt2, a2_row, w_key_row, decay, allow, CN, m):
    w_intra = jnp.exp2(jnp.where(allow, a2_row - jnp.concatenate([g2, g2], axis=1), -jnp.inf))
    w_inter = jnp.exp2(m * LOG2E - g2)
    s = (s_raw * w_intra).astype(BF16)
    nd = _dot(s, v_aug) + jnp.concatenate([w_inter, w_inter], axis=1) * _dot(q, CN.astype(BF16))
    num, den = nd[:, 0:M_HEAD_DIM], nd[:, M_HEAD_DIM:2 * M_HEAD_DIM]
    h = num / jnp.maximum(jnp.abs(den), jnp.exp2(-mt2))
    CN_new = decay * CN + _dot((kt * w_key_row).astype(BF16), v_aug)
    return h, CN_new


def _mlstm_kernel(has_init, emit_state, n_blocks, heads, seqs, *refs):
    q_ref, kt_ref, v_ref, mo_ref, stats_ref, ng_ref, spread_ref = refs[:7]
    pos = 7
    if has_init:
        c0_ref, n0_ref, m0_ref = refs[pos:pos + 3]
        pos += 3
    hm_ref = refs[pos]
    pos += 1
    if emit_state:
        c_ref, n_ref, m_ref = refs[pos:pos + 3]
        pos += 3

    L, Dh = M_BLOCK, M_HEAD_DIM
    t_idx = lax.broadcasted_iota(jnp.int32, (L, L), 0)
    s_idx = lax.broadcasted_iota(jnp.int32, (L, L), 1)
    allow = (s_idx <= t_idx, s_idx >= t_idx)
    spread = spread_ref[...]
    ones = jnp.ones((L, Dh), BF16)

    def time_lanes(sq, c):
        start = (sq * n_blocks + c) * L
        return slice(start, start + L)

    def lane_broadcast_n(n_row):
        return jnp.broadcast_to(n_row, (Dh, Dh)).T

    def init_state(sq, j, d):
        if has_init:
            return (jnp.concatenate([c0_ref[sq, d, j], lane_broadcast_n(n0_ref[sq, d, j])], axis=1),
                    m0_ref[sq, j, d:d + 1, 0:1])
        return jnp.zeros((Dh, 2 * Dh), F32), jnp.zeros((1, 1), F32)

    def gate_rows(sq, j, c, d, m):
        head = j if heads == M_HEADS else pl.program_id(1) * heads + j
        r, lanes = pl.ds(d * M_HEADS + head, 1), time_lanes(sq, c)
        return _mlstm_gate_rows(stats_ref[0, r, lanes], stats_ref[1, r, lanes], stats_ref[2, r, lanes], d == 0, m)

    def blocks(sq, j, jobs, states):
        loaded, rows6, cols_in = {}, [], []
        for (c, d), (CN, m) in zip(jobs, states):
            rows6.append(gate_rows(sq, j, c, d, m))
            cols_in += [rows6[-1][0], rows6[-1][1]]
            if c not in loaded:
                rows, cols = slice(c * L, (c + 1) * L), slice(j * Dh, (j + 1) * Dh)
                q, kt = q_ref[sq, rows, cols], kt_ref[cols, time_lanes(sq, c)]
                v_aug = jnp.concatenate([v_ref[sq, rows, cols], ones], axis=1)
                loaded[c] = (_dot(q, kt.astype(BF16)), q, kt, v_aug)
        cols_out = _rows_to_lane_broadcast(cols_in, spread)
        hs, new_states = [], []
        for idx, ((c, d), (CN, m)) in enumerate(zip(jobs, states)):
            g2 = cols_out[:, (2 * idx) * LANES:(2 * idx + 1) * LANES]
            mt2 = cols_out[:, (2 * idx + 1) * LANES:(2 * idx + 2) * LANES]
            _, _, a2_row, w_key_row, decay, m_new = rows6[idx]
            h, CN_new = _mlstm_block(*loaded[c], g2, mt2, a2_row, w_key_row, decay, allow[d], CN, m)
            hs.append(h)
            new_states.append((CN_new, m_new))
        return hs, new_states

    def finish(sq, j, rows, hs):
        cols = slice(j * Dh, (j + 1) * Dh)
        hn = _rms(hs, ng_ref[j])
        hm_ref[sq, rows, cols] = (hn * jax.nn.sigmoid(mo_ref[sq, rows, cols])).astype(hm_ref.dtype)

    def emit(sq, j, d, state):
        CN, m = state
        c_ref[sq, d, j] = CN[:, 0:Dh]
        n_ref[sq, d, j] = CN[:, Dh:2 * Dh].T[0:1, :]
        m_ref[sq, j, d:d + 1, :] = jnp.broadcast_to(m, (1, LANES))

    if n_blocks > 1:
        hf_scr, hb_scr = refs[pos:pos + 2]

    for sq, j in [(sq, j) for sq in range(seqs) for j in range(heads)]:
        cols = slice(j * Dh, (j + 1) * Dh)
        states = [init_state(sq, j, 0), init_state(sq, j, 1)]
        if n_blocks == 1:
            (hf, hb), states = blocks(sq, j, [(0, 0), (0, 1)], states)
            finish(sq, j, slice(0, L), hf + hb)
        else:
            for step in range(n_blocks):
                cf, cb = step, n_blocks - 1 - step
                (hf, hb), states = blocks(sq, j, [(cf, 0), (cb, 1)], states)
                hf_scr[cf * L:(cf + 1) * L, cols] = hf
                hb_scr[cb * L:(cb + 1) * L, cols] = hb
            finish(sq, j, slice(None), hf_scr[:, cols] + hb_scr[:, cols])
        if emit_state:
            emit(sq, j, 0, states[0])
            emit(sq, j, 1, states[1])


def _mlstm(mq, mkt, mv, mo, stats, norm_g, init_state, emit_state, heads, seqs):
    B, T, _ = mq.shape
    H, Dh, L = M_HEADS, M_HEAD_DIM, M_BLOCK
    nb = T // L
    w = heads * Dh
    has_init = init_state is not None
    n_rows = 4
    spread = jnp.tile(jnp.repeat(jnp.eye(n_rows, dtype=BF16), LANES, axis=1), (3, 1))
    spread = jnp.pad(spread, ((0, 2 * SUBLANES - 3 * n_rows), (0, 0)))
    tok = pl.BlockSpec((seqs, T, w), lambda b, h: (b, 0, h))
    assert mkt.shape == (B // seqs, M_WIDTH, seqs * T) and stats.shape == (B // seqs, 3, 2 * H, seqs * T)
    in_specs = [tok, pl.BlockSpec((None, w, seqs * T), lambda b, h: (b, h, 0)), tok, tok,
                pl.BlockSpec((None, 3, 2 * H, seqs * T), lambda b, h: (b, 0, 0, 0)),
                pl.BlockSpec((heads, 1, Dh), lambda b, h: (h, 0, 0)),
                _const_spec((2 * SUBLANES, n_rows * LANES))]
    args = [mq, mkt, mv, mo, stats, norm_g, spread]
    state_specs = [pl.BlockSpec((seqs, 2, heads, Dh, Dh), lambda b, h: (b, 0, h, 0, 0)),
                   pl.BlockSpec((seqs, 2, heads, 1, Dh), lambda b, h: (b, 0, h, 0, 0)),
                   pl.BlockSpec((seqs, heads, 2, LANES), lambda b, h: (b, h, 0, 0))]
    if has_init:
        in_specs += state_specs
        args += list(init_state)
    out_specs = [tok]
    out_shape = [jax.ShapeDtypeStruct((B, T, M_WIDTH), BF16)]
    if emit_state:
        out_specs += state_specs
        out_shape += [jax.ShapeDtypeStruct((B, 2, H, Dh, Dh), F32),
                      jax.ShapeDtypeStruct((B, 2, H, 1, Dh), F32),
                      jax.ShapeDtypeStruct((B, H, 2, LANES), F32)]
    scratch = [] if nb == 1 else [pltpu.VMEM((T, w), F32), pltpu.VMEM((T, w), F32)]
    return pl.pallas_call(
        functools.partial(_mlstm_kernel, has_init, emit_state, nb, heads, seqs),
        grid=(B // seqs, H // heads),
        in_specs=in_specs,
        out_specs=out_specs,
        out_shape=out_shape,
        scratch_shapes=scratch,
        compiler_params=_params(("parallel", "parallel")),
        name="mlstm_latent" if has_init else "mlstm_context",
    )(*args)


def _attn_kernel(has_ctx, seqs, *refs):
    if has_ctx:
        q_ref, k_ref, v_ref, ckv_ref, krp_ref, wkv_ref, khg_ref, o_ref, kc_ref, vc_ref = refs

        @pl.when(pl.program_id(1) == 0)
        def _():
            kvf = _dot(ckv_ref[0].astype(BF16), wkv_ref[...])
            vc_ref[...] = kvf[:, QK_PAD:QK_PAD + A_WIDTH].astype(BF16)
            _write_heads(kvf, krp_ref[0], khg_ref[0:1, :], None, kc_ref)
    else:
        q_ref, k_ref, v_ref, o_ref = refs
    lane = lax.broadcasted_iota(jnp.int32, (1, LANES), 1)
    ones = lambda n: jnp.ones((n, LANES), BF16)
    tq = q_ref.shape[1]
    sub = min(Q_SUBTILE, tq)
    for sq, r0, pair in [(sq, r0, pair) for sq in range(seqs) for r0 in range(0, tq, sub)
                         for pair in range(A_HEADS // 2)]:
        rows = slice(r0, r0 + sub)
        vsl = slice(pair * LANES, (pair + 1) * LANES)
        v_aug = jnp.concatenate([v_ref[sq, :, vsl], ones(v_ref.shape[1])], axis=1)
        if has_ctx:
            vc_aug = jnp.concatenate([vc_ref[:, vsl], ones(vc_ref.shape[0])], axis=1)
        outs = []
        for e in range(2):
            hsl = slice((2 * pair + e) * HEAD_PAD, (2 * pair + e + 1) * HEAD_PAD)
            qh = q_ref[sq, rows, hsl]
            s = _dot_nt(qh, k_ref[sq, :, hsl])
            mx = jnp.max(s, axis=1, keepdims=True)
            if has_ctx:
                sc = _dot_nt(qh, kc_ref[:, hsl])
                mx = jnp.maximum(mx, jnp.max(sc, axis=1, keepdims=True))
            od = _dot(jnp.exp2(s - mx).astype(BF16), v_aug)
            if has_ctx:
                od = od + _dot(jnp.exp2(sc - mx).astype(BF16), vc_aug)
            outs.append(od[:, 0:LANES] / od[:, LANES:2 * LANES])
        o_ref[sq, rows, vsl] = jnp.where(lane < A_VDIM, outs[0], outs[1]).astype(o_ref.dtype)


def _attn(q, k, v, ctx, seqs):
    B, T, _ = q.shape
    tq = min(Q_TILE, T)
    has_ctx = ctx is not None
    full = lambda n, w: pl.BlockSpec((seqs, n, w), lambda b, i: (b, 0, 0))
    in_specs = [pl.BlockSpec((seqs, tq, QK_PAD), lambda b, i: (b, i, 0)), full(T, QK_PAD), full(T, A_WIDTH)]
    args = [q, k, v]
    scratch = []
    if has_ctx:
        assert seqs == 1
        P = ctx[0].shape[1]
        in_specs += [full(P, KV_LORA), full(P, HEAD_PAD), _const_spec((KV_LORA, QK_PAD + A_WIDTH)),
                     _const_spec((2, HEAD_PAD))]
        args += list(ctx)
        scratch = [pltpu.VMEM((P, QK_PAD), BF16), pltpu.VMEM((P, A_WIDTH), BF16)]
    return pl.pallas_call(
        functools.partial(_attn_kernel, has_ctx, seqs),
        grid=(B // seqs, T // tq),
        in_specs=in_specs,
        out_specs=pl.BlockSpec((seqs, tq, A_WIDTH), lambda b, i: (b, i, 0)),
        out_shape=jax.ShapeDtypeStruct((B, T, A_WIDTH), BF16),
        scratch_shapes=scratch,
        compiler_params=_params(("parallel", "arbitrary" if has_ctx else "parallel")),
        name="attn_latent" if has_ctx else "attn_context",
    )(*args)


def _post_kernel(mod_row0, mod_per_batch, x_ref, hm_ref, ha_ref, mod_ref, g2_ref, wout_ref, wup_ref, wdown_ref,
                 y_ref):
    mod = _mod_row(mod_ref, mod_row0, mod_per_batch)
    gate1 = mod[:, 2 * D_MODEL:3 * D_MODEL]
    sh2 = mod[:, 3 * D_MODEL:4 * D_MODEL]
    sc2 = mod[:, 4 * D_MODEL:5 * D_MODEL]
    gate2 = mod[:, 5 * D_MODEL:6 * D_MODEL]
    mix = jnp.concatenate([hm_ref[...], ha_ref[...]], axis=-1)
    x1 = x_ref[...] + gate1 * _dot(mix, wout_ref[...])
    h2 = (_rms(x1, g2_ref[...]) * (1.0 + sc2) + sh2).astype(BF16)
    acc = jnp.zeros_like(x1)
    for c in range(D_FF // FF_TILE):
        sl = slice(c * FF_TILE, (c + 1) * FF_TILE)
        u = jnp.maximum(_dot(h2, wup_ref[:, sl]), 0.0)
        acc = acc + _dot((u * u).astype(BF16), wdown_ref[sl, :])
    y_ref[...] = x1 + gate2 * acc


def _post(x, hm, ha, mod3, mod_row0, mod_per_batch, wts):
    shape = x.shape
    if not mod_per_batch:
        x, hm, ha = (a.reshape(1, -1, a.shape[-1]) for a in (x, hm, ha))
    B, T, _ = x.shape
    tm = POST_TILE
    tok = lambda w: pl.BlockSpec((None, tm, w), lambda b, i: (b, i, 0))
    return _post_call(x, hm, ha, mod3, mod_row0, mod_per_batch, wts, B, T, tm, tok).reshape(shape)


def _post_call(x, hm, ha, mod3, mod_row0, mod_per_batch, wts, B, T, tm, tok):
    return pl.pallas_call(
        functools.partial(_post_kernel, mod_row0, mod_per_batch),
        grid=(B, T // tm),
        in_specs=[tok(D_MODEL), tok(M_WIDTH), tok(A_WIDTH),
                  _const_spec((8, 6 * D_MODEL)),
                  _const_spec((1, D_MODEL)),
                  _const_spec((M_WIDTH + A_WIDTH, D_MODEL)),
                  _const_spec((D_MODEL, D_FF)),
                  _const_spec((D_FF, D_MODEL))],
        out_specs=tok(D_MODEL),
        out_shape=jax.ShapeDtypeStruct((B, T, D_MODEL), F32),
        compiler_params=_params(("parallel", "parallel")),
        name="post",
    )(x, hm, ha, mod3, wts["g2"], wts["w_out"], wts["w_up"], wts["w_down"])


def _prepare_weights(norm1_g, norm2_g, w_in, mlstm_gate_b, q_lora_g, kv_lora_g, w_q_up, w_kv_up,
                     q_head_g, k_head_g, w_out, w_mlp_up, w_mlp_down):
    o_g = 4 * M_WIDTH
    o_q = o_g + N_GATES
    o_kv = o_q + Q_LORA
    o_kr = o_kv + KV_LORA
    half = A_ROPE // 2
    n_dh = 2 * M_HEADS
    wt = w_in.T.astype(BF16)
    w_gate = wt[o_g:o_q].reshape(2, 2, M_HEADS, D_MODEL)
    bias = mlstm_gate_b.reshape(2, 2, M_HEADS)

    def rot_partner(a):
        z = jnp.zeros(a.shape[:-1] + (A_NOPE,), a.dtype)
        return jnp.concatenate([z, a[..., A_NOPE + half:A_QK], a[..., A_NOPE:A_NOPE + half]], axis=-1)

    pad_tile = lambda a: jnp.pad(a, [(0, 0)] * (a.ndim - 1) + [(0, HEAD_PAD - A_QK)])
    w_kr = wt[o_kr:o_kr + A_ROPE]
    w_kr_partner = jnp.concatenate([w_kr[half:], w_kr[:half]], axis=0)
    w_lat = jnp.concatenate([wt[o_q:o_kr], w_gate[:, 0].reshape(n_dh, D_MODEL), w_gate[:, 1].reshape(n_dh, D_MODEL),
                             jnp.zeros((A_NOPE - 2 * n_dh, D_MODEL), BF16), w_kr, w_kr_partner], axis=0)
    gate_bias = jnp.pad(jnp.stack([bias[:, 0, :].reshape(n_dh), bias[:, 1, :].reshape(n_dh)], axis=0),
                        ((0, 0), (0, LANES - n_dh)))
    w_q3 = w_q_up.reshape(Q_LORA, A_HEADS, A_QK)
    w_q = pad_tile(w_q3).reshape(Q_LORA, QK_PAD)
    w_q_partner = pad_tile(rot_partner(w_q3)).reshape(Q_LORA, QK_PAD)
    w_kv3 = w_kv_up.reshape(KV_LORA, A_HEADS, A_NOPE + A_VDIM)
    w_k = jnp.pad(w_kv3[:, :, :A_NOPE], ((0, 0), (0, 0), (0, HEAD_PAD - A_NOPE)))
    w_v = w_kv3[:, :, A_NOPE:]
    w_kv = jnp.concatenate([w_k.reshape(KV_LORA, QK_PAD), w_v.reshape(KV_LORA, A_WIDTH)], axis=1)
    pad_head = lambda g: jnp.stack([pad_tile(g), pad_tile(rot_partner(g))], axis=0)
    return {
        "g1": norm1_g.reshape(1, D_MODEL),
        "g2": norm2_g.reshape(1, D_MODEL),
        "w_main": jnp.concatenate([wt[0:M_WIDTH], wt[2 * M_WIDTH:o_g]], axis=0),
        "w_kt": wt[M_WIDTH:2 * M_WIDTH],
        "w_lat": w_lat,
        "gate_bias": gate_bias,
        "q_lora_g": q_lora_g.reshape(1, Q_LORA),
        "kv_lora_g": kv_lora_g.reshape(1, KV_LORA),
        "w_q": w_q.astype(BF16),
        "w_q_rot": jnp.concatenate([w_q, w_q_partner], axis=1).astype(BF16),
        "w_kv": w_kv.astype(BF16),
        "q_head_g": pad_head(q_head_g),
        "k_head_g": pad_head(k_head_g),
        "w_out_f32": w_out,
        "w_up_f32": w_mlp_up,
        "w_down_f32": w_mlp_down,
    }


def _rope_tables(T):
    rows = T // GRID_W
    row = np.repeat(np.arange(rows, dtype=np.float32), GRID_W)
    col = np.tile(np.arange(GRID_W, dtype=np.float32), rows)
    half = A_ROPE // 2
    inv = (np.float32(ROPE_BASE) ** (-np.arange(0, half, 2, dtype=np.float32) / np.float32(half))).astype(np.float32)
    ang = np.concatenate([row[:, None] * inv, col[:, None] * inv], axis=-1)
    cos, sin = np.cos(ang), np.sin(ang)
    ones = np.ones((T, A_NOPE), np.float32)
    z = lambda w: np.zeros((T, w), np.float32)
    tail = LANES - A_QK
    cos_t = np.concatenate([ones, cos, cos, z(tail)], axis=1)
    sin_t = np.concatenate([z(A_NOPE), -sin, sin, z(tail)], axis=1)
    return jnp.asarray(np.stack([cos_t, sin_t], axis=0).astype(np.float32))


def _layer_pass(x, mod3, mod_row0, mod_per_batch, wts, norm_g, rope_tab, init_state, ctx_kv, is_context):
    pre = _pre(x, mod3, mod_row0, mod_per_batch, wts, rope_tab, emit_cache=is_context)
    mq, mkt, mv, mo, stats, q, k, v = pre[:8]
    if is_context:
        wts = dict(wts, w_out=pre[10], w_up=pre[11], w_down=pre[12])
    ml = _mlstm(mq, mkt, mv, mo, stats, norm_g, init_state, emit_state=is_context,
                heads=M_HEADS if is_context else 1, seqs=MLSTM_CONTEXT_SEQS if is_context else 1)
    ha = _attn(q, k, v, ctx_kv, seqs=ATTN_CONTEXT_SEQS if is_context else 1)
    y = _post(x, ml[0], ha, mod3, mod_row0, mod_per_batch, wts)
    return y, pre[8:10], ml[1:], wts


def kernel(x_prompt, x_sample, cache_mla_ckv, cache_mla_krope, state_mlstm_C, state_mlstm_n, state_mlstm_m,
           c, c_ctx, norm1_g, norm2_g, w_ada, b_ada, w_in, mlstm_gate_b, mlstm_norm_g,
           q_lora_g, kv_lora_g, w_q_up, w_kv_up, q_head_g, k_head_g, w_out, w_mlp_up, w_mlp_down):
    depth = w_in.shape[0]
    Bd = x_sample.shape[0]
    cond8 = jnp.concatenate([c_ctx[None, :], c, jnp.zeros((8 - 1 - Bd, D_MODEL), F32)], axis=0)
    rope_tab = _rope_tables(x_sample.shape[1])

    y, z = x_prompt, x_sample
    ckvs, kropes, Cs, ns, ms = [], [], [], [], []
    for l in range(depth):
        wts = _prepare_weights(norm1_g[l], norm2_g[l], w_in[l], mlstm_gate_b[l], q_lora_g[l], kv_lora_g[l],
                               w_q_up[l], w_kv_up[l], q_head_g[l], k_head_g[l], w_out[l], w_mlp_up[l],
                               w_mlp_down[l])
        norm_g = mlstm_norm_g[l].reshape(M_HEADS, 1, M_HEAD_DIM)
        mod3 = _ada(cond8, w_ada[l], b_ada[l])
        y, (ckv, krope), (C_new, n_new, m_new), wts = _layer_pass(
            y, mod3, 0, 0, wts, norm_g, None, None, None, True)
        ckvs.append(ckv)
        kropes.append(krope)
        Cs.append(C_new)
        ns.append(n_new[:, :, :, 0, :])
        ms.append(m_new[:, :, :, 0].transpose(0, 2, 1))

        init_state = (state_mlstm_C[:, l],
                      state_mlstm_n[:, l][:, :, :, None, :],
                      jnp.broadcast_to(state_mlstm_m[:, l].transpose(0, 2, 1)[..., None],
                                       (Bd, M_HEADS, 2, LANES)))
        krope_placed = jnp.pad(cache_mla_krope[:, l], ((0, 0), (0, 0), (A_NOPE, LANES - A_QK)))
        ctx = (cache_mla_ckv[:, l], krope_placed, wts["w_kv"], wts["k_head_g"])
        z, _, _, _ = _layer_pass(z, mod3, 1, 1, wts, norm_g, rope_tab, init_state, ctx, False)

    return (y, z, jnp.stack(ckvs, axis=1), jnp.stack(kropes, axis=1), jnp.stack(Cs, axis=1),
            jnp.stack(ns, axis=1), jnp.stack(ms, axis=1))
```

```python
import functools

import jax
import jax.numpy as jnp
import numpy as np
from jax import lax
from jax.experimental import pallas as pl
from jax.experimental.pallas import tpu as pltpu

F32 = jnp.float32
BF16 = jnp.bfloat16

D_MODEL = 1024
GRID_W = 64
M_HEADS = 4
M_HEAD_DIM = 128
M_WIDTH = M_HEADS * M_HEAD_DIM
M_BLOCK = 256
A_HEADS = 8
A_NOPE = 64
A_ROPE = 32
A_QK = A_NOPE + A_ROPE
A_VDIM = 64
A_WIDTH = A_HEADS * A_VDIM
Q_LORA = 384
KV_LORA = 256
ROPE_BASE = 10000.0
D_FF = 4 * D_MODEL
EPS = 1e-6

LANES = 128
SUBLANES = 8
LOG2E = 1.4426950408889634
HEAD_PAD = LANES
QK_PAD = A_HEADS * HEAD_PAD
N_GATES = 4 * M_HEADS
LAT_WIDTH = Q_LORA + KV_LORA + LANES
VMEM_LIMIT = 56 * 1024 * 1024

TOKEN_TILE = 512
POST_TILE = 512
MLSTM_CONTEXT_SEQS = 4
ATTN_CONTEXT_SEQS = 8
Q_TILE = 1024
Q_SUBTILE = 256
ADA_TILE_N = 1536
FF_TILE = 1024


def _dot(a, b):
    return jnp.dot(a, b, preferred_element_type=F32)


def _dot_nt(a, b):
    return lax.dot_general(a, b, (((1,), (1,)), ((), ())), preferred_element_type=F32)


def _dot_tn(a, b):
    return lax.dot_general(a, b, (((0,), (0,)), ((), ())), preferred_element_type=F32)


def _rms(x, g):
    y = x * lax.rsqrt(jnp.mean(x * x, axis=-1, keepdims=True) + EPS)
    return y * g


def _params(sem):
    return pltpu.CompilerParams(dimension_semantics=sem, vmem_limit_bytes=VMEM_LIMIT)


def _const_spec(shape):
    zeros = (0,) * len(shape)
    return pl.BlockSpec(shape, lambda *_: zeros, pipeline_mode=pl.Buffered(1))


def _ada_kernel(cond_ref, w_ref, b_ref, o_ref):
    c = cond_ref[...]
    s = (c * jax.nn.sigmoid(c)).astype(BF16)
    o_ref[...] = _dot(s, w_ref[...].astype(BF16)) + b_ref[...]


def _ada(cond8, w_ada, b_ada):
    n = w_ada.shape[1]
    return pl.pallas_call(
        _ada_kernel,
        grid=(n // ADA_TILE_N,),
        in_specs=[
            pl.BlockSpec((8, D_MODEL), lambda j: (0, 0)),
            pl.BlockSpec((D_MODEL, ADA_TILE_N), lambda j: (0, j)),
            pl.BlockSpec((1, ADA_TILE_N), lambda j: (0, j)),
        ],
        out_specs=pl.BlockSpec((8, ADA_TILE_N), lambda j: (0, j)),
        out_shape=jax.ShapeDtypeStruct((8, n), F32),
        compiler_params=_params(("parallel",)),
        name="ada",
    )(cond8, w_ada, b_ada.reshape(1, n))


def _write_heads(src, extra, g_pad, rot, dst_ref):
    for h in range(A_HEADS):
        sl = slice(h * HEAD_PAD, (h + 1) * HEAD_PAD)
        xh = src[:, sl]
        if extra is not None:
            xh = xh + extra
        ss = jnp.sum(xh * xh, axis=-1, keepdims=True) * (1.0 / A_QK)
        r = lax.rsqrt(ss + EPS)
        if rot is None:
            y = xh * r * g_pad
        else:
            partner, cos_g, sin_g = rot
            ph = partner if partner.shape[1] == HEAD_PAD else partner[:, sl]
            y = (xh * cos_g + ph * sin_g) * r
        dst_ref[:, sl] = y.astype(dst_ref.dtype)


def _time_scan(x, op, identity, reverse):
    n = x.shape[0]
    row = lax.broadcasted_iota(jnp.int32, x.shape, 0)
    shift = 1
    while shift < n:
        if shift < SUBLANES:
            if reverse:
                moved = jnp.where(row < n - shift, pltpu.roll(x, n - shift, 0), identity)
            else:
                moved = jnp.where(row >= shift, pltpu.roll(x, shift, 0), identity)
        else:
            fill = jnp.full((shift, x.shape[1]), identity, x.dtype)
            moved = (jnp.concatenate([x[shift:], fill], axis=0) if reverse
                     else jnp.concatenate([fill, x[:n - shift]], axis=0))
        x = op(x, moved)
        shift *= 2
    return x


def _mod_row(mod_ref, row0, per_batch):
    if per_batch:
        return mod_ref[pl.ds(row0 + pl.program_id(0) * per_batch, 1), :]
    return mod_ref[row0:row0 + 1, :]


def _pre_kernel(has_rope, emit_cache, mod_row0, mod_per_batch, *refs):
    (x_ref, mod_ref, g1_ref, wmain_ref, wkt_ref, wlat_ref, gbias_ref, qlg_ref, kvg_ref, wq_ref, wkv_ref,
     qhg_ref, khg_ref) = refs[:13]
    pos = 13
    if has_rope:
        rope_ref = refs[pos]
        pos += 1
    if emit_cache:
        cast_in = refs[pos:pos + 3]
        pos += 3
    (mq_ref, mkt_ref, mv_ref, mo_ref, stats_ref, q_ref, k_ref, v_ref) = refs[pos:pos + 8]
    pos += 8

    x = x_ref[...]
    mod = _mod_row(mod_ref, mod_row0, mod_per_batch)
    sh1 = mod[:, 0:D_MODEL]
    sc1 = mod[:, D_MODEL:2 * D_MODEL]
    h = _rms(x, g1_ref[...]) * (1.0 + sc1) + sh1
    hb = h.astype(BF16)

    plat = _dot_nt(hb, wlat_ref[...])
    q_lat = plat[:, 0:Q_LORA]
    kv_lat = plat[:, Q_LORA:Q_LORA + KV_LORA]
    tail = plat[:, Q_LORA + KV_LORA:LAT_WIDTH]
    tail_f = pltpu.roll(tail, LANES - 2 * M_HEADS, 1)
    tail2 = pltpu.roll(tail, LANES - A_ROPE, 1)

    lane = lax.broadcasted_iota(jnp.int32, (1, LANES), 1)
    fwd = lane < M_HEADS
    gate_i = tail + gbias_ref[0:1, :]
    gate_f = tail_f + gbias_ref[1:2, :]
    log_f = jnp.minimum(gate_f, 0.0) - jnp.log1p(jnp.exp(-jnp.abs(gate_f)))
    def block_scan(v, op, identity):
        parts = [v[r:r + M_BLOCK] for r in range(0, v.shape[0], M_BLOCK)]
        return jnp.where(fwd, jnp.concatenate([_time_scan(p, op, identity, False) for p in parts], axis=0),
                         jnp.concatenate([_time_scan(p, op, identity, True) for p in parts], axis=0))

    b = block_scan(log_f, jnp.add, 0.0)
    a = gate_i - b
    amax = block_scan(a, jnp.maximum, -jnp.inf)
    stats_ref[0] = b.T[0:2 * M_HEADS, :]
    stats_ref[1] = a.T[0:2 * M_HEADS, :]
    stats_ref[2] = amax.T[0:2 * M_HEADS, :]

    krope_placed = jnp.where((lane >= A_NOPE) & (lane < A_QK), tail, 0.0)
    ckv = _rms(kv_lat, kvg_ref[...])
    qn = _rms(q_lat, qlg_ref[...])
    qf = _dot(qn.astype(BF16), wq_ref[...])
    kvf = _dot(ckv.astype(BF16), wkv_ref[...])
    v_ref[...] = kvf[:, QK_PAD:QK_PAD + A_WIDTH].astype(BF16)
    q_rot = k_rot = None
    qhg = qhg_ref[...] * (A_QK ** -0.5 * LOG2E)
    if has_rope:
        cos_t, sin_t = rope_ref[0], rope_ref[1]
        q_rot = (qf[:, QK_PAD:2 * QK_PAD], cos_t * qhg[0:1, :], sin_t * qhg[1:2, :])
        k_rot = (tail2, cos_t * khg_ref[0:1, :], sin_t * khg_ref[1:2, :])
    _write_heads(qf, None, qhg[0:1, :], q_rot, q_ref)
    _write_heads(kvf, krope_placed, khg_ref[0:1, :], k_rot, k_ref)

    if emit_cache:
        ckv_ref, krope_ref = refs[pos:pos + 2]
        ckv_ref[...] = ckv
        krope_ref[...] = tail[:, A_NOPE:A_QK]
        for src_ref, dst_ref in zip(cast_in, refs[pos + 2:pos + 5]):
            dst_ref[...] = src_ref[...].astype(BF16)

    pm = _dot_nt(hb, wmain_ref[...])
    mq_ref[...] = pm[:, 0:M_WIDTH].astype(BF16)
    mv_ref[...] = pm[:, M_WIDTH:2 * M_WIDTH].astype(BF16)
    mo_ref[...] = pm[:, 2 * M_WIDTH:3 * M_WIDTH]
    mkt_ref[...] = _dot_nt(wkt_ref[...], hb) * (M_HEAD_DIM ** -0.5)


def _pre(x, mod3, mod_row0, mod_per_batch, wts, rope_tab, emit_cache):
    shape = x.shape
    tm = TOKEN_TILE
    if shape[1] < tm:
        assert not mod_per_batch and tm % shape[1] == 0
        x = x.reshape(-1, tm, shape[2])
    B, T, _ = x.shape
    has_rope = rope_tab is not None
    tok = lambda w: pl.BlockSpec((None, tm, w), lambda b, i: (b, i, 0))
    in_specs = [
        tok(D_MODEL),
        _const_spec((8, 6 * D_MODEL)),
        _const_spec((1, D_MODEL)),
        _const_spec((3 * M_WIDTH, D_MODEL)),
        _const_spec((M_WIDTH, D_MODEL)),
        _const_spec((LAT_WIDTH, D_MODEL)),
        _const_spec((2, LANES)),
        _const_spec((1, Q_LORA)),
        _const_spec((1, KV_LORA)),
        _const_spec((Q_LORA, 2 * QK_PAD if has_rope else QK_PAD)),
        _const_spec((KV_LORA, QK_PAD + A_WIDTH)),
        _const_spec((2, HEAD_PAD)),
        _const_spec((2, HEAD_PAD)),
    ]
    assert tm % M_BLOCK == 0
    args = [x, mod3, wts["g1"], wts["w_main"], wts["w_kt"], wts["w_lat"], wts["gate_bias"], wts["q_lora_g"],
            wts["kv_lora_g"], wts["w_q_rot"] if has_rope else wts["w_q"], wts["w_kv"],
            wts["q_head_g"], wts["k_head_g"]]
    if has_rope:
        in_specs.append(pl.BlockSpec((2, tm, HEAD_PAD), lambda b, i: (0, i, 0)))
        args.append(rope_tab)
    out_specs = [tok(M_WIDTH),
                 pl.BlockSpec((None, M_WIDTH, tm), lambda b, i: (b, 0, i)),
                 tok(M_WIDTH), tok(M_WIDTH),
                 pl.BlockSpec((None, 3, 2 * M_HEADS, tm), lambda b, i: (b, 0, 0, i)),
                 tok(QK_PAD), tok(QK_PAD), tok(A_WIDTH)]
    out_shape = [
        jax.ShapeDtypeStruct((B, T, M_WIDTH), BF16),
        jax.ShapeDtypeStruct((B, M_WIDTH, T), F32),
        jax.ShapeDtypeStruct((B, T, M_WIDTH), BF16),
        jax.ShapeDtypeStruct((B, T, M_WIDTH), F32),
        jax.ShapeDtypeStruct((B, 3, 2 * M_HEADS, T), F32),
        jax.ShapeDtypeStruct((B, T, QK_PAD), BF16),
        jax.ShapeDtypeStruct((B, T, QK_PAD), BF16),
        jax.ShapeDtypeStruct((B, T, A_WIDTH), BF16),
    ]
    if emit_cache:
        out_specs += [tok(KV_LORA), tok(A_ROPE)]
        out_shape += [jax.ShapeDtypeStruct((B, T, KV_LORA), F32),
                      jax.ShapeDtypeStruct((B, T, A_ROPE), F32)]
        steps, per_b = B * (T // tm), T // tm
        for w in (wts["w_out_f32"], wts["w_up_f32"], wts["w_down_f32"]):
            rows = w.shape[0] // steps
            assert rows * steps == w.shape[0] and rows % (2 * SUBLANES) == 0
            spec = pl.BlockSpec((rows, w.shape[1]), lambda b, i: (b * per_b + i, 0))
            in_specs.append(spec)
            args.append(w)
            out_specs.append(spec)
            out_shape.append(jax.ShapeDtypeStruct(w.shape, BF16))
    outs = pl.pallas_call(
        functools.partial(_pre_kernel, has_rope, emit_cache, mod_row0, mod_per_batch),
        grid=(B, T // tm),
        in_specs=in_specs,
        out_specs=out_specs,
        out_shape=out_shape,
        compiler_params=_params(("parallel", "parallel")),
        name="pre_latent" if has_rope else "pre_context",
    )(*args)
    keep = (1, 4, 10, 11, 12)
    return [o if n in keep else o.reshape(shape[:2] + o.shape[2:]) for n, o in enumerate(outs)]


def _rows_to_lane_broadcast(rows, spread):
    x = jnp.concatenate(rows, axis=0)
    p1 = x.astype(BF16)
    r1 = x - p1.astype(F32)
    p2 = r1.astype(BF16)
    p3 = (r1 - p2.astype(F32)).astype(BF16)
    pad = jnp.zeros((spread.shape[0] - 3 * len(rows), x.shape[1]), BF16)
    return _dot_tn(jnp.concatenate([p1, p2, p3, pad], axis=0), spread)


def _mlstm_gate_rows(b_row, a_row, amax_row, forward, m):
    L = b_row.shape[1]
    last = slice(L - 1, L) if forward else slice(0, 1)
    total = b_row[:, last]
    g_row = jnp.maximum(m, amax_row)
    m_new = total + jnp.maximum(m, amax_row[:, last])
    w_key_row = jnp.exp2((a_row + (total - m_new)) * LOG2E)
    decay = jnp.exp(total + m - m_new)
    return g_row * LOG2E, (b_row + g_row) * LOG2E, a_row * LOG2E, w_key_row, decay, m_new


def _mlstm_block(s_raw, q, kt, v_aug, g2, mt2, a2_row, w_key_row, decay, allow, CN, m):
    w_intra = jnp.exp2(jnp.where(allow, a2_row - jnp.concatenate([g2, g2], axis=1), -jnp.inf))
    w_inter = jnp.exp2(m * LOG2E - g2)
    s = (s_raw * w_intra).astype(BF16)
    nd = _dot(s, v_aug) + jnp.concatenate([w_inter, w_inter], axis=1) * _dot(q, CN.astype(BF16))
    num, den = nd[:, 0:M_HEAD_DIM], nd[:, M_HEAD_DIM:2 * M_HEAD_DIM]
    h = num / jnp.maximum(jnp.abs(den), jnp.exp2(-mt2))
    CN_new = decay * CN + _dot((kt * w_key_row).astype(BF16), v_aug)
    return h, CN_new


def _mlstm_kernel(has_init, emit_state, n_blocks, heads, seqs, group, *refs):
    q_ref, kt_ref, v_ref, mo_ref, stats_ref, ng_ref, spread_ref = refs[:7]
    pos = 7
    if has_init:
        c0_ref, n0_ref, m0_ref = refs[pos:pos + 3]
        pos += 3
    hm_ref = refs[pos]
    pos += 1
    if emit_state:
        c_ref, n_ref, m_ref = refs[pos:pos + 3]
        pos += 3

    L, Dh = M_BLOCK, M_HEAD_DIM
    t_idx = lax.broadcasted_iota(jnp.int32, (L, L), 0)
    s_idx = lax.broadcasted_iota(jnp.int32, (L, L), 1)
    allow = (s_idx <= t_idx, s_idx >= t_idx)
    spread = spread_ref[...]
    ones = jnp.ones((L, Dh), BF16)

    def time_lanes(sq, c):
        start = ((sq % group) * n_blocks + c) * L
        return sq // group, slice(start, start + L)

    def lane_broadcast_n(n_row):
        return jnp.broadcast_to(n_row, (Dh, Dh)).T

    def init_state(sq, j, d):
        if has_init:
            return (jnp.concatenate([c0_ref[sq, d, j], lane_broadcast_n(n0_ref[sq, d, j])], axis=1),
                    m0_ref[sq, j, d:d + 1, 0:1])
        return jnp.zeros((Dh, 2 * Dh), F32), jnp.zeros((1, 1), F32)

    def gate_rows(sq, j, c, d, m):
        head = j if heads == M_HEADS else pl.program_id(1) * heads + j
        r, (g, lanes) = pl.ds(d * M_HEADS + head, 1), time_lanes(sq, c)
        return _mlstm_gate_rows(stats_ref[g, 0, r, lanes], stats_ref[g, 1, r, lanes], stats_ref[g, 2, r, lanes],
                                d == 0, m)

    def blocks(sq, j, jobs, states):
        loaded, rows6, cols_in = {}, [], []
        for (c, d), (CN, m) in zip(jobs, states):
            rows6.append(gate_rows(sq, j, c, d, m))
            cols_in += [rows6[-1][0], rows6[-1][1]]
            if c not in loaded:
                rows, cols = slice(c * L, (c + 1) * L), slice(j * Dh, (j + 1) * Dh)
                g, lanes = time_lanes(sq, c)
                q, kt = q_ref[sq, rows, cols], kt_ref[g, cols, lanes]
                v_aug = jnp.concatenate([v_ref[sq, rows, cols], ones], axis=1)
                loaded[c] = (_dot(q, kt.astype(BF16)), q, kt, v_aug)
        cols_out = _rows_to_lane_broadcast(cols_in, spread)
        hs, new_states = [], []
        for idx, ((c, d), (CN, m)) in enumerate(zip(jobs, states)):
            g2 = cols_out[:, (2 * idx) * LANES:(2 * idx + 1) * LANES]
            mt2 = cols_out[:, (2 * idx + 1) * LANES:(2 * idx + 2) * LANES]
            _, _, a2_row, w_key_row, decay, m_new = rows6[idx]
            h, CN_new = _mlstm_block(*loaded[c], g2, mt2, a2_row, w_key_row, decay, allow[d], CN, m)
            hs.append(h)
            new_states.append((CN_new, m_new))
        return hs, new_states

    def finish(sq, j, rows, hs):
        cols = slice(j * Dh, (j + 1) * Dh)
        hn = _rms(hs, ng_ref[j])
        hm_ref[sq, rows, cols] = (hn * jax.nn.sigmoid(mo_ref[sq, rows, cols])).astype(hm_ref.dtype)

    def emit(sq, j, d, state):
        CN, m = state
        c_ref[sq, d, j] = CN[:, 0:Dh]
        n_ref[sq, d, j] = CN[:, Dh:2 * Dh].T[0:1, :]
        m_ref[sq, j, d:d + 1, :] = jnp.broadcast_to(m, (1, LANES))

    if n_blocks > 1:
        hf_scr, hb_scr = refs[pos:pos + 2]

    for sq, j in [(sq, j) for sq in range(seqs) for j in range(heads)]:
        cols = slice(j * Dh, (j + 1) * Dh)
        states = [init_state(sq, j, 0), init_state(sq, j, 1)]
        if n_blocks == 1:
            (hf, hb), states = blocks(sq, j, [(0, 0), (0, 1)], states)
            finish(sq, j, slice(0, L), hf + hb)
        else:
            for step in range(n_blocks):
                cf, cb = step, n_blocks - 1 - step
                (hf, hb), states = blocks(sq, j, [(cf, 0), (cb, 1)], states)
                hf_scr[cf * L:(cf + 1) * L, cols] = hf
                hb_scr[cb * L:(cb + 1) * L, cols] = hb
            finish(sq, j, slice(None), hf_scr[:, cols] + hb_scr[:, cols])
        if emit_state:
            emit(sq, j, 0, states[0])
            emit(sq, j, 1, states[1])


def _mlstm(mq, mkt, mv, mo, stats, norm_g, init_state, emit_state, heads, seqs):
    B, T, _ = mq.shape
    H, Dh, L = M_HEADS, M_HEAD_DIM, M_BLOCK
    nb = T // L
    w = heads * Dh
    has_init = init_state is not None
    n_rows = 4
    spread = jnp.tile(jnp.repeat(jnp.eye(n_rows, dtype=BF16), LANES, axis=1), (3, 1))
    spread = jnp.pad(spread, ((0, 2 * SUBLANES - 3 * n_rows), (0, 0)))
    tok = pl.BlockSpec((seqs, T, w), lambda b, h: (b, 0, h))
    group = mkt.shape[2] // T
    assert seqs % group == 0
    assert mkt.shape == (B // group, M_WIDTH, group * T) and stats.shape == (B // group, 3, 2 * H, group * T)
    in_specs = [tok, pl.BlockSpec((seqs // group, w, group * T), lambda b, h: (b, h, 0)), tok, tok,
                pl.BlockSpec((seqs // group, 3, 2 * H, group * T), lambda b, h: (b, 0, 0, 0)),
                pl.BlockSpec((heads, 1, Dh), lambda b, h: (h, 0, 0)),
                _const_spec((2 * SUBLANES, n_rows * LANES))]
    args = [mq, mkt, mv, mo, stats, norm_g, spread]
    state_specs = [pl.BlockSpec((seqs, 2, heads, Dh, Dh), lambda b, h: (b, 0, h, 0, 0)),
                   pl.BlockSpec((seqs, 2, heads, 1, Dh), lambda b, h: (b, 0, h, 0, 0)),
                   pl.BlockSpec((seqs, heads, 2, LANES), lambda b, h: (b, h, 0, 0))]
    if has_init:
        in_specs += state_specs
        args += list(init_state)
    out_specs = [tok]
    out_shape = [jax.ShapeDtypeStruct((B, T, M_WIDTH), BF16)]
    if emit_state:
        out_specs += state_specs
        out_shape += [jax.ShapeDtypeStruct((B, 2, H, Dh, Dh), F32),
                      jax.ShapeDtypeStruct((B, 2, H, 1, Dh), F32),
                      jax.ShapeDtypeStruct((B, H, 2, LANES), F32)]
    scratch = [] if nb == 1 else [pltpu.VMEM((T, w), F32), pltpu.VMEM((T, w), F32)]
    return pl.pallas_call(
        functools.partial(_mlstm_kernel, has_init, emit_state, nb, heads, seqs, group),
        grid=(B // seqs, H // heads),
        in_specs=in_specs,
        out_specs=out_specs,
        out_shape=out_shape,
        scratch_shapes=scratch,
        compiler_params=_params(("parallel", "parallel")),
        name="mlstm_latent" if has_init else "mlstm_context",
    )(*args)


def _attn_kernel(has_ctx, seqs, *refs):
    if has_ctx:
        q_ref, k_ref, v_ref, ckv_ref, krp_ref, wkv_ref, khg_ref, o_ref, kc_ref, vc_ref = refs

        @pl.when(pl.program_id(1) == 0)
        def _():
            kvf = _dot(ckv_ref[0].astype(BF16), wkv_ref[...])
            vc_ref[...] = kvf[:, QK_PAD:QK_PAD + A_WIDTH].astype(BF16)
            _write_heads(kvf, krp_ref[0], khg_ref[0:1, :], None, kc_ref)
    else:
        q_ref, k_ref, v_ref, o_ref = refs
    lane = lax.broadcasted_iota(jnp.int32, (1, LANES), 1)
    ones = lambda n: jnp.ones((n, LANES), BF16)
    tq = q_ref.shape[1]
    sub = min(Q_SUBTILE, tq)
    for sq, r0, pair in [(sq, r0, pair) for sq in range(seqs) for r0 in range(0, tq, sub)
                         for pair in range(A_HEADS // 2)]:
        rows = slice(r0, r0 + sub)
        vsl = slice(pair * LANES, (pair + 1) * LANES)
        v_aug = jnp.concatenate([v_ref[sq, :, vsl], ones(v_ref.shape[1])], axis=1)
        if has_ctx:
            vc_aug = jnp.concatenate([vc_ref[:, vsl], ones(vc_ref.shape[0])], axis=1)
        outs = []
        for e in range(2):
            hsl = slice((2 * pair + e) * HEAD_PAD, (2 * pair + e + 1) * HEAD_PAD)
            qh = q_ref[sq, rows, hsl]
            s = _dot_nt(qh, k_ref[sq, :, hsl])
            mx = jnp.max(s, axis=1, keepdims=True)
            if has_ctx:
                sc = _dot_nt(qh, kc_ref[:, hsl])
                mx = jnp.maximum(mx, jnp.max(sc, axis=1, keepdims=True))
            od = _dot(jnp.exp2(s - mx).astype(BF16), v_aug)
            if has_ctx:
                od = od + _dot(jnp.exp2(sc - mx).astype(BF16), vc_aug)
            outs.append(od[:, 0:LANES] / od[:, LANES:2 * LANES])
        o_ref[sq, rows, vsl] = jnp.where(lane < A_VDIM, outs[0], outs[1]).astype(o_ref.dtype)


def _attn(q, k, v, ctx, seqs):
    B, T, _ = q.shape
    tq = min(Q_TILE, T)
    has_ctx = ctx is not None
    full = lambda n, w: pl.BlockSpec((seqs, n, w), lambda b, i: (b, 0, 0))
    in_specs = [pl.BlockSpec((seqs, tq, QK_PAD), lambda b, i: (b, i, 0)), full(T, QK_PAD), full(T, A_WIDTH)]
    args = [q, k, v]
    scratch = []
    if has_ctx:
        assert seqs == 1
        P = ctx[0].shape[1]
        in_specs += [full(P, KV_LORA), full(P, HEAD_PAD), _const_spec((KV_LORA, QK_PAD + A_WIDTH)),
                     _const_spec((2, HEAD_PAD))]
        args += list(ctx)
        scratch = [pltpu.VMEM((P, QK_PAD), BF16), pltpu.VMEM((P, A_WIDTH), BF16)]
    return pl.pallas_call(
        functools.partial(_attn_kernel, has_ctx, seqs),
        grid=(B // seqs, T // tq),
        in_specs=in_specs,
        out_specs=pl.BlockSpec((seqs, tq, A_WIDTH), lambda b, i: (b, i, 0)),
        out_shape=jax.ShapeDtypeStruct((B, T, A_WIDTH), BF16),
        scratch_shapes=scratch,
        compiler_params=_params(("parallel", "arbitrary" if has_ctx else "parallel")),
        name="attn_latent" if has_ctx else "attn_context",
    )(*args)


def _post_kernel(mod_row0, mod_per_batch, x_ref, hm_ref, ha_ref, mod_ref, g2_ref, wout_ref, wup_ref, wdown_ref,
                 y_ref):
    mod = _mod_row(mod_ref, mod_row0, mod_per_batch)
    gate1 = mod[:, 2 * D_MODEL:3 * D_MODEL]
    sh2 = mod[:, 3 * D_MODEL:4 * D_MODEL]
    sc2 = mod[:, 4 * D_MODEL:5 * D_MODEL]
    gate2 = mod[:, 5 * D_MODEL:6 * D_MODEL]
    mix = jnp.concatenate([hm_ref[...], ha_ref[...]], axis=-1)
    x1 = x_ref[...] + gate1 * _dot(mix, wout_ref[...])
    h2 = (_rms(x1, g2_ref[...]) * (1.0 + sc2) + sh2).astype(BF16)
    acc = jnp.zeros_like(x1)
    for c in range(D_FF // FF_TILE):
        sl = slice(c * FF_TILE, (c + 1) * FF_TILE)
        u = jnp.maximum(_dot(h2, wup_ref[:, sl]), 0.0)
        acc = acc + _dot((u * u).astype(BF16), wdown_ref[sl, :])
    y_ref[...] = x1 + gate2 * acc


def _post(x, hm, ha, mod3, mod_row0, mod_per_batch, wts):
    shape = x.shape
    if not mod_per_batch:
        x, hm, ha = (a.reshape(1, -1, a.shape[-1]) for a in (x, hm, ha))
    B, T, _ = x.shape
    tm = POST_TILE
    tok = lambda w: pl.BlockSpec((None, tm, w), lambda b, i: (b, i, 0))
    return _post_call(x, hm, ha, mod3, mod_row0, mod_per_batch, wts, B, T, tm, tok).reshape(shape)


def _post_call(x, hm, ha, mod3, mod_row0, mod_per_batch, wts, B, T, tm, tok):
    return pl.pallas_call(
        functools.partial(_post_kernel, mod_row0, mod_per_batch),
        grid=(B, T // tm),
        in_specs=[tok(D_MODEL), tok(M_WIDTH), tok(A_WIDTH),
                  _const_spec((8, 6 * D_MODEL)),
                  _const_spec((1, D_MODEL)),
                  _const_spec((M_WIDTH + A_WIDTH, D_MODEL)),
                  _const_spec((D_MODEL, D_FF)),
                  _const_spec((D_FF, D_MODEL))],
        out_specs=tok(D_MODEL),
        out_shape=jax.ShapeDtypeStruct((B, T, D_MODEL), F32),
        compiler_params=_params(("parallel", "parallel")),
        name="post",
    )(x, hm, ha, mod3, wts["g2"], wts["w_out"], wts["w_up"], wts["w_down"])


def _prepare_weights(norm1_g, norm2_g, w_in, mlstm_gate_b, q_lora_g, kv_lora_g, w_q_up, w_kv_up,
                     q_head_g, k_head_g, w_out, w_mlp_up, w_mlp_down):
    o_g = 4 * M_WIDTH
    o_q = o_g + N_GATES
    o_kv = o_q + Q_LORA
    o_kr = o_kv + KV_LORA
    half = A_ROPE // 2
    n_dh = 2 * M_HEADS
    wt = w_in.T.astype(BF16)
    w_gate = wt[o_g:o_q].reshape(2, 2, M_HEADS, D_MODEL)
    bias = mlstm_gate_b.reshape(2, 2, M_HEADS)

    def rot_partner(a):
        z = jnp.zeros(a.shape[:-1] + (A_NOPE,), a.dtype)
        return jnp.concatenate([z, a[..., A_NOPE + half:A_QK], a[..., A_NOPE:A_NOPE + half]], axis=-1)

    pad_tile = lambda a: jnp.pad(a, [(0, 0)] * (a.ndim - 1) + [(0, HEAD_PAD - A_QK)])
    w_kr = wt[o_kr:o_kr + A_ROPE]
    w_kr_partner = jnp.concatenate([w_kr[half:], w_kr[:half]], axis=0)
    w_lat = jnp.concatenate([wt[o_q:o_kr], w_gate[:, 0].reshape(n_dh, D_MODEL), w_gate[:, 1].reshape(n_dh, D_MODEL),
                             jnp.zeros((A_NOPE - 2 * n_dh, D_MODEL), BF16), w_kr, w_kr_partner], axis=0)
    gate_bias = jnp.pad(jnp.stack([bias[:, 0, :].reshape(n_dh), bias[:, 1, :].reshape(n_dh)], axis=0),
                        ((0, 0), (0, LANES - n_dh)))
    w_q3 = w_q_up.reshape(Q_LORA, A_HEADS, A_QK)
    w_q = pad_tile(w_q3).reshape(Q_LORA, QK_PAD)
    w_q_partner = pad_tile(rot_partner(w_q3)).reshape(Q_LORA, QK_PAD)
    w_kv3 = w_kv_up.reshape(KV_LORA, A_HEADS, A_NOPE + A_VDIM)
    w_k = jnp.pad(w_kv3[:, :, :A_NOPE], ((0, 0), (0, 0), (0, HEAD_PAD - A_NOPE)))
    w_v = w_kv3[:, :, A_NOPE:]
    w_kv = jnp.concatenate([w_k.reshape(KV_LORA, QK_PAD), w_v.reshape(KV_LORA, A_WIDTH)], axis=1)
    pad_head = lambda g: jnp.stack([pad_tile(g), pad_tile(rot_partner(g))], axis=0)
    return {
        "g1": norm1_g.reshape(1, D_MODEL),
        "g2": norm2_g.reshape(1, D_MODEL),
        "w_main": jnp.concatenate([wt[0:M_WIDTH], wt[2 * M_WIDTH:o_g]], axis=0),
        "w_kt": wt[M_WIDTH:2 * M_WIDTH],
        "w_lat": w_lat,
        "gate_bias": gate_bias,
        "q_lora_g": q_lora_g.reshape(1, Q_LORA),
        "kv_lora_g": kv_lora_g.reshape(1, KV_LORA),
        "w_q": w_q.astype(BF16),
        "w_q_rot": jnp.concatenate([w_q, w_q_partner], axis=1).astype(BF16),
        "w_kv": w_kv.astype(BF16),
        "q_head_g": pad_head(q_head_g),
        "k_head_g": pad_head(k_head_g),
        "w_out_f32": w_out,
        "w_up_f32": w_mlp_up,
        "w_down_f32": w_mlp_down,
    }


def _rope_tables(T):
    rows = T // GRID_W
    row = np.repeat(np.arange(rows, dtype=np.float32), GRID_W)
    col = np.tile(np.arange(GRID_W, dtype=np.float32), rows)
    half = A_ROPE // 2
    inv = (np.float32(ROPE_BASE) ** (-np.arange(0, half, 2, dtype=np.float32) / np.float32(half))).astype(np.float32)
    ang = np.concatenate([row[:, None] * inv, col[:, None] * inv], axis=-1)
    cos, sin = np.cos(ang), np.sin(ang)
    ones = np.ones((T, A_NOPE), np.float32)
    z = lambda w: np.zeros((T, w), np.float32)
    tail = LANES - A_QK
    cos_t = np.concatenate([ones, cos, cos, z(tail)], axis=1)
    sin_t = np.concatenate([z(A_NOPE), -sin, sin, z(tail)], axis=1)
    return jnp.asarray(np.stack([cos_t, sin_t], axis=0).astype(np.float32))


def _layer_pass(x, mod3, mod_row0, mod_per_batch, wts, norm_g, rope_tab, init_state, ctx_kv, is_context):
    pre = _pre(x, mod3, mod_row0, mod_per_batch, wts, rope_tab, emit_cache=is_context)
    mq, mkt, mv, mo, stats, q, k, v = pre[:8]
    if is_context:
        wts = dict(wts, w_out=pre[10], w_up=pre[11], w_down=pre[12])
    ml = _mlstm(mq, mkt, mv, mo, stats, norm_g, init_state, emit_state=is_context,
                heads=M_HEADS if is_context else 1, seqs=MLSTM_CONTEXT_SEQS if is_context else 1)
    ha = _attn(q, k, v, ctx_kv, seqs=ATTN_CONTEXT_SEQS if is_context else 1)
    y = _post(x, ml[0], ha, mod3, mod_row0, mod_per_batch, wts)
    return y, pre[8:10], ml[1:], wts


def kernel(x_prompt, x_sample, cache_mla_ckv, cache_mla_krope, state_mlstm_C, state_mlstm_n, state_mlstm_m,
           c, c_ctx, norm1_g, norm2_g, w_ada, b_ada, w_in, mlstm_gate_b, mlstm_norm_g,
           q_lora_g, kv_lora_g, w_q_up, w_kv_up, q_head_g, k_head_g, w_out, w_mlp_up, w_mlp_down):
    depth = w_in.shape[0]
    Bd = x_sample.shape[0]
    cond8 = jnp.concatenate([c_ctx[None, :], c, jnp.zeros((8 - 1 - Bd, D_MODEL), F32)], axis=0)
    rope_tab = _rope_tables(x_sample.shape[1])

    y, z = x_prompt, x_sample
    ckvs, kropes, Cs, ns, ms = [], [], [], [], []
    for l in range(depth):
        wts = _prepare_weights(norm1_g[l], norm2_g[l], w_in[l], mlstm_gate_b[l], q_lora_g[l], kv_lora_g[l],
                               w_q_up[l], w_kv_up[l], q_head_g[l], k_head_g[l], w_out[l], w_mlp_up[l],
                               w_mlp_down[l])
        norm_g = mlstm_norm_g[l].reshape(M_HEADS, 1, M_HEAD_DIM)
        mod3 = _ada(cond8, w_ada[l], b_ada[l])
        y, (ckv, krope), (C_new, n_new, m_new), wts = _layer_pass(
            y, mod3, 0, 0, wts, norm_g, None, None, None, True)
        ckvs.append(ckv)
        kropes.append(krope)
        Cs.append(C_new)
        ns.append(n_new[:, :, :, 0, :])
        ms.append(m_new[:, :, :, 0].transpose(0, 2, 1))

        init_state = (state_mlstm_C[:, l],
                      state_mlstm_n[:, l][:, :, :, None, :],
                      jnp.broadcast_to(state_mlstm_m[:, l].transpose(0, 2, 1)[..., None],
                                       (Bd, M_HEADS, 2, LANES)))
        krope_placed = jnp.pad(cache_mla_krope[:, l], ((0, 0), (0, 0), (A_NOPE, LANES - A_QK)))
        ctx = (cache_mla_ckv[:, l], krope_placed, wts["w_kv"], wts["k_head_g"])
        z, _, _, _ = _layer_pass(z, mod3, 1, 1, wts, norm_g, rope_tab, init_state, ctx, False)

    return (y, z, jnp.stack(ckvs, axis=1), jnp.stack(kropes, axis=1), jnp.stack(Cs, axis=1),
            jnp.stack(ns, axis=1), jnp.stack(ms, axis=1))
```

```python
import functools

import jax
import jax.numpy as jnp
import numpy as np
from jax import lax
from jax.experimental import pallas as pl
from jax.experimental.pallas import tpu as pltpu

F32 = jnp.float32
BF16 = jnp.bfloat16

D_MODEL = 1024
GRID_W = 64
M_HEADS = 4
M_HEAD_DIM = 128
M_WIDTH = M_HEADS * M_HEAD_DIM
M_BLOCK = 256
A_HEADS = 8
A_NOPE = 64
A_ROPE = 32
A_QK = A_NOPE + A_ROPE
A_VDIM = 64
A_WIDTH = A_HEADS * A_VDIM
Q_LORA = 384
KV_LORA = 256
ROPE_BASE = 10000.0
D_FF = 4 * D_MODEL
EPS = 1e-6

LANES = 128
SUBLANES = 8
LOG2E = 1.4426950408889634
HEAD_PAD = LANES
QK_PAD = A_HEADS * HEAD_PAD
N_GATES = 4 * M_HEADS
LAT_WIDTH = Q_LORA + KV_LORA + LANES
VMEM_LIMIT = 56 * 1024 * 1024

TOKEN_TILE = 512
POST_TILE = 512
MLSTM_CONTEXT_SEQS = 4
ATTN_CONTEXT_SEQS = 8
Q_TILE = 1024
Q_SUBTILE = 256
ADA_TILE_N = 1536
FF_TILE = 1024


def _dot(a, b):
    return jnp.dot(a, b, preferred_element_type=F32)


def _dot_nt(a, b):
    return lax.dot_general(a, b, (((1,), (1,)), ((), ())), preferred_element_type=F32)


def _dot_tn(a, b):
    return lax.dot_general(a, b, (((0,), (0,)), ((), ())), preferred_element_type=F32)


def _rms(x, g):
    y = x * lax.rsqrt(jnp.mean(x * x, axis=-1, keepdims=True) + EPS)
    return y * g


def _params(sem):
    return pltpu.CompilerParams(dimension_semantics=sem, vmem_limit_bytes=VMEM_LIMIT)


def _const_spec(shape):
    zeros = (0,) * len(shape)
    return pl.BlockSpec(shape, lambda *_: zeros, pipeline_mode=pl.Buffered(1))


def _ada_kernel(cond_ref, w_ref, b_ref, o_ref):
    c = cond_ref[...]
    s = (c * jax.nn.sigmoid(c)).astype(BF16)
    o_ref[...] = _dot(s, w_ref[...].astype(BF16)) + b_ref[...]


def _ada(cond8, w_ada, b_ada):
    n = w_ada.shape[1]
    return pl.pallas_call(
        _ada_kernel,
        grid=(n // ADA_TILE_N,),
        in_specs=[
            pl.BlockSpec((8, D_MODEL), lambda j: (0, 0)),
            pl.BlockSpec((D_MODEL, ADA_TILE_N), lambda j: (0, j)),
            pl.BlockSpec((1, ADA_TILE_N), lambda j: (0, j)),
        ],
        out_specs=pl.BlockSpec((8, ADA_TILE_N), lambda j: (0, j)),
        out_shape=jax.ShapeDtypeStruct((8, n), F32),
        compiler_params=_params(("parallel",)),
        name="ada",
    )(cond8, w_ada, b_ada.reshape(1, n))


def _write_heads(src, extra, g_pad, rot, dst_ref):
    for h in range(A_HEADS):
        sl = slice(h * HEAD_PAD, (h + 1) * HEAD_PAD)
        xh = src[:, sl]
        if extra is not None:
            xh = xh + extra
        ss = jnp.sum(xh * xh, axis=-1, keepdims=True) * (1.0 / A_QK)
        r = lax.rsqrt(ss + EPS)
        if rot is None:
            y = xh * r * g_pad
        else:
            partner, cos_g, sin_g = rot
            ph = partner if partner.shape[1] == HEAD_PAD else partner[:, sl]
            y = (xh * cos_g + ph * sin_g) * r
        dst_ref[:, sl] = y.astype(dst_ref.dtype)


def _time_scan(x, op, identity, reverse):
    n = x.shape[0]
    row = lax.broadcasted_iota(jnp.int32, x.shape, 0)
    shift = 1
    while shift < n:
        if shift < SUBLANES:
            if reverse:
                moved = jnp.where(row < n - shift, pltpu.roll(x, n - shift, 0), identity)
            else:
                moved = jnp.where(row >= shift, pltpu.roll(x, shift, 0), identity)
        else:
            fill = jnp.full((shift, x.shape[1]), identity, x.dtype)
            moved = (jnp.concatenate([x[shift:], fill], axis=0) if reverse
                     else jnp.concatenate([fill, x[:n - shift]], axis=0))
        x = op(x, moved)
        shift *= 2
    return x


def _mod_row(mod_ref, row0, per_batch):
    if per_batch:
        return mod_ref[pl.ds(row0 + pl.program_id(0) * per_batch, 1), :]
    return mod_ref[row0:row0 + 1, :]


def _pre_kernel(has_rope, emit_cache, mod_row0, mod_per_batch, *refs):
    (x_ref, mod_ref, g1_ref, wmain_ref, wkt_ref, wlat_ref, gbias_ref, qlg_ref, kvg_ref, wq_ref, wkv_ref,
     qhg_ref, khg_ref) = refs[:13]
    pos = 13
    if has_rope:
        rope_ref = refs[pos]
        pos += 1
    if emit_cache:
        cast_in = refs[pos:pos + 3]
        pos += 3
    (mq_ref, mkt_ref, mv_ref, mo_ref, stats_ref, q_ref, k_ref, v_ref) = refs[pos:pos + 8]
    pos += 8

    x = x_ref[...]
    mod = _mod_row(mod_ref, mod_row0, mod_per_batch)
    sh1 = mod[:, 0:D_MODEL]
    sc1 = mod[:, D_MODEL:2 * D_MODEL]
    h = _rms(x, g1_ref[...]) * (1.0 + sc1) + sh1
    hb_ref = refs[-1]
    hb_ref[...] = h.astype(BF16)

    plat = _dot_nt(hb_ref[...], wlat_ref[...])
    q_lat = plat[:, 0:Q_LORA]
    kv_lat = plat[:, Q_LORA:Q_LORA + KV_LORA]
    tail = plat[:, Q_LORA + KV_LORA:LAT_WIDTH]
    tail_f = pltpu.roll(tail, LANES - 2 * M_HEADS, 1)
    tail2 = pltpu.roll(tail, LANES - A_ROPE, 1)

    lane = lax.broadcasted_iota(jnp.int32, (1, LANES), 1)
    fwd = lane < M_HEADS
    gate_i = tail + gbias_ref[0:1, :]
    gate_f = tail_f + gbias_ref[1:2, :]
    log_f = jnp.minimum(gate_f, 0.0) - jnp.log1p(jnp.exp(-jnp.abs(gate_f)))
    def block_scan(v, op, identity):
        parts = [v[r:r + M_BLOCK] for r in range(0, v.shape[0], M_BLOCK)]
        return jnp.where(fwd, jnp.concatenate([_time_scan(p, op, identity, False) for p in parts], axis=0),
                         jnp.concatenate([_time_scan(p, op, identity, True) for p in parts], axis=0))

    b = block_scan(log_f, jnp.add, 0.0)
    a = gate_i - b
    amax = block_scan(a, jnp.maximum, -jnp.inf)
    stats_ref[0] = b.T[0:2 * M_HEADS, :]
    stats_ref[1] = a.T[0:2 * M_HEADS, :]
    stats_ref[2] = amax.T[0:2 * M_HEADS, :]

    krope_placed = jnp.where((lane >= A_NOPE) & (lane < A_QK), tail, 0.0)
    ckv = _rms(kv_lat, kvg_ref[...])
    qn = _rms(q_lat, qlg_ref[...])
    qf = _dot(qn.astype(BF16), wq_ref[...])
    kvf = _dot(ckv.astype(BF16), wkv_ref[...])
    v_ref[...] = kvf[:, QK_PAD:QK_PAD + A_WIDTH].astype(BF16)
    q_rot = k_rot = None
    qhg = qhg_ref[...] * (A_QK ** -0.5 * LOG2E)
    if has_rope:
        cos_t, sin_t = rope_ref[0], rope_ref[1]
        q_rot = (qf[:, QK_PAD:2 * QK_PAD], cos_t * qhg[0:1, :], sin_t * qhg[1:2, :])
        k_rot = (tail2, cos_t * khg_ref[0:1, :], sin_t * khg_ref[1:2, :])
    _write_heads(qf, None, qhg[0:1, :], q_rot, q_ref)
    _write_heads(kvf, krope_placed, khg_ref[0:1, :], k_rot, k_ref)

    if emit_cache:
        ckv_ref, krope_ref = refs[pos:pos + 2]
        ckv_ref[...] = ckv
        krope_ref[...] = tail[:, A_NOPE:A_QK]
        for src_ref, dst_ref in zip(cast_in, refs[pos + 2:pos + 5]):
            dst_ref[...] = src_ref[...].astype(BF16)

    pm = _dot_nt(hb_ref[...], wmain_ref[...])
    mq_ref[...] = pm[:, 0:M_WIDTH].astype(BF16)
    mv_ref[...] = pm[:, M_WIDTH:2 * M_WIDTH].astype(BF16)
    mo_ref[...] = pm[:, 2 * M_WIDTH:3 * M_WIDTH]
    mkt_ref[...] = _dot_nt(wkt_ref[...], hb_ref[...]) * (M_HEAD_DIM ** -0.5)


def _pre(x, mod3, mod_row0, mod_per_batch, wts, rope_tab, emit_cache):
    shape = x.shape
    tm = TOKEN_TILE
    if shape[1] < tm:
        assert not mod_per_batch and tm % shape[1] == 0
        x = x.reshape(-1, tm, shape[2])
    B, T, _ = x.shape
    has_rope = rope_tab is not None
    tok = lambda w: pl.BlockSpec((None, tm, w), lambda b, i: (b, i, 0))
    in_specs = [
        tok(D_MODEL),
        _const_spec((8, 6 * D_MODEL)),
        _const_spec((1, D_MODEL)),
        _const_spec((3 * M_WIDTH, D_MODEL)),
        _const_spec((M_WIDTH, D_MODEL)),
        _const_spec((LAT_WIDTH, D_MODEL)),
        _const_spec((2, LANES)),
        _const_spec((1, Q_LORA)),
        _const_spec((1, KV_LORA)),
        _const_spec((Q_LORA, 2 * QK_PAD if has_rope else QK_PAD)),
        _const_spec((KV_LORA, QK_PAD + A_WIDTH)),
        _const_spec((2, HEAD_PAD)),
        _const_spec((2, HEAD_PAD)),
    ]
    assert tm % M_BLOCK == 0
    args = [x, mod3, wts["g1"], wts["w_main"], wts["w_kt"], wts["w_lat"], wts["gate_bias"], wts["q_lora_g"],
            wts["kv_lora_g"], wts["w_q_rot"] if has_rope else wts["w_q"], wts["w_kv"],
            wts["q_head_g"], wts["k_head_g"]]
    if has_rope:
        in_specs.append(pl.BlockSpec((2, tm, HEAD_PAD), lambda b, i: (0, i, 0)))
        args.append(rope_tab)
    out_specs = [tok(M_WIDTH),
                 pl.BlockSpec((None, M_WIDTH, tm), lambda b, i: (b, 0, i)),
                 tok(M_WIDTH), tok(M_WIDTH),
                 pl.BlockSpec((None, 3, 2 * M_HEADS, tm), lambda b, i: (b, 0, 0, i)),
                 tok(QK_PAD), tok(QK_PAD), tok(A_WIDTH)]
    out_shape = [
        jax.ShapeDtypeStruct((B, T, M_WIDTH), BF16),
        jax.ShapeDtypeStruct((B, M_WIDTH, T), F32),
        jax.ShapeDtypeStruct((B, T, M_WIDTH), BF16),
        jax.ShapeDtypeStruct((B, T, M_WIDTH), F32),
        jax.ShapeDtypeStruct((B, 3, 2 * M_HEADS, T), F32),
        jax.ShapeDtypeStruct((B, T, QK_PAD), BF16),
        jax.ShapeDtypeStruct((B, T, QK_PAD), BF16),
        jax.ShapeDtypeStruct((B, T, A_WIDTH), BF16),
    ]
    if emit_cache:
        out_specs += [tok(KV_LORA), tok(A_ROPE)]
        out_shape += [jax.ShapeDtypeStruct((B, T, KV_LORA), F32),
                      jax.ShapeDtypeStruct((B, T, A_ROPE), F32)]
        steps, per_b = B * (T // tm), T // tm
        for w in (wts["w_out_f32"], wts["w_up_f32"], wts["w_down_f32"]):
            rows = w.shape[0] // steps
            assert rows * steps == w.shape[0] and rows % (2 * SUBLANES) == 0
            spec = pl.BlockSpec((rows, w.shape[1]), lambda b, i: (b * per_b + i, 0))
            in_specs.append(spec)
            args.append(w)
            out_specs.append(spec)
            out_shape.append(jax.ShapeDtypeStruct(w.shape, BF16))
    outs = pl.pallas_call(
        functools.partial(_pre_kernel, has_rope, emit_cache, mod_row0, mod_per_batch),
        grid=(B, T // tm),
        in_specs=in_specs,
        out_specs=out_specs,
        out_shape=out_shape,
        scratch_shapes=[pltpu.VMEM((tm, D_MODEL), BF16)],
        compiler_params=_params(("parallel", "parallel")),
        name="pre_latent" if has_rope else "pre_context",
    )(*args)
    keep = (1, 4, 10, 11, 12)
    return [o if n in keep else o.reshape(shape[:2] + o.shape[2:]) for n, o in enumerate(outs)]


def _rows_to_lane_broadcast(rows, spread):
    x = jnp.concatenate(rows, axis=0)
    p1 = x.astype(BF16)
    r1 = x - p1.astype(F32)
    p2 = r1.astype(BF16)
    p3 = (r1 - p2.astype(F32)).astype(BF16)
    pad = jnp.zeros((spread.shape[0] - 3 * len(rows), x.shape[1]), BF16)
    return _dot_tn(jnp.concatenate([p1, p2, p3, pad], axis=0), spread)


def _mlstm_gate_rows(b_row, a_row, amax_row, forward, m):
    L = b_row.shape[1]
    last = slice(L - 1, L) if forward else slice(0, 1)
    total = b_row[:, last]
    g_row = jnp.maximum(m, amax_row)
    m_new = total + jnp.maximum(m, amax_row[:, last])
    w_key_row = jnp.exp2((a_row + (total - m_new)) * LOG2E)
    decay = jnp.exp(total + m - m_new)
    return g_row * LOG2E, (b_row + g_row) * LOG2E, a_row * LOG2E, w_key_row, decay, m_new


def _mlstm_block(s_raw, q, kt, v_aug, g2, mt2, a2_row, w_key_row, decay, allow, CN, m):
    w_intra = jnp.exp2(jnp.where(allow, a2_row - jnp.concatenate([g2, g2], axis=1), -jnp.inf))
    w_inter = jnp.exp2(m * LOG2E - g2)
    s = (s_raw * w_intra).astype(BF16)
    nd = _dot(s, v_aug) + jnp.concatenate([w_inter, w_inter], axis=1) * _dot(q, CN.astype(BF16))
    num, den = nd[:, 0:M_HEAD_DIM], nd[:, M_HEAD_DIM:2 * M_HEAD_DIM]
    h = num / jnp.maximum(jnp.abs(den), jnp.exp2(-mt2))
    CN_new = decay * CN + _dot((kt * w_key_row).astype(BF16), v_aug)
    return h, CN_new


def _mlstm_kernel(has_init, emit_state, n_blocks, heads, seqs, group, *refs):
    q_ref, kt_ref, v_ref, mo_ref, stats_ref, ng_ref, spread_ref = refs[:7]
    pos = 7
    if has_init:
        c0_ref, n0_ref, m0_ref = refs[pos:pos + 3]
        pos += 3
    hm_ref = refs[pos]
    pos += 1
    if emit_state:
        c_ref, n_ref, m_ref = refs[pos:pos + 3]
        pos += 3

    L, Dh = M_BLOCK, M_HEAD_DIM
    t_idx = lax.broadcasted_iota(jnp.int32, (L, L), 0)
    s_idx = lax.broadcasted_iota(jnp.int32, (L, L), 1)
    allow = (s_idx <= t_idx, s_idx >= t_idx)
    spread = spread_ref[...]
    ones = jnp.ones((L, Dh), BF16)

    def time_lanes(sq, c):
        start = ((sq % group) * n_blocks + c) * L
        return sq // group, slice(start, start + L)

    def lane_broadcast_n(n_row):
        return jnp.broadcast_to(n_row, (Dh, Dh)).T

    def init_state(sq, j, d):
        if has_init:
            return (jnp.concatenate([c0_ref[sq, d, j], lane_broadcast_n(n0_ref[sq, d, j])], axis=1),
                    m0_ref[sq, j, d:d + 1, 0:1])
        return jnp.zeros((Dh, 2 * Dh), F32), jnp.zeros((1, 1), F32)

    def gate_rows(sq, j, c, d, m):
        head = j if heads == M_HEADS else pl.program_id(1) * heads + j
        r, (g, lanes) = pl.ds(d * M_HEADS + head, 1), time_lanes(sq, c)
        return _mlstm_gate_rows(stats_ref[g, 0, r, lanes], stats_ref[g, 1, r, lanes], stats_ref[g, 2, r, lanes],
                                d == 0, m)

    def blocks(sq, j, jobs, states):
        loaded, rows6, cols_in = {}, [], []
        for (c, d), (CN, m) in zip(jobs, states):
            rows6.append(gate_rows(sq, j, c, d, m))
            cols_in += [rows6[-1][0], rows6[-1][1]]
            if c not in loaded:
                rows, cols = slice(c * L, (c + 1) * L), slice(j * Dh, (j + 1) * Dh)
                g, lanes = time_lanes(sq, c)
                q, kt = q_ref[sq, rows, cols], kt_ref[g, cols, lanes]
                v_aug = jnp.concatenate([v_ref[sq, rows, cols], ones], axis=1)
                loaded[c] = (_dot(q, kt.astype(BF16)), q, kt, v_aug)
        cols_out = _rows_to_lane_broadcast(cols_in, spread)
        hs, new_states = [], []
        for idx, ((c, d), (CN, m)) in enumerate(zip(jobs, states)):
            g2 = cols_out[:, (2 * idx) * LANES:(2 * idx + 1) * LANES]
            mt2 = cols_out[:, (2 * idx + 1) * LANES:(2 * idx + 2) * LANES]
            _, _, a2_row, w_key_row, decay, m_new = rows6[idx]
            h, CN_new = _mlstm_block(*loaded[c], g2, mt2, a2_row, w_key_row, decay, allow[d], CN, m)
            hs.append(h)
            new_states.append((CN_new, m_new))
        return hs, new_states

    def finish(sq, j, rows, hs):
        cols = slice(j * Dh, (j + 1) * Dh)
        hn = _rms(hs, ng_ref[j])
        hm_ref[sq, rows, cols] = (hn * jax.nn.sigmoid(mo_ref[sq, rows, cols])).astype(hm_ref.dtype)

    def emit(sq, j, d, state):
        CN, m = state
        c_ref[sq, d, j] = CN[:, 0:Dh]
        n_ref[sq, d, j] = CN[:, Dh:2 * Dh].T[0:1, :]
        m_ref[sq, j, d:d + 1, :] = jnp.broadcast_to(m, (1, LANES))

    if n_blocks > 1:
        hf_scr, hb_scr = refs[pos:pos + 2]

    for sq, j in [(sq, j) for sq in range(seqs) for j in range(heads)]:
        cols = slice(j * Dh, (j + 1) * Dh)
        states = [init_state(sq, j, 0), init_state(sq, j, 1)]
        if n_blocks == 1:
            (hf, hb), states = blocks(sq, j, [(0, 0), (0, 1)], states)
            finish(sq, j, slice(0, L), hf + hb)
        else:
            for step in range(n_blocks):
                cf, cb = step, n_blocks - 1 - step
                (hf, hb), states = blocks(sq, j, [(cf, 0), (cb, 1)], states)
                hf_scr[cf * L:(cf + 1) * L, cols] = hf
                hb_scr[cb * L:(cb + 1) * L, cols] = hb
            finish(sq, j, slice(None), hf_scr[:, cols] + hb_scr[:, cols])
        if emit_state:
            emit(sq, j, 0, states[0])
            emit(sq, j, 1, states[1])


def _mlstm(mq, mkt, mv, mo, stats, norm_g, init_state, emit_state, heads, seqs):
    B, T, _ = mq.shape
    H, Dh, L = M_HEADS, M_HEAD_DIM, M_BLOCK
    nb = T // L
    w = heads * Dh
    has_init = init_state is not None
    n_rows = 4
    spread = jnp.tile(jnp.repeat(jnp.eye(n_rows, dtype=BF16), LANES, axis=1), (3, 1))
    spread = jnp.pad(spread, ((0, 2 * SUBLANES - 3 * n_rows), (0, 0)))
    tok = pl.BlockSpec((seqs, T, w), lambda b, h: (b, 0, h))
    group = mkt.shape[2] // T
    assert seqs % group == 0
    assert mkt.shape == (B // group, M_WIDTH, group * T) and stats.shape == (B // group, 3, 2 * H, group * T)
    in_specs = [tok, pl.BlockSpec((seqs // group, w, group * T), lambda b, h: (b, h, 0)), tok, tok,
                pl.BlockSpec((seqs // group, 3, 2 * H, group * T), lambda b, h: (b, 0, 0, 0)),
                pl.BlockSpec((heads, 1, Dh), lambda b, h: (h, 0, 0)),
                _const_spec((2 * SUBLANES, n_rows * LANES))]
    args = [mq, mkt, mv, mo, stats, norm_g, spread]
    state_specs = [pl.BlockSpec((seqs, 2, heads, Dh, Dh), lambda b, h: (b, 0, h, 0, 0)),
                   pl.BlockSpec((seqs, 2, heads, 1, Dh), lambda b, h: (b, 0, h, 0, 0)),
                   pl.BlockSpec((seqs, heads, 2, LANES), lambda b, h: (b, h, 0, 0))]
    if has_init:
        in_specs += state_specs
        args += list(init_state)
    out_specs = [tok]
    out_shape = [jax.ShapeDtypeStruct((B, T, M_WIDTH), BF16)]
    if emit_state:
        out_specs += state_specs
        out_shape += [jax.ShapeDtypeStruct((B, 2, H, Dh, Dh), F32),
                      jax.ShapeDtypeStruct((B, 2, H, 1, Dh), F32),
                      jax.ShapeDtypeStruct((B, H, 2, LANES), F32)]
    scratch = [] if nb == 1 else [pltpu.VMEM((T, w), F32), pltpu.VMEM((T, w), F32)]
    return pl.pallas_call(
        functools.partial(_mlstm_kernel, has_init, emit_state, nb, heads, seqs, group),
        grid=(B // seqs, H // heads),
        in_specs=in_specs,
        out_specs=out_specs,
        out_shape=out_shape,
        scratch_shapes=scratch,
        compiler_params=_params(("parallel", "parallel")),
        name="mlstm_latent" if has_init else "mlstm_context",
    )(*args)


def _attn_kernel(has_ctx, seqs, *refs):
    if has_ctx:
        q_ref, k_ref, v_ref, ckv_ref, krp_ref, wkv_ref, khg_ref, o_ref, kc_ref, vc_ref = refs

        @pl.when(pl.program_id(1) == 0)
        def _():
            kvf = _dot(ckv_ref[0].astype(BF16), wkv_ref[...])
            vc_ref[...] = kvf[:, QK_PAD:QK_PAD + A_WIDTH].astype(BF16)
            _write_heads(kvf, krp_ref[0], khg_ref[0:1, :], None, kc_ref)
    else:
        q_ref, k_ref, v_ref, o_ref = refs
    lane = lax.broadcasted_iota(jnp.int32, (1, LANES), 1)
    ones = lambda n: jnp.ones((n, LANES), BF16)
    tq = q_ref.shape[1]
    sub = min(Q_SUBTILE, tq)
    for sq, r0, pair in [(sq, r0, pair) for sq in range(seqs) for r0 in range(0, tq, sub)
                         for pair in range(A_HEADS // 2)]:
        rows = slice(r0, r0 + sub)
        vsl = slice(pair * LANES, (pair + 1) * LANES)
        v_aug = jnp.concatenate([v_ref[sq, :, vsl], ones(v_ref.shape[1])], axis=1)
        if has_ctx:
            vc_aug = jnp.concatenate([vc_ref[:, vsl], ones(vc_ref.shape[0])], axis=1)
        outs = []
        for e in range(2):
            hsl = slice((2 * pair + e) * HEAD_PAD, (2 * pair + e + 1) * HEAD_PAD)
            qh = q_ref[sq, rows, hsl]
            s = _dot_nt(qh, k_ref[sq, :, hsl])
            mx = jnp.max(s, axis=1, keepdims=True)
            if has_ctx:
                sc = _dot_nt(qh, kc_ref[:, hsl])
                mx = jnp.maximum(mx, jnp.max(sc, axis=1, keepdims=True))
            od = _dot(jnp.exp2(s - mx).astype(BF16), v_aug)
            if has_ctx:
                od = od + _dot(jnp.exp2(sc - mx).astype(BF16), vc_aug)
            outs.append(od[:, 0:LANES] / od[:, LANES:2 * LANES])
        o_ref[sq, rows, vsl] = jnp.where(lane < A_VDIM, outs[0], outs[1]).astype(o_ref.dtype)


def _attn(q, k, v, ctx, seqs):
    B, T, _ = q.shape
    tq = min(Q_TILE, T)
    has_ctx = ctx is not None
    full = lambda n, w: pl.BlockSpec((seqs, n, w), lambda b, i: (b, 0, 0))
    in_specs = [pl.BlockSpec((seqs, tq, QK_PAD), lambda b, i: (b, i, 0)), full(T, QK_PAD), full(T, A_WIDTH)]
    args = [q, k, v]
    scratch = []
    if has_ctx:
        assert seqs == 1
        P = ctx[0].shape[1]
        in_specs += [full(P, KV_LORA), full(P, HEAD_PAD), _const_spec((KV_LORA, QK_PAD + A_WIDTH)),
                     _const_spec((2, HEAD_PAD))]
        args += list(ctx)
        scratch = [pltpu.VMEM((P, QK_PAD), BF16), pltpu.VMEM((P, A_WIDTH), BF16)]
    return pl.pallas_call(
        functools.partial(_attn_kernel, has_ctx, seqs),
        grid=(B // seqs, T // tq),
        in_specs=in_specs,
        out_specs=pl.BlockSpec((seqs, tq, A_WIDTH), lambda b, i: (b, i, 0)),
        out_shape=jax.ShapeDtypeStruct((B, T, A_WIDTH), BF16),
        scratch_shapes=scratch,
        compiler_params=_params(("parallel", "arbitrary" if has_ctx else "parallel")),
        name="attn_latent" if has_ctx else "attn_context",
    )(*args)


def _post_kernel(mod_row0, mod_per_batch, x_ref, hm_ref, ha_ref, mod_ref, g2_ref, wout_ref, wup_ref, wdown_ref,
                 y_ref):
    mod = _mod_row(mod_ref, mod_row0, mod_per_batch)
    gate1 = mod[:, 2 * D_MODEL:3 * D_MODEL]
    sh2 = mod[:, 3 * D_MODEL:4 * D_MODEL]
    sc2 = mod[:, 4 * D_MODEL:5 * D_MODEL]
    gate2 = mod[:, 5 * D_MODEL:6 * D_MODEL]
    mix = jnp.concatenate([hm_ref[...], ha_ref[...]], axis=-1)
    x1 = x_ref[...] + gate1 * _dot(mix, wout_ref[...])
    h2 = (_rms(x1, g2_ref[...]) * (1.0 + sc2) + sh2).astype(BF16)
    acc = jnp.zeros_like(x1)
    for c in range(D_FF // FF_TILE):
        sl = slice(c * FF_TILE, (c + 1) * FF_TILE)
        u = jnp.maximum(_dot(h2, wup_ref[:, sl]), 0.0)
        acc = acc + _dot((u * u).astype(BF16), wdown_ref[sl, :])
    y_ref[...] = x1 + gate2 * acc


def _post(x, hm, ha, mod3, mod_row0, mod_per_batch, wts):
    shape = x.shape
    if not mod_per_batch:
        x, hm, ha = (a.reshape(1, -1, a.shape[-1]) for a in (x, hm, ha))
    B, T, _ = x.shape
    tm = POST_TILE
    tok = lambda w: pl.BlockSpec((None, tm, w), lambda b, i: (b, i, 0))
    return _post_call(x, hm, ha, mod3, mod_row0, mod_per_batch, wts, B, T, tm, tok).reshape(shape)


def _post_call(x, hm, ha, mod3, mod_row0, mod_per_batch, wts, B, T, tm, tok):
    return pl.pallas_call(
        functools.partial(_post_kernel, mod_row0, mod_per_batch),
        grid=(B, T // tm),
        in_specs=[tok(D_MODEL), tok(M_WIDTH), tok(A_WIDTH),
                  _const_spec((8, 6 * D_MODEL)),
                  _const_spec((1, D_MODEL)),
                  _const_spec((M_WIDTH + A_WIDTH, D_MODEL)),
                  _const_spec((D_MODEL, D_FF)),
                  _const_spec((D_FF, D_MODEL))],
        out_specs=tok(D_MODEL),
        out_shape=jax.ShapeDtypeStruct((B, T, D_MODEL), F32),
        compiler_params=_params(("parallel", "parallel")),
        name="post",
    )(x, hm, ha, mod3, wts["g2"], wts["w_out"], wts["w_up"], wts["w_down"])


def _prepare_weights(norm1_g, norm2_g, w_in, mlstm_gate_b, q_lora_g, kv_lora_g, w_q_up, w_kv_up,
                     q_head_g, k_head_g, w_out, w_mlp_up, w_mlp_down):
    o_g = 4 * M_WIDTH
    o_q = o_g + N_GATES
    o_kv = o_q + Q_LORA
    o_kr = o_kv + KV_LORA
    half = A_ROPE // 2
    n_dh = 2 * M_HEADS
    wt = w_in.T.astype(BF16)
    w_gate = wt[o_g:o_q].reshape(2, 2, M_HEADS, D_MODEL)
    bias = mlstm_gate_b.reshape(2, 2, M_HEADS)

    def rot_partner(a):
        z = jnp.zeros(a.shape[:-1] + (A_NOPE,), a.dtype)
        return jnp.concatenate([z, a[..., A_NOPE + half:A_QK], a[..., A_NOPE:A_NOPE + half]], axis=-1)

    pad_tile = lambda a: jnp.pad(a, [(0, 0)] * (a.ndim - 1) + [(0, HEAD_PAD - A_QK)])
    w_kr = wt[o_kr:o_kr + A_ROPE]
    w_kr_partner = jnp.concatenate([w_kr[half:], w_kr[:half]], axis=0)
    w_lat = jnp.concatenate([wt[o_q:o_kr], w_gate[:, 0].reshape(n_dh, D_MODEL), w_gate[:, 1].reshape(n_dh, D_MODEL),
                             jnp.zeros((A_NOPE - 2 * n_dh, D_MODEL), BF16), w_kr, w_kr_partner], axis=0)
    gate_bias = jnp.pad(jnp.stack([bias[:, 0, :].reshape(n_dh), bias[:, 1, :].reshape(n_dh)], axis=0),
                        ((0, 0), (0, LANES - n_dh)))
    w_q3 = w_q_up.reshape(Q_LORA, A_HEADS, A_QK)
    w_q = pad_tile(w_q3).reshape(Q_LORA, QK_PAD)
    w_q_partner = pad_tile(rot_partner(w_q3)).reshape(Q_LORA, QK_PAD)
    w_kv3 = w_kv_up.reshape(KV_LORA, A_HEADS, A_NOPE + A_VDIM)
    w_k = jnp.pad(w_kv3[:, :, :A_NOPE], ((0, 0), (0, 0), (0, HEAD_PAD - A_NOPE)))
    w_v = w_kv3[:, :, A_NOPE:]
    w_kv = jnp.concatenate([w_k.reshape(KV_LORA, QK_PAD), w_v.reshape(KV_LORA, A_WIDTH)], axis=1)
    pad_head = lambda g: jnp.stack([pad_tile(g), pad_tile(rot_partner(g))], axis=0)
    return {
        "g1": norm1_g.reshape(1, D_MODEL),
        "g2": norm2_g.reshape(1, D_MODEL),
        "w_main": jnp.concatenate([wt[0:M_WIDTH], wt[2 * M_WIDTH:o_g]], axis=0),
        "w_kt": wt[M_WIDTH:2 * M_WIDTH],
        "w_lat": w_lat,
        "gate_bias": gate_bias,
        "q_lora_g": q_lora_g.reshape(1, Q_LORA),
        "kv_lora_g": kv_lora_g.reshape(1, KV_LORA),
        "w_q": w_q.astype(BF16),
        "w_q_rot": jnp.concatenate([w_q, w_q_partner], axis=1).astype(BF16),
        "w_kv": w_kv.astype(BF16),
        "q_head_g": pad_head(q_head_g),
        "k_head_g": pad_head(k_head_g),
        "w_out_f32": w_out,
        "w_up_f32": w_mlp_up,
        "w_down_f32": w_mlp_down,
    }


def _rope_tables(T):
    rows = T // GRID_W
    row = np.repeat(np.arange(rows, dtype=np.float32), GRID_W)
    col = np.tile(np.arange(GRID_W, dtype=np.float32), rows)
    half = A_ROPE // 2
    inv = (np.float32(ROPE_BASE) ** (-np.arange(0, half, 2, dtype=np.float32) / np.float32(half))).astype(np.float32)
    ang = np.concatenate([row[:, None] * inv, col[:, None] * inv], axis=-1)
    cos, sin = np.cos(ang), np.sin(ang)
    ones = np.ones((T, A_NOPE), np.float32)
    z = lambda w: np.zeros((T, w), np.float32)
    tail = LANES - A_QK
    cos_t = np.concatenate([ones, cos, cos, z(tail)], axis=1)
    sin_t = np.concatenate([z(A_NOPE), -sin, sin, z(tail)], axis=1)
    return jnp.asarray(np.stack([cos_t, sin_t], axis=0).astype(np.float32))


def _layer_pass(x, mod3, mod_row0, mod_per_batch, wts, norm_g, rope_tab, init_state, ctx_kv, is_context):
    pre = _pre(x, mod3, mod_row0, mod_per_batch, wts, rope_tab, emit_cache=is_context)
    mq, mkt, mv, mo, stats, q, k, v = pre[:8]
    if is_context:
        wts = dict(wts, w_out=pre[10], w_up=pre[11], w_down=pre[12])
    ml = _mlstm(mq, mkt, mv, mo, stats, norm_g, init_state, emit_state=is_context,
                heads=M_HEADS if is_context else 1, seqs=MLSTM_CONTEXT_SEQS if is_context else 1)
    ha = _attn(q, k, v, ctx_kv, seqs=ATTN_CONTEXT_SEQS if is_context else 1)
    y = _post(x, ml[0], ha, mod3, mod_row0, mod_per_batch, wts)
    return y, pre[8:10], ml[1:], wts


def kernel(x_prompt, x_sample, cache_mla_ckv, cache_mla_krope, state_mlstm_C, state_mlstm_n, state_mlstm_m,
           c, c_ctx, norm1_g, norm2_g, w_ada, b_ada, w_in, mlstm_gate_b, mlstm_norm_g,
           q_lora_g, kv_lora_g, w_q_up, w_kv_up, q_head_g, k_head_g, w_out, w_mlp_up, w_mlp_down):
    depth = w_in.shape[0]
    Bd = x_sample.shape[0]
    cond8 = jnp.concatenate([c_ctx[None, :], c, jnp.zeros((8 - 1 - Bd, D_MODEL), F32)], axis=0)
    rope_tab = _rope_tables(x_sample.shape[1])

    y, z = x_prompt, x_sample
    ckvs, kropes, Cs, ns, ms = [], [], [], [], []
    for l in range(depth):
        wts = _prepare_weights(norm1_g[l], norm2_g[l], w_in[l], mlstm_gate_b[l], q_lora_g[l], kv_lora_g[l],
                               w_q_up[l], w_kv_up[l], q_head_g[l], k_head_g[l], w_out[l], w_mlp_up[l],
                               w_mlp_down[l])
        norm_g = mlstm_norm_g[l].reshape(M_HEADS, 1, M_HEAD_DIM)
        mod3 = _ada(cond8, w_ada[l], b_ada[l])
        y, (ckv, krope), (C_new, n_new, m_new), wts = _layer_pass(
            y, mod3, 0, 0, wts, norm_g, None, None, None, True)
        ckvs.append(ckv)
        kropes.append(krope)
        Cs.append(C_new)
        ns.append(n_new[:, :, :, 0, :])
        ms.append(m_new[:, :, :, 0].transpose(0, 2, 1))

        init_state = (state_mlstm_C[:, l],
                      state_mlstm_n[:, l][:, :, :, None, :],
                      jnp.broadcast_to(state_mlstm_m[:, l].transpose(0, 2, 1)[..., None],
                                       (Bd, M_HEADS, 2, LANES)))
        krope_placed = jnp.pad(cache_mla_krope[:, l], ((0, 0), (0, 0), (A_NOPE, LANES - A_QK)))
        ctx = (cache_mla_ckv[:, l], krope_placed, wts["w_kv"], wts["k_head_g"])
        z, _, _, _ = _layer_pass(z, mod3, 1, 1, wts, norm_g, rope_tab, init_state, ctx, False)

    return (y, z, jnp.stack(ckvs, axis=1), jnp.stack(kropes, axis=1), jnp.stack(Cs, axis=1),
            jnp.stack(ns, axis=1), jnp.stack(ms, axis=1))
```

```python
import functools

import jax
import jax.numpy as jnp
import numpy as np
from jax import lax
from jax.experimental import pallas as pl
from jax.experimental.pallas import tpu as pltpu

F32 = jnp.float32
BF16 = jnp.bfloat16

D_MODEL = 1024
GRID_W = 64
M_HEADS = 4
M_HEAD_DIM = 128
M_WIDTH = M_HEADS * M_HEAD_DIM
M_BLOCK = 256
A_HEADS = 8
A_NOPE = 64
A_ROPE = 32
A_QK = A_NOPE + A_ROPE
A_VDIM = 64
A_WIDTH = A_HEADS * A_VDIM
Q_LORA = 384
KV_LORA = 256
ROPE_BASE = 10000.0
D_FF = 4 * D_MODEL
EPS = 1e-6

LANES = 128
SUBLANES = 8
LOG2E = 1.4426950408889634
HEAD_PAD = LANES
QK_PAD = A_HEADS * HEAD_PAD
N_GATES = 4 * M_HEADS
LAT_WIDTH = Q_LORA + KV_LORA + LANES
VMEM_LIMIT = 56 * 1024 * 1024

TOKEN_TILE = 512
POST_TILE = 512
MLSTM_CONTEXT_SEQS = 4
ATTN_CONTEXT_SEQS = 8
Q_TILE = 1024
Q_SUBTILE = 256
ADA_TILE_N = 1536
FF_TILE = 1024


def _dot(a, b):
    return jnp.dot(a, b, preferred_element_type=F32)


def _dot_nt(a, b):
    return lax.dot_general(a, b, (((1,), (1,)), ((), ())), preferred_element_type=F32)


def _dot_tn(a, b):
    return lax.dot_general(a, b, (((0,), (0,)), ((), ())), preferred_element_type=F32)


def _rms(x, g):
    y = x * lax.rsqrt(jnp.mean(x * x, axis=-1, keepdims=True) + EPS)
    return y * g


def _params(sem):
    return pltpu.CompilerParams(dimension_semantics=sem, vmem_limit_bytes=VMEM_LIMIT)


def _const_spec(shape):
    zeros = (0,) * len(shape)
    return pl.BlockSpec(shape, lambda *_: zeros, pipeline_mode=pl.Buffered(1))


def _ada_kernel(cond_ref, w_ref, b_ref, o_ref):
    c = cond_ref[...]
    s = (c * jax.nn.sigmoid(c)).astype(BF16)
    o_ref[...] = _dot(s, w_ref[...].astype(BF16)) + b_ref[...]


def _ada(cond8, w_ada, b_ada):
    n = w_ada.shape[1]
    return pl.pallas_call(
        _ada_kernel,
        grid=(n // ADA_TILE_N,),
        in_specs=[
            pl.BlockSpec((8, D_MODEL), lambda j: (0, 0)),
            pl.BlockSpec((D_MODEL, ADA_TILE_N), lambda j: (0, j)),
            pl.BlockSpec((1, ADA_TILE_N), lambda j: (0, j)),
        ],
        out_specs=pl.BlockSpec((8, ADA_TILE_N), lambda j: (0, j)),
        out_shape=jax.ShapeDtypeStruct((8, n), F32),
        compiler_params=_params(("parallel",)),
        name="ada",
    )(cond8, w_ada, b_ada.reshape(1, n))


def _write_heads(src, extra, g_pad, rot, dst_ref):
    for h in range(A_HEADS):
        sl = slice(h * HEAD_PAD, (h + 1) * HEAD_PAD)
        xh = src[:, sl]
        if extra is not None:
            xh = xh + extra
        ss = jnp.sum(xh * xh, axis=-1, keepdims=True) * (1.0 / A_QK)
        r = lax.rsqrt(ss + EPS)
        if rot is None:
            y = xh * r * g_pad
        else:
            partner, cos_g, sin_g = rot
            ph = partner if partner.shape[1] == HEAD_PAD else partner[:, sl]
            y = (xh * cos_g + ph * sin_g) * r
        dst_ref[:, sl] = y.astype(dst_ref.dtype)


def _time_scan(x, op, identity, reverse):
    n = x.shape[0]
    row = lax.broadcasted_iota(jnp.int32, x.shape, 0)
    shift = 1
    while shift < n:
        if shift < SUBLANES:
            if reverse:
                moved = jnp.where(row < n - shift, pltpu.roll(x, n - shift, 0), identity)
            else:
                moved = jnp.where(row >= shift, pltpu.roll(x, shift, 0), identity)
        else:
            fill = jnp.full((shift, x.shape[1]), identity, x.dtype)
            moved = (jnp.concatenate([x[shift:], fill], axis=0) if reverse
                     else jnp.concatenate([fill, x[:n - shift]], axis=0))
        x = op(x, moved)
        shift *= 2
    return x


def _mod_row(mod_ref, row0, per_batch):
    if per_batch:
        return mod_ref[pl.ds(row0 + pl.program_id(0) * per_batch, 1), :]
    return mod_ref[row0:row0 + 1, :]


def _pre_kernel(has_rope, emit_cache, mod_row0, mod_per_batch, *refs):
    (x_ref, mod_ref, g1_ref, wmain_ref, wkt_ref, wlat_ref, gbias_ref, qlg_ref, kvg_ref, wq_ref, wkv_ref,
     qhg_ref, khg_ref) = refs[:13]
    pos = 13
    if has_rope:
        rope_ref = refs[pos]
        pos += 1
    if emit_cache:
        cast_in = refs[pos:pos + 3]
        pos += 3
    (mq_ref, mkt_ref, mv_ref, mo_ref, stats_ref, q_ref, k_ref, v_ref) = refs[pos:pos + 8]
    pos += 8

    x = x_ref[...]
    mod = _mod_row(mod_ref, mod_row0, mod_per_batch)
    sh1 = mod[:, 0:D_MODEL]
    sc1 = mod[:, D_MODEL:2 * D_MODEL]
    h = _rms(x, g1_ref[...]) * (1.0 + sc1) + sh1
    hb = h.astype(BF16)

    plat = _dot_nt(hb, wlat_ref[...])
    q_lat = plat[:, 0:Q_LORA]
    kv_lat = plat[:, Q_LORA:Q_LORA + KV_LORA]
    tail = plat[:, Q_LORA + KV_LORA:LAT_WIDTH]
    tail_f = pltpu.roll(tail, LANES - 2 * M_HEADS, 1)
    tail2 = pltpu.roll(tail, LANES - A_ROPE, 1)

    lane = lax.broadcasted_iota(jnp.int32, (1, LANES), 1)
    fwd = lane < M_HEADS
    gate_i = tail + gbias_ref[0:1, :]
    gate_f = tail_f + gbias_ref[1:2, :]
    log_f = jnp.minimum(gate_f, 0.0) - jnp.log1p(jnp.exp(-jnp.abs(gate_f)))
    def block_scan(v, op, identity):
        parts = [v[r:r + M_BLOCK] for r in range(0, v.shape[0], M_BLOCK)]
        return jnp.where(fwd, jnp.concatenate([_time_scan(p, op, identity, False) for p in parts], axis=0),
                         jnp.concatenate([_time_scan(p, op, identity, True) for p in parts], axis=0))

    b = block_scan(log_f, jnp.add, 0.0)
    a = gate_i - b
    amax = block_scan(a, jnp.maximum, -jnp.inf)
    stats_ref[0] = b.T[0:2 * M_HEADS, :]
    stats_ref[1] = a.T[0:2 * M_HEADS, :]
    stats_ref[2] = amax.T[0:2 * M_HEADS, :]

    krope_placed = jnp.where((lane >= A_NOPE) & (lane < A_QK), tail, 0.0)
    ckv = _rms(kv_lat, kvg_ref[...])
    qn = _rms(q_lat, qlg_ref[...])
    qf = _dot(qn.astype(BF16), wq_ref[...])
    kvf = _dot(ckv.astype(BF16), wkv_ref[...])
    v_ref[...] = kvf[:, QK_PAD:QK_PAD + A_WIDTH].astype(BF16)
    q_rot = k_rot = None
    qhg = qhg_ref[...] * (A_QK ** -0.5 * LOG2E)
    if has_rope:
        cos_t, sin_t = rope_ref[0], rope_ref[1]
        q_rot = (qf[:, QK_PAD:2 * QK_PAD], cos_t * qhg[0:1, :], sin_t * qhg[1:2, :])
        k_rot = (tail2, cos_t * khg_ref[0:1, :], sin_t * khg_ref[1:2, :])
    _write_heads(qf, None, qhg[0:1, :], q_rot, q_ref)
    _write_heads(kvf, krope_placed, khg_ref[0:1, :], k_rot, k_ref)

    if emit_cache:
        ckv_ref, krope_ref = refs[pos:pos + 2]
        ckv_ref[...] = ckv
        krope_ref[...] = tail[:, A_NOPE:A_QK]
        for src_ref, dst_ref in zip(cast_in, refs[pos + 2:pos + 5]):
            dst_ref[...] = src_ref[...].astype(BF16)

    pm = _dot_nt(hb, wmain_ref[...])
    mq_ref[...] = pm[:, 0:M_WIDTH].astype(BF16)
    mv_ref[...] = pm[:, M_WIDTH:2 * M_WIDTH].astype(BF16)
    mo_ref[...] = pm[:, 2 * M_WIDTH:3 * M_WIDTH]
    mkt_ref[...] = _dot_nt(wkt_ref[...], hb) * (M_HEAD_DIM ** -0.5)


def _pre(x, mod3, mod_row0, mod_per_batch, wts, rope_tab, emit_cache):
    shape = x.shape
    tm = TOKEN_TILE
    if shape[1] < tm:
        assert not mod_per_batch and tm % shape[1] == 0
        x = x.reshape(-1, tm, shape[2])
    B, T, _ = x.shape
    has_rope = rope_tab is not None
    tok = lambda w: pl.BlockSpec((None, tm, w), lambda b, i: (b, i, 0))
    in_specs = [
        tok(D_MODEL),
        _const_spec((8, 6 * D_MODEL)),
        _const_spec((1, D_MODEL)),
        _const_spec((3 * M_WIDTH, D_MODEL)),
        _const_spec((M_WIDTH, D_MODEL)),
        _const_spec((LAT_WIDTH, D_MODEL)),
        _const_spec((2, LANES)),
        _const_spec((1, Q_LORA)),
        _const_spec((1, KV_LORA)),
        _const_spec((Q_LORA, 2 * QK_PAD if has_rope else QK_PAD)),
        _const_spec((KV_LORA, QK_PAD + A_WIDTH)),
        _const_spec((2, HEAD_PAD)),
        _const_spec((2, HEAD_PAD)),
    ]
    assert tm % M_BLOCK == 0
    args = [x, mod3, wts["g1"], wts["w_main"], wts["w_kt"], wts["w_lat"], wts["gate_bias"], wts["q_lora_g"],
            wts["kv_lora_g"], wts["w_q_rot"] if has_rope else wts["w_q"], wts["w_kv"],
            wts["q_head_g"], wts["k_head_g"]]
    if has_rope:
        in_specs.append(pl.BlockSpec((2, tm, HEAD_PAD), lambda b, i: (0, i, 0)))
        args.append(rope_tab)
    out_specs = [tok(M_WIDTH),
                 pl.BlockSpec((None, M_WIDTH, tm), lambda b, i: (b, 0, i)),
                 tok(M_WIDTH), tok(M_WIDTH),
                 pl.BlockSpec((None, 3, 2 * M_HEADS, tm), lambda b, i: (b, 0, 0, i)),
                 tok(QK_PAD), tok(QK_PAD), tok(A_WIDTH)]
    out_shape = [
        jax.ShapeDtypeStruct((B, T, M_WIDTH), BF16),
        jax.ShapeDtypeStruct((B, M_WIDTH, T), F32),
        jax.ShapeDtypeStruct((B, T, M_WIDTH), BF16),
        jax.ShapeDtypeStruct((B, T, M_WIDTH), F32),
        jax.ShapeDtypeStruct((B, 3, 2 * M_HEADS, T), F32),
        jax.ShapeDtypeStruct((B, T, QK_PAD), BF16),
        jax.ShapeDtypeStruct((B, T, QK_PAD), BF16),
        jax.ShapeDtypeStruct((B, T, A_WIDTH), BF16),
    ]
    if emit_cache:
        out_specs += [tok(KV_LORA), tok(A_ROPE)]
        out_shape += [jax.ShapeDtypeStruct((B, T, KV_LORA), F32),
                      jax.ShapeDtypeStruct((B, T, A_ROPE), F32)]
        steps, per_b = B * (T // tm), T // tm
        for w in (wts["w_out_f32"], wts["w_up_f32"], wts["w_down_f32"]):
            rows = w.shape[0] // steps
            assert rows * steps == w.shape[0] and rows % (2 * SUBLANES) == 0
            spec = pl.BlockSpec((rows, w.shape[1]), lambda b, i: (b * per_b + i, 0))
            in_specs.append(spec)
            args.append(w)
            out_specs.append(spec)
            out_shape.append(jax.ShapeDtypeStruct(w.shape, BF16))
    outs = pl.pallas_call(
        functools.partial(_pre_kernel, has_rope, emit_cache, mod_row0, mod_per_batch),
        grid=(B, T // tm),
        in_specs=in_specs,
        out_specs=out_specs,
        out_shape=out_shape,
        compiler_params=_params(("parallel", "parallel")),
        name="pre_latent" if has_rope else "pre_context",
    )(*args)
    keep = (1, 4, 10, 11, 12)
    return [o if n in keep else o.reshape(shape[:2] + o.shape[2:]) for n, o in enumerate(outs)]


def _rows_to_lane_broadcast(rows, spread):
    x = jnp.concatenate(rows, axis=0)
    p1 = x.astype(BF16)
    r1 = x - p1.astype(F32)
    p2 = r1.astype(BF16)
    p3 = (r1 - p2.astype(F32)).astype(BF16)
    pad = jnp.zeros((spread.shape[0] - 3 * len(rows), x.shape[1]), BF16)
    return _dot_tn(jnp.concatenate([p1, p2, p3, pad], axis=0), spread)


def _mlstm_gate_rows(b_row, a_row, amax_row, forward, m):
    L = b_row.shape[1]
    last = slice(L - 1, L) if forward else slice(0, 1)
    total = b_row[:, last]
    g_row = jnp.maximum(m, amax_row)
    m_new = total + jnp.maximum(m, amax_row[:, last])
    w_key_row = jnp.exp2((a_row + (total - m_new)) * LOG2E)
    decay = jnp.exp(total + m - m_new)
    return g_row * LOG2E, (b_row + g_row) * LOG2E, a_row * LOG2E, w_key_row, decay, m_new


def _mlstm_block(s_raw, q, kt, v_aug, g2, mt2, a2_row, w_key_row, decay, allow, CN, m):
    w_intra = jnp.exp2(jnp.where(allow, a2_row - jnp.concatenate([g2, g2], axis=1), -jnp.inf))
    w_inter = jnp.exp2(m * LOG2E - g2)
    s = (s_raw * w_intra).astype(BF16)
    nd = _dot(s, v_aug) + jnp.concatenate([w_inter, w_inter], axis=1) * _dot(q, CN.astype(BF16))
    num, den = nd[:, 0:M_HEAD_DIM], nd[:, M_HEAD_DIM:2 * M_HEAD_DIM]
    h = num / jnp.maximum(jnp.abs(den), jnp.exp2(-mt2))
    CN_new = decay * CN + _dot((kt * w_key_row).astype(BF16), v_aug)
    return h, CN_new


def _mlstm_kernel(has_init, emit_state, n_blocks, heads, seqs, group, *refs):
    q_ref, kt_ref, v_ref, mo_ref, stats_ref, ng_ref, spread_ref = refs[:7]
    pos = 7
    if has_init:
        c0_ref, n0_ref, m0_ref = refs[pos:pos + 3]
        pos += 3
    hm_ref = refs[pos]
    pos += 1
    if emit_state:
        c_ref, n_ref, m_ref = refs[pos:pos + 3]
        pos += 3

    L, Dh = M_BLOCK, M_HEAD_DIM
    t_idx = lax.broadcasted_iota(jnp.int32, (L, L), 0)
    s_idx = lax.broadcasted_iota(jnp.int32, (L, L), 1)
    allow = (s_idx <= t_idx, s_idx >= t_idx)
    spread = spread_ref[...]
    ones = jnp.ones((L, Dh), BF16)

    def time_lanes(sq, c):
        start = ((sq % group) * n_blocks + c) * L
        return sq // group, slice(start, start + L)

    def lane_broadcast_n(n_row):
        return jnp.broadcast_to(n_row, (Dh, Dh)).T

    def init_state(sq, j, d):
        if has_init:
            return (jnp.concatenate([c0_ref[sq, d, j], lane_broadcast_n(n0_ref[sq, d, j])], axis=1),
                    m0_ref[sq, j, d:d + 1, 0:1])
        return jnp.zeros((Dh, 2 * Dh), F32), jnp.zeros((1, 1), F32)

    def gate_rows(sq, j, c, d, m):
        head = j if heads == M_HEADS else pl.program_id(1) * heads + j
        r, (g, lanes) = pl.ds(d * M_HEADS + head, 1), time_lanes(sq, c)
        return _mlstm_gate_rows(stats_ref[g, 0, r, lanes], stats_ref[g, 1, r, lanes], stats_ref[g, 2, r, lanes],
                                d == 0, m)

    def blocks(sq, j, jobs, states):
        loaded, rows6, cols_in = {}, [], []
        for (c, d), (CN, m) in zip(jobs, states):
            rows6.append(gate_rows(sq, j, c, d, m))
            cols_in += [rows6[-1][0], rows6[-1][1]]
            if c not in loaded:
                rows, cols = slice(c * L, (c + 1) * L), slice(j * Dh, (j + 1) * Dh)
                g, lanes = time_lanes(sq, c)
                q, kt = q_ref[sq, rows, cols], kt_ref[g, cols, lanes]
                v_aug = jnp.concatenate([v_ref[sq, rows, cols], ones], axis=1)
                loaded[c] = (_dot(q, kt.astype(BF16)), q, kt, v_aug)
        cols_out = _rows_to_lane_broadcast(cols_in, spread)
        hs, new_states = [], []
        for idx, ((c, d), (CN, m)) in enumerate(zip(jobs, states)):
            g2 = cols_out[:, (2 * idx) * LANES:(2 * idx + 1) * LANES]
            mt2 = cols_out[:, (2 * idx + 1) * LANES:(2 * idx + 2) * LANES]
            _, _, a2_row, w_key_row, decay, m_new = rows6[idx]
            h, CN_new = _mlstm_block(*loaded[c], g2, mt2, a2_row, w_key_row, decay, allow[d], CN, m)
            hs.append(h)
            new_states.append((CN_new, m_new))
        return hs, new_states

    def finish(sq, j, rows, hs):
        cols = slice(j * Dh, (j + 1) * Dh)
        hn = _rms(hs, ng_ref[j])
        hm_ref[sq, rows, cols] = (hn * jax.nn.sigmoid(mo_ref[sq, rows, cols])).astype(hm_ref.dtype)

    def emit(sq, j, d, state):
        CN, m = state
        c_ref[sq, d, j] = CN[:, 0:Dh]
        n_ref[sq, d, j] = CN[:, Dh:2 * Dh].T[0:1, :]
        m_ref[sq, j, d:d + 1, :] = jnp.broadcast_to(m, (1, LANES))

    if n_blocks > 1:
        hf_scr, hb_scr = refs[pos:pos + 2]

    for sq, j in [(sq, j) for sq in range(seqs) for j in range(heads)]:
        cols = slice(j * Dh, (j + 1) * Dh)
        states = [init_state(sq, j, 0), init_state(sq, j, 1)]
        if n_blocks == 1:
            (hf, hb), states = blocks(sq, j, [(0, 0), (0, 1)], states)
            finish(sq, j, slice(0, L), hf + hb)
        else:
            for step in range(n_blocks):
                cf, cb = step, n_blocks - 1 - step
                (hf, hb), states = blocks(sq, j, [(cf, 0), (cb, 1)], states)
                hf_scr[cf * L:(cf + 1) * L, cols] = hf
                hb_scr[cb * L:(cb + 1) * L, cols] = hb
            finish(sq, j, slice(None), hf_scr[:, cols] + hb_scr[:, cols])
        if emit_state:
            emit(sq, j, 0, states[0])
            emit(sq, j, 1, states[1])


def _mlstm(mq, mkt, mv, mo, stats, norm_g, init_state, emit_state, heads, seqs):
    B, T, _ = mq.shape
    H, Dh, L = M_HEADS, M_HEAD_DIM, M_BLOCK
    nb = T // L
    w = heads * Dh
    has_init = init_state is not None
    n_rows = 4
    spread = jnp.tile(jnp.repeat(jnp.eye(n_rows, dtype=BF16), LANES, axis=1), (3, 1))
    spread = jnp.pad(spread, ((0, 2 * SUBLANES - 3 * n_rows), (0, 0)))
    tok = pl.BlockSpec((seqs, T, w), lambda b, h: (b, 0, h))
    group = mkt.shape[2] // T
    assert seqs % group == 0
    assert mkt.shape == (B // group, M_WIDTH, group * T) and stats.shape == (B // group, 3, 2 * H, group * T)
    in_specs = [tok, pl.BlockSpec((seqs // group, w, group * T), lambda b, h: (b, h, 0)), tok, tok,
                pl.BlockSpec((seqs // group, 3, 2 * H, group * T), lambda b, h: (b, 0, 0, 0)),
                pl.BlockSpec((heads, 1, Dh), lambda b, h: (h, 0, 0)),
                _const_spec((2 * SUBLANES, n_rows * LANES))]
    args = [mq, mkt, mv, mo, stats, norm_g, spread]
    state_specs = [pl.BlockSpec((seqs, 2, heads, Dh, Dh), lambda b, h: (b, 0, h, 0, 0)),
                   pl.BlockSpec((seqs, 2, heads, 1, Dh), lambda b, h: (b, 0, h, 0, 0)),
                   pl.BlockSpec((seqs, heads, 2, LANES), lambda b, h: (b, h, 0, 0))]
    if has_init:
        in_specs += state_specs
        args += list(init_state)
    out_specs = [tok]
    out_shape = [jax.ShapeDtypeStruct((B, T, M_WIDTH), BF16)]
    if emit_state:
        out_specs += state_specs
        out_shape += [jax.ShapeDtypeStruct((B, 2, H, Dh, Dh), F32),
                      jax.ShapeDtypeStruct((B, 2, H, 1, Dh), F32),
                      jax.ShapeDtypeStruct((B, H, 2, LANES), F32)]
    scratch = [] if nb == 1 else [pltpu.VMEM((T, w), F32), pltpu.VMEM((T, w), F32)]
    return pl.pallas_call(
        functools.partial(_mlstm_kernel, has_init, emit_state, nb, heads, seqs, group),
        grid=(B // seqs, H // heads),
        in_specs=in_specs,
        out_specs=out_specs,
        out_shape=out_shape,
        scratch_shapes=scratch,
        compiler_params=_params(("parallel", "parallel")),
        name="mlstm_latent" if has_init else "mlstm_context",
    )(*args)


def _attn_kernel(has_ctx, seqs, *refs):
    if has_ctx:
        q_ref, k_ref, v_ref, ckv_ref, krp_ref, wkv_ref, khg_ref, o_ref, kc_ref, vc_ref = refs

        @pl.when(pl.program_id(1) == 0)
        def _():
            kvf = _dot(ckv_ref[0].astype(BF16), wkv_ref[...])
            vc_ref[...] = kvf[:, QK_PAD:QK_PAD + A_WIDTH].astype(BF16)
            _write_heads(kvf, krp_ref[0], khg_ref[0:1, :], None, kc_ref)
    else:
        q_ref, k_ref, v_ref, o_ref = refs
    lane = lax.broadcasted_iota(jnp.int32, (1, LANES), 1)
    ones = lambda n: jnp.ones((n, LANES), BF16)
    tq = q_ref.shape[1]
    sub = min(Q_SUBTILE, tq)
    sizes = [sub] * (tq // sub)
    if len(sizes) > 2:
        sizes = [sub // 2] + sizes[1:] + [sub // 2]
    starts = [sum(sizes[:n]) for n in range(len(sizes))]
    for sq, (r0, size), pair in [(sq, rs, pair) for sq in range(seqs) for rs in zip(starts, sizes)
                                 for pair in range(A_HEADS // 2)]:
        rows = slice(r0, r0 + size)
        vsl = slice(pair * LANES, (pair + 1) * LANES)
        v_aug = jnp.concatenate([v_ref[sq, :, vsl], ones(v_ref.shape[1])], axis=1)
        if has_ctx:
            vc_aug = jnp.concatenate([vc_ref[:, vsl], ones(vc_ref.shape[0])], axis=1)
        outs = []
        for e in range(2):
            hsl = slice((2 * pair + e) * HEAD_PAD, (2 * pair + e + 1) * HEAD_PAD)
            qh = q_ref[sq, rows, hsl]
            s = _dot_nt(qh, k_ref[sq, :, hsl])
            mx = jnp.max(s, axis=1, keepdims=True)
            if has_ctx:
                sc = _dot_nt(qh, kc_ref[:, hsl])
                mx = jnp.maximum(mx, jnp.max(sc, axis=1, keepdims=True))
            od = _dot(jnp.exp2(s - mx).astype(BF16), v_aug)
            if has_ctx:
                od = od + _dot(jnp.exp2(sc - mx).astype(BF16), vc_aug)
            outs.append(od[:, 0:LANES] / od[:, LANES:2 * LANES])
        o_ref[sq, rows, vsl] = jnp.where(lane < A_VDIM, outs[0], outs[1]).astype(o_ref.dtype)


def _attn(q, k, v, ctx, seqs):
    B, T, _ = q.shape
    tq = min(Q_TILE, T)
    has_ctx = ctx is not None
    full = lambda n, w: pl.BlockSpec((seqs, n, w), lambda b, i: (b, 0, 0))
    in_specs = [pl.BlockSpec((seqs, tq, QK_PAD), lambda b, i: (b, i, 0)), full(T, QK_PAD), full(T, A_WIDTH)]
    args = [q, k, v]
    scratch = []
    if has_ctx:
        assert seqs == 1
        P = ctx[0].shape[1]
        in_specs += [full(P, KV_LORA), full(P, HEAD_PAD), _const_spec((KV_LORA, QK_PAD + A_WIDTH)),
                     _const_spec((2, HEAD_PAD))]
        args += list(ctx)
        scratch = [pltpu.VMEM((P, QK_PAD), BF16), pltpu.VMEM((P, A_WIDTH), BF16)]
    return pl.pallas_call(
        functools.partial(_attn_kernel, has_ctx, seqs),
        grid=(B // seqs, T // tq),
        in_specs=in_specs,
        out_specs=pl.BlockSpec((seqs, tq, A_WIDTH), lambda b, i: (b, i, 0)),
        out_shape=jax.ShapeDtypeStruct((B, T, A_WIDTH), BF16),
        scratch_shapes=scratch,
        compiler_params=_params(("parallel", "arbitrary" if has_ctx else "parallel")),
        name="attn_latent" if has_ctx else "attn_context",
    )(*args)


def _post_kernel(mod_row0, mod_per_batch, x_ref, hm_ref, ha_ref, mod_ref, g2_ref, wout_ref, wup_ref, wdown_ref,
                 y_ref):
    mod = _mod_row(mod_ref, mod_row0, mod_per_batch)
    gate1 = mod[:, 2 * D_MODEL:3 * D_MODEL]
    sh2 = mod[:, 3 * D_MODEL:4 * D_MODEL]
    sc2 = mod[:, 4 * D_MODEL:5 * D_MODEL]
    gate2 = mod[:, 5 * D_MODEL:6 * D_MODEL]
    mix = jnp.concatenate([hm_ref[...], ha_ref[...]], axis=-1)
    x1 = x_ref[...] + gate1 * _dot(mix, wout_ref[...])
    h2 = (_rms(x1, g2_ref[...]) * (1.0 + sc2) + sh2).astype(BF16)
    acc = jnp.zeros_like(x1)
    for c in range(D_FF // FF_TILE):
        sl = slice(c * FF_TILE, (c + 1) * FF_TILE)
        u = jnp.maximum(_dot(h2, wup_ref[:, sl]), 0.0)
        acc = acc + _dot((u * u).astype(BF16), wdown_ref[sl, :])
    y_ref[...] = x1 + gate2 * acc


def _post(x, hm, ha, mod3, mod_row0, mod_per_batch, wts):
    shape = x.shape
    if not mod_per_batch:
        x, hm, ha = (a.reshape(1, -1, a.shape[-1]) for a in (x, hm, ha))
    B, T, _ = x.shape
    tm = POST_TILE
    tok = lambda w: pl.BlockSpec((None, tm, w), lambda b, i: (b, i, 0))
    return _post_call(x, hm, ha, mod3, mod_row0, mod_per_batch, wts, B, T, tm, tok).reshape(shape)


def _post_call(x, hm, ha, mod3, mod_row0, mod_per_batch, wts, B, T, tm, tok):
    return pl.pallas_call(
        functools.partial(_post_kernel, mod_row0, mod_per_batch),
        grid=(B, T // tm),
        in_specs=[tok(D_MODEL), tok(M_WIDTH), tok(A_WIDTH),
                  _const_spec((8, 6 * D_MODEL)),
                  _const_spec((1, D_MODEL)),
                  _const_spec((M_WIDTH + A_WIDTH, D_MODEL)),
                  _const_spec((D_MODEL, D_FF)),
                  _const_spec((D_FF, D_MODEL))],
        out_specs=tok(D_MODEL),
        out_shape=jax.ShapeDtypeStruct((B, T, D_MODEL), F32),
        compiler_params=_params(("parallel", "parallel")),
        name="post",
    )(x, hm, ha, mod3, wts["g2"], wts["w_out"], wts["w_up"], wts["w_down"])


def _prepare_weights(norm1_g, norm2_g, w_in, mlstm_gate_b, q_lora_g, kv_lora_g, w_q_up, w_kv_up,
                     q_head_g, k_head_g, w_out, w_mlp_up, w_mlp_down):
    o_g = 4 * M_WIDTH
    o_q = o_g + N_GATES
    o_kv = o_q + Q_LORA
    o_kr = o_kv + KV_LORA
    half = A_ROPE // 2
    n_dh = 2 * M_HEADS
    wt = w_in.T.astype(BF16)
    w_gate = wt[o_g:o_q].reshape(2, 2, M_HEADS, D_MODEL)
    bias = mlstm_gate_b.reshape(2, 2, M_HEADS)

    def rot_partner(a):
        z = jnp.zeros(a.shape[:-1] + (A_NOPE,), a.dtype)
        return jnp.concatenate([z, a[..., A_NOPE + half:A_QK], a[..., A_NOPE:A_NOPE + half]], axis=-1)

    pad_tile = lambda a: jnp.pad(a, [(0, 0)] * (a.ndim - 1) + [(0, HEAD_PAD - A_QK)])
    w_kr = wt[o_kr:o_kr + A_ROPE]
    w_kr_partner = jnp.concatenate([w_kr[half:], w_kr[:half]], axis=0)
    w_lat = jnp.concatenate([wt[o_q:o_kr], w_gate[:, 0].reshape(n_dh, D_MODEL), w_gate[:, 1].reshape(n_dh, D_MODEL),
                             jnp.zeros((A_NOPE - 2 * n_dh, D_MODEL), BF16), w_kr, w_kr_partner], axis=0)
    gate_bias = jnp.pad(jnp.stack([bias[:, 0, :].reshape(n_dh), bias[:, 1, :].reshape(n_dh)], axis=0),
                        ((0, 0), (0, LANES - n_dh)))
    w_q3 = w_q_up.reshape(Q_LORA, A_HEADS, A_QK)
    w_q = pad_tile(w_q3).reshape(Q_LORA, QK_PAD)
    w_q_partner = pad_tile(rot_partner(w_q3)).reshape(Q_LORA, QK_PAD)
    w_kv3 = w_kv_up.reshape(KV_LORA, A_HEADS, A_NOPE + A_VDIM)
    w_k = jnp.pad(w_kv3[:, :, :A_NOPE], ((0, 0), (0, 0), (0, HEAD_PAD - A_NOPE)))
    w_v = w_kv3[:, :, A_NOPE:]
    w_kv = jnp.concatenate([w_k.reshape(KV_LORA, QK_PAD), w_v.reshape(KV_LORA, A_WIDTH)], axis=1)
    pad_head = lambda g: jnp.stack([pad_tile(g), pad_tile(rot_partner(g))], axis=0)
    return {
        "g1": norm1_g.reshape(1, D_MODEL),
        "g2": norm2_g.reshape(1, D_MODEL),
        "w_main": jnp.concatenate([wt[0:M_WIDTH], wt[2 * M_WIDTH:o_g]], axis=0),
        "w_kt": wt[M_WIDTH:2 * M_WIDTH],
        "w_lat": w_lat,
        "gate_bias": gate_bias,
        "q_lora_g": q_lora_g.reshape(1, Q_LORA),
        "kv_lora_g": kv_lora_g.reshape(1, KV_LORA),
        "w_q": w_q.astype(BF16),
        "w_q_rot": jnp.concatenate([w_q, w_q_partner], axis=1).astype(BF16),
        "w_kv": w_kv.astype(BF16),
        "q_head_g": pad_head(q_head_g),
        "k_head_g": pad_head(k_head_g),
        "w_out_f32": w_out,
        "w_up_f32": w_mlp_up,
        "w_down_f32": w_mlp_down,
    }


def _rope_tables(T):
    rows = T // GRID_W
    row = np.repeat(np.arange(rows, dtype=np.float32), GRID_W)
    col = np.tile(np.arange(GRID_W, dtype=np.float32), rows)
    half = A_ROPE // 2
    inv = (np.float32(ROPE_BASE) ** (-np.arange(0, half, 2, dtype=np.float32) / np.float32(half))).astype(np.float32)
    ang = np.concatenate([row[:, None] * inv, col[:, None] * inv], axis=-1)
    cos, sin = np.cos(ang), np.sin(ang)
    ones = np.ones((T, A_NOPE), np.float32)
    z = lambda w: np.zeros((T, w), np.float32)
    tail = LANES - A_QK
    cos_t = np.concatenate([ones, cos, cos, z(tail)], axis=1)
    sin_t = np.concatenate([z(A_NOPE), -sin, sin, z(tail)], axis=1)
    return jnp.asarray(np.stack([cos_t, sin_t], axis=0).astype(np.float32))


def _layer_pass(x, mod3, mod_row0, mod_per_batch, wts, norm_g, rope_tab, init_state, ctx_kv, is_context):
    pre = _pre(x, mod3, mod_row0, mod_per_batch, wts, rope_tab, emit_cache=is_context)
    mq, mkt, mv, mo, stats, q, k, v = pre[:8]
    if is_context:
        wts = dict(wts, w_out=pre[10], w_up=pre[11], w_down=pre[12])
    ml = _mlstm(mq, mkt, mv, mo, stats, norm_g, init_state, emit_state=is_context,
                heads=M_HEADS if is_context else 1, seqs=MLSTM_CONTEXT_SEQS if is_context else 1)
    ha = _attn(q, k, v, ctx_kv, seqs=ATTN_CONTEXT_SEQS if is_context else 1)
    y = _post(x, ml[0], ha, mod3, mod_row0, mod_per_batch, wts)
    return y, pre[8:10], ml[1:], wts


def kernel(x_prompt, x_sample, cache_mla_ckv, cache_mla_krope, state_mlstm_C, state_mlstm_n, state_mlstm_m,
           c, c_ctx, norm1_g, norm2_g, w_ada, b_ada, w_in, mlstm_gate_b, mlstm_norm_g,
           q_lora_g, kv_lora_g, w_q_up, w_kv_up, q_head_g, k_head_g, w_out, w_mlp_up, w_mlp_down):
    depth = w_in.shape[0]
    Bd = x_sample.shape[0]
    cond8 = jnp.concatenate([c_ctx[None, :], c, jnp.zeros((8 - 1 - Bd, D_MODEL), F32)], axis=0)
    rope_tab = _rope_tables(x_sample.shape[1])

    y, z = x_prompt, x_sample
    ckvs, kropes, Cs, ns, ms = [], [], [], [], []
    for l in range(depth):
        wts = _prepare_weights(norm1_g[l], norm2_g[l], w_in[l], mlstm_gate_b[l], q_lora_g[l], kv_lora_g[l],
                               w_q_up[l], w_kv_up[l], q_head_g[l], k_head_g[l], w_out[l], w_mlp_up[l],
                               w_mlp_down[l])
        norm_g = mlstm_norm_g[l].reshape(M_HEADS, 1, M_HEAD_DIM)
        mod3 = _ada(cond8, w_ada[l], b_ada[l])
        y, (ckv, krope), (C_new, n_new, m_new), wts = _layer_pass(
            y, mod3, 0, 0, wts, norm_g, None, None, None, True)
        ckvs.append(ckv)
        kropes.append(krope)
        Cs.append(C_new)
        ns.append(n_new[:, :, :, 0, :])
        ms.append(m_new[:, :, :, 0].transpose(0, 2, 1))

        init_state = (state_mlstm_C[:, l],
                      state_mlstm_n[:, l][:, :, :, None, :],
                      jnp.broadcast_to(state_mlstm_m[:, l].transpose(0, 2, 1)[..., None],
                                       (Bd, M_HEADS, 2, LANES)))
        krope_placed = jnp.pad(cache_mla_krope[:, l], ((0, 0), (0, 0), (A_NOPE, LANES - A_QK)))
        ctx = (cache_mla_ckv[:, l], krope_placed, wts["w_kv"], wts["k_head_g"])
        z, _, _, _ = _layer_pass(z, mod3, 1, 1, wts, norm_g, rope_tab, init_state, ctx, False)

    return (y, z, jnp.stack(ckvs, axis=1), jnp.stack(kropes, axis=1), jnp.stack(Cs, axis=1),
            jnp.stack(ns, axis=1), jnp.stack(ms, axis=1))
```

```python
import functools

import jax
import jax.numpy as jnp
import numpy as np
from jax import lax
from jax.experimental import pallas as pl
from jax.experimental.pallas import tpu as pltpu

F32 = jnp.float32
BF16 = jnp.bfloat16

D_MODEL = 1024
GRID_W = 64
M_HEADS = 4
M_HEAD_DIM = 128
M_WIDTH = M_HEADS * M_HEAD_DIM
M_BLOCK = 256
A_HEADS = 8
A_NOPE = 64
A_ROPE = 32
A_QK = A_NOPE + A_ROPE
A_VDIM = 64
A_WIDTH = A_HEADS * A_VDIM
Q_LORA = 384
KV_LORA = 256
ROPE_BASE = 10000.0
D_FF = 4 * D_MODEL
EPS = 1e-6

LANES = 128
SUBLANES = 8
LOG2E = 1.4426950408889634
HEAD_PAD = LANES
QK_PAD = A_HEADS * HEAD_PAD
N_GATES = 4 * M_HEADS
LAT_WIDTH = Q_LORA + KV_LORA + LANES
VMEM_LIMIT = 56 * 1024 * 1024

TOKEN_TILE = 512
POST_TILE = 512
MLSTM_CONTEXT_SEQS = 4
ATTN_CONTEXT_SEQS = 8
Q_TILE = 1024
Q_SUBTILE = 256
ADA_TILE_N = 1536
FF_TILE = 1024


def _dot(a, b):
    return jnp.dot(a, b, preferred_element_type=F32)


def _dot_nt(a, b):
    return lax.dot_general(a, b, (((1,), (1,)), ((), ())), preferred_element_type=F32)


def _dot_tn(a, b):
    return lax.dot_general(a, b, (((0,), (0,)), ((), ())), preferred_element_type=F32)


def _rms(x, g):
    y = x * lax.rsqrt(jnp.mean(x * x, axis=-1, keepdims=True) + EPS)
    return y * g


def _params(sem):
    return pltpu.CompilerParams(dimension_semantics=sem, vmem_limit_bytes=VMEM_LIMIT)


def _const_spec(shape):
    zeros = (0,) * len(shape)
    return pl.BlockSpec(shape, lambda *_: zeros, pipeline_mode=pl.Buffered(1))


def _ada_kernel(cond_ref, w_ref, b_ref, o_ref):
    c = cond_ref[...]
    s = (c * jax.nn.sigmoid(c)).astype(BF16)
    o_ref[...] = _dot(s, w_ref[...].astype(BF16)) + b_ref[...]


def _ada(cond8, w_ada, b_ada):
    n = w_ada.shape[1]
    return pl.pallas_call(
        _ada_kernel,
        grid=(n // ADA_TILE_N,),
        in_specs=[
            pl.BlockSpec((8, D_MODEL), lambda j: (0, 0)),
            pl.BlockSpec((D_MODEL, ADA_TILE_N), lambda j: (0, j)),
            pl.BlockSpec((1, ADA_TILE_N), lambda j: (0, j)),
        ],
        out_specs=pl.BlockSpec((8, ADA_TILE_N), lambda j: (0, j)),
        out_shape=jax.ShapeDtypeStruct((8, n), F32),
        compiler_params=_params(("parallel",)),
        name="ada",
    )(cond8, w_ada, b_ada.reshape(1, n))


def _write_heads(src, extra, g_pad, rot, dst_ref):
    for h in range(A_HEADS):
        sl = slice(h * HEAD_PAD, (h + 1) * HEAD_PAD)
        xh = src[:, sl]
        if extra is not None:
            xh = xh + extra
        ss = jnp.sum(xh * xh, axis=-1, keepdims=True) * (1.0 / A_QK)
        r = lax.rsqrt(ss + EPS)
        if rot is None:
            y = xh * r * g_pad
        else:
            partners, cos_g, sin_g = rot
            y = (xh * cos_g + partners[h] * sin_g) * r
        dst_ref[:, sl] = y.astype(dst_ref.dtype)


def _time_scan(x, op, identity, reverse):
    n = x.shape[0]
    row = lax.broadcasted_iota(jnp.int32, x.shape, 0)
    shift = 1
    while shift < n:
        if shift < SUBLANES:
            if reverse:
                moved = jnp.where(row < n - shift, pltpu.roll(x, n - shift, 0), identity)
            else:
                moved = jnp.where(row >= shift, pltpu.roll(x, shift, 0), identity)
        else:
            fill = jnp.full((shift, x.shape[1]), identity, x.dtype)
            moved = (jnp.concatenate([x[shift:], fill], axis=0) if reverse
                     else jnp.concatenate([fill, x[:n - shift]], axis=0))
        x = op(x, moved)
        shift *= 2
    return x


def _mod_row(mod_ref, row0, per_batch):
    if per_batch:
        return mod_ref[pl.ds(row0 + pl.program_id(0) * per_batch, 1), :]
    return mod_ref[row0:row0 + 1, :]


def _pre_kernel(has_rope, emit_cache, mod_row0, mod_per_batch, *refs):
    (x_ref, mod_ref, g1_ref, wmain_ref, wkt_ref, wlat_ref, gbias_ref, qlg_ref, kvg_ref, wq_ref, wkv_ref,
     qhg_ref, khg_ref) = refs[:13]
    pos = 13
    if has_rope:
        rope_ref = refs[pos]
        pos += 1
    if emit_cache:
        cast_in = refs[pos:pos + 3]
        pos += 3
    (mq_ref, mkt_ref, mv_ref, mo_ref, stats_ref, q_ref, k_ref, v_ref) = refs[pos:pos + 8]
    pos += 8

    x = x_ref[...]
    mod = _mod_row(mod_ref, mod_row0, mod_per_batch)
    sh1 = mod[:, 0:D_MODEL]
    sc1 = mod[:, D_MODEL:2 * D_MODEL]
    h = _rms(x, g1_ref[...]) * (1.0 + sc1) + sh1
    hb = h.astype(BF16)

    plat = _dot_nt(hb, wlat_ref[...])
    q_lat = plat[:, 0:Q_LORA]
    kv_lat = plat[:, Q_LORA:Q_LORA + KV_LORA]
    tail = plat[:, Q_LORA + KV_LORA:LAT_WIDTH]
    tail_f = pltpu.roll(tail, LANES - 2 * M_HEADS, 1)
    tail2 = pltpu.roll(tail, LANES - A_ROPE, 1)

    lane = lax.broadcasted_iota(jnp.int32, (1, LANES), 1)
    fwd = lane < M_HEADS
    gate_i = tail + gbias_ref[0:1, :]
    gate_f = tail_f + gbias_ref[1:2, :]
    log_f = jnp.minimum(gate_f, 0.0) - jnp.log1p(jnp.exp(-jnp.abs(gate_f)))
    def block_scan(v, op, identity):
        parts = [v[r:r + M_BLOCK] for r in range(0, v.shape[0], M_BLOCK)]
        return jnp.where(fwd, jnp.concatenate([_time_scan(p, op, identity, False) for p in parts], axis=0),
                         jnp.concatenate([_time_scan(p, op, identity, True) for p in parts], axis=0))

    b = block_scan(log_f, jnp.add, 0.0)
    a = gate_i - b
    amax = block_scan(a, jnp.maximum, -jnp.inf)
    stats_ref[0] = b.T[0:2 * M_HEADS, :]
    stats_ref[1] = a.T[0:2 * M_HEADS, :]
    stats_ref[2] = amax.T[0:2 * M_HEADS, :]

    krope_placed = jnp.where((lane >= A_NOPE) & (lane < A_QK), tail, 0.0)
    ckv = _rms(kv_lat, kvg_ref[...])
    qn = _rms(q_lat, qlg_ref[...])
    qf = _dot(qn.astype(BF16), wq_ref[...])
    kvf = _dot(ckv.astype(BF16), wkv_ref[...])
    v_ref[...] = kvf[:, QK_PAD:QK_PAD + A_WIDTH].astype(BF16)
    q_rot = k_rot = None
    qhg = qhg_ref[...] * (A_QK ** -0.5 * LOG2E)
    if has_rope:
        cos_t, sin_t = rope_ref[0], rope_ref[1]
        q_partners = []
        for h in range(A_HEADS):
            packed = qf[:, QK_PAD + (h // 4) * LANES:QK_PAD + (h // 4 + 1) * LANES]
            shift = (A_NOPE - (h % 4) * A_ROPE) % LANES
            q_partners.append(pltpu.roll(packed, shift, 1) if shift else packed)
        q_rot = (q_partners, cos_t * qhg[0:1, :], sin_t * qhg[1:2, :])
        k_rot = ([tail2] * A_HEADS, cos_t * khg_ref[0:1, :], sin_t * khg_ref[1:2, :])
    _write_heads(qf, None, qhg[0:1, :], q_rot, q_ref)
    _write_heads(kvf, krope_placed, khg_ref[0:1, :], k_rot, k_ref)

    if emit_cache:
        ckv_ref, krope_ref = refs[pos:pos + 2]
        ckv_ref[...] = ckv
        krope_ref[...] = tail[:, A_NOPE:A_QK]
        for src_ref, dst_ref in zip(cast_in, refs[pos + 2:pos + 5]):
            dst_ref[...] = src_ref[...].astype(BF16)

    pm = _dot_nt(hb, wmain_ref[...])
    mq_ref[...] = pm[:, 0:M_WIDTH].astype(BF16)
    mv_ref[...] = pm[:, M_WIDTH:2 * M_WIDTH].astype(BF16)
    mo_ref[...] = pm[:, 2 * M_WIDTH:3 * M_WIDTH]
    mkt_ref[...] = _dot_nt(wkt_ref[...], hb) * (M_HEAD_DIM ** -0.5)


def _pre(x, mod3, mod_row0, mod_per_batch, wts, rope_tab, emit_cache):
    shape = x.shape
    tm = TOKEN_TILE
    if shape[1] < tm:
        assert not mod_per_batch and tm % shape[1] == 0
        x = x.reshape(-1, tm, shape[2])
    B, T, _ = x.shape
    has_rope = rope_tab is not None
    tok = lambda w: pl.BlockSpec((None, tm, w), lambda b, i: (b, i, 0))
    in_specs = [
        tok(D_MODEL),
        _const_spec((8, 6 * D_MODEL)),
        _const_spec((1, D_MODEL)),
        _const_spec((3 * M_WIDTH, D_MODEL)),
        _const_spec((M_WIDTH, D_MODEL)),
        _const_spec((LAT_WIDTH, D_MODEL)),
        _const_spec((2, LANES)),
        _const_spec((1, Q_LORA)),
        _const_spec((1, KV_LORA)),
        _const_spec((Q_LORA, QK_PAD + A_HEADS * A_ROPE if has_rope else QK_PAD)),
        _const_spec((KV_LORA, QK_PAD + A_WIDTH)),
        _const_spec((2, HEAD_PAD)),
        _const_spec((2, HEAD_PAD)),
    ]
    assert tm % M_BLOCK == 0
    args = [x, mod3, wts["g1"], wts["w_main"], wts["w_kt"], wts["w_lat"], wts["gate_bias"], wts["q_lora_g"],
            wts["kv_lora_g"], wts["w_q_rot"] if has_rope else wts["w_q"], wts["w_kv"],
            wts["q_head_g"], wts["k_head_g"]]
    if has_rope:
        in_specs.append(pl.BlockSpec((2, tm, HEAD_PAD), lambda b, i: (0, i, 0)))
        args.append(rope_tab)
    out_specs = [tok(M_WIDTH),
                 pl.BlockSpec((None, M_WIDTH, tm), lambda b, i: (b, 0, i)),
                 tok(M_WIDTH), tok(M_WIDTH),
                 pl.BlockSpec((None, 3, 2 * M_HEADS, tm), lambda b, i: (b, 0, 0, i)),
                 tok(QK_PAD), tok(QK_PAD), tok(A_WIDTH)]
    out_shape = [
        jax.ShapeDtypeStruct((B, T, M_WIDTH), BF16),
        jax.ShapeDtypeStruct((B, M_WIDTH, T), F32),
        jax.ShapeDtypeStruct((B, T, M_WIDTH), BF16),
        jax.ShapeDtypeStruct((B, T, M_WIDTH), F32),
        jax.ShapeDtypeStruct((B, 3, 2 * M_HEADS, T), F32),
        jax.ShapeDtypeStruct((B, T, QK_PAD), BF16),
        jax.ShapeDtypeStruct((B, T, QK_PAD), BF16),
        jax.ShapeDtypeStruct((B, T, A_WIDTH), BF16),
    ]
    if emit_cache:
        out_specs += [tok(KV_LORA), tok(A_ROPE)]
        out_shape += [jax.ShapeDtypeStruct((B, T, KV_LORA), F32),
                      jax.ShapeDtypeStruct((B, T, A_ROPE), F32)]
        steps, per_b = B * (T // tm), T // tm
        for w in (wts["w_out_f32"], wts["w_up_f32"], wts["w_down_f32"]):
            rows = w.shape[0] // steps
            assert rows * steps == w.shape[0] and rows % (2 * SUBLANES) == 0
            spec = pl.BlockSpec((rows, w.shape[1]), lambda b, i: (b * per_b + i, 0))
            in_specs.append(spec)
            args.append(w)
            out_specs.append(spec)
            out_shape.append(jax.ShapeDtypeStruct(w.shape, BF16))
    outs = pl.pallas_call(
        functools.partial(_pre_kernel, has_rope, emit_cache, mod_row0, mod_per_batch),
        grid=(B, T // tm),
        in_specs=in_specs,
        out_specs=out_specs,
        out_shape=out_shape,
        compiler_params=_params(("parallel", "parallel")),
        name="pre_latent" if has_rope else "pre_context",
    )(*args)
    keep = (1, 4, 10, 11, 12)
    return [o if n in keep else o.reshape(shape[:2] + o.shape[2:]) for n, o in enumerate(outs)]


def _rows_to_lane_broadcast(rows, spread):
    x = jnp.concatenate(rows, axis=0)
    p1 = x.astype(BF16)
    r1 = x - p1.astype(F32)
    p2 = r1.astype(BF16)
    p3 = (r1 - p2.astype(F32)).astype(BF16)
    pad = jnp.zeros((spread.shape[0] - 3 * len(rows), x.shape[1]), BF16)
    return _dot_tn(jnp.concatenate([p1, p2, p3, pad], axis=0), spread)


def _mlstm_gate_rows(b_row, a_row, amax_row, forward, m):
    L = b_row.shape[1]
    last = slice(L - 1, L) if forward else slice(0, 1)
    total = b_row[:, last]
    g_row = jnp.maximum(m, amax_row)
    m_new = total + jnp.maximum(m, amax_row[:, last])
    w_key_row = jnp.exp2((a_row + (total - m_new)) * LOG2E)
    decay = jnp.exp(total + m - m_new)
    return g_row * LOG2E, (b_row + g_row) * LOG2E, a_row * LOG2E, w_key_row, decay, m_new


def _mlstm_block(s_raw, q, kt, v_aug, g2, mt2, a2_row, w_key_row, decay, allow, CN, m):
    w_intra = jnp.exp2(jnp.where(allow, a2_row - jnp.concatenate([g2, g2], axis=1), -jnp.inf))
    w_inter = jnp.exp2(m * LOG2E - g2)
    s = (s_raw * w_intra).astype(BF16)
    nd = _dot(s, v_aug) + jnp.concatenate([w_inter, w_inter], axis=1) * _dot(q, CN.astype(BF16))
    num, den = nd[:, 0:M_HEAD_DIM], nd[:, M_HEAD_DIM:2 * M_HEAD_DIM]
    h = num / jnp.maximum(jnp.abs(den), jnp.exp2(-mt2))
    CN_new = decay * CN + _dot((kt * w_key_row).astype(BF16), v_aug)
    return h, CN_new


def _mlstm_kernel(has_init, emit_state, n_blocks, heads, seqs, group, *refs):
    q_ref, kt_ref, v_ref, mo_ref, stats_ref, ng_ref, spread_ref = refs[:7]
    pos = 7
    if has_init:
        c0_ref, n0_ref, m0_ref = refs[pos:pos + 3]
        pos += 3
    hm_ref = refs[pos]
    pos += 1
    if emit_state:
        c_ref, n_ref, m_ref = refs[pos:pos + 3]
        pos += 3

    L, Dh = M_BLOCK, M_HEAD_DIM
    t_idx = lax.broadcasted_iota(jnp.int32, (L, L), 0)
    s_idx = lax.broadcasted_iota(jnp.int32, (L, L), 1)
    allow = (s_idx <= t_idx, s_idx >= t_idx)
    spread = spread_ref[...]
    ones = jnp.ones((L, Dh), BF16)

    def time_lanes(sq, c):
        start = ((sq % group) * n_blocks + c) * L
        return sq // group, slice(start, start + L)

    def lane_broadcast_n(n_row):
        return jnp.broadcast_to(n_row, (Dh, Dh)).T

    def init_state(sq, j, d):
        if has_init:
            return (jnp.concatenate([c0_ref[sq, d, j], lane_broadcast_n(n0_ref[sq, d, j])], axis=1),
                    m0_ref[sq, j, d:d + 1, 0:1])
        return jnp.zeros((Dh, 2 * Dh), F32), jnp.zeros((1, 1), F32)

    def gate_rows(sq, j, c, d, m):
        head = j if heads == M_HEADS else pl.program_id(1) * heads + j
        r, (g, lanes) = pl.ds(d * M_HEADS + head, 1), time_lanes(sq, c)
        return _mlstm_gate_rows(stats_ref[g, 0, r, lanes], stats_ref[g, 1, r, lanes], stats_ref[g, 2, r, lanes],
                                d == 0, m)

    def blocks(sq, j, jobs, states):
        loaded, rows6, cols_in = {}, [], []
        for (c, d), (CN, m) in zip(jobs, states):
            rows6.append(gate_rows(sq, j, c, d, m))
            cols_in += [rows6[-1][0], rows6[-1][1]]
            if c not in loaded:
                rows, cols = slice(c * L, (c + 1) * L), slice(j * Dh, (j + 1) * Dh)
                g, lanes = time_lanes(sq, c)
                q, kt = q_ref[sq, rows, cols], kt_ref[g, cols, lanes]
                v_aug = jnp.concatenate([v_ref[sq, rows, cols], ones], axis=1)
                loaded[c] = (_dot(q, kt.astype(BF16)), q, kt, v_aug)
        cols_out = _rows_to_lane_broadcast(cols_in, spread)
        hs, new_states = [], []
        for idx, ((c, d), (CN, m)) in enumerate(zip(jobs, states)):
            g2 = cols_out[:, (2 * idx) * LANES:(2 * idx + 1) * LANES]
            mt2 = cols_out[:, (2 * idx + 1) * LANES:(2 * idx + 2) * LANES]
            _, _, a2_row, w_key_row, decay, m_new = rows6[idx]
            h, CN_new = _mlstm_block(*loaded[c], g2, mt2, a2_row, w_key_row, decay, allow[d], CN, m)
            hs.append(h)
            new_states.append((CN_new, m_new))
        return hs, new_states

    def finish(sq, j, rows, hs):
        cols = slice(j * Dh, (j + 1) * Dh)
        hn = _rms(hs, ng_ref[j])
        hm_ref[sq, rows, cols] = (hn * jax.nn.sigmoid(mo_ref[sq, rows, cols])).astype(hm_ref.dtype)

    def emit(sq, j, d, state):
        CN, m = state
        c_ref[sq, d, j] = CN[:, 0:Dh]
        n_ref[sq, d, j] = CN[:, Dh:2 * Dh].T[0:1, :]
        m_ref[sq, j, d:d + 1, :] = jnp.broadcast_to(m, (1, LANES))

    if n_blocks > 1:
        hf_scr, hb_scr = refs[pos:pos + 2]

    for sq, j in [(sq, j) for sq in range(seqs) for j in range(heads)]:
        cols = slice(j * Dh, (j + 1) * Dh)
        states = [init_state(sq, j, 0), init_state(sq, j, 1)]
        if n_blocks == 1:
            (hf, hb), states = blocks(sq, j, [(0, 0), (0, 1)], states)
            finish(sq, j, slice(0, L), hf + hb)
        else:
            for step in range(n_blocks):
                cf, cb = step, n_blocks - 1 - step
                (hf, hb), states = blocks(sq, j, [(cf, 0), (cb, 1)], states)
                hf_scr[cf * L:(cf + 1) * L, cols] = hf
                hb_scr[cb * L:(cb + 1) * L, cols] = hb
            finish(sq, j, slice(None), hf_scr[:, cols] + hb_scr[:, cols])
        if emit_state:
            emit(sq, j, 0, states[0])
            emit(sq, j, 1, states[1])


def _mlstm(mq, mkt, mv, mo, stats, norm_g, init_state, emit_state, heads, seqs):
    B, T, _ = mq.shape
    H, Dh, L = M_HEADS, M_HEAD_DIM, M_BLOCK
    nb = T // L
    w = heads * Dh
    has_init = init_state is not None
    n_rows = 4
    spread = jnp.tile(jnp.repeat(jnp.eye(n_rows, dtype=BF16), LANES, axis=1), (3, 1))
    spread = jnp.pad(spread, ((0, 2 * SUBLANES - 3 * n_rows), (0, 0)))
    tok = pl.BlockSpec((seqs, T, w), lambda b, h: (b, 0, h))
    group = mkt.shape[2] // T
    assert seqs % group == 0
    assert mkt.shape == (B // group, M_WIDTH, group * T) and stats.shape == (B // group, 3, 2 * H, group * T)
    in_specs = [tok, pl.BlockSpec((seqs // group, w, group * T), lambda b, h: (b, h, 0)), tok, tok,
                pl.BlockSpec((seqs // group, 3, 2 * H, group * T), lambda b, h: (b, 0, 0, 0)),
                pl.BlockSpec((heads, 1, Dh), lambda b, h: (h, 0, 0)),
                _const_spec((2 * SUBLANES, n_rows * LANES))]
    args = [mq, mkt, mv, mo, stats, norm_g, spread]
    state_specs = [pl.BlockSpec((seqs, 2, heads, Dh, Dh), lambda b, h: (b, 0, h, 0, 0)),
                   pl.BlockSpec((seqs, 2, heads, 1, Dh), lambda b, h: (b, 0, h, 0, 0)),
                   pl.BlockSpec((seqs, heads, 2, LANES), lambda b, h: (b, h, 0, 0))]
    if has_init:
        in_specs += state_specs
        args += list(init_state)
    out_specs = [tok]
    out_shape = [jax.ShapeDtypeStruct((B, T, M_WIDTH), BF16)]
    if emit_state:
        out_specs += state_specs
        out_shape += [jax.ShapeDtypeStruct((B, 2, H, Dh, Dh), F32),
                      jax.ShapeDtypeStruct((B, 2, H, 1, Dh), F32),
                      jax.ShapeDtypeStruct((B, H, 2, LANES), F32)]
    scratch = [] if nb == 1 else [pltpu.VMEM((T, w), F32), pltpu.VMEM((T, w), F32)]
    return pl.pallas_call(
        functools.partial(_mlstm_kernel, has_init, emit_state, nb, heads, seqs, group),
        grid=(B // seqs, H // heads),
        in_specs=in_specs,
        out_specs=out_specs,
        out_shape=out_shape,
        scratch_shapes=scratch,
        compiler_params=_params(("parallel", "parallel")),
        name="mlstm_latent" if has_init else "mlstm_context",
    )(*args)


def _attn_kernel(has_ctx, seqs, *refs):
    if has_ctx:
        q_ref, k_ref, v_ref, ckv_ref, krp_ref, wkv_ref, khg_ref, o_ref, kc_ref, vc_ref = refs

        @pl.when(pl.program_id(1) == 0)
        def _():
            kvf = _dot(ckv_ref[0].astype(BF16), wkv_ref[...])
            vc_ref[...] = kvf[:, QK_PAD:QK_PAD + A_WIDTH].astype(BF16)
            _write_heads(kvf, krp_ref[0], khg_ref[0:1, :], None, kc_ref)
    else:
        q_ref, k_ref, v_ref, o_ref = refs
    lane = lax.broadcasted_iota(jnp.int32, (1, LANES), 1)
    ones = lambda n: jnp.ones((n, LANES), BF16)
    tq = q_ref.shape[1]
    sub = min(Q_SUBTILE, tq)
    for sq, r0, pair in [(sq, r0, pair) for sq in range(seqs) for r0 in range(0, tq, sub)
                         for pair in range(A_HEADS // 2)]:
        rows = slice(r0, r0 + sub)
        vsl = slice(pair * LANES, (pair + 1) * LANES)
        v_aug = jnp.concatenate([v_ref[sq, :, vsl], ones(v_ref.shape[1])], axis=1)
        if has_ctx:
            vc_aug = jnp.concatenate([vc_ref[:, vsl], ones(vc_ref.shape[0])], axis=1)
        outs = []
        for e in range(2):
            hsl = slice((2 * pair + e) * HEAD_PAD, (2 * pair + e + 1) * HEAD_PAD)
            qh = q_ref[sq, rows, hsl]
            s = _dot_nt(qh, k_ref[sq, :, hsl])
            mx = jnp.max(s, axis=1, keepdims=True)
            if has_ctx:
                sc = _dot_nt(qh, kc_ref[:, hsl])
                mx = jnp.maximum(mx, jnp.max(sc, axis=1, keepdims=True))
            od = _dot(jnp.exp2(s - mx).astype(BF16), v_aug)
            if has_ctx:
                od = od + _dot(jnp.exp2(sc - mx).astype(BF16), vc_aug)
            outs.append(od[:, 0:LANES] / od[:, LANES:2 * LANES])
        o_ref[sq, rows, vsl] = jnp.where(lane < A_VDIM, outs[0], outs[1]).astype(o_ref.dtype)


def _attn(q, k, v, ctx, seqs):
    B, T, _ = q.shape
    tq = min(Q_TILE, T)
    has_ctx = ctx is not None
    full = lambda n, w: pl.BlockSpec((seqs, n, w), lambda b, i: (b, 0, 0))
    in_specs = [pl.BlockSpec((seqs, tq, QK_PAD), lambda b, i: (b, i, 0)), full(T, QK_PAD), full(T, A_WIDTH)]
    args = [q, k, v]
    scratch = []
    if has_ctx:
        assert seqs == 1
        P = ctx[0].shape[1]
        in_specs += [full(P, KV_LORA), full(P, HEAD_PAD), _const_spec((KV_LORA, QK_PAD + A_WIDTH)),
                     _const_spec((2, HEAD_PAD))]
        args += list(ctx)
        scratch = [pltpu.VMEM((P, QK_PAD), BF16), pltpu.VMEM((P, A_WIDTH), BF16)]
    return pl.pallas_call(
        functools.partial(_attn_kernel, has_ctx, seqs),
        grid=(B // seqs, T // tq),
        in_specs=in_specs,
        out_specs=pl.BlockSpec((seqs, tq, A_WIDTH), lambda b, i: (b, i, 0)),
        out_shape=jax.ShapeDtypeStruct((B, T, A_WIDTH), BF16),
        scratch_shapes=scratch,
        compiler_params=_params(("parallel", "arbitrary" if has_ctx else "parallel")),
        name="attn_latent" if has_ctx else "attn_context",
    )(*args)


def _post_kernel(mod_row0, mod_per_batch, x_ref, hm_ref, ha_ref, mod_ref, g2_ref, wout_ref, wup_ref, wdown_ref,
                 y_ref):
    mod = _mod_row(mod_ref, mod_row0, mod_per_batch)
    gate1 = mod[:, 2 * D_MODEL:3 * D_MODEL]
    sh2 = mod[:, 3 * D_MODEL:4 * D_MODEL]
    sc2 = mod[:, 4 * D_MODEL:5 * D_MODEL]
    gate2 = mod[:, 5 * D_MODEL:6 * D_MODEL]
    mix = jnp.concatenate([hm_ref[...], ha_ref[...]], axis=-1)
    x1 = x_ref[...] + gate1 * _dot(mix, wout_ref[...])
    h2 = (_rms(x1, g2_ref[...]) * (1.0 + sc2) + sh2).astype(BF16)
    acc = jnp.zeros_like(x1)
    for c in range(D_FF // FF_TILE):
        sl = slice(c * FF_TILE, (c + 1) * FF_TILE)
        u = jnp.maximum(_dot(h2, wup_ref[:, sl]), 0.0)
        acc = acc + _dot((u * u).astype(BF16), wdown_ref[sl, :])
    y_ref[...] = x1 + gate2 * acc


def _post(x, hm, ha, mod3, mod_row0, mod_per_batch, wts):
    shape = x.shape
    if not mod_per_batch:
        x, hm, ha = (a.reshape(1, -1, a.shape[-1]) for a in (x, hm, ha))
    B, T, _ = x.shape
    tm = POST_TILE
    tok = lambda w: pl.BlockSpec((None, tm, w), lambda b, i: (b, i, 0))
    return _post_call(x, hm, ha, mod3, mod_row0, mod_per_batch, wts, B, T, tm, tok).reshape(shape)


def _post_call(x, hm, ha, mod3, mod_row0, mod_per_batch, wts, B, T, tm, tok):
    return pl.pallas_call(
        functools.partial(_post_kernel, mod_row0, mod_per_batch),
        grid=(B, T // tm),
        in_specs=[tok(D_MODEL), tok(M_WIDTH), tok(A_WIDTH),
                  _const_spec((8, 6 * D_MODEL)),
                  _const_spec((1, D_MODEL)),
                  _const_spec((M_WIDTH + A_WIDTH, D_MODEL)),
                  _const_spec((D_MODEL, D_FF)),
                  _const_spec((D_FF, D_MODEL))],
        out_specs=tok(D_MODEL),
        out_shape=jax.ShapeDtypeStruct((B, T, D_MODEL), F32),
        compiler_params=_params(("parallel", "parallel")),
        name="post",
    )(x, hm, ha, mod3, wts["g2"], wts["w_out"], wts["w_up"], wts["w_down"])


def _prepare_weights(norm1_g, norm2_g, w_in, mlstm_gate_b, q_lora_g, kv_lora_g, w_q_up, w_kv_up,
                     q_head_g, k_head_g, w_out, w_mlp_up, w_mlp_down):
    o_g = 4 * M_WIDTH
    o_q = o_g + N_GATES
    o_kv = o_q + Q_LORA
    o_kr = o_kv + KV_LORA
    half = A_ROPE // 2
    n_dh = 2 * M_HEADS
    wt = w_in.T.astype(BF16)
    w_gate = wt[o_g:o_q].reshape(2, 2, M_HEADS, D_MODEL)
    bias = mlstm_gate_b.reshape(2, 2, M_HEADS)

    def rot_partner(a):
        z = jnp.zeros(a.shape[:-1] + (A_NOPE,), a.dtype)
        return jnp.concatenate([z, a[..., A_NOPE + half:A_QK], a[..., A_NOPE:A_NOPE + half]], axis=-1)

    pad_tile = lambda a: jnp.pad(a, [(0, 0)] * (a.ndim - 1) + [(0, HEAD_PAD - A_QK)])
    w_kr = wt[o_kr:o_kr + A_ROPE]
    w_kr_partner = jnp.concatenate([w_kr[half:], w_kr[:half]], axis=0)
    w_lat = jnp.concatenate([wt[o_q:o_kr], w_gate[:, 0].reshape(n_dh, D_MODEL), w_gate[:, 1].reshape(n_dh, D_MODEL),
                             jnp.zeros((A_NOPE - 2 * n_dh, D_MODEL), BF16), w_kr, w_kr_partner], axis=0)
    gate_bias = jnp.pad(jnp.stack([bias[:, 0, :].reshape(n_dh), bias[:, 1, :].reshape(n_dh)], axis=0),
                        ((0, 0), (0, LANES - n_dh)))
    w_q3 = w_q_up.reshape(Q_LORA, A_HEADS, A_QK)
    w_q = pad_tile(w_q3).reshape(Q_LORA, QK_PAD)
    w_q_partner = jnp.concatenate([w_q3[:, :, A_NOPE + half:], w_q3[:, :, A_NOPE:A_NOPE + half]],
                                  axis=-1).reshape(Q_LORA, A_HEADS * A_ROPE)
    w_kv3 = w_kv_up.reshape(KV_LORA, A_HEADS, A_NOPE + A_VDIM)
    w_k = jnp.pad(w_kv3[:, :, :A_NOPE], ((0, 0), (0, 0), (0, HEAD_PAD - A_NOPE)))
    w_v = w_kv3[:, :, A_NOPE:]
    w_kv = jnp.concatenate([w_k.reshape(KV_LORA, QK_PAD), w_v.reshape(KV_LORA, A_WIDTH)], axis=1)
    pad_head = lambda g: jnp.stack([pad_tile(g), pad_tile(rot_partner(g))], axis=0)
    return {
        "g1": norm1_g.reshape(1, D_MODEL),
        "g2": norm2_g.reshape(1, D_MODEL),
        "w_main": jnp.concatenate([wt[0:M_WIDTH], wt[2 * M_WIDTH:o_g]], axis=0),
        "w_kt": wt[M_WIDTH:2 * M_WIDTH],
        "w_lat": w_lat,
        "gate_bias": gate_bias,
        "q_lora_g": q_lora_g.reshape(1, Q_LORA),
        "kv_lora_g": kv_lora_g.reshape(1, KV_LORA),
        "w_q": w_q.astype(BF16),
        "w_q_rot": jnp.concatenate([w_q, w_q_partner], axis=1).astype(BF16),
        "w_kv": w_kv.astype(BF16),
        "q_head_g": pad_head(q_head_g),
        "k_head_g": pad_head(k_head_g),
        "w_out_f32": w_out,
        "w_up_f32": w_mlp_up,
        "w_down_f32": w_mlp_down,
    }


def _rope_tables(T):
    rows = T // GRID_W
    row = np.repeat(np.arange(rows, dtype=np.float32), GRID_W)
    col = np.tile(np.arange(GRID_W, dtype=np.float32), rows)
    half = A_ROPE // 2
    inv = (np.float32(ROPE_BASE) ** (-np.arange(0, half, 2, dtype=np.float32) / np.float32(half))).astype(np.float32)
    ang = np.concatenate([row[:, None] * inv, col[:, None] * inv], axis=-1)
    cos, sin = np.cos(ang), np.sin(ang)
    ones = np.ones((T, A_NOPE), np.float32)
    z = lambda w: np.zeros((T, w), np.float32)
    tail = LANES - A_QK
    cos_t = np.concatenate([ones, cos, cos, z(tail)], axis=1)
    sin_t = np.concatenate([z(A_NOPE), -sin, sin, z(tail)], axis=1)
    return jnp.asarray(np.stack([cos_t, sin_t], axis=0).astype(np.float32))


def _layer_pass(x, mod3, mod_row0, mod_per_batch, wts, norm_g, rope_tab, init_state, ctx_kv, is_context):
    pre = _pre(x, mod3, mod_row0, mod_per_batch, wts, rope_tab, emit_cache=is_context)
    mq, mkt, mv, mo, stats, q, k, v = pre[:8]
    if is_context:
        wts = dict(wts, w_out=pre[10], w_up=pre[11], w_down=pre[12])
    ml = _mlstm(mq, mkt, mv, mo, stats, norm_g, init_state, emit_state=is_context,
                heads=M_HEADS if is_context else 1, seqs=MLSTM_CONTEXT_SEQS if is_context else 1)
    ha = _attn(q, k, v, ctx_kv, seqs=ATTN_CONTEXT_SEQS if is_context else 1)
    y = _post(x, ml[0], ha, mod3, mod_row0, mod_per_batch, wts)
    return y, pre[8:10], ml[1:], wts


def kernel(x_prompt, x_sample, cache_mla_ckv, cache_mla_krope, state_mlstm_C, state_mlstm_n, state_mlstm_m,
           c, c_ctx, norm1_g, norm2_g, w_ada, b_ada, w_in, mlstm_gate_b, mlstm_norm_g,
           q_lora_g, kv_lora_g, w_q_up, w_kv_up, q_head_g, k_head_g, w_out, w_mlp_up, w_mlp_down):
    depth = w_in.shape[0]
    Bd = x_sample.shape[0]
    cond8 = jnp.concatenate([c_ctx[None, :], c, jnp.zeros((8 - 1 - Bd, D_MODEL), F32)], axis=0)
    rope_tab = _rope_tables(x_sample.shape[1])

    y, z = x_prompt, x_sample
    ckvs, kropes, Cs, ns, ms = [], [], [], [], []
    for l in range(depth):
        wts = _prepare_weights(norm1_g[l], norm2_g[l], w_in[l], mlstm_gate_b[l], q_lora_g[l], kv_lora_g[l],
                               w_q_up[l], w_kv_up[l], q_head_g[l], k_head_g[l], w_out[l], w_mlp_up[l],
                               w_mlp_down[l])
        norm_g = mlstm_norm_g[l].reshape(M_HEADS, 1, M_HEAD_DIM)
        mod3 = _ada(cond8, w_ada[l], b_ada[l])
        y, (ckv, krope), (C_new, n_new, m_new), wts = _layer_pass(
            y, mod3, 0, 0, wts, norm_g, None, None, None, True)
        ckvs.append(ckv)
        kropes.append(krope)
        Cs.append(C_new)
        ns.append(n_new[:, :, :, 0, :])
        ms.append(m_new[:, :, :, 0].transpose(0, 2, 1))

        init_state = (state_mlstm_C[:, l],
                      state_mlstm_n[:, l][:, :, :, None, :],
                      jnp.broadcast_to(state_mlstm_m[:, l].transpose(0, 2, 1)[..., None],
                                       (Bd, M_HEADS, 2, LANES)))
        krope_placed = jnp.pad(cache_mla_krope[:, l], ((0, 0), (0, 0), (A_NOPE, LANES - A_QK)))
        ctx = (cache_mla_ckv[:, l], krope_placed, wts["w_kv"], wts["k_head_g"])
        z, _, _, _ = _layer_pass(z, mod3, 1, 1, wts, norm_g, rope_tab, init_state, ctx, False)

    return (y, z, jnp.stack(ckvs, axis=1), jnp.stack(kropes, axis=1), jnp.stack(Cs, axis=1),
            jnp.stack(ns, axis=1), jnp.stack(ms, axis=1))
```

```python
import functools

import jax
import jax.numpy as jnp
import numpy as np
from jax import lax
from jax.experimental import pallas as pl
from jax.experimental.pallas import tpu as pltpu

F32 = jnp.float32
BF16 = jnp.bfloat16

D_MODEL = 1024
GRID_W = 64
M_HEADS = 4
M_HEAD_DIM = 128
M_WIDTH = M_HEADS * M_HEAD_DIM
M_BLOCK = 256
A_HEADS = 8
A_NOPE = 64
A_ROPE = 32
A_QK = A_NOPE + A_ROPE
A_VDIM = 64
A_WIDTH = A_HEADS * A_VDIM
Q_LORA = 384
KV_LORA = 256
ROPE_BASE = 10000.0
D_FF = 4 * D_MODEL
EPS = 1e-6

LANES = 128
SUBLANES = 8
LOG2E = 1.4426950408889634
HEAD_PAD = LANES
QK_PAD = A_HEADS * HEAD_PAD
K_NOPE_WIDTH = A_HEADS * A_NOPE
N_GATES = 4 * M_HEADS
LAT_WIDTH = Q_LORA + KV_LORA + LANES
VMEM_LIMIT = 56 * 1024 * 1024

TOKEN_TILE = 512
POST_TILE = 512
MLSTM_CONTEXT_SEQS = 4
ATTN_CONTEXT_SEQS = 8
Q_TILE = 1024
Q_SUBTILE = 256
ADA_TILE_N = 1536
FF_TILE = 1024


def _dot(a, b):
    return jnp.dot(a, b, preferred_element_type=F32)


def _dot_nt(a, b):
    return lax.dot_general(a, b, (((1,), (1,)), ((), ())), preferred_element_type=F32)


def _dot_tn(a, b):
    return lax.dot_general(a, b, (((0,), (0,)), ((), ())), preferred_element_type=F32)


def _rms(x, g):
    y = x * lax.rsqrt(jnp.mean(x * x, axis=-1, keepdims=True) + EPS)
    return y * g


def _params(sem):
    return pltpu.CompilerParams(dimension_semantics=sem, vmem_limit_bytes=VMEM_LIMIT)


def _const_spec(shape):
    zeros = (0,) * len(shape)
    return pl.BlockSpec(shape, lambda *_: zeros, pipeline_mode=pl.Buffered(1))


def _ada_kernel(cond_ref, w_ref, b_ref, o_ref):
    c = cond_ref[...]
    s = (c * jax.nn.sigmoid(c)).astype(BF16)
    o_ref[...] = _dot(s, w_ref[...].astype(BF16)) + b_ref[...]


def _ada(cond8, w_ada, b_ada):
    n = w_ada.shape[1]
    return pl.pallas_call(
        _ada_kernel,
        grid=(n // ADA_TILE_N,),
        in_specs=[
            pl.BlockSpec((8, D_MODEL), lambda j: (0, 0)),
            pl.BlockSpec((D_MODEL, ADA_TILE_N), lambda j: (0, j)),
            pl.BlockSpec((1, ADA_TILE_N), lambda j: (0, j)),
        ],
        out_specs=pl.BlockSpec((8, ADA_TILE_N), lambda j: (0, j)),
        out_shape=jax.ShapeDtypeStruct((8, n), F32),
        compiler_params=_params(("parallel",)),
        name="ada",
    )(cond8, w_ada, b_ada.reshape(1, n))


def _k_nope_tiles(kvf):
    lane = lax.broadcasted_iota(jnp.int32, (1, LANES), 1)
    tiles = []
    for h in range(A_HEADS):
        packed = kvf[:, (h // 2) * LANES:(h // 2 + 1) * LANES]
        if h % 2:
            packed = pltpu.roll(packed, A_NOPE, 1)
        tiles.append(jnp.where(lane < A_NOPE, packed, 0.0))
    return tiles


def _write_heads(tiles, extra, g_pad, rot, dst_ref):
    for h in range(A_HEADS):
        sl = slice(h * HEAD_PAD, (h + 1) * HEAD_PAD)
        xh = tiles[h]
        if extra is not None:
            xh = xh + extra
        ss = jnp.sum(xh * xh, axis=-1, keepdims=True) * (1.0 / A_QK)
        r = lax.rsqrt(ss + EPS)
        if rot is None:
            y = xh * r * g_pad
        else:
            partners, cos_g, sin_g = rot
            y = (xh * cos_g + partners[h] * sin_g) * r
        dst_ref[:, sl] = y.astype(dst_ref.dtype)


def _time_scan(x, op, identity, reverse):
    n = x.shape[0]
    row = lax.broadcasted_iota(jnp.int32, x.shape, 0)
    shift = 1
    while shift < n:
        if shift < SUBLANES:
            if reverse:
                moved = jnp.where(row < n - shift, pltpu.roll(x, n - shift, 0), identity)
            else:
                moved = jnp.where(row >= shift, pltpu.roll(x, shift, 0), identity)
        else:
            fill = jnp.full((shift, x.shape[1]), identity, x.dtype)
            moved = (jnp.concatenate([x[shift:], fill], axis=0) if reverse
                     else jnp.concatenate([fill, x[:n - shift]], axis=0))
        x = op(x, moved)
        shift *= 2
    return x


def _mod_row(mod_ref, row0, per_batch):
    if per_batch:
        return mod_ref[pl.ds(row0 + pl.program_id(0) * per_batch, 1), :]
    return mod_ref[row0:row0 + 1, :]


def _pre_kernel(has_rope, emit_cache, mod_row0, mod_per_batch, *refs):
    (x_ref, mod_ref, g1_ref, wmain_ref, wkt_ref, wlat_ref, gbias_ref, qlg_ref, kvg_ref, wq_ref, wkv_ref,
     qhg_ref, khg_ref) = refs[:13]
    pos = 13
    if has_rope:
        rope_ref = refs[pos]
        pos += 1
    if emit_cache:
        cast_in = refs[pos:pos + 3]
        pos += 3
    (mq_ref, mkt_ref, mv_ref, mo_ref, stats_ref, q_ref, k_ref, v_ref) = refs[pos:pos + 8]
    pos += 8

    x = x_ref[...]
    mod = _mod_row(mod_ref, mod_row0, mod_per_batch)
    sh1 = mod[:, 0:D_MODEL]
    sc1 = mod[:, D_MODEL:2 * D_MODEL]
    h = _rms(x, g1_ref[...]) * (1.0 + sc1) + sh1
    hb = h.astype(BF16)

    plat = _dot_nt(hb, wlat_ref[...])
    q_lat = plat[:, 0:Q_LORA]
    kv_lat = plat[:, Q_LORA:Q_LORA + KV_LORA]
    tail = plat[:, Q_LORA + KV_LORA:LAT_WIDTH]
    tail_f = pltpu.roll(tail, LANES - 2 * M_HEADS, 1)
    tail2 = pltpu.roll(tail, LANES - A_ROPE, 1)

    lane = lax.broadcasted_iota(jnp.int32, (1, LANES), 1)
    fwd = lane < M_HEADS
    gate_i = tail + gbias_ref[0:1, :]
    gate_f = tail_f + gbias_ref[1:2, :]
    log_f = jnp.minimum(gate_f, 0.0) - jnp.log1p(jnp.exp(-jnp.abs(gate_f)))
    def block_scan(v, op, identity):
        parts = [v[r:r + M_BLOCK] for r in range(0, v.shape[0], M_BLOCK)]
        return jnp.where(fwd, jnp.concatenate([_time_scan(p, op, identity, False) for p in parts], axis=0),
                         jnp.concatenate([_time_scan(p, op, identity, True) for p in parts], axis=0))

    b = block_scan(log_f, jnp.add, 0.0)
    a = gate_i - b
    amax = block_scan(a, jnp.maximum, -jnp.inf)
    stats_ref[0] = b.T[0:2 * M_HEADS, :]
    stats_ref[1] = a.T[0:2 * M_HEADS, :]
    stats_ref[2] = amax.T[0:2 * M_HEADS, :]

    krope_placed = jnp.where((lane >= A_NOPE) & (lane < A_QK), tail, 0.0)
    ckv = _rms(kv_lat, kvg_ref[...])
    qn = _rms(q_lat, qlg_ref[...])
    qf = _dot(qn.astype(BF16), wq_ref[...])
    kvf = _dot(ckv.astype(BF16), wkv_ref[...])
    v_ref[...] = kvf[:, K_NOPE_WIDTH:K_NOPE_WIDTH + A_WIDTH].astype(BF16)
    q_rot = k_rot = None
    qhg = qhg_ref[...] * (A_QK ** -0.5 * LOG2E)
    if has_rope:
        cos_t, sin_t = rope_ref[0], rope_ref[1]
        q_partners = []
        for h in range(A_HEADS):
            packed = qf[:, QK_PAD + (h // 4) * LANES:QK_PAD + (h // 4 + 1) * LANES]
            shift = (A_NOPE - (h % 4) * A_ROPE) % LANES
            q_partners.append(pltpu.roll(packed, shift, 1) if shift else packed)
        q_rot = (q_partners, cos_t * qhg[0:1, :], sin_t * qhg[1:2, :])
        k_rot = ([tail2] * A_HEADS, cos_t * khg_ref[0:1, :], sin_t * khg_ref[1:2, :])
    _write_heads([qf[:, h * HEAD_PAD:(h + 1) * HEAD_PAD] for h in range(A_HEADS)], None, qhg[0:1, :], q_rot, q_ref)
    _write_heads(_k_nope_tiles(kvf), krope_placed, khg_ref[0:1, :], k_rot, k_ref)

    if emit_cache:
        ckv_ref, krope_ref = refs[pos:pos + 2]
        ckv_ref[...] = ckv
        krope_ref[...] = tail[:, A_NOPE:A_QK]
        for src_ref, dst_ref in zip(cast_in, refs[pos + 2:pos + 5]):
            dst_ref[...] = src_ref[...].astype(BF16)

    pm = _dot_nt(hb, wmain_ref[...])
    mq_ref[...] = pm[:, 0:M_WIDTH].astype(BF16)
    mv_ref[...] = pm[:, M_WIDTH:2 * M_WIDTH].astype(BF16)
    mo_ref[...] = pm[:, 2 * M_WIDTH:3 * M_WIDTH]
    mkt_ref[...] = _dot_nt(wkt_ref[...], hb) * (M_HEAD_DIM ** -0.5)


def _pre(x, mod3, mod_row0, mod_per_batch, wts, rope_tab, emit_cache):
    shape = x.shape
    tm = TOKEN_TILE
    if shape[1] < tm:
        assert not mod_per_batch and tm % shape[1] == 0
        x = x.reshape(-1, tm, shape[2])
    B, T, _ = x.shape
    has_rope = rope_tab is not None
    tok = lambda w: pl.BlockSpec((None, tm, w), lambda b, i: (b, i, 0))
    in_specs = [
        tok(D_MODEL),
        _const_spec((8, 6 * D_MODEL)),
        _const_spec((1, D_MODEL)),
        _const_spec((3 * M_WIDTH, D_MODEL)),
        _const_spec((M_WIDTH, D_MODEL)),
        _const_spec((LAT_WIDTH, D_MODEL)),
        _const_spec((2, LANES)),
        _const_spec((1, Q_LORA)),
        _const_spec((1, KV_LORA)),
        _const_spec((Q_LORA, QK_PAD + A_HEADS * A_ROPE if has_rope else QK_PAD)),
        _const_spec((KV_LORA, K_NOPE_WIDTH + A_WIDTH)),
        _const_spec((2, HEAD_PAD)),
        _const_spec((2, HEAD_PAD)),
    ]
    assert tm % M_BLOCK == 0
    args = [x, mod3, wts["g1"], wts["w_main"], wts["w_kt"], wts["w_lat"], wts["gate_bias"], wts["q_lora_g"],
            wts["kv_lora_g"], wts["w_q_rot"] if has_rope else wts["w_q"], wts["w_kv"],
            wts["q_head_g"], wts["k_head_g"]]
    if has_rope:
        in_specs.append(pl.BlockSpec((2, tm, HEAD_PAD), lambda b, i: (0, i, 0)))
        args.append(rope_tab)
    out_specs = [tok(M_WIDTH),
                 pl.BlockSpec((None, M_WIDTH, tm), lambda b, i: (b, 0, i)),
                 tok(M_WIDTH), tok(M_WIDTH),
                 pl.BlockSpec((None, 3, 2 * M_HEADS, tm), lambda b, i: (b, 0, 0, i)),
                 tok(QK_PAD), tok(QK_PAD), tok(A_WIDTH)]
    out_shape = [
        jax.ShapeDtypeStruct((B, T, M_WIDTH), BF16),
        jax.ShapeDtypeStruct((B, M_WIDTH, T), F32),
        jax.ShapeDtypeStruct((B, T, M_WIDTH), BF16),
        jax.ShapeDtypeStruct((B, T, M_WIDTH), F32),
        jax.ShapeDtypeStruct((B, 3, 2 * M_HEADS, T), F32),
        jax.ShapeDtypeStruct((B, T, QK_PAD), BF16),
        jax.ShapeDtypeStruct((B, T, QK_PAD), BF16),
        jax.ShapeDtypeStruct((B, T, A_WIDTH), BF16),
    ]
    if emit_cache:
        out_specs += [tok(KV_LORA), tok(A_ROPE)]
        out_shape += [jax.ShapeDtypeStruct((B, T, KV_LORA), F32),
                      jax.ShapeDtypeStruct((B, T, A_ROPE), F32)]
        steps, per_b = B * (T // tm), T // tm
        for w in (wts["w_out_f32"], wts["w_up_f32"], wts["w_down_f32"]):
            rows = w.shape[0] // steps
            assert rows * steps == w.shape[0] and rows % (2 * SUBLANES) == 0
            spec = pl.BlockSpec((rows, w.shape[1]), lambda b, i: (b * per_b + i, 0))
            in_specs.append(spec)
            args.append(w)
            out_specs.append(spec)
            out_shape.append(jax.ShapeDtypeStruct(w.shape, BF16))
    outs = pl.pallas_call(
        functools.partial(_pre_kernel, has_rope, emit_cache, mod_row0, mod_per_batch),
        grid=(B, T // tm),
        in_specs=in_specs,
        out_specs=out_specs,
        out_shape=out_shape,
        compiler_params=_params(("parallel", "parallel")),
        name="pre_latent" if has_rope else "pre_context",
    )(*args)
    keep = (1, 4, 10, 11, 12)
    return [o if n in keep else o.reshape(shape[:2] + o.shape[2:]) for n, o in enumerate(outs)]


def _rows_to_lane_broadcast(rows, spread):
    x = jnp.concatenate(rows, axis=0)
    p1 = x.astype(BF16)
    r1 = x - p1.astype(F32)
    p2 = r1.astype(BF16)
    p3 = (r1 - p2.astype(F32)).astype(BF16)
    pad = jnp.zeros((spread.shape[0] - 3 * len(rows), x.shape[1]), BF16)
    return _dot_tn(jnp.concatenate([p1, p2, p3, pad], axis=0), spread)


def _mlstm_gate_rows(b_row, a_row, amax_row, forward, m):
    L = b_row.shape[1]
    last = slice(L - 1, L) if forward else slice(0, 1)
    total = b_row[:, last]
    g_row = jnp.maximum(m, amax_row)
    m_new = total + jnp.maximum(m, amax_row[:, last])
    w_key_row = jnp.exp2((a_row + (total - m_new)) * LOG2E)
    decay = jnp.exp(total + m - m_new)
    return g_row * LOG2E, (b_row + g_row) * LOG2E, a_row * LOG2E, w_key_row, decay, m_new


def _mlstm_block(s_raw, q, kt, v_aug, g2, mt2, a2_row, w_key_row, decay, allow, CN, m):
    w_intra = jnp.exp2(jnp.where(allow, a2_row - jnp.concatenate([g2, g2], axis=1), -jnp.inf))
    w_inter = jnp.exp2(m * LOG2E - g2)
    s = (s_raw * w_intra).astype(BF16)
    nd = _dot(s, v_aug) + jnp.concatenate([w_inter, w_inter], axis=1) * _dot(q, CN.astype(BF16))
    num, den = nd[:, 0:M_HEAD_DIM], nd[:, M_HEAD_DIM:2 * M_HEAD_DIM]
    h = num / jnp.maximum(jnp.abs(den), jnp.exp2(-mt2))
    CN_new = decay * CN + _dot((kt * w_key_row).astype(BF16), v_aug)
    return h, CN_new


def _mlstm_kernel(has_init, emit_state, n_blocks, heads, seqs, group, *refs):
    q_ref, kt_ref, v_ref, mo_ref, stats_ref, ng_ref, spread_ref = refs[:7]
    pos = 7
    if has_init:
        c0_ref, n0_ref, m0_ref = refs[pos:pos + 3]
        pos += 3
    hm_ref = refs[pos]
    pos += 1
    if emit_state:
        c_ref, n_ref, m_ref = refs[pos:pos + 3]
        pos += 3

    L, Dh = M_BLOCK, M_HEAD_DIM
    t_idx = lax.broadcasted_iota(jnp.int32, (L, L), 0)
    s_idx = lax.broadcasted_iota(jnp.int32, (L, L), 1)
    allow = (s_idx <= t_idx, s_idx >= t_idx)
    spread = spread_ref[...]
    ones = jnp.ones((L, Dh), BF16)

    def time_lanes(sq, c):
        start = ((sq % group) * n_blocks + c) * L
        return sq // group, slice(start, start + L)

    def lane_broadcast_n(n_row):
        return jnp.broadcast_to(n_row, (Dh, Dh)).T

    def init_state(sq, j, d):
        if has_init:
            return (jnp.concatenate([c0_ref[sq, d, j], lane_broadcast_n(n0_ref[sq, d, j])], axis=1),
                    m0_ref[sq, j, d:d + 1, 0:1])
        return jnp.zeros((Dh, 2 * Dh), F32), jnp.zeros((1, 1), F32)

    def gate_rows(sq, j, c, d, m):
        head = j if heads == M_HEADS else pl.program_id(1) * heads + j
        r, (g, lanes) = pl.ds(d * M_HEADS + head, 1), time_lanes(sq, c)
        return _mlstm_gate_rows(stats_ref[g, 0, r, lanes], stats_ref[g, 1, r, lanes], stats_ref[g, 2, r, lanes],
                                d == 0, m)

    def blocks(sq, j, jobs, states):
        loaded, rows6, cols_in = {}, [], []
        for (c, d), (CN, m) in zip(jobs, states):
            rows6.append(gate_rows(sq, j, c, d, m))
            cols_in += [rows6[-1][0], rows6[-1][1]]
            if c not in loaded:
                rows, cols = slice(c * L, (c + 1) * L), slice(j * Dh, (j + 1) * Dh)
                g, lanes = time_lanes(sq, c)
                q, kt = q_ref[sq, rows, cols], kt_ref[g, cols, lanes]
                v_aug = jnp.concatenate([v_ref[sq, rows, cols], ones], axis=1)
                loaded[c] = (_dot(q, kt.astype(BF16)), q, kt, v_aug)
        cols_out = _rows_to_lane_broadcast(cols_in, spread)
        hs, new_states = [], []
        for idx, ((c, d), (CN, m)) in enumerate(zip(jobs, states)):
            g2 = cols_out[:, (2 * idx) * LANES:(2 * idx + 1) * LANES]
            mt2 = cols_out[:, (2 * idx + 1) * LANES:(2 * idx + 2) * LANES]
            _, _, a2_row, w_key_row, decay, m_new = rows6[idx]
            h, CN_new = _mlstm_block(*loaded[c], g2, mt2, a2_row, w_key_row, decay, allow[d], CN, m)
            hs.append(h)
            new_states.append((CN_new, m_new))
        return hs, new_states

    def finish(sq, j, rows, hs):
        cols = slice(j * Dh, (j + 1) * Dh)
        hn = _rms(hs, ng_ref[j])
        hm_ref[sq, rows, cols] = (hn * jax.nn.sigmoid(mo_ref[sq, rows, cols])).astype(hm_ref.dtype)

    def emit(sq, j, d, state):
        CN, m = state
        c_ref[sq, d, j] = CN[:, 0:Dh]
        n_ref[sq, d, j] = CN[:, Dh:2 * Dh].T[0:1, :]
        m_ref[sq, j, d:d + 1, :] = jnp.broadcast_to(m, (1, LANES))

    if n_blocks > 1:
        hf_scr, hb_scr = refs[pos:pos + 2]

    for sq, j in [(sq, j) for sq in range(seqs) for j in range(heads)]:
        cols = slice(j * Dh, (j + 1) * Dh)
        states = [init_state(sq, j, 0), init_state(sq, j, 1)]
        if n_blocks == 1:
            (hf, hb), states = blocks(sq, j, [(0, 0), (0, 1)], states)
            finish(sq, j, slice(0, L), hf + hb)
        else:
            for step in range(n_blocks):
                cf, cb = step, n_blocks - 1 - step
                (hf, hb), states = blocks(sq, j, [(cf, 0), (cb, 1)], states)
                hf_scr[cf * L:(cf + 1) * L, cols] = hf
                hb_scr[cb * L:(cb + 1) * L, cols] = hb
            finish(sq, j, slice(None), hf_scr[:, cols] + hb_scr[:, cols])
        if emit_state:
            emit(sq, j, 0, states[0])
            emit(sq, j, 1, states[1])


def _mlstm(mq, mkt, mv, mo, stats, norm_g, init_state, emit_state, heads, seqs):
    B, T, _ = mq.shape
    H, Dh, L = M_HEADS, M_HEAD_DIM, M_BLOCK
    nb = T // L
    w = heads * Dh
    has_init = init_state is not None
    n_rows = 4
    spread = jnp.tile(jnp.repeat(jnp.eye(n_rows, dtype=BF16), LANES, axis=1), (3, 1))
    spread = jnp.pad(spread, ((0, 2 * SUBLANES - 3 * n_rows), (0, 0)))
    tok = pl.BlockSpec((seqs, T, w), lambda b, h: (b, 0, h))
    group = mkt.shape[2] // T
    assert seqs % group == 0
    assert mkt.shape == (B // group, M_WIDTH, group * T) and stats.shape == (B // group, 3, 2 * H, group * T)
    in_specs = [tok, pl.BlockSpec((seqs // group, w, group * T), lambda b, h: (b, h, 0)), tok, tok,
                pl.BlockSpec((seqs // group, 3, 2 * H, group * T), lambda b, h: (b, 0, 0, 0)),
                pl.BlockSpec((heads, 1, Dh), lambda b, h: (h, 0, 0)),
                _const_spec((2 * SUBLANES, n_rows * LANES))]
    args = [mq, mkt, mv, mo, stats, norm_g, spread]
    state_specs = [pl.BlockSpec((seqs, 2, heads, Dh, Dh), lambda b, h: (b, 0, h, 0, 0)),
                   pl.BlockSpec((seqs, 2, heads, 1, Dh), lambda b, h: (b, 0, h, 0, 0)),
                   pl.BlockSpec((seqs, heads, 2, LANES), lambda b, h: (b, h, 0, 0))]
    if has_init:
        in_specs += state_specs
        args += list(init_state)
    out_specs = [tok]
    out_shape = [jax.ShapeDtypeStruct((B, T, M_WIDTH), BF16)]
    if emit_state:
        out_specs += state_specs
        out_shape += [jax.ShapeDtypeStruct((B, 2, H, Dh, Dh), F32),
                      jax.ShapeDtypeStruct((B, 2, H, 1, Dh), F32),
                      jax.ShapeDtypeStruct((B, H, 2, LANES), F32)]
    scratch = [] if nb == 1 else [pltpu.VMEM((T, w), F32), pltpu.VMEM((T, w), F32)]
    return pl.pallas_call(
        functools.partial(_mlstm_kernel, has_init, emit_state, nb, heads, seqs, group),
        grid=(B // seqs, H // heads),
        in_specs=in_specs,
        out_specs=out_specs,
        out_shape=out_shape,
        scratch_shapes=scratch,
        compiler_params=_params(("parallel", "parallel")),
        name="mlstm_latent" if has_init else "mlstm_context",
    )(*args)


def _attn_kernel(has_ctx, seqs, *refs):
    if has_ctx:
        q_ref, k_ref, v_ref, ckv_ref, krp_ref, wkv_ref, khg_ref, o_ref, kc_ref, vc_ref = refs

        @pl.when(pl.program_id(1) == 0)
        def _():
            kvf = _dot(ckv_ref[0].astype(BF16), wkv_ref[...])
            vc_ref[...] = kvf[:, K_NOPE_WIDTH:K_NOPE_WIDTH + A_WIDTH].astype(BF16)
            _write_heads(_k_nope_tiles(kvf), krp_ref[0], khg_ref[0:1, :], None, kc_ref)
    else:
        q_ref, k_ref, v_ref, o_ref = refs
    lane = lax.broadcasted_iota(jnp.int32, (1, LANES), 1)
    ones = lambda n: jnp.ones((n, LANES), BF16)
    tq = q_ref.shape[1]
    sub = min(Q_SUBTILE, tq)
    for sq, r0, pair in [(sq, r0, pair) for sq in range(seqs) for r0 in range(0, tq, sub)
                         for pair in range(A_HEADS // 2)]:
        rows = slice(r0, r0 + sub)
        vsl = slice(pair * LANES, (pair + 1) * LANES)
        v_aug = jnp.concatenate([v_ref[sq, :, vsl], ones(v_ref.shape[1])], axis=1)
        if has_ctx:
            vc_aug = jnp.concatenate([vc_ref[:, vsl], ones(vc_ref.shape[0])], axis=1)
        outs = []
        for e in range(2):
            hsl = slice((2 * pair + e) * HEAD_PAD, (2 * pair + e + 1) * HEAD_PAD)
            qh = q_ref[sq, rows, hsl]
            s = _dot_nt(qh, k_ref[sq, :, hsl])
            mx = jnp.max(s, axis=1, keepdims=True)
            if has_ctx:
                sc = _dot_nt(qh, kc_ref[:, hsl])
                mx = jnp.maximum(mx, jnp.max(sc, axis=1, keepdims=True))
            od = _dot(jnp.exp2(s - mx).astype(BF16), v_aug)
            if has_ctx:
                od = od + _dot(jnp.exp2(sc - mx).astype(BF16), vc_aug)
            outs.append(od[:, 0:LANES] / od[:, LANES:2 * LANES])
        o_ref[sq, rows, vsl] = jnp.where(lane < A_VDIM, outs[0], outs[1]).astype(o_ref.dtype)


def _attn(q, k, v, ctx, seqs):
    B, T, _ = q.shape
    tq = min(Q_TILE, T)
    has_ctx = ctx is not None
    full = lambda n, w: pl.BlockSpec((seqs, n, w), lambda b, i: (b, 0, 0))
    in_specs = [pl.BlockSpec((seqs, tq, QK_PAD), lambda b, i: (b, i, 0)), full(T, QK_PAD), full(T, A_WIDTH)]
    args = [q, k, v]
    scratch = []
    if has_ctx:
        assert seqs == 1
        P = ctx[0].shape[1]
        in_specs += [full(P, KV_LORA), full(P, HEAD_PAD), _const_spec((KV_LORA, K_NOPE_WIDTH + A_WIDTH)),
                     _const_spec((2, HEAD_PAD))]
        args += list(ctx)
        scratch = [pltpu.VMEM((P, QK_PAD), BF16), pltpu.VMEM((P, A_WIDTH), BF16)]
    return pl.pallas_call(
        functools.partial(_attn_kernel, has_ctx, seqs),
        grid=(B // seqs, T // tq),
        in_specs=in_specs,
        out_specs=pl.BlockSpec((seqs, tq, A_WIDTH), lambda b, i: (b, i, 0)),
        out_shape=jax.ShapeDtypeStruct((B, T, A_WIDTH), BF16),
        scratch_shapes=scratch,
        compiler_params=_params(("parallel", "arbitrary" if has_ctx else "parallel")),
        name="attn_latent" if has_ctx else "attn_context",
    )(*args)


def _post_kernel(mod_row0, mod_per_batch, x_ref, hm_ref, ha_ref, mod_ref, g2_ref, wout_ref, wup_ref, wdown_ref,
                 y_ref):
    mod = _mod_row(mod_ref, mod_row0, mod_per_batch)
    gate1 = mod[:, 2 * D_MODEL:3 * D_MODEL]
    sh2 = mod[:, 3 * D_MODEL:4 * D_MODEL]
    sc2 = mod[:, 4 * D_MODEL:5 * D_MODEL]
    gate2 = mod[:, 5 * D_MODEL:6 * D_MODEL]
    mix = jnp.concatenate([hm_ref[...], ha_ref[...]], axis=-1)
    x1 = x_ref[...] + gate1 * _dot(mix, wout_ref[...])
    h2 = (_rms(x1, g2_ref[...]) * (1.0 + sc2) + sh2).astype(BF16)
    acc = jnp.zeros_like(x1)
    for c in range(D_FF // FF_TILE):
        sl = slice(c * FF_TILE, (c + 1) * FF_TILE)
        u = jnp.maximum(_dot(h2, wup_ref[:, sl]), 0.0)
        acc = acc + _dot((u * u).astype(BF16), wdown_ref[sl, :])
    y_ref[...] = x1 + gate2 * acc


def _post(x, hm, ha, mod3, mod_row0, mod_per_batch, wts):
    shape = x.shape
    if not mod_per_batch:
        x, hm, ha = (a.reshape(1, -1, a.shape[-1]) for a in (x, hm, ha))
    B, T, _ = x.shape
    tm = POST_TILE
    tok = lambda w: pl.BlockSpec((None, tm, w), lambda b, i: (b, i, 0))
    return _post_call(x, hm, ha, mod3, mod_row0, mod_per_batch, wts, B, T, tm, tok).reshape(shape)


def _post_call(x, hm, ha, mod3, mod_row0, mod_per_batch, wts, B, T, tm, tok):
    return pl.pallas_call(
        functools.partial(_post_kernel, mod_row0, mod_per_batch),
        grid=(B, T // tm),
        in_specs=[tok(D_MODEL), tok(M_WIDTH), tok(A_WIDTH),
                  _const_spec((8, 6 * D_MODEL)),
                  _const_spec((1, D_MODEL)),
                  _const_spec((M_WIDTH + A_WIDTH, D_MODEL)),
                  _const_spec((D_MODEL, D_FF)),
                  _const_spec((D_FF, D_MODEL))],
        out_specs=tok(D_MODEL),
        out_shape=jax.ShapeDtypeStruct((B, T, D_MODEL), F32),
        compiler_params=_params(("parallel", "parallel")),
        name="post",
    )(x, hm, ha, mod3, wts["g2"], wts["w_out"], wts["w_up"], wts["w_down"])


def _prepare_weights(norm1_g, norm2_g, w_in, mlstm_gate_b, q_lora_g, kv_lora_g, w_q_up, w_kv_up,
                     q_head_g, k_head_g, w_out, w_mlp_up, w_mlp_down):
    o_g = 4 * M_WIDTH
    o_q = o_g + N_GATES
    o_kv = o_q + Q_LORA
    o_kr = o_kv + KV_LORA
    half = A_ROPE // 2
    n_dh = 2 * M_HEADS
    wt = w_in.T.astype(BF16)
    w_gate = wt[o_g:o_q].reshape(2, 2, M_HEADS, D_MODEL)
    bias = mlstm_gate_b.reshape(2, 2, M_HEADS)

    def rot_partner(a):
        z = jnp.zeros(a.shape[:-1] + (A_NOPE,), a.dtype)
        return jnp.concatenate([z, a[..., A_NOPE + half:A_QK], a[..., A_NOPE:A_NOPE + half]], axis=-1)

    pad_tile = lambda a: jnp.pad(a, [(0, 0)] * (a.ndim - 1) + [(0, HEAD_PAD - A_QK)])
    w_kr = wt[o_kr:o_kr + A_ROPE]
    w_kr_partner = jnp.concatenate([w_kr[half:], w_kr[:half]], axis=0)
    w_lat = jnp.concatenate([wt[o_q:o_kr], w_gate[:, 0].reshape(n_dh, D_MODEL), w_gate[:, 1].reshape(n_dh, D_MODEL),
                             jnp.zeros((A_NOPE - 2 * n_dh, D_MODEL), BF16), w_kr, w_kr_partner], axis=0)
    gate_bias = jnp.pad(jnp.stack([bias[:, 0, :].reshape(n_dh), bias[:, 1, :].reshape(n_dh)], axis=0),
                        ((0, 0), (0, LANES - n_dh)))
    w_q3 = w_q_up.reshape(Q_LORA, A_HEADS, A_QK)
    w_q = pad_tile(w_q3).reshape(Q_LORA, QK_PAD)
    w_q_partner = jnp.concatenate([w_q3[:, :, A_NOPE + half:], w_q3[:, :, A_NOPE:A_NOPE + half]],
                                  axis=-1).reshape(Q_LORA, A_HEADS * A_ROPE)
    w_kv3 = w_kv_up.reshape(KV_LORA, A_HEADS, A_NOPE + A_VDIM)
    w_k = w_kv3[:, :, :A_NOPE]
    w_v = w_kv3[:, :, A_NOPE:]
    w_kv = jnp.concatenate([w_k.reshape(KV_LORA, K_NOPE_WIDTH), w_v.reshape(KV_LORA, A_WIDTH)], axis=1)
    pad_head = lambda g: jnp.stack([pad_tile(g), pad_tile(rot_partner(g))], axis=0)
    return {
        "g1": norm1_g.reshape(1, D_MODEL),
        "g2": norm2_g.reshape(1, D_MODEL),
        "w_main": jnp.concatenate([wt[0:M_WIDTH], wt[2 * M_WIDTH:o_g]], axis=0),
        "w_kt": wt[M_WIDTH:2 * M_WIDTH],
        "w_lat": w_lat,
        "gate_bias": gate_bias,
        "q_lora_g": q_lora_g.reshape(1, Q_LORA),
        "kv_lora_g": kv_lora_g.reshape(1, KV_LORA),
        "w_q": w_q.astype(BF16),
        "w_q_rot": jnp.concatenate([w_q, w_q_partner], axis=1).astype(BF16),
        "w_kv": w_kv.astype(BF16),
        "q_head_g": pad_head(q_head_g),
        "k_head_g": pad_head(k_head_g),
        "w_out_f32": w_out,
        "w_up_f32": w_mlp_up,
        "w_down_f32": w_mlp_down,
    }


def _rope_tables(T):
    rows = T // GRID_W
    row = np.repeat(np.arange(rows, dtype=np.float32), GRID_W)
    col = np.tile(np.arange(GRID_W, dtype=np.float32), rows)
    half = A_ROPE // 2
    inv = (np.float32(ROPE_BASE) ** (-np.arange(0, half, 2, dtype=np.float32) / np.float32(half))).astype(np.float32)
    ang = np.concatenate([row[:, None] * inv, col[:, None] * inv], axis=-1)
    cos, sin = np.cos(ang), np.sin(ang)
    ones = np.ones((T, A_NOPE), np.float32)
    z = lambda w: np.zeros((T, w), np.float32)
    tail = LANES - A_QK
    cos_t = np.concatenate([ones, cos, cos, z(tail)], axis=1)
    sin_t = np.concatenate([z(A_NOPE), -sin, sin, z(tail)], axis=1)
    return jnp.asarray(np.stack([cos_t, sin_t], axis=0).astype(np.float32))


def _layer_pass(x, mod3, mod_row0, mod_per_batch, wts, norm_g, rope_tab, init_state, ctx_kv, is_context):
    pre = _pre(x, mod3, mod_row0, mod_per_batch, wts, rope_tab, emit_cache=is_context)
    mq, mkt, mv, mo, stats, q, k, v = pre[:8]
    if is_context:
        wts = dict(wts, w_out=pre[10], w_up=pre[11], w_down=pre[12])
    ml = _mlstm(mq, mkt, mv, mo, stats, norm_g, init_state, emit_state=is_context,
                heads=M_HEADS if is_context else 1, seqs=MLSTM_CONTEXT_SEQS if is_context else 1)
    ha = _attn(q, k, v, ctx_kv, seqs=ATTN_CONTEXT_SEQS if is_context else 1)
    y = _post(x, ml[0], ha, mod3, mod_row0, mod_per_batch, wts)
    return y, pre[8:10], ml[1:], wts


def kernel(x_prompt, x_sample, cache_mla_ckv, cache_mla_krope, state_mlstm_C, state_mlstm_n, state_mlstm_m,
           c, c_ctx, norm1_g, norm2_g, w_ada, b_ada, w_in, mlstm_gate_b, mlstm_norm_g,
           q_lora_g, kv_lora_g, w_q_up, w_kv_up, q_head_g, k_head_g, w_out, w_mlp_up, w_mlp_down):
    depth = w_in.shape[0]
    Bd = x_sample.shape[0]
    cond8 = jnp.concatenate([c_ctx[None, :], c, jnp.zeros((8 - 1 - Bd, D_MODEL), F32)], axis=0)
    rope_tab = _rope_tables(x_sample.shape[1])

    y, z = x_prompt, x_sample
    ckvs, kropes, Cs, ns, ms = [], [], [], [], []
    for l in range(depth):
        wts = _prepare_weights(norm1_g[l], norm2_g[l], w_in[l], mlstm_gate_b[l], q_lora_g[l], kv_lora_g[l],
                               w_q_up[l], w_kv_up[l], q_head_g[l], k_head_g[l], w_out[l], w_mlp_up[l],
                               w_mlp_down[l])
        norm_g = mlstm_norm_g[l].reshape(M_HEADS, 1, M_HEAD_DIM)
        mod3 = _ada(cond8, w_ada[l], b_ada[l])
        y, (ckv, krope), (C_new, n_new, m_new), wts = _layer_pass(
            y, mod3, 0, 0, wts, norm_g, None, None, None, True)
        ckvs.append(ckv)
        kropes.append(krope)
        Cs.append(C_new)
        ns.append(n_new[:, :, :, 0, :])
        ms.append(m_new[:, :, :, 0].transpose(0, 2, 1))

        init_state = (state_mlstm_C[:, l],
                      state_mlstm_n[:, l][:, :, :, None, :],
                      jnp.broadcast_to(state_mlstm_m[:, l].transpose(0, 2, 1)[..., None],
                                       (Bd, M_HEADS, 2, LANES)))
        krope_placed = jnp.pad(cache_mla_krope[:, l], ((0, 0), (0, 0), (A_NOPE, LANES - A_QK)))
        ctx = (cache_mla_ckv[:, l], krope_placed, wts["w_kv"], wts["k_head_g"])
        z, _, _, _ = _layer_pass(z, mod3, 1, 1, wts, norm_g, rope_tab, init_state, ctx, False)

    return (y, z, jnp.stack(ckvs, axis=1), jnp.stack(kropes, axis=1), jnp.stack(Cs, axis=1),
            jnp.stack(ns, axis=1), jnp.stack(ms, axis=1))
```

```python
import functools

import jax
import jax.numpy as jnp
import numpy as np
from jax import lax
from jax.experimental import pallas as pl
from jax.experimental.pallas import tpu as pltpu

F32 = jnp.float32
BF16 = jnp.bfloat16

D_MODEL = 1024
GRID_W = 64
M_HEADS = 4
M_HEAD_DIM = 128
M_WIDTH = M_HEADS * M_HEAD_DIM
M_BLOCK = 256
A_HEADS = 8
A_NOPE = 64
A_ROPE = 32
A_QK = A_NOPE + A_ROPE
A_VDIM = 64
A_WIDTH = A_HEADS * A_VDIM
Q_LORA = 384
KV_LORA = 256
ROPE_BASE = 10000.0
D_FF = 4 * D_MODEL
EPS = 1e-6

LANES = 128
SUBLANES = 8
LOG2E = 1.4426950408889634
HEAD_PAD = LANES
QK_PAD = A_HEADS * HEAD_PAD
N_GATES = 4 * M_HEADS
LAT_WIDTH = Q_LORA + KV_LORA + LANES
VMEM_LIMIT = 56 * 1024 * 1024

TOKEN_TILE = 512
LONG_TOKEN_TILE = 1024
POST_TILE = 512
MLSTM_CONTEXT_SEQS = 4
ATTN_CONTEXT_SEQS = 8
Q_TILE = 1024
Q_SUBTILE = 256
ADA_TILE_N = 1536
FF_TILE = 1024


def _dot(a, b):
    return jnp.dot(a, b, preferred_element_type=F32)


def _dot_nt(a, b):
    return lax.dot_general(a, b, (((1,), (1,)), ((), ())), preferred_element_type=F32)


def _dot_tn(a, b):
    return lax.dot_general(a, b, (((0,), (0,)), ((), ())), preferred_element_type=F32)


def _rms(x, g):
    y = x * lax.rsqrt(jnp.mean(x * x, axis=-1, keepdims=True) + EPS)
    return y * g


def _params(sem):
    return pltpu.CompilerParams(dimension_semantics=sem, vmem_limit_bytes=VMEM_LIMIT)


def _const_spec(shape):
    zeros = (0,) * len(shape)
    return pl.BlockSpec(shape, lambda *_: zeros, pipeline_mode=pl.Buffered(1))


def _ada_kernel(cond_ref, w_ref, b_ref, o_ref):
    c = cond_ref[...]
    s = (c * jax.nn.sigmoid(c)).astype(BF16)
    o_ref[...] = _dot(s, w_ref[...].astype(BF16)) + b_ref[...]


def _ada(cond8, w_ada, b_ada):
    n = w_ada.shape[1]
    return pl.pallas_call(
        _ada_kernel,
        grid=(n // ADA_TILE_N,),
        in_specs=[
            pl.BlockSpec((8, D_MODEL), lambda j: (0, 0)),
            pl.BlockSpec((D_MODEL, ADA_TILE_N), lambda j: (0, j)),
            pl.BlockSpec((1, ADA_TILE_N), lambda j: (0, j)),
        ],
        out_specs=pl.BlockSpec((8, ADA_TILE_N), lambda j: (0, j)),
        out_shape=jax.ShapeDtypeStruct((8, n), F32),
        compiler_params=_params(("parallel",)),
        name="ada",
    )(cond8, w_ada, b_ada.reshape(1, n))


def _write_heads(src, extra, g_pad, rot, dst_ref):
    for h in range(A_HEADS):
        sl = slice(h * HEAD_PAD, (h + 1) * HEAD_PAD)
        xh = src[:, sl]
        if extra is not None:
            xh = xh + extra
        ss = jnp.sum(xh * xh, axis=-1, keepdims=True) * (1.0 / A_QK)
        r = lax.rsqrt(ss + EPS)
        if rot is None:
            y = xh * r * g_pad
        else:
            partners, cos_g, sin_g = rot
            y = (xh * cos_g + partners[h] * sin_g) * r
        dst_ref[:, sl] = y.astype(dst_ref.dtype)


def _time_scan(x, op, identity, reverse):
    n = x.shape[0]
    row = lax.broadcasted_iota(jnp.int32, x.shape, 0)
    shift = 1
    while shift < n:
        if shift < SUBLANES:
            if reverse:
                moved = jnp.where(row < n - shift, pltpu.roll(x, n - shift, 0), identity)
            else:
                moved = jnp.where(row >= shift, pltpu.roll(x, shift, 0), identity)
        else:
            fill = jnp.full((shift, x.shape[1]), identity, x.dtype)
            moved = (jnp.concatenate([x[shift:], fill], axis=0) if reverse
                     else jnp.concatenate([fill, x[:n - shift]], axis=0))
        x = op(x, moved)
        shift *= 2
    return x


def _mod_row(mod_ref, row0, per_batch):
    if per_batch:
        return mod_ref[pl.ds(row0 + pl.program_id(0) * per_batch, 1), :]
    return mod_ref[row0:row0 + 1, :]


def _pre_kernel(has_rope, emit_cache, mod_row0, mod_per_batch, *refs):
    (x_ref, mod_ref, g1_ref, wmain_ref, wkt_ref, wlat_ref, gbias_ref, qlg_ref, kvg_ref, wq_ref, wkv_ref,
     qhg_ref, khg_ref) = refs[:13]
    pos = 13
    if has_rope:
        rope_ref = refs[pos]
        pos += 1
    if emit_cache:
        cast_in = refs[pos:pos + 3]
        pos += 3
    (mq_ref, mkt_ref, mv_ref, mo_ref, stats_ref, q_ref, k_ref, v_ref) = refs[pos:pos + 8]
    pos += 8

    x = x_ref[...]
    mod = _mod_row(mod_ref, mod_row0, mod_per_batch)
    sh1 = mod[:, 0:D_MODEL]
    sc1 = mod[:, D_MODEL:2 * D_MODEL]
    h = _rms(x, g1_ref[...]) * (1.0 + sc1) + sh1
    hb = h.astype(BF16)

    plat = _dot_nt(hb, wlat_ref[...])
    q_lat = plat[:, 0:Q_LORA]
    kv_lat = plat[:, Q_LORA:Q_LORA + KV_LORA]
    tail = plat[:, Q_LORA + KV_LORA:LAT_WIDTH]
    tail_f = pltpu.roll(tail, LANES - 2 * M_HEADS, 1)
    tail2 = pltpu.roll(tail, LANES - A_ROPE, 1)

    lane = lax.broadcasted_iota(jnp.int32, (1, LANES), 1)
    fwd = lane < M_HEADS
    gate_i = tail + gbias_ref[0:1, :]
    gate_f = tail_f + gbias_ref[1:2, :]
    log_f = jnp.minimum(gate_f, 0.0) - jnp.log1p(jnp.exp(-jnp.abs(gate_f)))
    def block_scan(v, op, identity):
        parts = [v[r:r + M_BLOCK] for r in range(0, v.shape[0], M_BLOCK)]
        return jnp.where(fwd, jnp.concatenate([_time_scan(p, op, identity, False) for p in parts], axis=0),
                         jnp.concatenate([_time_scan(p, op, identity, True) for p in parts], axis=0))

    b = block_scan(log_f, jnp.add, 0.0)
    a = gate_i - b
    amax = block_scan(a, jnp.maximum, -jnp.inf)
    stats_ref[0] = b.T[0:2 * M_HEADS, :]
    stats_ref[1] = a.T[0:2 * M_HEADS, :]
    stats_ref[2] = amax.T[0:2 * M_HEADS, :]

    krope_placed = jnp.where((lane >= A_NOPE) & (lane < A_QK), tail, 0.0)
    ckv = _rms(kv_lat, kvg_ref[...])
    qn = _rms(q_lat, qlg_ref[...])
    qf = _dot(qn.astype(BF16), wq_ref[...])
    kvf = _dot(ckv.astype(BF16), wkv_ref[...])
    v_ref[...] = kvf[:, QK_PAD:QK_PAD + A_WIDTH].astype(BF16)
    q_rot = k_rot = None
    qhg = qhg_ref[...] * (A_QK ** -0.5 * LOG2E)
    if has_rope:
        cos_t, sin_t = rope_ref[0], rope_ref[1]
        q_partners = []
        for h in range(A_HEADS):
            packed = qf[:, QK_PAD + (h // 4) * LANES:QK_PAD + (h // 4 + 1) * LANES]
            shift = (A_NOPE - (h % 4) * A_ROPE) % LANES
            q_partners.append(pltpu.roll(packed, shift, 1) if shift else packed)
        q_rot = (q_partners, cos_t * qhg[0:1, :], sin_t * qhg[1:2, :])
        k_rot = ([tail2] * A_HEADS, cos_t * khg_ref[0:1, :], sin_t * khg_ref[1:2, :])
    _write_heads(qf, None, qhg[0:1, :], q_rot, q_ref)
    _write_heads(kvf, krope_placed, khg_ref[0:1, :], k_rot, k_ref)

    if emit_cache:
        ckv_ref, krope_ref = refs[pos:pos + 2]
        ckv_ref[...] = ckv
        krope_ref[...] = tail[:, A_NOPE:A_QK]
        for src_ref, dst_ref in zip(cast_in, refs[pos + 2:pos + 5]):
            dst_ref[...] = src_ref[...].astype(BF16)

    pm = _dot_nt(hb, wmain_ref[...])
    mq_ref[...] = pm[:, 0:M_WIDTH].astype(BF16)
    mv_ref[...] = pm[:, M_WIDTH:2 * M_WIDTH].astype(BF16)
    mo_ref[...] = pm[:, 2 * M_WIDTH:3 * M_WIDTH]
    mkt_ref[...] = _dot_nt(wkt_ref[...], hb) * (M_HEAD_DIM ** -0.5)


def _pre(x, mod3, mod_row0, mod_per_batch, wts, rope_tab, emit_cache):
    shape = x.shape
    tm = LONG_TOKEN_TILE if shape[1] >= LONG_TOKEN_TILE else TOKEN_TILE
    if shape[1] < tm:
        assert not mod_per_batch and tm % shape[1] == 0
        x = x.reshape(-1, tm, shape[2])
    B, T, _ = x.shape
    has_rope = rope_tab is not None
    tok = lambda w: pl.BlockSpec((None, tm, w), lambda b, i: (b, i, 0))
    in_specs = [
        tok(D_MODEL),
        _const_spec((8, 6 * D_MODEL)),
        _const_spec((1, D_MODEL)),
        _const_spec((3 * M_WIDTH, D_MODEL)),
        _const_spec((M_WIDTH, D_MODEL)),
        _const_spec((LAT_WIDTH, D_MODEL)),
        _const_spec((2, LANES)),
        _const_spec((1, Q_LORA)),
        _const_spec((1, KV_LORA)),
        _const_spec((Q_LORA, QK_PAD + A_HEADS * A_ROPE if has_rope else QK_PAD)),
        _const_spec((KV_LORA, QK_PAD + A_WIDTH)),
        _const_spec((2, HEAD_PAD)),
        _const_spec((2, HEAD_PAD)),
    ]
    assert tm % M_BLOCK == 0
    args = [x, mod3, wts["g1"], wts["w_main"], wts["w_kt"], wts["w_lat"], wts["gate_bias"], wts["q_lora_g"],
            wts["kv_lora_g"], wts["w_q_rot"] if has_rope else wts["w_q"], wts["w_kv"],
            wts["q_head_g"], wts["k_head_g"]]
    if has_rope:
        in_specs.append(pl.BlockSpec((2, tm, HEAD_PAD), lambda b, i: (0, i, 0)))
        args.append(rope_tab)
    out_specs = [tok(M_WIDTH),
                 pl.BlockSpec((None, M_WIDTH, tm), lambda b, i: (b, 0, i)),
                 tok(M_WIDTH), tok(M_WIDTH),
                 pl.BlockSpec((None, 3, 2 * M_HEADS, tm), lambda b, i: (b, 0, 0, i)),
                 tok(QK_PAD), tok(QK_PAD), tok(A_WIDTH)]
    out_shape = [
        jax.ShapeDtypeStruct((B, T, M_WIDTH), BF16),
        jax.ShapeDtypeStruct((B, M_WIDTH, T), F32),
        jax.ShapeDtypeStruct((B, T, M_WIDTH), BF16),
        jax.ShapeDtypeStruct((B, T, M_WIDTH), F32),
        jax.ShapeDtypeStruct((B, 3, 2 * M_HEADS, T), F32),
        jax.ShapeDtypeStruct((B, T, QK_PAD), BF16),
        jax.ShapeDtypeStruct((B, T, QK_PAD), BF16),
        jax.ShapeDtypeStruct((B, T, A_WIDTH), BF16),
    ]
    if emit_cache:
        out_specs += [tok(KV_LORA), tok(A_ROPE)]
        out_shape += [jax.ShapeDtypeStruct((B, T, KV_LORA), F32),
                      jax.ShapeDtypeStruct((B, T, A_ROPE), F32)]
        steps, per_b = B * (T // tm), T // tm
        for w in (wts["w_out_f32"], wts["w_up_f32"], wts["w_down_f32"]):
            rows = w.shape[0] // steps
            assert rows * steps == w.shape[0] and rows % (2 * SUBLANES) == 0
            spec = pl.BlockSpec((rows, w.shape[1]), lambda b, i: (b * per_b + i, 0))
            in_specs.append(spec)
            args.append(w)
            out_specs.append(spec)
            out_shape.append(jax.ShapeDtypeStruct(w.shape, BF16))
    outs = pl.pallas_call(
        functools.partial(_pre_kernel, has_rope, emit_cache, mod_row0, mod_per_batch),
        grid=(B, T // tm),
        in_specs=in_specs,
        out_specs=out_specs,
        out_shape=out_shape,
        compiler_params=_params(("parallel", "parallel")),
        name="pre_latent" if has_rope else "pre_context",
    )(*args)
    keep = (1, 4, 10, 11, 12)
    return [o if n in keep else o.reshape(shape[:2] + o.shape[2:]) for n, o in enumerate(outs)]


def _rows_to_lane_broadcast(rows, spread):
    x = jnp.concatenate(rows, axis=0)
    p1 = x.astype(BF16)
    r1 = x - p1.astype(F32)
    p2 = r1.astype(BF16)
    p3 = (r1 - p2.astype(F32)).astype(BF16)
    pad = jnp.zeros((spread.shape[0] - 3 * len(rows), x.shape[1]), BF16)
    return _dot_tn(jnp.concatenate([p1, p2, p3, pad], axis=0), spread)


def _mlstm_gate_rows(b_row, a_row, amax_row, forward, m):
    L = b_row.shape[1]
    last = slice(L - 1, L) if forward else slice(0, 1)
    total = b_row[:, last]
    g_row = jnp.maximum(m, amax_row)
    m_new = total + jnp.maximum(m, amax_row[:, last])
    w_key_row = jnp.exp2((a_row + (total - m_new)) * LOG2E)
    decay = jnp.exp(total + m - m_new)
    return g_row * LOG2E, (b_row + g_row) * LOG2E, a_row * LOG2E, w_key_row, decay, m_new


def _mlstm_block(s_raw, q, kt, v_aug, g2, mt2, a2_row, w_key_row, decay, allow, CN, m):
    w_intra = jnp.exp2(jnp.where(allow, a2_row - jnp.concatenate([g2, g2], axis=1), -jnp.inf))
    w_inter = jnp.exp2(m * LOG2E - g2)
    s = (s_raw * w_intra).astype(BF16)
    nd = _dot(s, v_aug) + jnp.concatenate([w_inter, w_inter], axis=1) * _dot(q, CN.astype(BF16))
    num, den = nd[:, 0:M_HEAD_DIM], nd[:, M_HEAD_DIM:2 * M_HEAD_DIM]
    h = num / jnp.maximum(jnp.abs(den), jnp.exp2(-mt2))
    CN_new = decay * CN + _dot((kt * w_key_row).astype(BF16), v_aug)
    return h, CN_new


def _mlstm_kernel(has_init, emit_state, n_blocks, heads, seqs, group, *refs):
    q_ref, kt_ref, v_ref, mo_ref, stats_ref, ng_ref, spread_ref = refs[:7]
    pos = 7
    if has_init:
        c0_ref, n0_ref, m0_ref = refs[pos:pos + 3]
        pos += 3
    hm_ref = refs[pos]
    pos += 1
    if emit_state:
        c_ref, n_ref, m_ref = refs[pos:pos + 3]
        pos += 3

    L, Dh = M_BLOCK, M_HEAD_DIM
    t_idx = lax.broadcasted_iota(jnp.int32, (L, L), 0)
    s_idx = lax.broadcasted_iota(jnp.int32, (L, L), 1)
    allow = (s_idx <= t_idx, s_idx >= t_idx)
    spread = spread_ref[...]
    ones = jnp.ones((L, Dh), BF16)

    def time_lanes(sq, c):
        start = ((sq % group) * n_blocks + c) * L
        return sq // group, slice(start, start + L)

    def lane_broadcast_n(n_row):
        return jnp.broadcast_to(n_row, (Dh, Dh)).T

    def init_state(sq, j, d):
        if has_init:
            return (jnp.concatenate([c0_ref[sq, d, j], lane_broadcast_n(n0_ref[sq, d, j])], axis=1),
                    m0_ref[sq, j, d:d + 1, 0:1])
        return jnp.zeros((Dh, 2 * Dh), F32), jnp.zeros((1, 1), F32)

    def gate_rows(sq, j, c, d, m):
        head = j if heads == M_HEADS else pl.program_id(1) * heads + j
        r, (g, lanes) = pl.ds(d * M_HEADS + head, 1), time_lanes(sq, c)
        return _mlstm_gate_rows(stats_ref[g, 0, r, lanes], stats_ref[g, 1, r, lanes], stats_ref[g, 2, r, lanes],
                                d == 0, m)

    def blocks(sq, j, jobs, states):
        loaded, rows6, cols_in = {}, [], []
        for (c, d), (CN, m) in zip(jobs, states):
            rows6.append(gate_rows(sq, j, c, d, m))
            cols_in += [rows6[-1][0], rows6[-1][1]]
            if c not in loaded:
                rows, cols = slice(c * L, (c + 1) * L), slice(j * Dh, (j + 1) * Dh)
                g, lanes = time_lanes(sq, c)
                q, kt = q_ref[sq, rows, cols], kt_ref[g, cols, lanes]
                v_aug = jnp.concatenate([v_ref[sq, rows, cols], ones], axis=1)
                loaded[c] = (_dot(q, kt.astype(BF16)), q, kt, v_aug)
        cols_out = _rows_to_lane_broadcast(cols_in, spread)
        hs, new_states = [], []
        for idx, ((c, d), (CN, m)) in enumerate(zip(jobs, states)):
            g2 = cols_out[:, (2 * idx) * LANES:(2 * idx + 1) * LANES]
            mt2 = cols_out[:, (2 * idx + 1) * LANES:(2 * idx + 2) * LANES]
            _, _, a2_row, w_key_row, decay, m_new = rows6[idx]
            h, CN_new = _mlstm_block(*loaded[c], g2, mt2, a2_row, w_key_row, decay, allow[d], CN, m)
            hs.append(h)
            new_states.append((CN_new, m_new))
        return hs, new_states

    def finish(sq, j, rows, hs):
        cols = slice(j * Dh, (j + 1) * Dh)
        hn = _rms(hs, ng_ref[j])
        hm_ref[sq, rows, cols] = (hn * jax.nn.sigmoid(mo_ref[sq, rows, cols])).astype(hm_ref.dtype)

    def emit(sq, j, d, state):
        CN, m = state
        c_ref[sq, d, j] = CN[:, 0:Dh]
        n_ref[sq, d, j] = CN[:, Dh:2 * Dh].T[0:1, :]
        m_ref[sq, j, d:d + 1, :] = jnp.broadcast_to(m, (1, LANES))

    if n_blocks > 1:
        hf_scr, hb_scr = refs[pos:pos + 2]

    for sq, j in [(sq, j) for sq in range(seqs) for j in range(heads)]:
        cols = slice(j * Dh, (j + 1) * Dh)
        states = [init_state(sq, j, 0), init_state(sq, j, 1)]
        if n_blocks == 1:
            (hf, hb), states = blocks(sq, j, [(0, 0), (0, 1)], states)
            finish(sq, j, slice(0, L), hf + hb)
        else:
            for step in range(n_blocks):
                cf, cb = step, n_blocks - 1 - step
                (hf, hb), states = blocks(sq, j, [(cf, 0), (cb, 1)], states)
                hf_scr[cf * L:(cf + 1) * L, cols] = hf
                hb_scr[cb * L:(cb + 1) * L, cols] = hb
            finish(sq, j, slice(None), hf_scr[:, cols] + hb_scr[:, cols])
        if emit_state:
            emit(sq, j, 0, states[0])
            emit(sq, j, 1, states[1])


def _mlstm(mq, mkt, mv, mo, stats, norm_g, init_state, emit_state, heads, seqs):
    B, T, _ = mq.shape
    H, Dh, L = M_HEADS, M_HEAD_DIM, M_BLOCK
    nb = T // L
    w = heads * Dh
    has_init = init_state is not None
    n_rows = 4
    spread = jnp.tile(jnp.repeat(jnp.eye(n_rows, dtype=BF16), LANES, axis=1), (3, 1))
    spread = jnp.pad(spread, ((0, 2 * SUBLANES - 3 * n_rows), (0, 0)))
    tok = pl.BlockSpec((seqs, T, w), lambda b, h: (b, 0, h))
    group = mkt.shape[2] // T
    assert seqs % group == 0
    assert mkt.shape == (B // group, M_WIDTH, group * T) and stats.shape == (B // group, 3, 2 * H, group * T)
    in_specs = [tok, pl.BlockSpec((seqs // group, w, group * T), lambda b, h: (b, h, 0)), tok, tok,
                pl.BlockSpec((seqs // group, 3, 2 * H, group * T), lambda b, h: (b, 0, 0, 0)),
                pl.BlockSpec((heads, 1, Dh), lambda b, h: (h, 0, 0)),
                _const_spec((2 * SUBLANES, n_rows * LANES))]
    args = [mq, mkt, mv, mo, stats, norm_g, spread]
    state_specs = [pl.BlockSpec((seqs, 2, heads, Dh, Dh), lambda b, h: (b, 0, h, 0, 0)),
                   pl.BlockSpec((seqs, 2, heads, 1, Dh), lambda b, h: (b, 0, h, 0, 0)),
                   pl.BlockSpec((seqs, heads, 2, LANES), lambda b, h: (b, h, 0, 0))]
    if has_init:
        in_specs += state_specs
        args += list(init_state)
    out_specs = [tok]
    out_shape = [jax.ShapeDtypeStruct((B, T, M_WIDTH), BF16)]
    if emit_state:
        out_specs += state_specs
        out_shape += [jax.ShapeDtypeStruct((B, 2, H, Dh, Dh), F32),
                      jax.ShapeDtypeStruct((B, 2, H, 1, Dh), F32),
                      jax.ShapeDtypeStruct((B, H, 2, LANES), F32)]
    scratch = [] if nb == 1 else [pltpu.VMEM((T, w), F32), pltpu.VMEM((T, w), F32)]
    return pl.pallas_call(
        functools.partial(_mlstm_kernel, has_init, emit_state, nb, heads, seqs, group),
        grid=(B // seqs, H // heads),
        in_specs=in_specs,
        out_specs=out_specs,
        out_shape=out_shape,
        scratch_shapes=scratch,
        compiler_params=_params(("parallel", "parallel")),
        name="mlstm_latent" if has_init else "mlstm_context",
    )(*args)


def _attn_kernel(has_ctx, seqs, *refs):
    if has_ctx:
        q_ref, k_ref, v_ref, ckv_ref, krp_ref, wkv_ref, khg_ref, o_ref, kc_ref, vc_ref = refs

        @pl.when(pl.program_id(1) == 0)
        def _():
            kvf = _dot(ckv_ref[0].astype(BF16), wkv_ref[...])
            vc_ref[...] = kvf[:, QK_PAD:QK_PAD + A_WIDTH].astype(BF16)
            _write_heads(kvf, krp_ref[0], khg_ref[0:1, :], None, kc_ref)
    else:
        q_ref, k_ref, v_ref, o_ref = refs
    lane = lax.broadcasted_iota(jnp.int32, (1, LANES), 1)
    ones = lambda n: jnp.ones((n, LANES), BF16)
    tq = q_ref.shape[1]
    sub = min(Q_SUBTILE, tq)
    for sq, r0, pair in [(sq, r0, pair) for sq in range(seqs) for r0 in range(0, tq, sub)
                         for pair in range(A_HEADS // 2)]:
        rows = slice(r0, r0 + sub)
        vsl = slice(pair * LANES, (pair + 1) * LANES)
        v_aug = jnp.concatenate([v_ref[sq, :, vsl], ones(v_ref.shape[1])], axis=1)
        if has_ctx:
            vc_aug = jnp.concatenate([vc_ref[:, vsl], ones(vc_ref.shape[0])], axis=1)
        outs = []
        for e in range(2):
            hsl = slice((2 * pair + e) * HEAD_PAD, (2 * pair + e + 1) * HEAD_PAD)
            qh = q_ref[sq, rows, hsl]
            s = _dot_nt(qh, k_ref[sq, :, hsl])
            mx = jnp.max(s, axis=1, keepdims=True)
            if has_ctx:
                sc = _dot_nt(qh, kc_ref[:, hsl])
                mx = jnp.maximum(mx, jnp.max(sc, axis=1, keepdims=True))
            od = _dot(jnp.exp2(s - mx).astype(BF16), v_aug)
            if has_ctx:
                od = od + _dot(jnp.exp2(sc - mx).astype(BF16), vc_aug)
            outs.append(od[:, 0:LANES] / od[:, LANES:2 * LANES])
        o_ref[sq, rows, vsl] = jnp.where(lane < A_VDIM, outs[0], outs[1]).astype(o_ref.dtype)


def _attn(q, k, v, ctx, seqs):
    B, T, _ = q.shape
    tq = min(Q_TILE, T)
    has_ctx = ctx is not None
    full = lambda n, w: pl.BlockSpec((seqs, n, w), lambda b, i: (b, 0, 0))
    in_specs = [pl.BlockSpec((seqs, tq, QK_PAD), lambda b, i: (b, i, 0)), full(T, QK_PAD), full(T, A_WIDTH)]
    args = [q, k, v]
    scratch = []
    if has_ctx:
        assert seqs == 1
        P = ctx[0].shape[1]
        in_specs += [full(P, KV_LORA), full(P, HEAD_PAD), _const_spec((KV_LORA, QK_PAD + A_WIDTH)),
                     _const_spec((2, HEAD_PAD))]
        args += list(ctx)
        scratch = [pltpu.VMEM((P, QK_PAD), BF16), pltpu.VMEM((P, A_WIDTH), BF16)]
    return pl.pallas_call(
        functools.partial(_attn_kernel, has_ctx, seqs),
        grid=(B // seqs, T // tq),
        in_specs=in_specs,
        out_specs=pl.BlockSpec((seqs, tq, A_WIDTH), lambda b, i: (b, i, 0)),
        out_shape=jax.ShapeDtypeStruct((B, T, A_WIDTH), BF16),
        scratch_shapes=scratch,
        compiler_params=_params(("parallel", "arbitrary" if has_ctx else "parallel")),
        name="attn_latent" if has_ctx else "attn_context",
    )(*args)


def _post_kernel(mod_row0, mod_per_batch, x_ref, hm_ref, ha_ref, mod_ref, g2_ref, wout_ref, wup_ref, wdown_ref,
                 y_ref):
    mod = _mod_row(mod_ref, mod_row0, mod_per_batch)
    gate1 = mod[:, 2 * D_MODEL:3 * D_MODEL]
    sh2 = mod[:, 3 * D_MODEL:4 * D_MODEL]
    sc2 = mod[:, 4 * D_MODEL:5 * D_MODEL]
    gate2 = mod[:, 5 * D_MODEL:6 * D_MODEL]
    mix = jnp.concatenate([hm_ref[...], ha_ref[...]], axis=-1)
    x1 = x_ref[...] + gate1 * _dot(mix, wout_ref[...])
    h2 = (_rms(x1, g2_ref[...]) * (1.0 + sc2) + sh2).astype(BF16)
    acc = jnp.zeros_like(x1)
    for c in range(D_FF // FF_TILE):
        sl = slice(c * FF_TILE, (c + 1) * FF_TILE)
        u = jnp.maximum(_dot(h2, wup_ref[:, sl]), 0.0)
        acc = acc + _dot((u * u).astype(BF16), wdown_ref[sl, :])
    y_ref[...] = x1 + gate2 * acc


def _post(x, hm, ha, mod3, mod_row0, mod_per_batch, wts):
    shape = x.shape
    if not mod_per_batch:
        x, hm, ha = (a.reshape(1, -1, a.shape[-1]) for a in (x, hm, ha))
    B, T, _ = x.shape
    tm = POST_TILE
    tok = lambda w: pl.BlockSpec((None, tm, w), lambda b, i: (b, i, 0))
    return _post_call(x, hm, ha, mod3, mod_row0, mod_per_batch, wts, B, T, tm, tok).reshape(shape)


def _post_call(x, hm, ha, mod3, mod_row0, mod_per_batch, wts, B, T, tm, tok):
    return pl.pallas_call(
        functools.partial(_post_kernel, mod_row0, mod_per_batch),
        grid=(B, T // tm),
        in_specs=[tok(D_MODEL), tok(M_WIDTH), tok(A_WIDTH),
                  _const_spec((8, 6 * D_MODEL)),
                  _const_spec((1, D_MODEL)),
                  _const_spec((M_WIDTH + A_WIDTH, D_MODEL)),
                  _const_spec((D_MODEL, D_FF)),
                  _const_spec((D_FF, D_MODEL))],
        out_specs=tok(D_MODEL),
        out_shape=jax.ShapeDtypeStruct((B, T, D_MODEL), F32),
        compiler_params=_params(("parallel", "parallel")),
        name="post",
    )(x, hm, ha, mod3, wts["g2"], wts["w_out"], wts["w_up"], wts["w_down"])


def _prepare_weights(norm1_g, norm2_g, w_in, mlstm_gate_b, q_lora_g, kv_lora_g, w_q_up, w_kv_up,
                     q_head_g, k_head_g, w_out, w_mlp_up, w_mlp_down):
    o_g = 4 * M_WIDTH
    o_q = o_g + N_GATES
    o_kv = o_q + Q_LORA
    o_kr = o_kv + KV_LORA
    half = A_ROPE // 2
    n_dh = 2 * M_HEADS
    wt = w_in.T.astype(BF16)
    w_gate = wt[o_g:o_q].reshape(2, 2, M_HEADS, D_MODEL)
    bias = mlstm_gate_b.reshape(2, 2, M_HEADS)

    def rot_partner(a):
        z = jnp.zeros(a.shape[:-1] + (A_NOPE,), a.dtype)
        return jnp.concatenate([z, a[..., A_NOPE + half:A_QK], a[..., A_NOPE:A_NOPE + half]], axis=-1)

    pad_tile = lambda a: jnp.pad(a, [(0, 0)] * (a.ndim - 1) + [(0, HEAD_PAD - A_QK)])
    w_kr = wt[o_kr:o_kr + A_ROPE]
    w_kr_partner = jnp.concatenate([w_kr[half:], w_kr[:half]], axis=0)
    w_lat = jnp.concatenate([wt[o_q:o_kr], w_gate[:, 0].reshape(n_dh, D_MODEL), w_gate[:, 1].reshape(n_dh, D_MODEL),
                             jnp.zeros((A_NOPE - 2 * n_dh, D_MODEL), BF16), w_kr, w_kr_partner], axis=0)
    gate_bias = jnp.pad(jnp.stack([bias[:, 0, :].reshape(n_dh), bias[:, 1, :].reshape(n_dh)], axis=0),
                        ((0, 0), (0, LANES - n_dh)))
    w_q3 = w_q_up.reshape(Q_LORA, A_HEADS, A_QK)
    w_q = pad_tile(w_q3).reshape(Q_LORA, QK_PAD)
    w_q_partner = jnp.concatenate([w_q3[:, :, A_NOPE + half:], w_q3[:, :, A_NOPE:A_NOPE + half]],
                                  axis=-1).reshape(Q_LORA, A_HEADS * A_ROPE)
    w_kv3 = w_kv_up.reshape(KV_LORA, A_HEADS, A_NOPE + A_VDIM)
    w_k = jnp.pad(w_kv3[:, :, :A_NOPE], ((0, 0), (0, 0), (0, HEAD_PAD - A_NOPE)))
    w_v = w_kv3[:, :, A_NOPE:]
    w_kv = jnp.concatenate([w_k.reshape(KV_LORA, QK_PAD), w_v.reshape(KV_LORA, A_WIDTH)], axis=1)
    pad_head = lambda g: jnp.stack([pad_tile(g), pad_tile(rot_partner(g))], axis=0)
    return {
        "g1": norm1_g.reshape(1, D_MODEL),
        "g2": norm2_g.reshape(1, D_MODEL),
        "w_main": jnp.concatenate([wt[0:M_WIDTH], wt[2 * M_WIDTH:o_g]], axis=0),
        "w_kt": wt[M_WIDTH:2 * M_WIDTH],
        "w_lat": w_lat,
        "gate_bias": gate_bias,
        "q_lora_g": q_lora_g.reshape(1, Q_LORA),
        "kv_lora_g": kv_lora_g.reshape(1, KV_LORA),
        "w_q": w_q.astype(BF16),
        "w_q_rot": jnp.concatenate([w_q, w_q_partner], axis=1).astype(BF16),
        "w_kv": w_kv.astype(BF16),
        "q_head_g": pad_head(q_head_g),
        "k_head_g": pad_head(k_head_g),
        "w_out_f32": w_out,
        "w_up_f32": w_mlp_up,
        "w_down_f32": w_mlp_down,
    }


def _rope_tables(T):
    rows = T // GRID_W
    row = np.repeat(np.arange(rows, dtype=np.float32), GRID_W)
    col = np.tile(np.arange(GRID_W, dtype=np.float32), rows)
    half = A_ROPE // 2
    inv = (np.float32(ROPE_BASE) ** (-np.arange(0, half, 2, dtype=np.float32) / np.float32(half))).astype(np.float32)
    ang = np.concatenate([row[:, None] * inv, col[:, None] * inv], axis=-1)
    cos, sin = np.cos(ang), np.sin(ang)
    ones = np.ones((T, A_NOPE), np.float32)
    z = lambda w: np.zeros((T, w), np.float32)
    tail = LANES - A_QK
    cos_t = np.concatenate([ones, cos, cos, z(tail)], axis=1)
    sin_t = np.concatenate([z(A_NOPE), -sin, sin, z(tail)], axis=1)
    return jnp.asarray(np.stack([cos_t, sin_t], axis=0).astype(np.float32))


def _layer_pass(x, mod3, mod_row0, mod_per_batch, wts, norm_g, rope_tab, init_state, ctx_kv, is_context):
    pre = _pre(x, mod3, mod_row0, mod_per_batch, wts, rope_tab, emit_cache=is_context)
    mq, mkt, mv, mo, stats, q, k, v = pre[:8]
    if is_context:
        wts = dict(wts, w_out=pre[10], w_up=pre[11], w_down=pre[12])
    ml = _mlstm(mq, mkt, mv, mo, stats, norm_g, init_state, emit_state=is_context,
                heads=M_HEADS if is_context else 1, seqs=MLSTM_CONTEXT_SEQS if is_context else 1)
    ha = _attn(q, k, v, ctx_kv, seqs=ATTN_CONTEXT_SEQS if is_context else 1)
    y = _post(x, ml[0], ha, mod3, mod_row0, mod_per_batch, wts)
    return y, pre[8:10], ml[1:], wts


def kernel(x_prompt, x_sample, cache_mla_ckv, cache_mla_krope, state_mlstm_C, state_mlstm_n, state_mlstm_m,
           c, c_ctx, norm1_g, norm2_g, w_ada, b_ada, w_in, mlstm_gate_b, mlstm_norm_g,
           q_lora_g, kv_lora_g, w_q_up, w_kv_up, q_head_g, k_head_g, w_out, w_mlp_up, w_mlp_down):
    depth = w_in.shape[0]
    Bd = x_sample.shape[0]
    cond8 = jnp.concatenate([c_ctx[None, :], c, jnp.zeros((8 - 1 - Bd, D_MODEL), F32)], axis=0)
    rope_tab = _rope_tables(x_sample.shape[1])

    y, z = x_prompt, x_sample
    ckvs, kropes, Cs, ns, ms = [], [], [], [], []
    for l in range(depth):
        wts = _prepare_weights(norm1_g[l], norm2_g[l], w_in[l], mlstm_gate_b[l], q_lora_g[l], kv_lora_g[l],
                               w_q_up[l], w_kv_up[l], q_head_g[l], k_head_g[l], w_out[l], w_mlp_up[l],
                               w_mlp_down[l])
        norm_g = mlstm_norm_g[l].reshape(M_HEADS, 1, M_HEAD_DIM)
        mod3 = _ada(cond8, w_ada[l], b_ada[l])
        y, (ckv, krope), (C_new, n_new, m_new), wts = _layer_pass(
            y, mod3, 0, 0, wts, norm_g, None, None, None, True)
        ckvs.append(ckv)
        kropes.append(krope)
        Cs.append(C_new)
        ns.append(n_new[:, :, :, 0, :])
        ms.append(m_new[:, :, :, 0].transpose(0, 2, 1))

        init_state = (state_mlstm_C[:, l],
                      state_mlstm_n[:, l][:, :, :, None, :],
                      jnp.broadcast_to(state_mlstm_m[:, l].transpose(0, 2, 1)[..., None],
                                       (Bd, M_HEADS, 2, LANES)))
        krope_placed = jnp.pad(cache_mla_krope[:, l], ((0, 0), (0, 0), (A_NOPE, LANES - A_QK)))
        ctx = (cache_mla_ckv[:, l], krope_placed, wts["w_kv"], wts["k_head_g"])
        z, _, _, _ = _layer_pass(z, mod3, 1, 1, wts, norm_g, rope_tab, init_state, ctx, False)

    return (y, z, jnp.stack(ckvs, axis=1), jnp.stack(kropes, axis=1), jnp.stack(Cs, axis=1),
            jnp.stack(ns, axis=1), jnp.stack(ms, axis=1))
```

```python
import functools

import jax
import jax.numpy as jnp
import numpy as np
from jax import lax
from jax.experimental import pallas as pl
from jax.experimental.pallas import tpu as pltpu

F32 = jnp.float32
BF16 = jnp.bfloat16

D_MODEL = 1024
GRID_W = 64
M_HEADS = 4
M_HEAD_DIM = 128
M_WIDTH = M_HEADS * M_HEAD_DIM
M_BLOCK = 256
A_HEADS = 8
A_NOPE = 64
A_ROPE = 32
A_QK = A_NOPE + A_ROPE
A_VDIM = 64
A_WIDTH = A_HEADS * A_VDIM
Q_LORA = 384
KV_LORA = 256
ROPE_BASE = 10000.0
D_FF = 4 * D_MODEL
EPS = 1e-6

LANES = 128
SUBLANES = 8
LOG2E = 1.4426950408889634
HEAD_PAD = LANES
QK_PAD = A_HEADS * HEAD_PAD
N_GATES = 4 * M_HEADS
LAT_WIDTH = Q_LORA + KV_LORA + LANES
VMEM_LIMIT = 56 * 1024 * 1024

TOKEN_TILE = 512
LONG_TOKEN_TILE = 1024
POST_TILE = 512
MLSTM_CONTEXT_SEQS = 4
MLSTM_LATENT_HEADS = 2
ATTN_CONTEXT_SEQS = 8
Q_TILE = 1024
Q_SUBTILE = 256
ADA_TILE_N = 1536
FF_TILE = 1024


def _dot(a, b):
    return jnp.dot(a, b, preferred_element_type=F32)


def _dot_nt(a, b):
    return lax.dot_general(a, b, (((1,), (1,)), ((), ())), preferred_element_type=F32)


def _dot_tn(a, b):
    return lax.dot_general(a, b, (((0,), (0,)), ((), ())), preferred_element_type=F32)


def _rms(x, g):
    y = x * lax.rsqrt(jnp.mean(x * x, axis=-1, keepdims=True) + EPS)
    return y * g


def _params(sem):
    return pltpu.CompilerParams(dimension_semantics=sem, vmem_limit_bytes=VMEM_LIMIT)


def _const_spec(shape):
    zeros = (0,) * len(shape)
    return pl.BlockSpec(shape, lambda *_: zeros, pipeline_mode=pl.Buffered(1))


def _ada_kernel(cond_ref, w_ref, b_ref, o_ref):
    c = cond_ref[...]
    s = (c * jax.nn.sigmoid(c)).astype(BF16)
    o_ref[...] = _dot(s, w_ref[...].astype(BF16)) + b_ref[...]


def _ada(cond8, w_ada, b_ada):
    n = w_ada.shape[1]
    return pl.pallas_call(
        _ada_kernel,
        grid=(n // ADA_TILE_N,),
        in_specs=[
            pl.BlockSpec((8, D_MODEL), lambda j: (0, 0)),
            pl.BlockSpec((D_MODEL, ADA_TILE_N), lambda j: (0, j)),
            pl.BlockSpec((1, ADA_TILE_N), lambda j: (0, j)),
        ],
        out_specs=pl.BlockSpec((8, ADA_TILE_N), lambda j: (0, j)),
        out_shape=jax.ShapeDtypeStruct((8, n), F32),
        compiler_params=_params(("parallel",)),
        name="ada",
    )(cond8, w_ada, b_ada.reshape(1, n))


def _write_heads(src, extra, g_pad, rot, dst_ref):
    for h in range(A_HEADS):
        sl = slice(h * HEAD_PAD, (h + 1) * HEAD_PAD)
        xh = src[:, sl]
        if extra is not None:
            xh = xh + extra
        ss = jnp.sum(xh * xh, axis=-1, keepdims=True) * (1.0 / A_QK)
        r = lax.rsqrt(ss + EPS)
        if rot is None:
            y = xh * r * g_pad
        else:
            partners, cos_g, sin_g = rot
            y = (xh * cos_g + partners[h] * sin_g) * r
        dst_ref[:, sl] = y.astype(dst_ref.dtype)


def _time_scan(x, op, identity, reverse):
    n = x.shape[0]
    row = lax.broadcasted_iota(jnp.int32, x.shape, 0)
    shift = 1
    while shift < n:
        if shift < SUBLANES:
            if reverse:
                moved = jnp.where(row < n - shift, pltpu.roll(x, n - shift, 0), identity)
            else:
                moved = jnp.where(row >= shift, pltpu.roll(x, shift, 0), identity)
        else:
            fill = jnp.full((shift, x.shape[1]), identity, x.dtype)
            moved = (jnp.concatenate([x[shift:], fill], axis=0) if reverse
                     else jnp.concatenate([fill, x[:n - shift]], axis=0))
        x = op(x, moved)
        shift *= 2
    return x


def _mod_row(mod_ref, row0, per_batch):
    if per_batch:
        return mod_ref[pl.ds(row0 + pl.program_id(0) * per_batch, 1), :]
    return mod_ref[row0:row0 + 1, :]


def _pre_kernel(has_rope, emit_cache, mod_row0, mod_per_batch, *refs):
    (x_ref, mod_ref, g1_ref, wmain_ref, wkt_ref, wlat_ref, gbias_ref, qlg_ref, kvg_ref, wq_ref, wkv_ref,
     qhg_ref, khg_ref) = refs[:13]
    pos = 13
    if has_rope:
        rope_ref = refs[pos]
        pos += 1
    if emit_cache:
        cast_in = refs[pos:pos + 3]
        pos += 3
    (mq_ref, mkt_ref, mv_ref, mo_ref, stats_ref, q_ref, k_ref, v_ref) = refs[pos:pos + 8]
    pos += 8

    x = x_ref[...]
    mod = _mod_row(mod_ref, mod_row0, mod_per_batch)
    sh1 = mod[:, 0:D_MODEL]
    sc1 = mod[:, D_MODEL:2 * D_MODEL]
    h = _rms(x, g1_ref[...]) * (1.0 + sc1) + sh1
    hb = h.astype(BF16)

    plat = _dot_nt(hb, wlat_ref[...])
    q_lat = plat[:, 0:Q_LORA]
    kv_lat = plat[:, Q_LORA:Q_LORA + KV_LORA]
    tail = plat[:, Q_LORA + KV_LORA:LAT_WIDTH]
    tail_f = pltpu.roll(tail, LANES - 2 * M_HEADS, 1)
    tail2 = pltpu.roll(tail, LANES - A_ROPE, 1)

    lane = lax.broadcasted_iota(jnp.int32, (1, LANES), 1)
    fwd = lane < M_HEADS
    gate_i = tail + gbias_ref[0:1, :]
    gate_f = tail_f + gbias_ref[1:2, :]
    log_f = jnp.minimum(gate_f, 0.0) - jnp.log1p(jnp.exp(-jnp.abs(gate_f)))
    def block_scan(v, op, identity):
        parts = [v[r:r + M_BLOCK] for r in range(0, v.shape[0], M_BLOCK)]
        return jnp.where(fwd, jnp.concatenate([_time_scan(p, op, identity, False) for p in parts], axis=0),
                         jnp.concatenate([_time_scan(p, op, identity, True) for p in parts], axis=0))

    b = block_scan(log_f, jnp.add, 0.0)
    a = gate_i - b
    amax = block_scan(a, jnp.maximum, -jnp.inf)
    stats_ref[0] = b.T[0:2 * M_HEADS, :]
    stats_ref[1] = a.T[0:2 * M_HEADS, :]
    stats_ref[2] = amax.T[0:2 * M_HEADS, :]

    krope_placed = jnp.where((lane >= A_NOPE) & (lane < A_QK), tail, 0.0)
    ckv = _rms(kv_lat, kvg_ref[...])
    qn = _rms(q_lat, qlg_ref[...])
    qf = _dot(qn.astype(BF16), wq_ref[...])
    kvf = _dot(ckv.astype(BF16), wkv_ref[...])
    v_ref[...] = kvf[:, QK_PAD:QK_PAD + A_WIDTH].astype(BF16)
    q_rot = k_rot = None
    qhg = qhg_ref[...] * (A_QK ** -0.5 * LOG2E)
    if has_rope:
        cos_t, sin_t = rope_ref[0], rope_ref[1]
        q_partners = []
        for h in range(A_HEADS):
            packed = qf[:, QK_PAD + (h // 4) * LANES:QK_PAD + (h // 4 + 1) * LANES]
            shift = (A_NOPE - (h % 4) * A_ROPE) % LANES
            q_partners.append(pltpu.roll(packed, shift, 1) if shift else packed)
        q_rot = (q_partners, cos_t * qhg[0:1, :], sin_t * qhg[1:2, :])
        k_rot = ([tail2] * A_HEADS, cos_t * khg_ref[0:1, :], sin_t * khg_ref[1:2, :])
    _write_heads(qf, None, qhg[0:1, :], q_rot, q_ref)
    _write_heads(kvf, krope_placed, khg_ref[0:1, :], k_rot, k_ref)

    if emit_cache:
        ckv_ref, krope_ref = refs[pos:pos + 2]
        ckv_ref[...] = ckv
        krope_ref[...] = tail[:, A_NOPE:A_QK]
        for src_ref, dst_ref in zip(cast_in, refs[pos + 2:pos + 5]):
            dst_ref[...] = src_ref[...].astype(BF16)

    pm = _dot_nt(hb, wmain_ref[...])
    mq_ref[...] = pm[:, 0:M_WIDTH].astype(BF16)
    mv_ref[...] = pm[:, M_WIDTH:2 * M_WIDTH].astype(BF16)
    mo_ref[...] = pm[:, 2 * M_WIDTH:3 * M_WIDTH]
    mkt_ref[...] = _dot_nt(wkt_ref[...], hb) * (M_HEAD_DIM ** -0.5)


def _pre(x, mod3, mod_row0, mod_per_batch, wts, rope_tab, emit_cache):
    shape = x.shape
    tm = LONG_TOKEN_TILE if shape[1] >= LONG_TOKEN_TILE else TOKEN_TILE
    if shape[1] < tm:
        assert not mod_per_batch and tm % shape[1] == 0
        x = x.reshape(-1, tm, shape[2])
    B, T, _ = x.shape
    has_rope = rope_tab is not None
    tok = lambda w: pl.BlockSpec((None, tm, w), lambda b, i: (b, i, 0))
    in_specs = [
        tok(D_MODEL),
        _const_spec((8, 6 * D_MODEL)),
        _const_spec((1, D_MODEL)),
        _const_spec((3 * M_WIDTH, D_MODEL)),
        _const_spec((M_WIDTH, D_MODEL)),
        _const_spec((LAT_WIDTH, D_MODEL)),
        _const_spec((2, LANES)),
        _const_spec((1, Q_LORA)),
        _const_spec((1, KV_LORA)),
        _const_spec((Q_LORA, QK_PAD + A_HEADS * A_ROPE if has_rope else QK_PAD)),
        _const_spec((KV_LORA, QK_PAD + A_WIDTH)),
        _const_spec((2, HEAD_PAD)),
        _const_spec((2, HEAD_PAD)),
    ]
    assert tm % M_BLOCK == 0
    args = [x, mod3, wts["g1"], wts["w_main"], wts["w_kt"], wts["w_lat"], wts["gate_bias"], wts["q_lora_g"],
            wts["kv_lora_g"], wts["w_q_rot"] if has_rope else wts["w_q"], wts["w_kv"],
            wts["q_head_g"], wts["k_head_g"]]
    if has_rope:
        in_specs.append(pl.BlockSpec((2, tm, HEAD_PAD), lambda b, i: (0, i, 0)))
        args.append(rope_tab)
    out_specs = [tok(M_WIDTH),
                 pl.BlockSpec((None, M_WIDTH, tm), lambda b, i: (b, 0, i)),
                 tok(M_WIDTH), tok(M_WIDTH),
                 pl.BlockSpec((None, 3, 2 * M_HEADS, tm), lambda b, i: (b, 0, 0, i)),
                 tok(QK_PAD), tok(QK_PAD), tok(A_WIDTH)]
    out_shape = [
        jax.ShapeDtypeStruct((B, T, M_WIDTH), BF16),
        jax.ShapeDtypeStruct((B, M_WIDTH, T), F32),
        jax.ShapeDtypeStruct((B, T, M_WIDTH), BF16),
        jax.ShapeDtypeStruct((B, T, M_WIDTH), F32),
        jax.ShapeDtypeStruct((B, 3, 2 * M_HEADS, T), F32),
        jax.ShapeDtypeStruct((B, T, QK_PAD), BF16),
        jax.ShapeDtypeStruct((B, T, QK_PAD), BF16),
        jax.ShapeDtypeStruct((B, T, A_WIDTH), BF16),
    ]
    if emit_cache:
        out_specs += [tok(KV_LORA), tok(A_ROPE)]
        out_shape += [jax.ShapeDtypeStruct((B, T, KV_LORA), F32),
                      jax.ShapeDtypeStruct((B, T, A_ROPE), F32)]
        steps, per_b = B * (T // tm), T // tm
        for w in (wts["w_out_f32"], wts["w_up_f32"], wts["w_down_f32"]):
            rows = w.shape[0] // steps
            assert rows * steps == w.shape[0] and rows % (2 * SUBLANES) == 0
            spec = pl.BlockSpec((rows, w.shape[1]), lambda b, i: (b * per_b + i, 0))
            in_specs.append(spec)
            args.append(w)
            out_specs.append(spec)
            out_shape.append(jax.ShapeDtypeStruct(w.shape, BF16))
    outs = pl.pallas_call(
        functools.partial(_pre_kernel, has_rope, emit_cache, mod_row0, mod_per_batch),
        grid=(B, T // tm),
        in_specs=in_specs,
        out_specs=out_specs,
        out_shape=out_shape,
        compiler_params=_params(("parallel", "parallel")),
        name="pre_latent" if has_rope else "pre_context",
    )(*args)
    keep = (1, 4, 10, 11, 12)
    return [o if n in keep else o.reshape(shape[:2] + o.shape[2:]) for n, o in enumerate(outs)]


def _rows_to_lane_broadcast(rows, spread):
    x = jnp.concatenate(rows, axis=0)
    p1 = x.astype(BF16)
    r1 = x - p1.astype(F32)
    p2 = r1.astype(BF16)
    p3 = (r1 - p2.astype(F32)).astype(BF16)
    pad = jnp.zeros((spread.shape[0] - 3 * len(rows), x.shape[1]), BF16)
    return _dot_tn(jnp.concatenate([p1, p2, p3, pad], axis=0), spread)


def _mlstm_gate_rows(b_row, a_row, amax_row, forward, m):
    L = b_row.shape[1]
    last = slice(L - 1, L) if forward else slice(0, 1)
    total = b_row[:, last]
    g_row = jnp.maximum(m, amax_row)
    m_new = total + jnp.maximum(m, amax_row[:, last])
    w_key_row = jnp.exp2((a_row + (total - m_new)) * LOG2E)
    decay = jnp.exp(total + m - m_new)
    return g_row * LOG2E, (b_row + g_row) * LOG2E, a_row * LOG2E, w_key_row, decay, m_new


def _mlstm_block(s_raw, q, kt, v_aug, g2, mt2, a2_row, w_key_row, decay, allow, CN, m):
    w_intra = jnp.exp2(jnp.where(allow, a2_row - jnp.concatenate([g2, g2], axis=1), -jnp.inf))
    w_inter = jnp.exp2(m * LOG2E - g2)
    s = (s_raw * w_intra).astype(BF16)
    nd = _dot(s, v_aug) + jnp.concatenate([w_inter, w_inter], axis=1) * _dot(q, CN.astype(BF16))
    num, den = nd[:, 0:M_HEAD_DIM], nd[:, M_HEAD_DIM:2 * M_HEAD_DIM]
    h = num / jnp.maximum(jnp.abs(den), jnp.exp2(-mt2))
    CN_new = decay * CN + _dot((kt * w_key_row).astype(BF16), v_aug)
    return h, CN_new


def _mlstm_kernel(has_init, emit_state, n_blocks, heads, seqs, group, *refs):
    q_ref, kt_ref, v_ref, mo_ref, stats_ref, ng_ref, spread_ref = refs[:7]
    pos = 7
    if has_init:
        c0_ref, n0_ref, m0_ref = refs[pos:pos + 3]
        pos += 3
    hm_ref = refs[pos]
    pos += 1
    if emit_state:
        c_ref, n_ref, m_ref = refs[pos:pos + 3]
        pos += 3

    L, Dh = M_BLOCK, M_HEAD_DIM
    t_idx = lax.broadcasted_iota(jnp.int32, (L, L), 0)
    s_idx = lax.broadcasted_iota(jnp.int32, (L, L), 1)
    allow = (s_idx <= t_idx, s_idx >= t_idx)
    spread = spread_ref[...]
    ones = jnp.ones((L, Dh), BF16)

    def time_lanes(sq, c):
        start = ((sq % group) * n_blocks + c) * L
        return sq // group, slice(start, start + L)

    def lane_broadcast_n(n_row):
        return jnp.broadcast_to(n_row, (Dh, Dh)).T

    def init_state(sq, j, d):
        if has_init:
            return (jnp.concatenate([c0_ref[sq, d, j], lane_broadcast_n(n0_ref[sq, d, j])], axis=1),
                    m0_ref[sq, j, d:d + 1, 0:1])
        return jnp.zeros((Dh, 2 * Dh), F32), jnp.zeros((1, 1), F32)

    def gate_rows(sq, j, c, d, m):
        head = j if heads == M_HEADS else pl.program_id(1) * heads + j
        r, (g, lanes) = pl.ds(d * M_HEADS + head, 1), time_lanes(sq, c)
        return _mlstm_gate_rows(stats_ref[g, 0, r, lanes], stats_ref[g, 1, r, lanes], stats_ref[g, 2, r, lanes],
                                d == 0, m)

    def blocks(sq, j, jobs, states):
        loaded, rows6, cols_in = {}, [], []
        for (c, d), (CN, m) in zip(jobs, states):
            rows6.append(gate_rows(sq, j, c, d, m))
            cols_in += [rows6[-1][0], rows6[-1][1]]
            if c not in loaded:
                rows, cols = slice(c * L, (c + 1) * L), slice(j * Dh, (j + 1) * Dh)
                g, lanes = time_lanes(sq, c)
                q, kt = q_ref[sq, rows, cols], kt_ref[g, cols, lanes]
                v_aug = jnp.concatenate([v_ref[sq, rows, cols], ones], axis=1)
                loaded[c] = (_dot(q, kt.astype(BF16)), q, kt, v_aug)
        cols_out = _rows_to_lane_broadcast(cols_in, spread)
        hs, new_states = [], []
        for idx, ((c, d), (CN, m)) in enumerate(zip(jobs, states)):
            g2 = cols_out[:, (2 * idx) * LANES:(2 * idx + 1) * LANES]
            mt2 = cols_out[:, (2 * idx + 1) * LANES:(2 * idx + 2) * LANES]
            _, _, a2_row, w_key_row, decay, m_new = rows6[idx]
            h, CN_new = _mlstm_block(*loaded[c], g2, mt2, a2_row, w_key_row, decay, allow[d], CN, m)
            hs.append(h)
            new_states.append((CN_new, m_new))
        return hs, new_states

    def finish(sq, j, rows, hs):
        cols = slice(j * Dh, (j + 1) * Dh)
        hn = _rms(hs, ng_ref[j])
        hm_ref[sq, rows, cols] = (hn * jax.nn.sigmoid(mo_ref[sq, rows, cols])).astype(hm_ref.dtype)

    def emit(sq, j, d, state):
        CN, m = state
        c_ref[sq, d, j] = CN[:, 0:Dh]
        n_ref[sq, d, j] = CN[:, Dh:2 * Dh].T[0:1, :]
        m_ref[sq, j, d:d + 1, :] = jnp.broadcast_to(m, (1, LANES))

    if n_blocks > 1:
        hf_scr, hb_scr = refs[pos:pos + 2]

    for sq, j in [(sq, j) for sq in range(seqs) for j in range(heads)]:
        cols = slice(j * Dh, (j + 1) * Dh)
        states = [init_state(sq, j, 0), init_state(sq, j, 1)]
        if n_blocks == 1:
            (hf, hb), states = blocks(sq, j, [(0, 0), (0, 1)], states)
            finish(sq, j, slice(0, L), hf + hb)
        else:
            for step in range(n_blocks):
                cf, cb = step, n_blocks - 1 - step
                (hf, hb), states = blocks(sq, j, [(cf, 0), (cb, 1)], states)
                hf_scr[cf * L:(cf + 1) * L, cols] = hf
                hb_scr[cb * L:(cb + 1) * L, cols] = hb
            finish(sq, j, slice(None), hf_scr[:, cols] + hb_scr[:, cols])
        if emit_state:
            emit(sq, j, 0, states[0])
            emit(sq, j, 1, states[1])


def _mlstm(mq, mkt, mv, mo, stats, norm_g, init_state, emit_state, heads, seqs):
    B, T, _ = mq.shape
    H, Dh, L = M_HEADS, M_HEAD_DIM, M_BLOCK
    nb = T // L
    w = heads * Dh
    has_init = init_state is not None
    n_rows = 4
    spread = jnp.tile(jnp.repeat(jnp.eye(n_rows, dtype=BF16), LANES, axis=1), (3, 1))
    spread = jnp.pad(spread, ((0, 2 * SUBLANES - 3 * n_rows), (0, 0)))
    tok = pl.BlockSpec((seqs, T, w), lambda b, h: (b, 0, h))
    group = mkt.shape[2] // T
    assert seqs % group == 0
    assert mkt.shape == (B // group, M_WIDTH, group * T) and stats.shape == (B // group, 3, 2 * H, group * T)
    in_specs = [tok, pl.BlockSpec((seqs // group, w, group * T), lambda b, h: (b, h, 0)), tok, tok,
                pl.BlockSpec((seqs // group, 3, 2 * H, group * T), lambda b, h: (b, 0, 0, 0)),
                pl.BlockSpec((heads, 1, Dh), lambda b, h: (h, 0, 0)),
                _const_spec((2 * SUBLANES, n_rows * LANES))]
    args = [mq, mkt, mv, mo, stats, norm_g, spread]
    state_specs = [pl.BlockSpec((seqs, 2, heads, Dh, Dh), lambda b, h: (b, 0, h, 0, 0)),
                   pl.BlockSpec((seqs, 2, heads, 1, Dh), lambda b, h: (b, 0, h, 0, 0)),
                   pl.BlockSpec((seqs, heads, 2, LANES), lambda b, h: (b, h, 0, 0))]
    if has_init:
        in_specs += state_specs
        args += list(init_state)
    out_specs = [tok]
    out_shape = [jax.ShapeDtypeStruct((B, T, M_WIDTH), BF16)]
    if emit_state:
        out_specs += state_specs
        out_shape += [jax.ShapeDtypeStruct((B, 2, H, Dh, Dh), F32),
                      jax.ShapeDtypeStruct((B, 2, H, 1, Dh), F32),
                      jax.ShapeDtypeStruct((B, H, 2, LANES), F32)]
    scratch = [] if nb == 1 else [pltpu.VMEM((T, w), F32), pltpu.VMEM((T, w), F32)]
    return pl.pallas_call(
        functools.partial(_mlstm_kernel, has_init, emit_state, nb, heads, seqs, group),
        grid=(B // seqs, H // heads),
        in_specs=in_specs,
        out_specs=out_specs,
        out_shape=out_shape,
        scratch_shapes=scratch,
        compiler_params=_params(("parallel", "parallel")),
        name="mlstm_latent" if has_init else "mlstm_context",
    )(*args)


def _attn_kernel(has_ctx, seqs, *refs):
    if has_ctx:
        q_ref, k_ref, v_ref, ckv_ref, krp_ref, wkv_ref, khg_ref, o_ref, kc_ref, vc_ref = refs

        @pl.when(pl.program_id(1) == 0)
        def _():
            kvf = _dot(ckv_ref[0].astype(BF16), wkv_ref[...])
            vc_ref[...] = kvf[:, QK_PAD:QK_PAD + A_WIDTH].astype(BF16)
            _write_heads(kvf, krp_ref[0], khg_ref[0:1, :], None, kc_ref)
    else:
        q_ref, k_ref, v_ref, o_ref = refs
    lane = lax.broadcasted_iota(jnp.int32, (1, LANES), 1)
    ones = lambda n: jnp.ones((n, LANES), BF16)
    tq = q_ref.shape[1]
    sub = min(Q_SUBTILE, tq)
    for sq, r0, pair in [(sq, r0, pair) for sq in range(seqs) for r0 in range(0, tq, sub)
                         for pair in range(A_HEADS // 2)]:
        rows = slice(r0, r0 + sub)
        vsl = slice(pair * LANES, (pair + 1) * LANES)
        v_aug = jnp.concatenate([v_ref[sq, :, vsl], ones(v_ref.shape[1])], axis=1)
        if has_ctx:
            vc_aug = jnp.concatenate([vc_ref[:, vsl], ones(vc_ref.shape[0])], axis=1)
        outs = []
        for e in range(2):
            hsl = slice((2 * pair + e) * HEAD_PAD, (2 * pair + e + 1) * HEAD_PAD)
            qh = q_ref[sq, rows, hsl]
            s = _dot_nt(qh, k_ref[sq, :, hsl])
            mx = jnp.max(s, axis=1, keepdims=True)
            if has_ctx:
                sc = _dot_nt(qh, kc_ref[:, hsl])
                mx = jnp.maximum(mx, jnp.max(sc, axis=1, keepdims=True))
            od = _dot(jnp.exp2(s - mx).astype(BF16), v_aug)
            if has_ctx:
                od = od + _dot(jnp.exp2(sc - mx).astype(BF16), vc_aug)
            outs.append(od[:, 0:LANES] / od[:, LANES:2 * LANES])
        o_ref[sq, rows, vsl] = jnp.where(lane < A_VDIM, outs[0], outs[1]).astype(o_ref.dtype)


def _attn(q, k, v, ctx, seqs):
    B, T, _ = q.shape
    tq = min(Q_TILE, T)
    has_ctx = ctx is not None
    full = lambda n, w: pl.BlockSpec((seqs, n, w), lambda b, i: (b, 0, 0))
    in_specs = [pl.BlockSpec((seqs, tq, QK_PAD), lambda b, i: (b, i, 0)), full(T, QK_PAD), full(T, A_WIDTH)]
    args = [q, k, v]
    scratch = []
    if has_ctx:
        assert seqs == 1
        P = ctx[0].shape[1]
        in_specs += [full(P, KV_LORA), full(P, HEAD_PAD), _const_spec((KV_LORA, QK_PAD + A_WIDTH)),
                     _const_spec((2, HEAD_PAD))]
        args += list(ctx)
        scratch = [pltpu.VMEM((P, QK_PAD), BF16), pltpu.VMEM((P, A_WIDTH), BF16)]
    return pl.pallas_call(
        functools.partial(_attn_kernel, has_ctx, seqs),
        grid=(B // seqs, T // tq),
        in_specs=in_specs,
        out_specs=pl.BlockSpec((seqs, tq, A_WIDTH), lambda b, i: (b, i, 0)),
        out_shape=jax.ShapeDtypeStruct((B, T, A_WIDTH), BF16),
        scratch_shapes=scratch,
        compiler_params=_params(("parallel", "arbitrary" if has_ctx else "parallel")),
        name="attn_latent" if has_ctx else "attn_context",
    )(*args)


def _post_kernel(mod_row0, mod_per_batch, x_ref, hm_ref, ha_ref, mod_ref, g2_ref, wout_ref, wup_ref, wdown_ref,
                 y_ref):
    mod = _mod_row(mod_ref, mod_row0, mod_per_batch)
    gate1 = mod[:, 2 * D_MODEL:3 * D_MODEL]
    sh2 = mod[:, 3 * D_MODEL:4 * D_MODEL]
    sc2 = mod[:, 4 * D_MODEL:5 * D_MODEL]
    gate2 = mod[:, 5 * D_MODEL:6 * D_MODEL]
    mix = jnp.concatenate([hm_ref[...], ha_ref[...]], axis=-1)
    x1 = x_ref[...] + gate1 * _dot(mix, wout_ref[...])
    h2 = (_rms(x1, g2_ref[...]) * (1.0 + sc2) + sh2).astype(BF16)
    acc = jnp.zeros_like(x1)
    for c in range(D_FF // FF_TILE):
        sl = slice(c * FF_TILE, (c + 1) * FF_TILE)
        u = jnp.maximum(_dot(h2, wup_ref[:, sl]), 0.0)
        acc = acc + _dot((u * u).astype(BF16), wdown_ref[sl, :])
    y_ref[...] = x1 + gate2 * acc


def _post(x, hm, ha, mod3, mod_row0, mod_per_batch, wts):
    shape = x.shape
    if not mod_per_batch:
        x, hm, ha = (a.reshape(1, -1, a.shape[-1]) for a in (x, hm, ha))
    B, T, _ = x.shape
    tm = POST_TILE
    tok = lambda w: pl.BlockSpec((None, tm, w), lambda b, i: (b, i, 0))
    return _post_call(x, hm, ha, mod3, mod_row0, mod_per_batch, wts, B, T, tm, tok).reshape(shape)


def _post_call(x, hm, ha, mod3, mod_row0, mod_per_batch, wts, B, T, tm, tok):
    return pl.pallas_call(
        functools.partial(_post_kernel, mod_row0, mod_per_batch),
        grid=(B, T // tm),
        in_specs=[tok(D_MODEL), tok(M_WIDTH), tok(A_WIDTH),
                  _const_spec((8, 6 * D_MODEL)),
                  _const_spec((1, D_MODEL)),
                  _const_spec((M_WIDTH + A_WIDTH, D_MODEL)),
                  _const_spec((D_MODEL, D_FF)),
                  _const_spec((D_FF, D_MODEL))],
        out_specs=tok(D_MODEL),
        out_shape=jax.ShapeDtypeStruct((B, T, D_MODEL), F32),
        compiler_params=_params(("parallel", "parallel")),
        name="post",
    )(x, hm, ha, mod3, wts["g2"], wts["w_out"], wts["w_up"], wts["w_down"])


def _prepare_weights(norm1_g, norm2_g, w_in, mlstm_gate_b, q_lora_g, kv_lora_g, w_q_up, w_kv_up,
                     q_head_g, k_head_g, w_out, w_mlp_up, w_mlp_down):
    o_g = 4 * M_WIDTH
    o_q = o_g + N_GATES
    o_kv = o_q + Q_LORA
    o_kr = o_kv + KV_LORA
    half = A_ROPE // 2
    n_dh = 2 * M_HEADS
    wt = w_in.T.astype(BF16)
    w_gate = wt[o_g:o_q].reshape(2, 2, M_HEADS, D_MODEL)
    bias = mlstm_gate_b.reshape(2, 2, M_HEADS)

    def rot_partner(a):
        z = jnp.zeros(a.shape[:-1] + (A_NOPE,), a.dtype)
        return jnp.concatenate([z, a[..., A_NOPE + half:A_QK], a[..., A_NOPE:A_NOPE + half]], axis=-1)

    pad_tile = lambda a: jnp.pad(a, [(0, 0)] * (a.ndim - 1) + [(0, HEAD_PAD - A_QK)])
    w_kr = wt[o_kr:o_kr + A_ROPE]
    w_kr_partner = jnp.concatenate([w_kr[half:], w_kr[:half]], axis=0)
    w_lat = jnp.concatenate([wt[o_q:o_kr], w_gate[:, 0].reshape(n_dh, D_MODEL), w_gate[:, 1].reshape(n_dh, D_MODEL),
                             jnp.zeros((A_NOPE - 2 * n_dh, D_MODEL), BF16), w_kr, w_kr_partner], axis=0)
    gate_bias = jnp.pad(jnp.stack([bias[:, 0, :].reshape(n_dh), bias[:, 1, :].reshape(n_dh)], axis=0),
                        ((0, 0), (0, LANES - n_dh)))
    w_q3 = w_q_up.reshape(Q_LORA, A_HEADS, A_QK)
    w_q = pad_tile(w_q3).reshape(Q_LORA, QK_PAD)
    w_q_partner = jnp.concatenate([w_q3[:, :, A_NOPE + half:], w_q3[:, :, A_NOPE:A_NOPE + half]],
                                  axis=-1).reshape(Q_LORA, A_HEADS * A_ROPE)
    w_kv3 = w_kv_up.reshape(KV_LORA, A_HEADS, A_NOPE + A_VDIM)
    w_k = jnp.pad(w_kv3[:, :, :A_NOPE], ((0, 0), (0, 0), (0, HEAD_PAD - A_NOPE)))
    w_v = w_kv3[:, :, A_NOPE:]
    w_kv = jnp.concatenate([w_k.reshape(KV_LORA, QK_PAD), w_v.reshape(KV_LORA, A_WIDTH)], axis=1)
    pad_head = lambda g: jnp.stack([pad_tile(g), pad_tile(rot_partner(g))], axis=0)
    return {
        "g1": norm1_g.reshape(1, D_MODEL),
        "g2": norm2_g.reshape(1, D_MODEL),
        "w_main": jnp.concatenate([wt[0:M_WIDTH], wt[2 * M_WIDTH:o_g]], axis=0),
        "w_kt": wt[M_WIDTH:2 * M_WIDTH],
        "w_lat": w_lat,
        "gate_bias": gate_bias,
        "q_lora_g": q_lora_g.reshape(1, Q_LORA),
        "kv_lora_g": kv_lora_g.reshape(1, KV_LORA),
        "w_q": w_q.astype(BF16),
        "w_q_rot": jnp.concatenate([w_q, w_q_partner], axis=1).astype(BF16),
        "w_kv": w_kv.astype(BF16),
        "q_head_g": pad_head(q_head_g),
        "k_head_g": pad_head(k_head_g),
        "w_out_f32": w_out,
        "w_up_f32": w_mlp_up,
        "w_down_f32": w_mlp_down,
    }


def _rope_tables(T):
    rows = T // GRID_W
    row = np.repeat(np.arange(rows, dtype=np.float32), GRID_W)
    col = np.tile(np.arange(GRID_W, dtype=np.float32), rows)
    half = A_ROPE // 2
    inv = (np.float32(ROPE_BASE) ** (-np.arange(0, half, 2, dtype=np.float32) / np.float32(half))).astype(np.float32)
    ang = np.concatenate([row[:, None] * inv, col[:, None] * inv], axis=-1)
    cos, sin = np.cos(ang), np.sin(ang)
    ones = np.ones((T, A_NOPE), np.float32)
    z = lambda w: np.zeros((T, w), np.float32)
    tail = LANES - A_QK
    cos_t = np.concatenate([ones, cos, cos, z(tail)], axis=1)
    sin_t = np.concatenate([z(A_NOPE), -sin, sin, z(tail)], axis=1)
    return jnp.asarray(np.stack([cos_t, sin_t], axis=0).astype(np.float32))


def _layer_pass(x, mod3, mod_row0, mod_per_batch, wts, norm_g, rope_tab, init_state, ctx_kv, is_context):
    pre = _pre(x, mod3, mod_row0, mod_per_batch, wts, rope_tab, emit_cache=is_context)
    mq, mkt, mv, mo, stats, q, k, v = pre[:8]
    if is_context:
        wts = dict(wts, w_out=pre[10], w_up=pre[11], w_down=pre[12])
    ml = _mlstm(mq, mkt, mv, mo, stats, norm_g, init_state, emit_state=is_context,
                heads=M_HEADS if is_context else MLSTM_LATENT_HEADS,
                seqs=MLSTM_CONTEXT_SEQS if is_context else 1)
    ha = _attn(q, k, v, ctx_kv, seqs=ATTN_CONTEXT_SEQS if is_context else 1)
    y = _post(x, ml[0], ha, mod3, mod_row0, mod_per_batch, wts)
    return y, pre[8:10], ml[1:], wts


def kernel(x_prompt, x_sample, cache_mla_ckv, cache_mla_krope, state_mlstm_C, state_mlstm_n, state_mlstm_m,
           c, c_ctx, norm1_g, norm2_g, w_ada, b_ada, w_in, mlstm_gate_b, mlstm_norm_g,
           q_lora_g, kv_lora_g, w_q_up, w_kv_up, q_head_g, k_head_g, w_out, w_mlp_up, w_mlp_down):
    depth = w_in.shape[0]
    Bd = x_sample.shape[0]
    cond8 = jnp.concatenate([c_ctx[None, :], c, jnp.zeros((8 - 1 - Bd, D_MODEL), F32)], axis=0)
    rope_tab = _rope_tables(x_sample.shape[1])

    y, z = x_prompt, x_sample
    ckvs, kropes, Cs, ns, ms = [], [], [], [], []
    for l in range(depth):
        wts = _prepare_weights(norm1_g[l], norm2_g[l], w_in[l], mlstm_gate_b[l], q_lora_g[l], kv_lora_g[l],
                               w_q_up[l], w_kv_up[l], q_head_g[l], k_head_g[l], w_out[l], w_mlp_up[l],
                               w_mlp_down[l])
        norm_g = mlstm_norm_g[l].reshape(M_HEADS, 1, M_HEAD_DIM)
        mod3 = _ada(cond8, w_ada[l], b_ada[l])
        y, (ckv, krope), (C_new, n_new, m_new), wts = _layer_pass(
            y, mod3, 0, 0, wts, norm_g, None, None, None, True)
        ckvs.append(ckv)
        kropes.append(krope)
        Cs.append(C_new)
        ns.append(n_new[:, :, :, 0, :])
        ms.append(m_new[:, :, :, 0].transpose(0, 2, 1))

        init_state = (state_mlstm_C[:, l],
                      state_mlstm_n[:, l][:, :, :, None, :],
                      jnp.broadcast_to(state_mlstm_m[:, l].transpose(0, 2, 1)[..., None],
                                       (Bd, M_HEADS, 2, LANES)))
        krope_placed = jnp.pad(cache_mla_krope[:, l], ((0, 0), (0, 0), (A_NOPE, LANES - A_QK)))
        ctx = (cache_mla_ckv[:, l], krope_placed, wts["w_kv"], wts["k_head_g"])
        z, _, _, _ = _layer_pass(z, mod3, 1, 1, wts, norm_g, rope_tab, init_state, ctx, False)

    return (y, z, jnp.stack(ckvs, axis=1), jnp.stack(kropes, axis=1), jnp.stack(Cs, axis=1),
            jnp.stack(ns, axis=1), jnp.stack(ms, axis=1))
```

```python
import functools

import jax
import jax.numpy as jnp
import numpy as np
from jax import lax
from jax.experimental import pallas as pl
from jax.experimental.pallas import tpu as pltpu

F32 = jnp.float32
BF16 = jnp.bfloat16

D_MODEL = 1024
GRID_W = 64
M_HEADS = 4
M_HEAD_DIM = 128
M_WIDTH = M_HEADS * M_HEAD_DIM
M_BLOCK = 256
A_HEADS = 8
A_NOPE = 64
A_ROPE = 32
A_QK = A_NOPE + A_ROPE
A_VDIM = 64
A_WIDTH = A_HEADS * A_VDIM
Q_LORA = 384
KV_LORA = 256
ROPE_BASE = 10000.0
D_FF = 4 * D_MODEL
EPS = 1e-6

LANES = 128
SUBLANES = 8
LOG2E = 1.4426950408889634
HEAD_PAD = LANES
QK_PAD = A_HEADS * HEAD_PAD
N_GATES = 4 * M_HEADS
LAT_WIDTH = Q_LORA + KV_LORA + LANES
VMEM_LIMIT = 56 * 1024 * 1024

TOKEN_TILE = 512
LONG_TOKEN_TILE = 1024
POST_TILE = 512
MLSTM_CONTEXT_SEQS = 4
ATTN_CONTEXT_SEQS = 8
Q_TILE = 1024
Q_SUBTILE = 256
ADA_TILE_N = 1536
FF_TILE = 1024


def _dot(a, b):
    return jnp.dot(a, b, preferred_element_type=F32)


def _dot_nt(a, b):
    return lax.dot_general(a, b, (((1,), (1,)), ((), ())), preferred_element_type=F32)


def _dot_tn(a, b):
    return lax.dot_general(a, b, (((0,), (0,)), ((), ())), preferred_element_type=F32)


def _rms(x, g):
    y = x * lax.rsqrt(jnp.mean(x * x, axis=-1, keepdims=True) + EPS)
    return y * g


def _params(sem):
    return pltpu.CompilerParams(dimension_semantics=sem, vmem_limit_bytes=VMEM_LIMIT)


def _const_spec(shape):
    zeros = (0,) * len(shape)
    return pl.BlockSpec(shape, lambda *_: zeros, pipeline_mode=pl.Buffered(1))


def _ada_kernel(cond_ref, w_ref, b_ref, o_ref):
    c = cond_ref[...]
    s = (c * jax.nn.sigmoid(c)).astype(BF16)
    o_ref[...] = _dot(s, w_ref[...].astype(BF16)) + b_ref[...]


def _ada(cond8, w_ada, b_ada):
    n = w_ada.shape[1]
    return pl.pallas_call(
        _ada_kernel,
        grid=(n // ADA_TILE_N,),
        in_specs=[
            pl.BlockSpec((8, D_MODEL), lambda j: (0, 0)),
            pl.BlockSpec((D_MODEL, ADA_TILE_N), lambda j: (0, j)),
            pl.BlockSpec((1, ADA_TILE_N), lambda j: (0, j)),
        ],
        out_specs=pl.BlockSpec((8, ADA_TILE_N), lambda j: (0, j)),
        out_shape=jax.ShapeDtypeStruct((8, n), F32),
        compiler_params=_params(("parallel",)),
        name="ada",
    )(cond8, w_ada, b_ada.reshape(1, n))


def _write_heads(src, extra, g_pad, rot, dst_ref):
    for h in range(A_HEADS):
        sl = slice(h * HEAD_PAD, (h + 1) * HEAD_PAD)
        xh = src[:, sl]
        if extra is not None:
            xh = xh + extra
        ss = jnp.sum(xh * xh, axis=-1, keepdims=True) * (1.0 / A_QK)
        r = lax.rsqrt(ss + EPS)
        if rot is None:
            y = xh * r * g_pad
        else:
            partners, cos_g, sin_g = rot
            y = (xh * cos_g + partners[h] * sin_g) * r
        dst_ref[:, sl] = y.astype(dst_ref.dtype)


def _time_scan(x, op, identity, reverse):
    n = x.shape[0]
    row = lax.broadcasted_iota(jnp.int32, x.shape, 0)
    shift = 1
    while shift < n:
        if shift < SUBLANES:
            if reverse:
                moved = jnp.where(row < n - shift, pltpu.roll(x, n - shift, 0), identity)
            else:
                moved = jnp.where(row >= shift, pltpu.roll(x, shift, 0), identity)
        else:
            fill = jnp.full((shift, x.shape[1]), identity, x.dtype)
            moved = (jnp.concatenate([x[shift:], fill], axis=0) if reverse
                     else jnp.concatenate([fill, x[:n - shift]], axis=0))
        x = op(x, moved)
        shift *= 2
    return x


def _mod_row(mod_ref, row0, per_batch):
    if per_batch:
        return mod_ref[pl.ds(row0 + pl.program_id(0) * per_batch, 1), :]
    return mod_ref[row0:row0 + 1, :]


def _pre_kernel(has_rope, emit_cache, mod_row0, mod_per_batch, *refs):
    (x_ref, mod_ref, g1_ref, wmain_ref, wkt_ref, wlat_ref, gbias_ref, qlg_ref, kvg_ref, wq_ref, wkv_ref,
     qhg_ref, khg_ref) = refs[:13]
    pos = 13
    if has_rope:
        rope_ref = refs[pos]
        pos += 1
    if emit_cache:
        cast_in = refs[pos:pos + 3]
        pos += 3
    (mq_ref, mkt_ref, mv_ref, mo_ref, stats_ref, q_ref, k_ref, v_ref) = refs[pos:pos + 8]
    pos += 8

    x = x_ref[...]
    mod = _mod_row(mod_ref, mod_row0, mod_per_batch)
    sh1 = mod[:, 0:D_MODEL]
    sc1 = mod[:, D_MODEL:2 * D_MODEL]
    h = _rms(x, g1_ref[...]) * (1.0 + sc1) + sh1
    hb = h.astype(BF16)

    plat = _dot_nt(hb, wlat_ref[...])
    q_lat = plat[:, 0:Q_LORA]
    kv_lat = plat[:, Q_LORA:Q_LORA + KV_LORA]
    tail = plat[:, Q_LORA + KV_LORA:LAT_WIDTH]
    tail_f = pltpu.roll(tail, LANES - 2 * M_HEADS, 1)
    tail2 = pltpu.roll(tail, LANES - A_ROPE, 1)

    lane = lax.broadcasted_iota(jnp.int32, (1, LANES), 1)
    fwd = lane < M_HEADS
    gate_i = tail + gbias_ref[0:1, :]
    gate_f = tail_f + gbias_ref[1:2, :]
    log_f = jnp.minimum(gate_f, 0.0) - jnp.log1p(jnp.exp(-jnp.abs(gate_f)))
    def block_scan(v, op, identity):
        parts = [v[r:r + M_BLOCK] for r in range(0, v.shape[0], M_BLOCK)]
        return jnp.where(fwd, jnp.concatenate([_time_scan(p, op, identity, False) for p in parts], axis=0),
                         jnp.concatenate([_time_scan(p, op, identity, True) for p in parts], axis=0))

    b = block_scan(log_f, jnp.add, 0.0)
    a = gate_i - b
    amax = block_scan(a, jnp.maximum, -jnp.inf)
    stats_ref[0] = b.T[0:2 * M_HEADS, :]
    stats_ref[1] = a.T[0:2 * M_HEADS, :]
    stats_ref[2] = amax.T[0:2 * M_HEADS, :]

    krope_placed = jnp.where((lane >= A_NOPE) & (lane < A_QK), tail, 0.0)
    ckv = _rms(kv_lat, kvg_ref[...])
    qn = _rms(q_lat, qlg_ref[...])
    qf = _dot(qn.astype(BF16), wq_ref[...])
    kvf = _dot(ckv.astype(BF16), wkv_ref[...])
    v_ref[...] = kvf[:, QK_PAD:QK_PAD + A_WIDTH].astype(BF16)
    q_rot = k_rot = None
    qhg = qhg_ref[...] * (A_QK ** -0.5 * LOG2E)
    if has_rope:
        cos_t, sin_t = rope_ref[0], rope_ref[1]
        q_partners = []
        for h in range(A_HEADS):
            packed = qf[:, QK_PAD + (h // 4) * LANES:QK_PAD + (h // 4 + 1) * LANES]
            shift = (A_NOPE - (h % 4) * A_ROPE) % LANES
            q_partners.append(pltpu.roll(packed, shift, 1) if shift else packed)
        q_rot = (q_partners, cos_t * qhg[0:1, :], sin_t * qhg[1:2, :])
        k_rot = ([tail2] * A_HEADS, cos_t * khg_ref[0:1, :], sin_t * khg_ref[1:2, :])
    _write_heads(qf, None, qhg[0:1, :], q_rot, q_ref)
    _write_heads(kvf, krope_placed, khg_ref[0:1, :], k_rot, k_ref)

    if emit_cache:
        ckv_ref, krope_ref = refs[pos:pos + 2]
        ckv_ref[...] = ckv
        krope_ref[...] = tail[:, A_NOPE:A_QK]
        for src_ref, dst_ref in zip(cast_in, refs[pos + 2:pos + 5]):
            dst_ref[...] = src_ref[...].astype(BF16)

    pm = _dot_nt(hb, wmain_ref[...])
    mq_ref[...] = pm[:, 0:M_WIDTH].astype(BF16)
    mv_ref[...] = pm[:, M_WIDTH:2 * M_WIDTH].astype(BF16)
    mo_ref[...] = pm[:, 2 * M_WIDTH:3 * M_WIDTH]
    mkt_ref[...] = _dot_nt(wkt_ref[...], hb) * (M_HEAD_DIM ** -0.5)


def _pre(x, mod3, mod_row0, mod_per_batch, wts, rope_tab, emit_cache):
    shape = x.shape
    tm = LONG_TOKEN_TILE if shape[1] >= LONG_TOKEN_TILE else TOKEN_TILE
    if shape[1] < tm:
        assert not mod_per_batch and tm % shape[1] == 0
        x = x.reshape(-1, tm, shape[2])
    B, T, _ = x.shape
    has_rope = rope_tab is not None
    tok = lambda w: pl.BlockSpec((None, tm, w), lambda b, i: (b, i, 0))
    in_specs = [
        tok(D_MODEL),
        _const_spec((8, 6 * D_MODEL)),
        _const_spec((1, D_MODEL)),
        _const_spec((3 * M_WIDTH, D_MODEL)),
        _const_spec((M_WIDTH, D_MODEL)),
        _const_spec((LAT_WIDTH, D_MODEL)),
        _const_spec((2, LANES)),
        _const_spec((1, Q_LORA)),
        _const_spec((1, KV_LORA)),
        _const_spec((Q_LORA, QK_PAD + A_HEADS * A_ROPE if has_rope else QK_PAD)),
        _const_spec((KV_LORA, QK_PAD + A_WIDTH)),
        _const_spec((2, HEAD_PAD)),
        _const_spec((2, HEAD_PAD)),
    ]
    assert tm % M_BLOCK == 0
    args = [x, mod3, wts["g1"], wts["w_main"], wts["w_kt"], wts["w_lat"], wts["gate_bias"], wts["q_lora_g"],
            wts["kv_lora_g"], wts["w_q_rot"] if has_rope else wts["w_q"], wts["w_kv"],
            wts["q_head_g"], wts["k_head_g"]]
    if has_rope:
        in_specs.append(pl.BlockSpec((2, tm, HEAD_PAD), lambda b, i: (0, i, 0)))
        args.append(rope_tab)
    out_specs = [tok(M_WIDTH),
                 pl.BlockSpec((None, M_WIDTH, tm), lambda b, i: (b, 0, i)),
                 tok(M_WIDTH), tok(M_WIDTH),
                 pl.BlockSpec((None, 3, 2 * M_HEADS, tm), lambda b, i: (b, 0, 0, i)),
                 tok(QK_PAD), tok(QK_PAD), tok(A_WIDTH)]
    out_shape = [
        jax.ShapeDtypeStruct((B, T, M_WIDTH), BF16),
        jax.ShapeDtypeStruct((B, M_WIDTH, T), F32),
        jax.ShapeDtypeStruct((B, T, M_WIDTH), BF16),
        jax.ShapeDtypeStruct((B, T, M_WIDTH), F32),
        jax.ShapeDtypeStruct((B, 3, 2 * M_HEADS, T), F32),
        jax.ShapeDtypeStruct((B, T, QK_PAD), BF16),
        jax.ShapeDtypeStruct((B, T, QK_PAD), BF16),
        jax.ShapeDtypeStruct((B, T, A_WIDTH), BF16),
    ]
    if emit_cache:
        out_specs += [tok(KV_LORA), tok(A_ROPE)]
        out_shape += [jax.ShapeDtypeStruct((B, T, KV_LORA), F32),
                      jax.ShapeDtypeStruct((B, T, A_ROPE), F32)]
        steps, per_b = B * (T // tm), T // tm
        for w in (wts["w_out_f32"], wts["w_up_f32"], wts["w_down_f32"]):
            rows = w.shape[0] // steps
            assert rows * steps == w.shape[0] and rows % (2 * SUBLANES) == 0
            spec = pl.BlockSpec((rows, w.shape[1]), lambda b, i: (b * per_b + i, 0))
            in_specs.append(spec)
            args.append(w)
            out_specs.append(spec)
            out_shape.append(jax.ShapeDtypeStruct(w.shape, BF16))
    outs = pl.pallas_call(
        functools.partial(_pre_kernel, has_rope, emit_cache, mod_row0, mod_per_batch),
        grid=(B, T // tm),
        in_specs=in_specs,
        out_specs=out_specs,
        out_shape=out_shape,
        compiler_params=_params(("parallel", "parallel")),
        name="pre_latent" if has_rope else "pre_context",
    )(*args)
    keep = (1, 4, 10, 11, 12)
    return [o if n in keep else o.reshape(shape[:2] + o.shape[2:]) for n, o in enumerate(outs)]


def _rows_to_lane_broadcast(rows, spread):
    x = jnp.concatenate(rows, axis=0)
    p1 = x.astype(BF16)
    r1 = x - p1.astype(F32)
    p2 = r1.astype(BF16)
    p3 = (r1 - p2.astype(F32)).astype(BF16)
    pad = jnp.zeros((spread.shape[0] - 3 * len(rows), x.shape[1]), BF16)
    return _dot_tn(jnp.concatenate([p1, p2, p3, pad], axis=0), spread)


def _mlstm_gate_rows(b_row, a_row, amax_row, forward, m):
    L = b_row.shape[1]
    last = slice(L - 1, L) if forward else slice(0, 1)
    total = b_row[:, last]
    g_row = jnp.maximum(m, amax_row)
    m_new = total + jnp.maximum(m, amax_row[:, last])
    w_key_row = jnp.exp2((a_row + (total - m_new)) * LOG2E)
    decay = jnp.exp(total + m - m_new)
    return g_row * LOG2E, (b_row + g_row) * LOG2E, a_row * LOG2E, w_key_row, decay, m_new


def _mlstm_block(s_raw, q, kt, v_aug, g2, mt2, a2_row, w_key_row, decay, allow, CN, m):
    w_intra = jnp.exp2(jnp.where(allow, a2_row - jnp.concatenate([g2, g2], axis=1), -jnp.inf))
    w_inter = jnp.exp2(m * LOG2E - g2)
    s = (s_raw * w_intra).astype(BF16)
    nd = _dot(s, v_aug) + jnp.concatenate([w_inter, w_inter], axis=1) * _dot(q, CN.astype(BF16))
    num, den = nd[:, 0:M_HEAD_DIM], nd[:, M_HEAD_DIM:2 * M_HEAD_DIM]
    h = num / jnp.maximum(jnp.abs(den), jnp.exp2(-mt2))
    CN_new = decay * CN + _dot((kt * w_key_row).astype(BF16), v_aug)
    return h, CN_new


def _mlstm_kernel(has_init, emit_state, n_blocks, heads, seqs, group, *refs):
    q_ref, kt_ref, v_ref, mo_ref, stats_ref, ng_ref, spread_ref = refs[:7]
    pos = 7
    if has_init:
        c0_ref, n0_ref, m0_ref = refs[pos:pos + 3]
        pos += 3
    hm_ref = refs[pos]
    pos += 1
    if emit_state:
        c_ref, n_ref, m_ref = refs[pos:pos + 3]
        pos += 3

    L, Dh = M_BLOCK, M_HEAD_DIM
    t_idx = lax.broadcasted_iota(jnp.int32, (L, L), 0)
    s_idx = lax.broadcasted_iota(jnp.int32, (L, L), 1)
    allow = (s_idx <= t_idx, s_idx >= t_idx)
    spread = spread_ref[...]
    ones = jnp.ones((L, Dh), BF16)

    def time_lanes(sq, c):
        start = ((sq % group) * n_blocks + c) * L
        return sq // group, slice(start, start + L)

    def lane_broadcast_n(n_row):
        return jnp.broadcast_to(n_row, (Dh, Dh)).T

    def init_state(sq, j, d):
        if has_init:
            return (jnp.concatenate([c0_ref[sq, d, j], lane_broadcast_n(n0_ref[sq, d, j])], axis=1),
                    m0_ref[sq, j, d:d + 1, 0:1])
        return jnp.zeros((Dh, 2 * Dh), F32), jnp.zeros((1, 1), F32)

    def gate_rows(sq, j, c, d, m):
        head = j if heads == M_HEADS else pl.program_id(1) * heads + j
        r, (g, lanes) = pl.ds(d * M_HEADS + head, 1), time_lanes(sq, c)
        return _mlstm_gate_rows(stats_ref[g, 0, r, lanes], stats_ref[g, 1, r, lanes], stats_ref[g, 2, r, lanes],
                                d == 0, m)

    def blocks(sq, j, jobs, states):
        loaded, rows6, cols_in = {}, [], []
        for (c, d), (CN, m) in zip(jobs, states):
            rows6.append(gate_rows(sq, j, c, d, m))
            cols_in += [rows6[-1][0], rows6[-1][1]]
            if c not in loaded:
                rows, cols = slice(c * L, (c + 1) * L), slice(j * Dh, (j + 1) * Dh)
                g, lanes = time_lanes(sq, c)
                q, kt = q_ref[sq, rows, cols], kt_ref[g, cols, lanes]
                v_aug = jnp.concatenate([v_ref[sq, rows, cols], ones], axis=1)
                loaded[c] = (_dot(q, kt.astype(BF16)), q, kt, v_aug)
        cols_out = _rows_to_lane_broadcast(cols_in, spread)
        hs, new_states = [], []
        for idx, ((c, d), (CN, m)) in enumerate(zip(jobs, states)):
            g2 = cols_out[:, (2 * idx) * LANES:(2 * idx + 1) * LANES]
            mt2 = cols_out[:, (2 * idx + 1) * LANES:(2 * idx + 2) * LANES]
            _, _, a2_row, w_key_row, decay, m_new = rows6[idx]
            h, CN_new = _mlstm_block(*loaded[c], g2, mt2, a2_row, w_key_row, decay, allow[d], CN, m)
            hs.append(h)
            new_states.append((CN_new, m_new))
        return hs, new_states

    def finish(sq, j, rows, hs):
        cols = slice(j * Dh, (j + 1) * Dh)
        hn = _rms(hs, ng_ref[j])
        hm_ref[sq, rows, cols] = (hn * jax.nn.sigmoid(mo_ref[sq, rows, cols])).astype(hm_ref.dtype)

    def emit(sq, j, d, state):
        CN, m = state
        c_ref[sq, d, j] = CN[:, 0:Dh]
        n_ref[sq, d, j] = CN[:, Dh:2 * Dh].T[0:1, :]
        m_ref[sq, j, d:d + 1, :] = jnp.broadcast_to(m, (1, LANES))

    if n_blocks > 1:
        hf_scr, hb_scr = refs[pos:pos + 2]

    for sq, j in [(sq, j) for sq in range(seqs) for j in range(heads)]:
        cols = slice(j * Dh, (j + 1) * Dh)
        states = [init_state(sq, j, 0), init_state(sq, j, 1)]
        if n_blocks == 1:
            (hf, hb), states = blocks(sq, j, [(0, 0), (0, 1)], states)
            finish(sq, j, slice(0, L), hf + hb)
        else:
            for step in range(n_blocks):
                cf, cb = step, n_blocks - 1 - step
                (hf, hb), states = blocks(sq, j, [(cf, 0), (cb, 1)], states)
                hf_scr[cf * L:(cf + 1) * L, cols] = hf
                hb_scr[cb * L:(cb + 1) * L, cols] = hb
            finish(sq, j, slice(None), hf_scr[:, cols] + hb_scr[:, cols])
        if emit_state:
            emit(sq, j, 0, states[0])
            emit(sq, j, 1, states[1])


def _mlstm(mq, mkt, mv, mo, stats, norm_g, init_state, emit_state, heads, seqs):
    B, T, _ = mq.shape
    H, Dh, L = M_HEADS, M_HEAD_DIM, M_BLOCK
    nb = T // L
    w = heads * Dh
    has_init = init_state is not None
    n_rows = 4
    spread = jnp.tile(jnp.repeat(jnp.eye(n_rows, dtype=BF16), LANES, axis=1), (3, 1))
    spread = jnp.pad(spread, ((0, 2 * SUBLANES - 3 * n_rows), (0, 0)))
    tok = pl.BlockSpec((seqs, T, w), lambda b, h: (b, 0, h))
    group = mkt.shape[2] // T
    assert seqs % group == 0
    assert mkt.shape == (B // group, M_WIDTH, group * T) and stats.shape == (B // group, 3, 2 * H, group * T)
    in_specs = [tok, pl.BlockSpec((seqs // group, w, group * T), lambda b, h: (b, h, 0)), tok, tok,
                pl.BlockSpec((seqs // group, 3, 2 * H, group * T), lambda b, h: (b, 0, 0, 0)),
                pl.BlockSpec((heads, 1, Dh), lambda b, h: (h, 0, 0)),
                _const_spec((2 * SUBLANES, n_rows * LANES))]
    args = [mq, mkt, mv, mo, stats, norm_g, spread]
    state_specs = [pl.BlockSpec((seqs, 2, heads, Dh, Dh), lambda b, h: (b, 0, h, 0, 0)),
                   pl.BlockSpec((seqs, 2, heads, 1, Dh), lambda b, h: (b, 0, h, 0, 0)),
                   pl.BlockSpec((seqs, heads, 2, LANES), lambda b, h: (b, h, 0, 0))]
    if has_init:
        in_specs += state_specs
        args += list(init_state)
    out_specs = [tok]
    out_shape = [jax.ShapeDtypeStruct((B, T, M_WIDTH), BF16)]
    if emit_state:
        out_specs += state_specs
        out_shape += [jax.ShapeDtypeStruct((B, 2, H, Dh, Dh), F32),
                      jax.ShapeDtypeStruct((B, 2, H, 1, Dh), F32),
                      jax.ShapeDtypeStruct((B, H, 2, LANES), F32)]
    scratch = [] if nb == 1 else [pltpu.VMEM((T, w), F32), pltpu.VMEM((T, w), F32)]
    return pl.pallas_call(
        functools.partial(_mlstm_kernel, has_init, emit_state, nb, heads, seqs, group),
        grid=(B // seqs, H // heads),
        in_specs=in_specs,
        out_specs=out_specs,
        out_shape=out_shape,
        scratch_shapes=scratch,
        compiler_params=_params(("parallel", "parallel")),
        name="mlstm_latent" if has_init else "mlstm_context",
    )(*args)


def _attn_kernel(has_ctx, seqs, *refs):
    if has_ctx:
        q_ref, k_ref, v_ref, ckv_ref, krp_ref, wkv_ref, khg_ref, o_ref, kc_ref, vc_ref = refs

        @pl.when(pl.program_id(1) == 0)
        def _():
            kvf = _dot(ckv_ref[0].astype(BF16), wkv_ref[...])
            vc_ref[...] = kvf[:, QK_PAD:QK_PAD + A_WIDTH].astype(BF16)
            _write_heads(kvf, krp_ref[0], khg_ref[0:1, :], None, kc_ref)
    else:
        q_ref, k_ref, v_ref, o_ref = refs
    lane = lax.broadcasted_iota(jnp.int32, (1, LANES), 1)
    ones = lambda n: jnp.ones((n, LANES), BF16)
    tq = q_ref.shape[1]
    sub = min(Q_SUBTILE, tq)
    n_sub = tq // sub
    per_iter = 2 if (n_sub > 2 and n_sub % 2 == 0) else n_sub

    def run(sq, base):
        for k, pair in [(k, pair) for k in range(per_iter) for pair in range(A_HEADS // 2)]:
            rows = (slice(base + k * sub, base + (k + 1) * sub) if isinstance(base, int)
                    else pl.ds(base + k * sub, sub))
            head_pair(sq, rows, pair)

    def head_pair(sq, rows, pair):
        vsl = slice(pair * LANES, (pair + 1) * LANES)
        v_aug = jnp.concatenate([v_ref[sq, :, vsl], ones(v_ref.shape[1])], axis=1)
        if has_ctx:
            vc_aug = jnp.concatenate([vc_ref[:, vsl], ones(vc_ref.shape[0])], axis=1)
        outs = []
        for e in range(2):
            hsl = slice((2 * pair + e) * HEAD_PAD, (2 * pair + e + 1) * HEAD_PAD)
            qh = q_ref[sq, rows, hsl]
            s = _dot_nt(qh, k_ref[sq, :, hsl])
            mx = jnp.max(s, axis=1, keepdims=True)
            if has_ctx:
                sc = _dot_nt(qh, kc_ref[:, hsl])
                mx = jnp.maximum(mx, jnp.max(sc, axis=1, keepdims=True))
            od = _dot(jnp.exp2(s - mx).astype(BF16), v_aug)
            if has_ctx:
                od = od + _dot(jnp.exp2(sc - mx).astype(BF16), vc_aug)
            outs.append(od[:, 0:LANES] / od[:, LANES:2 * LANES])
        o_ref[sq, rows, vsl] = jnp.where(lane < A_VDIM, outs[0], outs[1]).astype(o_ref.dtype)

    for sq in range(seqs):
        if n_sub == per_iter:
            run(sq, 0)
        else:
            def body(i, carry, sq=sq):
                run(sq, pl.multiple_of(i * (per_iter * sub), per_iter * sub))
                return carry

            lax.fori_loop(0, n_sub // per_iter, body, 0)


def _attn(q, k, v, ctx, seqs):
    B, T, _ = q.shape
    tq = min(Q_TILE, T)
    has_ctx = ctx is not None
    full = lambda n, w: pl.BlockSpec((seqs, n, w), lambda b, i: (b, 0, 0))
    in_specs = [pl.BlockSpec((seqs, tq, QK_PAD), lambda b, i: (b, i, 0)), full(T, QK_PAD), full(T, A_WIDTH)]
    args = [q, k, v]
    scratch = []
    if has_ctx:
        assert seqs == 1
        P = ctx[0].shape[1]
        in_specs += [full(P, KV_LORA), full(P, HEAD_PAD), _const_spec((KV_LORA, QK_PAD + A_WIDTH)),
                     _const_spec((2, HEAD_PAD))]
        args += list(ctx)
        scratch = [pltpu.VMEM((P, QK_PAD), BF16), pltpu.VMEM((P, A_WIDTH), BF16)]
    return pl.pallas_call(
        functools.partial(_attn_kernel, has_ctx, seqs),
        grid=(B // seqs, T // tq),
        in_specs=in_specs,
        out_specs=pl.BlockSpec((seqs, tq, A_WIDTH), lambda b, i: (b, i, 0)),
        out_shape=jax.ShapeDtypeStruct((B, T, A_WIDTH), BF16),
        scratch_shapes=scratch,
        compiler_params=_params(("parallel", "arbitrary" if has_ctx else "parallel")),
        name="attn_latent" if has_ctx else "attn_context",
    )(*args)


def _post_kernel(mod_row0, mod_per_batch, x_ref, hm_ref, ha_ref, mod_ref, g2_ref, wout_ref, wup_ref, wdown_ref,
                 y_ref):
    mod = _mod_row(mod_ref, mod_row0, mod_per_batch)
    gate1 = mod[:, 2 * D_MODEL:3 * D_MODEL]
    sh2 = mod[:, 3 * D_MODEL:4 * D_MODEL]
    sc2 = mod[:, 4 * D_MODEL:5 * D_MODEL]
    gate2 = mod[:, 5 * D_MODEL:6 * D_MODEL]
    mix = jnp.concatenate([hm_ref[...], ha_ref[...]], axis=-1)
    x1 = x_ref[...] + gate1 * _dot(mix, wout_ref[...])
    h2 = (_rms(x1, g2_ref[...]) * (1.0 + sc2) + sh2).astype(BF16)
    acc = jnp.zeros_like(x1)
    for c in range(D_FF // FF_TILE):
        sl = slice(c * FF_TILE, (c + 1) * FF_TILE)
        u = jnp.maximum(_dot(h2, wup_ref[:, sl]), 0.0)
        acc = acc + _dot((u * u).astype(BF16), wdown_ref[sl, :])
    y_ref[...] = x1 + gate2 * acc


def _post(x, hm, ha, mod3, mod_row0, mod_per_batch, wts):
    shape = x.shape
    if not mod_per_batch:
        x, hm, ha = (a.reshape(1, -1, a.shape[-1]) for a in (x, hm, ha))
    B, T, _ = x.shape
    tm = POST_TILE
    tok = lambda w: pl.BlockSpec((None, tm, w), lambda b, i: (b, i, 0))
    return _post_call(x, hm, ha, mod3, mod_row0, mod_per_batch, wts, B, T, tm, tok).reshape(shape)


def _post_call(x, hm, ha, mod3, mod_row0, mod_per_batch, wts, B, T, tm, tok):
    return pl.pallas_call(
        functools.partial(_post_kernel, mod_row0, mod_per_batch),
        grid=(B, T // tm),
        in_specs=[tok(D_MODEL), tok(M_WIDTH), tok(A_WIDTH),
                  _const_spec((8, 6 * D_MODEL)),
                  _const_spec((1, D_MODEL)),
                  _const_spec((M_WIDTH + A_WIDTH, D_MODEL)),
                  _const_spec((D_MODEL, D_FF)),
                  _const_spec((D_FF, D_MODEL))],
        out_specs=tok(D_MODEL),
        out_shape=jax.ShapeDtypeStruct((B, T, D_MODEL), F32),
        compiler_params=_params(("parallel", "parallel")),
        name="post",
    )(x, hm, ha, mod3, wts["g2"], wts["w_out"], wts["w_up"], wts["w_down"])


def _prepare_weights(norm1_g, norm2_g, w_in, mlstm_gate_b, q_lora_g, kv_lora_g, w_q_up, w_kv_up,
                     q_head_g, k_head_g, w_out, w_mlp_up, w_mlp_down):
    o_g = 4 * M_WIDTH
    o_q = o_g + N_GATES
    o_kv = o_q + Q_LORA
    o_kr = o_kv + KV_LORA
    half = A_ROPE // 2
    n_dh = 2 * M_HEADS
    wt = w_in.T.astype(BF16)
    w_gate = wt[o_g:o_q].reshape(2, 2, M_HEADS, D_MODEL)
    bias = mlstm_gate_b.reshape(2, 2, M_HEADS)

    def rot_partner(a):
        z = jnp.zeros(a.shape[:-1] + (A_NOPE,), a.dtype)
        return jnp.concatenate([z, a[..., A_NOPE + half:A_QK], a[..., A_NOPE:A_NOPE + half]], axis=-1)

    pad_tile = lambda a: jnp.pad(a, [(0, 0)] * (a.ndim - 1) + [(0, HEAD_PAD - A_QK)])
    w_kr = wt[o_kr:o_kr + A_ROPE]
    w_kr_partner = jnp.concatenate([w_kr[half:], w_kr[:half]], axis=0)
    w_lat = jnp.concatenate([wt[o_q:o_kr], w_gate[:, 0].reshape(n_dh, D_MODEL), w_gate[:, 1].reshape(n_dh, D_MODEL),
                             jnp.zeros((A_NOPE - 2 * n_dh, D_MODEL), BF16), w_kr, w_kr_partner], axis=0)
    gate_bias = jnp.pad(jnp.stack([bias[:, 0, :].reshape(n_dh), bias[:, 1, :].reshape(n_dh)], axis=0),
                        ((0, 0), (0, LANES - n_dh)))
    w_q3 = w_q_up.reshape(Q_LORA, A_HEADS, A_QK)
    w_q = pad_tile(w_q3).reshape(Q_LORA, QK_PAD)
    w_q_partner = jnp.concatenate([w_q3[:, :, A_NOPE + half:], w_q3[:, :, A_NOPE:A_NOPE + half]],
                                  axis=-1).reshape(Q_LORA, A_HEADS * A_ROPE)
    w_kv3 = w_kv_up.reshape(KV_LORA, A_HEADS, A_NOPE + A_VDIM)
    w_k = jnp.pad(w_kv3[:, :, :A_NOPE], ((0, 0), (0, 0), (0, HEAD_PAD - A_NOPE)))
    w_v = w_kv3[:, :, A_NOPE:]
    w_kv = jnp.concatenate([w_k.reshape(KV_LORA, QK_PAD), w_v.reshape(KV_LORA, A_WIDTH)], axis=1)
    pad_head = lambda g: jnp.stack([pad_tile(g), pad_tile(rot_partner(g))], axis=0)
    return {
        "g1": norm1_g.reshape(1, D_MODEL),
        "g2": norm2_g.reshape(1, D_MODEL),
        "w_main": jnp.concatenate([wt[0:M_WIDTH], wt[2 * M_WIDTH:o_g]], axis=0),
        "w_kt": wt[M_WIDTH:2 * M_WIDTH],
        "w_lat": w_lat,
        "gate_bias": gate_bias,
        "q_lora_g": q_lora_g.reshape(1, Q_LORA),
        "kv_lora_g": kv_lora_g.reshape(1, KV_LORA),
        "w_q": w_q.astype(BF16),
        "w_q_rot": jnp.concatenate([w_q, w_q_partner], axis=1).astype(BF16),
        "w_kv": w_kv.astype(BF16),
        "q_head_g": pad_head(q_head_g),
        "k_head_g": pad_head(k_head_g),
        "w_out_f32": w_out,
        "w_up_f32": w_mlp_up,
        "w_down_f32": w_mlp_down,
    }


def _rope_tables(T):
    rows = T // GRID_W
    row = np.repeat(np.arange(rows, dtype=np.float32), GRID_W)
    col = np.tile(np.arange(GRID_W, dtype=np.float32), rows)
    half = A_ROPE // 2
    inv = (np.float32(ROPE_BASE) ** (-np.arange(0, half, 2, dtype=np.float32) / np.float32(half))).astype(np.float32)
    ang = np.concatenate([row[:, None] * inv, col[:, None] * inv], axis=-1)
    cos, sin = np.cos(ang), np.sin(ang)
    ones = np.ones((T, A_NOPE), np.float32)
    z = lambda w: np.zeros((T, w), np.float32)
    tail = LANES - A_QK
    cos_t = np.concatenate([ones, cos, cos, z(tail)], axis=1)
    sin_t = np.concatenate([z(A_NOPE), -sin, sin, z(tail)], axis=1)
    return jnp.asarray(np.stack([cos_t, sin_t], axis=0).astype(np.float32))


def _layer_pass(x, mod3, mod_row0, mod_per_batch, wts, norm_g, rope_tab, init_state, ctx_kv, is_context):
    pre = _pre(x, mod3, mod_row0, mod_per_batch, wts, rope_tab, emit_cache=is_context)
    mq, mkt, mv, mo, stats, q, k, v = pre[:8]
    if is_context:
        wts = dict(wts, w_out=pre[10], w_up=pre[11], w_down=pre[12])
    ml = _mlstm(mq, mkt, mv, mo, stats, norm_g, init_state, emit_state=is_context,
                heads=M_HEADS if is_context else 1, seqs=MLSTM_CONTEXT_SEQS if is_context else 1)
    ha = _attn(q, k, v, ctx_kv, seqs=ATTN_CONTEXT_SEQS if is_context else 1)
    y = _post(x, ml[0], ha, mod3, mod_row0, mod_per_batch, wts)
    return y, pre[8:10], ml[1:], wts


def kernel(x_prompt, x_sample, cache_mla_ckv, cache_mla_krope, state_mlstm_C, state_mlstm_n, state_mlstm_m,
           c, c_ctx, norm1_g, norm2_g, w_ada, b_ada, w_in, mlstm_gate_b, mlstm_norm_g,
           q_lora_g, kv_lora_g, w_q_up, w_kv_up, q_head_g, k_head_g, w_out, w_mlp_up, w_mlp_down):
    depth = w_in.shape[0]
    Bd = x_sample.shape[0]
    cond8 = jnp.concatenate([c_ctx[None, :], c, jnp.zeros((8 - 1 - Bd, D_MODEL), F32)], axis=0)
    rope_tab = _rope_tables(x_sample.shape[1])

    y, z = x_prompt, x_sample
    ckvs, kropes, Cs, ns, ms = [], [], [], [], []
    for l in range(depth):
        wts = _prepare_weights(norm1_g[l], norm2_g[l], w_in[l], mlstm_gate_b[l], q_lora_g[l], kv_lora_g[l],
                               w_q_up[l], w_kv_up[l], q_head_g[l], k_head_g[l], w_out[l], w_mlp_up[l],
                               w_mlp_down[l])
        norm_g = mlstm_norm_g[l].reshape(M_HEADS, 1, M_HEAD_DIM)
        mod3 = _ada(cond8, w_ada[l], b_ada[l])
        y, (ckv, krope), (C_new, n_new, m_new), wts = _layer_pass(
            y, mod3, 0, 0, wts, norm_g, None, None, None, True)
        ckvs.append(ckv)
        kropes.append(krope)
        Cs.append(C_new)
        ns.append(n_new[:, :, :, 0, :])
        ms.append(m_new[:, :, :, 0].transpose(0, 2, 1))

        init_state = (state_mlstm_C[:, l],
                      state_mlstm_n[:, l][:, :, :, None, :],
                      jnp.broadcast_to(state_mlstm_m[:, l].transpose(0, 2, 1)[..., None],
                                       (Bd, M_HEADS, 2, LANES)))
        krope_placed = jnp.pad(cache_mla_krope[:, l], ((0, 0), (0, 0), (A_NOPE, LANES - A_QK)))
        ctx = (cache_mla_ckv[:, l], krope_placed, wts["w_kv"], wts["k_head_g"])
        z, _, _, _ = _layer_pass(z, mod3, 1, 1, wts, norm_g, rope_tab, init_state, ctx, False)

    return (y, z, jnp.stack(ckvs, axis=1), jnp.stack(kropes, axis=1), jnp.stack(Cs, axis=1),
            jnp.stack(ns, axis=1), jnp.stack(ms, axis=1))
```
